```python
import math
import jax, jax.numpy as jnp
from jax import lax
import numpy as np

D_MODEL = 2048
BATCH = 8
SEQ = 4096
DEPTH = 1

MLA_NOPE = 128
MLA_ROPE = 64
MLA_V = 128
MLA_HEADS = D_MODEL // MLA_V
MLA_Q_RANK = 768
MLA_KV_RANK = 512
MLA_QK = MLA_NOPE + MLA_ROPE
ROPE_THETA = 10000.0
SWA_HEAD_DIM = 64
SWA_HEADS = D_MODEL // SWA_HEAD_DIM
SWA_KV_HEADS = 4
SWA_GROUP = SWA_HEADS // SWA_KV_HEADS
WINDOW = 128
BLOCK = 128
REL_BUCKETS = 32
REL_MAX_DIST = 128
D_FF = 5632
CONV_WIDTH = 3
EPS = 1e-6
NEG = -1e30

MLA_IN = MLA_Q_RANK + MLA_KV_RANK + MLA_ROPE
SWA_Q = SWA_HEADS * SWA_HEAD_DIM
SWA_KV = SWA_KV_HEADS * SWA_HEAD_DIM
N_BRANCH = 2
IN_COLS = MLA_IN + SWA_Q + 2 * SWA_KV + N_BRANCH * D_MODEL

kernel_name = "hybrid_mla_swa_convffn_block"


def rms_norm(x, g):
    xf = x.astype(jnp.float32)
    y = xf * lax.rsqrt(jnp.mean(xf * xf, axis=-1, keepdims=True) + EPS)
    return (y * g.astype(jnp.float32)).astype(x.dtype)


def rope_tables(seq):
    pos = jnp.arange(seq, dtype=jnp.float32)
    inv = ROPE_THETA ** (-jnp.arange(0, MLA_ROPE, 2, dtype=jnp.float32) / MLA_ROPE)
    ang = pos[:, None] * inv[None, :]
    ang = jnp.concatenate([ang, ang], axis=-1)
    return jnp.cos(ang), jnp.sin(ang)


def apply_rope(x, cos, sin):
    half = x.shape[-1] // 2
    x1, x2 = x[..., :half], x[..., half:]
    rot = jnp.concatenate([-x2, x1], axis=-1)
    return x * cos.astype(x.dtype) + rot * sin.astype(x.dtype)


def t5_bucket(dist):
    max_exact = REL_BUCKETS // 2
    n = jnp.maximum(dist, 0)
    large = max_exact + (jnp.log(jnp.maximum(n, 1).astype(jnp.float32) / max_exact)
                         / math.log(REL_MAX_DIST / max_exact)
                         * (REL_BUCKETS - max_exact)).astype(jnp.int32)
    large = jnp.minimum(large, REL_BUCKETS - 1)
    return jnp.where(n < max_exact, n, large)


def mla_branch(cq, ckv, k_rope, g_q, w_uq, g_kv, w_ukv):
    B, S, _ = cq.shape
    nb = S // BLOCK
    cos, sin = rope_tables(S)
    q = (rms_norm(cq, g_q) @ w_uq).reshape(B, S, MLA_HEADS, MLA_QK)
    q_nope = q[..., :MLA_NOPE]
    q_rope = apply_rope(q[..., MLA_NOPE:], cos[:, None, :], sin[:, None, :])
    kv = (rms_norm(ckv, g_kv) @ w_ukv).reshape(B, S, MLA_HEADS, MLA_NOPE + MLA_V)
    k_nope, v = kv[..., :MLA_NOPE], kv[..., MLA_NOPE:]
    k_rope = apply_rope(k_rope, cos, sin)
    scale = MLA_QK ** -0.5
    qn_blocks = q_nope.reshape(B, nb, BLOCK, MLA_HEADS, MLA_NOPE).transpose(1, 0, 2, 3, 4)
    qr_blocks = q_rope.reshape(B, nb, BLOCK, MLA_HEADS, MLA_ROPE).transpose(1, 0, 2, 3, 4)
    kpos = jnp.arange(S)

    def one_block(args):
        qn, qr, i = args
        s = (jnp.einsum('bqhd,bkhd->bhqk', qn, k_nope)
             + jnp.einsum('bqhd,bkd->bhqk', qr, k_rope)).astype(jnp.float32) * scale
        qpos = i * BLOCK + jnp.arange(BLOCK)
        s = jnp.where(kpos[None, :] <= qpos[:, None], s, NEG)
        p = jax.nn.softmax(s, axis=-1).astype(v.dtype)
        return jnp.einsum('bhqk,bkhd->bqhd', p, v)

    out = lax.map(one_block, (qn_blocks, qr_blocks, jnp.arange(nb)))
    return out.transpose(1, 0, 2, 3, 4).reshape(B, S, MLA_HEADS * MLA_V)


def swa_branch(q, k, v, rel_bias, sinks):
    B, S, _ = q.shape
    nb = S // BLOCK
    q = q.reshape(B, nb, BLOCK, SWA_KV_HEADS, SWA_GROUP, SWA_HEAD_DIM)

    def band(t):
        t = t.reshape(B, S, SWA_KV_HEADS, SWA_HEAD_DIM)
        t = jnp.pad(t, ((0, 0), (BLOCK, 0), (0, 0), (0, 0)))
        t = t.reshape(B, nb + 1, BLOCK, SWA_KV_HEADS, SWA_HEAD_DIM)
        return jnp.concatenate([t[:, :-1], t[:, 1:]], axis=2)

    kb, vb = band(k), band(v)
    s = jnp.einsum('bnqhgd,bnshd->bhgnqs', q, kb).astype(jnp.float32) * (SWA_HEAD_DIM ** -0.5)
    a = jnp.arange(BLOCK)
    bidx = jnp.arange(2 * BLOCK)
    dist = BLOCK + a[:, None] - bidx[None, :]
    bias = rel_bias[t5_bucket(dist)].astype(jnp.float32)
    bias = bias.transpose(2, 0, 1).reshape(SWA_KV_HEADS, SWA_GROUP, 1, BLOCK, 2 * BLOCK)
    kpos = jnp.arange(nb)[:, None, None] * BLOCK - BLOCK + bidx[None, None, :]
    mask = (dist >= 0)[None] & (dist < WINDOW)[None] & (kpos >= 0)
    s = jnp.where(mask, s + bias, NEG)
    sink = sinks.astype(jnp.float32).reshape(SWA_KV_HEADS, SWA_GROUP, 1, 1, 1)
    m = jnp.maximum(jnp.max(s, axis=-1, keepdims=True), sink)
    e = jnp.exp(s - m)
    p = e / (jnp.sum(e, axis=-1, keepdims=True) + jnp.exp(sink - m))
    o = jnp.einsum('bhgnqs,bnshd->bnqhgd', p.astype(vb.dtype), vb)
    return o.reshape(B, S, SWA_Q)


def causal_dwconv(u, w, b):
    S = u.shape[1]
    up = jnp.pad(u, ((0, 0), (CONV_WIDTH - 1, 0), (0, 0)))
    y = b
    for j in range(CONV_WIDTH):
        y = y + w[j] * up[:, j:j + S]
    return y


def _fwd_setup_inputs(seed: int = 0) -> dict:
    key = jax.random.key(seed)
    ks = jax.random.split(key, 24)
    f32 = jnp.float32
    D, L = D_MODEL, DEPTH

    def nrm(k, shape, scale):
        return jax.random.normal(k, shape, f32) * scale

    def gain(k, shape):
        return 1.0 + 0.1 * jax.random.normal(k, shape, f32)

    return {
        "x": nrm(ks[0], (BATCH, SEQ, D), 1.0),
        "c": nrm(ks[1], (BATCH, D), 1.0),
        "w_ada": nrm(ks[2], (L, D, 6 * D), 0.5 * D ** -0.5),
        "b_ada": nrm(ks[3], (L, 6 * D), 0.02),
        "g_pre_mix": gain(ks[4], (L, D)),
        "g_post_mix": gain(ks[5], (L, D)),
        "w_in": nrm(ks[6], (L, D, IN_COLS), D ** -0.5),
        "g_q_lat": gain(ks[7], (L, MLA_Q_RANK)),
        "w_uq": nrm(ks[8], (L, MLA_Q_RANK, MLA_HEADS * MLA_QK), MLA_Q_RANK ** -0.5),
        "g_kv_lat": gain(ks[9], (L, MLA_KV_RANK)),
        "w_ukv": nrm(ks[10], (L, MLA_KV_RANK, MLA_HEADS * (MLA_NOPE + MLA_V)), MLA_KV_RANK ** -0.5),
        "rel_bias": nrm(ks[11], (REL_BUCKETS, SWA_HEADS), 0.5),
        "sinks": nrm(ks[12], (L, SWA_HEADS), 1.0),
        "w_o": nrm(ks[13], (L, D, D), D ** -0.5),
        "g_pre_ffn": gain(ks[14], (L, D)),
        "g_post_ffn": gain(ks[15], (L, D)),
        "w_up": nrm(ks[16], (L, D, 2 * D_FF), D ** -0.5),
        "conv_w": nrm(ks[17], (L, CONV_WIDTH, 2 * D_FF), CONV_WIDTH ** -0.5),
        "conv_b": nrm(ks[18], (L, 2 * D_FF), 0.02),
        "w_down": nrm(ks[19], (L, D_FF, D), D_FF ** -0.5),
    }


def _fwd_reference(x, c, w_ada, b_ada, g_pre_mix, g_post_mix, w_in, g_q_lat, w_uq, g_kv_lat,
              w_ukv, rel_bias, sinks, w_o, g_pre_ffn, g_post_ffn, w_up, conv_w, conv_b, w_down):
    D = D_MODEL
    c_act = jax.nn.silu(c)
    for l in range(DEPTH):
        mod = (c_act @ w_ada[l] + b_ada[l])[:, None, :]
        sh1, sc1, gt1, sh2, sc2, gt2 = jnp.split(mod, 6, axis=-1)

        h = rms_norm(x, g_pre_mix[l]) * (1.0 + sc1) + sh1
        z = h @ w_in[l]
        o0 = 0
        cq = z[..., o0:o0 + MLA_Q_RANK]; o0 += MLA_Q_RANK
        ckv = z[..., o0:o0 + MLA_KV_RANK]; o0 += MLA_KV_RANK
        kr = z[..., o0:o0 + MLA_ROPE]; o0 += MLA_ROPE
        qs = z[..., o0:o0 + SWA_Q]; o0 += SWA_Q
        ks_ = z[..., o0:o0 + SWA_KV]; o0 += SWA_KV
        vs = z[..., o0:o0 + SWA_KV]; o0 += SWA_KV
        gates = jax.nn.sigmoid(z[..., o0:o0 + N_BRANCH * D])
        g_a, g_b = gates[..., :D], gates[..., D:]

        o_a = mla_branch(cq, ckv, kr, g_q_lat[l], w_uq[l], g_kv_lat[l], w_ukv[l])
        o_b = swa_branch(qs, ks_, vs, rel_bias, sinks[l])
        mix = (g_a * o_a + g_b * o_b) @ w_o[l]
        x = x + gt1 * rms_norm(mix, g_post_mix[l])

        h = rms_norm(x, g_pre_ffn[l]) * (1.0 + sc2) + sh2
        u = causal_dwconv(h @ w_up[l], conv_w[l], conv_b[l])
        y = (jax.nn.silu(u[..., :D_FF]) * u[..., D_FF:]) @ w_down[l]
        x = x + gt2 * rms_norm(y, g_post_ffn[l])
    return x


import jax as _jax
import jax.numpy as _jnp

TWIN_FORMAT = 'train_step'
FWD_PARAMS = ['x', 'c', 'w_ada', 'b_ada', 'g_pre_mix', 'g_post_mix', 'w_in', 'g_q_lat', 'w_uq', 'g_kv_lat', 'w_ukv', 'rel_bias', 'sinks', 'w_o', 'g_pre_ffn', 'g_post_ffn', 'w_up', 'conv_w', 'conv_b', 'w_down']
TWIN_WEIGHTS = ['w_ada', 'b_ada', 'g_pre_mix', 'g_post_mix', 'w_in', 'g_q_lat', 'w_uq', 'g_kv_lat', 'w_ukv', 'rel_bias', 'sinks', 'w_o', 'g_pre_ffn', 'g_post_ffn', 'w_up', 'conv_w', 'conv_b', 'w_down']
TWIN_DIFF_INPUT = 'x'
TWIN_INPUTS = ['x', 'c', 'w_ada', 'b_ada', 'g_pre_mix', 'g_post_mix', 'w_in', 'g_q_lat', 'w_uq', 'g_kv_lat', 'w_ukv', 'rel_bias', 'sinks', 'w_o', 'g_pre_ffn', 'g_post_ffn', 'w_up', 'conv_w', 'conv_b', 'w_down', 'loss_target', 'm_w_ada', 'm_b_ada', 'm_g_pre_mix', 'm_g_post_mix', 'm_w_in', 'm_g_q_lat', 'm_w_uq', 'm_g_kv_lat', 'm_w_ukv', 'm_rel_bias', 'm_sinks', 'm_w_o', 'm_g_pre_ffn', 'm_g_post_ffn', 'm_w_up', 'm_conv_w', 'm_conv_b', 'm_w_down', 'v_w_ada', 'v_b_ada', 'v_g_pre_mix', 'v_g_post_mix', 'v_w_in', 'v_g_q_lat', 'v_w_uq', 'v_g_kv_lat', 'v_w_ukv', 'v_rel_bias', 'v_sinks', 'v_w_o', 'v_g_pre_ffn', 'v_g_post_ffn', 'v_w_up', 'v_conv_w', 'v_conv_b', 'v_w_down']
TWIN_OUTPUTS = ['loss', 'grad_x', 'grad_w_ada', 'grad_b_ada', 'grad_g_pre_mix', 'grad_g_post_mix', 'grad_w_in', 'grad_g_q_lat', 'grad_w_uq', 'grad_g_kv_lat', 'grad_w_ukv', 'grad_rel_bias', 'grad_sinks', 'grad_w_o', 'grad_g_pre_ffn', 'grad_g_post_ffn', 'grad_w_up', 'grad_conv_w', 'grad_conv_b', 'grad_w_down', 'delta_w_ada', 'delta_b_ada', 'delta_g_pre_mix', 'delta_g_post_mix', 'delta_w_in', 'delta_g_q_lat', 'delta_w_uq', 'delta_g_kv_lat', 'delta_w_ukv', 'delta_rel_bias', 'delta_sinks', 'delta_w_o', 'delta_g_pre_ffn', 'delta_g_post_ffn', 'delta_w_up', 'delta_conv_w', 'delta_conv_b', 'delta_w_down', 'new_m_w_ada', 'new_m_b_ada', 'new_m_g_pre_mix', 'new_m_g_post_mix', 'new_m_w_in', 'new_m_g_q_lat', 'new_m_w_uq', 'new_m_g_kv_lat', 'new_m_w_ukv', 'new_m_rel_bias', 'new_m_sinks', 'new_m_w_o', 'new_m_g_pre_ffn', 'new_m_g_post_ffn', 'new_m_w_up', 'new_m_conv_w', 'new_m_conv_b', 'new_m_w_down', 'new_v_w_ada', 'new_v_b_ada', 'new_v_g_pre_mix', 'new_v_g_post_mix', 'new_v_w_in', 'new_v_g_q_lat', 'new_v_w_uq', 'new_v_g_kv_lat', 'new_v_w_ukv', 'new_v_rel_bias', 'new_v_sinks', 'new_v_w_o', 'new_v_g_pre_ffn', 'new_v_g_post_ffn', 'new_v_w_up', 'new_v_conv_w', 'new_v_conv_b', 'new_v_w_down']
TWIN_LEAF_KINDS = {'loss': 'loss', 'grad_x': 'grad_x', 'grad_w_ada': 'grad_w', 'grad_b_ada': 'grad_w', 'grad_g_pre_mix': 'grad_w', 'grad_g_post_mix': 'grad_w', 'grad_w_in': 'grad_w', 'grad_g_q_lat': 'grad_w', 'grad_w_uq': 'grad_w', 'grad_g_kv_lat': 'grad_w', 'grad_w_ukv': 'grad_w', 'grad_rel_bias': 'grad_w', 'grad_sinks': 'grad_w', 'grad_w_o': 'grad_w', 'grad_g_pre_ffn': 'grad_w', 'grad_g_post_ffn': 'grad_w', 'grad_w_up': 'grad_w', 'grad_conv_w': 'grad_w', 'grad_conv_b': 'grad_w', 'grad_w_down': 'grad_w', 'delta_w_ada': 'delta_w', 'delta_b_ada': 'delta_w', 'delta_g_pre_mix': 'delta_w', 'delta_g_post_mix': 'delta_w', 'delta_w_in': 'delta_w', 'delta_g_q_lat': 'delta_w', 'delta_w_uq': 'delta_w', 'delta_g_kv_lat': 'delta_w', 'delta_w_ukv': 'delta_w', 'delta_rel_bias': 'delta_w', 'delta_sinks': 'delta_w', 'delta_w_o': 'delta_w', 'delta_g_pre_ffn': 'delta_w', 'delta_g_post_ffn': 'delta_w', 'delta_w_up': 'delta_w', 'delta_conv_w': 'delta_w', 'delta_conv_b': 'delta_w', 'delta_w_down': 'delta_w', 'new_m_w_ada': 'new_m', 'new_m_b_ada': 'new_m', 'new_m_g_pre_mix': 'new_m', 'new_m_g_post_mix': 'new_m', 'new_m_w_in': 'new_m', 'new_m_g_q_lat': 'new_m', 'new_m_w_uq': 'new_m', 'new_m_g_kv_lat': 'new_m', 'new_m_w_ukv': 'new_m', 'new_m_rel_bias': 'new_m', 'new_m_sinks': 'new_m', 'new_m_w_o': 'new_m', 'new_m_g_pre_ffn': 'new_m', 'new_m_g_post_ffn': 'new_m', 'new_m_w_up': 'new_m', 'new_m_conv_w': 'new_m', 'new_m_conv_b': 'new_m', 'new_m_w_down': 'new_m', 'new_v_w_ada': 'new_v', 'new_v_b_ada': 'new_v', 'new_v_g_pre_mix': 'new_v', 'new_v_g_post_mix': 'new_v', 'new_v_w_in': 'new_v', 'new_v_g_q_lat': 'new_v', 'new_v_w_uq': 'new_v', 'new_v_g_kv_lat': 'new_v', 'new_v_w_ukv': 'new_v', 'new_v_rel_bias': 'new_v', 'new_v_sinks': 'new_v', 'new_v_w_o': 'new_v', 'new_v_g_pre_ffn': 'new_v', 'new_v_g_post_ffn': 'new_v', 'new_v_w_up': 'new_v', 'new_v_conv_w': 'new_v', 'new_v_conv_b': 'new_v', 'new_v_w_down': 'new_v'}


def _forward(args):
    return _fwd_reference(*[args[k] for k in FWD_PARAMS])


def _output_shape():
    def fwd():
        inp = _fwd_setup_inputs(0)
        return _fwd_reference(*[inp[k] for k in FWD_PARAMS])
    out = _jax.eval_shape(fwd)
    return out.shape, out.dtype

N_MICROBATCH = 1
ADAM_LR = 0.001
ADAM_B1 = 0.9
ADAM_B2 = 0.999
ADAM_EPS = 1e-08
ADAM_WD = 0.01
ADAM_STEP = 10
PER_EXAMPLE_BATCH_AXIS = {'x': 0, 'c': 0, 'loss_target': 0}
SHARED_INPUTS = []
_WEIGHT_DTYPES = {'w_ada': _jnp.float32, 'b_ada': _jnp.float32, 'g_pre_mix': _jnp.float32, 'g_post_mix': _jnp.float32, 'w_in': _jnp.float32, 'g_q_lat': _jnp.float32, 'w_uq': _jnp.float32, 'g_kv_lat': _jnp.float32, 'w_ukv': _jnp.float32, 'rel_bias': _jnp.float32, 'sinks': _jnp.float32, 'w_o': _jnp.float32, 'g_pre_ffn': _jnp.float32, 'g_post_ffn': _jnp.float32, 'w_up': _jnp.float32, 'conv_w': _jnp.float32, 'conv_b': _jnp.float32, 'w_down': _jnp.float32}
MOMENT_SCALE = {'w_ada': 8.652908e-01, 'b_ada': 1.693259e+00, 'g_pre_mix': 5.610645e-02, 'g_post_mix': 1.822707e+00, 'w_in': 2.652674e-01, 'g_q_lat': 2.523852e-02, 'w_uq': 1.245823e-02, 'g_kv_lat': 6.337161e-01, 'w_ukv': 2.410437e-01, 'rel_bias': 2.360087e-02, 'sinks': 1.561321e-02, 'w_o': 5.012553e-01, 'g_pre_ffn': 7.245424e-02, 'g_post_ffn': 1.648437e+00, 'w_up': 3.933905e-02, 'conv_w': 4.094540e-02, 'conv_b': 9.086490e-02, 'w_down': 7.857814e-02}


def _to_microbatches(a, axis):
    t = _jnp.moveaxis(a, axis, 0)
    t = t.reshape((N_MICROBATCH, t.shape[0] // N_MICROBATCH) + t.shape[1:])
    return _jnp.moveaxis(t, 1, axis + 1)


def setup_inputs(seed: int = 0) -> dict:
    inp = _fwd_setup_inputs(seed)
    key = _jax.random.fold_in(_jax.random.key(seed), 7919)
    shape, _ = _output_shape()
    out = dict(inp)
    out["loss_target"] = _jax.random.normal(_jax.random.fold_in(key, 0), shape, _jnp.float32)
    for i, name in enumerate(TWIN_WEIGHTS):
        w = inp[name].astype(_jnp.float32)
        if MOMENT_SCALE is None:
            s = _jnp.sqrt(_jnp.mean(_jnp.square(w)) + 1e-30)
        else:
            s = MOMENT_SCALE[name]
        km, kv = _jax.random.split(_jax.random.fold_in(key, i + 1))
        out[name] = w
        out["m_" + name] = s * _jax.random.normal(km, w.shape, _jnp.float32)
        out["v_" + name] = (s * s) * _jax.random.uniform(kv, w.shape, _jnp.float32, 0.5, 1.5)
    if N_MICROBATCH > 1:
        for name, axis in PER_EXAMPLE_BATCH_AXIS.items():
            out[name] = _to_microbatches(out[name], axis)
    return {'x': out['x'], 'c': out['c'], 'w_ada': out['w_ada'], 'b_ada': out['b_ada'], 'g_pre_mix': out['g_pre_mix'], 'g_post_mix': out['g_post_mix'], 'w_in': out['w_in'], 'g_q_lat': out['g_q_lat'], 'w_uq': out['w_uq'], 'g_kv_lat': out['g_kv_lat'], 'w_ukv': out['w_ukv'], 'rel_bias': out['rel_bias'], 'sinks': out['sinks'], 'w_o': out['w_o'], 'g_pre_ffn': out['g_pre_ffn'], 'g_post_ffn': out['g_post_ffn'], 'w_up': out['w_up'], 'conv_w': out['conv_w'], 'conv_b': out['conv_b'], 'w_down': out['w_down'], 'loss_target': out['loss_target'], 'm_w_ada': out['m_w_ada'], 'm_b_ada': out['m_b_ada'], 'm_g_pre_mix': out['m_g_pre_mix'], 'm_g_post_mix': out['m_g_post_mix'], 'm_w_in': out['m_w_in'], 'm_g_q_lat': out['m_g_q_lat'], 'm_w_uq': out['m_w_uq'], 'm_g_kv_lat': out['m_g_kv_lat'], 'm_w_ukv': out['m_w_ukv'], 'm_rel_bias': out['m_rel_bias'], 'm_sinks': out['m_sinks'], 'm_w_o': out['m_w_o'], 'm_g_pre_ffn': out['m_g_pre_ffn'], 'm_g_post_ffn': out['m_g_post_ffn'], 'm_w_up': out['m_w_up'], 'm_conv_w': out['m_conv_w'], 'm_conv_b': out['m_conv_b'], 'm_w_down': out['m_w_down'], 'v_w_ada': out['v_w_ada'], 'v_b_ada': out['v_b_ada'], 'v_g_pre_mix': out['v_g_pre_mix'], 'v_g_post_mix': out['v_g_post_mix'], 'v_w_in': out['v_w_in'], 'v_g_q_lat': out['v_g_q_lat'], 'v_w_uq': out['v_w_uq'], 'v_g_kv_lat': out['v_g_kv_lat'], 'v_w_ukv': out['v_w_ukv'], 'v_rel_bias': out['v_rel_bias'], 'v_sinks': out['v_sinks'], 'v_w_o': out['v_w_o'], 'v_g_pre_ffn': out['v_g_pre_ffn'], 'v_g_post_ffn': out['v_g_post_ffn'], 'v_w_up': out['v_w_up'], 'v_conv_w': out['v_conv_w'], 'v_conv_b': out['v_conv_b'], 'v_w_down': out['v_w_down']}


def _loss(weights, diff, rest, loss_target):
    with _jax.named_scope("forward"):
        args = {**rest, TWIN_DIFF_INPUT: diff, **{k: w.astype(_WEIGHT_DTYPES[k]) for k, w in weights.items()}}
        y = _forward(args)
    with _jax.named_scope("loss_head"):
        err = _jnp.square(y.astype(_jnp.float32) - loss_target)
        return 0.5 * _jnp.sum(_jnp.mean(err, axis=-1)) if err.ndim else 0.5 * err


def _adamw(w, g, m, v):
    m = ADAM_B1 * m + (1.0 - ADAM_B1) * g
    v = ADAM_B2 * v + (1.0 - ADAM_B2) * _jnp.square(g)
    m_hat = m / (1.0 - ADAM_B1 ** ADAM_STEP)
    v_hat = v / (1.0 - ADAM_B2 ** ADAM_STEP)
    delta = -ADAM_LR * (m_hat / (_jnp.sqrt(v_hat) + ADAM_EPS) + ADAM_WD * w)
    return delta, m, v


def reference(x, c, w_ada, b_ada, g_pre_mix, g_post_mix, w_in, g_q_lat, w_uq, g_kv_lat, w_ukv, rel_bias, sinks, w_o, g_pre_ffn, g_post_ffn, w_up, conv_w, conv_b, w_down, loss_target, m_w_ada, m_b_ada, m_g_pre_mix, m_g_post_mix, m_w_in, m_g_q_lat, m_w_uq, m_g_kv_lat, m_w_ukv, m_rel_bias, m_sinks, m_w_o, m_g_pre_ffn, m_g_post_ffn, m_w_up, m_conv_w, m_conv_b, m_w_down, v_w_ada, v_b_ada, v_g_pre_mix, v_g_post_mix, v_w_in, v_g_q_lat, v_w_uq, v_g_kv_lat, v_w_ukv, v_rel_bias, v_sinks, v_w_o, v_g_pre_ffn, v_g_post_ffn, v_w_up, v_conv_w, v_conv_b, v_w_down):
    given = dict(x=x, c=c, w_ada=w_ada, b_ada=b_ada, g_pre_mix=g_pre_mix, g_post_mix=g_post_mix, w_in=w_in, g_q_lat=g_q_lat, w_uq=w_uq, g_kv_lat=g_kv_lat, w_ukv=w_ukv, rel_bias=rel_bias, sinks=sinks, w_o=w_o, g_pre_ffn=g_pre_ffn, g_post_ffn=g_post_ffn, w_up=w_up, conv_w=conv_w, conv_b=conv_b, w_down=w_down, loss_target=loss_target, m_w_ada=m_w_ada, m_b_ada=m_b_ada, m_g_pre_mix=m_g_pre_mix, m_g_post_mix=m_g_post_mix, m_w_in=m_w_in, m_g_q_lat=m_g_q_lat, m_w_uq=m_w_uq, m_g_kv_lat=m_g_kv_lat, m_w_ukv=m_w_ukv, m_rel_bias=m_rel_bias, m_sinks=m_sinks, m_w_o=m_w_o, m_g_pre_ffn=m_g_pre_ffn, m_g_post_ffn=m_g_post_ffn, m_w_up=m_w_up, m_conv_w=m_conv_w, m_conv_b=m_conv_b, m_w_down=m_w_down, v_w_ada=v_w_ada, v_b_ada=v_b_ada, v_g_pre_mix=v_g_pre_mix, v_g_post_mix=v_g_post_mix, v_w_in=v_w_in, v_g_q_lat=v_g_q_lat, v_w_uq=v_w_uq, v_g_kv_lat=v_g_kv_lat, v_w_ukv=v_w_ukv, v_rel_bias=v_rel_bias, v_sinks=v_sinks, v_w_o=v_w_o, v_g_pre_ffn=v_g_pre_ffn, v_g_post_ffn=v_g_post_ffn, v_w_up=v_w_up, v_conv_w=v_conv_w, v_conv_b=v_conv_b, v_w_down=v_w_down)
    weights = {n: given[n] for n in TWIN_WEIGHTS}
    shared = {n: given[n] for n in SHARED_INPUTS}
    per_example = {n: given[n] for n in ['x', 'c']}
    grad_fn = _jax.value_and_grad(_loss, argnums=(0, 1))

    def one_microbatch(ex, loss_target):
        ex = dict(ex)
        diff = ex.pop(TWIN_DIFF_INPUT)
        return grad_fn(weights, diff, {**shared, **ex}, loss_target)

    if N_MICROBATCH == 1:
        loss, (grad_w, grad_x) = one_microbatch(per_example, given["loss_target"])
    else:
        def body(carry, xs):
            loss_sum, grad_sum = carry
            l_k, (gw_k, gx_k) = one_microbatch(xs[0], xs[1])
            with _jax.named_scope("update"):
                return (loss_sum + l_k, _jax.tree.map(_jnp.add, grad_sum, gw_k)), gx_k

        init = (_jnp.zeros((), _jnp.float32), _jax.tree.map(_jnp.zeros_like, weights))
        (loss, grad_w), grad_x = _jax.lax.scan(body, init, (per_example, given["loss_target"]))
    with _jax.named_scope("update"):
        delta_w, new_m, new_v = {}, {}, {}
        for n in TWIN_WEIGHTS:
            delta_w[n], new_m[n], new_v[n] = _adamw(weights[n], grad_w[n], given["m_" + n], given["v_" + n])
    return (loss, grad_x, *[grad_w[n] for n in TWIN_WEIGHTS], *[delta_w[n] for n in TWIN_WEIGHTS],
            *[new_m[n] for n in TWIN_WEIGHTS], *[new_v[n] for n in TWIN_WEIGHTS])
```

```python
import functools

import jax
import jax.numpy as jnp
from jax import lax
from jax.experimental import pallas as pl
from jax.experimental.pallas import tpu as pltpu

F32 = jnp.float32
BF16 = jnp.bfloat16
MESH = pl.DeviceIdType.MESH
HIGHEST = lax.Precision.HIGHEST

N_DEV = 8
N_CHIP = 4
LANES = 128
MLA_NOPE = 128
MLA_ROPE = 64
MLA_V = 128
MLA_QK = MLA_NOPE + MLA_ROPE
MLA_QK_PAD = 256
ROPE_THETA = 10000.0
SWA_HD = 64
SWA_KVH = 4
SWA_BLOCK = 128
REL_BUCKETS = 32
REL_MAX_DIST = 128
PAIR = 512
EPS = 1e-6
NEG = -1e30
ADAM_LR = 0.001
ADAM_B1 = 0.9
ADAM_B2 = 0.999
ADAM_EPS = 1e-08
ADAM_WD = 0.01
ADAM_STEP = 10

ANY = pl.BlockSpec(memory_space=pl.ANY)
VMEM_FULL = pl.BlockSpec(memory_space=pltpu.VMEM)
SMEM_FULL = pl.BlockSpec(memory_space=pltpu.SMEM)


def _params(*sem):
    return pltpu.CompilerParams(dimension_semantics=sem if sem else None)


def _tile(n, pref, unit=LANES):
    best = None
    for t in range(unit, min(n, pref) + 1, unit):
        if n % t == 0:
            best = t
    return n if best is None else best


def _matmul(a, b, *, ta=False, tb=False, out_dtype=F32, name):
    if ta:
        K, M = a.shape
    else:
        M, K = a.shape
    if tb:
        N, K2 = b.shape
    else:
        K2, N = b.shape
    assert K == K2, (a.shape, b.shape, ta, tb)
    exact = a.dtype == F32
    tm = _tile(M, 512, 8) if M >= 8 else M
    tn = _tile(N, 1536)
    tk = _tile(K, 1024)
    nk = K // tk
    dn = (((0 if ta else 1,), (1 if tb else 0,)), ((), ()))

    def body(a_ref, b_ref, o_ref, acc_ref):
        k = pl.program_id(2)

        @pl.when(k == 0)
        def _():
            acc_ref[...] = jnp.zeros_like(acc_ref)

        acc_ref[...] += lax.dot_general(a_ref[...], b_ref[...], dn, preferred_element_type=F32,
                                        precision=HIGHEST if exact else None)

        @pl.when(k == nk - 1)
        def _():
            o_ref[...] = acc_ref[...].astype(o_ref.dtype)

    a_spec = pl.BlockSpec((tk, tm), lambda i, j, k: (k, i)) if ta else pl.BlockSpec((tm, tk), lambda i, j, k: (i, k))
    b_spec = pl.BlockSpec((tn, tk), lambda i, j, k: (j, k)) if tb else pl.BlockSpec((tk, tn), lambda i, j, k: (k, j))
    return pl.pallas_call(
        body, name=name,
        out_shape=jax.ShapeDtypeStruct((M, N), out_dtype),
        grid=(M // tm, N // tn, nk),
        in_specs=[a_spec, b_spec],
        out_specs=pl.BlockSpec((tm, tn), lambda i, j, k: (i, j)),
        scratch_shapes=[pltpu.VMEM((tm, tn), F32)],
        compiler_params=_params("parallel", "parallel", "arbitrary"),
    )(a, b)


def _row_tile(S, width):
    return _tile(S, max(8, (1 << 19) // width), 8)


def _rstd(x):
    return lax.rsqrt(jnp.mean(x * x, axis=-1, keepdims=True) + EPS)


def _acc_rows(ref, val, first):
    s = jnp.sum(val, axis=0, keepdims=True)

    @pl.when(first)
    def _():
        ref[...] = s

    @pl.when(jnp.logical_not(first))
    def _():
        ref[...] += s


def _modnorm_fwd(x, g, sc, sh, *, name):
    S, D = x.shape
    tr = _row_tile(S, D)

    def body(x_ref, g_ref, sc_ref, sh_ref, h_ref):
        xv = x_ref[...]
        n = (xv * _rstd(xv)) * g_ref[...]
        h_ref[...] = (n * (1.0 + sc_ref[...]) + sh_ref[...]).astype(BF16)

    row = pl.BlockSpec((tr, D), lambda i: (i, 0))
    vec = pl.BlockSpec((1, D), lambda i: (0, 0))
    return pl.pallas_call(
        body, name=name, out_shape=jax.ShapeDtypeStruct((S, D), BF16), grid=(S // tr,),
        in_specs=[row, vec, vec, vec], out_specs=row, compiler_params=_params("parallel"),
    )(x, g, sc, sh)


def _modnorm_bwd(dh, x, g, sc, dres, *, name):
    S, D = x.shape
    tr = _row_tile(S, D)

    def body(dh_ref, x_ref, g_ref, sc_ref, dres_ref, dx_ref, dg_ref, dsc_ref, dsh_ref):
        first = pl.program_id(0) == 0
        xv = x_ref[...]
        dhv = dh_ref[...]
        gv = g_ref[...]
        r = _rstd(xv)
        xhat = xv * r
        _acc_rows(dsh_ref, dhv, first)
        _acc_rows(dsc_ref, dhv * (xhat * gv), first)
        dn = dhv * (1.0 + sc_ref[...])
        _acc_rows(dg_ref, dn * xhat, first)
        dxhat = dn * gv
        proj = jnp.mean(dxhat * xhat, axis=-1, keepdims=True)
        dx_ref[...] = r * (dxhat - xhat * proj) + dres_ref[...]

    row = pl.BlockSpec((tr, D), lambda i: (i, 0))
    vec = pl.BlockSpec((1, D), lambda i: (0, 0))
    vshape = jax.ShapeDtypeStruct((1, D), F32)
    return pl.pallas_call(
        body, name=name,
        out_shape=(jax.ShapeDtypeStruct((S, D), F32), vshape, vshape, vshape), grid=(S // tr,),
        in_specs=[row, row, vec, vec, row], out_specs=(row, vec, vec, vec),
        compiler_params=_params("arbitrary"),
    )(dh, x, g, sc, dres)


def _resnorm_fwd(xres, m, g, gt, *, name):
    S, D = xres.shape
    tr = _row_tile(S, D)

    def body(x_ref, m_ref, g_ref, gt_ref, o_ref):
        mv = m_ref[...]
        o_ref[...] = x_ref[...] + gt_ref[...] * ((mv * _rstd(mv)) * g_ref[...])

    row = pl.BlockSpec((tr, D), lambda i: (i, 0))
    vec = pl.BlockSpec((1, D), lambda i: (0, 0))
    return pl.pallas_call(
        body, name=name, out_shape=jax.ShapeDtypeStruct((S, D), F32), grid=(S // tr,),
        in_specs=[row, row, vec, vec], out_specs=row, compiler_params=_params("parallel"),
    )(xres, m, g, gt)


def _resnorm_loss(xres, m, g, gt, target, *, name):
    S, D = xres.shape
    tr = _row_tile(S, D)

    def body(x_ref, m_ref, g_ref, gt_ref, t_ref, d_ref, loss_ref):
        mv = m_ref[...]
        out = x_ref[...] + gt_ref[...] * ((mv * _rstd(mv)) * g_ref[...])
        err = out - t_ref[...]
        d_ref[...] = err * (1.0 / D)
        part = 0.5 * jnp.sum(jnp.mean(err * err, axis=-1, keepdims=True), axis=0, keepdims=True)
        part = jnp.broadcast_to(part, loss_ref.shape)

        @pl.when(pl.program_id(0) == 0)
        def _():
            loss_ref[...] = part

        @pl.when(pl.program_id(0) != 0)
        def _():
            loss_ref[...] += part

    row = pl.BlockSpec((tr, D), lambda i: (i, 0))
    vec = pl.BlockSpec((1, D), lambda i: (0, 0))
    return pl.pallas_call(
        body, name=name,
        out_shape=(jax.ShapeDtypeStruct((S, D), F32), jax.ShapeDtypeStruct((8, LANES), F32)), grid=(S // tr,),
        in_specs=[row, row, vec, vec, row], out_specs=(row, pl.BlockSpec((8, LANES), lambda i: (0, 0))),
        compiler_params=_params("arbitrary"),
    )(xres, m, g, gt, target)


def _resnorm_bwd(dout, m, g, gt, *, name):
    S, D = m.shape
    tr = _row_tile(S, D)

    def body(d_ref, m_ref, g_ref, gt_ref, dm_ref, dg_ref, dgt_ref):
        first = pl.program_id(0) == 0
        mv = m_ref[...]
        dv = d_ref[...]
        gv = g_ref[...]
        r = _rstd(mv)
        mhat = mv * r
        _acc_rows(dgt_ref, dv * (mhat * gv), first)
        dn = dv * gt_ref[...]
        _acc_rows(dg_ref, dn * mhat, first)
        dmhat = dn * gv
        proj = jnp.mean(dmhat * mhat, axis=-1, keepdims=True)
        dm_ref[...] = (r * (dmhat - mhat * proj)).astype(BF16)

    row = pl.BlockSpec((tr, D), lambda i: (i, 0))
    vec = pl.BlockSpec((1, D), lambda i: (0, 0))
    vshape = jax.ShapeDtypeStruct((1, D), F32)
    return pl.pallas_call(
        body, name=name, out_shape=(jax.ShapeDtypeStruct((S, D), BF16), vshape, vshape), grid=(S // tr,),
        in_specs=[row, row, vec, vec], out_specs=(row, vec, vec), compiler_params=_params("arbitrary"),
    )(dout, m, g, gt)


def _lat_norm_fwd(z_lat, g_q, g_kv, *, name):
    S, W = z_lat.shape
    Rq, Rkv = g_q.shape[1], g_kv.shape[1]
    tr = _row_tile(S, W)

    def body(z_ref, gq_ref, gkv_ref, nq_ref, nkv_ref):
        cq = z_ref[:, :Rq]
        ckv = z_ref[:, Rq:Rq + Rkv]
        nq_ref[...] = ((cq * _rstd(cq)) * gq_ref[...]).astype(BF16)
        nkv_ref[...] = ((ckv * _rstd(ckv)) * gkv_ref[...]).astype(BF16)

    return pl.pallas_call(
        body, name=name,
        out_shape=(jax.ShapeDtypeStruct((S, Rq), BF16), jax.ShapeDtypeStruct((S, Rkv), BF16)), grid=(S // tr,),
        in_specs=[pl.BlockSpec((tr, W), lambda i: (i, 0)), pl.BlockSpec((1, Rq), lambda i: (0, 0)),
                  pl.BlockSpec((1, Rkv), lambda i: (0, 0))],
        out_specs=(pl.BlockSpec((tr, Rq), lambda i: (i, 0)), pl.BlockSpec((tr, Rkv), lambda i: (i, 0))),
        compiler_params=_params("parallel"),
    )(z_lat, g_q, g_kv)


def _lat_norm_bwd(z_lat, dnq, dnkv, dkr, g_q, g_kv, *, name):
    S, W = z_lat.shape
    Rq, Rkv = g_q.shape[1], g_kv.shape[1]
    tr = _row_tile(S, W)

    def one(c, dn, gv):
        r = _rstd(c)
        chat = c * r
        dchat = dn * gv
        proj = jnp.mean(dchat * chat, axis=-1, keepdims=True)
        return r * (dchat - chat * proj), dn * chat

    def body(z_ref, dnq_ref, dnkv_ref, dkr_ref, gq_ref, gkv_ref, dz_ref, dgq_ref, dgkv_ref):
        first = pl.program_id(0) == 0
        dcq, pq = one(z_ref[:, :Rq], dnq_ref[...], gq_ref[...])
        dckv, pkv = one(z_ref[:, Rq:Rq + Rkv], dnkv_ref[...], gkv_ref[...])
        _acc_rows(dgq_ref, pq, first)
        _acc_rows(dgkv_ref, pkv, first)
        dz_ref[:, :Rq] = dcq.astype(BF16)
        dz_ref[:, Rq:Rq + Rkv] = dckv.astype(BF16)
        dz_ref[:, Rq + Rkv:] = dkr_ref[...].astype(BF16)

    return pl.pallas_call(
        body, name=name,
        out_shape=(jax.ShapeDtypeStruct((S, W), BF16), jax.ShapeDtypeStruct((1, Rq), F32),
                   jax.ShapeDtypeStruct((1, Rkv), F32)), grid=(S // tr,),
        in_specs=[pl.BlockSpec((tr, W), lambda i: (i, 0)), pl.BlockSpec((tr, Rq), lambda i: (i, 0)),
                  pl.BlockSpec((tr, Rkv), lambda i: (i, 0)), pl.BlockSpec((tr, LANES), lambda i: (i, 0)),
                  pl.BlockSpec((1, Rq), lambda i: (0, 0)), pl.BlockSpec((1, Rkv), lambda i: (0, 0))],
        out_specs=(pl.BlockSpec((tr, W), lambda i: (i, 0)), pl.BlockSpec((1, Rq), lambda i: (0, 0)),
                   pl.BlockSpec((1, Rkv), lambda i: (0, 0))),
        compiler_params=_params("arbitrary"),
    )(z_lat, dnq, dnkv, dkr, g_q, g_kv)


def _rot(x, lo32):
    a = pltpu.roll(x, 32, 1)
    b = pltpu.roll(x, LANES - 32, 1)
    return jnp.where(lo32, -b, a)


def _rot_t(g, lo32):
    a = pltpu.roll(g, 32, 1)
    b = pltpu.roll(g, LANES - 32, 1)
    return jnp.where(lo32, b, -a)


def _mla_pack_fwd(q_raw, kv_raw, z_lat, cos, sin, kr_off, *, name):
    S = q_raw.shape[0]
    H = kv_raw.shape[1] // (MLA_NOPE + MLA_V)
    W = z_lat.shape[1]
    scale = MLA_QK ** -0.5
    tr = min(S, 128)
    nope_w = H * MLA_NOPE

    def body(q_ref, kv_ref, z_ref, cos_ref, sin_ref, qp_ref, kp_ref, v_ref):
        lane = lax.broadcasted_iota(jnp.int32, (tr, LANES), 1)
        lo32 = (lane % 64) < 32
        lo64 = lane < 64
        c = cos_ref[...]
        s = sin_ref[...]
        kr = z_ref[:, kr_off:kr_off + LANES]
        kr = (kr * c + _rot(kr, lo32) * s).astype(BF16)
        for hp in range(H // 2):
            xb = q_ref[:, nope_w + hp * LANES:nope_w + (hp + 1) * LANES]
            rb = (xb * c + _rot(xb, lo32) * s) * scale
            for e in range(2):
                h = 2 * hp + e
                base = h * MLA_QK_PAD
                qp_ref[:, base:base + LANES] = (q_ref[:, h * LANES:(h + 1) * LANES] * scale).astype(BF16)
                keep = lo64 if e == 0 else jnp.logical_not(lo64)
                qp_ref[:, base + LANES:base + 2 * LANES] = jnp.where(keep, rb, 0.0).astype(BF16)
                kp_ref[:, base:base + LANES] = kv_ref[:, h * LANES:(h + 1) * LANES].astype(BF16)
                kp_ref[:, base + LANES:base + 2 * LANES] = kr
        v_ref[...] = kv_ref[:, nope_w:].astype(BF16)

    return pl.pallas_call(
        body, name=name,
        out_shape=(jax.ShapeDtypeStruct((S, H * MLA_QK_PAD), BF16), jax.ShapeDtypeStruct((S, H * MLA_QK_PAD), BF16),
                   jax.ShapeDtypeStruct((S, H * MLA_V), BF16)), grid=(S // tr,),
        in_specs=[pl.BlockSpec((tr, q_raw.shape[1]), lambda i: (i, 0)), pl.BlockSpec((tr, kv_raw.shape[1]), lambda i: (i, 0)),
                  pl.BlockSpec((tr, W), lambda i: (i, 0)), pl.BlockSpec((tr, LANES), lambda i: (i, 0)),
                  pl.BlockSpec((tr, LANES), lambda i: (i, 0))],
        out_specs=(pl.BlockSpec((tr, H * MLA_QK_PAD), lambda i: (i, 0)), pl.BlockSpec((tr, H * MLA_QK_PAD), lambda i: (i, 0)),
                   pl.BlockSpec((tr, H * MLA_V), lambda i: (i, 0))),
        compiler_params=_params("parallel"),
    )(q_raw, kv_raw, z_lat, cos, sin)


def _mla_pack_bwd(dqp, dkp, dv, cos, sin, *, name):
    S = dqp.shape[0]
    H = dv.shape[1] // MLA_V
    scale = MLA_QK ** -0.5
    tr = min(S, 128)
    nope_w = H * MLA_NOPE

    def body(dqp_ref, dkp_ref, dv_ref, cos_ref, sin_ref, dq_ref, dkv_ref, dkr_ref):
        lane = lax.broadcasted_iota(jnp.int32, (tr, LANES), 1)
        lo32 = (lane % 64) < 32
        lo64 = lane < 64
        c = cos_ref[...]
        s = sin_ref[...]
        dkr2 = jnp.zeros((tr, LANES), F32)
        for hp in range(H // 2):
            be = (2 * hp) * MLA_QK_PAD
            bo = (2 * hp + 1) * MLA_QK_PAD
            g = jnp.where(lo64, dqp_ref[:, be + LANES:be + 2 * LANES], dqp_ref[:, bo + LANES:bo + 2 * LANES]) * scale
            dq_ref[:, nope_w + hp * LANES:nope_w + (hp + 1) * LANES] = (g * c + _rot_t(g * s, lo32)).astype(BF16)
            for h, base in ((2 * hp, be), (2 * hp + 1, bo)):
                dq_ref[:, h * LANES:(h + 1) * LANES] = (dqp_ref[:, base:base + LANES] * scale).astype(BF16)
                dkv_ref[:, h * LANES:(h + 1) * LANES] = dkp_ref[:, base:base + LANES].astype(BF16)
                dkr2 = dkr2 + dkp_ref[:, base + LANES:base + 2 * LANES]
        dkr2 = dkr2 * c + _rot_t(dkr2 * s, lo32)
        dkr2 = dkr2 + pltpu.roll(dkr2, 64, 1)
        dkr_ref[...] = jnp.where(lo64, dkr2, 0.0)
        dkv_ref[:, nope_w:] = dv_ref[...].astype(BF16)

    return pl.pallas_call(
        body, name=name,
        out_shape=(jax.ShapeDtypeStruct((S, nope_w + H * MLA_ROPE), BF16), jax.ShapeDtypeStruct((S, 2 * nope_w), BF16),
                   jax.ShapeDtypeStruct((S, LANES), F32)), grid=(S // tr,),
        in_specs=[pl.BlockSpec((tr, H * MLA_QK_PAD), lambda i: (i, 0)), pl.BlockSpec((tr, H * MLA_QK_PAD), lambda i: (i, 0)),
                  pl.BlockSpec((tr, H * MLA_V), lambda i: (i, 0)), pl.BlockSpec((tr, LANES), lambda i: (i, 0)),
                  pl.BlockSpec((tr, LANES), lambda i: (i, 0))],
        out_specs=(pl.BlockSpec((tr, nope_w + H * MLA_ROPE), lambda i: (i, 0)), pl.BlockSpec((tr, 2 * nope_w), lambda i: (i, 0)),
                   pl.BlockSpec((tr, LANES), lambda i: (i, 0))),
        compiler_params=_params("parallel"),
    )(dqp, dkp, dv, cos, sin)


def _causal_scores(q, k, i, j, t):
    s = lax.dot_general(q, k, (((1,), (1,)), ((), ())), preferred_element_type=F32)
    row = lax.broadcasted_iota(jnp.int32, (t, t), 0) + i * t
    col = lax.broadcasted_iota(jnp.int32, (t, t), 1) + j * t
    return jnp.where(col <= row, s, NEG)


def _flash_fwd(qp, kp, v, *, name):
    S = qp.shape[0]
    H = v.shape[1] // MLA_V
    t = min(S, 512)
    nb = S // t

    def body(q_ref, k_ref, v_ref, o_ref, lse_ref, m_s, l_s, acc_s):
        i = pl.program_id(1)
        j = pl.program_id(2)

        @pl.when(j == 0)
        def _():
            m_s[...] = jnp.full_like(m_s, NEG)
            l_s[...] = jnp.zeros_like(l_s)
            acc_s[...] = jnp.zeros_like(acc_s)

        @pl.when(j <= i)
        def _():
            s = _causal_scores(q_ref[...], k_ref[...], i, j, t)
            m_prev = m_s[...]
            m_cur = jnp.maximum(m_prev, jnp.max(s, axis=1, keepdims=True))
            alpha = jnp.exp(m_prev - m_cur)
            p = jnp.exp(s - m_cur[:, :1])
            l_s[...] = alpha * l_s[...] + jnp.sum(p, axis=1, keepdims=True)
            acc_s[...] = alpha * acc_s[...] + jnp.dot(p.astype(BF16), v_ref[...], preferred_element_type=F32)
            m_s[...] = m_cur

        @pl.when(j == nb - 1)
        def _():
            o_ref[...] = acc_s[...] / l_s[...]
            lse_ref[...] = m_s[...] + jnp.log(l_s[...])

    return pl.pallas_call(
        body, name=name,
        out_shape=(jax.ShapeDtypeStruct((S, H * MLA_V), F32), jax.ShapeDtypeStruct((H, S, LANES), F32)),
        grid=(H, nb, nb),
        in_specs=[pl.BlockSpec((t, MLA_QK_PAD), lambda h, i, j: (i, h)),
                  pl.BlockSpec((t, MLA_QK_PAD), lambda h, i, j: (jnp.minimum(j, i), h)),
                  pl.BlockSpec((t, MLA_V), lambda h, i, j: (jnp.minimum(j, i), h))],
        out_specs=(pl.BlockSpec((t, MLA_V), lambda h, i, j: (i, h)),
                   pl.BlockSpec((None, t, LANES), lambda h, i, j: (h, i, 0))),
        scratch_shapes=[pltpu.VMEM((t, LANES), F32), pltpu.VMEM((t, LANES), F32), pltpu.VMEM((t, MLA_V), F32)],
        compiler_params=_params("parallel", "parallel", "arbitrary"),
    )(qp, kp, v)


def _flash_probs(q_ref, k_ref, v_ref, o_ref, do_ref, lse_ref, i, j, t):
    s = _causal_scores(q_ref[...], k_ref[...], i, j, t)
    p = jnp.exp(s - lse_ref[...][:, :1])
    do = do_ref[...]
    delta = jnp.sum(do.astype(F32) * o_ref[...], axis=1, keepdims=True)
    dp = lax.dot_general(do, v_ref[...], (((1,), (1,)), ((), ())), preferred_element_type=F32)
    return p, p * (dp - delta), do


def _flash_bwd_dkv(qp, kp, v, o, do, lse, *, name):
    S = qp.shape[0]
    H = v.shape[1] // MLA_V
    t = min(S, 512)
    nb = S // t
    tn = (((0,), (0,)), ((), ()))

    def body(q_ref, k_ref, v_ref, o_ref, do_ref, lse_ref, dk_ref, dv_ref, dk_s, dv_s):
        j = pl.program_id(1)
        i = pl.program_id(2)

        @pl.when(i == 0)
        def _():
            dk_s[...] = jnp.zeros_like(dk_s)
            dv_s[...] = jnp.zeros_like(dv_s)

        @pl.when(i >= j)
        def _():
            p, ds, dob = _flash_probs(q_ref, k_ref, v_ref, o_ref, do_ref, lse_ref, i, j, t)
            dv_s[...] += lax.dot_general(p.astype(BF16), dob, tn, preferred_element_type=F32)
            dk_s[...] += lax.dot_general(ds.astype(BF16), q_ref[...], tn, preferred_element_type=F32)

        @pl.when(i == nb - 1)
        def _():
            dk_ref[...] = dk_s[...]
            dv_ref[...] = dv_s[...]

    qside = lambda h, j, i: (jnp.maximum(i, j), h)
    kside = lambda h, j, i: (j, h)
    return pl.pallas_call(
        body, name=name,
        out_shape=(jax.ShapeDtypeStruct((S, H * MLA_QK_PAD), F32), jax.ShapeDtypeStruct((S, H * MLA_V), F32)),
        grid=(H, nb, nb),
        in_specs=[pl.BlockSpec((t, MLA_QK_PAD), qside), pl.BlockSpec((t, MLA_QK_PAD), kside),
                  pl.BlockSpec((t, MLA_V), kside), pl.BlockSpec((t, MLA_V), qside), pl.BlockSpec((t, MLA_V), qside),
                  pl.BlockSpec((None, t, LANES), lambda h, j, i: (h, jnp.maximum(i, j), 0))],
        out_specs=(pl.BlockSpec((t, MLA_QK_PAD), kside), pl.BlockSpec((t, MLA_V), kside)),
        scratch_shapes=[pltpu.VMEM((t, MLA_QK_PAD), F32), pltpu.VMEM((t, MLA_V), F32)],
        compiler_params=_params("parallel", "parallel", "arbitrary"),
    )(qp, kp, v, o, do, lse)


def _flash_bwd_dq(qp, kp, v, o, do, lse, *, name):
    S = qp.shape[0]
    H = v.shape[1] // MLA_V
    t = min(S, 512)
    nb = S // t

    def body(q_ref, k_ref, v_ref, o_ref, do_ref, lse_ref, dq_ref, dq_s):
        i = pl.program_id(1)
        j = pl.program_id(2)

        @pl.when(j == 0)
        def _():
            dq_s[...] = jnp.zeros_like(dq_s)

        @pl.when(j <= i)
        def _():
            _, ds, _ = _flash_probs(q_ref, k_ref, v_ref, o_ref, do_ref, lse_ref, i, j, t)
            dq_s[...] += jnp.dot(ds.astype(BF16), k_ref[...], preferred_element_type=F32)

        @pl.when(j == nb - 1)
        def _():
            dq_ref[...] = dq_s[...]

    qside = lambda h, i, j: (i, h)
    kside = lambda h, i, j: (jnp.minimum(j, i), h)
    return pl.pallas_call(
        body, name=name,
        out_shape=jax.ShapeDtypeStruct((S, H * MLA_QK_PAD), F32),
        grid=(H, nb, nb),
        in_specs=[pl.BlockSpec((t, MLA_QK_PAD), qside), pl.BlockSpec((t, MLA_QK_PAD), kside),
                  pl.BlockSpec((t, MLA_V), kside), pl.BlockSpec((t, MLA_V), qside), pl.BlockSpec((t, MLA_V), qside),
                  pl.BlockSpec((None, t, LANES), lambda h, i, j: (h, i, 0))],
        out_specs=pl.BlockSpec((t, MLA_QK_PAD), qside),
        scratch_shapes=[pltpu.VMEM((t, MLA_QK_PAD), F32)],
        compiler_params=_params("parallel", "parallel", "arbitrary"),
    )(qp, kp, v, o, do, lse)


def _swa_kv_halves(blk, hf, lo):
    if hf == 0:
        a = jnp.where(lo, blk, 0.0)
        b = pltpu.roll(a, 64, 1)
    else:
        b = jnp.where(lo, 0.0, blk)
        a = pltpu.roll(b, 64, 1)
    return a.astype(BF16), b.astype(BF16)


def _swa_softmax(qb, kx, bias, neg0, sk):
    s = lax.dot_general(qb, kx, (((1,), (1,)), ((), ())), preferred_element_type=F32) + bias + neg0
    m = jnp.maximum(jnp.max(s, axis=1, keepdims=True), sk)
    e = jnp.exp(s - m)
    es = jnp.exp(sk - m)
    inv = 1.0 / (jnp.sum(e, axis=1, keepdims=True) + es)
    return e * inv, es * inv


def _swa_fwd(z_swa, bias_m, sinks, *, name):
    S, W = z_swa.shape
    NH = bias_m.shape[0]
    G = NH // SWA_KVH
    QW = NH * SWA_HD
    KW = SWA_KVH * SWA_HD
    nb = S // SWA_BLOCK
    B = SWA_BLOCK
    assert G % 2 == 0 and SWA_KVH % 2 == 0 and W == QW + 2 * KW

    def body(sink_ref, q_ref, kvc_ref, kvp_ref, b_ref, o_ref):
        n = pl.program_id(0)
        lo = lax.broadcasted_iota(jnp.int32, (2 * B, LANES), 1) < 64
        col = lax.broadcasted_iota(jnp.int32, (B, 2 * B), 1)
        neg0 = jnp.where(jnp.logical_and(col < B, n == 0), NEG, 0.0)
        for kb in range(SWA_KVH // 2):
            kblk = jnp.concatenate([kvp_ref[:, kb * LANES:(kb + 1) * LANES], kvc_ref[:, kb * LANES:(kb + 1) * LANES]], axis=0)
            vblk = jnp.concatenate([kvp_ref[:, KW + kb * LANES:KW + (kb + 1) * LANES],
                                    kvc_ref[:, KW + kb * LANES:KW + (kb + 1) * LANES]], axis=0)
            for hf in range(2):
                kvh = 2 * kb + hf
                ks = _swa_kv_halves(kblk, hf, lo)
                vs = _swa_kv_halves(vblk, hf, lo)
                for pb in range(G // 2):
                    P = kvh * (G // 2) + pb
                    qb = (q_ref[:, P * LANES:(P + 1) * LANES] * (SWA_HD ** -0.5)).astype(BF16)
                    acc = jnp.zeros((B, LANES), F32)
                    for e in range(2):
                        h = 2 * P + e
                        p, _ = _swa_softmax(qb, ks[e], b_ref[h], neg0, sink_ref[h])
                        acc = acc + jnp.dot(p.astype(BF16), vs[e], preferred_element_type=F32)
                    o_ref[:, P * LANES:(P + 1) * LANES] = acc

    kvcol = QW // (2 * KW)
    assert QW % (2 * KW) == 0
    return pl.pallas_call(
        body, name=name,
        out_shape=jax.ShapeDtypeStruct((S, QW), F32), grid=(nb,),
        in_specs=[SMEM_FULL, pl.BlockSpec((B, QW), lambda n: (n, 0)), pl.BlockSpec((B, 2 * KW), lambda n: (n, kvcol)),
                  pl.BlockSpec((B, 2 * KW), lambda n: (jnp.maximum(n - 1, 0), kvcol)),
                  pl.BlockSpec((NH, B, 2 * B), lambda n: (0, 0, 0))],
        out_specs=pl.BlockSpec((B, QW), lambda n: (n, 0)),
        compiler_params=_params("parallel"),
    )(sinks, z_swa, z_swa, z_swa, bias_m)


def _swa_bwd(z_swa, bias_m, sinks, o, do, *, name):
    S, W = z_swa.shape
    NH = bias_m.shape[0]
    G = NH // SWA_KVH
    QW = NH * SWA_HD
    KW = SWA_KVH * SWA_HD
    nb = S // SWA_BLOCK
    B = SWA_BLOCK
    scale = SWA_HD ** -0.5
    tn = (((0,), (0,)), ((), ()))
    nt = (((1,), (1,)), ((), ()))

    def fold(x, hf, lo):
        x = x + pltpu.roll(x, 64, 1)
        return jnp.where(lo, x, 0.0) if hf == 0 else jnp.where(lo, 0.0, x)

    def body(sink_ref, q_ref, kvc_ref, kvp_ref, b_ref, o_ref, do_ref, dz_ref, dbias_ref, dsink_ref,
             cq_s, ck_s, cv_s, nq_s, nk_s, nv_s, pk_s, pv_s):
        n = pl.program_id(0)

        @pl.when(n == 0)
        def _():
            dbias_ref[...] = jnp.zeros_like(dbias_ref)
            dsink_ref[...] = jnp.zeros_like(dsink_ref)
            cq_s[...] = jnp.zeros_like(cq_s)
            ck_s[...] = jnp.zeros_like(ck_s)
            cv_s[...] = jnp.zeros_like(cv_s)

        @pl.when(n == nb)
        def _():
            pk_s[...] = jnp.zeros_like(pk_s)
            pv_s[...] = jnp.zeros_like(pv_s)

        @pl.when(n < nb)
        def _():
            lo = lax.broadcasted_iota(jnp.int32, (2 * B, LANES), 1) < 64
            lo1 = lax.broadcasted_iota(jnp.int32, (B, LANES), 1) < 64
            lane1 = lax.broadcasted_iota(jnp.int32, (1, LANES), 1)
            col = lax.broadcasted_iota(jnp.int32, (B, 2 * B), 1)
            neg0 = jnp.where(jnp.logical_and(col < B, n == 0), NEG, 0.0)
            dsink = jnp.zeros((1, LANES), F32)
            for kb in range(SWA_KVH // 2):
                kblk = jnp.concatenate([kvp_ref[:, kb * LANES:(kb + 1) * LANES], kvc_ref[:, kb * LANES:(kb + 1) * LANES]], axis=0)
                vblk = jnp.concatenate([kvp_ref[:, KW + kb * LANES:KW + (kb + 1) * LANES],
                                        kvc_ref[:, KW + kb * LANES:KW + (kb + 1) * LANES]], axis=0)
                dkblk = jnp.zeros((2 * B, LANES), F32)
                dvblk = jnp.zeros((2 * B, LANES), F32)
                for hf in range(2):
                    kvh = 2 * kb + hf
                    ks = _swa_kv_halves(kblk, hf, lo)
                    vs = _swa_kv_halves(vblk, hf, lo)
                    dkj = jnp.zeros((2 * B, LANES), F32)
                    dvj = jnp.zeros((2 * B, LANES), F32)
                    for pb in range(G // 2):
                        P = kvh * (G // 2) + pb
                        qb = (q_ref[:, P * LANES:(P + 1) * LANES] * scale).astype(BF16)
                        dob = do_ref[:, P * LANES:(P + 1) * LANES]
                        prod = dob * o_ref[:, P * LANES:(P + 1) * LANES]
                        dob = dob.astype(BF16)
                        dqp = jnp.zeros((B, LANES), F32)
                        for e in range(2):
                            h = 2 * P + e
                            keep = lo1 if e == 0 else jnp.logical_not(lo1)
                            p, psink = _swa_softmax(qb, ks[e], b_ref[h], neg0, sink_ref[h])
                            delta = jnp.sum(jnp.where(keep, prod, 0.0), axis=1, keepdims=True)
                            dp = lax.dot_general(dob, vs[e], nt, preferred_element_type=F32)
                            ds = p * (dp - delta)
                            dbias_ref[h] += ds
                            dsh = -jnp.sum(psink * delta, axis=0, keepdims=True)
                            dsink = dsink + jnp.where(lane1 == h, dsh, 0.0)
                            dsb = ds.astype(BF16)
                            dqp = dqp + jnp.dot(dsb, ks[e], preferred_element_type=F32)
                            keep2 = lo if e == 0 else jnp.logical_not(lo)
                            dkj = dkj + jnp.where(keep2, lax.dot_general(dsb, qb, tn, preferred_element_type=F32), 0.0)
                            dvj = dvj + jnp.where(keep2, lax.dot_general(p.astype(BF16), dob, tn, preferred_element_type=F32), 0.0)
                        nq_s[:, P * LANES:(P + 1) * LANES] = dqp * scale
                    dkblk = dkblk + fold(dkj, hf, lo)
                    dvblk = dvblk + fold(dvj, hf, lo)
                pk_s[:, kb * LANES:(kb + 1) * LANES] = dkblk[:B]
                nk_s[:, kb * LANES:(kb + 1) * LANES] = dkblk[B:]
                pv_s[:, kb * LANES:(kb + 1) * LANES] = dvblk[:B]
                nv_s[:, kb * LANES:(kb + 1) * LANES] = dvblk[B:]
            dsink_ref[...] += dsink

        dz_ref[:, :QW] = cq_s[...].astype(BF16)
        dz_ref[:, QW:QW + KW] = (ck_s[...] + pk_s[...]).astype(BF16)
        dz_ref[:, QW + KW:] = (cv_s[...] + pv_s[...]).astype(BF16)

        @pl.when(n < nb)
        def _():
            cq_s[...] = nq_s[...]
            ck_s[...] = nk_s[...]
            cv_s[...] = nv_s[...]

    kvcol = QW // (2 * KW)
    cur = lambda n: (jnp.minimum(n, nb - 1), 0)
    return pl.pallas_call(
        body, name=name,
        out_shape=(jax.ShapeDtypeStruct((S, W), BF16), jax.ShapeDtypeStruct((NH, B, 2 * B), F32),
                   jax.ShapeDtypeStruct((1, LANES), F32)),
        grid=(nb + 1,),
        in_specs=[SMEM_FULL, pl.BlockSpec((B, QW), cur), pl.BlockSpec((B, 2 * KW), lambda n: (jnp.minimum(n, nb - 1), kvcol)),
                  pl.BlockSpec((B, 2 * KW), lambda n: (jnp.maximum(jnp.minimum(n, nb - 1) - 1, 0), kvcol)),
                  pl.BlockSpec((NH, B, 2 * B), lambda n: (0, 0, 0)), pl.BlockSpec((B, QW), cur), pl.BlockSpec((B, QW), cur)],
        out_specs=(pl.BlockSpec((B, W), lambda n: (jnp.maximum(n - 1, 0), 0)),
                   pl.BlockSpec((NH, B, 2 * B), lambda n: (0, 0, 0)), pl.BlockSpec((1, LANES), lambda n: (0, 0))),
        scratch_shapes=[pltpu.VMEM((B, QW), F32), pltpu.VMEM((B, KW), F32), pltpu.VMEM((B, KW), F32),
                        pltpu.VMEM((B, QW), F32), pltpu.VMEM((B, KW), F32), pltpu.VMEM((B, KW), F32),
                        pltpu.VMEM((B, KW), F32), pltpu.VMEM((B, KW), F32)],
        compiler_params=_params("arbitrary"),
    )(sinks, z_swa, z_swa, z_swa, bias_m, o, do)


def _gate_fwd(zg, o_a, o_b, *, name):
    S, D = o_a.shape
    tr = min(S, 256)

    def body(z_ref, a_ref, b_ref, m_ref):
        ga = jax.nn.sigmoid(z_ref[:, :PAIR])
        gb = jax.nn.sigmoid(z_ref[:, PAIR:])
        m_ref[...] = (ga * a_ref[...] + gb * b_ref[...]).astype(BF16)

    col = pl.BlockSpec((tr, PAIR), lambda i, j: (i, j))
    return pl.pallas_call(
        body, name=name, out_shape=jax.ShapeDtypeStruct((S, D), BF16), grid=(S // tr, D // PAIR),
        in_specs=[pl.BlockSpec((tr, 2 * PAIR), lambda i, j: (i, j)), col, col], out_specs=col,
        compiler_params=_params("parallel", "parallel"),
    )(zg, o_a, o_b)


def _gate_bwd(dmix, zg, o_a, o_b, *, name):
    S, D = o_a.shape
    tr = min(S, 256)

    def body(d_ref, z_ref, a_ref, b_ref, da_ref, db_ref, dz_ref):
        d = d_ref[...]
        ga = jax.nn.sigmoid(z_ref[:, :PAIR])
        gb = jax.nn.sigmoid(z_ref[:, PAIR:])
        da_ref[...] = (d * ga).astype(BF16)
        db_ref[...] = d * gb
        dz_ref[:, :PAIR] = (d * a_ref[...] * (ga * (1.0 - ga))).astype(BF16)
        dz_ref[:, PAIR:] = (d * b_ref[...] * (gb * (1.0 - gb))).astype(BF16)

    col = pl.BlockSpec((tr, PAIR), lambda i, j: (i, j))
    wide = pl.BlockSpec((tr, 2 * PAIR), lambda i, j: (i, j))
    return pl.pallas_call(
        body, name=name,
        out_shape=(jax.ShapeDtypeStruct((S, D), BF16), jax.ShapeDtypeStruct((S, D), F32), jax.ShapeDtypeStruct((S, 2 * D), BF16)),
        grid=(S // tr, D // PAIR), in_specs=[col, wide, col, col], out_specs=(col, col, wide),
        compiler_params=_params("parallel", "parallel"),
    )(dmix, zg, o_a, o_b)


def _conv_taps(t_ref, prev_ref, i, tr):
    cur = t_ref[...]
    live = (i > 0).astype(F32)
    p6 = prev_ref[6:7, :] * live
    p7 = prev_ref[7:8, :] * live
    row = lax.broadcasted_iota(jnp.int32, cur.shape, 0)
    t1 = jnp.where(row == 0, p7, pltpu.roll(cur, 1, 0))
    t2 = jnp.where(row == 0, p6, jnp.where(row == 1, p7, pltpu.roll(cur, 2, 0)))
    return cur, t1, t2


def _conv_u(t_ref, prev_ref, w_ref, b_ref, i, tr):
    cur, t1, t2 = _conv_taps(t_ref, prev_ref, i, tr)
    u = ((b_ref[...] + w_ref[0:1, :] * t2) + w_ref[1:2, :] * t1) + w_ref[2:3, :] * cur
    return u, cur, t1, t2


def _conv_specs(tr, S):
    blk = pl.BlockSpec((tr, 2 * PAIR), lambda j, i: (i, j))
    prev = pl.BlockSpec((8, 2 * PAIR), lambda j, i: (jnp.maximum(i * (tr // 8) - 1, 0), j))
    w3 = pl.BlockSpec((3, 2 * PAIR), lambda j, i: (0, j))
    w1 = pl.BlockSpec((1, 2 * PAIR), lambda j, i: (0, j))
    return blk, prev, w3, w1


def _conv_gate_fwd(t, cw, cb, *, name):
    S, F2 = t.shape
    tr = min(S, 256)
    blk, prev, w3, w1 = _conv_specs(tr, S)

    def body(t_ref, prev_ref, w_ref, b_ref, a_ref):
        u, _, _, _ = _conv_u(t_ref, prev_ref, w_ref, b_ref, pl.program_id(1), tr)
        a_ref[...] = (jax.nn.silu(u[:, :PAIR]) * u[:, PAIR:]).astype(BF16)

    return pl.pallas_call(
        body, name=name, out_shape=jax.ShapeDtypeStruct((S, F2 // 2), BF16), grid=(F2 // (2 * PAIR), S // tr),
        in_specs=[blk, prev, w3, w1], out_specs=pl.BlockSpec((tr, PAIR), lambda j, i: (i, j)),
        compiler_params=_params("parallel", "parallel"),
    )(t, t, cw, cb)


def _conv_gate_bwd(t, da, cw, cb, *, name):
    S, F2 = t.shape
    tr = min(S, 256)
    blk, prev, w3, w1 = _conv_specs(tr, S)

    def body(t_ref, prev_ref, da_ref, w_ref, b_ref, du_ref, dw_ref, db_ref):
        i = pl.program_id(1)
        u, cur, t1, t2 = _conv_u(t_ref, prev_ref, w_ref, b_ref, i, tr)
        u1 = u[:, :PAIR]
        u2 = u[:, PAIR:]
        d = da_ref[...]
        sg = jax.nn.sigmoid(u1)
        du1 = d * u2 * (sg * (1.0 + u1 * (1.0 - sg)))
        du2 = d * (u1 * sg)
        du = jnp.concatenate([du1, du2], axis=1)
        du_ref[...] = du.astype(BF16)
        first = i == 0
        _acc_rows(db_ref, du, first)
        dw = jnp.concatenate([jnp.sum(du * t2, axis=0, keepdims=True), jnp.sum(du * t1, axis=0, keepdims=True),
                              jnp.sum(du * cur, axis=0, keepdims=True)], axis=0)

        @pl.when(first)
        def _():
            dw_ref[...] = dw

        @pl.when(jnp.logical_not(first))
        def _():
            dw_ref[...] += dw

    return pl.pallas_call(
        body, name=name,
        out_shape=(jax.ShapeDtypeStruct((S, F2), BF16), jax.ShapeDtypeStruct((3, F2), F32), jax.ShapeDtypeStruct((1, F2), F32)),
        grid=(F2 // (2 * PAIR), S // tr),
        in_specs=[blk, prev, pl.BlockSpec((tr, PAIR), lambda j, i: (i, j)), w3, w1], out_specs=(blk, w3, w1),
        compiler_params=_params("parallel", "arbitrary"),
    )(t, t, da, cw, cb)


def _conv_bwd_dt(du, cw, *, name):
    S, F2 = du.shape
    tr = min(S, 256)
    nrow = S // tr
    blk, _, w3, _ = _conv_specs(tr, S)
    nxt = pl.BlockSpec((16, 2 * PAIR), lambda j, i: (jnp.minimum((i + 1) * (tr // 16), S // 16 - 1), j))

    def body(d_ref, next_ref, w_ref, dt_ref):
        i = pl.program_id(1)
        cur = d_ref[...].astype(F32)
        live = (i < nrow - 1).astype(F32)
        n0 = next_ref[0:1, :].astype(F32) * live
        n1 = next_ref[1:2, :].astype(F32) * live
        row = lax.broadcasted_iota(jnp.int32, cur.shape, 0)
        d1 = jnp.where(row == tr - 1, n0, pltpu.roll(cur, tr - 1, 0))
        d2 = jnp.where(row == tr - 1, n1, jnp.where(row == tr - 2, n0, pltpu.roll(cur, tr - 2, 0)))
        dt_ref[...] = ((w_ref[2:3, :] * cur + w_ref[1:2, :] * d1) + w_ref[0:1, :] * d2).astype(BF16)

    return pl.pallas_call(
        body, name=name, out_shape=jax.ShapeDtypeStruct((S, F2), BF16), grid=(F2 // (2 * PAIR), nrow),
        in_specs=[blk, nxt, w3], out_specs=blk, compiler_params=_params("parallel", "parallel"),
    )(du, du, cw)


def _ada_fwd(c_all, w, b, *, name):
    Bn, D = c_all.shape
    N = w.shape[1]
    tn = _tile(N, 512)

    def body(c_ref, w_ref, b_ref, o_ref):
        o_ref[...] = jnp.dot(jax.nn.silu(c_ref[...]), w_ref[...], preferred_element_type=F32, precision=HIGHEST) + b_ref[...]

    return pl.pallas_call(
        body, name=name, out_shape=jax.ShapeDtypeStruct((Bn, N), F32), grid=(N // tn,),
        in_specs=[pl.BlockSpec((Bn, D), lambda j: (0, 0)), pl.BlockSpec((D, tn), lambda j: (0, j)),
                  pl.BlockSpec((1, tn), lambda j: (0, j))],
        out_specs=pl.BlockSpec((Bn, tn), lambda j: (0, j)), compiler_params=_params("parallel"),
    )(c_all, w, b)


def _ada_bwd(c_all_t, dmod, *, name):
    D, Bn = c_all_t.shape
    N = dmod.shape[1]
    tm = _tile(D, 512, 8)
    tn = _tile(N, 1536)

    def body(c_ref, d_ref, o_ref):
        o_ref[...] = jnp.dot(jax.nn.silu(c_ref[...]), d_ref[...], preferred_element_type=F32, precision=HIGHEST)

    return pl.pallas_call(
        body, name=name, out_shape=jax.ShapeDtypeStruct((D, N), F32), grid=(D // tm, N // tn),
        in_specs=[pl.BlockSpec((tm, Bn), lambda i, j: (i, 0)), pl.BlockSpec((Bn, tn), lambda i, j: (0, j))],
        out_specs=pl.BlockSpec((tm, tn), lambda i, j: (i, j)), compiler_params=_params("parallel", "parallel"),
    )(c_all_t, dmod)


def _adamw(w, g, m, v, *, name):
    R, C = w.shape
    tr = R if R * C <= (1 << 18) else _tile(R, max(8, (1 << 18) // C), 8)

    def body(w_ref, g_ref, m_ref, v_ref, d_ref, nm_ref, nv_ref):
        gv = g_ref[...]
        nm = ADAM_B1 * m_ref[...] + (1.0 - ADAM_B1) * gv
        nv = ADAM_B2 * v_ref[...] + (1.0 - ADAM_B2) * (gv * gv)
        m_hat = nm / (1.0 - ADAM_B1 ** ADAM_STEP)
        v_hat = nv / (1.0 - ADAM_B2 ** ADAM_STEP)
        d_ref[...] = -ADAM_LR * (m_hat / (jnp.sqrt(v_hat) + ADAM_EPS) + ADAM_WD * w_ref[...])
        nm_ref[...] = nm
        nv_ref[...] = nv

    blk = pl.BlockSpec((tr, C), lambda i: (i, 0))
    shp = jax.ShapeDtypeStruct((R, C), F32)
    return pl.pallas_call(
        body, name=name, out_shape=(shp, shp, shp), grid=(R // tr,), in_specs=[blk] * 4, out_specs=(blk,) * 3,
        compiler_params=_params("parallel"),
    )(w, g, m, v)


def _place():
    x, y, c = lax.axis_index("x"), lax.axis_index("y"), lax.axis_index("c")
    return x, y, c, [(1 - x, y), (x, 1 - y), (1 - x, 1 - y)]


def _remote(src, dst, send_sem, recv_sem, dev):
    return pltpu.make_async_remote_copy(src_ref=src, dst_ref=dst, send_sem=send_sem, recv_sem=recv_sem,
                                        device_id=dev, device_id_type=MESH)


def _allgather8(v, *, name):
    R, C = v.shape

    def body(v_ref, out_ref, send_sems, recv_sems, local_sem):
        x, y, c, chips = _place()
        me, sibling = (x, y, c), (x, y, 1 - c)

        def rows(px, py, pc):
            return out_ref.at[pl.ds((4 * px + 2 * py + pc) * R, R), :]

        def copy(k, block, to, src=None):
            return _remote(rows(*block) if src is None else src, rows(*block), send_sems.at[k], recv_sems.at[k], to)

        mine = pltpu.make_async_copy(v_ref, rows(*me), local_sem)
        mine.start()
        first = [copy(0, me, sibling, src=v_ref)]
        first += [copy(1 + j, me, (*chip, c), src=v_ref) for j, chip in enumerate(chips)]
        for cp in first:
            cp.start()
        passed = [copy(4 + j, (*chip, c), sibling) for j, chip in enumerate(chips)]
        for j, chip in enumerate(chips):
            copy(1 + j, (*chip, c), me).wait_recv()
            passed[j].start()
        copy(0, sibling, me).wait_recv()
        for j, chip in enumerate(chips):
            copy(4 + j, (*chip, 1 - c), me).wait_recv()
        for cp in first + passed:
            cp.wait_send()
        mine.wait()

    out = pl.pallas_call(
        body, name=name, out_shape=jax.ShapeDtypeStruct((N_DEV * R, C), v.dtype),
        in_specs=[VMEM_FULL], out_specs=VMEM_FULL,
        scratch_shapes=[pltpu.SemaphoreType.DMA((7,)), pltpu.SemaphoreType.DMA((7,)), pltpu.SemaphoreType.DMA],
    )(v)
    return out.reshape(N_DEV, R, C)


def _gather_weights(ws, *, name):
    n = len(ws)

    def body(*refs):
        ins, outs = refs[:n], refs[n:2 * n]
        send_sems, recv_sems, local_sems = refs[2 * n:]
        x, y, c, chips = _place()
        k = 2 * x + y
        sibling = (x, y, 1 - c)

        def piece(i, chip_idx, who):
            r2 = ws[i].shape[0] // 2
            return outs[i].at[chip_idx, pl.ds(who * r2, r2)]

        local = [pltpu.make_async_copy(ins[i], outs[i].at[k], local_sems.at[i]) for i in range(n)]
        for cp in local:
            cp.start()
        sends = []
        for i in range(n):
            r2 = ws[i].shape[0] // 2
            for r, (cx, cy) in enumerate(chips):
                sends.append(_remote(ins[i].at[pl.ds(c * r2, r2)], piece(i, k, c), send_sems.at[6 * i + r],
                                     recv_sems.at[6 * i + r], (cx, cy, c)))
                sends[-1].start()
        for i in range(n):
            for r, (cx, cy) in enumerate(chips):
                got = piece(i, 2 * cx + cy, c)
                _remote(got, got, send_sems.at[6 * i + r], recv_sems.at[6 * i + r], (cx, cy, c)).wait_recv()
                sends.append(_remote(got, got, send_sems.at[6 * i + 3 + r], recv_sems.at[6 * i + 3 + r], sibling))
                sends[-1].start()
        for i in range(n):
            for r, (cx, cy) in enumerate(chips):
                got = piece(i, 2 * cx + cy, 1 - c)
                _remote(got, got, send_sems.at[6 * i + 3 + r], recv_sems.at[6 * i + 3 + r], sibling).wait_recv()
        for cp in sends:
            cp.wait_send()
        for cp in local:
            cp.wait()

    return pl.pallas_call(
        body, name=name,
        out_shape=[jax.ShapeDtypeStruct((N_CHIP,) + w.shape, w.dtype) for w in ws],
        in_specs=[ANY] * n, out_specs=[ANY] * n,
        scratch_shapes=[pltpu.SemaphoreType.DMA((6 * n,)), pltpu.SemaphoreType.DMA((6 * n,)), pltpu.SemaphoreType.DMA((n,))],
    )(*ws)


def _send_other_halves(gs, *, name):
    n = len(gs)

    def body(*refs):
        ins, outs = refs[:n], refs[n:2 * n]
        send_sems, recv_sems = refs[2 * n:]
        x, y, c, _ = _place()
        sibling = (x, y, 1 - c)
        cps = []
        for i in range(n):
            r2 = gs[i].shape[1] // 2
            cps.append(_remote(ins[i].at[:, pl.ds((1 - c) * r2, r2)], outs[i], send_sems.at[i], recv_sems.at[i], sibling))
            cps[-1].start()
        for cp in cps:
            cp.wait()

    return pl.pallas_call(
        body, name=name,
        out_shape=[jax.ShapeDtypeStruct((N_CHIP, g.shape[1] // 2, g.shape[2]), g.dtype) for g in gs],
        in_specs=[ANY] * n, out_specs=[ANY] * n,
        scratch_shapes=[pltpu.SemaphoreType.DMA((n,)), pltpu.SemaphoreType.DMA((n,))],
    )(*gs)


def _scatter_to_chips(ps, *, name):
    n = len(ps)

    def body(*refs):
        ins, outs = refs[:n], refs[n:2 * n]
        send_sems, recv_sems, local_sems = refs[2 * n:]
        x, y, c, chips = _place()
        k = 2 * x + y
        local = [pltpu.make_async_copy(ins[i].at[k], outs[i].at[k], local_sems.at[i]) for i in range(n)]
        for cp in local:
            cp.start()
        cps = []
        for i in range(n):
            for r, (cx, cy) in enumerate(chips):
                cps.append(_remote(ins[i].at[2 * cx + cy], outs[i].at[k], send_sems.at[3 * i + r], recv_sems.at[3 * i + r],
                                   (cx, cy, c)))
                cps[-1].start()
        for i in range(n):
            for r, (cx, cy) in enumerate(chips):
                got = outs[i].at[2 * cx + cy]
                _remote(got, got, send_sems.at[3 * i + r], recv_sems.at[3 * i + r], (cx, cy, c)).wait_recv()
        for cp in cps:
            cp.wait_send()
        for cp in local:
            cp.wait()

    return pl.pallas_call(
        body, name=name,
        out_shape=[jax.ShapeDtypeStruct(p.shape, p.dtype) for p in ps],
        in_specs=[ANY] * n, out_specs=[ANY] * n,
        scratch_shapes=[pltpu.SemaphoreType.DMA((3 * n,)), pltpu.SemaphoreType.DMA((3 * n,)), pltpu.SemaphoreType.DMA((n,))],
    )(*ps)


def _share_halves(ts, *, name):
    n = len(ts)

    def body(*refs):
        ins, outs = refs[:n], refs[n:2 * n]
        send_sems, recv_sems, local_sems = refs[2 * n:]
        x, y, c, _ = _place()
        sibling = (x, y, 1 - c)
        cps, local = [], []
        for i in range(n):
            r2 = ts[i].shape[0]
            mine = outs[i].at[pl.ds(c * r2, r2)]
            local.append(pltpu.make_async_copy(ins[i], mine, local_sems.at[i]))
            local[-1].start()
            cps.append(_remote(ins[i], mine, send_sems.at[i], recv_sems.at[i], sibling))
            cps[-1].start()
        for i in range(n):
            r2 = ts[i].shape[0]
            got = outs[i].at[pl.ds((1 - c) * r2, r2)]
            _remote(got, got, send_sems.at[i], recv_sems.at[i], sibling).wait_recv()
        for cp in cps:
            cp.wait_send()
        for cp in local:
            cp.wait()

    return pl.pallas_call(
        body, name=name,
        out_shape=[jax.ShapeDtypeStruct((2 * t.shape[0], t.shape[1]), t.dtype) for t in ts],
        in_specs=[ANY] * n, out_specs=[ANY] * n,
        scratch_shapes=[pltpu.SemaphoreType.DMA((n,)), pltpu.SemaphoreType.DMA((n,)), pltpu.SemaphoreType.DMA((n,))],
    )(*ts)


def _add_sibling(g, recv, c_idx, *, name):
    _, R, C = g.shape
    r2 = R // 2
    tr = _tile(r2, max(16, (1 << 19) // C), 16)
    nr = r2 // tr

    def body(c_ref, g_ref, r_ref, o_ref):
        o_ref[...] = (g_ref[...].astype(F32) + r_ref[...].astype(F32)).astype(BF16)

    return pl.pallas_call(
        body, name=name, out_shape=jax.ShapeDtypeStruct((N_CHIP, r2, C), BF16),
        grid_spec=pltpu.PrefetchScalarGridSpec(
            num_scalar_prefetch=1, grid=(N_CHIP, nr),
            in_specs=[pl.BlockSpec((None, tr, C), lambda j, i, c_ref: (j, c_ref[0] * nr + i, 0)),
                      pl.BlockSpec((None, tr, C), lambda j, i, c_ref: (j, i, 0))],
            out_specs=pl.BlockSpec((None, tr, C), lambda j, i, c_ref: (j, i, 0))),
        compiler_params=_params("parallel", "parallel"),
    )(c_idx, g, recv)


def _sum_chips(p, *, name):
    _, r2, C = p.shape
    tr = _tile(r2, max(16, (1 << 18) // C), 16)

    def body(p_ref, o_ref):
        acc = p_ref[0].astype(F32)
        for j in range(1, N_CHIP):
            acc = acc + p_ref[j].astype(F32)
        o_ref[...] = acc

    return pl.pallas_call(
        body, name=name, out_shape=jax.ShapeDtypeStruct((r2, C), F32), grid=(r2 // tr,),
        in_specs=[pl.BlockSpec((N_CHIP, tr, C), lambda i: (0, i, 0))], out_specs=pl.BlockSpec((tr, C), lambda i: (i, 0)),
        compiler_params=_params("parallel"),
    )(p)


def _sum_devices(v, *, name):
    n, R, C = v.shape

    def body(v_ref, o_ref):
        acc = v_ref[0]
        for j in range(1, n):
            acc = acc + v_ref[j]
        o_ref[...] = acc

    return pl.pallas_call(body, name=name, out_shape=jax.ShapeDtypeStruct((R, C), F32),
                          in_specs=[VMEM_FULL], out_specs=VMEM_FULL)(v)


def _pair(a, b):
    R, n = a.shape
    return jnp.stack([a.reshape(R, n // PAIR, PAIR), b.reshape(R, n // PAIR, PAIR)], axis=2).reshape(R, 2 * n)


def _unpair(p):
    R, n2 = p.shape
    q = p.reshape(R, n2 // (2 * PAIR), 2, PAIR)
    return q[:, :, 0].reshape(R, n2 // 2), q[:, :, 1].reshape(R, n2 // 2)


def _from_col_shards(g):
    return jnp.transpose(g, (1, 0, 2)).reshape(g.shape[1], N_CHIP * g.shape[2])


def _to_col_shards(w):
    R, N = w.shape
    return jnp.transpose(w.reshape(R, N_CHIP, N // N_CHIP), (1, 0, 2))


def _split_heads(w, widths):
    R, N = w.shape
    per = sum(widths)
    w3 = w.reshape(R, N // per, per)
    lo = w3[:, :, :widths[0]].reshape(R, -1)
    hi = w3[:, :, widths[0]:].reshape(R, -1)
    return jnp.concatenate([lo, hi], axis=1)


def _merge_heads(w, widths):
    R, N = w.shape
    H = N // sum(widths)
    lo = w[:, :H * widths[0]].reshape(R, H, widths[0])
    hi = w[:, H * widths[0]:].reshape(R, H, widths[1])
    return jnp.concatenate([lo, hi], axis=2).reshape(R, N)


def _t5_bucket(dist):
    max_exact = REL_BUCKETS // 2
    n = jnp.maximum(dist, 0)
    large = max_exact + (jnp.log(jnp.maximum(n, 1).astype(F32) / max_exact)
                         / jnp.log(jnp.asarray(REL_MAX_DIST / max_exact, F32))
                         * (REL_BUCKETS - max_exact)).astype(jnp.int32)
    large = jnp.minimum(large, REL_BUCKETS - 1)
    return jnp.where(n < max_exact, n, large)


def _rel_tables():
    a = jnp.arange(SWA_BLOCK)
    b = jnp.arange(2 * SWA_BLOCK)
    dist = SWA_BLOCK + a[:, None] - b[None, :]
    valid = jnp.logical_and(dist >= 0, dist < SWA_BLOCK)
    onehot = jnp.logical_and(_t5_bucket(dist)[..., None] == jnp.arange(REL_BUCKETS), valid[..., None])
    onehot = onehot.astype(F32).reshape(2 * SWA_BLOCK * SWA_BLOCK, REL_BUCKETS)
    negmask = jnp.where(valid, 0.0, NEG).astype(F32).reshape(1, -1)
    return onehot, negmask


def _rope_tables(S):
    pos = jnp.arange(S, dtype=F32)
    inv = ROPE_THETA ** (-jnp.arange(0, MLA_ROPE, 2, dtype=F32) / MLA_ROPE)
    ang = pos[:, None] * inv[None, :]
    ang = jnp.concatenate([ang, ang, ang, ang], axis=-1)
    return jnp.cos(ang), jnp.sin(ang)


def _flat_pad(parts, rows=8):
    flat = jnp.concatenate([p.reshape(1, -1) for p in parts], axis=1)
    n = flat.shape[1]
    width = -(-n // (rows * LANES)) * LANES
    return jnp.pad(flat, ((0, 0), (0, rows * width - n))).reshape(rows, width)


def _unflat(vec, shapes):
    flat = vec.reshape(-1)
    out, off = [], 0
    for s in shapes:
        n = 1
        for d in s:
            n *= d
        out.append(flat[off:off + n].reshape(s))
        off += n
    return out


def kernel(x, c, w_ada, b_ada, g_pre_mix, g_post_mix, w_in, g_q_lat, w_uq, g_kv_lat, w_ukv, rel_bias, sinks, w_o, g_pre_ffn, g_post_ffn, w_up, conv_w, conv_b, w_down, loss_target, m_w_ada, m_b_ada, m_g_pre_mix, m_g_post_mix, m_w_in, m_g_q_lat, m_w_uq, m_g_kv_lat, m_w_ukv, m_rel_bias, m_sinks, m_w_o, m_g_pre_ffn, m_g_post_ffn, m_w_up, m_conv_w, m_conv_b, m_w_down, v_w_ada, v_b_ada, v_g_pre_mix, v_g_post_mix, v_w_in, v_g_q_lat, v_w_uq, v_g_kv_lat, v_w_ukv, v_rel_bias, v_sinks, v_w_o, v_g_pre_ffn, v_g_post_ffn, v_w_up, v_conv_w, v_conv_b, v_w_down):
    S, D = x.shape[1], x.shape[2]
    Rq, Rkv = g_q_lat.shape[1], g_kv_lat.shape[1]
    H = D // MLA_V
    NH = D // SWA_HD
    KW = SWA_KVH * SWA_HD
    F = w_down.shape[1] * N_CHIP
    xi, yi, ci = lax.axis_index("x"), lax.axis_index("y"), lax.axis_index("c")
    chip = 2 * xi + yi
    me = 2 * chip + ci
    x2, tgt = x[0], loss_target[0]

    c_all = _allgather8(jnp.broadcast_to(c, (8, D)), name="gather_c")[:, 0, :]
    n3 = w_ada.shape[2]
    mod_part = _ada_fwd(c_all, w_ada[0], lax.dynamic_slice(b_ada, (0, chip * n3), (1, n3)), name="ada_fwd")
    mod_all = _allgather8(mod_part, name="gather_mod")
    mod_me = lax.dynamic_index_in_dim(mod_all[0::2], me, axis=1, keepdims=False).reshape(1, 6 * D)
    sh1, sc1, gt1, sh2, sc2, gt2 = [mod_me[:, k * D:(k + 1) * D] for k in range(6)]

    big = [w_in[0], w_uq[0], w_ukv[0], w_o[0], w_up[0], w_down[0]]
    gin, guq, gukv, go, gup, gdown = _gather_weights([w.astype(BF16) for w in big], name="gather_weights")
    win = _from_col_shards(gin)
    o_kr = Rq + Rkv
    o_q = o_kr + MLA_ROPE
    o_g = o_q + NH * SWA_HD + 2 * KW
    w_lat = jnp.concatenate([win[:, :o_q], win[:, o_kr:o_q]], axis=1)
    w_swa = win[:, o_q:o_g]
    w_gate = _pair(win[:, o_g:o_g + D], win[:, o_g + D:])
    w_in_all = jnp.concatenate([w_lat, w_swa, w_gate], axis=1)
    n_lat, n_swa = w_lat.shape[1], w_swa.shape[1]
    wuq = _split_heads(_from_col_shards(guq), (MLA_NOPE, MLA_ROPE))
    wukv = _split_heads(_from_col_shards(gukv), (MLA_NOPE, MLA_V))
    wo = go.reshape(D, D)
    wup_full = _from_col_shards(gup)
    wup = _pair(wup_full[:, :F], wup_full[:, F:])
    wdown = gdown.reshape(F, D)
    cw_all = _allgather8(jnp.pad(conv_w[0], ((0, 5), (0, 0))), name="gather_conv_w")[0::2, :3]
    cw_full = _from_col_shards(cw_all)
    cw = _pair(cw_full[:, :F], cw_full[:, F:])
    cb = _pair(conv_b[:, :F], conv_b[:, F:])
    onehot, negmask = _rel_tables()
    bias_m = (_matmul(rel_bias.T, onehot.T, name="rel_bias_table") + negmask).reshape(NH, SWA_BLOCK, 2 * SWA_BLOCK)
    cos, sin = _rope_tables(S)
    sink_v = sinks.reshape(NH)

    h = _modnorm_fwd(x2, g_pre_mix, sc1, sh1, name="pre_mix_norm")
    z_lat = _matmul(h, w_lat, name="in_proj_lat")
    z_swa = _matmul(h, w_swa, name="in_proj_swa")
    zg = _matmul(h, w_gate, name="in_proj_gate")
    nq, nkv = _lat_norm_fwd(z_lat, g_q_lat, g_kv_lat, name="lat_norm")
    q_raw = _matmul(nq, wuq, name="uq_proj")
    kv_raw = _matmul(nkv, wukv, name="ukv_proj")
    qp, kp, vv = _mla_pack_fwd(q_raw, kv_raw, z_lat, cos, sin, o_kr, name="mla_pack")
    o_a, lse = _flash_fwd(qp, kp, vv, name="mla_attn")
    o_b = _swa_fwd(z_swa, bias_m, sink_v, name="swa_attn")
    mixin = _gate_fwd(zg, o_a, o_b, name="gate_mix")
    mix = _matmul(mixin, wo, name="o_proj")
    x1 = _resnorm_fwd(x2, mix, g_post_mix, gt1, name="post_mix_norm")
    h2 = _modnorm_fwd(x1, g_pre_ffn, sc2, sh2, name="pre_ffn_norm")
    t = _matmul(h2, wup, name="up_proj")
    a = _conv_gate_fwd(t, cw, cb, name="conv_gate")
    yv = _matmul(a, wdown, name="down_proj")
    dout, loss_tile = _resnorm_loss(x1, yv, g_post_ffn, gt2, tgt, name="post_ffn_norm_loss")

    dy, dg_post_ffn, dgt2 = _resnorm_bwd(dout, yv, g_post_ffn, gt2, name="post_ffn_norm_bwd")
    da = _matmul(dy, wdown, tb=True, name="down_proj_dx")
    dw_down = _matmul(a, dy, ta=True, out_dtype=BF16, name="down_proj_dw")
    du, dcw_p, dcb_p = _conv_gate_bwd(t, da, cw, cb, name="conv_gate_bwd")
    dt = _conv_bwd_dt(du, cw, name="conv_bwd_dt")
    dh2 = _matmul(dt, wup, tb=True, name="up_proj_dx")
    dw_up_p = _matmul(h2, dt, ta=True, out_dtype=BF16, name="up_proj_dw")
    dx1, dg_pre_ffn, dsc2, dsh2 = _modnorm_bwd(dh2, x1, g_pre_ffn, sc2, dout, name="pre_ffn_norm_bwd")
    dmix, dg_post_mix, dgt1 = _resnorm_bwd(dx1, mix, g_post_mix, gt1, name="post_mix_norm_bwd")
    dmixin = _matmul(dmix, wo, tb=True, name="o_proj_dx")
    dw_o = _matmul(mixin, dmix, ta=True, out_dtype=BF16, name="o_proj_dw")
    do_a, do_b, dzg = _gate_bwd(dmixin, zg, o_a, o_b, name="gate_mix_bwd")
    dkp, dvv = _flash_bwd_dkv(qp, kp, vv, o_a, do_a, lse, name="mla_attn_dkv")
    dqp = _flash_bwd_dq(qp, kp, vv, o_a, do_a, lse, name="mla_attn_dq")
    dq_raw, dkv_raw, dkr = _mla_pack_bwd(dqp, dkp, dvv, cos, sin, name="mla_pack_bwd")
    dnq = _matmul(dq_raw, wuq, tb=True, name="uq_proj_dx")
    dw_uq_p = _matmul(nq, dq_raw, ta=True, out_dtype=BF16, name="uq_proj_dw")
    dnkv = _matmul(dkv_raw, wukv, tb=True, name="ukv_proj_dx")
    dw_ukv_p = _matmul(nkv, dkv_raw, ta=True, out_dtype=BF16, name="ukv_proj_dw")
    dz_lat, dg_q, dg_kv = _lat_norm_bwd(z_lat, dnq, dnkv, dkr, g_q_lat, g_kv_lat, name="lat_norm_bwd")
    dz_swa, dbias, dsink = _swa_bwd(z_swa, bias_m, sink_v, o_b, do_b, name="swa_attn_bwd")
    dz = jnp.concatenate([dz_lat, dz_swa, dzg], axis=1)
    dh = _matmul(dz, w_in_all, tb=True, name="in_proj_dx")
    dw_in_p = _matmul(h, dz, ta=True, out_dtype=BF16, name="in_proj_dw")
    grad_x, dg_pre_mix, dsc1, dsh1 = _modnorm_bwd(dh, x2, g_pre_mix, sc1, dx1, name="pre_mix_norm_bwd")
    drel = _matmul(dbias.reshape(NH, -1), onehot, name="rel_bias_bwd").T

    dga, dgb = _unpair(dw_in_p[:, n_lat + n_swa:])
    dw_in = jnp.concatenate([dw_in_p[:, :o_q], dw_in_p[:, n_lat:n_lat + n_swa], dga, dgb], axis=1)
    dw_uq = _merge_heads(dw_uq_p, (MLA_NOPE, MLA_ROPE))
    dw_ukv = _merge_heads(dw_ukv_p, (MLA_NOPE, MLA_V))
    dw_up = jnp.concatenate(_unpair(dw_up_p), axis=1)
    dcw = jnp.concatenate(_unpair(dcw_p), axis=1)
    dcb = jnp.concatenate(_unpair(dcb_p), axis=1)
    dmod = jnp.concatenate([dsh1, dsc1, dgt1, dsh2, dsc2, dgt2], axis=1)
    small = [dmod, dg_pre_mix, dg_post_mix, dg_pre_ffn, dg_post_ffn, dg_q, dg_kv, drel, dsink[:, :NH], dcb, dcw]
    shapes = [p.shape for p in small]
    small_all = _allgather8(_flat_pad(small), name="gather_small_grads")
    tot = _unflat(_sum_devices(small_all, name="sum_small_grads"), shapes)
    g_b_ada, g_pre_mix_g, g_post_mix_g, g_pre_ffn_g, g_post_ffn_g, g_q_g, g_kv_g, g_rel, g_sinks, g_cb, g_cw_full = tot
    dmod_all = small_all.reshape(N_DEV, -1)[:, :6 * D]
    g_w_ada = _ada_bwd(c_all.T, lax.dynamic_slice(dmod_all, (0, chip * n3), (N_DEV, n3)), name="ada_bwd")
    ncw = conv_w.shape[2]
    g_cw = lax.dynamic_slice(g_cw_full, (0, chip * ncw), (3, ncw))

    gs = [_to_col_shards(dw_in), _to_col_shards(dw_uq), _to_col_shards(dw_ukv), dw_o.reshape(N_CHIP, D // N_CHIP, D),
          _to_col_shards(dw_up), dw_down.reshape(N_CHIP, F // N_CHIP, D)]
    names = ["w_in", "w_uq", "w_ukv", "w_o", "w_up", "w_down"]
    recv = _send_other_halves(gs, name="grads_to_sibling")
    c_idx = jnp.reshape(ci, (1,)).astype(jnp.int32)
    part = [_add_sibling(g, r, c_idx, name="grad_add_sibling_" + nm) for g, r, nm in zip(gs, recv, names)]
    got = _scatter_to_chips(part, name="grads_to_chips")
    halves = [_sum_chips(p, name="grad_sum_chips_" + nm) for p, nm in zip(got, names)]
    g_in, g_uq, g_ukv, g_o, g_up, g_down = _share_halves(halves, name="grads_share_halves")

    res = {}
    bigs = dict(w_ada=(w_ada[0], g_w_ada, m_w_ada[0], v_w_ada[0]), w_in=(w_in[0], g_in, m_w_in[0], v_w_in[0]),
                w_uq=(w_uq[0], g_uq, m_w_uq[0], v_w_uq[0]), w_ukv=(w_ukv[0], g_ukv, m_w_ukv[0], v_w_ukv[0]),
                w_o=(w_o[0], g_o, m_w_o[0], v_w_o[0]), w_up=(w_up[0], g_up, m_w_up[0], v_w_up[0]),
                w_down=(w_down[0], g_down, m_w_down[0], v_w_down[0]))
    for nm, (w, g, m, v) in bigs.items():
        res[nm] = (g,) + tuple(_adamw(w, g, m, v, name="adamw_" + nm))
    snames = ["b_ada", "g_pre_mix", "g_post_mix", "g_pre_ffn", "g_post_ffn", "g_q_lat", "g_kv_lat", "rel_bias", "sinks",
              "conv_b", "conv_w"]
    sw = [b_ada, g_pre_mix, g_post_mix, g_pre_ffn, g_post_ffn, g_q_lat, g_kv_lat, rel_bias, sinks, conv_b, conv_w]
    sm = [m_b_ada, m_g_pre_mix, m_g_post_mix, m_g_pre_ffn, m_g_post_ffn, m_g_q_lat, m_g_kv_lat, m_rel_bias, m_sinks,
          m_conv_b, m_conv_w]
    sv = [v_b_ada, v_g_pre_mix, v_g_post_mix, v_g_pre_ffn, v_g_post_ffn, v_g_q_lat, v_g_kv_lat, v_rel_bias, v_sinks,
          v_conv_b, v_conv_w]
    sg = [g_b_ada, g_pre_mix_g, g_post_mix_g, g_pre_ffn_g, g_post_ffn_g, g_q_g, g_kv_g, g_rel, g_sinks, g_cb, g_cw]
    sshapes = [w.shape for w in sw]
    sd, snm, snv = _adamw(_flat_pad(sw), _flat_pad(sg), _flat_pad(sm), _flat_pad(sv), name="adamw_small")
    sd, snm, snv = _unflat(sd, sshapes), _unflat(snm, sshapes), _unflat(snv, sshapes)
    for k, nm in enumerate(snames):
        res[nm] = (sg[k].reshape(sshapes[k]), sd[k], snm[k], snv[k])

    order = ["w_ada", "b_ada", "g_pre_mix", "g_post_mix", "w_in", "g_q_lat", "w_uq", "g_kv_lat", "w_ukv", "rel_bias", "sinks",
             "w_o", "g_pre_ffn", "g_post_ffn", "w_up", "conv_w", "conv_b", "w_down"]
    ref_shapes = dict(w_ada=w_ada.shape, w_in=w_in.shape, w_uq=w_uq.shape, w_ukv=w_ukv.shape, w_o=w_o.shape,
                      w_up=w_up.shape, w_down=w_down.shape)
    outs = []
    for k in range(4):
        for nm in order:
            arr = res[nm][k]
            outs.append(arr.reshape(ref_shapes[nm]) if nm in ref_shapes else arr)
    loss = lax.psum(loss_tile[0, 0], ("x", "y", "c"))
    return (loss, grad_x[None], *outs)
```

```python
import functools

import jax
import jax.numpy as jnp
from jax import lax
from jax.experimental import pallas as pl
from jax.experimental.pallas import tpu as pltpu

F32 = jnp.float32
BF16 = jnp.bfloat16
MESH = pl.DeviceIdType.MESH
HIGHEST = lax.Precision.HIGHEST

N_DEV = 8
N_CHIP = 4
LANES = 128
MLA_NOPE = 128
MLA_ROPE = 64
MLA_V = 128
MLA_QK = MLA_NOPE + MLA_ROPE
MLA_QK_PAD = 256
ROPE_THETA = 10000.0
SWA_HD = 64
SWA_KVH = 4
SWA_BLOCK = 128
REL_BUCKETS = 32
REL_MAX_DIST = 128
PAIR = 512
EPS = 1e-6
NEG = -1e30
ADAM_LR = 0.001
ADAM_B1 = 0.9
ADAM_B2 = 0.999
ADAM_EPS = 1e-08
ADAM_WD = 0.01
ADAM_STEP = 10

ANY = pl.BlockSpec(memory_space=pl.ANY)
VMEM_FULL = pl.BlockSpec(memory_space=pltpu.VMEM)
SMEM_FULL = pl.BlockSpec(memory_space=pltpu.SMEM)


def _params(*sem):
    return pltpu.CompilerParams(dimension_semantics=sem if sem else None)


def _tile(n, pref, unit=LANES):
    best = None
    for t in range(unit, min(n, pref) + 1, unit):
        if n % t == 0:
            best = t
    return n if best is None else best


def _matmul(a, b, *, ta=False, tb=False, out_dtype=F32, name):
    if ta:
        K, M = a.shape
    else:
        M, K = a.shape
    if tb:
        N, K2 = b.shape
    else:
        K2, N = b.shape
    assert K == K2, (a.shape, b.shape, ta, tb)
    exact = a.dtype == F32
    tm = _tile(M, 512, 8) if M >= 8 else M
    tn = _tile(N, 1536)
    tk = _tile(K, 2048)
    nk = K // tk
    dn = (((0 if ta else 1,), (1 if tb else 0,)), ((), ()))

    def product(a_ref, b_ref):
        return lax.dot_general(a_ref[...], b_ref[...], dn, preferred_element_type=F32,
                               precision=HIGHEST if exact else None)

    def body_acc(a_ref, b_ref, o_ref, acc_ref):
        k = pl.program_id(2)

        @pl.when(k == 0)
        def _():
            acc_ref[...] = product(a_ref, b_ref)

        @pl.when(jnp.logical_and(k > 0, k < nk - 1))
        def _():
            acc_ref[...] += product(a_ref, b_ref)

        @pl.when(k == nk - 1)
        def _():
            o_ref[...] = (acc_ref[...] + product(a_ref, b_ref)).astype(o_ref.dtype)

    def body_one(a_ref, b_ref, o_ref):
        o_ref[...] = product(a_ref, b_ref).astype(o_ref.dtype)

    body = body_one if nk == 1 else body_acc
    a_spec = pl.BlockSpec((tk, tm), lambda i, j, k: (k, i)) if ta else pl.BlockSpec((tm, tk), lambda i, j, k: (i, k))
    b_spec = pl.BlockSpec((tn, tk), lambda i, j, k: (j, k)) if tb else pl.BlockSpec((tk, tn), lambda i, j, k: (k, j))
    return pl.pallas_call(
        body, name=name,
        out_shape=jax.ShapeDtypeStruct((M, N), out_dtype),
        grid=(M // tm, N // tn, nk),
        in_specs=[a_spec, b_spec],
        out_specs=pl.BlockSpec((tm, tn), lambda i, j, k: (i, j)),
        scratch_shapes=[] if nk == 1 else [pltpu.VMEM((tm, tn), F32)],
        compiler_params=_params("parallel", "parallel", "arbitrary"),
    )(a, b)


def _row_tile(S, width):
    return _tile(S, max(8, (1 << 19) // width), 8)


def _rstd(x):
    return lax.rsqrt(jnp.mean(x * x, axis=-1, keepdims=True) + EPS)


def _acc_rows(ref, val, first):
    s = jnp.sum(val, axis=0, keepdims=True)

    @pl.when(first)
    def _():
        ref[...] = s

    @pl.when(jnp.logical_not(first))
    def _():
        ref[...] += s


def _modnorm_fwd(x, g, sc, sh, *, name):
    S, D = x.shape
    tr = _row_tile(S, D)

    def body(x_ref, g_ref, sc_ref, sh_ref, h_ref):
        xv = x_ref[...]
        n = (xv * _rstd(xv)) * g_ref[...]
        h_ref[...] = (n * (1.0 + sc_ref[...]) + sh_ref[...]).astype(BF16)

    row = pl.BlockSpec((tr, D), lambda i: (i, 0))
    vec = pl.BlockSpec((1, D), lambda i: (0, 0))
    return pl.pallas_call(
        body, name=name, out_shape=jax.ShapeDtypeStruct((S, D), BF16), grid=(S // tr,),
        in_specs=[row, vec, vec, vec], out_specs=row, compiler_params=_params("parallel"),
    )(x, g, sc, sh)


def _modnorm_bwd(dh, x, g, sc, dres, *, name):
    S, D = x.shape
    tr = _row_tile(S, D)

    def body(dh_ref, x_ref, g_ref, sc_ref, dres_ref, dx_ref, dg_ref, dsc_ref, dsh_ref):
        first = pl.program_id(0) == 0
        xv = x_ref[...]
        dhv = dh_ref[...]
        gv = g_ref[...]
        r = _rstd(xv)
        xhat = xv * r
        _acc_rows(dsh_ref, dhv, first)
        _acc_rows(dsc_ref, dhv * (xhat * gv), first)
        dn = dhv * (1.0 + sc_ref[...])
        _acc_rows(dg_ref, dn * xhat, first)
        dxhat = dn * gv
        proj = jnp.mean(dxhat * xhat, axis=-1, keepdims=True)
        dx_ref[...] = r * (dxhat - xhat * proj) + dres_ref[...]

    row = pl.BlockSpec((tr, D), lambda i: (i, 0))
    vec = pl.BlockSpec((1, D), lambda i: (0, 0))
    vshape = jax.ShapeDtypeStruct((1, D), F32)
    return pl.pallas_call(
        body, name=name,
        out_shape=(jax.ShapeDtypeStruct((S, D), F32), vshape, vshape, vshape), grid=(S // tr,),
        in_specs=[row, row, vec, vec, row], out_specs=(row, vec, vec, vec),
        compiler_params=_params("arbitrary"),
    )(dh, x, g, sc, dres)


def _resnorm_fwd(xres, m, g, gt, *, name):
    S, D = xres.shape
    tr = _row_tile(S, D)

    def body(x_ref, m_ref, g_ref, gt_ref, o_ref):
        mv = m_ref[...]
        o_ref[...] = x_ref[...] + gt_ref[...] * ((mv * _rstd(mv)) * g_ref[...])

    row = pl.BlockSpec((tr, D), lambda i: (i, 0))
    vec = pl.BlockSpec((1, D), lambda i: (0, 0))
    return pl.pallas_call(
        body, name=name, out_shape=jax.ShapeDtypeStruct((S, D), F32), grid=(S // tr,),
        in_specs=[row, row, vec, vec], out_specs=row, compiler_params=_params("parallel"),
    )(xres, m, g, gt)


def _resnorm_loss(xres, m, g, gt, target, *, name):
    S, D = xres.shape
    tr = _row_tile(S, D)

    def body(x_ref, m_ref, g_ref, gt_ref, t_ref, d_ref, loss_ref):
        mv = m_ref[...]
        out = x_ref[...] + gt_ref[...] * ((mv * _rstd(mv)) * g_ref[...])
        err = out - t_ref[...]
        d_ref[...] = err * (1.0 / D)
        part = 0.5 * jnp.sum(jnp.mean(err * err, axis=-1, keepdims=True), axis=0, keepdims=True)
        part = jnp.broadcast_to(part, loss_ref.shape)

        @pl.when(pl.program_id(0) == 0)
        def _():
            loss_ref[...] = part

        @pl.when(pl.program_id(0) != 0)
        def _():
            loss_ref[...] += part

    row = pl.BlockSpec((tr, D), lambda i: (i, 0))
    vec = pl.BlockSpec((1, D), lambda i: (0, 0))
    return pl.pallas_call(
        body, name=name,
        out_shape=(jax.ShapeDtypeStruct((S, D), F32), jax.ShapeDtypeStruct((8, LANES), F32)), grid=(S // tr,),
        in_specs=[row, row, vec, vec, row], out_specs=(row, pl.BlockSpec((8, LANES), lambda i: (0, 0))),
        compiler_params=_params("arbitrary"),
    )(xres, m, g, gt, target)


def _resnorm_bwd(dout, m, g, gt, *, name):
    S, D = m.shape
    tr = _row_tile(S, D)

    def body(d_ref, m_ref, g_ref, gt_ref, dm_ref, dg_ref, dgt_ref):
        first = pl.program_id(0) == 0
        mv = m_ref[...]
        dv = d_ref[...]
        gv = g_ref[...]
        r = _rstd(mv)
        mhat = mv * r
        _acc_rows(dgt_ref, dv * (mhat * gv), first)
        dn = dv * gt_ref[...]
        _acc_rows(dg_ref, dn * mhat, first)
        dmhat = dn * gv
        proj = jnp.mean(dmhat * mhat, axis=-1, keepdims=True)
        dm_ref[...] = (r * (dmhat - mhat * proj)).astype(BF16)

    row = pl.BlockSpec((tr, D), lambda i: (i, 0))
    vec = pl.BlockSpec((1, D), lambda i: (0, 0))
    vshape = jax.ShapeDtypeStruct((1, D), F32)
    return pl.pallas_call(
        body, name=name, out_shape=(jax.ShapeDtypeStruct((S, D), BF16), vshape, vshape), grid=(S // tr,),
        in_specs=[row, row, vec, vec], out_specs=(row, vec, vec), compiler_params=_params("arbitrary"),
    )(dout, m, g, gt)


def _lat_norm_fwd(z_lat, g_q, g_kv, *, name):
    S, W = z_lat.shape
    Rq, Rkv = g_q.shape[1], g_kv.shape[1]
    tr = _row_tile(S, W)

    def body(z_ref, gq_ref, gkv_ref, nq_ref, nkv_ref):
        cq = z_ref[:, :Rq]
        ckv = z_ref[:, Rq:Rq + Rkv]
        nq_ref[...] = ((cq * _rstd(cq)) * gq_ref[...]).astype(BF16)
        nkv_ref[...] = ((ckv * _rstd(ckv)) * gkv_ref[...]).astype(BF16)

    return pl.pallas_call(
        body, name=name,
        out_shape=(jax.ShapeDtypeStruct((S, Rq), BF16), jax.ShapeDtypeStruct((S, Rkv), BF16)), grid=(S // tr,),
        in_specs=[pl.BlockSpec((tr, W), lambda i: (i, 0)), pl.BlockSpec((1, Rq), lambda i: (0, 0)),
                  pl.BlockSpec((1, Rkv), lambda i: (0, 0))],
        out_specs=(pl.BlockSpec((tr, Rq), lambda i: (i, 0)), pl.BlockSpec((tr, Rkv), lambda i: (i, 0))),
        compiler_params=_params("parallel"),
    )(z_lat, g_q, g_kv)


def _lat_norm_bwd(z_lat, dnq, dnkv, dkr, g_q, g_kv, *, name):
    S, W = z_lat.shape
    Rq, Rkv = g_q.shape[1], g_kv.shape[1]
    tr = _row_tile(S, W)

    def one(c, dn, gv):
        r = _rstd(c)
        chat = c * r
        dchat = dn * gv
        proj = jnp.mean(dchat * chat, axis=-1, keepdims=True)
        return r * (dchat - chat * proj), dn * chat

    def body(z_ref, dnq_ref, dnkv_ref, dkr_ref, gq_ref, gkv_ref, dz_ref, dgq_ref, dgkv_ref):
        first = pl.program_id(0) == 0
        dcq, pq = one(z_ref[:, :Rq], dnq_ref[...], gq_ref[...])
        dckv, pkv = one(z_ref[:, Rq:Rq + Rkv], dnkv_ref[...], gkv_ref[...])
        _acc_rows(dgq_ref, pq, first)
        _acc_rows(dgkv_ref, pkv, first)
        dz_ref[:, :Rq] = dcq.astype(BF16)
        dz_ref[:, Rq:Rq + Rkv] = dckv.astype(BF16)
        dz_ref[:, Rq + Rkv:] = dkr_ref[...].astype(BF16)

    return pl.pallas_call(
        body, name=name,
        out_shape=(jax.ShapeDtypeStruct((S, W), BF16), jax.ShapeDtypeStruct((1, Rq), F32),
                   jax.ShapeDtypeStruct((1, Rkv), F32)), grid=(S // tr,),
        in_specs=[pl.BlockSpec((tr, W), lambda i: (i, 0)), pl.BlockSpec((tr, Rq), lambda i: (i, 0)),
                  pl.BlockSpec((tr, Rkv), lambda i: (i, 0)), pl.BlockSpec((tr, LANES), lambda i: (i, 0)),
                  pl.BlockSpec((1, Rq), lambda i: (0, 0)), pl.BlockSpec((1, Rkv), lambda i: (0, 0))],
        out_specs=(pl.BlockSpec((tr, W), lambda i: (i, 0)), pl.BlockSpec((1, Rq), lambda i: (0, 0)),
                   pl.BlockSpec((1, Rkv), lambda i: (0, 0))),
        compiler_params=_params("arbitrary"),
    )(z_lat, dnq, dnkv, dkr, g_q, g_kv)


def _rot(x, lo32):
    a = pltpu.roll(x, 32, 1)
    b = pltpu.roll(x, LANES - 32, 1)
    return jnp.where(lo32, -b, a)


def _rot_t(g, lo32):
    a = pltpu.roll(g, 32, 1)
    b = pltpu.roll(g, LANES - 32, 1)
    return jnp.where(lo32, b, -a)


def _mla_pack_fwd(q_raw, kv_raw, z_lat, cos, sin, kr_off, *, name):
    S = q_raw.shape[0]
    H = kv_raw.shape[1] // (MLA_NOPE + MLA_V)
    W = z_lat.shape[1]
    scale = MLA_QK ** -0.5
    tr = min(S, 128)
    nope_w = H * MLA_NOPE

    def body(q_ref, kv_ref, z_ref, cos_ref, sin_ref, qp_ref, kp_ref, v_ref):
        lane = lax.broadcasted_iota(jnp.int32, (tr, LANES), 1)
        lo32 = (lane % 64) < 32
        lo64 = lane < 64
        c = cos_ref[...]
        s = sin_ref[...]
        kr = z_ref[:, kr_off:kr_off + LANES]
        kr = (kr * c + _rot(kr, lo32) * s).astype(BF16)
        for hp in range(H // 2):
            xb = q_ref[:, nope_w + hp * LANES:nope_w + (hp + 1) * LANES]
            rb = (xb * c + _rot(xb, lo32) * s) * scale
            for e in range(2):
                h = 2 * hp + e
                base = h * MLA_QK_PAD
                qp_ref[:, base:base + LANES] = (q_ref[:, h * LANES:(h + 1) * LANES] * scale).astype(BF16)
                keep = lo64 if e == 0 else jnp.logical_not(lo64)
                qp_ref[:, base + LANES:base + 2 * LANES] = jnp.where(keep, rb, 0.0).astype(BF16)
                kp_ref[:, base:base + LANES] = kv_ref[:, h * LANES:(h + 1) * LANES].astype(BF16)
                kp_ref[:, base + LANES:base + 2 * LANES] = kr
        v_ref[...] = kv_ref[:, nope_w:].astype(BF16)

    return pl.pallas_call(
        body, name=name,
        out_shape=(jax.ShapeDtypeStruct((S, H * MLA_QK_PAD), BF16), jax.ShapeDtypeStruct((S, H * MLA_QK_PAD), BF16),
                   jax.ShapeDtypeStruct((S, H * MLA_V), BF16)), grid=(S // tr,),
        in_specs=[pl.BlockSpec((tr, q_raw.shape[1]), lambda i: (i, 0)), pl.BlockSpec((tr, kv_raw.shape[1]), lambda i: (i, 0)),
                  pl.BlockSpec((tr, W), lambda i: (i, 0)), pl.BlockSpec((tr, LANES), lambda i: (i, 0)),
                  pl.BlockSpec((tr, LANES), lambda i: (i, 0))],
        out_specs=(pl.BlockSpec((tr, H * MLA_QK_PAD), lambda i: (i, 0)), pl.BlockSpec((tr, H * MLA_QK_PAD), lambda i: (i, 0)),
                   pl.BlockSpec((tr, H * MLA_V), lambda i: (i, 0))),
        compiler_params=_params("parallel"),
    )(q_raw, kv_raw, z_lat, cos, sin)


def _mla_pack_bwd(dqp, dkp, dv, cos, sin, *, name):
    S = dqp.shape[0]
    H = dv.shape[1] // MLA_V
    scale = MLA_QK ** -0.5
    tr = min(S, 128)
    nope_w = H * MLA_NOPE

    def body(dqp_ref, dkp_ref, dv_ref, cos_ref, sin_ref, dq_ref, dkv_ref, dkr_ref):
        lane = lax.broadcasted_iota(jnp.int32, (tr, LANES), 1)
        lo32 = (lane % 64) < 32
        lo64 = lane < 64
        c = cos_ref[...]
        s = sin_ref[...]
        dkr2 = jnp.zeros((tr, LANES), F32)
        for hp in range(H // 2):
            be = (2 * hp) * MLA_QK_PAD
            bo = (2 * hp + 1) * MLA_QK_PAD
            g = jnp.where(lo64, dqp_ref[:, be + LANES:be + 2 * LANES], dqp_ref[:, bo + LANES:bo + 2 * LANES]) * scale
            dq_ref[:, nope_w + hp * LANES:nope_w + (hp + 1) * LANES] = (g * c + _rot_t(g * s, lo32)).astype(BF16)
            for h, base in ((2 * hp, be), (2 * hp + 1, bo)):
                dq_ref[:, h * LANES:(h + 1) * LANES] = (dqp_ref[:, base:base + LANES] * scale).astype(BF16)
                dkv_ref[:, h * LANES:(h + 1) * LANES] = dkp_ref[:, base:base + LANES].astype(BF16)
                dkr2 = dkr2 + dkp_ref[:, base + LANES:base + 2 * LANES]
        dkr2 = dkr2 * c + _rot_t(dkr2 * s, lo32)
        dkr2 = dkr2 + pltpu.roll(dkr2, 64, 1)
        dkr_ref[...] = jnp.where(lo64, dkr2, 0.0)
        dkv_ref[:, nope_w:] = dv_ref[...].astype(BF16)

    return pl.pallas_call(
        body, name=name,
        out_shape=(jax.ShapeDtypeStruct((S, nope_w + H * MLA_ROPE), BF16), jax.ShapeDtypeStruct((S, 2 * nope_w), BF16),
                   jax.ShapeDtypeStruct((S, LANES), F32)), grid=(S // tr,),
        in_specs=[pl.BlockSpec((tr, H * MLA_QK_PAD), lambda i: (i, 0)), pl.BlockSpec((tr, H * MLA_QK_PAD), lambda i: (i, 0)),
                  pl.BlockSpec((tr, H * MLA_V), lambda i: (i, 0)), pl.BlockSpec((tr, LANES), lambda i: (i, 0)),
                  pl.BlockSpec((tr, LANES), lambda i: (i, 0))],
        out_specs=(pl.BlockSpec((tr, nope_w + H * MLA_ROPE), lambda i: (i, 0)), pl.BlockSpec((tr, 2 * nope_w), lambda i: (i, 0)),
                   pl.BlockSpec((tr, LANES), lambda i: (i, 0))),
        compiler_params=_params("parallel"),
    )(dqp, dkp, dv, cos, sin)


FLASH_HB = 2


def _causal_pairs(nb):
    qi = [i for i in range(nb) for j in range(i + 1)]
    kj = [j for i in range(nb) for j in range(i + 1)]
    return jnp.asarray(qi, jnp.int32), jnp.asarray(kj, jnp.int32)


def _scores(q, k, diagonal, t):
    s = lax.dot_general(q, k, (((1,), (1,)), ((), ())), preferred_element_type=F32)
    if diagonal:
        row = lax.broadcasted_iota(jnp.int32, (t, t), 0)
        col = lax.broadcasted_iota(jnp.int32, (t, t), 1)
        s = jnp.where(col <= row, s, NEG)
    return s


def _flash_fwd(qp, kp, v, *, name):
    S = qp.shape[0]
    H = v.shape[1] // MLA_V
    t = min(S, 512)
    nb = S // t
    HB = FLASH_HB
    qi, kj = _causal_pairs(nb)
    QW, VW = MLA_QK_PAD, MLA_V

    def body(qi_ref, kj_ref, q_ref, k_ref, v_ref, o_ref, lse_ref, m_s, l_s, acc_s):
        pr = pl.program_id(1)
        i = qi_ref[pr]
        j = kj_ref[pr]

        @pl.when(j == 0)
        def _():
            m_s[...] = jnp.full_like(m_s, NEG)
            l_s[...] = jnp.zeros_like(l_s)
            acc_s[...] = jnp.zeros_like(acc_s)

        def step(diagonal):
            for hh in range(HB):
                s = _scores(q_ref[:, hh * QW:(hh + 1) * QW], k_ref[:, hh * QW:(hh + 1) * QW], diagonal, t)
                m_prev = m_s[hh]
                m_cur = jnp.maximum(m_prev, jnp.max(s, axis=1, keepdims=True))
                alpha = jnp.exp(m_prev - m_cur)
                p = jnp.exp(s - m_cur[:, :1])
                l_new = alpha * l_s[hh] + jnp.sum(p, axis=1, keepdims=True)
                acc = alpha * acc_s[hh] + jnp.dot(p.astype(BF16), v_ref[:, hh * VW:(hh + 1) * VW], preferred_element_type=F32)
                if diagonal:
                    o_ref[:, hh * VW:(hh + 1) * VW] = acc / l_new
                    lse_ref[hh] = m_cur + jnp.log(l_new)
                else:
                    l_s[hh] = l_new
                    acc_s[hh] = acc
                    m_s[hh] = m_cur

        @pl.when(i != j)
        def _():
            step(False)

        @pl.when(i == j)
        def _():
            step(True)

    return pl.pallas_call(
        body, name=name,
        out_shape=(jax.ShapeDtypeStruct((S, H * VW), F32), jax.ShapeDtypeStruct((H, S, LANES), F32)),
        grid_spec=pltpu.PrefetchScalarGridSpec(
            num_scalar_prefetch=2, grid=(H // HB, qi.shape[0]),
            in_specs=[pl.BlockSpec((t, HB * QW), lambda g, p, qi, kj: (qi[p], g)),
                      pl.BlockSpec((t, HB * QW), lambda g, p, qi, kj: (kj[p], g)),
                      pl.BlockSpec((t, HB * VW), lambda g, p, qi, kj: (kj[p], g))],
            out_specs=(pl.BlockSpec((t, HB * VW), lambda g, p, qi, kj: (qi[p], g)),
                       pl.BlockSpec((HB, t, LANES), lambda g, p, qi, kj: (g, qi[p], 0))),
            scratch_shapes=[pltpu.VMEM((HB, t, LANES), F32), pltpu.VMEM((HB, t, LANES), F32), pltpu.VMEM((HB, t, VW), F32)]),
        compiler_params=_params("parallel", "arbitrary"),
    )(qi, kj, qp, kp, v)


def _flash_bwd(qp, kp, v, o, do, lse, *, name):
    S = qp.shape[0]
    H = v.shape[1] // MLA_V
    t = min(S, 512)
    nb = S // t
    HB = FLASH_HB
    qi, kj = _causal_pairs(nb)
    QW, VW = MLA_QK_PAD, MLA_V
    tn = (((0,), (0,)), ((), ()))
    nt = (((1,), (1,)), ((), ()))

    def body(qi_ref, kj_ref, q_ref, k_ref, v_ref, o_ref, do_ref, lse_ref, dq_ref, dk_ref, dv_ref, dq_s):
        pr = pl.program_id(1)
        i = qi_ref[pr]
        j = kj_ref[pr]
        rows = pl.ds(pl.multiple_of(j * t, t), t)

        @pl.when(pr == 0)
        def _():
            dk_ref[...] = jnp.zeros_like(dk_ref)
            dv_ref[...] = jnp.zeros_like(dv_ref)

        @pl.when(j == 0)
        def _():
            dq_s[...] = jnp.zeros_like(dq_s)

        def step(diagonal):
            for hh in range(HB):
                q = q_ref[:, hh * QW:(hh + 1) * QW]
                k = k_ref[:, hh * QW:(hh + 1) * QW]
                dob = do_ref[:, hh * VW:(hh + 1) * VW]
                p = jnp.exp(_scores(q, k, diagonal, t) - lse_ref[hh][:, :1])
                delta = jnp.sum(dob.astype(F32) * o_ref[:, hh * VW:(hh + 1) * VW], axis=1, keepdims=True)
                dp = lax.dot_general(dob, v_ref[:, hh * VW:(hh + 1) * VW], nt, preferred_element_type=F32)
                dsb = (p * (dp - delta)).astype(BF16)
                dv_ref[rows, hh * VW:(hh + 1) * VW] += lax.dot_general(p.astype(BF16), dob, tn, preferred_element_type=F32)
                dk_ref[rows, hh * QW:(hh + 1) * QW] += lax.dot_general(dsb, q, tn, preferred_element_type=F32)
                dq = dq_s[:, hh * QW:(hh + 1) * QW] + jnp.dot(dsb, k, preferred_element_type=F32)
                if diagonal:
                    dq_ref[:, hh * QW:(hh + 1) * QW] = dq
                else:
                    dq_s[:, hh * QW:(hh + 1) * QW] = dq

        @pl.when(i != j)
        def _():
            step(False)

        @pl.when(i == j)
        def _():
            step(True)

    qside = lambda g, p, qi, kj: (qi[p], g)
    kside = lambda g, p, qi, kj: (kj[p], g)
    whole = lambda g, p, qi, kj: (0, g)
    return pl.pallas_call(
        body, name=name,
        out_shape=(jax.ShapeDtypeStruct((S, H * QW), F32), jax.ShapeDtypeStruct((S, H * QW), F32),
                   jax.ShapeDtypeStruct((S, H * VW), F32)),
        grid_spec=pltpu.PrefetchScalarGridSpec(
            num_scalar_prefetch=2, grid=(H // HB, qi.shape[0]),
            in_specs=[pl.BlockSpec((t, HB * QW), qside), pl.BlockSpec((t, HB * QW), kside), pl.BlockSpec((t, HB * VW), kside),
                      pl.BlockSpec((t, HB * VW), qside), pl.BlockSpec((t, HB * VW), qside),
                      pl.BlockSpec((HB, t, LANES), lambda g, p, qi, kj: (g, qi[p], 0))],
            out_specs=(pl.BlockSpec((t, HB * QW), qside), pl.BlockSpec((S, HB * QW), whole), pl.BlockSpec((S, HB * VW), whole)),
            scratch_shapes=[pltpu.VMEM((t, HB * QW), F32)]),
        compiler_params=_params("parallel", "arbitrary"),
    )(qi, kj, qp, kp, v, o, do, lse)


def _swa_kv_halves(blk, hf, lo):
    if hf == 0:
        a = jnp.where(lo, blk, 0.0)
        b = pltpu.roll(a, 64, 1)
    else:
        b = jnp.where(lo, 0.0, blk)
        a = pltpu.roll(b, 64, 1)
    return a.astype(BF16), b.astype(BF16)


def _swa_softmax(qb, kx, bias, neg0, sk):
    s = lax.dot_general(qb, kx, (((1,), (1,)), ((), ())), preferred_element_type=F32) + bias + neg0
    m = jnp.maximum(jnp.max(s, axis=1, keepdims=True), sk)
    e = jnp.exp(s - m)
    es = jnp.exp(sk - m)
    inv = 1.0 / (jnp.sum(e, axis=1, keepdims=True) + es)
    return e * inv, es * inv


def _swa_fwd(z_swa, bias_m, sinks, *, name):
    S, W = z_swa.shape
    NH = bias_m.shape[0]
    G = NH // SWA_KVH
    QW = NH * SWA_HD
    KW = SWA_KVH * SWA_HD
    nb = S // SWA_BLOCK
    B = SWA_BLOCK
    assert G % 2 == 0 and SWA_KVH % 2 == 0 and W == QW + 2 * KW

    def body(sink_ref, q_ref, kvc_ref, kvp_ref, b_ref, o_ref):
        n = pl.program_id(0)
        lo = lax.broadcasted_iota(jnp.int32, (2 * B, LANES), 1) < 64
        col = lax.broadcasted_iota(jnp.int32, (B, 2 * B), 1)
        neg0 = jnp.where(jnp.logical_and(col < B, n == 0), NEG, 0.0)
        for kb in range(SWA_KVH // 2):
            kblk = jnp.concatenate([kvp_ref[:, kb * LANES:(kb + 1) * LANES], kvc_ref[:, kb * LANES:(kb + 1) * LANES]], axis=0)
            vblk = jnp.concatenate([kvp_ref[:, KW + kb * LANES:KW + (kb + 1) * LANES],
                                    kvc_ref[:, KW + kb * LANES:KW + (kb + 1) * LANES]], axis=0)
            for hf in range(2):
                kvh = 2 * kb + hf
                ks = _swa_kv_halves(kblk, hf, lo)
                vs = _swa_kv_halves(vblk, hf, lo)
                for pb in range(G // 2):
                    P = kvh * (G // 2) + pb
                    qb = (q_ref[:, P * LANES:(P + 1) * LANES] * (SWA_HD ** -0.5)).astype(BF16)
                    acc = jnp.zeros((B, LANES), F32)
                    for e in range(2):
                        h = 2 * P + e
                        p, _ = _swa_softmax(qb, ks[e], b_ref[h], neg0, sink_ref[h])
                        acc = acc + jnp.dot(p.astype(BF16), vs[e], preferred_element_type=F32)
                    o_ref[:, P * LANES:(P + 1) * LANES] = acc

    kvcol = QW // (2 * KW)
    assert QW % (2 * KW) == 0
    return pl.pallas_call(
        body, name=name,
        out_shape=jax.ShapeDtypeStruct((S, QW), F32), grid=(nb,),
        in_specs=[SMEM_FULL, pl.BlockSpec((B, QW), lambda n: (n, 0)), pl.BlockSpec((B, 2 * KW), lambda n: (n, kvcol)),
                  pl.BlockSpec((B, 2 * KW), lambda n: (jnp.maximum(n - 1, 0), kvcol)),
                  pl.BlockSpec((NH, B, 2 * B), lambda n: (0, 0, 0))],
        out_specs=pl.BlockSpec((B, QW), lambda n: (n, 0)),
        compiler_params=_params("parallel"),
    )(sinks, z_swa, z_swa, z_swa, bias_m)


def _swa_bwd(z_swa, bias_m, sinks, o, do, *, name):
    S, W = z_swa.shape
    NH = bias_m.shape[0]
    G = NH // SWA_KVH
    QW = NH * SWA_HD
    KW = SWA_KVH * SWA_HD
    nb = S // SWA_BLOCK
    B = SWA_BLOCK
    scale = SWA_HD ** -0.5
    tn = (((0,), (0,)), ((), ()))
    nt = (((1,), (1,)), ((), ()))

    def fold(x, hf, lo):
        x = x + pltpu.roll(x, 64, 1)
        return jnp.where(lo, x, 0.0) if hf == 0 else jnp.where(lo, 0.0, x)

    def body(sink_ref, q_ref, kvc_ref, kvp_ref, b_ref, o_ref, do_ref, dz_ref, dbias_ref, dsink_ref,
             cq_s, ck_s, cv_s, nq_s, nk_s, nv_s, pk_s, pv_s):
        n = pl.program_id(0)

        @pl.when(n == 0)
        def _():
            dbias_ref[...] = jnp.zeros_like(dbias_ref)
            dsink_ref[...] = jnp.zeros_like(dsink_ref)
            cq_s[...] = jnp.zeros_like(cq_s)
            ck_s[...] = jnp.zeros_like(ck_s)
            cv_s[...] = jnp.zeros_like(cv_s)

        @pl.when(n == nb)
        def _():
            pk_s[...] = jnp.zeros_like(pk_s)
            pv_s[...] = jnp.zeros_like(pv_s)

        @pl.when(n < nb)
        def _():
            lo = lax.broadcasted_iota(jnp.int32, (2 * B, LANES), 1) < 64
            lo1 = lax.broadcasted_iota(jnp.int32, (B, LANES), 1) < 64
            lane1 = lax.broadcasted_iota(jnp.int32, (1, LANES), 1)
            col = lax.broadcasted_iota(jnp.int32, (B, 2 * B), 1)
            neg0 = jnp.where(jnp.logical_and(col < B, n == 0), NEG, 0.0)
            dsink = jnp.zeros((1, LANES), F32)
            for kb in range(SWA_KVH // 2):
                kblk = jnp.concatenate([kvp_ref[:, kb * LANES:(kb + 1) * LANES], kvc_ref[:, kb * LANES:(kb + 1) * LANES]], axis=0)
                vblk = jnp.concatenate([kvp_ref[:, KW + kb * LANES:KW + (kb + 1) * LANES],
                                        kvc_ref[:, KW + kb * LANES:KW + (kb + 1) * LANES]], axis=0)
                dkblk = jnp.zeros((2 * B, LANES), F32)
                dvblk = jnp.zeros((2 * B, LANES), F32)
                for hf in range(2):
                    kvh = 2 * kb + hf
                    ks = _swa_kv_halves(kblk, hf, lo)
                    vs = _swa_kv_halves(vblk, hf, lo)
                    dkj = jnp.zeros((2 * B, LANES), F32)
                    dvj = jnp.zeros((2 * B, LANES), F32)
                    for pb in range(G // 2):
                        P = kvh * (G // 2) + pb
                        qb = (q_ref[:, P * LANES:(P + 1) * LANES] * scale).astype(BF16)
                        dob = do_ref[:, P * LANES:(P + 1) * LANES]
                        prod = dob * o_ref[:, P * LANES:(P + 1) * LANES]
                        dob = dob.astype(BF16)
                        dqp = jnp.zeros((B, LANES), F32)
                        for e in range(2):
                            h = 2 * P + e
                            keep = lo1 if e == 0 else jnp.logical_not(lo1)
                            p, psink = _swa_softmax(qb, ks[e], b_ref[h], neg0, sink_ref[h])
                            delta = jnp.sum(jnp.where(keep, prod, 0.0), axis=1, keepdims=True)
                            dp = lax.dot_general(dob, vs[e], nt, preferred_element_type=F32)
                            ds = p * (dp - delta)
                            dbias_ref[h] += ds
                            dsh = -jnp.sum(psink * delta, axis=0, keepdims=True)
                            dsink = dsink + jnp.where(lane1 == h, dsh, 0.0)
                            dsb = ds.astype(BF16)
                            dqp = dqp + jnp.dot(dsb, ks[e], preferred_element_type=F32)
                            keep2 = lo if e == 0 else jnp.logical_not(lo)
                            dkj = dkj + jnp.where(keep2, lax.dot_general(dsb, qb, tn, preferred_element_type=F32), 0.0)
                            dvj = dvj + jnp.where(keep2, lax.dot_general(p.astype(BF16), dob, tn, preferred_element_type=F32), 0.0)
                        nq_s[:, P * LANES:(P + 1) * LANES] = dqp * scale
                    dkblk = dkblk + fold(dkj, hf, lo)
                    dvblk = dvblk + fold(dvj, hf, lo)
                pk_s[:, kb * LANES:(kb + 1) * LANES] = dkblk[:B]
                nk_s[:, kb * LANES:(kb + 1) * LANES] = dkblk[B:]
                pv_s[:, kb * LANES:(kb + 1) * LANES] = dvblk[:B]
                nv_s[:, kb * LANES:(kb + 1) * LANES] = dvblk[B:]
            dsink_ref[...] += dsink

        dz_ref[:, :QW] = cq_s[...].astype(BF16)
        dz_ref[:, QW:QW + KW] = (ck_s[...] + pk_s[...]).astype(BF16)
        dz_ref[:, QW + KW:] = (cv_s[...] + pv_s[...]).astype(BF16)

        @pl.when(n < nb)
        def _():
            cq_s[...] = nq_s[...]
            ck_s[...] = nk_s[...]
            cv_s[...] = nv_s[...]

    kvcol = QW // (2 * KW)
    cur = lambda n: (jnp.minimum(n, nb - 1), 0)
    return pl.pallas_call(
        body, name=name,
        out_shape=(jax.ShapeDtypeStruct((S, W), BF16), jax.ShapeDtypeStruct((NH, B, 2 * B), F32),
                   jax.ShapeDtypeStruct((1, LANES), F32)),
        grid=(nb + 1,),
        in_specs=[SMEM_FULL, pl.BlockSpec((B, QW), cur), pl.BlockSpec((B, 2 * KW), lambda n: (jnp.minimum(n, nb - 1), kvcol)),
                  pl.BlockSpec((B, 2 * KW), lambda n: (jnp.maximum(jnp.minimum(n, nb - 1) - 1, 0), kvcol)),
                  pl.BlockSpec((NH, B, 2 * B), lambda n: (0, 0, 0)), pl.BlockSpec((B, QW), cur), pl.BlockSpec((B, QW), cur)],
        out_specs=(pl.BlockSpec((B, W), lambda n: (jnp.maximum(n - 1, 0), 0)),
                   pl.BlockSpec((NH, B, 2 * B), lambda n: (0, 0, 0)), pl.BlockSpec((1, LANES), lambda n: (0, 0))),
        scratch_shapes=[pltpu.VMEM((B, QW), F32), pltpu.VMEM((B, KW), F32), pltpu.VMEM((B, KW), F32),
                        pltpu.VMEM((B, QW), F32), pltpu.VMEM((B, KW), F32), pltpu.VMEM((B, KW), F32),
                        pltpu.VMEM((B, KW), F32), pltpu.VMEM((B, KW), F32)],
        compiler_params=_params("arbitrary"),
    )(sinks, z_swa, z_swa, z_swa, bias_m, o, do)


def _gate_fwd(zg, o_a, o_b, *, name):
    S, D = o_a.shape
    tr = min(S, 512)

    def body(z_ref, a_ref, b_ref, m_ref):
        ga = jax.nn.sigmoid(z_ref[:, :PAIR])
        gb = jax.nn.sigmoid(z_ref[:, PAIR:])
        m_ref[...] = (ga * a_ref[...] + gb * b_ref[...]).astype(BF16)

    col = pl.BlockSpec((tr, PAIR), lambda i, j: (i, j))
    return pl.pallas_call(
        body, name=name, out_shape=jax.ShapeDtypeStruct((S, D), BF16), grid=(S // tr, D // PAIR),
        in_specs=[pl.BlockSpec((tr, 2 * PAIR), lambda i, j: (i, j)), col, col], out_specs=col,
        compiler_params=_params("parallel", "parallel"),
    )(zg, o_a, o_b)


def _gate_bwd(dmix, zg, o_a, o_b, *, name):
    S, D = o_a.shape
    tr = min(S, 512)

    def body(d_ref, z_ref, a_ref, b_ref, da_ref, db_ref, dz_ref):
        d = d_ref[...]
        ga = jax.nn.sigmoid(z_ref[:, :PAIR])
        gb = jax.nn.sigmoid(z_ref[:, PAIR:])
        da_ref[...] = (d * ga).astype(BF16)
        db_ref[...] = d * gb
        dz_ref[:, :PAIR] = (d * a_ref[...] * (ga * (1.0 - ga))).astype(BF16)
        dz_ref[:, PAIR:] = (d * b_ref[...] * (gb * (1.0 - gb))).astype(BF16)

    col = pl.BlockSpec((tr, PAIR), lambda i, j: (i, j))
    wide = pl.BlockSpec((tr, 2 * PAIR), lambda i, j: (i, j))
    return pl.pallas_call(
        body, name=name,
        out_shape=(jax.ShapeDtypeStruct((S, D), BF16), jax.ShapeDtypeStruct((S, D), F32), jax.ShapeDtypeStruct((S, 2 * D), BF16)),
        grid=(S // tr, D // PAIR), in_specs=[col, wide, col, col], out_specs=(col, col, wide),
        compiler_params=_params("parallel", "parallel"),
    )(dmix, zg, o_a, o_b)


def _conv_taps(t_ref, prev_ref, i, tr):
    cur = t_ref[...]
    live = (i > 0).astype(F32)
    p6 = prev_ref[6:7, :] * live
    p7 = prev_ref[7:8, :] * live
    row = lax.broadcasted_iota(jnp.int32, cur.shape, 0)
    t1 = jnp.where(row == 0, p7, pltpu.roll(cur, 1, 0))
    t2 = jnp.where(row == 0, p6, jnp.where(row == 1, p7, pltpu.roll(cur, 2, 0)))
    return cur, t1, t2


def _conv_u(t_ref, prev_ref, w_ref, b_ref, i, tr):
    cur, t1, t2 = _conv_taps(t_ref, prev_ref, i, tr)
    u = ((b_ref[...] + w_ref[0:1, :] * t2) + w_ref[1:2, :] * t1) + w_ref[2:3, :] * cur
    return u, cur, t1, t2


def _conv_specs(tr, S):
    blk = pl.BlockSpec((tr, 2 * PAIR), lambda j, i: (i, j))
    prev = pl.BlockSpec((8, 2 * PAIR), lambda j, i: (jnp.maximum(i * (tr // 8) - 1, 0), j))
    w3 = pl.BlockSpec((3, 2 * PAIR), lambda j, i: (0, j))
    w1 = pl.BlockSpec((1, 2 * PAIR), lambda j, i: (0, j))
    return blk, prev, w3, w1


def _conv_gate_fwd(t, cw, cb, *, name):
    S, F2 = t.shape
    tr = min(S, 512)
    blk, prev, w3, w1 = _conv_specs(tr, S)

    def body(t_ref, prev_ref, w_ref, b_ref, a_ref):
        u, _, _, _ = _conv_u(t_ref, prev_ref, w_ref, b_ref, pl.program_id(1), tr)
        a_ref[...] = (jax.nn.silu(u[:, :PAIR]) * u[:, PAIR:]).astype(BF16)

    return pl.pallas_call(
        body, name=name, out_shape=jax.ShapeDtypeStruct((S, F2 // 2), BF16), grid=(F2 // (2 * PAIR), S // tr),
        in_specs=[blk, prev, w3, w1], out_specs=pl.BlockSpec((tr, PAIR), lambda j, i: (i, j)),
        compiler_params=_params("parallel", "parallel"),
    )(t, t, cw, cb)


def _conv_gate_bwd(t, da, cw, cb, *, name):
    S, F2 = t.shape
    tr = min(S, 256)
    blk, prev, w3, w1 = _conv_specs(tr, S)

    def body(t_ref, prev_ref, da_ref, w_ref, b_ref, du_ref, dw_ref, db_ref):
        i = pl.program_id(1)
        u, cur, t1, t2 = _conv_u(t_ref, prev_ref, w_ref, b_ref, i, tr)
        u1 = u[:, :PAIR]
        u2 = u[:, PAIR:]
        d = da_ref[...]
        sg = jax.nn.sigmoid(u1)
        du1 = d * u2 * (sg * (1.0 + u1 * (1.0 - sg)))
        du2 = d * (u1 * sg)
        du = jnp.concatenate([du1, du2], axis=1)
        du_ref[...] = du.astype(BF16)
        first = i == 0
        _acc_rows(db_ref, du, first)
        dw = jnp.concatenate([jnp.sum(du * t2, axis=0, keepdims=True), jnp.sum(du * t1, axis=0, keepdims=True),
                              jnp.sum(du * cur, axis=0, keepdims=True)], axis=0)

        @pl.when(first)
        def _():
            dw_ref[...] = dw

        @pl.when(jnp.logical_not(first))
        def _():
            dw_ref[...] += dw

    return pl.pallas_call(
        body, name=name,
        out_shape=(jax.ShapeDtypeStruct((S, F2), BF16), jax.ShapeDtypeStruct((3, F2), F32), jax.ShapeDtypeStruct((1, F2), F32)),
        grid=(F2 // (2 * PAIR), S // tr),
        in_specs=[blk, prev, pl.BlockSpec((tr, PAIR), lambda j, i: (i, j)), w3, w1], out_specs=(blk, w3, w1),
        compiler_params=_params("parallel", "arbitrary"),
    )(t, t, da, cw, cb)


def _conv_bwd_dt(du, cw, *, name):
    S, F2 = du.shape
    tr = min(S, 512)
    nrow = S // tr
    blk, _, w3, _ = _conv_specs(tr, S)
    nxt = pl.BlockSpec((16, 2 * PAIR), lambda j, i: (jnp.minimum((i + 1) * (tr // 16), S // 16 - 1), j))

    def body(d_ref, next_ref, w_ref, dt_ref):
        i = pl.program_id(1)
        cur = d_ref[...].astype(F32)
        live = (i < nrow - 1).astype(F32)
        n0 = next_ref[0:1, :].astype(F32) * live
        n1 = next_ref[1:2, :].astype(F32) * live
        row = lax.broadcasted_iota(jnp.int32, cur.shape, 0)
        d1 = jnp.where(row == tr - 1, n0, pltpu.roll(cur, tr - 1, 0))
        d2 = jnp.where(row == tr - 1, n1, jnp.where(row == tr - 2, n0, pltpu.roll(cur, tr - 2, 0)))
        dt_ref[...] = ((w_ref[2:3, :] * cur + w_ref[1:2, :] * d1) + w_ref[0:1, :] * d2).astype(BF16)

    return pl.pallas_call(
        body, name=name, out_shape=jax.ShapeDtypeStruct((S, F2), BF16), grid=(F2 // (2 * PAIR), nrow),
        in_specs=[blk, nxt, w3], out_specs=blk, compiler_params=_params("parallel", "parallel"),
    )(du, du, cw)


def _ada_fwd(c_all, w, b, *, name):
    Bn, D = c_all.shape
    N = w.shape[1]
    tn = _tile(N, 512)

    def body(c_ref, w_ref, b_ref, o_ref):
        o_ref[...] = jnp.dot(jax.nn.silu(c_ref[...]), w_ref[...], preferred_element_type=F32, precision=HIGHEST) + b_ref[...]

    return pl.pallas_call(
        body, name=name, out_shape=jax.ShapeDtypeStruct((Bn, N), F32), grid=(N // tn,),
        in_specs=[pl.BlockSpec((Bn, D), lambda j: (0, 0)), pl.BlockSpec((D, tn), lambda j: (0, j)),
                  pl.BlockSpec((1, tn), lambda j: (0, j))],
        out_specs=pl.BlockSpec((Bn, tn), lambda j: (0, j)), compiler_params=_params("parallel"),
    )(c_all, w, b)


def _ada_bwd(c_all_t, dmod, *, name):
    D, Bn = c_all_t.shape
    N = dmod.shape[1]
    tm = _tile(D, 512, 8)
    tn = _tile(N, 1536)

    def body(c_ref, d_ref, o_ref):
        o_ref[...] = jnp.dot(jax.nn.silu(c_ref[...]), d_ref[...], preferred_element_type=F32, precision=HIGHEST)

    return pl.pallas_call(
        body, name=name, out_shape=jax.ShapeDtypeStruct((D, N), F32), grid=(D // tm, N // tn),
        in_specs=[pl.BlockSpec((tm, Bn), lambda i, j: (i, 0)), pl.BlockSpec((Bn, tn), lambda i, j: (0, j))],
        out_specs=pl.BlockSpec((tm, tn), lambda i, j: (i, j)), compiler_params=_params("parallel", "parallel"),
    )(c_all_t, dmod)


def _adamw(w, g, m, v, *, name):
    R, C = w.shape
    tr = R if R * C <= (1 << 18) else _tile(R, max(8, (1 << 18) // C), 8)

    def body(w_ref, g_ref, m_ref, v_ref, d_ref, nm_ref, nv_ref):
        gv = g_ref[...]
        nm = ADAM_B1 * m_ref[...] + (1.0 - ADAM_B1) * gv
        nv = ADAM_B2 * v_ref[...] + (1.0 - ADAM_B2) * (gv * gv)
        m_hat = nm / (1.0 - ADAM_B1 ** ADAM_STEP)
        v_hat = nv / (1.0 - ADAM_B2 ** ADAM_STEP)
        d_ref[...] = -ADAM_LR * (m_hat / (jnp.sqrt(v_hat) + ADAM_EPS) + ADAM_WD * w_ref[...])
        nm_ref[...] = nm
        nv_ref[...] = nv

    blk = pl.BlockSpec((tr, C), lambda i: (i, 0))
    shp = jax.ShapeDtypeStruct((R, C), F32)
    return pl.pallas_call(
        body, name=name, out_shape=(shp, shp, shp), grid=(R // tr,), in_specs=[blk] * 4, out_specs=(blk,) * 3,
        compiler_params=_params("parallel"),
    )(w, g, m, v)


def _place():
    x, y, c = lax.axis_index("x"), lax.axis_index("y"), lax.axis_index("c")
    return x, y, c, [(1 - x, y), (x, 1 - y), (1 - x, 1 - y)]


def _remote(src, dst, send_sem, recv_sem, dev):
    return pltpu.make_async_remote_copy(src_ref=src, dst_ref=dst, send_sem=send_sem, recv_sem=recv_sem,
                                        device_id=dev, device_id_type=MESH)


def _allgather8(v, *, name):
    R, C = v.shape

    def body(v_ref, out_ref, send_sems, recv_sems, local_sem):
        x, y, c, chips = _place()
        me, sibling = (x, y, c), (x, y, 1 - c)

        def rows(px, py, pc):
            return out_ref.at[pl.ds((4 * px + 2 * py + pc) * R, R), :]

        def copy(k, block, to, src=None):
            return _remote(rows(*block) if src is None else src, rows(*block), send_sems.at[k], recv_sems.at[k], to)

        mine = pltpu.make_async_copy(v_ref, rows(*me), local_sem)
        mine.start()
        first = [copy(0, me, sibling, src=v_ref)]
        first += [copy(1 + j, me, (*chip, c), src=v_ref) for j, chip in enumerate(chips)]
        for cp in first:
            cp.start()
        passed = [copy(4 + j, (*chip, c), sibling) for j, chip in enumerate(chips)]
        for j, chip in enumerate(chips):
            copy(1 + j, (*chip, c), me).wait_recv()
            passed[j].start()
        copy(0, sibling, me).wait_recv()
        for j, chip in enumerate(chips):
            copy(4 + j, (*chip, 1 - c), me).wait_recv()
        for cp in first + passed:
            cp.wait_send()
        mine.wait()

    out = pl.pallas_call(
        body, name=name, out_shape=jax.ShapeDtypeStruct((N_DEV * R, C), v.dtype),
        in_specs=[VMEM_FULL], out_specs=VMEM_FULL,
        scratch_shapes=[pltpu.SemaphoreType.DMA((7,)), pltpu.SemaphoreType.DMA((7,)), pltpu.SemaphoreType.DMA],
    )(v)
    return out.reshape(N_DEV, R, C)


def _gather_weights(ws, *, name):
    n = len(ws)

    def body(*refs):
        ins, outs = refs[:n], refs[n:2 * n]
        send_sems, recv_sems, local_sems = refs[2 * n:]
        x, y, c, chips = _place()
        k = 2 * x + y
        sibling = (x, y, 1 - c)

        def piece(i, chip_idx, who):
            r2 = ws[i].shape[0] // 2
            return outs[i].at[chip_idx, pl.ds(who * r2, r2)]

        local = [pltpu.make_async_copy(ins[i], outs[i].at[k], local_sems.at[i]) for i in range(n)]
        for cp in local:
            cp.start()
        sends = []
        for i in range(n):
            r2 = ws[i].shape[0] // 2
            for r, (cx, cy) in enumerate(chips):
                sends.append(_remote(ins[i].at[pl.ds(c * r2, r2)], piece(i, k, c), send_sems.at[6 * i + r],
                                     recv_sems.at[6 * i + r], (cx, cy, c)))
                sends[-1].start()
        for i in range(n):
            for r, (cx, cy) in enumerate(chips):
                got = piece(i, 2 * cx + cy, c)
                _remote(got, got, send_sems.at[6 * i + r], recv_sems.at[6 * i + r], (cx, cy, c)).wait_recv()
                sends.append(_remote(got, got, send_sems.at[6 * i + 3 + r], recv_sems.at[6 * i + 3 + r], sibling))
                sends[-1].start()
        for i in range(n):
            for r, (cx, cy) in enumerate(chips):
                got = piece(i, 2 * cx + cy, 1 - c)
                _remote(got, got, send_sems.at[6 * i + 3 + r], recv_sems.at[6 * i + 3 + r], sibling).wait_recv()
        for cp in sends:
            cp.wait_send()
        for cp in local:
            cp.wait()

    return pl.pallas_call(
        body, name=name,
        out_shape=[jax.ShapeDtypeStruct((N_CHIP,) + w.shape, w.dtype) for w in ws],
        in_specs=[ANY] * n, out_specs=[ANY] * n,
        scratch_shapes=[pltpu.SemaphoreType.DMA((6 * n,)), pltpu.SemaphoreType.DMA((6 * n,)), pltpu.SemaphoreType.DMA((n,))],
    )(*ws)


def _send_other_halves(gs, *, name):
    n = len(gs)

    def body(*refs):
        ins, outs = refs[:n], refs[n:2 * n]
        send_sems, recv_sems = refs[2 * n:]
        x, y, c, _ = _place()
        sibling = (x, y, 1 - c)
        cps = []
        for i in range(n):
            r2 = gs[i].shape[1] // 2
            cps.append(_remote(ins[i].at[:, pl.ds((1 - c) * r2, r2)], outs[i], send_sems.at[i], recv_sems.at[i], sibling))
            cps[-1].start()
        for cp in cps:
            cp.wait()

    return pl.pallas_call(
        body, name=name,
        out_shape=[jax.ShapeDtypeStruct((N_CHIP, g.shape[1] // 2, g.shape[2]), g.dtype) for g in gs],
        in_specs=[ANY] * n, out_specs=[ANY] * n,
        scratch_shapes=[pltpu.SemaphoreType.DMA((n,)), pltpu.SemaphoreType.DMA((n,))],
    )(*gs)


def _scatter_to_chips(ps, *, name):
    n = len(ps)

    def body(*refs):
        ins, outs = refs[:n], refs[n:2 * n]
        send_sems, recv_sems, local_sems = refs[2 * n:]
        x, y, c, chips = _place()
        k = 2 * x + y
        local = [pltpu.make_async_copy(ins[i].at[k], outs[i].at[k], local_sems.at[i]) for i in range(n)]
        for cp in local:
            cp.start()
        cps = []
        for i in range(n):
            for r, (cx, cy) in enumerate(chips):
                cps.append(_remote(ins[i].at[2 * cx + cy], outs[i].at[k], send_sems.at[3 * i + r], recv_sems.at[3 * i + r],
                                   (cx, cy, c)))
                cps[-1].start()
        for i in range(n):
            for r, (cx, cy) in enumerate(chips):
                got = outs[i].at[2 * cx + cy]
                _remote(got, got, send_sems.at[3 * i + r], recv_sems.at[3 * i + r], (cx, cy, c)).wait_recv()
        for cp in cps:
            cp.wait_send()
        for cp in local:
            cp.wait()

    return pl.pallas_call(
        body, name=name,
        out_shape=[jax.ShapeDtypeStruct(p.shape, p.dtype) for p in ps],
        in_specs=[ANY] * n, out_specs=[ANY] * n,
        scratch_shapes=[pltpu.SemaphoreType.DMA((3 * n,)), pltpu.SemaphoreType.DMA((3 * n,)), pltpu.SemaphoreType.DMA((n,))],
    )(*ps)


def _share_halves(ts, *, name):
    n = len(ts)

    def body(*refs):
        outs = refs[n:2 * n]
        send_sems, recv_sems = refs[2 * n:]
        x, y, c, _ = _place()
        sibling = (x, y, 1 - c)
        cps = []
        for i in range(n):
            r2 = ts[i].shape[0] // 2
            mine = outs[i].at[pl.ds(c * r2, r2)]
            cps.append(_remote(mine, mine, send_sems.at[i], recv_sems.at[i], sibling))
            cps[-1].start()
        for i in range(n):
            r2 = ts[i].shape[0] // 2
            got = outs[i].at[pl.ds((1 - c) * r2, r2)]
            _remote(got, got, send_sems.at[i], recv_sems.at[i], sibling).wait_recv()
        for cp in cps:
            cp.wait_send()

    return pl.pallas_call(
        body, name=name,
        out_shape=[jax.ShapeDtypeStruct(t.shape, t.dtype) for t in ts],
        in_specs=[ANY] * n, out_specs=[ANY] * n, input_output_aliases={i: i for i in range(n)},
        scratch_shapes=[pltpu.SemaphoreType.DMA((n,)), pltpu.SemaphoreType.DMA((n,))],
    )(*ts)


def _add_sibling(g, recv, c_idx, *, name):
    _, R, C = g.shape
    r2 = R // 2
    tr = _tile(r2, max(16, (1 << 19) // C), 16)
    nr = r2 // tr

    def body(c_ref, g_ref, r_ref, o_ref):
        o_ref[...] = (g_ref[...].astype(F32) + r_ref[...].astype(F32)).astype(BF16)

    return pl.pallas_call(
        body, name=name, out_shape=jax.ShapeDtypeStruct((N_CHIP, r2, C), BF16),
        grid_spec=pltpu.PrefetchScalarGridSpec(
            num_scalar_prefetch=1, grid=(N_CHIP, nr),
            in_specs=[pl.BlockSpec((None, tr, C), lambda j, i, c_ref: (j, c_ref[0] * nr + i, 0)),
                      pl.BlockSpec((None, tr, C), lambda j, i, c_ref: (j, i, 0))],
            out_specs=pl.BlockSpec((None, tr, C), lambda j, i, c_ref: (j, i, 0))),
        compiler_params=_params("parallel", "parallel"),
    )(c_idx, g, recv)


def _sum_chips(p, c_idx, *, name):
    _, r2, C = p.shape
    tr = _tile(r2, max(16, (1 << 18) // C), 16)
    nr = r2 // tr

    def body(c_ref, p_ref, o_ref):
        acc = p_ref[0].astype(F32)
        for j in range(1, N_CHIP):
            acc = acc + p_ref[j].astype(F32)
        o_ref[...] = acc

    return pl.pallas_call(
        body, name=name, out_shape=jax.ShapeDtypeStruct((2 * r2, C), F32),
        grid_spec=pltpu.PrefetchScalarGridSpec(
            num_scalar_prefetch=1, grid=(nr,),
            in_specs=[pl.BlockSpec((N_CHIP, tr, C), lambda i, c_ref: (0, i, 0))],
            out_specs=pl.BlockSpec((tr, C), lambda i, c_ref: (c_ref[0] * nr + i, 0))),
        compiler_params=_params("parallel"),
    )(c_idx, p)


def _sum_devices(v, *, name):
    n, R, C = v.shape

    def body(v_ref, o_ref):
        acc = v_ref[0]
        for j in range(1, n):
            acc = acc + v_ref[j]
        o_ref[...] = acc

    return pl.pallas_call(body, name=name, out_shape=jax.ShapeDtypeStruct((R, C), F32),
                          in_specs=[VMEM_FULL], out_specs=VMEM_FULL)(v)


def _pair(a, b):
    n = a.shape[1]
    parts = []
    for j in range(n // PAIR):
        parts += [a[:, j * PAIR:(j + 1) * PAIR], b[:, j * PAIR:(j + 1) * PAIR]]
    return jnp.concatenate(parts, axis=1)


def _unpair(p):
    nt = p.shape[1] // (2 * PAIR)
    a = jnp.concatenate([p[:, 2 * j * PAIR:(2 * j + 1) * PAIR] for j in range(nt)], axis=1)
    b = jnp.concatenate([p[:, (2 * j + 1) * PAIR:(2 * j + 2) * PAIR] for j in range(nt)], axis=1)
    return a, b


def _from_col_shards(g):
    return jnp.transpose(g, (1, 0, 2)).reshape(g.shape[1], N_CHIP * g.shape[2])


def _to_col_shards(w):
    R, N = w.shape
    return jnp.transpose(w.reshape(R, N_CHIP, N // N_CHIP), (1, 0, 2))


def _split_heads(w, widths):
    R, N = w.shape
    per = sum(widths)
    w3 = w.reshape(R, N // per, per)
    lo = w3[:, :, :widths[0]].reshape(R, -1)
    hi = w3[:, :, widths[0]:].reshape(R, -1)
    return jnp.concatenate([lo, hi], axis=1)


def _merge_heads(w, widths):
    R, N = w.shape
    H = N // sum(widths)
    lo = w[:, :H * widths[0]].reshape(R, H, widths[0])
    hi = w[:, H * widths[0]:].reshape(R, H, widths[1])
    return jnp.concatenate([lo, hi], axis=2).reshape(R, N)


def _t5_bucket(dist):
    max_exact = REL_BUCKETS // 2
    n = jnp.maximum(dist, 0)
    large = max_exact + (jnp.log(jnp.maximum(n, 1).astype(F32) / max_exact)
                         / jnp.log(jnp.asarray(REL_MAX_DIST / max_exact, F32))
                         * (REL_BUCKETS - max_exact)).astype(jnp.int32)
    large = jnp.minimum(large, REL_BUCKETS - 1)
    return jnp.where(n < max_exact, n, large)


def _rel_tables():
    a = jnp.arange(SWA_BLOCK)
    b = jnp.arange(2 * SWA_BLOCK)
    dist = SWA_BLOCK + a[:, None] - b[None, :]
    valid = jnp.logical_and(dist >= 0, dist < SWA_BLOCK)
    onehot = jnp.logical_and(_t5_bucket(dist)[..., None] == jnp.arange(REL_BUCKETS), valid[..., None])
    onehot = onehot.astype(F32).reshape(2 * SWA_BLOCK * SWA_BLOCK, REL_BUCKETS)
    negmask = jnp.where(valid, 0.0, NEG).astype(F32).reshape(1, -1)
    return onehot, negmask


def _rope_tables(S):
    pos = jnp.arange(S, dtype=F32)
    inv = ROPE_THETA ** (-jnp.arange(0, MLA_ROPE, 2, dtype=F32) / MLA_ROPE)
    ang = pos[:, None] * inv[None, :]
    ang = jnp.concatenate([ang, ang, ang, ang], axis=-1)
    return jnp.cos(ang), jnp.sin(ang)


def _flat_pad(parts, rows=8):
    flat = jnp.concatenate([p.reshape(1, -1) for p in parts], axis=1)
    n = flat.shape[1]
    width = -(-n // (rows * LANES)) * LANES
    return jnp.pad(flat, ((0, 0), (0, rows * width - n))).reshape(rows, width)


def _unflat(vec, shapes):
    flat = vec.reshape(-1)
    out, off = [], 0
    for s in shapes:
        n = 1
        for d in s:
            n *= d
        out.append(flat[off:off + n].reshape(s))
        off += n
    return out


def kernel(x, c, w_ada, b_ada, g_pre_mix, g_post_mix, w_in, g_q_lat, w_uq, g_kv_lat, w_ukv, rel_bias, sinks, w_o, g_pre_ffn, g_post_ffn, w_up, conv_w, conv_b, w_down, loss_target, m_w_ada, m_b_ada, m_g_pre_mix, m_g_post_mix, m_w_in, m_g_q_lat, m_w_uq, m_g_kv_lat, m_w_ukv, m_rel_bias, m_sinks, m_w_o, m_g_pre_ffn, m_g_post_ffn, m_w_up, m_conv_w, m_conv_b, m_w_down, v_w_ada, v_b_ada, v_g_pre_mix, v_g_post_mix, v_w_in, v_g_q_lat, v_w_uq, v_g_kv_lat, v_w_ukv, v_rel_bias, v_sinks, v_w_o, v_g_pre_ffn, v_g_post_ffn, v_w_up, v_conv_w, v_conv_b, v_w_down):
    S, D = x.shape[1], x.shape[2]
    Rq, Rkv = g_q_lat.shape[1], g_kv_lat.shape[1]
    H = D // MLA_V
    NH = D // SWA_HD
    KW = SWA_KVH * SWA_HD
    F = w_down.shape[1] * N_CHIP
    xi, yi, ci = lax.axis_index("x"), lax.axis_index("y"), lax.axis_index("c")
    chip = 2 * xi + yi
    me = 2 * chip + ci
    x2, tgt = x[0], loss_target[0]

    c_all = _allgather8(jnp.broadcast_to(c, (8, D)), name="gather_c")[:, 0, :]
    n3 = w_ada.shape[2]
    mod_part = _ada_fwd(c_all, w_ada[0], lax.dynamic_slice(b_ada, (0, chip * n3), (1, n3)), name="ada_fwd")
    mod_all = _allgather8(mod_part, name="gather_mod")
    mod_me = lax.dynamic_index_in_dim(mod_all[0::2], me, axis=1, keepdims=False).reshape(1, 6 * D)
    sh1, sc1, gt1, sh2, sc2, gt2 = [mod_me[:, k * D:(k + 1) * D] for k in range(6)]

    big = [w_in[0], w_uq[0], w_ukv[0], w_o[0], w_up[0], w_down[0]]
    gin, guq, gukv, go, gup, gdown = _gather_weights([w.astype(BF16) for w in big], name="gather_weights")
    win = _from_col_shards(gin)
    o_kr = Rq + Rkv
    o_q = o_kr + MLA_ROPE
    o_g = o_q + NH * SWA_HD + 2 * KW
    w_lat = jnp.concatenate([win[:, :o_q], win[:, o_kr:o_q]], axis=1)
    w_swa = win[:, o_q:o_g]
    w_gate = _pair(win[:, o_g:o_g + D], win[:, o_g + D:])
    w_in_all = jnp.concatenate([w_lat, w_swa, w_gate], axis=1)
    n_lat, n_swa = w_lat.shape[1], w_swa.shape[1]
    wuq = _split_heads(_from_col_shards(guq), (MLA_NOPE, MLA_ROPE))
    wukv = _split_heads(_from_col_shards(gukv), (MLA_NOPE, MLA_V))
    wo = go.reshape(D, D)
    wup_full = _from_col_shards(gup)
    wup = _pair(wup_full[:, :F], wup_full[:, F:])
    wdown = gdown.reshape(F, D)
    cw_all = _allgather8(jnp.pad(conv_w[0], ((0, 5), (0, 0))), name="gather_conv_w")[0::2, :3]
    cw_full = _from_col_shards(cw_all)
    cw = _pair(cw_full[:, :F], cw_full[:, F:])
    cb = _pair(conv_b[:, :F], conv_b[:, F:])
    onehot, negmask = _rel_tables()
    bias_m = (_matmul(rel_bias.T, onehot.T, name="rel_bias_table") + negmask).reshape(NH, SWA_BLOCK, 2 * SWA_BLOCK)
    cos, sin = _rope_tables(S)
    sink_v = sinks.reshape(NH)

    h = _modnorm_fwd(x2, g_pre_mix, sc1, sh1, name="pre_mix_norm")
    z_lat = _matmul(h, w_lat, name="in_proj_lat")
    z_swa = _matmul(h, w_swa, name="in_proj_swa")
    zg = _matmul(h, w_gate, name="in_proj_gate")
    nq, nkv = _lat_norm_fwd(z_lat, g_q_lat, g_kv_lat, name="lat_norm")
    q_raw = _matmul(nq, wuq, name="uq_proj")
    kv_raw = _matmul(nkv, wukv, name="ukv_proj")
    qp, kp, vv = _mla_pack_fwd(q_raw, kv_raw, z_lat, cos, sin, o_kr, name="mla_pack")
    o_a, lse = _flash_fwd(qp, kp, vv, name="mla_attn")
    o_b = _swa_fwd(z_swa, bias_m, sink_v, name="swa_attn")
    mixin = _gate_fwd(zg, o_a, o_b, name="gate_mix")
    mix = _matmul(mixin, wo, name="o_proj")
    x1 = _resnorm_fwd(x2, mix, g_post_mix, gt1, name="post_mix_norm")
    h2 = _modnorm_fwd(x1, g_pre_ffn, sc2, sh2, name="pre_ffn_norm")
    t = _matmul(h2, wup, name="up_proj")
    a = _conv_gate_fwd(t, cw, cb, name="conv_gate")
    yv = _matmul(a, wdown, name="down_proj")
    dout, loss_tile = _resnorm_loss(x1, yv, g_post_ffn, gt2, tgt, name="post_ffn_norm_loss")

    dy, dg_post_ffn, dgt2 = _resnorm_bwd(dout, yv, g_post_ffn, gt2, name="post_ffn_norm_bwd")
    da = _matmul(dy, wdown, tb=True, name="down_proj_dx")
    dw_down = _matmul(a, dy, ta=True, out_dtype=BF16, name="down_proj_dw")
    du, dcw_p, dcb_p = _conv_gate_bwd(t, da, cw, cb, name="conv_gate_bwd")
    dt = _conv_bwd_dt(du, cw, name="conv_bwd_dt")
    dh2 = _matmul(dt, wup, tb=True, name="up_proj_dx")
    dw_up_p = _matmul(h2, dt, ta=True, out_dtype=BF16, name="up_proj_dw")
    dx1, dg_pre_ffn, dsc2, dsh2 = _modnorm_bwd(dh2, x1, g_pre_ffn, sc2, dout, name="pre_ffn_norm_bwd")
    dmix, dg_post_mix, dgt1 = _resnorm_bwd(dx1, mix, g_post_mix, gt1, name="post_mix_norm_bwd")
    dmixin = _matmul(dmix, wo, tb=True, name="o_proj_dx")
    dw_o = _matmul(mixin, dmix, ta=True, out_dtype=BF16, name="o_proj_dw")
    do_a, do_b, dzg = _gate_bwd(dmixin, zg, o_a, o_b, name="gate_mix_bwd")
    dqp, dkp, dvv = _flash_bwd(qp, kp, vv, o_a, do_a, lse, name="mla_attn_bwd")
    dq_raw, dkv_raw, dkr = _mla_pack_bwd(dqp, dkp, dvv, cos, sin, name="mla_pack_bwd")
    dnq = _matmul(dq_raw, wuq, tb=True, name="uq_proj_dx")
    dw_uq_p = _matmul(nq, dq_raw, ta=True, out_dtype=BF16, name="uq_proj_dw")
    dnkv = _matmul(dkv_raw, wukv, tb=True, name="ukv_proj_dx")
    dw_ukv_p = _matmul(nkv, dkv_raw, ta=True, out_dtype=BF16, name="ukv_proj_dw")
    dz_lat, dg_q, dg_kv = _lat_norm_bwd(z_lat, dnq, dnkv, dkr, g_q_lat, g_kv_lat, name="lat_norm_bwd")
    dz_swa, dbias, dsink = _swa_bwd(z_swa, bias_m, sink_v, o_b, do_b, name="swa_attn_bwd")
    dz = jnp.concatenate([dz_lat, dz_swa, dzg], axis=1)
    dh = _matmul(dz, w_in_all, tb=True, name="in_proj_dx")
    dw_in_p = _matmul(h, dz, ta=True, out_dtype=BF16, name="in_proj_dw")
    grad_x, dg_pre_mix, dsc1, dsh1 = _modnorm_bwd(dh, x2, g_pre_mix, sc1, dx1, name="pre_mix_norm_bwd")
    drel = _matmul(dbias.reshape(NH, -1), onehot, name="rel_bias_bwd").T

    dga, dgb = _unpair(dw_in_p[:, n_lat + n_swa:])
    dw_in = jnp.concatenate([dw_in_p[:, :o_q], dw_in_p[:, n_lat:n_lat + n_swa], dga, dgb], axis=1)
    dw_uq = _merge_heads(dw_uq_p, (MLA_NOPE, MLA_ROPE))
    dw_ukv = _merge_heads(dw_ukv_p, (MLA_NOPE, MLA_V))
    dw_up = jnp.concatenate(_unpair(dw_up_p), axis=1)
    dcw = jnp.concatenate(_unpair(dcw_p), axis=1)
    dcb = jnp.concatenate(_unpair(dcb_p), axis=1)
    dmod = jnp.concatenate([dsh1, dsc1, dgt1, dsh2, dsc2, dgt2], axis=1)
    small = [dmod, dg_pre_mix, dg_post_mix, dg_pre_ffn, dg_post_ffn, dg_q, dg_kv, drel, dsink[:, :NH], dcb, dcw]
    shapes = [p.shape for p in small]
    small_all = _allgather8(_flat_pad(small), name="gather_small_grads")
    tot = _unflat(_sum_devices(small_all, name="sum_small_grads"), shapes)
    g_b_ada, g_pre_mix_g, g_post_mix_g, g_pre_ffn_g, g_post_ffn_g, g_q_g, g_kv_g, g_rel, g_sinks, g_cb, g_cw_full = tot
    dmod_all = small_all.reshape(N_DEV, -1)[:, :6 * D]
    g_w_ada = _ada_bwd(c_all.T, lax.dynamic_slice(dmod_all, (0, chip * n3), (N_DEV, n3)), name="ada_bwd")
    ncw = conv_w.shape[2]
    g_cw = lax.dynamic_slice(g_cw_full, (0, chip * ncw), (3, ncw))

    gs = [_to_col_shards(dw_in), _to_col_shards(dw_uq), _to_col_shards(dw_ukv), dw_o.reshape(N_CHIP, D // N_CHIP, D),
          _to_col_shards(dw_up), dw_down.reshape(N_CHIP, F // N_CHIP, D)]
    names = ["w_in", "w_uq", "w_ukv", "w_o", "w_up", "w_down"]
    recv = _send_other_halves(gs, name="grads_to_sibling")
    c_idx = jnp.reshape(ci, (1,)).astype(jnp.int32)
    part = [_add_sibling(g, r, c_idx, name="grad_add_sibling_" + nm) for g, r, nm in zip(gs, recv, names)]
    got = _scatter_to_chips(part, name="grads_to_chips")
    halves = [_sum_chips(p, c_idx, name="grad_sum_chips_" + nm) for p, nm in zip(got, names)]
    g_in, g_uq, g_ukv, g_o, g_up, g_down = _share_halves(halves, name="grads_share_halves")

    res = {}
    bigs = dict(w_ada=(w_ada[0], g_w_ada, m_w_ada[0], v_w_ada[0]), w_in=(w_in[0], g_in, m_w_in[0], v_w_in[0]),
                w_uq=(w_uq[0], g_uq, m_w_uq[0], v_w_uq[0]), w_ukv=(w_ukv[0], g_ukv, m_w_ukv[0], v_w_ukv[0]),
                w_o=(w_o[0], g_o, m_w_o[0], v_w_o[0]), w_up=(w_up[0], g_up, m_w_up[0], v_w_up[0]),
                w_down=(w_down[0], g_down, m_w_down[0], v_w_down[0]))
    for nm, (w, g, m, v) in bigs.items():
        res[nm] = (g,) + tuple(_adamw(w, g, m, v, name="adamw_" + nm))
    snames = ["b_ada", "g_pre_mix", "g_post_mix", "g_pre_ffn", "g_post_ffn", "g_q_lat", "g_kv_lat", "rel_bias", "sinks",
              "conv_b", "conv_w"]
    sw = [b_ada, g_pre_mix, g_post_mix, g_pre_ffn, g_post_ffn, g_q_lat, g_kv_lat, rel_bias, sinks, conv_b, conv_w]
    sm = [m_b_ada, m_g_pre_mix, m_g_post_mix, m_g_pre_ffn, m_g_post_ffn, m_g_q_lat, m_g_kv_lat, m_rel_bias, m_sinks,
          m_conv_b, m_conv_w]
    sv = [v_b_ada, v_g_pre_mix, v_g_post_mix, v_g_pre_ffn, v_g_post_ffn, v_g_q_lat, v_g_kv_lat, v_rel_bias, v_sinks,
          v_conv_b, v_conv_w]
    sg = [g_b_ada, g_pre_mix_g, g_post_mix_g, g_pre_ffn_g, g_post_ffn_g, g_q_g, g_kv_g, g_rel, g_sinks, g_cb, g_cw]
    sshapes = [w.shape for w in sw]
    sd, snm, snv = _adamw(_flat_pad(sw), _flat_pad(sg), _flat_pad(sm), _flat_pad(sv), name="adamw_small")
    sd, snm, snv = _unflat(sd, sshapes), _unflat(snm, sshapes), _unflat(snv, sshapes)
    for k, nm in enumerate(snames):
        res[nm] = (sg[k].reshape(sshapes[k]), sd[k], snm[k], snv[k])

    order = ["w_ada", "b_ada", "g_pre_mix", "g_post_mix", "w_in", "g_q_lat", "w_uq", "g_kv_lat", "w_ukv", "rel_bias", "sinks",
             "w_o", "g_pre_ffn", "g_post_ffn", "w_up", "conv_w", "conv_b", "w_down"]
    ref_shapes = dict(w_ada=w_ada.shape, w_in=w_in.shape, w_uq=w_uq.shape, w_ukv=w_ukv.shape, w_o=w_o.shape,
                      w_up=w_up.shape, w_down=w_down.shape)
    outs = []
    for k in range(4):
        for nm in order:
            arr = res[nm][k]
            outs.append(arr.reshape(ref_shapes[nm]) if nm in ref_shapes else arr)
    loss = lax.psum(loss_tile[0, 0], ("x", "y", "c"))
    return (loss, grad_x[None], *outs)
```

```python
import functools

import jax
import jax.numpy as jnp
from jax import lax
from jax.experimental import pallas as pl
from jax.experimental.pallas import tpu as pltpu

F32 = jnp.float32
BF16 = jnp.bfloat16
MESH = pl.DeviceIdType.MESH
HIGHEST = lax.Precision.HIGHEST

N_DEV = 8
N_CHIP = 4
LANES = 128
MLA_NOPE = 128
MLA_ROPE = 64
MLA_V = 128
MLA_QK = MLA_NOPE + MLA_ROPE
MLA_QK_PAD = 256
ROPE_THETA = 10000.0
SWA_HD = 64
SWA_KVH = 4
SWA_BLOCK = 128
REL_BUCKETS = 32
REL_MAX_DIST = 128
PAIR = 512
EPS = 1e-6
NEG = -1e30
ADAM_LR = 0.001
ADAM_B1 = 0.9
ADAM_B2 = 0.999
ADAM_EPS = 1e-08
ADAM_WD = 0.01
ADAM_STEP = 10

ANY = pl.BlockSpec(memory_space=pl.ANY)
VMEM_FULL = pl.BlockSpec(memory_space=pltpu.VMEM)
SMEM_FULL = pl.BlockSpec(memory_space=pltpu.SMEM)


def _params(*sem):
    return pltpu.CompilerParams(dimension_semantics=sem if sem else None)


def _tile(n, pref, unit=LANES):
    best = None
    for t in range(unit, min(n, pref) + 1, unit):
        if n % t == 0:
            best = t
    return n if best is None else best


def _matmul(a, b, *, ta=False, tb=False, out_dtype=F32, name):
    if ta:
        K, M = a.shape
    else:
        M, K = a.shape
    if tb:
        N, K2 = b.shape
    else:
        K2, N = b.shape
    assert K == K2, (a.shape, b.shape, ta, tb)
    exact = a.dtype == F32
    tm = _tile(M, 512, 8) if M >= 8 else M
    tn = _tile(N, 1536)
    tk = _tile(K, 2048)
    nk = K // tk
    dn = (((0 if ta else 1,), (1 if tb else 0,)), ((), ()))

    def product(a_ref, b_ref):
        return lax.dot_general(a_ref[...], b_ref[...], dn, preferred_element_type=F32,
                               precision=HIGHEST if exact else None)

    def body_acc(a_ref, b_ref, o_ref, acc_ref):
        k = pl.program_id(2)

        @pl.when(k == 0)
        def _():
            acc_ref[...] = product(a_ref, b_ref)

        @pl.when(jnp.logical_and(k > 0, k < nk - 1))
        def _():
            acc_ref[...] += product(a_ref, b_ref)

        @pl.when(k == nk - 1)
        def _():
            o_ref[...] = (acc_ref[...] + product(a_ref, b_ref)).astype(o_ref.dtype)

    def body_one(a_ref, b_ref, o_ref):
        o_ref[...] = product(a_ref, b_ref).astype(o_ref.dtype)

    body = body_one if nk == 1 else body_acc
    a_spec = pl.BlockSpec((tk, tm), lambda i, j, k: (k, i)) if ta else pl.BlockSpec((tm, tk), lambda i, j, k: (i, k))
    b_spec = pl.BlockSpec((tn, tk), lambda i, j, k: (j, k)) if tb else pl.BlockSpec((tk, tn), lambda i, j, k: (k, j))
    return pl.pallas_call(
        body, name=name,
        out_shape=jax.ShapeDtypeStruct((M, N), out_dtype),
        grid=(M // tm, N // tn, nk),
        in_specs=[a_spec, b_spec],
        out_specs=pl.BlockSpec((tm, tn), lambda i, j, k: (i, j)),
        scratch_shapes=[] if nk == 1 else [pltpu.VMEM((tm, tn), F32)],
        compiler_params=_params("parallel", "parallel", "arbitrary"),
    )(a, b)


def _row_tile(S, width):
    return _tile(S, max(8, (1 << 19) // width), 8)


def _rstd(x):
    return lax.rsqrt(jnp.mean(x * x, axis=-1, keepdims=True) + EPS)


def _acc_rows(ref, val, first):
    s = jnp.sum(val, axis=0, keepdims=True)

    @pl.when(first)
    def _():
        ref[...] = s

    @pl.when(jnp.logical_not(first))
    def _():
        ref[...] += s


def _modnorm_fwd(x, g, sc, sh, *, name):
    S, D = x.shape
    tr = _row_tile(S, D)

    def body(x_ref, g_ref, sc_ref, sh_ref, h_ref):
        xv = x_ref[...]
        n = (xv * _rstd(xv)) * g_ref[...]
        h_ref[...] = (n * (1.0 + sc_ref[...]) + sh_ref[...]).astype(BF16)

    row = pl.BlockSpec((tr, D), lambda i: (i, 0))
    vec = pl.BlockSpec((1, D), lambda i: (0, 0))
    return pl.pallas_call(
        body, name=name, out_shape=jax.ShapeDtypeStruct((S, D), BF16), grid=(S // tr,),
        in_specs=[row, vec, vec, vec], out_specs=row, compiler_params=_params("parallel"),
    )(x, g, sc, sh)


def _modnorm_bwd(dh, x, g, sc, dres, *, name):
    S, D = x.shape
    tr = _row_tile(S, D)

    def body(dh_ref, x_ref, g_ref, sc_ref, dres_ref, dx_ref, dg_ref, dsc_ref, dsh_ref):
        first = pl.program_id(0) == 0
        xv = x_ref[...]
        dhv = dh_ref[...]
        gv = g_ref[...]
        r = _rstd(xv)
        xhat = xv * r
        _acc_rows(dsh_ref, dhv, first)
        _acc_rows(dsc_ref, dhv * (xhat * gv), first)
        dn = dhv * (1.0 + sc_ref[...])
        _acc_rows(dg_ref, dn * xhat, first)
        dxhat = dn * gv
        proj = jnp.mean(dxhat * xhat, axis=-1, keepdims=True)
        dx_ref[...] = r * (dxhat - xhat * proj) + dres_ref[...]

    row = pl.BlockSpec((tr, D), lambda i: (i, 0))
    vec = pl.BlockSpec((1, D), lambda i: (0, 0))
    vshape = jax.ShapeDtypeStruct((1, D), F32)
    return pl.pallas_call(
        body, name=name,
        out_shape=(jax.ShapeDtypeStruct((S, D), F32), vshape, vshape, vshape), grid=(S // tr,),
        in_specs=[row, row, vec, vec, row], out_specs=(row, vec, vec, vec),
        compiler_params=_params("arbitrary"),
    )(dh, x, g, sc, dres)


def _resnorm_fwd(xres, m, g, gt, *, name):
    S, D = xres.shape
    tr = _row_tile(S, D)

    def body(x_ref, m_ref, g_ref, gt_ref, o_ref):
        mv = m_ref[...]
        o_ref[...] = x_ref[...] + gt_ref[...] * ((mv * _rstd(mv)) * g_ref[...])

    row = pl.BlockSpec((tr, D), lambda i: (i, 0))
    vec = pl.BlockSpec((1, D), lambda i: (0, 0))
    return pl.pallas_call(
        body, name=name, out_shape=jax.ShapeDtypeStruct((S, D), F32), grid=(S // tr,),
        in_specs=[row, row, vec, vec], out_specs=row, compiler_params=_params("parallel"),
    )(xres, m, g, gt)


def _resnorm_loss(xres, m, g, gt, target, *, name):
    S, D = xres.shape
    tr = _row_tile(S, D)

    def body(x_ref, m_ref, g_ref, gt_ref, t_ref, d_ref, loss_ref):
        mv = m_ref[...]
        out = x_ref[...] + gt_ref[...] * ((mv * _rstd(mv)) * g_ref[...])
        err = out - t_ref[...]
        d_ref[...] = err * (1.0 / D)
        part = 0.5 * jnp.sum(jnp.mean(err * err, axis=-1, keepdims=True), axis=0, keepdims=True)
        part = jnp.broadcast_to(part, loss_ref.shape)

        @pl.when(pl.program_id(0) == 0)
        def _():
            loss_ref[...] = part

        @pl.when(pl.program_id(0) != 0)
        def _():
            loss_ref[...] += part

    row = pl.BlockSpec((tr, D), lambda i: (i, 0))
    vec = pl.BlockSpec((1, D), lambda i: (0, 0))
    return pl.pallas_call(
        body, name=name,
        out_shape=(jax.ShapeDtypeStruct((S, D), F32), jax.ShapeDtypeStruct((8, LANES), F32)), grid=(S // tr,),
        in_specs=[row, row, vec, vec, row], out_specs=(row, pl.BlockSpec((8, LANES), lambda i: (0, 0))),
        compiler_params=_params("arbitrary"),
    )(xres, m, g, gt, target)


def _resnorm_bwd(dout, m, g, gt, *, name):
    S, D = m.shape
    tr = _row_tile(S, D)

    def body(d_ref, m_ref, g_ref, gt_ref, dm_ref, dg_ref, dgt_ref):
        first = pl.program_id(0) == 0
        mv = m_ref[...]
        dv = d_ref[...]
        gv = g_ref[...]
        r = _rstd(mv)
        mhat = mv * r
        _acc_rows(dgt_ref, dv * (mhat * gv), first)
        dn = dv * gt_ref[...]
        _acc_rows(dg_ref, dn * mhat, first)
        dmhat = dn * gv
        proj = jnp.mean(dmhat * mhat, axis=-1, keepdims=True)
        dm_ref[...] = (r * (dmhat - mhat * proj)).astype(BF16)

    row = pl.BlockSpec((tr, D), lambda i: (i, 0))
    vec = pl.BlockSpec((1, D), lambda i: (0, 0))
    vshape = jax.ShapeDtypeStruct((1, D), F32)
    return pl.pallas_call(
        body, name=name, out_shape=(jax.ShapeDtypeStruct((S, D), BF16), vshape, vshape), grid=(S // tr,),
        in_specs=[row, row, vec, vec], out_specs=(row, vec, vec), compiler_params=_params("arbitrary"),
    )(dout, m, g, gt)


def _lat_norm_fwd(z_lat, g_q, g_kv, *, name):
    S, W = z_lat.shape
    Rq, Rkv = g_q.shape[1], g_kv.shape[1]
    tr = _row_tile(S, W)

    def body(z_ref, gq_ref, gkv_ref, nq_ref, nkv_ref):
        cq = z_ref[:, :Rq]
        ckv = z_ref[:, Rq:Rq + Rkv]
        nq_ref[...] = ((cq * _rstd(cq)) * gq_ref[...]).astype(BF16)
        nkv_ref[...] = ((ckv * _rstd(ckv)) * gkv_ref[...]).astype(BF16)

    return pl.pallas_call(
        body, name=name,
        out_shape=(jax.ShapeDtypeStruct((S, Rq), BF16), jax.ShapeDtypeStruct((S, Rkv), BF16)), grid=(S // tr,),
        in_specs=[pl.BlockSpec((tr, W), lambda i: (i, 0)), pl.BlockSpec((1, Rq), lambda i: (0, 0)),
                  pl.BlockSpec((1, Rkv), lambda i: (0, 0))],
        out_specs=(pl.BlockSpec((tr, Rq), lambda i: (i, 0)), pl.BlockSpec((tr, Rkv), lambda i: (i, 0))),
        compiler_params=_params("parallel"),
    )(z_lat, g_q, g_kv)


def _lat_norm_bwd(z_lat, dnq, dnkv, dkr, g_q, g_kv, *, name):
    S, W = z_lat.shape
    Rq, Rkv = g_q.shape[1], g_kv.shape[1]
    tr = _row_tile(S, W)

    def one(c, dn, gv):
        r = _rstd(c)
        chat = c * r
        dchat = dn * gv
        proj = jnp.mean(dchat * chat, axis=-1, keepdims=True)
        return r * (dchat - chat * proj), dn * chat

    def body(z_ref, dnq_ref, dnkv_ref, dkr_ref, gq_ref, gkv_ref, dz_ref, dgq_ref, dgkv_ref):
        first = pl.program_id(0) == 0
        dcq, pq = one(z_ref[:, :Rq], dnq_ref[...], gq_ref[...])
        dckv, pkv = one(z_ref[:, Rq:Rq + Rkv], dnkv_ref[...], gkv_ref[...])
        _acc_rows(dgq_ref, pq, first)
        _acc_rows(dgkv_ref, pkv, first)
        dz_ref[:, :Rq] = dcq.astype(BF16)
        dz_ref[:, Rq:Rq + Rkv] = dckv.astype(BF16)
        dz_ref[:, Rq + Rkv:] = dkr_ref[...].astype(BF16)

    return pl.pallas_call(
        body, name=name,
        out_shape=(jax.ShapeDtypeStruct((S, W), BF16), jax.ShapeDtypeStruct((1, Rq), F32),
                   jax.ShapeDtypeStruct((1, Rkv), F32)), grid=(S // tr,),
        in_specs=[pl.BlockSpec((tr, W), lambda i: (i, 0)), pl.BlockSpec((tr, Rq), lambda i: (i, 0)),
                  pl.BlockSpec((tr, Rkv), lambda i: (i, 0)), pl.BlockSpec((tr, LANES), lambda i: (i, 0)),
                  pl.BlockSpec((1, Rq), lambda i: (0, 0)), pl.BlockSpec((1, Rkv), lambda i: (0, 0))],
        out_specs=(pl.BlockSpec((tr, W), lambda i: (i, 0)), pl.BlockSpec((1, Rq), lambda i: (0, 0)),
                   pl.BlockSpec((1, Rkv), lambda i: (0, 0))),
        compiler_params=_params("arbitrary"),
    )(z_lat, dnq, dnkv, dkr, g_q, g_kv)


def _rot(x, lo32):
    a = pltpu.roll(x, 32, 1)
    b = pltpu.roll(x, LANES - 32, 1)
    return jnp.where(lo32, -b, a)


def _rot_t(g, lo32):
    a = pltpu.roll(g, 32, 1)
    b = pltpu.roll(g, LANES - 32, 1)
    return jnp.where(lo32, b, -a)


def _mla_pack_fwd(q_raw, kv_raw, z_lat, cos, sin, kr_off, *, name):
    S = q_raw.shape[0]
    H = kv_raw.shape[1] // (MLA_NOPE + MLA_V)
    W = z_lat.shape[1]
    scale = MLA_QK ** -0.5
    tr = min(S, 128)
    nope_w = H * MLA_NOPE

    def body(q_ref, kv_ref, z_ref, cos_ref, sin_ref, qp_ref, kp_ref, v_ref):
        lane = lax.broadcasted_iota(jnp.int32, (tr, LANES), 1)
        lo32 = (lane % 64) < 32
        lo64 = lane < 64
        c = cos_ref[...]
        s = sin_ref[...]
        kr = z_ref[:, kr_off:kr_off + LANES]
        kr = (kr * c + _rot(kr, lo32) * s).astype(BF16)
        for hp in range(H // 2):
            xb = q_ref[:, nope_w + hp * LANES:nope_w + (hp + 1) * LANES]
            rb = (xb * c + _rot(xb, lo32) * s) * scale
            for e in range(2):
                h = 2 * hp + e
                base = h * MLA_QK_PAD
                qp_ref[:, base:base + LANES] = (q_ref[:, h * LANES:(h + 1) * LANES] * scale).astype(BF16)
                keep = lo64 if e == 0 else jnp.logical_not(lo64)
                qp_ref[:, base + LANES:base + 2 * LANES] = jnp.where(keep, rb, 0.0).astype(BF16)
                kp_ref[:, base:base + LANES] = kv_ref[:, h * LANES:(h + 1) * LANES].astype(BF16)
                kp_ref[:, base + LANES:base + 2 * LANES] = kr
        v_ref[...] = kv_ref[:, nope_w:].astype(BF16)

    return pl.pallas_call(
        body, name=name,
        out_shape=(jax.ShapeDtypeStruct((S, H * MLA_QK_PAD), BF16), jax.ShapeDtypeStruct((S, H * MLA_QK_PAD), BF16),
                   jax.ShapeDtypeStruct((S, H * MLA_V), BF16)), grid=(S // tr,),
        in_specs=[pl.BlockSpec((tr, q_raw.shape[1]), lambda i: (i, 0)), pl.BlockSpec((tr, kv_raw.shape[1]), lambda i: (i, 0)),
                  pl.BlockSpec((tr, W), lambda i: (i, 0)), pl.BlockSpec((tr, LANES), lambda i: (i, 0)),
                  pl.BlockSpec((tr, LANES), lambda i: (i, 0))],
        out_specs=(pl.BlockSpec((tr, H * MLA_QK_PAD), lambda i: (i, 0)), pl.BlockSpec((tr, H * MLA_QK_PAD), lambda i: (i, 0)),
                   pl.BlockSpec((tr, H * MLA_V), lambda i: (i, 0))),
        compiler_params=_params("parallel"),
    )(q_raw, kv_raw, z_lat, cos, sin)


def _mla_pack_bwd(dqp, dkp, dv, cos, sin, *, name):
    S = dqp.shape[0]
    H = dv.shape[1] // MLA_V
    scale = MLA_QK ** -0.5
    tr = min(S, 128)
    nope_w = H * MLA_NOPE

    def body(dqp_ref, dkp_ref, dv_ref, cos_ref, sin_ref, dq_ref, dkv_ref, dkr_ref):
        lane = lax.broadcasted_iota(jnp.int32, (tr, LANES), 1)
        lo32 = (lane % 64) < 32
        lo64 = lane < 64
        c = cos_ref[...]
        s = sin_ref[...]
        dkr2 = jnp.zeros((tr, LANES), F32)
        for hp in range(H // 2):
            be = (2 * hp) * MLA_QK_PAD
            bo = (2 * hp + 1) * MLA_QK_PAD
            g = jnp.where(lo64, dqp_ref[:, be + LANES:be + 2 * LANES], dqp_ref[:, bo + LANES:bo + 2 * LANES]) * scale
            dq_ref[:, nope_w + hp * LANES:nope_w + (hp + 1) * LANES] = (g * c + _rot_t(g * s, lo32)).astype(BF16)
            for h, base in ((2 * hp, be), (2 * hp + 1, bo)):
                dq_ref[:, h * LANES:(h + 1) * LANES] = (dqp_ref[:, base:base + LANES] * scale).astype(BF16)
                dkv_ref[:, h * LANES:(h + 1) * LANES] = dkp_ref[:, base:base + LANES].astype(BF16)
                dkr2 = dkr2 + dkp_ref[:, base + LANES:base + 2 * LANES]
        dkr2 = dkr2 * c + _rot_t(dkr2 * s, lo32)
        dkr2 = dkr2 + pltpu.roll(dkr2, 64, 1)
        dkr_ref[...] = jnp.where(lo64, dkr2, 0.0)
        dkv_ref[:, nope_w:] = dv_ref[...].astype(BF16)

    return pl.pallas_call(
        body, name=name,
        out_shape=(jax.ShapeDtypeStruct((S, nope_w + H * MLA_ROPE), BF16), jax.ShapeDtypeStruct((S, 2 * nope_w), BF16),
                   jax.ShapeDtypeStruct((S, LANES), F32)), grid=(S // tr,),
        in_specs=[pl.BlockSpec((tr, H * MLA_QK_PAD), lambda i: (i, 0)), pl.BlockSpec((tr, H * MLA_QK_PAD), lambda i: (i, 0)),
                  pl.BlockSpec((tr, H * MLA_V), lambda i: (i, 0)), pl.BlockSpec((tr, LANES), lambda i: (i, 0)),
                  pl.BlockSpec((tr, LANES), lambda i: (i, 0))],
        out_specs=(pl.BlockSpec((tr, nope_w + H * MLA_ROPE), lambda i: (i, 0)), pl.BlockSpec((tr, 2 * nope_w), lambda i: (i, 0)),
                   pl.BlockSpec((tr, LANES), lambda i: (i, 0))),
        compiler_params=_params("parallel"),
    )(dqp, dkp, dv, cos, sin)


FLASH_HB = 2


def _causal_pairs(nb):
    qi = [i for i in range(nb) for j in range(i + 1)]
    kj = [j for i in range(nb) for j in range(i + 1)]
    return jnp.asarray(qi, jnp.int32), jnp.asarray(kj, jnp.int32)


def _scores(q, k, diagonal, t):
    s = lax.dot_general(q, k, (((1,), (1,)), ((), ())), preferred_element_type=F32)
    if diagonal:
        row = lax.broadcasted_iota(jnp.int32, (t, t), 0)
        col = lax.broadcasted_iota(jnp.int32, (t, t), 1)
        s = jnp.where(col <= row, s, NEG)
    return s


def _flash_fwd(qp, kp, v, *, name):
    S = qp.shape[0]
    H = v.shape[1] // MLA_V
    t = min(S, 512)
    nb = S // t
    HB = FLASH_HB
    qi, kj = _causal_pairs(nb)
    QW, VW = MLA_QK_PAD, MLA_V

    def body(qi_ref, kj_ref, q_ref, k_ref, v_ref, o_ref, lse_ref, m_s, l_s, acc_s):
        pr = pl.program_id(1)
        i = qi_ref[pr]
        j = kj_ref[pr]

        @pl.when(j == 0)
        def _():
            m_s[...] = jnp.full_like(m_s, NEG)
            l_s[...] = jnp.zeros_like(l_s)
            acc_s[...] = jnp.zeros_like(acc_s)

        def step(diagonal):
            for hh in range(HB):
                s = _scores(q_ref[:, hh * QW:(hh + 1) * QW], k_ref[:, hh * QW:(hh + 1) * QW], diagonal, t)
                m_prev = m_s[hh]
                m_cur = jnp.maximum(m_prev, jnp.max(s, axis=1, keepdims=True))
                alpha = jnp.exp(m_prev - m_cur)
                p = jnp.exp(s - m_cur[:, :1])
                l_new = alpha * l_s[hh] + jnp.sum(p, axis=1, keepdims=True)
                acc = alpha * acc_s[hh] + jnp.dot(p.astype(BF16), v_ref[:, hh * VW:(hh + 1) * VW], preferred_element_type=F32)
                if diagonal:
                    o_ref[:, hh * VW:(hh + 1) * VW] = acc / l_new
                    lse_ref[hh] = m_cur + jnp.log(l_new)
                else:
                    l_s[hh] = l_new
                    acc_s[hh] = acc
                    m_s[hh] = m_cur

        @pl.when(i != j)
        def _():
            step(False)

        @pl.when(i == j)
        def _():
            step(True)

    return pl.pallas_call(
        body, name=name,
        out_shape=(jax.ShapeDtypeStruct((S, H * VW), F32), jax.ShapeDtypeStruct((H, S, LANES), F32)),
        grid_spec=pltpu.PrefetchScalarGridSpec(
            num_scalar_prefetch=2, grid=(H // HB, qi.shape[0]),
            in_specs=[pl.BlockSpec((t, HB * QW), lambda g, p, qi, kj: (qi[p], g)),
                      pl.BlockSpec((t, HB * QW), lambda g, p, qi, kj: (kj[p], g)),
                      pl.BlockSpec((t, HB * VW), lambda g, p, qi, kj: (kj[p], g))],
            out_specs=(pl.BlockSpec((t, HB * VW), lambda g, p, qi, kj: (qi[p], g)),
                       pl.BlockSpec((HB, t, LANES), lambda g, p, qi, kj: (g, qi[p], 0))),
            scratch_shapes=[pltpu.VMEM((HB, t, LANES), F32), pltpu.VMEM((HB, t, LANES), F32), pltpu.VMEM((HB, t, VW), F32)]),
        compiler_params=_params("parallel", "arbitrary"),
    )(qi, kj, qp, kp, v)


def _flash_bwd(qp, kp, v, o, do, lse, *, name):
    S = qp.shape[0]
    H = v.shape[1] // MLA_V
    t = min(S, 512)
    nb = S // t
    HB = FLASH_HB
    qi, kj = _causal_pairs(nb)
    QW, VW = MLA_QK_PAD, MLA_V
    tn = (((0,), (0,)), ((), ()))
    nt = (((1,), (1,)), ((), ()))

    def body(qi_ref, kj_ref, q_ref, k_ref, v_ref, o_ref, do_ref, lse_ref, dq_ref, dk_ref, dv_ref, dq_s):
        pr = pl.program_id(1)
        i = qi_ref[pr]
        j = kj_ref[pr]
        rows = pl.ds(pl.multiple_of(j * t, t), t)

        @pl.when(pr == 0)
        def _():
            dk_ref[...] = jnp.zeros_like(dk_ref)
            dv_ref[...] = jnp.zeros_like(dv_ref)

        @pl.when(j == 0)
        def _():
            dq_s[...] = jnp.zeros_like(dq_s)

        def step(diagonal):
            for hh in range(HB):
                q = q_ref[:, hh * QW:(hh + 1) * QW]
                k = k_ref[:, hh * QW:(hh + 1) * QW]
                dob = do_ref[:, hh * VW:(hh + 1) * VW]
                p = jnp.exp(_scores(q, k, diagonal, t) - lse_ref[hh][:, :1])
                delta = jnp.sum(dob.astype(F32) * o_ref[:, hh * VW:(hh + 1) * VW], axis=1, keepdims=True)
                dp = lax.dot_general(dob, v_ref[:, hh * VW:(hh + 1) * VW], nt, preferred_element_type=F32)
                dsb = (p * (dp - delta)).astype(BF16)
                dv_ref[rows, hh * VW:(hh + 1) * VW] += lax.dot_general(p.astype(BF16), dob, tn, preferred_element_type=F32)
                dk_ref[rows, hh * QW:(hh + 1) * QW] += lax.dot_general(dsb, q, tn, preferred_element_type=F32)
                dq = dq_s[:, hh * QW:(hh + 1) * QW] + jnp.dot(dsb, k, preferred_element_type=F32)
                if diagonal:
                    dq_ref[:, hh * QW:(hh + 1) * QW] = dq
                else:
                    dq_s[:, hh * QW:(hh + 1) * QW] = dq

        @pl.when(i != j)
        def _():
            step(False)

        @pl.when(i == j)
        def _():
            step(True)

    qside = lambda g, p, qi, kj: (qi[p], g)
    kside = lambda g, p, qi, kj: (kj[p], g)
    whole = lambda g, p, qi, kj: (0, g)
    return pl.pallas_call(
        body, name=name,
        out_shape=(jax.ShapeDtypeStruct((S, H * QW), F32), jax.ShapeDtypeStruct((S, H * QW), F32),
                   jax.ShapeDtypeStruct((S, H * VW), F32)),
        grid_spec=pltpu.PrefetchScalarGridSpec(
            num_scalar_prefetch=2, grid=(H // HB, qi.shape[0]),
            in_specs=[pl.BlockSpec((t, HB * QW), qside), pl.BlockSpec((t, HB * QW), kside), pl.BlockSpec((t, HB * VW), kside),
                      pl.BlockSpec((t, HB * VW), qside), pl.BlockSpec((t, HB * VW), qside),
                      pl.BlockSpec((HB, t, LANES), lambda g, p, qi, kj: (g, qi[p], 0))],
            out_specs=(pl.BlockSpec((t, HB * QW), qside), pl.BlockSpec((S, HB * QW), whole), pl.BlockSpec((S, HB * VW), whole)),
            scratch_shapes=[pltpu.VMEM((t, HB * QW), F32)]),
        compiler_params=_params("parallel", "arbitrary"),
    )(qi, kj, qp, kp, v, o, do, lse)


def _swa_kv_halves(blk, hf, lo):
    if hf == 0:
        a = jnp.where(lo, blk, 0.0)
        b = pltpu.roll(a, 64, 1)
    else:
        b = jnp.where(lo, 0.0, blk)
        a = pltpu.roll(b, 64, 1)
    return a.astype(BF16), b.astype(BF16)


def _swa_softmax(qb, kx, bias, neg0, sk):
    s = lax.dot_general(qb, kx, (((1,), (1,)), ((), ())), preferred_element_type=F32) + bias + neg0
    m = jnp.maximum(jnp.max(s, axis=1, keepdims=True), sk)
    e = jnp.exp(s - m)
    es = jnp.exp(sk - m)
    inv = 1.0 / (jnp.sum(e, axis=1, keepdims=True) + es)
    return e * inv, es * inv


def _swa_fwd(z_swa, bias_m, sinks, *, name):
    S, W = z_swa.shape
    NH = bias_m.shape[0]
    G = NH // SWA_KVH
    QW = NH * SWA_HD
    KW = SWA_KVH * SWA_HD
    nb = S // SWA_BLOCK
    B = SWA_BLOCK
    assert G % 2 == 0 and SWA_KVH % 2 == 0 and W == QW + 2 * KW

    def body(sink_ref, q_ref, kvc_ref, kvp_ref, b_ref, o_ref):
        n = pl.program_id(0)
        lo = lax.broadcasted_iota(jnp.int32, (2 * B, LANES), 1) < 64
        col = lax.broadcasted_iota(jnp.int32, (B, 2 * B), 1)
        neg0 = jnp.where(jnp.logical_and(col < B, n == 0), NEG, 0.0)
        for kb in range(SWA_KVH // 2):
            kblk = jnp.concatenate([kvp_ref[:, kb * LANES:(kb + 1) * LANES], kvc_ref[:, kb * LANES:(kb + 1) * LANES]], axis=0)
            vblk = jnp.concatenate([kvp_ref[:, KW + kb * LANES:KW + (kb + 1) * LANES],
                                    kvc_ref[:, KW + kb * LANES:KW + (kb + 1) * LANES]], axis=0)
            for hf in range(2):
                kvh = 2 * kb + hf
                ks = _swa_kv_halves(kblk, hf, lo)
                vs = _swa_kv_halves(vblk, hf, lo)
                for pb in range(G // 2):
                    P = kvh * (G // 2) + pb
                    qb = (q_ref[:, P * LANES:(P + 1) * LANES] * (SWA_HD ** -0.5)).astype(BF16)
                    acc = jnp.zeros((B, LANES), F32)
                    for e in range(2):
                        h = 2 * P + e
                        p, _ = _swa_softmax(qb, ks[e], b_ref[h], neg0, sink_ref[h])
                        acc = acc + jnp.dot(p.astype(BF16), vs[e], preferred_element_type=F32)
                    o_ref[:, P * LANES:(P + 1) * LANES] = acc

    kvcol = QW // (2 * KW)
    assert QW % (2 * KW) == 0
    return pl.pallas_call(
        body, name=name,
        out_shape=jax.ShapeDtypeStruct((S, QW), F32), grid=(nb,),
        in_specs=[SMEM_FULL, pl.BlockSpec((B, QW), lambda n: (n, 0)), pl.BlockSpec((B, 2 * KW), lambda n: (n, kvcol)),
                  pl.BlockSpec((B, 2 * KW), lambda n: (jnp.maximum(n - 1, 0), kvcol)),
                  pl.BlockSpec((NH, B, 2 * B), lambda n: (0, 0, 0))],
        out_specs=pl.BlockSpec((B, QW), lambda n: (n, 0)),
        compiler_params=_params("parallel"),
    )(sinks, z_swa, z_swa, z_swa, bias_m)


def _swa_bwd(z_swa, bias_m, sinks, o, do, *, name):
    S, W = z_swa.shape
    NH = bias_m.shape[0]
    G = NH // SWA_KVH
    QW = NH * SWA_HD
    KW = SWA_KVH * SWA_HD
    nb = S // SWA_BLOCK
    B = SWA_BLOCK
    scale = SWA_HD ** -0.5
    tn = (((0,), (0,)), ((), ()))
    nt = (((1,), (1,)), ((), ()))

    def fold(x, hf, lo):
        x = x + pltpu.roll(x, 64, 1)
        return jnp.where(lo, x, 0.0) if hf == 0 else jnp.where(lo, 0.0, x)

    def body(sink_ref, q_ref, kvc_ref, kvp_ref, b_ref, o_ref, do_ref, dz_ref, dbias_ref, dsink_ref,
             cq_s, ck_s, cv_s, nq_s, nk_s, nv_s, pk_s, pv_s):
        n = pl.program_id(0)

        @pl.when(n == 0)
        def _():
            dbias_ref[...] = jnp.zeros_like(dbias_ref)
            dsink_ref[...] = jnp.zeros_like(dsink_ref)
            cq_s[...] = jnp.zeros_like(cq_s)
            ck_s[...] = jnp.zeros_like(ck_s)
            cv_s[...] = jnp.zeros_like(cv_s)

        @pl.when(n == nb)
        def _():
            pk_s[...] = jnp.zeros_like(pk_s)
            pv_s[...] = jnp.zeros_like(pv_s)

        @pl.when(n < nb)
        def _():
            lo = lax.broadcasted_iota(jnp.int32, (2 * B, LANES), 1) < 64
            lo1 = lax.broadcasted_iota(jnp.int32, (B, LANES), 1) < 64
            lane1 = lax.broadcasted_iota(jnp.int32, (1, LANES), 1)
            col = lax.broadcasted_iota(jnp.int32, (B, 2 * B), 1)
            neg0 = jnp.where(jnp.logical_and(col < B, n == 0), NEG, 0.0)
            dsink = jnp.zeros((1, LANES), F32)
            for kb in range(SWA_KVH // 2):
                kblk = jnp.concatenate([kvp_ref[:, kb * LANES:(kb + 1) * LANES], kvc_ref[:, kb * LANES:(kb + 1) * LANES]], axis=0)
                vblk = jnp.concatenate([kvp_ref[:, KW + kb * LANES:KW + (kb + 1) * LANES],
                                        kvc_ref[:, KW + kb * LANES:KW + (kb + 1) * LANES]], axis=0)
                dkblk = jnp.zeros((2 * B, LANES), F32)
                dvblk = jnp.zeros((2 * B, LANES), F32)
                for hf in range(2):
                    kvh = 2 * kb + hf
                    ks = _swa_kv_halves(kblk, hf, lo)
                    vs = _swa_kv_halves(vblk, hf, lo)
                    dkj = jnp.zeros((2 * B, LANES), F32)
                    dvj = jnp.zeros((2 * B, LANES), F32)
                    for pb in range(G // 2):
                        P = kvh * (G // 2) + pb
                        qb = (q_ref[:, P * LANES:(P + 1) * LANES] * scale).astype(BF16)
                        dob = do_ref[:, P * LANES:(P + 1) * LANES]
                        prod = dob * o_ref[:, P * LANES:(P + 1) * LANES]
                        dob = dob.astype(BF16)
                        dqp = jnp.zeros((B, LANES), F32)
                        for e in range(2):
                            h = 2 * P + e
                            keep = lo1 if e == 0 else jnp.logical_not(lo1)
                            p, psink = _swa_softmax(qb, ks[e], b_ref[h], neg0, sink_ref[h])
                            delta = jnp.sum(jnp.where(keep, prod, 0.0), axis=1, keepdims=True)
                            dp = lax.dot_general(dob, vs[e], nt, preferred_element_type=F32)
                            ds = p * (dp - delta)
                            dbias_ref[h] += ds
                            dsh = -jnp.sum(psink * delta, axis=0, keepdims=True)
                            dsink = dsink + jnp.where(lane1 == h, dsh, 0.0)
                            dsb = ds.astype(BF16)
                            dqp = dqp + jnp.dot(dsb, ks[e], preferred_element_type=F32)
                            keep2 = lo if e == 0 else jnp.logical_not(lo)
                            dkj = dkj + jnp.where(keep2, lax.dot_general(dsb, qb, tn, preferred_element_type=F32), 0.0)
                            dvj = dvj + jnp.where(keep2, lax.dot_general(p.astype(BF16), dob, tn, preferred_element_type=F32), 0.0)
                        nq_s[:, P * LANES:(P + 1) * LANES] = dqp * scale
                    dkblk = dkblk + fold(dkj, hf, lo)
                    dvblk = dvblk + fold(dvj, hf, lo)
                pk_s[:, kb * LANES:(kb + 1) * LANES] = dkblk[:B]
                nk_s[:, kb * LANES:(kb + 1) * LANES] = dkblk[B:]
                pv_s[:, kb * LANES:(kb + 1) * LANES] = dvblk[:B]
                nv_s[:, kb * LANES:(kb + 1) * LANES] = dvblk[B:]
            dsink_ref[...] += dsink

        dz_ref[:, :QW] = cq_s[...].astype(BF16)
        dz_ref[:, QW:QW + KW] = (ck_s[...] + pk_s[...]).astype(BF16)
        dz_ref[:, QW + KW:] = (cv_s[...] + pv_s[...]).astype(BF16)

        @pl.when(n < nb)
        def _():
            cq_s[...] = nq_s[...]
            ck_s[...] = nk_s[...]
            cv_s[...] = nv_s[...]

    kvcol = QW // (2 * KW)
    cur = lambda n: (jnp.minimum(n, nb - 1), 0)
    return pl.pallas_call(
        body, name=name,
        out_shape=(jax.ShapeDtypeStruct((S, W), BF16), jax.ShapeDtypeStruct((NH, B, 2 * B), F32),
                   jax.ShapeDtypeStruct((1, LANES), F32)),
        grid=(nb + 1,),
        in_specs=[SMEM_FULL, pl.BlockSpec((B, QW), cur), pl.BlockSpec((B, 2 * KW), lambda n: (jnp.minimum(n, nb - 1), kvcol)),
                  pl.BlockSpec((B, 2 * KW), lambda n: (jnp.maximum(jnp.minimum(n, nb - 1) - 1, 0), kvcol)),
                  pl.BlockSpec((NH, B, 2 * B), lambda n: (0, 0, 0)), pl.BlockSpec((B, QW), cur), pl.BlockSpec((B, QW), cur)],
        out_specs=(pl.BlockSpec((B, W), lambda n: (jnp.maximum(n - 1, 0), 0)),
                   pl.BlockSpec((NH, B, 2 * B), lambda n: (0, 0, 0)), pl.BlockSpec((1, LANES), lambda n: (0, 0))),
        scratch_shapes=[pltpu.VMEM((B, QW), F32), pltpu.VMEM((B, KW), F32), pltpu.VMEM((B, KW), F32),
                        pltpu.VMEM((B, QW), F32), pltpu.VMEM((B, KW), F32), pltpu.VMEM((B, KW), F32),
                        pltpu.VMEM((B, KW), F32), pltpu.VMEM((B, KW), F32)],
        compiler_params=_params("arbitrary"),
    )(sinks, z_swa, z_swa, z_swa, bias_m, o, do)


def _gate_fwd(zg, o_a, o_b, *, name):
    S, D = o_a.shape
    tr = min(S, 512)

    def body(z_ref, a_ref, b_ref, m_ref):
        ga = jax.nn.sigmoid(z_ref[:, :PAIR])
        gb = jax.nn.sigmoid(z_ref[:, PAIR:])
        m_ref[...] = (ga * a_ref[...] + gb * b_ref[...]).astype(BF16)

    col = pl.BlockSpec((tr, PAIR), lambda i, j: (i, j))
    return pl.pallas_call(
        body, name=name, out_shape=jax.ShapeDtypeStruct((S, D), BF16), grid=(S // tr, D // PAIR),
        in_specs=[pl.BlockSpec((tr, 2 * PAIR), lambda i, j: (i, j)), col, col], out_specs=col,
        compiler_params=_params("parallel", "parallel"),
    )(zg, o_a, o_b)


def _gate_bwd(dmix, zg, o_a, o_b, *, name):
    S, D = o_a.shape
    tr = min(S, 512)

    def body(d_ref, z_ref, a_ref, b_ref, da_ref, db_ref, dz_ref):
        d = d_ref[...]
        ga = jax.nn.sigmoid(z_ref[:, :PAIR])
        gb = jax.nn.sigmoid(z_ref[:, PAIR:])
        da_ref[...] = (d * ga).astype(BF16)
        db_ref[...] = d * gb
        dz_ref[:, :PAIR] = (d * a_ref[...] * (ga * (1.0 - ga))).astype(BF16)
        dz_ref[:, PAIR:] = (d * b_ref[...] * (gb * (1.0 - gb))).astype(BF16)

    col = pl.BlockSpec((tr, PAIR), lambda i, j: (i, j))
    wide = pl.BlockSpec((tr, 2 * PAIR), lambda i, j: (i, j))
    return pl.pallas_call(
        body, name=name,
        out_shape=(jax.ShapeDtypeStruct((S, D), BF16), jax.ShapeDtypeStruct((S, D), F32), jax.ShapeDtypeStruct((S, 2 * D), BF16)),
        grid=(S // tr, D // PAIR), in_specs=[col, wide, col, col], out_specs=(col, col, wide),
        compiler_params=_params("parallel", "parallel"),
    )(dmix, zg, o_a, o_b)


def _conv_taps(t_ref, prev_ref, i, tr):
    cur = t_ref[...]
    live = (i > 0).astype(F32)
    p6 = prev_ref[6:7, :] * live
    p7 = prev_ref[7:8, :] * live
    row = lax.broadcasted_iota(jnp.int32, cur.shape, 0)
    t1 = jnp.where(row == 0, p7, pltpu.roll(cur, 1, 0))
    t2 = jnp.where(row == 0, p6, jnp.where(row == 1, p7, pltpu.roll(cur, 2, 0)))
    return cur, t1, t2


def _conv_u(t_ref, prev_ref, w_ref, b_ref, i, tr):
    cur, t1, t2 = _conv_taps(t_ref, prev_ref, i, tr)
    u = ((b_ref[...] + w_ref[0:1, :] * t2) + w_ref[1:2, :] * t1) + w_ref[2:3, :] * cur
    return u, cur, t1, t2


def _conv_specs(tr, S):
    blk = pl.BlockSpec((tr, 2 * PAIR), lambda j, i: (i, j))
    prev = pl.BlockSpec((8, 2 * PAIR), lambda j, i: (jnp.maximum(i * (tr // 8) - 1, 0), j))
    w3 = pl.BlockSpec((3, 2 * PAIR), lambda j, i: (0, j))
    w1 = pl.BlockSpec((1, 2 * PAIR), lambda j, i: (0, j))
    return blk, prev, w3, w1


def _conv_gate_fwd(t, cw, cb, *, name):
    S, F2 = t.shape
    tr = min(S, 512)
    blk, prev, w3, w1 = _conv_specs(tr, S)

    def body(t_ref, prev_ref, w_ref, b_ref, a_ref):
        u, _, _, _ = _conv_u(t_ref, prev_ref, w_ref, b_ref, pl.program_id(1), tr)
        a_ref[...] = (jax.nn.silu(u[:, :PAIR]) * u[:, PAIR:]).astype(BF16)

    return pl.pallas_call(
        body, name=name, out_shape=jax.ShapeDtypeStruct((S, F2 // 2), BF16), grid=(F2 // (2 * PAIR), S // tr),
        in_specs=[blk, prev, w3, w1], out_specs=pl.BlockSpec((tr, PAIR), lambda j, i: (i, j)),
        compiler_params=_params("parallel", "parallel"),
    )(t, t, cw, cb)


def _conv_gate_bwd(t, da, cw, cb, *, name):
    S, F2 = t.shape
    tr = min(S, 256)
    blk, prev, w3, w1 = _conv_specs(tr, S)

    def body(t_ref, prev_ref, da_ref, w_ref, b_ref, du_ref, dw_ref, db_ref):
        i = pl.program_id(1)
        u, cur, t1, t2 = _conv_u(t_ref, prev_ref, w_ref, b_ref, i, tr)
        u1 = u[:, :PAIR]
        u2 = u[:, PAIR:]
        d = da_ref[...]
        sg = jax.nn.sigmoid(u1)
        du1 = d * u2 * (sg * (1.0 + u1 * (1.0 - sg)))
        du2 = d * (u1 * sg)
        du = jnp.concatenate([du1, du2], axis=1)
        du_ref[...] = du.astype(BF16)
        first = i == 0
        _acc_rows(db_ref, du, first)
        dw = jnp.concatenate([jnp.sum(du * t2, axis=0, keepdims=True), jnp.sum(du * t1, axis=0, keepdims=True),
                              jnp.sum(du * cur, axis=0, keepdims=True)], axis=0)

        @pl.when(first)
        def _():
            dw_ref[...] = dw

        @pl.when(jnp.logical_not(first))
        def _():
            dw_ref[...] += dw

    return pl.pallas_call(
        body, name=name,
        out_shape=(jax.ShapeDtypeStruct((S, F2), BF16), jax.ShapeDtypeStruct((3, F2), F32), jax.ShapeDtypeStruct((1, F2), F32)),
        grid=(F2 // (2 * PAIR), S // tr),
        in_specs=[blk, prev, pl.BlockSpec((tr, PAIR), lambda j, i: (i, j)), w3, w1], out_specs=(blk, w3, w1),
        compiler_params=_params("parallel", "arbitrary"),
    )(t, t, da, cw, cb)


def _conv_bwd_dt(du, cw, *, name):
    S, F2 = du.shape
    tr = min(S, 512)
    nrow = S // tr
    blk, _, w3, _ = _conv_specs(tr, S)
    nxt = pl.BlockSpec((16, 2 * PAIR), lambda j, i: (jnp.minimum((i + 1) * (tr // 16), S // 16 - 1), j))

    def body(d_ref, next_ref, w_ref, dt_ref):
        i = pl.program_id(1)
        cur = d_ref[...].astype(F32)
        live = (i < nrow - 1).astype(F32)
        n0 = next_ref[0:1, :].astype(F32) * live
        n1 = next_ref[1:2, :].astype(F32) * live
        row = lax.broadcasted_iota(jnp.int32, cur.shape, 0)
        d1 = jnp.where(row == tr - 1, n0, pltpu.roll(cur, tr - 1, 0))
        d2 = jnp.where(row == tr - 1, n1, jnp.where(row == tr - 2, n0, pltpu.roll(cur, tr - 2, 0)))
        dt_ref[...] = ((w_ref[2:3, :] * cur + w_ref[1:2, :] * d1) + w_ref[0:1, :] * d2).astype(BF16)

    return pl.pallas_call(
        body, name=name, out_shape=jax.ShapeDtypeStruct((S, F2), BF16), grid=(F2 // (2 * PAIR), nrow),
        in_specs=[blk, nxt, w3], out_specs=blk, compiler_params=_params("parallel", "parallel"),
    )(du, du, cw)


def _ada_fwd(c_all, w, b, *, name):
    Bn, D = c_all.shape
    N = w.shape[1]
    tn = _tile(N, 512)

    def body(c_ref, w_ref, b_ref, o_ref):
        o_ref[...] = jnp.dot(jax.nn.silu(c_ref[...]), w_ref[...], preferred_element_type=F32, precision=HIGHEST) + b_ref[...]

    return pl.pallas_call(
        body, name=name, out_shape=jax.ShapeDtypeStruct((Bn, N), F32), grid=(N // tn,),
        in_specs=[pl.BlockSpec((Bn, D), lambda j: (0, 0)), pl.BlockSpec((D, tn), lambda j: (0, j)),
                  pl.BlockSpec((1, tn), lambda j: (0, j))],
        out_specs=pl.BlockSpec((Bn, tn), lambda j: (0, j)), compiler_params=_params("parallel"),
    )(c_all, w, b)


def _ada_bwd(c_all_t, dmod, *, name):
    D, Bn = c_all_t.shape
    N = dmod.shape[1]
    tm = _tile(D, 512, 8)
    tn = _tile(N, 1536)

    def body(c_ref, d_ref, o_ref):
        o_ref[...] = jnp.dot(jax.nn.silu(c_ref[...]), d_ref[...], preferred_element_type=F32, precision=HIGHEST)

    return pl.pallas_call(
        body, name=name, out_shape=jax.ShapeDtypeStruct((D, N), F32), grid=(D // tm, N // tn),
        in_specs=[pl.BlockSpec((tm, Bn), lambda i, j: (i, 0)), pl.BlockSpec((Bn, tn), lambda i, j: (0, j))],
        out_specs=pl.BlockSpec((tm, tn), lambda i, j: (i, j)), compiler_params=_params("parallel", "parallel"),
    )(c_all_t, dmod)


def _adamw(w, g, m, v, *, name):
    R, C = w.shape
    tr = R if R * C <= (1 << 18) else _tile(R, max(8, (1 << 18) // C), 8)

    def body(w_ref, g_ref, m_ref, v_ref, d_ref, nm_ref, nv_ref):
        gv = g_ref[...]
        nm = ADAM_B1 * m_ref[...] + (1.0 - ADAM_B1) * gv
        nv = ADAM_B2 * v_ref[...] + (1.0 - ADAM_B2) * (gv * gv)
        m_hat = nm / (1.0 - ADAM_B1 ** ADAM_STEP)
        v_hat = nv / (1.0 - ADAM_B2 ** ADAM_STEP)
        d_ref[...] = -ADAM_LR * (m_hat / (jnp.sqrt(v_hat) + ADAM_EPS) + ADAM_WD * w_ref[...])
        nm_ref[...] = nm
        nv_ref[...] = nv

    blk = pl.BlockSpec((tr, C), lambda i: (i, 0))
    shp = jax.ShapeDtypeStruct((R, C), F32)
    return pl.pallas_call(
        body, name=name, out_shape=(shp, shp, shp), grid=(R // tr,), in_specs=[blk] * 4, out_specs=(blk,) * 3,
        compiler_params=_params("parallel"),
    )(w, g, m, v)


def _place():
    x, y, c = lax.axis_index("x"), lax.axis_index("y"), lax.axis_index("c")
    return x, y, c, [(1 - x, y), (x, 1 - y), (1 - x, 1 - y)]


def _remote(src, dst, send_sem, recv_sem, dev):
    return pltpu.make_async_remote_copy(src_ref=src, dst_ref=dst, send_sem=send_sem, recv_sem=recv_sem,
                                        device_id=dev, device_id_type=MESH)


def _allgather8(v, *, name):
    R, C = v.shape

    def body(v_ref, out_ref, send_sems, recv_sems, local_sem):
        x, y, c, chips = _place()
        me, sibling = (x, y, c), (x, y, 1 - c)

        def rows(px, py, pc):
            return out_ref.at[pl.ds((4 * px + 2 * py + pc) * R, R), :]

        def copy(k, block, to, src=None):
            return _remote(rows(*block) if src is None else src, rows(*block), send_sems.at[k], recv_sems.at[k], to)

        mine = pltpu.make_async_copy(v_ref, rows(*me), local_sem)
        mine.start()
        first = [copy(0, me, sibling, src=v_ref)]
        first += [copy(1 + j, me, (*chip, c), src=v_ref) for j, chip in enumerate(chips)]
        for cp in first:
            cp.start()
        passed = [copy(4 + j, (*chip, c), sibling) for j, chip in enumerate(chips)]
        for j, chip in enumerate(chips):
            copy(1 + j, (*chip, c), me).wait_recv()
            passed[j].start()
        copy(0, sibling, me).wait_recv()
        for j, chip in enumerate(chips):
            copy(4 + j, (*chip, 1 - c), me).wait_recv()
        for cp in first + passed:
            cp.wait_send()
        mine.wait()

    out = pl.pallas_call(
        body, name=name, out_shape=jax.ShapeDtypeStruct((N_DEV * R, C), v.dtype),
        in_specs=[VMEM_FULL], out_specs=VMEM_FULL,
        scratch_shapes=[pltpu.SemaphoreType.DMA((7,)), pltpu.SemaphoreType.DMA((7,)), pltpu.SemaphoreType.DMA],
    )(v)
    return out.reshape(N_DEV, R, C)


def _gather_weights(ws, *, name):
    n = len(ws)

    def body(*refs):
        ins, outs = refs[:n], refs[n:2 * n]
        send_sems, recv_sems, local_sems = refs[2 * n:]
        x, y, c, chips = _place()
        k = 2 * x + y
        sibling = (x, y, 1 - c)

        def piece(i, chip_idx, who):
            r2 = ws[i].shape[0] // 2
            return outs[i].at[chip_idx, pl.ds(who * r2, r2)]

        local = [pltpu.make_async_copy(ins[i], outs[i].at[k], local_sems.at[i]) for i in range(n)]
        for cp in local:
            cp.start()
        sends = []
        for i in range(n):
            r2 = ws[i].shape[0] // 2
            for r, (cx, cy) in enumerate(chips):
                sends.append(_remote(ins[i].at[pl.ds(c * r2, r2)], piece(i, k, c), send_sems.at[6 * i + r],
                                     recv_sems.at[6 * i + r], (cx, cy, c)))
                sends[-1].start()
        for i in range(n):
            for r, (cx, cy) in enumerate(chips):
                got = piece(i, 2 * cx + cy, c)
                _remote(got, got, send_sems.at[6 * i + r], recv_sems.at[6 * i + r], (cx, cy, c)).wait_recv()
                sends.append(_remote(got, got, send_sems.at[6 * i + 3 + r], recv_sems.at[6 * i + 3 + r], sibling))
                sends[-1].start()
        for i in range(n):
            for r, (cx, cy) in enumerate(chips):
                got = piece(i, 2 * cx + cy, 1 - c)
                _remote(got, got, send_sems.at[6 * i + 3 + r], recv_sems.at[6 * i + 3 + r], sibling).wait_recv()
        for cp in sends:
            cp.wait_send()
        for cp in local:
            cp.wait()

    return pl.pallas_call(
        body, name=name,
        out_shape=[jax.ShapeDtypeStruct((N_CHIP,) + w.shape, w.dtype) for w in ws],
        in_specs=[ANY] * n, out_specs=[ANY] * n,
        scratch_shapes=[pltpu.SemaphoreType.DMA((6 * n,)), pltpu.SemaphoreType.DMA((6 * n,)), pltpu.SemaphoreType.DMA((n,))],
    )(*ws)


SEM = pl.BlockSpec(memory_space=pltpu.SEMAPHORE)
HBM = pl.BlockSpec(memory_space=pltpu.HBM)
EFFECT = pltpu.SideEffectType.DATAFLOW_SIDE_EFFECTING
DMA_SEM = pltpu.SemaphoreType.DMA(())


def _in_hbm(a):
    return pltpu.with_memory_space_constraint(a, pltpu.HBM)


def _three_halves(land, r2):
    return land.at[pl.ds(0, N_CHIP - 1), pl.ds(0, r2)]


def _gather_start(ws, *, name):
    n = len(ws)
    lands = [lax.empty((N_CHIP,) + w.shape, w.dtype) for w in ws]

    def body(*refs):
        w_refs, land_refs = refs[:n], refs[n:2 * n]
        send, recv = refs[2 * n:3 * n], refs[3 * n:4 * n]
        token = refs[6 * n]
        x, y, c, chips = _place()
        k = 2 * x + y
        for i in range(n):
            r2 = ws[i].shape[0] // 2
            for cx, cy in chips:
                _remote(w_refs[i].at[pl.ds(c * r2, r2)], land_refs[i].at[k, pl.ds(c * r2, r2)], send[i], recv[i],
                        (cx, cy, c)).start()
        token[...] = jnp.zeros_like(token)

    outs = pl.pallas_call(
        body, name=name,
        out_shape=[DMA_SEM] * (2 * n) + [pltpu.HBM(w.shape, w.dtype) for w in ws] + [pltpu.HBM(l.shape, l.dtype) for l in lands]
        + [jax.ShapeDtypeStruct((8, LANES), F32)],
        in_specs=[HBM] * (2 * n), out_specs=[SEM] * (2 * n) + [HBM] * (2 * n) + [VMEM_FULL],
        input_output_aliases={i: 2 * n + i for i in range(2 * n)},
        compiler_params=pltpu.CompilerParams(has_side_effects=EFFECT),
    )(*[_in_hbm(w) for w in ws], *[_in_hbm(l) for l in lands])
    return outs[:n], outs[n:2 * n], outs[2 * n:3 * n], outs[3 * n:4 * n], outs[4 * n]


def _gather_forward(send, recv, ws, lands, after, *, name):
    n = len(ws)

    def body(*refs):
        w_refs, land_refs = refs[:n], refs[n:2 * n]
        send1, recv1 = refs[2 * n:3 * n], refs[3 * n:4 * n]
        send2, recv2 = refs[4 * n + 1 + 2 * n:4 * n + 1 + 3 * n], refs[4 * n + 1 + 3 * n:4 * n + 1 + 4 * n]
        x, y, c, chips = _place()
        sibling = (x, y, 1 - c)
        for i in range(n):
            r2 = ws[i].shape[0] // 2
            win = _three_halves(land_refs[i], r2)
            done = _remote(win, win, send1[i], recv1[i], sibling)
            done.wait_send()
            done.wait_recv()
            for cx, cy in chips:
                got = land_refs[i].at[2 * cx + cy, pl.ds(c * r2, r2)]
                _remote(got, got, send2[i], recv2[i], sibling).start()

    outs = pl.pallas_call(
        body, name=name,
        out_shape=[pltpu.HBM(w.shape, w.dtype) for w in ws] + [pltpu.HBM(l.shape, l.dtype) for l in lands] + [DMA_SEM] * (2 * n),
        in_specs=[HBM] * (2 * n) + [SEM] * (2 * n) + [ANY], out_specs=[HBM] * (2 * n) + [SEM] * (2 * n),
        input_output_aliases={i: i for i in range(2 * n)},
        compiler_params=pltpu.CompilerParams(has_side_effects=EFFECT),
    )(*ws, *lands, *send, *recv, after)
    return outs[2 * n:3 * n], outs[3 * n:4 * n], outs[n:2 * n]


def _gather_finish(send, recv, lands, after, *, name):
    n = len(lands)

    def body(*refs):
        land_refs = refs[:n]
        send2, recv2 = refs[n:2 * n], refs[2 * n:3 * n]
        x, y, c, _ = _place()
        for i in range(n):
            win = _three_halves(land_refs[i], lands[i].shape[1] // 2)
            done = _remote(win, win, send2[i], recv2[i], (x, y, 1 - c))
            done.wait_send()
            done.wait_recv()

    return pl.pallas_call(
        body, name=name,
        out_shape=[pltpu.HBM(l.shape, l.dtype) for l in lands],
        in_specs=[HBM] * n + [SEM] * (2 * n) + [ANY], out_specs=[HBM] * n,
        input_output_aliases={i: i for i in range(n)},
        compiler_params=pltpu.CompilerParams(has_side_effects=EFFECT),
    )(*lands, *send, *recv, after)


def _send_other_halves(gs, *, name):
    n = len(gs)

    def body(*refs):
        ins, outs = refs[:n], refs[n:2 * n]
        send_sems, recv_sems = refs[2 * n:]
        x, y, c, _ = _place()
        sibling = (x, y, 1 - c)
        cps = []
        for i in range(n):
            r2 = gs[i].shape[1] // 2
            cps.append(_remote(ins[i].at[:, pl.ds((1 - c) * r2, r2)], outs[i], send_sems.at[i], recv_sems.at[i], sibling))
            cps[-1].start()
        for cp in cps:
            cp.wait()

    return pl.pallas_call(
        body, name=name,
        out_shape=[jax.ShapeDtypeStruct((N_CHIP, g.shape[1] // 2, g.shape[2]), g.dtype) for g in gs],
        in_specs=[ANY] * n, out_specs=[ANY] * n,
        scratch_shapes=[pltpu.SemaphoreType.DMA((n,)), pltpu.SemaphoreType.DMA((n,))],
    )(*gs)


def _scatter_to_chips(ps, *, name):
    n = len(ps)

    def body(*refs):
        ins, outs = refs[:n], refs[n:2 * n]
        send_sems, recv_sems, local_sems = refs[2 * n:]
        x, y, c, chips = _place()
        k = 2 * x + y
        local = [pltpu.make_async_copy(ins[i].at[k], outs[i].at[k], local_sems.at[i]) for i in range(n)]
        for cp in local:
            cp.start()
        cps = []
        for i in range(n):
            for r, (cx, cy) in enumerate(chips):
                cps.append(_remote(ins[i].at[2 * cx + cy], outs[i].at[k], send_sems.at[3 * i + r], recv_sems.at[3 * i + r],
                                   (cx, cy, c)))
                cps[-1].start()
        for i in range(n):
            for r, (cx, cy) in enumerate(chips):
                got = outs[i].at[2 * cx + cy]
                _remote(got, got, send_sems.at[3 * i + r], recv_sems.at[3 * i + r], (cx, cy, c)).wait_recv()
        for cp in cps:
            cp.wait_send()
        for cp in local:
            cp.wait()

    return pl.pallas_call(
        body, name=name,
        out_shape=[jax.ShapeDtypeStruct(p.shape, p.dtype) for p in ps],
        in_specs=[ANY] * n, out_specs=[ANY] * n,
        scratch_shapes=[pltpu.SemaphoreType.DMA((3 * n,)), pltpu.SemaphoreType.DMA((3 * n,)), pltpu.SemaphoreType.DMA((n,))],
    )(*ps)


def _share_halves(ts, *, name):
    n = len(ts)

    def body(*refs):
        outs = refs[n:2 * n]
        send_sems, recv_sems = refs[2 * n:]
        x, y, c, _ = _place()
        sibling = (x, y, 1 - c)
        cps = []
        for i in range(n):
            r2 = ts[i].shape[0] // 2
            mine = outs[i].at[pl.ds(c * r2, r2)]
            cps.append(_remote(mine, mine, send_sems.at[i], recv_sems.at[i], sibling))
            cps[-1].start()
        for i in range(n):
            r2 = ts[i].shape[0] // 2
            got = outs[i].at[pl.ds((1 - c) * r2, r2)]
            _remote(got, got, send_sems.at[i], recv_sems.at[i], sibling).wait_recv()
        for cp in cps:
            cp.wait_send()

    return pl.pallas_call(
        body, name=name,
        out_shape=[jax.ShapeDtypeStruct(t.shape, t.dtype) for t in ts],
        in_specs=[ANY] * n, out_specs=[ANY] * n, input_output_aliases={i: i for i in range(n)},
        scratch_shapes=[pltpu.SemaphoreType.DMA((n,)), pltpu.SemaphoreType.DMA((n,))],
    )(*ts)


def _add_sibling(g, recv, c_idx, *, name):
    _, R, C = g.shape
    r2 = R // 2
    tr = _tile(r2, max(16, (1 << 19) // C), 16)
    nr = r2 // tr

    def body(c_ref, g_ref, r_ref, o_ref):
        o_ref[...] = (g_ref[...].astype(F32) + r_ref[...].astype(F32)).astype(BF16)

    return pl.pallas_call(
        body, name=name, out_shape=jax.ShapeDtypeStruct((N_CHIP, r2, C), BF16),
        grid_spec=pltpu.PrefetchScalarGridSpec(
            num_scalar_prefetch=1, grid=(N_CHIP, nr),
            in_specs=[pl.BlockSpec((None, tr, C), lambda j, i, c_ref: (j, c_ref[0] * nr + i, 0)),
                      pl.BlockSpec((None, tr, C), lambda j, i, c_ref: (j, i, 0))],
            out_specs=pl.BlockSpec((None, tr, C), lambda j, i, c_ref: (j, i, 0))),
        compiler_params=_params("parallel", "parallel"),
    )(c_idx, g, recv)


def _sum_chips(p, c_idx, *, name):
    _, r2, C = p.shape
    tr = _tile(r2, max(16, (1 << 18) // C), 16)
    nr = r2 // tr

    def body(c_ref, p_ref, o_ref):
        acc = p_ref[0].astype(F32)
        for j in range(1, N_CHIP):
            acc = acc + p_ref[j].astype(F32)
        o_ref[...] = acc

    return pl.pallas_call(
        body, name=name, out_shape=jax.ShapeDtypeStruct((2 * r2, C), F32),
        grid_spec=pltpu.PrefetchScalarGridSpec(
            num_scalar_prefetch=1, grid=(nr,),
            in_specs=[pl.BlockSpec((N_CHIP, tr, C), lambda i, c_ref: (0, i, 0))],
            out_specs=pl.BlockSpec((tr, C), lambda i, c_ref: (c_ref[0] * nr + i, 0))),
        compiler_params=_params("parallel"),
    )(c_idx, p)


def _sum_devices(v, *, name):
    n, R, C = v.shape

    def body(v_ref, o_ref):
        acc = v_ref[0]
        for j in range(1, n):
            acc = acc + v_ref[j]
        o_ref[...] = acc

    return pl.pallas_call(body, name=name, out_shape=jax.ShapeDtypeStruct((R, C), F32),
                          in_specs=[VMEM_FULL], out_specs=VMEM_FULL)(v)


def _pair(a, b):
    n = a.shape[1]
    parts = []
    for j in range(n // PAIR):
        parts += [a[:, j * PAIR:(j + 1) * PAIR], b[:, j * PAIR:(j + 1) * PAIR]]
    return jnp.concatenate(parts, axis=1)


def _unpair(p):
    nt = p.shape[1] // (2 * PAIR)
    a = jnp.concatenate([p[:, 2 * j * PAIR:(2 * j + 1) * PAIR] for j in range(nt)], axis=1)
    b = jnp.concatenate([p[:, (2 * j + 1) * PAIR:(2 * j + 2) * PAIR] for j in range(nt)], axis=1)
    return a, b


def _from_col_shards(g):
    return jnp.transpose(g, (1, 0, 2)).reshape(g.shape[1], N_CHIP * g.shape[2])


def _to_col_shards(w):
    R, N = w.shape
    return jnp.transpose(w.reshape(R, N_CHIP, N // N_CHIP), (1, 0, 2))


def _split_heads(w, widths):
    R, N = w.shape
    per = sum(widths)
    w3 = w.reshape(R, N // per, per)
    lo = w3[:, :, :widths[0]].reshape(R, -1)
    hi = w3[:, :, widths[0]:].reshape(R, -1)
    return jnp.concatenate([lo, hi], axis=1)


def _merge_heads(w, widths):
    R, N = w.shape
    H = N // sum(widths)
    lo = w[:, :H * widths[0]].reshape(R, H, widths[0])
    hi = w[:, H * widths[0]:].reshape(R, H, widths[1])
    return jnp.concatenate([lo, hi], axis=2).reshape(R, N)


def _t5_bucket(dist):
    max_exact = REL_BUCKETS // 2
    n = jnp.maximum(dist, 0)
    large = max_exact + (jnp.log(jnp.maximum(n, 1).astype(F32) / max_exact)
                         / jnp.log(jnp.asarray(REL_MAX_DIST / max_exact, F32))
                         * (REL_BUCKETS - max_exact)).astype(jnp.int32)
    large = jnp.minimum(large, REL_BUCKETS - 1)
    return jnp.where(n < max_exact, n, large)


def _rel_tables():
    a = jnp.arange(SWA_BLOCK)
    b = jnp.arange(2 * SWA_BLOCK)
    dist = SWA_BLOCK + a[:, None] - b[None, :]
    valid = jnp.logical_and(dist >= 0, dist < SWA_BLOCK)
    onehot = jnp.logical_and(_t5_bucket(dist)[..., None] == jnp.arange(REL_BUCKETS), valid[..., None])
    onehot = onehot.astype(F32).reshape(2 * SWA_BLOCK * SWA_BLOCK, REL_BUCKETS)
    negmask = jnp.where(valid, 0.0, NEG).astype(F32).reshape(1, -1)
    return onehot, negmask


def _rope_tables(S):
    pos = jnp.arange(S, dtype=F32)
    inv = ROPE_THETA ** (-jnp.arange(0, MLA_ROPE, 2, dtype=F32) / MLA_ROPE)
    ang = pos[:, None] * inv[None, :]
    ang = jnp.concatenate([ang, ang, ang, ang], axis=-1)
    return jnp.cos(ang), jnp.sin(ang)


def _flat_pad(parts, rows=8):
    flat = jnp.concatenate([p.reshape(1, -1) for p in parts], axis=1)
    n = flat.shape[1]
    width = -(-n // (rows * LANES)) * LANES
    return jnp.pad(flat, ((0, 0), (0, rows * width - n))).reshape(rows, width)


def _unflat(vec, shapes):
    flat = vec.reshape(-1)
    out, off = [], 0
    for s in shapes:
        n = 1
        for d in s:
            n *= d
        out.append(flat[off:off + n].reshape(s))
        off += n
    return out


def kernel(x, c, w_ada, b_ada, g_pre_mix, g_post_mix, w_in, g_q_lat, w_uq, g_kv_lat, w_ukv, rel_bias, sinks, w_o, g_pre_ffn, g_post_ffn, w_up, conv_w, conv_b, w_down, loss_target, m_w_ada, m_b_ada, m_g_pre_mix, m_g_post_mix, m_w_in, m_g_q_lat, m_w_uq, m_g_kv_lat, m_w_ukv, m_rel_bias, m_sinks, m_w_o, m_g_pre_ffn, m_g_post_ffn, m_w_up, m_conv_w, m_conv_b, m_w_down, v_w_ada, v_b_ada, v_g_pre_mix, v_g_post_mix, v_w_in, v_g_q_lat, v_w_uq, v_g_kv_lat, v_w_ukv, v_rel_bias, v_sinks, v_w_o, v_g_pre_ffn, v_g_post_ffn, v_w_up, v_conv_w, v_conv_b, v_w_down):
    S, D = x.shape[1], x.shape[2]
    Rq, Rkv = g_q_lat.shape[1], g_kv_lat.shape[1]
    H = D // MLA_V
    NH = D // SWA_HD
    KW = SWA_KVH * SWA_HD
    F = w_down.shape[1] * N_CHIP
    xi, yi, ci = lax.axis_index("x"), lax.axis_index("y"), lax.axis_index("c")
    chip = 2 * xi + yi
    me = 2 * chip + ci
    x2, tgt = x[0], loss_target[0]

    big = [w_in[0], w_uq[0], w_ukv[0], w_o[0], w_up[0], w_down[0]]
    local = [w.astype(BF16) for w in big]
    send1, recv1, srcs, lands, token = _gather_start(local, name="gather_start")

    c_all = _allgather8(jnp.broadcast_to(c + token[0, 0], (8, D)), name="gather_c")[:, 0, :]
    n3 = w_ada.shape[2]
    mod_part = _ada_fwd(c_all, w_ada[0], lax.dynamic_slice(b_ada, (0, chip * n3), (1, n3)), name="ada_fwd")
    mod_all = _allgather8(mod_part, name="gather_mod")
    mod_me = lax.dynamic_index_in_dim(mod_all[0::2], me, axis=1, keepdims=False).reshape(1, 6 * D)
    sh1, sc1, gt1, sh2, sc2, gt2 = [mod_me[:, k * D:(k + 1) * D] for k in range(6)]

    cw_all = _allgather8(jnp.pad(conv_w[0], ((0, 5), (0, 0))), name="gather_conv_w")[0::2, :3]
    onehot, negmask = _rel_tables()
    bias_m = (_matmul(rel_bias.T, onehot.T, name="rel_bias_table") + negmask).reshape(NH, SWA_BLOCK, 2 * SWA_BLOCK)
    h = _modnorm_fwd(x2, g_pre_mix, sc1, sh1, name="pre_mix_norm")

    def whole(land, i):
        return lax.dynamic_update_index_in_dim(land, local[i], chip, 0)

    s2, r2, l_in = _gather_forward(send1[:1], recv1[:1], srcs[:1], lands[:1], h, name="gather_forward_in")
    (l_in,) = _gather_finish(s2, r2, l_in, h, name="gather_finish_in")
    s2b, r2b, l_b = _gather_forward(send1[1:4], recv1[1:4], srcs[1:4], lands[1:4], l_in, name="gather_forward_attn")
    gin = whole(l_in, 0)
    win = _from_col_shards(gin)
    o_kr = Rq + Rkv
    o_q = o_kr + MLA_ROPE
    o_g = o_q + NH * SWA_HD + 2 * KW
    w_lat = jnp.concatenate([win[:, :o_q], win[:, o_kr:o_q]], axis=1)
    w_swa = win[:, o_q:o_g]
    w_gate = _pair(win[:, o_g:o_g + D], win[:, o_g + D:])
    w_in_all = jnp.concatenate([w_lat, w_swa, w_gate], axis=1)
    n_lat, n_swa = w_lat.shape[1], w_swa.shape[1]
    cw_full = _from_col_shards(cw_all)
    cw = _pair(cw_full[:, :F], cw_full[:, F:])
    cb = _pair(conv_b[:, :F], conv_b[:, F:])
    cos, sin = _rope_tables(S)
    sink_v = sinks.reshape(NH)

    z_lat = _matmul(h, w_lat, name="in_proj_lat")
    z_swa = _matmul(h, w_swa, name="in_proj_swa")
    zg = _matmul(h, w_gate, name="in_proj_gate")
    nq, nkv = _lat_norm_fwd(z_lat, g_q_lat, g_kv_lat, name="lat_norm")
    l_uq, l_ukv, l_o = _gather_finish(s2b, r2b, l_b, nq, name="gather_finish_attn")
    wuq = _split_heads(_from_col_shards(whole(l_uq, 1)), (MLA_NOPE, MLA_ROPE))
    wukv = _split_heads(_from_col_shards(whole(l_ukv, 2)), (MLA_NOPE, MLA_V))
    wo = whole(l_o, 3).reshape(D, D)
    q_raw = _matmul(nq, wuq, name="uq_proj")
    kv_raw = _matmul(nkv, wukv, name="ukv_proj")
    qp, kp, vv = _mla_pack_fwd(q_raw, kv_raw, z_lat, cos, sin, o_kr, name="mla_pack")
    o_a, lse = _flash_fwd(qp, kp, vv, name="mla_attn")
    s2c, r2c, l_c = _gather_forward(send1[4:], recv1[4:], srcs[4:], lands[4:], o_a, name="gather_forward_ffn")
    o_b = _swa_fwd(z_swa, bias_m, sink_v, name="swa_attn")
    mixin = _gate_fwd(zg, o_a, o_b, name="gate_mix")
    mix = _matmul(mixin, wo, name="o_proj")
    x1 = _resnorm_fwd(x2, mix, g_post_mix, gt1, name="post_mix_norm")
    h2 = _modnorm_fwd(x1, g_pre_ffn, sc2, sh2, name="pre_ffn_norm")
    l_up, l_down = _gather_finish(s2c, r2c, l_c, h2, name="gather_finish_ffn")
    wup_full = _from_col_shards(whole(l_up, 4))
    wup = _pair(wup_full[:, :F], wup_full[:, F:])
    wdown = whole(l_down, 5).reshape(F, D)
    t = _matmul(h2, wup, name="up_proj")
    a = _conv_gate_fwd(t, cw, cb, name="conv_gate")
    yv = _matmul(a, wdown, name="down_proj")
    dout, loss_tile = _resnorm_loss(x1, yv, g_post_ffn, gt2, tgt, name="post_ffn_norm_loss")

    dy, dg_post_ffn, dgt2 = _resnorm_bwd(dout, yv, g_post_ffn, gt2, name="post_ffn_norm_bwd")
    da = _matmul(dy, wdown, tb=True, name="down_proj_dx")
    dw_down = _matmul(a, dy, ta=True, out_dtype=BF16, name="down_proj_dw")
    du, dcw_p, dcb_p = _conv_gate_bwd(t, da, cw, cb, name="conv_gate_bwd")
    dt = _conv_bwd_dt(du, cw, name="conv_bwd_dt")
    dh2 = _matmul(dt, wup, tb=True, name="up_proj_dx")
    dw_up_p = _matmul(h2, dt, ta=True, out_dtype=BF16, name="up_proj_dw")
    dx1, dg_pre_ffn, dsc2, dsh2 = _modnorm_bwd(dh2, x1, g_pre_ffn, sc2, dout, name="pre_ffn_norm_bwd")
    dmix, dg_post_mix, dgt1 = _resnorm_bwd(dx1, mix, g_post_mix, gt1, name="post_mix_norm_bwd")
    dmixin = _matmul(dmix, wo, tb=True, name="o_proj_dx")
    dw_o = _matmul(mixin, dmix, ta=True, out_dtype=BF16, name="o_proj_dw")
    do_a, do_b, dzg = _gate_bwd(dmixin, zg, o_a, o_b, name="gate_mix_bwd")
    dqp, dkp, dvv = _flash_bwd(qp, kp, vv, o_a, do_a, lse, name="mla_attn_bwd")
    dq_raw, dkv_raw, dkr = _mla_pack_bwd(dqp, dkp, dvv, cos, sin, name="mla_pack_bwd")
    dnq = _matmul(dq_raw, wuq, tb=True, name="uq_proj_dx")
    dw_uq_p = _matmul(nq, dq_raw, ta=True, out_dtype=BF16, name="uq_proj_dw")
    dnkv = _matmul(dkv_raw, wukv, tb=True, name="ukv_proj_dx")
    dw_ukv_p = _matmul(nkv, dkv_raw, ta=True, out_dtype=BF16, name="ukv_proj_dw")
    dz_lat, dg_q, dg_kv = _lat_norm_bwd(z_lat, dnq, dnkv, dkr, g_q_lat, g_kv_lat, name="lat_norm_bwd")
    dz_swa, dbias, dsink = _swa_bwd(z_swa, bias_m, sink_v, o_b, do_b, name="swa_attn_bwd")
    dz = jnp.concatenate([dz_lat, dz_swa, dzg], axis=1)
    dh = _matmul(dz, w_in_all, tb=True, name="in_proj_dx")
    dw_in_p = _matmul(h, dz, ta=True, out_dtype=BF16, name="in_proj_dw")
    grad_x, dg_pre_mix, dsc1, dsh1 = _modnorm_bwd(dh, x2, g_pre_mix, sc1, dx1, name="pre_mix_norm_bwd")
    drel = _matmul(dbias.reshape(NH, -1), onehot, name="rel_bias_bwd").T

    dga, dgb = _unpair(dw_in_p[:, n_lat + n_swa:])
    dw_in = jnp.concatenate([dw_in_p[:, :o_q], dw_in_p[:, n_lat:n_lat + n_swa], dga, dgb], axis=1)
    dw_uq = _merge_heads(dw_uq_p, (MLA_NOPE, MLA_ROPE))
    dw_ukv = _merge_heads(dw_ukv_p, (MLA_NOPE, MLA_V))
    dw_up = jnp.concatenate(_unpair(dw_up_p), axis=1)
    dcw = jnp.concatenate(_unpair(dcw_p), axis=1)
    dcb = jnp.concatenate(_unpair(dcb_p), axis=1)
    dmod = jnp.concatenate([dsh1, dsc1, dgt1, dsh2, dsc2, dgt2], axis=1)
    small = [dmod, dg_pre_mix, dg_post_mix, dg_pre_ffn, dg_post_ffn, dg_q, dg_kv, drel, dsink[:, :NH], dcb, dcw]
    shapes = [p.shape for p in small]
    small_all = _allgather8(_flat_pad(small), name="gather_small_grads")
    tot = _unflat(_sum_devices(small_all, name="sum_small_grads"), shapes)
    g_b_ada, g_pre_mix_g, g_post_mix_g, g_pre_ffn_g, g_post_ffn_g, g_q_g, g_kv_g, g_rel, g_sinks, g_cb, g_cw_full = tot
    dmod_all = small_all.reshape(N_DEV, -1)[:, :6 * D]
    g_w_ada = _ada_bwd(c_all.T, lax.dynamic_slice(dmod_all, (0, chip * n3), (N_DEV, n3)), name="ada_bwd")
    ncw = conv_w.shape[2]
    g_cw = lax.dynamic_slice(g_cw_full, (0, chip * ncw), (3, ncw))

    gs = [_to_col_shards(dw_in), _to_col_shards(dw_uq), _to_col_shards(dw_ukv), dw_o.reshape(N_CHIP, D // N_CHIP, D),
          _to_col_shards(dw_up), dw_down.reshape(N_CHIP, F // N_CHIP, D)]
    names = ["w_in", "w_uq", "w_ukv", "w_o", "w_up", "w_down"]
    recv = _send_other_halves(gs, name="grads_to_sibling")
    c_idx = jnp.reshape(ci, (1,)).astype(jnp.int32)
    part = [_add_sibling(g, r, c_idx, name="grad_add_sibling_" + nm) for g, r, nm in zip(gs, recv, names)]
    got = _scatter_to_chips(part, name="grads_to_chips")
    halves = [_sum_chips(p, c_idx, name="grad_sum_chips_" + nm) for p, nm in zip(got, names)]
    g_in, g_uq, g_ukv, g_o, g_up, g_down = _share_halves(halves, name="grads_share_halves")

    res = {}
    bigs = dict(w_ada=(w_ada[0], g_w_ada, m_w_ada[0], v_w_ada[0]), w_in=(w_in[0], g_in, m_w_in[0], v_w_in[0]),
                w_uq=(w_uq[0], g_uq, m_w_uq[0], v_w_uq[0]), w_ukv=(w_ukv[0], g_ukv, m_w_ukv[0], v_w_ukv[0]),
                w_o=(w_o[0], g_o, m_w_o[0], v_w_o[0]), w_up=(w_up[0], g_up, m_w_up[0], v_w_up[0]),
                w_down=(w_down[0], g_down, m_w_down[0], v_w_down[0]))
    for nm, (w, g, m, v) in bigs.items():
        res[nm] = (g,) + tuple(_adamw(w, g, m, v, name="adamw_" + nm))
    snames = ["b_ada", "g_pre_mix", "g_post_mix", "g_pre_ffn", "g_post_ffn", "g_q_lat", "g_kv_lat", "rel_bias", "sinks",
              "conv_b", "conv_w"]
    sw = [b_ada, g_pre_mix, g_post_mix, g_pre_ffn, g_post_ffn, g_q_lat, g_kv_lat, rel_bias, sinks, conv_b, conv_w]
    sm = [m_b_ada, m_g_pre_mix, m_g_post_mix, m_g_pre_ffn, m_g_post_ffn, m_g_q_lat, m_g_kv_lat, m_rel_bias, m_sinks,
          m_conv_b, m_conv_w]
    sv = [v_b_ada, v_g_pre_mix, v_g_post_mix, v_g_pre_ffn, v_g_post_ffn, v_g_q_lat, v_g_kv_lat, v_rel_bias, v_sinks,
          v_conv_b, v_conv_w]
    sg = [g_b_ada, g_pre_mix_g, g_post_mix_g, g_pre_ffn_g, g_post_ffn_g, g_q_g, g_kv_g, g_rel, g_sinks, g_cb, g_cw]
    sshapes = [w.shape for w in sw]
    sd, snm, snv = _adamw(_flat_pad(sw), _flat_pad(sg), _flat_pad(sm), _flat_pad(sv), name="adamw_small")
    sd, snm, snv = _unflat(sd, sshapes), _unflat(snm, sshapes), _unflat(snv, sshapes)
    for k, nm in enumerate(snames):
        res[nm] = (sg[k].reshape(sshapes[k]), sd[k], snm[k], snv[k])

    order = ["w_ada", "b_ada", "g_pre_mix", "g_post_mix", "w_in", "g_q_lat", "w_uq", "g_kv_lat", "w_ukv", "rel_bias", "sinks",
             "w_o", "g_pre_ffn", "g_post_ffn", "w_up", "conv_w", "conv_b", "w_down"]
    ref_shapes = dict(w_ada=w_ada.shape, w_in=w_in.shape, w_uq=w_uq.shape, w_ukv=w_ukv.shape, w_o=w_o.shape,
                      w_up=w_up.shape, w_down=w_down.shape)
    outs = []
    for k in range(4):
        for nm in order:
            arr = res[nm][k]
            outs.append(arr.reshape(ref_shapes[nm]) if nm in ref_shapes else arr)
    loss = lax.psum(loss_tile[0, 0], ("x", "y", "c"))
    return (loss, grad_x[None], *outs)
```

```python
import functools

import jax
import jax.numpy as jnp
from jax import lax
from jax.experimental import pallas as pl
from jax.experimental.pallas import tpu as pltpu

F32 = jnp.float32
BF16 = jnp.bfloat16
MESH = pl.DeviceIdType.MESH
HIGHEST = lax.Precision.HIGHEST

N_DEV = 8
N_CHIP = 4
LANES = 128
MLA_NOPE = 128
MLA_ROPE = 64
MLA_V = 128
MLA_QK = MLA_NOPE + MLA_ROPE
MLA_QK_PAD = 256
ROPE_THETA = 10000.0
SWA_HD = 64
SWA_KVH = 4
SWA_BLOCK = 128
REL_BUCKETS = 32
REL_MAX_DIST = 128
PAIR = 512
EPS = 1e-6
NEG = -1e30
ADAM_LR = 0.001
ADAM_B1 = 0.9
ADAM_B2 = 0.999
ADAM_EPS = 1e-08
ADAM_WD = 0.01
ADAM_STEP = 10

ANY = pl.BlockSpec(memory_space=pl.ANY)
VMEM_FULL = pl.BlockSpec(memory_space=pltpu.VMEM)
SMEM_FULL = pl.BlockSpec(memory_space=pltpu.SMEM)


def _params(*sem):
    return pltpu.CompilerParams(dimension_semantics=sem if sem else None)


def _tied(body, tie):
    if tie is None:
        return body, [], []

    def tied_body(tie_ref, *refs):
        body(*refs)

    return tied_body, [ANY], [tie]


def _tile(n, pref, unit=LANES):
    best = None
    for t in range(unit, min(n, pref) + 1, unit):
        if n % t == 0:
            best = t
    return n if best is None else best


def _matmul(a, b, *, ta=False, tb=False, out_dtype=F32, tie=None, name):
    if ta:
        K, M = a.shape
    else:
        M, K = a.shape
    if tb:
        N, K2 = b.shape
    else:
        K2, N = b.shape
    assert K == K2, (a.shape, b.shape, ta, tb)
    exact = a.dtype == F32
    tm = _tile(M, 512, 8) if M >= 8 else M
    tn = _tile(N, 1536)
    tk = _tile(K, 2048)
    nk = K // tk
    dn = (((0 if ta else 1,), (1 if tb else 0,)), ((), ()))

    def product(a_ref, b_ref):
        return lax.dot_general(a_ref[...], b_ref[...], dn, preferred_element_type=F32,
                               precision=HIGHEST if exact else None)

    def body_acc(a_ref, b_ref, o_ref, acc_ref):
        k = pl.program_id(2)

        @pl.when(k == 0)
        def _():
            acc_ref[...] = product(a_ref, b_ref)

        @pl.when(jnp.logical_and(k > 0, k < nk - 1))
        def _():
            acc_ref[...] += product(a_ref, b_ref)

        @pl.when(k == nk - 1)
        def _():
            o_ref[...] = (acc_ref[...] + product(a_ref, b_ref)).astype(o_ref.dtype)

    def body_one(a_ref, b_ref, o_ref):
        o_ref[...] = product(a_ref, b_ref).astype(o_ref.dtype)

    body, tspec, targ = _tied(body_one if nk == 1 else body_acc, tie)
    a_spec = pl.BlockSpec((tk, tm), lambda i, j, k: (k, i)) if ta else pl.BlockSpec((tm, tk), lambda i, j, k: (i, k))
    b_spec = pl.BlockSpec((tn, tk), lambda i, j, k: (j, k)) if tb else pl.BlockSpec((tk, tn), lambda i, j, k: (k, j))
    return pl.pallas_call(
        body, name=name,
        out_shape=jax.ShapeDtypeStruct((M, N), out_dtype),
        grid=(M // tm, N // tn, nk),
        in_specs=tspec + [a_spec, b_spec],
        out_specs=pl.BlockSpec((tm, tn), lambda i, j, k: (i, j)),
        scratch_shapes=[] if nk == 1 else [pltpu.VMEM((tm, tn), F32)],
        compiler_params=_params("parallel", "parallel", "arbitrary"),
    )(*targ, a, b)


def _row_tile(S, width):
    return _tile(S, max(8, (1 << 19) // width), 8)


def _rstd(x):
    return lax.rsqrt(jnp.mean(x * x, axis=-1, keepdims=True) + EPS)


def _acc_rows(ref, val, first):
    s = jnp.sum(val, axis=0, keepdims=True)

    @pl.when(first)
    def _():
        ref[...] = s

    @pl.when(jnp.logical_not(first))
    def _():
        ref[...] += s


def _modnorm_fwd(x, g, sc, sh, *, name):
    S, D = x.shape
    tr = _row_tile(S, D)

    def body(x_ref, g_ref, sc_ref, sh_ref, h_ref):
        xv = x_ref[...]
        n = (xv * _rstd(xv)) * g_ref[...]
        h_ref[...] = (n * (1.0 + sc_ref[...]) + sh_ref[...]).astype(BF16)

    row = pl.BlockSpec((tr, D), lambda i: (i, 0))
    vec = pl.BlockSpec((1, D), lambda i: (0, 0))
    return pl.pallas_call(
        body, name=name, out_shape=jax.ShapeDtypeStruct((S, D), BF16), grid=(S // tr,),
        in_specs=[row, vec, vec, vec], out_specs=row, compiler_params=_params("parallel"),
    )(x, g, sc, sh)


def _modnorm_bwd(dh, x, g, sc, dres, *, name):
    S, D = x.shape
    tr = _row_tile(S, D)

    def body(dh_ref, x_ref, g_ref, sc_ref, dres_ref, dx_ref, dg_ref, dsc_ref, dsh_ref):
        first = pl.program_id(0) == 0
        xv = x_ref[...]
        dhv = dh_ref[...]
        gv = g_ref[...]
        r = _rstd(xv)
        xhat = xv * r
        _acc_rows(dsh_ref, dhv, first)
        _acc_rows(dsc_ref, dhv * (xhat * gv), first)
        dn = dhv * (1.0 + sc_ref[...])
        _acc_rows(dg_ref, dn * xhat, first)
        dxhat = dn * gv
        proj = jnp.mean(dxhat * xhat, axis=-1, keepdims=True)
        dx_ref[...] = r * (dxhat - xhat * proj) + dres_ref[...]

    row = pl.BlockSpec((tr, D), lambda i: (i, 0))
    vec = pl.BlockSpec((1, D), lambda i: (0, 0))
    vshape = jax.ShapeDtypeStruct((1, D), F32)
    return pl.pallas_call(
        body, name=name,
        out_shape=(jax.ShapeDtypeStruct((S, D), F32), vshape, vshape, vshape), grid=(S // tr,),
        in_specs=[row, row, vec, vec, row], out_specs=(row, vec, vec, vec),
        compiler_params=_params("arbitrary"),
    )(dh, x, g, sc, dres)


def _resnorm_fwd(xres, m, g, gt, *, name):
    S, D = xres.shape
    tr = _row_tile(S, D)

    def body(x_ref, m_ref, g_ref, gt_ref, o_ref):
        mv = m_ref[...]
        o_ref[...] = x_ref[...] + gt_ref[...] * ((mv * _rstd(mv)) * g_ref[...])

    row = pl.BlockSpec((tr, D), lambda i: (i, 0))
    vec = pl.BlockSpec((1, D), lambda i: (0, 0))
    return pl.pallas_call(
        body, name=name, out_shape=jax.ShapeDtypeStruct((S, D), F32), grid=(S // tr,),
        in_specs=[row, row, vec, vec], out_specs=row, compiler_params=_params("parallel"),
    )(xres, m, g, gt)


def _resnorm_loss(xres, m, g, gt, target, *, name):
    S, D = xres.shape
    tr = _row_tile(S, D)

    def body(x_ref, m_ref, g_ref, gt_ref, t_ref, d_ref, loss_ref):
        mv = m_ref[...]
        out = x_ref[...] + gt_ref[...] * ((mv * _rstd(mv)) * g_ref[...])
        err = out - t_ref[...]
        d_ref[...] = err * (1.0 / D)
        part = 0.5 * jnp.sum(jnp.mean(err * err, axis=-1, keepdims=True), axis=0, keepdims=True)
        part = jnp.broadcast_to(part, loss_ref.shape)

        @pl.when(pl.program_id(0) == 0)
        def _():
            loss_ref[...] = part

        @pl.when(pl.program_id(0) != 0)
        def _():
            loss_ref[...] += part

    row = pl.BlockSpec((tr, D), lambda i: (i, 0))
    vec = pl.BlockSpec((1, D), lambda i: (0, 0))
    return pl.pallas_call(
        body, name=name,
        out_shape=(jax.ShapeDtypeStruct((S, D), F32), jax.ShapeDtypeStruct((8, LANES), F32)), grid=(S // tr,),
        in_specs=[row, row, vec, vec, row], out_specs=(row, pl.BlockSpec((8, LANES), lambda i: (0, 0))),
        compiler_params=_params("arbitrary"),
    )(xres, m, g, gt, target)


def _resnorm_bwd(dout, m, g, gt, *, name):
    S, D = m.shape
    tr = _row_tile(S, D)

    def body(d_ref, m_ref, g_ref, gt_ref, dm_ref, dg_ref, dgt_ref):
        first = pl.program_id(0) == 0
        mv = m_ref[...]
        dv = d_ref[...]
        gv = g_ref[...]
        r = _rstd(mv)
        mhat = mv * r
        _acc_rows(dgt_ref, dv * (mhat * gv), first)
        dn = dv * gt_ref[...]
        _acc_rows(dg_ref, dn * mhat, first)
        dmhat = dn * gv
        proj = jnp.mean(dmhat * mhat, axis=-1, keepdims=True)
        dm_ref[...] = (r * (dmhat - mhat * proj)).astype(BF16)

    row = pl.BlockSpec((tr, D), lambda i: (i, 0))
    vec = pl.BlockSpec((1, D), lambda i: (0, 0))
    vshape = jax.ShapeDtypeStruct((1, D), F32)
    return pl.pallas_call(
        body, name=name, out_shape=(jax.ShapeDtypeStruct((S, D), BF16), vshape, vshape), grid=(S // tr,),
        in_specs=[row, row, vec, vec], out_specs=(row, vec, vec), compiler_params=_params("arbitrary"),
    )(dout, m, g, gt)


def _lat_norm_fwd(z_lat, g_q, g_kv, *, name):
    S, W = z_lat.shape
    Rq, Rkv = g_q.shape[1], g_kv.shape[1]
    tr = _row_tile(S, W)

    def body(z_ref, gq_ref, gkv_ref, nq_ref, nkv_ref):
        cq = z_ref[:, :Rq]
        ckv = z_ref[:, Rq:Rq + Rkv]
        nq_ref[...] = ((cq * _rstd(cq)) * gq_ref[...]).astype(BF16)
        nkv_ref[...] = ((ckv * _rstd(ckv)) * gkv_ref[...]).astype(BF16)

    return pl.pallas_call(
        body, name=name,
        out_shape=(jax.ShapeDtypeStruct((S, Rq), BF16), jax.ShapeDtypeStruct((S, Rkv), BF16)), grid=(S // tr,),
        in_specs=[pl.BlockSpec((tr, W), lambda i: (i, 0)), pl.BlockSpec((1, Rq), lambda i: (0, 0)),
                  pl.BlockSpec((1, Rkv), lambda i: (0, 0))],
        out_specs=(pl.BlockSpec((tr, Rq), lambda i: (i, 0)), pl.BlockSpec((tr, Rkv), lambda i: (i, 0))),
        compiler_params=_params("parallel"),
    )(z_lat, g_q, g_kv)


def _lat_norm_bwd(z_lat, dnq, dnkv, dkr, g_q, g_kv, *, name):
    S, W = z_lat.shape
    Rq, Rkv = g_q.shape[1], g_kv.shape[1]
    tr = _row_tile(S, W)

    def one(c, dn, gv):
        r = _rstd(c)
        chat = c * r
        dchat = dn * gv
        proj = jnp.mean(dchat * chat, axis=-1, keepdims=True)
        return r * (dchat - chat * proj), dn * chat

    def body(z_ref, dnq_ref, dnkv_ref, dkr_ref, gq_ref, gkv_ref, dz_ref, dgq_ref, dgkv_ref):
        first = pl.program_id(0) == 0
        dcq, pq = one(z_ref[:, :Rq], dnq_ref[...], gq_ref[...])
        dckv, pkv = one(z_ref[:, Rq:Rq + Rkv], dnkv_ref[...], gkv_ref[...])
        _acc_rows(dgq_ref, pq, first)
        _acc_rows(dgkv_ref, pkv, first)
        dz_ref[:, :Rq] = dcq.astype(BF16)
        dz_ref[:, Rq:Rq + Rkv] = dckv.astype(BF16)
        dz_ref[:, Rq + Rkv:] = dkr_ref[...].astype(BF16)

    return pl.pallas_call(
        body, name=name,
        out_shape=(jax.ShapeDtypeStruct((S, W), BF16), jax.ShapeDtypeStruct((1, Rq), F32),
                   jax.ShapeDtypeStruct((1, Rkv), F32)), grid=(S // tr,),
        in_specs=[pl.BlockSpec((tr, W), lambda i: (i, 0)), pl.BlockSpec((tr, Rq), lambda i: (i, 0)),
                  pl.BlockSpec((tr, Rkv), lambda i: (i, 0)), pl.BlockSpec((tr, LANES), lambda i: (i, 0)),
                  pl.BlockSpec((1, Rq), lambda i: (0, 0)), pl.BlockSpec((1, Rkv), lambda i: (0, 0))],
        out_specs=(pl.BlockSpec((tr, W), lambda i: (i, 0)), pl.BlockSpec((1, Rq), lambda i: (0, 0)),
                   pl.BlockSpec((1, Rkv), lambda i: (0, 0))),
        compiler_params=_params("arbitrary"),
    )(z_lat, dnq, dnkv, dkr, g_q, g_kv)


def _rot(x, lo32):
    a = pltpu.roll(x, 32, 1)
    b = pltpu.roll(x, LANES - 32, 1)
    return jnp.where(lo32, -b, a)


def _rot_t(g, lo32):
    a = pltpu.roll(g, 32, 1)
    b = pltpu.roll(g, LANES - 32, 1)
    return jnp.where(lo32, b, -a)


def _mla_pack_fwd(q_raw, kv_raw, z_lat, cos, sin, kr_off, *, name):
    S = q_raw.shape[0]
    H = kv_raw.shape[1] // (MLA_NOPE + MLA_V)
    W = z_lat.shape[1]
    scale = MLA_QK ** -0.5
    tr = min(S, 128)
    nope_w = H * MLA_NOPE

    def body(q_ref, kv_ref, z_ref, cos_ref, sin_ref, qp_ref, kp_ref, v_ref):
        lane = lax.broadcasted_iota(jnp.int32, (tr, LANES), 1)
        lo32 = (lane % 64) < 32
        lo64 = lane < 64
        c = cos_ref[...]
        s = sin_ref[...]
        kr = z_ref[:, kr_off:kr_off + LANES]
        kr = (kr * c + _rot(kr, lo32) * s).astype(BF16)
        for hp in range(H // 2):
            xb = q_ref[:, nope_w + hp * LANES:nope_w + (hp + 1) * LANES]
            rb = (xb * c + _rot(xb, lo32) * s) * scale
            for e in range(2):
                h = 2 * hp + e
                base = h * MLA_QK_PAD
                qp_ref[:, base:base + LANES] = (q_ref[:, h * LANES:(h + 1) * LANES] * scale).astype(BF16)
                keep = lo64 if e == 0 else jnp.logical_not(lo64)
                qp_ref[:, base + LANES:base + 2 * LANES] = jnp.where(keep, rb, 0.0).astype(BF16)
                kp_ref[:, base:base + LANES] = kv_ref[:, h * LANES:(h + 1) * LANES].astype(BF16)
                kp_ref[:, base + LANES:base + 2 * LANES] = kr
        v_ref[...] = kv_ref[:, nope_w:].astype(BF16)

    return pl.pallas_call(
        body, name=name,
        out_shape=(jax.ShapeDtypeStruct((S, H * MLA_QK_PAD), BF16), jax.ShapeDtypeStruct((S, H * MLA_QK_PAD), BF16),
                   jax.ShapeDtypeStruct((S, H * MLA_V), BF16)), grid=(S // tr,),
        in_specs=[pl.BlockSpec((tr, q_raw.shape[1]), lambda i: (i, 0)), pl.BlockSpec((tr, kv_raw.shape[1]), lambda i: (i, 0)),
                  pl.BlockSpec((tr, W), lambda i: (i, 0)), pl.BlockSpec((tr, LANES), lambda i: (i, 0)),
                  pl.BlockSpec((tr, LANES), lambda i: (i, 0))],
        out_specs=(pl.BlockSpec((tr, H * MLA_QK_PAD), lambda i: (i, 0)), pl.BlockSpec((tr, H * MLA_QK_PAD), lambda i: (i, 0)),
                   pl.BlockSpec((tr, H * MLA_V), lambda i: (i, 0))),
        compiler_params=_params("parallel"),
    )(q_raw, kv_raw, z_lat, cos, sin)


def _mla_pack_bwd(dqp, dkp, dv, cos, sin, *, name):
    S = dqp.shape[0]
    H = dv.shape[1] // MLA_V
    scale = MLA_QK ** -0.5
    tr = min(S, 128)
    nope_w = H * MLA_NOPE

    def body(dqp_ref, dkp_ref, dv_ref, cos_ref, sin_ref, dq_ref, dkv_ref, dkr_ref):
        lane = lax.broadcasted_iota(jnp.int32, (tr, LANES), 1)
        lo32 = (lane % 64) < 32
        lo64 = lane < 64
        c = cos_ref[...]
        s = sin_ref[...]
        dkr2 = jnp.zeros((tr, LANES), F32)
        for hp in range(H // 2):
            be = (2 * hp) * MLA_QK_PAD
            bo = (2 * hp + 1) * MLA_QK_PAD
            g = jnp.where(lo64, dqp_ref[:, be + LANES:be + 2 * LANES], dqp_ref[:, bo + LANES:bo + 2 * LANES]) * scale
            dq_ref[:, nope_w + hp * LANES:nope_w + (hp + 1) * LANES] = (g * c + _rot_t(g * s, lo32)).astype(BF16)
            for h, base in ((2 * hp, be), (2 * hp + 1, bo)):
                dq_ref[:, h * LANES:(h + 1) * LANES] = (dqp_ref[:, base:base + LANES] * scale).astype(BF16)
                dkv_ref[:, h * LANES:(h + 1) * LANES] = dkp_ref[:, base:base + LANES].astype(BF16)
                dkr2 = dkr2 + dkp_ref[:, base + LANES:base + 2 * LANES]
        dkr2 = dkr2 * c + _rot_t(dkr2 * s, lo32)
        dkr2 = dkr2 + pltpu.roll(dkr2, 64, 1)
        dkr_ref[...] = jnp.where(lo64, dkr2, 0.0)
        dkv_ref[:, nope_w:] = dv_ref[...].astype(BF16)

    return pl.pallas_call(
        body, name=name,
        out_shape=(jax.ShapeDtypeStruct((S, nope_w + H * MLA_ROPE), BF16), jax.ShapeDtypeStruct((S, 2 * nope_w), BF16),
                   jax.ShapeDtypeStruct((S, LANES), F32)), grid=(S // tr,),
        in_specs=[pl.BlockSpec((tr, H * MLA_QK_PAD), lambda i: (i, 0)), pl.BlockSpec((tr, H * MLA_QK_PAD), lambda i: (i, 0)),
                  pl.BlockSpec((tr, H * MLA_V), lambda i: (i, 0)), pl.BlockSpec((tr, LANES), lambda i: (i, 0)),
                  pl.BlockSpec((tr, LANES), lambda i: (i, 0))],
        out_specs=(pl.BlockSpec((tr, nope_w + H * MLA_ROPE), lambda i: (i, 0)), pl.BlockSpec((tr, 2 * nope_w), lambda i: (i, 0)),
                   pl.BlockSpec((tr, LANES), lambda i: (i, 0))),
        compiler_params=_params("parallel"),
    )(dqp, dkp, dv, cos, sin)


FLASH_HB = 2


def _causal_pairs(nb):
    qi = [i for i in range(nb) for j in range(i + 1)]
    kj = [j for i in range(nb) for j in range(i + 1)]
    return jnp.asarray(qi, jnp.int32), jnp.asarray(kj, jnp.int32)


def _scores(q, k, diagonal, t):
    s = lax.dot_general(q, k, (((1,), (1,)), ((), ())), preferred_element_type=F32)
    if diagonal:
        row = lax.broadcasted_iota(jnp.int32, (t, t), 0)
        col = lax.broadcasted_iota(jnp.int32, (t, t), 1)
        s = jnp.where(col <= row, s, NEG)
    return s


def _flash_fwd(qp, kp, v, *, name):
    S = qp.shape[0]
    H = v.shape[1] // MLA_V
    t = min(S, 512)
    nb = S // t
    HB = FLASH_HB
    qi, kj = _causal_pairs(nb)
    QW, VW = MLA_QK_PAD, MLA_V

    def body(qi_ref, kj_ref, q_ref, k_ref, v_ref, o_ref, lse_ref, m_s, l_s, acc_s):
        pr = pl.program_id(1)
        i = qi_ref[pr]
        j = kj_ref[pr]

        @pl.when(j == 0)
        def _():
            m_s[...] = jnp.full_like(m_s, NEG)
            l_s[...] = jnp.zeros_like(l_s)
            acc_s[...] = jnp.zeros_like(acc_s)

        def step(diagonal):
            for hh in range(HB):
                s = _scores(q_ref[:, hh * QW:(hh + 1) * QW], k_ref[:, hh * QW:(hh + 1) * QW], diagonal, t)
                m_prev = m_s[hh]
                m_cur = jnp.maximum(m_prev, jnp.max(s, axis=1, keepdims=True))
                alpha = jnp.exp(m_prev - m_cur)
                p = jnp.exp(s - m_cur[:, :1])
                l_new = alpha * l_s[hh] + jnp.sum(p, axis=1, keepdims=True)
                acc = alpha * acc_s[hh] + jnp.dot(p.astype(BF16), v_ref[:, hh * VW:(hh + 1) * VW], preferred_element_type=F32)
                if diagonal:
                    o_ref[:, hh * VW:(hh + 1) * VW] = acc / l_new
                    lse_ref[hh] = m_cur + jnp.log(l_new)
                else:
                    l_s[hh] = l_new
                    acc_s[hh] = acc
                    m_s[hh] = m_cur

        @pl.when(i != j)
        def _():
            step(False)

        @pl.when(i == j)
        def _():
            step(True)

    return pl.pallas_call(
        body, name=name,
        out_shape=(jax.ShapeDtypeStruct((S, H * VW), F32), jax.ShapeDtypeStruct((H, S, LANES), F32)),
        grid_spec=pltpu.PrefetchScalarGridSpec(
            num_scalar_prefetch=2, grid=(H // HB, qi.shape[0]),
            in_specs=[pl.BlockSpec((t, HB * QW), lambda g, p, qi, kj: (qi[p], g)),
                      pl.BlockSpec((t, HB * QW), lambda g, p, qi, kj: (kj[p], g)),
                      pl.BlockSpec((t, HB * VW), lambda g, p, qi, kj: (kj[p], g))],
            out_specs=(pl.BlockSpec((t, HB * VW), lambda g, p, qi, kj: (qi[p], g)),
                       pl.BlockSpec((HB, t, LANES), lambda g, p, qi, kj: (g, qi[p], 0))),
            scratch_shapes=[pltpu.VMEM((HB, t, LANES), F32), pltpu.VMEM((HB, t, LANES), F32), pltpu.VMEM((HB, t, VW), F32)]),
        compiler_params=_params("parallel", "arbitrary"),
    )(qi, kj, qp, kp, v)


def _flash_bwd(qp, kp, v, o, do, lse, *, name):
    S = qp.shape[0]
    H = v.shape[1] // MLA_V
    t = min(S, 512)
    nb = S // t
    HB = FLASH_HB
    qi, kj = _causal_pairs(nb)
    QW, VW = MLA_QK_PAD, MLA_V
    tn = (((0,), (0,)), ((), ()))
    nt = (((1,), (1,)), ((), ()))

    def body(qi_ref, kj_ref, q_ref, k_ref, v_ref, o_ref, do_ref, lse_ref, dq_ref, dk_ref, dv_ref, dq_s):
        pr = pl.program_id(1)
        i = qi_ref[pr]
        j = kj_ref[pr]
        rows = pl.ds(pl.multiple_of(j * t, t), t)

        @pl.when(pr == 0)
        def _():
            dk_ref[...] = jnp.zeros_like(dk_ref)
            dv_ref[...] = jnp.zeros_like(dv_ref)

        @pl.when(j == 0)
        def _():
            dq_s[...] = jnp.zeros_like(dq_s)

        def step(diagonal):
            for hh in range(HB):
                q = q_ref[:, hh * QW:(hh + 1) * QW]
                k = k_ref[:, hh * QW:(hh + 1) * QW]
                dob = do_ref[:, hh * VW:(hh + 1) * VW]
                p = jnp.exp(_scores(q, k, diagonal, t) - lse_ref[hh][:, :1])
                delta = jnp.sum(dob.astype(F32) * o_ref[:, hh * VW:(hh + 1) * VW], axis=1, keepdims=True)
                dp = lax.dot_general(dob, v_ref[:, hh * VW:(hh + 1) * VW], nt, preferred_element_type=F32)
                dsb = (p * (dp - delta)).astype(BF16)
                dv_ref[rows, hh * VW:(hh + 1) * VW] += lax.dot_general(p.astype(BF16), dob, tn, preferred_element_type=F32)
                dk_ref[rows, hh * QW:(hh + 1) * QW] += lax.dot_general(dsb, q, tn, preferred_element_type=F32)
                dq = dq_s[:, hh * QW:(hh + 1) * QW] + jnp.dot(dsb, k, preferred_element_type=F32)
                if diagonal:
                    dq_ref[:, hh * QW:(hh + 1) * QW] = dq
                else:
                    dq_s[:, hh * QW:(hh + 1) * QW] = dq

        @pl.when(i != j)
        def _():
            step(False)

        @pl.when(i == j)
        def _():
            step(True)

    qside = lambda g, p, qi, kj: (qi[p], g)
    kside = lambda g, p, qi, kj: (kj[p], g)
    whole = lambda g, p, qi, kj: (0, g)
    return pl.pallas_call(
        body, name=name,
        out_shape=(jax.ShapeDtypeStruct((S, H * QW), F32), jax.ShapeDtypeStruct((S, H * QW), F32),
                   jax.ShapeDtypeStruct((S, H * VW), F32)),
        grid_spec=pltpu.PrefetchScalarGridSpec(
            num_scalar_prefetch=2, grid=(H // HB, qi.shape[0]),
            in_specs=[pl.BlockSpec((t, HB * QW), qside), pl.BlockSpec((t, HB * QW), kside), pl.BlockSpec((t, HB * VW), kside),
                      pl.BlockSpec((t, HB * VW), qside), pl.BlockSpec((t, HB * VW), qside),
                      pl.BlockSpec((HB, t, LANES), lambda g, p, qi, kj: (g, qi[p], 0))],
            out_specs=(pl.BlockSpec((t, HB * QW), qside), pl.BlockSpec((S, HB * QW), whole), pl.BlockSpec((S, HB * VW), whole)),
            scratch_shapes=[pltpu.VMEM((t, HB * QW), F32)]),
        compiler_params=_params("parallel", "arbitrary"),
    )(qi, kj, qp, kp, v, o, do, lse)


def _swa_kv_halves(blk, hf, lo):
    if hf == 0:
        a = jnp.where(lo, blk, 0.0)
        b = pltpu.roll(a, 64, 1)
    else:
        b = jnp.where(lo, 0.0, blk)
        a = pltpu.roll(b, 64, 1)
    return a.astype(BF16), b.astype(BF16)


def _swa_softmax(qb, kx, bias, neg0, sk):
    s = lax.dot_general(qb, kx, (((1,), (1,)), ((), ())), preferred_element_type=F32) + bias + neg0
    m = jnp.maximum(jnp.max(s, axis=1, keepdims=True), sk)
    e = jnp.exp(s - m)
    es = jnp.exp(sk - m)
    inv = 1.0 / (jnp.sum(e, axis=1, keepdims=True) + es)
    return e * inv, es * inv


def _swa_fwd(z_swa, bias_m, sinks, *, name):
    S, W = z_swa.shape
    NH = bias_m.shape[0]
    G = NH // SWA_KVH
    QW = NH * SWA_HD
    KW = SWA_KVH * SWA_HD
    nb = S // SWA_BLOCK
    B = SWA_BLOCK
    assert G % 2 == 0 and SWA_KVH % 2 == 0 and W == QW + 2 * KW

    def body(sink_ref, q_ref, kvc_ref, kvp_ref, b_ref, o_ref):
        n = pl.program_id(0)
        lo = lax.broadcasted_iota(jnp.int32, (2 * B, LANES), 1) < 64
        col = lax.broadcasted_iota(jnp.int32, (B, 2 * B), 1)
        neg0 = jnp.where(jnp.logical_and(col < B, n == 0), NEG, 0.0)
        for kb in range(SWA_KVH // 2):
            kblk = jnp.concatenate([kvp_ref[:, kb * LANES:(kb + 1) * LANES], kvc_ref[:, kb * LANES:(kb + 1) * LANES]], axis=0)
            vblk = jnp.concatenate([kvp_ref[:, KW + kb * LANES:KW + (kb + 1) * LANES],
                                    kvc_ref[:, KW + kb * LANES:KW + (kb + 1) * LANES]], axis=0)
            for hf in range(2):
                kvh = 2 * kb + hf
                ks = _swa_kv_halves(kblk, hf, lo)
                vs = _swa_kv_halves(vblk, hf, lo)
                for pb in range(G // 2):
                    P = kvh * (G // 2) + pb
                    qb = (q_ref[:, P * LANES:(P + 1) * LANES] * (SWA_HD ** -0.5)).astype(BF16)
                    acc = jnp.zeros((B, LANES), F32)
                    for e in range(2):
                        h = 2 * P + e
                        p, _ = _swa_softmax(qb, ks[e], b_ref[h], neg0, sink_ref[h])
                        acc = acc + jnp.dot(p.astype(BF16), vs[e], preferred_element_type=F32)
                    o_ref[:, P * LANES:(P + 1) * LANES] = acc

    kvcol = QW // (2 * KW)
    assert QW % (2 * KW) == 0
    return pl.pallas_call(
        body, name=name,
        out_shape=jax.ShapeDtypeStruct((S, QW), F32), grid=(nb,),
        in_specs=[SMEM_FULL, pl.BlockSpec((B, QW), lambda n: (n, 0)), pl.BlockSpec((B, 2 * KW), lambda n: (n, kvcol)),
                  pl.BlockSpec((B, 2 * KW), lambda n: (jnp.maximum(n - 1, 0), kvcol)),
                  pl.BlockSpec((NH, B, 2 * B), lambda n: (0, 0, 0))],
        out_specs=pl.BlockSpec((B, QW), lambda n: (n, 0)),
        compiler_params=_params("parallel"),
    )(sinks, z_swa, z_swa, z_swa, bias_m)


def _swa_bwd(z_swa, bias_m, sinks, o, do, *, name):
    S, W = z_swa.shape
    NH = bias_m.shape[0]
    G = NH // SWA_KVH
    QW = NH * SWA_HD
    KW = SWA_KVH * SWA_HD
    nb = S // SWA_BLOCK
    B = SWA_BLOCK
    scale = SWA_HD ** -0.5
    tn = (((0,), (0,)), ((), ()))
    nt = (((1,), (1,)), ((), ()))

    def fold(x, hf, lo):
        x = x + pltpu.roll(x, 64, 1)
        return jnp.where(lo, x, 0.0) if hf == 0 else jnp.where(lo, 0.0, x)

    def body(sink_ref, q_ref, kvc_ref, kvp_ref, b_ref, o_ref, do_ref, dz_ref, dbias_ref, dsink_ref,
             cq_s, ck_s, cv_s, nq_s, nk_s, nv_s, pk_s, pv_s):
        n = pl.program_id(0)

        @pl.when(n == 0)
        def _():
            dbias_ref[...] = jnp.zeros_like(dbias_ref)
            dsink_ref[...] = jnp.zeros_like(dsink_ref)
            cq_s[...] = jnp.zeros_like(cq_s)
            ck_s[...] = jnp.zeros_like(ck_s)
            cv_s[...] = jnp.zeros_like(cv_s)

        @pl.when(n == nb)
        def _():
            pk_s[...] = jnp.zeros_like(pk_s)
            pv_s[...] = jnp.zeros_like(pv_s)

        @pl.when(n < nb)
        def _():
            lo = lax.broadcasted_iota(jnp.int32, (2 * B, LANES), 1) < 64
            lo1 = lax.broadcasted_iota(jnp.int32, (B, LANES), 1) < 64
            lane1 = lax.broadcasted_iota(jnp.int32, (1, LANES), 1)
            col = lax.broadcasted_iota(jnp.int32, (B, 2 * B), 1)
            neg0 = jnp.where(jnp.logical_and(col < B, n == 0), NEG, 0.0)
            dsink = jnp.zeros((1, LANES), F32)
            for kb in range(SWA_KVH // 2):
                kblk = jnp.concatenate([kvp_ref[:, kb * LANES:(kb + 1) * LANES], kvc_ref[:, kb * LANES:(kb + 1) * LANES]], axis=0)
                vblk = jnp.concatenate([kvp_ref[:, KW + kb * LANES:KW + (kb + 1) * LANES],
                                        kvc_ref[:, KW + kb * LANES:KW + (kb + 1) * LANES]], axis=0)
                dkblk = jnp.zeros((2 * B, LANES), F32)
                dvblk = jnp.zeros((2 * B, LANES), F32)
                for hf in range(2):
                    kvh = 2 * kb + hf
                    ks = _swa_kv_halves(kblk, hf, lo)
                    vs = _swa_kv_halves(vblk, hf, lo)
                    dkj = jnp.zeros((2 * B, LANES), F32)
                    dvj = jnp.zeros((2 * B, LANES), F32)
                    for pb in range(G // 2):
                        P = kvh * (G // 2) + pb
                        qb = (q_ref[:, P * LANES:(P + 1) * LANES] * scale).astype(BF16)
                        dob = do_ref[:, P * LANES:(P + 1) * LANES]
                        prod = dob * o_ref[:, P * LANES:(P + 1) * LANES]
                        dob = dob.astype(BF16)
                        dqp = jnp.zeros((B, LANES), F32)
                        for e in range(2):
                            h = 2 * P + e
                            keep = lo1 if e == 0 else jnp.logical_not(lo1)
                            p, psink = _swa_softmax(qb, ks[e], b_ref[h], neg0, sink_ref[h])
                            delta = jnp.sum(jnp.where(keep, prod, 0.0), axis=1, keepdims=True)
                            dp = lax.dot_general(dob, vs[e], nt, preferred_element_type=F32)
                            ds = p * (dp - delta)
                            dbias_ref[h] += ds
                            dsh = -jnp.sum(psink * delta, axis=0, keepdims=True)
                            dsink = dsink + jnp.where(lane1 == h, dsh, 0.0)
                            dsb = ds.astype(BF16)
                            dqp = dqp + jnp.dot(dsb, ks[e], preferred_element_type=F32)
                            keep2 = lo if e == 0 else jnp.logical_not(lo)
                            dkj = dkj + jnp.where(keep2, lax.dot_general(dsb, qb, tn, preferred_element_type=F32), 0.0)
                            dvj = dvj + jnp.where(keep2, lax.dot_general(p.astype(BF16), dob, tn, preferred_element_type=F32), 0.0)
                        nq_s[:, P * LANES:(P + 1) * LANES] = dqp * scale
                    dkblk = dkblk + fold(dkj, hf, lo)
                    dvblk = dvblk + fold(dvj, hf, lo)
                pk_s[:, kb * LANES:(kb + 1) * LANES] = dkblk[:B]
                nk_s[:, kb * LANES:(kb + 1) * LANES] = dkblk[B:]
                pv_s[:, kb * LANES:(kb + 1) * LANES] = dvblk[:B]
                nv_s[:, kb * LANES:(kb + 1) * LANES] = dvblk[B:]
            dsink_ref[...] += dsink

        dz_ref[:, :QW] = cq_s[...].astype(BF16)
        dz_ref[:, QW:QW + KW] = (ck_s[...] + pk_s[...]).astype(BF16)
        dz_ref[:, QW + KW:] = (cv_s[...] + pv_s[...]).astype(BF16)

        @pl.when(n < nb)
        def _():
            cq_s[...] = nq_s[...]
            ck_s[...] = nk_s[...]
            cv_s[...] = nv_s[...]

    kvcol = QW // (2 * KW)
    cur = lambda n: (jnp.minimum(n, nb - 1), 0)
    return pl.pallas_call(
        body, name=name,
        out_shape=(jax.ShapeDtypeStruct((S, W), BF16), jax.ShapeDtypeStruct((NH, B, 2 * B), F32),
                   jax.ShapeDtypeStruct((1, LANES), F32)),
        grid=(nb + 1,),
        in_specs=[SMEM_FULL, pl.BlockSpec((B, QW), cur), pl.BlockSpec((B, 2 * KW), lambda n: (jnp.minimum(n, nb - 1), kvcol)),
                  pl.BlockSpec((B, 2 * KW), lambda n: (jnp.maximum(jnp.minimum(n, nb - 1) - 1, 0), kvcol)),
                  pl.BlockSpec((NH, B, 2 * B), lambda n: (0, 0, 0)), pl.BlockSpec((B, QW), cur), pl.BlockSpec((B, QW), cur)],
        out_specs=(pl.BlockSpec((B, W), lambda n: (jnp.maximum(n - 1, 0), 0)),
                   pl.BlockSpec((NH, B, 2 * B), lambda n: (0, 0, 0)), pl.BlockSpec((1, LANES), lambda n: (0, 0))),
        scratch_shapes=[pltpu.VMEM((B, QW), F32), pltpu.VMEM((B, KW), F32), pltpu.VMEM((B, KW), F32),
                        pltpu.VMEM((B, QW), F32), pltpu.VMEM((B, KW), F32), pltpu.VMEM((B, KW), F32),
                        pltpu.VMEM((B, KW), F32), pltpu.VMEM((B, KW), F32)],
        compiler_params=_params("arbitrary"),
    )(sinks, z_swa, z_swa, z_swa, bias_m, o, do)


def _gate_fwd(zg, o_a, o_b, *, name):
    S, D = o_a.shape
    tr = min(S, 512)

    def body(z_ref, a_ref, b_ref, m_ref):
        ga = jax.nn.sigmoid(z_ref[:, :PAIR])
        gb = jax.nn.sigmoid(z_ref[:, PAIR:])
        m_ref[...] = (ga * a_ref[...] + gb * b_ref[...]).astype(BF16)

    col = pl.BlockSpec((tr, PAIR), lambda i, j: (i, j))
    return pl.pallas_call(
        body, name=name, out_shape=jax.ShapeDtypeStruct((S, D), BF16), grid=(S // tr, D // PAIR),
        in_specs=[pl.BlockSpec((tr, 2 * PAIR), lambda i, j: (i, j)), col, col], out_specs=col,
        compiler_params=_params("parallel", "parallel"),
    )(zg, o_a, o_b)


def _gate_bwd(dmix, zg, o_a, o_b, *, name):
    S, D = o_a.shape
    tr = min(S, 512)

    def body(d_ref, z_ref, a_ref, b_ref, da_ref, db_ref, dz_ref):
        d = d_ref[...]
        ga = jax.nn.sigmoid(z_ref[:, :PAIR])
        gb = jax.nn.sigmoid(z_ref[:, PAIR:])
        da_ref[...] = (d * ga).astype(BF16)
        db_ref[...] = d * gb
        dz_ref[:, :PAIR] = (d * a_ref[...] * (ga * (1.0 - ga))).astype(BF16)
        dz_ref[:, PAIR:] = (d * b_ref[...] * (gb * (1.0 - gb))).astype(BF16)

    col = pl.BlockSpec((tr, PAIR), lambda i, j: (i, j))
    wide = pl.BlockSpec((tr, 2 * PAIR), lambda i, j: (i, j))
    return pl.pallas_call(
        body, name=name,
        out_shape=(jax.ShapeDtypeStruct((S, D), BF16), jax.ShapeDtypeStruct((S, D), F32), jax.ShapeDtypeStruct((S, 2 * D), BF16)),
        grid=(S // tr, D // PAIR), in_specs=[col, wide, col, col], out_specs=(col, col, wide),
        compiler_params=_params("parallel", "parallel"),
    )(dmix, zg, o_a, o_b)


def _conv_taps(t_ref, prev_ref, i, tr):
    cur = t_ref[...]
    live = (i > 0).astype(F32)
    p6 = prev_ref[6:7, :] * live
    p7 = prev_ref[7:8, :] * live
    row = lax.broadcasted_iota(jnp.int32, cur.shape, 0)
    t1 = jnp.where(row == 0, p7, pltpu.roll(cur, 1, 0))
    t2 = jnp.where(row == 0, p6, jnp.where(row == 1, p7, pltpu.roll(cur, 2, 0)))
    return cur, t1, t2


def _conv_u(t_ref, prev_ref, w_ref, b_ref, i, tr):
    cur, t1, t2 = _conv_taps(t_ref, prev_ref, i, tr)
    u = ((b_ref[...] + w_ref[0:1, :] * t2) + w_ref[1:2, :] * t1) + w_ref[2:3, :] * cur
    return u, cur, t1, t2


def _conv_specs(tr, S):
    blk = pl.BlockSpec((tr, 2 * PAIR), lambda j, i: (i, j))
    prev = pl.BlockSpec((8, 2 * PAIR), lambda j, i: (jnp.maximum(i * (tr // 8) - 1, 0), j))
    w3 = pl.BlockSpec((3, 2 * PAIR), lambda j, i: (0, j))
    w1 = pl.BlockSpec((1, 2 * PAIR), lambda j, i: (0, j))
    return blk, prev, w3, w1


def _conv_gate_fwd(t, cw, cb, *, name):
    S, F2 = t.shape
    tr = min(S, 512)
    blk, prev, w3, w1 = _conv_specs(tr, S)

    def body(t_ref, prev_ref, w_ref, b_ref, a_ref):
        u, _, _, _ = _conv_u(t_ref, prev_ref, w_ref, b_ref, pl.program_id(1), tr)
        a_ref[...] = (jax.nn.silu(u[:, :PAIR]) * u[:, PAIR:]).astype(BF16)

    return pl.pallas_call(
        body, name=name, out_shape=jax.ShapeDtypeStruct((S, F2 // 2), BF16), grid=(F2 // (2 * PAIR), S // tr),
        in_specs=[blk, prev, w3, w1], out_specs=pl.BlockSpec((tr, PAIR), lambda j, i: (i, j)),
        compiler_params=_params("parallel", "parallel"),
    )(t, t, cw, cb)


def _conv_gate_bwd(t, da, cw, cb, *, name):
    S, F2 = t.shape
    tr = min(S, 256)
    blk, prev, w3, w1 = _conv_specs(tr, S)

    def body(t_ref, prev_ref, da_ref, w_ref, b_ref, du_ref, dw_ref, db_ref):
        i = pl.program_id(1)
        u, cur, t1, t2 = _conv_u(t_ref, prev_ref, w_ref, b_ref, i, tr)
        u1 = u[:, :PAIR]
        u2 = u[:, PAIR:]
        d = da_ref[...]
        sg = jax.nn.sigmoid(u1)
        du1 = d * u2 * (sg * (1.0 + u1 * (1.0 - sg)))
        du2 = d * (u1 * sg)
        du = jnp.concatenate([du1, du2], axis=1)
        du_ref[...] = du.astype(BF16)
        first = i == 0
        _acc_rows(db_ref, du, first)
        dw = jnp.concatenate([jnp.sum(du * t2, axis=0, keepdims=True), jnp.sum(du * t1, axis=0, keepdims=True),
                              jnp.sum(du * cur, axis=0, keepdims=True)], axis=0)

        @pl.when(first)
        def _():
            dw_ref[...] = dw

        @pl.when(jnp.logical_not(first))
        def _():
            dw_ref[...] += dw

    return pl.pallas_call(
        body, name=name,
        out_shape=(jax.ShapeDtypeStruct((S, F2), BF16), jax.ShapeDtypeStruct((3, F2), F32), jax.ShapeDtypeStruct((1, F2), F32)),
        grid=(F2 // (2 * PAIR), S // tr),
        in_specs=[blk, prev, pl.BlockSpec((tr, PAIR), lambda j, i: (i, j)), w3, w1], out_specs=(blk, w3, w1),
        compiler_params=_params("parallel", "arbitrary"),
    )(t, t, da, cw, cb)


def _conv_bwd_dt(du, cw, *, name):
    S, F2 = du.shape
    tr = min(S, 512)
    nrow = S // tr
    blk, _, w3, _ = _conv_specs(tr, S)
    nxt = pl.BlockSpec((16, 2 * PAIR), lambda j, i: (jnp.minimum((i + 1) * (tr // 16), S // 16 - 1), j))

    def body(d_ref, next_ref, w_ref, dt_ref):
        i = pl.program_id(1)
        cur = d_ref[...].astype(F32)
        live = (i < nrow - 1).astype(F32)
        n0 = next_ref[0:1, :].astype(F32) * live
        n1 = next_ref[1:2, :].astype(F32) * live
        row = lax.broadcasted_iota(jnp.int32, cur.shape, 0)
        d1 = jnp.where(row == tr - 1, n0, pltpu.roll(cur, tr - 1, 0))
        d2 = jnp.where(row == tr - 1, n1, jnp.where(row == tr - 2, n0, pltpu.roll(cur, tr - 2, 0)))
        dt_ref[...] = ((w_ref[2:3, :] * cur + w_ref[1:2, :] * d1) + w_ref[0:1, :] * d2).astype(BF16)

    return pl.pallas_call(
        body, name=name, out_shape=jax.ShapeDtypeStruct((S, F2), BF16), grid=(F2 // (2 * PAIR), nrow),
        in_specs=[blk, nxt, w3], out_specs=blk, compiler_params=_params("parallel", "parallel"),
    )(du, du, cw)


def _ada_fwd(c_all, w, b, *, name):
    Bn, D = c_all.shape
    N = w.shape[1]
    tn = _tile(N, 512)

    def body(c_ref, w_ref, b_ref, o_ref):
        o_ref[...] = jnp.dot(jax.nn.silu(c_ref[...]), w_ref[...], preferred_element_type=F32, precision=HIGHEST) + b_ref[...]

    return pl.pallas_call(
        body, name=name, out_shape=jax.ShapeDtypeStruct((Bn, N), F32), grid=(N // tn,),
        in_specs=[pl.BlockSpec((Bn, D), lambda j: (0, 0)), pl.BlockSpec((D, tn), lambda j: (0, j)),
                  pl.BlockSpec((1, tn), lambda j: (0, j))],
        out_specs=pl.BlockSpec((Bn, tn), lambda j: (0, j)), compiler_params=_params("parallel"),
    )(c_all, w, b)


def _ada_bwd(c_all_t, dmod, *, name):
    D, Bn = c_all_t.shape
    N = dmod.shape[1]
    tm = _tile(D, 512, 8)
    tn = _tile(N, 1536)

    def body(c_ref, d_ref, o_ref):
        o_ref[...] = jnp.dot(jax.nn.silu(c_ref[...]), d_ref[...], preferred_element_type=F32, precision=HIGHEST)

    return pl.pallas_call(
        body, name=name, out_shape=jax.ShapeDtypeStruct((D, N), F32), grid=(D // tm, N // tn),
        in_specs=[pl.BlockSpec((tm, Bn), lambda i, j: (i, 0)), pl.BlockSpec((Bn, tn), lambda i, j: (0, j))],
        out_specs=pl.BlockSpec((tm, tn), lambda i, j: (i, j)), compiler_params=_params("parallel", "parallel"),
    )(c_all_t, dmod)


def _adamw(w, g, m, v, *, name):
    R, C = w.shape
    tr = R if R * C <= (1 << 18) else _tile(R, max(8, (1 << 18) // C), 8)

    def body(w_ref, g_ref, m_ref, v_ref, d_ref, nm_ref, nv_ref):
        gv = g_ref[...]
        nm = ADAM_B1 * m_ref[...] + (1.0 - ADAM_B1) * gv
        nv = ADAM_B2 * v_ref[...] + (1.0 - ADAM_B2) * (gv * gv)
        m_hat = nm / (1.0 - ADAM_B1 ** ADAM_STEP)
        v_hat = nv / (1.0 - ADAM_B2 ** ADAM_STEP)
        d_ref[...] = -ADAM_LR * (m_hat / (jnp.sqrt(v_hat) + ADAM_EPS) + ADAM_WD * w_ref[...])
        nm_ref[...] = nm
        nv_ref[...] = nv

    blk = pl.BlockSpec((tr, C), lambda i: (i, 0))
    shp = jax.ShapeDtypeStruct((R, C), F32)
    return pl.pallas_call(
        body, name=name, out_shape=(shp, shp, shp), grid=(R // tr,), in_specs=[blk] * 4, out_specs=(blk,) * 3,
        compiler_params=_params("parallel"),
    )(w, g, m, v)


def _place():
    x, y, c = lax.axis_index("x"), lax.axis_index("y"), lax.axis_index("c")
    return x, y, c, [(1 - x, y), (x, 1 - y), (1 - x, 1 - y)]


def _remote(src, dst, send_sem, recv_sem, dev):
    return pltpu.make_async_remote_copy(src_ref=src, dst_ref=dst, send_sem=send_sem, recv_sem=recv_sem,
                                        device_id=dev, device_id_type=MESH)


def _allgather8(v, *, name):
    R, C = v.shape

    def body(v_ref, out_ref, send_sems, recv_sems, local_sem):
        x, y, c, chips = _place()
        me, sibling = (x, y, c), (x, y, 1 - c)

        def rows(px, py, pc):
            return out_ref.at[pl.ds((4 * px + 2 * py + pc) * R, R), :]

        def copy(k, block, to, src=None):
            return _remote(rows(*block) if src is None else src, rows(*block), send_sems.at[k], recv_sems.at[k], to)

        mine = pltpu.make_async_copy(v_ref, rows(*me), local_sem)
        mine.start()
        first = [copy(0, me, sibling, src=v_ref)]
        first += [copy(1 + j, me, (*chip, c), src=v_ref) for j, chip in enumerate(chips)]
        for cp in first:
            cp.start()
        passed = [copy(4 + j, (*chip, c), sibling) for j, chip in enumerate(chips)]
        for j, chip in enumerate(chips):
            copy(1 + j, (*chip, c), me).wait_recv()
            passed[j].start()
        copy(0, sibling, me).wait_recv()
        for j, chip in enumerate(chips):
            copy(4 + j, (*chip, 1 - c), me).wait_recv()
        for cp in first + passed:
            cp.wait_send()
        mine.wait()

    out = pl.pallas_call(
        body, name=name, out_shape=jax.ShapeDtypeStruct((N_DEV * R, C), v.dtype),
        in_specs=[VMEM_FULL], out_specs=VMEM_FULL,
        scratch_shapes=[pltpu.SemaphoreType.DMA((7,)), pltpu.SemaphoreType.DMA((7,)), pltpu.SemaphoreType.DMA],
    )(v)
    return out.reshape(N_DEV, R, C)


SEM = pl.BlockSpec(memory_space=pltpu.SEMAPHORE)
HBM = pl.BlockSpec(memory_space=pltpu.HBM)
EFFECT = pltpu.SideEffectType.DATAFLOW_SIDE_EFFECTING
DMA_SEM = pltpu.SemaphoreType.DMA(())


def _in_hbm(a):
    return pltpu.with_memory_space_constraint(a, pltpu.HBM)


def _three_halves(land, r2):
    return land.at[pl.ds(0, N_CHIP - 1), pl.ds(0, r2)]


def _gather_start(ws, after, *, name):
    n = len(ws)
    na = len(after)
    lands = [lax.empty((N_CHIP,) + w.shape, w.dtype) for w in ws]

    def body(*refs):
        w_refs, land_refs = refs[:n], refs[n:2 * n]
        send, recv = refs[2 * n + na:3 * n + na], refs[3 * n + na:4 * n + na]
        token = refs[6 * n + na]
        x, y, c, chips = _place()
        k = 2 * x + y
        for i in range(n):
            r2 = ws[i].shape[0] // 2
            for cx, cy in chips:
                _remote(w_refs[i].at[pl.ds(c * r2, r2)], land_refs[i].at[k, pl.ds(c * r2, r2)], send[i], recv[i],
                        (cx, cy, c)).start()
        token[...] = jnp.zeros_like(token)

    outs = pl.pallas_call(
        body, name=name,
        out_shape=[DMA_SEM] * (2 * n) + [pltpu.HBM(w.shape, w.dtype) for w in ws] + [pltpu.HBM(l.shape, l.dtype) for l in lands]
        + [jax.ShapeDtypeStruct((8, LANES), F32)],
        in_specs=[HBM] * (2 * n) + [ANY] * na, out_specs=[SEM] * (2 * n) + [HBM] * (2 * n) + [VMEM_FULL],
        input_output_aliases={i: 2 * n + i for i in range(2 * n)},
        compiler_params=pltpu.CompilerParams(has_side_effects=EFFECT),
    )(*[_in_hbm(w) for w in ws], *[_in_hbm(l) for l in lands], *after)
    return outs[:n], outs[n:2 * n], outs[2 * n:3 * n], outs[3 * n:4 * n], outs[4 * n]


def _gather_forward(send, recv, ws, lands, after, *, name):
    n = len(ws)

    def body(*refs):
        w_refs, land_refs = refs[:n], refs[n:2 * n]
        send1, recv1 = refs[2 * n:3 * n], refs[3 * n:4 * n]
        send2, recv2 = refs[4 * n + 1 + 2 * n:4 * n + 1 + 3 * n], refs[4 * n + 1 + 3 * n:4 * n + 1 + 4 * n]
        x, y, c, chips = _place()
        sibling = (x, y, 1 - c)
        for i in range(n):
            r2 = ws[i].shape[0] // 2
            win = _three_halves(land_refs[i], r2)
            done = _remote(win, win, send1[i], recv1[i], sibling)
            done.wait_send()
            done.wait_recv()
            for cx, cy in chips:
                got = land_refs[i].at[2 * cx + cy, pl.ds(c * r2, r2)]
                _remote(got, got, send2[i], recv2[i], sibling).start()

    outs = pl.pallas_call(
        body, name=name,
        out_shape=[pltpu.HBM(w.shape, w.dtype) for w in ws] + [pltpu.HBM(l.shape, l.dtype) for l in lands] + [DMA_SEM] * (2 * n),
        in_specs=[HBM] * (2 * n) + [SEM] * (2 * n) + [ANY], out_specs=[HBM] * (2 * n) + [SEM] * (2 * n),
        input_output_aliases={i: i for i in range(2 * n)},
        compiler_params=pltpu.CompilerParams(has_side_effects=EFFECT),
    )(*ws, *lands, *send, *recv, after)
    return outs[2 * n:3 * n], outs[3 * n:4 * n], outs[n:2 * n]


def _gather_finish(send, recv, lands, after, *, name):
    n = len(lands)

    def body(*refs):
        land_refs = refs[:n]
        send2, recv2 = refs[n:2 * n], refs[2 * n:3 * n]
        x, y, c, _ = _place()
        for i in range(n):
            win = _three_halves(land_refs[i], lands[i].shape[1] // 2)
            done = _remote(win, win, send2[i], recv2[i], (x, y, 1 - c))
            done.wait_send()
            done.wait_recv()

    return pl.pallas_call(
        body, name=name,
        out_shape=[pltpu.HBM(l.shape, l.dtype) for l in lands],
        in_specs=[HBM] * n + [SEM] * (2 * n) + [ANY], out_specs=[HBM] * n,
        input_output_aliases={i: i for i in range(n)},
        compiler_params=pltpu.CompilerParams(has_side_effects=EFFECT),
    )(*lands, *send, *recv, after)


def _scatter_start(gs, *, name):
    n = len(gs)
    lands = [lax.empty((N_DEV, g.shape[1] // 2, g.shape[2]), g.dtype) for g in gs]

    def body(*refs):
        g_refs, land_refs = refs[:n], refs[n:2 * n]
        send, recv = refs[2 * n:3 * n], refs[3 * n:4 * n]
        token = refs[6 * n]
        x, y, c, chips = _place()
        k = 2 * x + y
        me = 2 * k + c
        for i in range(n):
            r2 = gs[i].shape[1] // 2
            for cx, cy in chips:
                for cc in range(2):
                    _remote(g_refs[i].at[2 * cx + cy, pl.ds(cc * r2, r2)], land_refs[i].at[me], send[i], recv[i],
                            (cx, cy, cc)).start()
            _remote(g_refs[i].at[k, pl.ds((1 - c) * r2, r2)], land_refs[i].at[me], send[i], recv[i], (x, y, 1 - c)).start()
        token[...] = jnp.zeros_like(token)

    outs = pl.pallas_call(
        body, name=name,
        out_shape=[DMA_SEM] * (2 * n) + [pltpu.HBM(g.shape, g.dtype) for g in gs] + [pltpu.HBM(l.shape, l.dtype) for l in lands]
        + [jax.ShapeDtypeStruct((8, LANES), F32)],
        in_specs=[HBM] * (2 * n), out_specs=[SEM] * (2 * n) + [HBM] * (2 * n) + [VMEM_FULL],
        input_output_aliases={i: 2 * n + i for i in range(2 * n)},
        compiler_params=pltpu.CompilerParams(has_side_effects=EFFECT),
    )(*[_in_hbm(g) for g in gs], *[_in_hbm(l) for l in lands])
    return outs[:n], outs[n:2 * n], outs[2 * n:3 * n], outs[3 * n:4 * n], outs[4 * n]


def _scatter_wait(send, recv, gs, lands, after, *, name):
    n = len(gs)

    def body(*refs):
        land_refs = refs[n:2 * n]
        send1, recv1 = refs[2 * n:3 * n], refs[3 * n:4 * n]
        x, y, c, _ = _place()
        for i in range(n):
            win = land_refs[i].at[pl.ds(0, N_DEV - 1)]
            done = _remote(win, win, send1[i], recv1[i], (x, y, 1 - c))
            done.wait_send()
            done.wait_recv()

    outs = pl.pallas_call(
        body, name=name,
        out_shape=[pltpu.HBM(g.shape, g.dtype) for g in gs] + [pltpu.HBM(l.shape, l.dtype) for l in lands],
        in_specs=[HBM] * (2 * n) + [SEM] * (2 * n) + [ANY], out_specs=[HBM] * (2 * n),
        input_output_aliases={i: i for i in range(2 * n)},
        compiler_params=pltpu.CompilerParams(has_side_effects=EFFECT),
    )(*gs, *lands, *send, *recv, after)
    return outs[:n], outs[n:]


def _share_halves(ts, *, name):
    n = len(ts)

    def body(*refs):
        outs = refs[n:2 * n]
        send_sems, recv_sems = refs[2 * n:]
        x, y, c, _ = _place()
        sibling = (x, y, 1 - c)
        cps = []
        for i in range(n):
            r2 = ts[i].shape[0] // 2
            mine = outs[i].at[pl.ds(c * r2, r2)]
            cps.append(_remote(mine, mine, send_sems.at[i], recv_sems.at[i], sibling))
            cps[-1].start()
        for i in range(n):
            r2 = ts[i].shape[0] // 2
            got = outs[i].at[pl.ds((1 - c) * r2, r2)]
            _remote(got, got, send_sems.at[i], recv_sems.at[i], sibling).wait_recv()
        for cp in cps:
            cp.wait_send()

    return pl.pallas_call(
        body, name=name,
        out_shape=[jax.ShapeDtypeStruct(t.shape, t.dtype) for t in ts],
        in_specs=[ANY] * n, out_specs=[ANY] * n, input_output_aliases={i: i for i in range(n)},
        scratch_shapes=[pltpu.SemaphoreType.DMA((n,)), pltpu.SemaphoreType.DMA((n,))],
    )(*ts)


def _sum_pieces(land, g, idx, *, name):
    _, r2, C = land.shape
    tr = _tile(r2, max(16, (1 << 19) // C), 16)
    nr = r2 // tr

    def body(idx_ref, land_ref, own_ref, o_ref, acc_ref):
        d = pl.program_id(1)
        mine = d == idx_ref[0]

        @pl.when(d == 0)
        def _():
            acc_ref[...] = jnp.zeros_like(acc_ref)

        @pl.when(mine)
        def _():
            acc_ref[...] += own_ref[...].astype(F32)

        @pl.when(jnp.logical_not(mine))
        def _():
            acc_ref[...] += land_ref[...].astype(F32)

        @pl.when(d == N_DEV - 1)
        def _():
            o_ref[...] = acc_ref[...]

    return pl.pallas_call(
        body, name=name, out_shape=jax.ShapeDtypeStruct((2 * r2, C), F32),
        grid_spec=pltpu.PrefetchScalarGridSpec(
            num_scalar_prefetch=1, grid=(nr, N_DEV),
            in_specs=[pl.BlockSpec((None, tr, C), lambda i, d, ix: (jnp.where(d == ix[0], (d + 1) % N_DEV, d), i, 0)),
                      pl.BlockSpec((None, tr, C), lambda i, d, ix: (ix[1], ix[2] * nr + i, 0))],
            out_specs=pl.BlockSpec((tr, C), lambda i, d, ix: (ix[2] * nr + i, 0)),
            scratch_shapes=[pltpu.VMEM((tr, C), F32)]),
        compiler_params=_params("parallel", "arbitrary"),
    )(idx, land, g)


def _sum_devices(v, *, name):
    n, R, C = v.shape

    def body(v_ref, o_ref):
        acc = v_ref[0]
        for j in range(1, n):
            acc = acc + v_ref[j]
        o_ref[...] = acc

    return pl.pallas_call(body, name=name, out_shape=jax.ShapeDtypeStruct((R, C), F32),
                          in_specs=[VMEM_FULL], out_specs=VMEM_FULL)(v)


def _pair(a, b):
    n = a.shape[1]
    parts = []
    for j in range(n // PAIR):
        parts += [a[:, j * PAIR:(j + 1) * PAIR], b[:, j * PAIR:(j + 1) * PAIR]]
    return jnp.concatenate(parts, axis=1)


def _unpair(p):
    nt = p.shape[1] // (2 * PAIR)
    a = jnp.concatenate([p[:, 2 * j * PAIR:(2 * j + 1) * PAIR] for j in range(nt)], axis=1)
    b = jnp.concatenate([p[:, (2 * j + 1) * PAIR:(2 * j + 2) * PAIR] for j in range(nt)], axis=1)
    return a, b


def _from_col_shards(g):
    return jnp.transpose(g, (1, 0, 2)).reshape(g.shape[1], N_CHIP * g.shape[2])


def _to_col_shards(w):
    R, N = w.shape
    return jnp.transpose(w.reshape(R, N_CHIP, N // N_CHIP), (1, 0, 2))


def _split_heads(w, widths):
    R, N = w.shape
    per = sum(widths)
    w3 = w.reshape(R, N // per, per)
    lo = w3[:, :, :widths[0]].reshape(R, -1)
    hi = w3[:, :, widths[0]:].reshape(R, -1)
    return jnp.concatenate([lo, hi], axis=1)


def _merge_heads(w, widths):
    R, N = w.shape
    H = N // sum(widths)
    lo = w[:, :H * widths[0]].reshape(R, H, widths[0])
    hi = w[:, H * widths[0]:].reshape(R, H, widths[1])
    return jnp.concatenate([lo, hi], axis=2).reshape(R, N)


def _t5_bucket(dist):
    max_exact = REL_BUCKETS // 2
    n = jnp.maximum(dist, 0)
    large = max_exact + (jnp.log(jnp.maximum(n, 1).astype(F32) / max_exact)
                         / jnp.log(jnp.asarray(REL_MAX_DIST / max_exact, F32))
                         * (REL_BUCKETS - max_exact)).astype(jnp.int32)
    large = jnp.minimum(large, REL_BUCKETS - 1)
    return jnp.where(n < max_exact, n, large)


def _rel_tables():
    a = jnp.arange(SWA_BLOCK)
    b = jnp.arange(2 * SWA_BLOCK)
    dist = SWA_BLOCK + a[:, None] - b[None, :]
    valid = jnp.logical_and(dist >= 0, dist < SWA_BLOCK)
    onehot = jnp.logical_and(_t5_bucket(dist)[..., None] == jnp.arange(REL_BUCKETS), valid[..., None])
    onehot = onehot.astype(F32).reshape(2 * SWA_BLOCK * SWA_BLOCK, REL_BUCKETS)
    negmask = jnp.where(valid, 0.0, NEG).astype(F32).reshape(1, -1)
    return onehot, negmask


def _rope_tables(S):
    pos = jnp.arange(S, dtype=F32)
    inv = ROPE_THETA ** (-jnp.arange(0, MLA_ROPE, 2, dtype=F32) / MLA_ROPE)
    ang = pos[:, None] * inv[None, :]
    ang = jnp.concatenate([ang, ang, ang, ang], axis=-1)
    return jnp.cos(ang), jnp.sin(ang)


def _flat_pad(parts, rows=8):
    flat = jnp.concatenate([p.reshape(1, -1) for p in parts], axis=1)
    n = flat.shape[1]
    width = -(-n // (rows * LANES)) * LANES
    return jnp.pad(flat, ((0, 0), (0, rows * width - n))).reshape(rows, width)


def _unflat(vec, shapes):
    flat = vec.reshape(-1)
    out, off = [], 0
    for s in shapes:
        n = 1
        for d in s:
            n *= d
        out.append(flat[off:off + n].reshape(s))
        off += n
    return out


def kernel(x, c, w_ada, b_ada, g_pre_mix, g_post_mix, w_in, g_q_lat, w_uq, g_kv_lat, w_ukv, rel_bias, sinks, w_o, g_pre_ffn, g_post_ffn, w_up, conv_w, conv_b, w_down, loss_target, m_w_ada, m_b_ada, m_g_pre_mix, m_g_post_mix, m_w_in, m_g_q_lat, m_w_uq, m_g_kv_lat, m_w_ukv, m_rel_bias, m_sinks, m_w_o, m_g_pre_ffn, m_g_post_ffn, m_w_up, m_conv_w, m_conv_b, m_w_down, v_w_ada, v_b_ada, v_g_pre_mix, v_g_post_mix, v_w_in, v_g_q_lat, v_w_uq, v_g_kv_lat, v_w_ukv, v_rel_bias, v_sinks, v_w_o, v_g_pre_ffn, v_g_post_ffn, v_w_up, v_conv_w, v_conv_b, v_w_down):
    S, D = x.shape[1], x.shape[2]
    Rq, Rkv = g_q_lat.shape[1], g_kv_lat.shape[1]
    H = D // MLA_V
    NH = D // SWA_HD
    KW = SWA_KVH * SWA_HD
    F = w_down.shape[1] * N_CHIP
    xi, yi, ci = lax.axis_index("x"), lax.axis_index("y"), lax.axis_index("c")
    chip = 2 * xi + yi
    me = 2 * chip + ci
    x2, tgt = x[0], loss_target[0]

    c_all = _allgather8(jnp.broadcast_to(c, (8, D)), name="gather_c")[:, 0, :]
    n3 = w_ada.shape[2]
    mod_part = _ada_fwd(c_all, w_ada[0], lax.dynamic_slice(b_ada, (0, chip * n3), (1, n3)), name="ada_fwd")
    mod_all = _allgather8(mod_part, name="gather_mod")
    mod_me = lax.dynamic_index_in_dim(mod_all[0::2], me, axis=1, keepdims=False).reshape(1, 6 * D)
    sh1, sc1, gt1, sh2, sc2, gt2 = [mod_me[:, k * D:(k + 1) * D] for k in range(6)]
    cw_all = _allgather8(jnp.pad(conv_w[0], ((0, 5), (0, 0))), name="gather_conv_w")[0::2, :3]

    big = [w_in[0], w_uq[0], w_ukv[0], w_o[0], w_up[0], w_down[0]]
    local = [w.astype(BF16) for w in big]
    send1, recv1, srcs, lands, token = _gather_start(local, (mod_all, cw_all), name="gather_start")
    onehot, negmask = _rel_tables()
    bias_m = (_matmul(rel_bias.T, onehot.T, tie=token, name="rel_bias_table") + negmask).reshape(NH, SWA_BLOCK, 2 * SWA_BLOCK)
    h = _modnorm_fwd(x2, g_pre_mix, sc1, sh1, name="pre_mix_norm")

    def whole(land, i):
        return lax.dynamic_update_index_in_dim(land, local[i], chip, 0)

    s2, r2, l_in = _gather_forward(send1[:1], recv1[:1], srcs[:1], lands[:1], h, name="gather_forward_in")
    (l_in,) = _gather_finish(s2, r2, l_in, h, name="gather_finish_in")
    s2b, r2b, l_b = _gather_forward(send1[1:4], recv1[1:4], srcs[1:4], lands[1:4], l_in, name="gather_forward_attn")
    gin = whole(l_in, 0)
    win = _from_col_shards(gin)
    o_kr = Rq + Rkv
    o_q = o_kr + MLA_ROPE
    o_g = o_q + NH * SWA_HD + 2 * KW
    w_lat = jnp.concatenate([win[:, :o_q], win[:, o_kr:o_q]], axis=1)
    w_swa = win[:, o_q:o_g]
    w_gate = _pair(win[:, o_g:o_g + D], win[:, o_g + D:])
    w_in_all = jnp.concatenate([w_lat, w_swa, w_gate], axis=1)
    n_lat, n_swa = w_lat.shape[1], w_swa.shape[1]
    cw_full = _from_col_shards(cw_all)
    cw = _pair(cw_full[:, :F], cw_full[:, F:])
    cb = _pair(conv_b[:, :F], conv_b[:, F:])
    cos, sin = _rope_tables(S)
    sink_v = sinks.reshape(NH)

    z_lat = _matmul(h, w_lat, name="in_proj_lat")
    z_swa = _matmul(h, w_swa, name="in_proj_swa")
    zg = _matmul(h, w_gate, name="in_proj_gate")
    nq, nkv = _lat_norm_fwd(z_lat, g_q_lat, g_kv_lat, name="lat_norm")
    l_uq, l_ukv, l_o = _gather_finish(s2b, r2b, l_b, nq, name="gather_finish_attn")
    wuq = _split_heads(_from_col_shards(whole(l_uq, 1)), (MLA_NOPE, MLA_ROPE))
    wukv = _split_heads(_from_col_shards(whole(l_ukv, 2)), (MLA_NOPE, MLA_V))
    wo = whole(l_o, 3).reshape(D, D)
    q_raw = _matmul(nq, wuq, name="uq_proj")
    kv_raw = _matmul(nkv, wukv, name="ukv_proj")
    qp, kp, vv = _mla_pack_fwd(q_raw, kv_raw, z_lat, cos, sin, o_kr, name="mla_pack")
    o_a, lse = _flash_fwd(qp, kp, vv, name="mla_attn")
    s2c, r2c, l_c = _gather_forward(send1[4:], recv1[4:], srcs[4:], lands[4:], o_a, name="gather_forward_ffn")
    o_b = _swa_fwd(z_swa, bias_m, sink_v, name="swa_attn")
    mixin = _gate_fwd(zg, o_a, o_b, name="gate_mix")
    mix = _matmul(mixin, wo, name="o_proj")
    x1 = _resnorm_fwd(x2, mix, g_post_mix, gt1, name="post_mix_norm")
    h2 = _modnorm_fwd(x1, g_pre_ffn, sc2, sh2, name="pre_ffn_norm")
    l_up, l_down = _gather_finish(s2c, r2c, l_c, h2, name="gather_finish_ffn")
    wup_full = _from_col_shards(whole(l_up, 4))
    wup = _pair(wup_full[:, :F], wup_full[:, F:])
    wdown = whole(l_down, 5).reshape(F, D)
    t = _matmul(h2, wup, name="up_proj")
    a = _conv_gate_fwd(t, cw, cb, name="conv_gate")
    yv = _matmul(a, wdown, name="down_proj")
    dout, loss_tile = _resnorm_loss(x1, yv, g_post_ffn, gt2, tgt, name="post_ffn_norm_loss")

    idx = jnp.stack([me, chip, ci]).astype(jnp.int32)
    big_params = dict(w_in=(w_in, m_w_in, v_w_in), w_uq=(w_uq, m_w_uq, v_w_uq), w_ukv=(w_ukv, m_w_ukv, v_w_ukv),
                      w_o=(w_o, m_w_o, v_w_o), w_up=(w_up, m_w_up, v_w_up), w_down=(w_down, m_w_down, v_w_down))
    res = {}

    def start(nms, gs):
        send, recv, gsrc, glands, tok = _scatter_start(gs, name="grads_start_" + nms[0])
        return (nms, send, recv, gsrc, glands), tok

    def finish(pending, after):
        nms, send, recv, gsrc, glands = pending
        gsrc, glands = _scatter_wait(send, recv, gsrc, glands, after, name="grads_wait_" + nms[0])
        halves = [_sum_pieces(l, g, idx, name="grad_sum_" + nm) for l, g, nm in zip(glands, gsrc, nms)]
        for nm, g in zip(nms, _share_halves(halves, name="grads_share_" + nms[0])):
            w, m, v = big_params[nm]
            res[nm] = (g,) + tuple(_adamw(w[0], g, m[0], v[0], name="adamw_" + nm))

    dy, dg_post_ffn, dgt2 = _resnorm_bwd(dout, yv, g_post_ffn, gt2, name="post_ffn_norm_bwd")
    dw_down = _matmul(a, dy, ta=True, out_dtype=BF16, name="down_proj_dw")
    p_down, tok = start(["w_down"], [dw_down.reshape(N_CHIP, F // N_CHIP, D)])
    da = _matmul(dy, wdown, tb=True, tie=tok, name="down_proj_dx")
    du, dcw_p, dcb_p = _conv_gate_bwd(t, da, cw, cb, name="conv_gate_bwd")
    dt = _conv_bwd_dt(du, cw, name="conv_bwd_dt")
    dw_up_p = _matmul(h2, dt, ta=True, out_dtype=BF16, name="up_proj_dw")
    p_up, tok = start(["w_up"], [_to_col_shards(jnp.concatenate(_unpair(dw_up_p), axis=1))])
    dh2 = _matmul(dt, wup, tb=True, tie=tok, name="up_proj_dx")
    finish(p_down, dh2)
    dx1, dg_pre_ffn, dsc2, dsh2 = _modnorm_bwd(dh2, x1, g_pre_ffn, sc2, dout, name="pre_ffn_norm_bwd")
    dmix, dg_post_mix, dgt1 = _resnorm_bwd(dx1, mix, g_post_mix, gt1, name="post_mix_norm_bwd")
    dw_o = _matmul(mixin, dmix, ta=True, out_dtype=BF16, name="o_proj_dw")
    p_o, tok = start(["w_o"], [dw_o.reshape(N_CHIP, D // N_CHIP, D)])
    dmixin = _matmul(dmix, wo, tb=True, tie=tok, name="o_proj_dx")
    do_a, do_b, dzg = _gate_bwd(dmixin, zg, o_a, o_b, name="gate_mix_bwd")
    dqp, dkp, dvv = _flash_bwd(qp, kp, vv, o_a, do_a, lse, name="mla_attn_bwd")
    dq_raw, dkv_raw, dkr = _mla_pack_bwd(dqp, dkp, dvv, cos, sin, name="mla_pack_bwd")
    finish(p_up, dq_raw)
    dw_uq_p = _matmul(nq, dq_raw, ta=True, out_dtype=BF16, name="uq_proj_dw")
    dw_ukv_p = _matmul(nkv, dkv_raw, ta=True, out_dtype=BF16, name="ukv_proj_dw")
    p_qkv, tok = start(["w_uq", "w_ukv"], [_to_col_shards(_merge_heads(dw_uq_p, (MLA_NOPE, MLA_ROPE))),
                                           _to_col_shards(_merge_heads(dw_ukv_p, (MLA_NOPE, MLA_V)))])
    dnq = _matmul(dq_raw, wuq, tb=True, tie=tok, name="uq_proj_dx")
    dnkv = _matmul(dkv_raw, wukv, tb=True, name="ukv_proj_dx")
    finish(p_o, dnkv)
    dz_lat, dg_q, dg_kv = _lat_norm_bwd(z_lat, dnq, dnkv, dkr, g_q_lat, g_kv_lat, name="lat_norm_bwd")
    dz_swa, dbias, dsink = _swa_bwd(z_swa, bias_m, sink_v, o_b, do_b, name="swa_attn_bwd")
    dz = jnp.concatenate([dz_lat, dz_swa, dzg], axis=1)
    dw_in_p = _matmul(h, dz, ta=True, out_dtype=BF16, name="in_proj_dw")
    dga, dgb = _unpair(dw_in_p[:, n_lat + n_swa:])
    dw_in = jnp.concatenate([dw_in_p[:, :o_q], dw_in_p[:, n_lat:n_lat + n_swa], dga, dgb], axis=1)
    p_in, tok = start(["w_in"], [_to_col_shards(dw_in)])
    dh = _matmul(dz, w_in_all, tb=True, tie=tok, name="in_proj_dx")
    finish(p_qkv, dh)
    grad_x, dg_pre_mix, dsc1, dsh1 = _modnorm_bwd(dh, x2, g_pre_mix, sc1, dx1, name="pre_mix_norm_bwd")
    drel = _matmul(dbias.reshape(NH, -1), onehot, name="rel_bias_bwd").T

    dcw = jnp.concatenate(_unpair(dcw_p), axis=1)
    dcb = jnp.concatenate(_unpair(dcb_p), axis=1)
    dmod = jnp.concatenate([dsh1, dsc1, dgt1, dsh2, dsc2, dgt2], axis=1)
    small = [dmod, dg_pre_mix, dg_post_mix, dg_pre_ffn, dg_post_ffn, dg_q, dg_kv, drel, dsink[:, :NH], dcb, dcw]
    shapes = [p.shape for p in small]
    small_all = _allgather8(_flat_pad(small), name="gather_small_grads")
    tot = _unflat(_sum_devices(small_all, name="sum_small_grads"), shapes)
    g_b_ada, g_pre_mix_g, g_post_mix_g, g_pre_ffn_g, g_post_ffn_g, g_q_g, g_kv_g, g_rel, g_sinks, g_cb, g_cw_full = tot
    dmod_all = small_all.reshape(N_DEV, -1)[:, :6 * D]
    g_w_ada = _ada_bwd(c_all.T, lax.dynamic_slice(dmod_all, (0, chip * n3), (N_DEV, n3)), name="ada_bwd")
    ncw = conv_w.shape[2]
    g_cw = lax.dynamic_slice(g_cw_full, (0, chip * ncw), (3, ncw))

    res["w_ada"] = (g_w_ada,) + tuple(_adamw(w_ada[0], g_w_ada, m_w_ada[0], v_w_ada[0], name="adamw_w_ada"))
    finish(p_in, g_w_ada)
    snames = ["b_ada", "g_pre_mix", "g_post_mix", "g_pre_ffn", "g_post_ffn", "g_q_lat", "g_kv_lat", "rel_bias", "sinks",
              "conv_b", "conv_w"]
    sw = [b_ada, g_pre_mix, g_post_mix, g_pre_ffn, g_post_ffn, g_q_lat, g_kv_lat, rel_bias, sinks, conv_b, conv_w]
    sm = [m_b_ada, m_g_pre_mix, m_g_post_mix, m_g_pre_ffn, m_g_post_ffn, m_g_q_lat, m_g_kv_lat, m_rel_bias, m_sinks,
          m_conv_b, m_conv_w]
    sv = [v_b_ada, v_g_pre_mix, v_g_post_mix, v_g_pre_ffn, v_g_post_ffn, v_g_q_lat, v_g_kv_lat, v_rel_bias, v_sinks,
          v_conv_b, v_conv_w]
    sg = [g_b_ada, g_pre_mix_g, g_post_mix_g, g_pre_ffn_g, g_post_ffn_g, g_q_g, g_kv_g, g_rel, g_sinks, g_cb, g_cw]
    sshapes = [w.shape for w in sw]
    sd, snm, snv = _adamw(_flat_pad(sw), _flat_pad(sg), _flat_pad(sm), _flat_pad(sv), name="adamw_small")
    sd, snm, snv = _unflat(sd, sshapes), _unflat(snm, sshapes), _unflat(snv, sshapes)
    for k, nm in enumerate(snames):
        res[nm] = (sg[k].reshape(sshapes[k]), sd[k], snm[k], snv[k])

    order = ["w_ada", "b_ada", "g_pre_mix", "g_post_mix", "w_in", "g_q_lat", "w_uq", "g_kv_lat", "w_ukv", "rel_bias", "sinks",
             "w_o", "g_pre_ffn", "g_post_ffn", "w_up", "conv_w", "conv_b", "w_down"]
    ref_shapes = dict(w_ada=w_ada.shape, w_in=w_in.shape, w_uq=w_uq.shape, w_ukv=w_ukv.shape, w_o=w_o.shape,
                      w_up=w_up.shape, w_down=w_down.shape)
    outs = []
    for k in range(4):
        for nm in order:
            arr = res[nm][k]
            outs.append(arr.reshape(ref_shapes[nm]) if nm in ref_shapes else arr)
    loss = lax.psum(loss_tile[0, 0], ("x", "y", "c"))
    return (loss, grad_x[None], *outs)
```

```python
import functools

import jax
import jax.numpy as jnp
from jax import lax
from jax.experimental import pallas as pl
from jax.experimental.pallas import tpu as pltpu

F32 = jnp.float32
BF16 = jnp.bfloat16
MESH = pl.DeviceIdType.MESH
HIGHEST = lax.Precision.HIGHEST

N_DEV = 8
N_CHIP = 4
LANES = 128
MLA_NOPE = 128
MLA_ROPE = 64
MLA_V = 128
MLA_QK = MLA_NOPE + MLA_ROPE
MLA_QK_PAD = 256
ROPE_THETA = 10000.0
SWA_HD = 64
SWA_KVH = 4
SWA_BLOCK = 128
REL_BUCKETS = 32
REL_MAX_DIST = 128
PAIR = 512
EPS = 1e-6
NEG = -1e30
ADAM_LR = 0.001
ADAM_B1 = 0.9
ADAM_B2 = 0.999
ADAM_EPS = 1e-08
ADAM_WD = 0.01
ADAM_STEP = 10

ANY = pl.BlockSpec(memory_space=pl.ANY)
VMEM_FULL = pl.BlockSpec(memory_space=pltpu.VMEM)
SMEM_FULL = pl.BlockSpec(memory_space=pltpu.SMEM)


def _params(*sem):
    return pltpu.CompilerParams(dimension_semantics=sem if sem else None)


def _tied(body, tie):
    if tie is None:
        return body, [], []

    def tied_body(tie_ref, *refs):
        body(*refs)

    return tied_body, [ANY], [tie]


def _tile(n, pref, unit=LANES):
    best = None
    for t in range(unit, min(n, pref) + 1, unit):
        if n % t == 0:
            best = t
    return n if best is None else best


def _matmul(a, b, *, ta=False, tb=False, out_dtype=F32, tie=None, name):
    if ta:
        K, M = a.shape
    else:
        M, K = a.shape
    if tb:
        N, K2 = b.shape
    else:
        K2, N = b.shape
    assert K == K2, (a.shape, b.shape, ta, tb)
    exact = a.dtype == F32
    tn = _tile(N, 1536)
    tk = _tile(K, 2048)
    nk = K // tk
    tm = M if M < 8 else _tile(M, 1024 if nk == 1 else 512, 8)
    dn = (((0 if ta else 1,), (1 if tb else 0,)), ((), ()))

    def product(a_ref, b_ref):
        return lax.dot_general(a_ref[...], b_ref[...], dn, preferred_element_type=F32,
                               precision=HIGHEST if exact else None)

    def body_acc(a_ref, b_ref, o_ref, acc_ref):
        k = pl.program_id(2)

        @pl.when(k == 0)
        def _():
            acc_ref[...] = product(a_ref, b_ref)

        @pl.when(jnp.logical_and(k > 0, k < nk - 1))
        def _():
            acc_ref[...] += product(a_ref, b_ref)

        @pl.when(k == nk - 1)
        def _():
            o_ref[...] = (acc_ref[...] + product(a_ref, b_ref)).astype(o_ref.dtype)

    def body_one(a_ref, b_ref, o_ref):
        o_ref[...] = product(a_ref, b_ref).astype(o_ref.dtype)

    body, tspec, targ = _tied(body_one if nk == 1 else body_acc, tie)
    a_spec = pl.BlockSpec((tk, tm), lambda i, j, k: (k, i)) if ta else pl.BlockSpec((tm, tk), lambda i, j, k: (i, k))
    b_spec = pl.BlockSpec((tn, tk), lambda i, j, k: (j, k)) if tb else pl.BlockSpec((tk, tn), lambda i, j, k: (k, j))
    return pl.pallas_call(
        body, name=name,
        out_shape=jax.ShapeDtypeStruct((M, N), out_dtype),
        grid=(M // tm, N // tn, nk),
        in_specs=tspec + [a_spec, b_spec],
        out_specs=pl.BlockSpec((tm, tn), lambda i, j, k: (i, j)),
        scratch_shapes=[] if nk == 1 else [pltpu.VMEM((tm, tn), F32)],
        compiler_params=_params("parallel", "parallel", "arbitrary"),
    )(*targ, a, b)


def _row_tile(S, width):
    return _tile(S, max(8, (1 << 19) // width), 8)


def _rstd(x):
    return lax.rsqrt(jnp.mean(x * x, axis=-1, keepdims=True) + EPS)


def _acc_rows(ref, val, first):
    s = jnp.sum(val, axis=0, keepdims=True)

    @pl.when(first)
    def _():
        ref[...] = s

    @pl.when(jnp.logical_not(first))
    def _():
        ref[...] += s


def _modnorm_fwd(x, g, sc, sh, *, name):
    S, D = x.shape
    tr = _row_tile(S, D)

    def body(x_ref, g_ref, sc_ref, sh_ref, h_ref):
        xv = x_ref[...]
        n = (xv * _rstd(xv)) * g_ref[...]
        h_ref[...] = (n * (1.0 + sc_ref[...]) + sh_ref[...]).astype(BF16)

    row = pl.BlockSpec((tr, D), lambda i: (i, 0))
    vec = pl.BlockSpec((1, D), lambda i: (0, 0))
    return pl.pallas_call(
        body, name=name, out_shape=jax.ShapeDtypeStruct((S, D), BF16), grid=(S // tr,),
        in_specs=[row, vec, vec, vec], out_specs=row, compiler_params=_params("parallel"),
    )(x, g, sc, sh)


def _modnorm_bwd(dh, x, g, sc, dres, *, name):
    S, D = x.shape
    tr = _row_tile(S, D)

    def body(dh_ref, x_ref, g_ref, sc_ref, dres_ref, dx_ref, dg_ref, dsc_ref, dsh_ref):
        first = pl.program_id(0) == 0
        xv = x_ref[...]
        dhv = dh_ref[...]
        gv = g_ref[...]
        r = _rstd(xv)
        xhat = xv * r
        _acc_rows(dsh_ref, dhv, first)
        _acc_rows(dsc_ref, dhv * (xhat * gv), first)
        dn = dhv * (1.0 + sc_ref[...])
        _acc_rows(dg_ref, dn * xhat, first)
        dxhat = dn * gv
        proj = jnp.mean(dxhat * xhat, axis=-1, keepdims=True)
        dx_ref[...] = r * (dxhat - xhat * proj) + dres_ref[...]

    row = pl.BlockSpec((tr, D), lambda i: (i, 0))
    vec = pl.BlockSpec((1, D), lambda i: (0, 0))
    vshape = jax.ShapeDtypeStruct((1, D), F32)
    return pl.pallas_call(
        body, name=name,
        out_shape=(jax.ShapeDtypeStruct((S, D), F32), vshape, vshape, vshape), grid=(S // tr,),
        in_specs=[row, row, vec, vec, row], out_specs=(row, vec, vec, vec),
        compiler_params=_params("arbitrary"),
    )(dh, x, g, sc, dres)


def _resnorm_fwd(xres, m, g, gt, *, name):
    S, D = xres.shape
    tr = _row_tile(S, D)

    def body(x_ref, m_ref, g_ref, gt_ref, o_ref):
        mv = m_ref[...]
        o_ref[...] = x_ref[...] + gt_ref[...] * ((mv * _rstd(mv)) * g_ref[...])

    row = pl.BlockSpec((tr, D), lambda i: (i, 0))
    vec = pl.BlockSpec((1, D), lambda i: (0, 0))
    return pl.pallas_call(
        body, name=name, out_shape=jax.ShapeDtypeStruct((S, D), F32), grid=(S // tr,),
        in_specs=[row, row, vec, vec], out_specs=row, compiler_params=_params("parallel"),
    )(xres, m, g, gt)


def _resnorm_loss(xres, m, g, gt, target, *, name):
    S, D = xres.shape
    tr = _row_tile(S, D)

    def body(x_ref, m_ref, g_ref, gt_ref, t_ref, d_ref, loss_ref):
        mv = m_ref[...]
        out = x_ref[...] + gt_ref[...] * ((mv * _rstd(mv)) * g_ref[...])
        err = out - t_ref[...]
        d_ref[...] = err * (1.0 / D)
        part = 0.5 * jnp.sum(jnp.mean(err * err, axis=-1, keepdims=True), axis=0, keepdims=True)
        part = jnp.broadcast_to(part, loss_ref.shape)

        @pl.when(pl.program_id(0) == 0)
        def _():
            loss_ref[...] = part

        @pl.when(pl.program_id(0) != 0)
        def _():
            loss_ref[...] += part

    row = pl.BlockSpec((tr, D), lambda i: (i, 0))
    vec = pl.BlockSpec((1, D), lambda i: (0, 0))
    return pl.pallas_call(
        body, name=name,
        out_shape=(jax.ShapeDtypeStruct((S, D), F32), jax.ShapeDtypeStruct((8, LANES), F32)), grid=(S // tr,),
        in_specs=[row, row, vec, vec, row], out_specs=(row, pl.BlockSpec((8, LANES), lambda i: (0, 0))),
        compiler_params=_params("arbitrary"),
    )(xres, m, g, gt, target)


def _resnorm_bwd(dout, m, g, gt, *, name):
    S, D = m.shape
    tr = _row_tile(S, D)

    def body(d_ref, m_ref, g_ref, gt_ref, dm_ref, dg_ref, dgt_ref):
        first = pl.program_id(0) == 0
        mv = m_ref[...]
        dv = d_ref[...]
        gv = g_ref[...]
        r = _rstd(mv)
        mhat = mv * r
        _acc_rows(dgt_ref, dv * (mhat * gv), first)
        dn = dv * gt_ref[...]
        _acc_rows(dg_ref, dn * mhat, first)
        dmhat = dn * gv
        proj = jnp.mean(dmhat * mhat, axis=-1, keepdims=True)
        dm_ref[...] = (r * (dmhat - mhat * proj)).astype(BF16)

    row = pl.BlockSpec((tr, D), lambda i: (i, 0))
    vec = pl.BlockSpec((1, D), lambda i: (0, 0))
    vshape = jax.ShapeDtypeStruct((1, D), F32)
    return pl.pallas_call(
        body, name=name, out_shape=(jax.ShapeDtypeStruct((S, D), BF16), vshape, vshape), grid=(S // tr,),
        in_specs=[row, row, vec, vec], out_specs=(row, vec, vec), compiler_params=_params("arbitrary"),
    )(dout, m, g, gt)


def _lat_norm_fwd(z_lat, g_q, g_kv, *, name):
    S, W = z_lat.shape
    Rq, Rkv = g_q.shape[1], g_kv.shape[1]
    tr = _row_tile(S, W)

    def body(z_ref, gq_ref, gkv_ref, nq_ref, nkv_ref):
        cq = z_ref[:, :Rq]
        ckv = z_ref[:, Rq:Rq + Rkv]
        nq_ref[...] = ((cq * _rstd(cq)) * gq_ref[...]).astype(BF16)
        nkv_ref[...] = ((ckv * _rstd(ckv)) * gkv_ref[...]).astype(BF16)

    return pl.pallas_call(
        body, name=name,
        out_shape=(jax.ShapeDtypeStruct((S, Rq), BF16), jax.ShapeDtypeStruct((S, Rkv), BF16)), grid=(S // tr,),
        in_specs=[pl.BlockSpec((tr, W), lambda i: (i, 0)), pl.BlockSpec((1, Rq), lambda i: (0, 0)),
                  pl.BlockSpec((1, Rkv), lambda i: (0, 0))],
        out_specs=(pl.BlockSpec((tr, Rq), lambda i: (i, 0)), pl.BlockSpec((tr, Rkv), lambda i: (i, 0))),
        compiler_params=_params("parallel"),
    )(z_lat, g_q, g_kv)


def _lat_norm_bwd(z_lat, dnq, dnkv, dkr, g_q, g_kv, *, name):
    S, W = z_lat.shape
    Rq, Rkv = g_q.shape[1], g_kv.shape[1]
    tr = _row_tile(S, W)

    def one(c, dn, gv):
        r = _rstd(c)
        chat = c * r
        dchat = dn * gv
        proj = jnp.mean(dchat * chat, axis=-1, keepdims=True)
        return r * (dchat - chat * proj), dn * chat

    def body(z_ref, dnq_ref, dnkv_ref, dkr_ref, gq_ref, gkv_ref, dz_ref, dgq_ref, dgkv_ref):
        first = pl.program_id(0) == 0
        dcq, pq = one(z_ref[:, :Rq], dnq_ref[...], gq_ref[...])
        dckv, pkv = one(z_ref[:, Rq:Rq + Rkv], dnkv_ref[...], gkv_ref[...])
        _acc_rows(dgq_ref, pq, first)
        _acc_rows(dgkv_ref, pkv, first)
        dz_ref[:, :Rq] = dcq.astype(BF16)
        dz_ref[:, Rq:Rq + Rkv] = dckv.astype(BF16)
        dz_ref[:, Rq + Rkv:] = dkr_ref[...].astype(BF16)

    return pl.pallas_call(
        body, name=name,
        out_shape=(jax.ShapeDtypeStruct((S, W), BF16), jax.ShapeDtypeStruct((1, Rq), F32),
                   jax.ShapeDtypeStruct((1, Rkv), F32)), grid=(S // tr,),
        in_specs=[pl.BlockSpec((tr, W), lambda i: (i, 0)), pl.BlockSpec((tr, Rq), lambda i: (i, 0)),
                  pl.BlockSpec((tr, Rkv), lambda i: (i, 0)), pl.BlockSpec((tr, LANES), lambda i: (i, 0)),
                  pl.BlockSpec((1, Rq), lambda i: (0, 0)), pl.BlockSpec((1, Rkv), lambda i: (0, 0))],
        out_specs=(pl.BlockSpec((tr, W), lambda i: (i, 0)), pl.BlockSpec((1, Rq), lambda i: (0, 0)),
                   pl.BlockSpec((1, Rkv), lambda i: (0, 0))),
        compiler_params=_params("arbitrary"),
    )(z_lat, dnq, dnkv, dkr, g_q, g_kv)


def _rot(x, lo32):
    a = pltpu.roll(x, 32, 1)
    b = pltpu.roll(x, LANES - 32, 1)
    return jnp.where(lo32, -b, a)


def _rot_t(g, lo32):
    a = pltpu.roll(g, 32, 1)
    b = pltpu.roll(g, LANES - 32, 1)
    return jnp.where(lo32, b, -a)


def _mla_pack_fwd(q_raw, kv_raw, z_lat, cos, sin, kr_off, *, name):
    S = q_raw.shape[0]
    H = kv_raw.shape[1] // (MLA_NOPE + MLA_V)
    W = z_lat.shape[1]
    scale = MLA_QK ** -0.5
    tr = min(S, 128)
    nope_w = H * MLA_NOPE

    def body(q_ref, kv_ref, z_ref, cos_ref, sin_ref, qp_ref, kp_ref, v_ref):
        lane = lax.broadcasted_iota(jnp.int32, (tr, LANES), 1)
        lo32 = (lane % 64) < 32
        lo64 = lane < 64
        c = cos_ref[...]
        s = sin_ref[...]
        kr = z_ref[:, kr_off:kr_off + LANES]
        kr = (kr * c + _rot(kr, lo32) * s).astype(BF16)
        for hp in range(H // 2):
            xb = q_ref[:, nope_w + hp * LANES:nope_w + (hp + 1) * LANES]
            rb = (xb * c + _rot(xb, lo32) * s) * scale
            for e in range(2):
                h = 2 * hp + e
                base = h * MLA_QK_PAD
                qp_ref[:, base:base + LANES] = (q_ref[:, h * LANES:(h + 1) * LANES] * scale).astype(BF16)
                keep = lo64 if e == 0 else jnp.logical_not(lo64)
                qp_ref[:, base + LANES:base + 2 * LANES] = jnp.where(keep, rb, 0.0).astype(BF16)
                kp_ref[:, base:base + LANES] = kv_ref[:, h * LANES:(h + 1) * LANES].astype(BF16)
                kp_ref[:, base + LANES:base + 2 * LANES] = kr
        v_ref[...] = kv_ref[:, nope_w:].astype(BF16)

    return pl.pallas_call(
        body, name=name,
        out_shape=(jax.ShapeDtypeStruct((S, H * MLA_QK_PAD), BF16), jax.ShapeDtypeStruct((S, H * MLA_QK_PAD), BF16),
                   jax.ShapeDtypeStruct((S, H * MLA_V), BF16)), grid=(S // tr,),
        in_specs=[pl.BlockSpec((tr, q_raw.shape[1]), lambda i: (i, 0)), pl.BlockSpec((tr, kv_raw.shape[1]), lambda i: (i, 0)),
                  pl.BlockSpec((tr, W), lambda i: (i, 0)), pl.BlockSpec((tr, LANES), lambda i: (i, 0)),
                  pl.BlockSpec((tr, LANES), lambda i: (i, 0))],
        out_specs=(pl.BlockSpec((tr, H * MLA_QK_PAD), lambda i: (i, 0)), pl.BlockSpec((tr, H * MLA_QK_PAD), lambda i: (i, 0)),
                   pl.BlockSpec((tr, H * MLA_V), lambda i: (i, 0))),
        compiler_params=_params("parallel"),
    )(q_raw, kv_raw, z_lat, cos, sin)


def _mla_pack_bwd(dqp, dkp, dv, cos, sin, *, name):
    S = dqp.shape[0]
    H = dv.shape[1] // MLA_V
    scale = MLA_QK ** -0.5
    tr = min(S, 128)
    nope_w = H * MLA_NOPE

    def body(dqp_ref, dkp_ref, dv_ref, cos_ref, sin_ref, dq_ref, dkv_ref, dkr_ref):
        lane = lax.broadcasted_iota(jnp.int32, (tr, LANES), 1)
        lo32 = (lane % 64) < 32
        lo64 = lane < 64
        c = cos_ref[...]
        s = sin_ref[...]
        dkr2 = jnp.zeros((tr, LANES), F32)
        for hp in range(H // 2):
            be = (2 * hp) * MLA_QK_PAD
            bo = (2 * hp + 1) * MLA_QK_PAD
            g = jnp.where(lo64, dqp_ref[:, be + LANES:be + 2 * LANES], dqp_ref[:, bo + LANES:bo + 2 * LANES]) * scale
            dq_ref[:, nope_w + hp * LANES:nope_w + (hp + 1) * LANES] = (g * c + _rot_t(g * s, lo32)).astype(BF16)
            for h, base in ((2 * hp, be), (2 * hp + 1, bo)):
                dq_ref[:, h * LANES:(h + 1) * LANES] = (dqp_ref[:, base:base + LANES] * scale).astype(BF16)
                dkv_ref[:, h * LANES:(h + 1) * LANES] = dkp_ref[:, base:base + LANES].astype(BF16)
                dkr2 = dkr2 + dkp_ref[:, base + LANES:base + 2 * LANES]
        dkr2 = dkr2 * c + _rot_t(dkr2 * s, lo32)
        dkr2 = dkr2 + pltpu.roll(dkr2, 64, 1)
        dkr_ref[...] = jnp.where(lo64, dkr2, 0.0)
        dkv_ref[:, nope_w:] = dv_ref[...].astype(BF16)

    return pl.pallas_call(
        body, name=name,
        out_shape=(jax.ShapeDtypeStruct((S, nope_w + H * MLA_ROPE), BF16), jax.ShapeDtypeStruct((S, 2 * nope_w), BF16),
                   jax.ShapeDtypeStruct((S, LANES), F32)), grid=(S // tr,),
        in_specs=[pl.BlockSpec((tr, H * MLA_QK_PAD), lambda i: (i, 0)), pl.BlockSpec((tr, H * MLA_QK_PAD), lambda i: (i, 0)),
                  pl.BlockSpec((tr, H * MLA_V), lambda i: (i, 0)), pl.BlockSpec((tr, LANES), lambda i: (i, 0)),
                  pl.BlockSpec((tr, LANES), lambda i: (i, 0))],
        out_specs=(pl.BlockSpec((tr, nope_w + H * MLA_ROPE), lambda i: (i, 0)), pl.BlockSpec((tr, 2 * nope_w), lambda i: (i, 0)),
                   pl.BlockSpec((tr, LANES), lambda i: (i, 0))),
        compiler_params=_params("parallel"),
    )(dqp, dkp, dv, cos, sin)


FLASH_HB = 2


def _causal_pairs(nb):
    qi = [i for i in range(nb) for j in range(i + 1)]
    kj = [j for i in range(nb) for j in range(i + 1)]
    return jnp.asarray(qi, jnp.int32), jnp.asarray(kj, jnp.int32)


def _scores(q, k, diagonal, t):
    s = lax.dot_general(q, k, (((1,), (1,)), ((), ())), preferred_element_type=F32)
    if diagonal:
        row = lax.broadcasted_iota(jnp.int32, (t, t), 0)
        col = lax.broadcasted_iota(jnp.int32, (t, t), 1)
        s = jnp.where(col <= row, s, NEG)
    return s


def _flash_fwd(qp, kp, v, *, name):
    S = qp.shape[0]
    H = v.shape[1] // MLA_V
    t = min(S, 512)
    nb = S // t
    HB = 2 * FLASH_HB
    qi, kj = _causal_pairs(nb)
    QW, VW = MLA_QK_PAD, MLA_V

    def body(qi_ref, kj_ref, q_ref, k_ref, v_ref, o_ref, lse_ref, m_s, l_s, acc_s):
        pr = pl.program_id(1)
        i = qi_ref[pr]
        j = kj_ref[pr]

        @pl.when(j == 0)
        def _():
            m_s[...] = jnp.full_like(m_s, NEG)
            l_s[...] = jnp.zeros_like(l_s)
            acc_s[...] = jnp.zeros_like(acc_s)

        def step(diagonal):
            for hh in range(HB):
                s = _scores(q_ref[:, hh * QW:(hh + 1) * QW], k_ref[:, hh * QW:(hh + 1) * QW], diagonal, t)
                m_prev = m_s[hh]
                m_cur = jnp.maximum(m_prev, jnp.max(s, axis=1, keepdims=True))
                alpha = jnp.exp(m_prev - m_cur)
                p = jnp.exp(s - m_cur[:, :1])
                l_new = alpha * l_s[hh] + jnp.sum(p, axis=1, keepdims=True)
                acc = alpha * acc_s[hh] + jnp.dot(p.astype(BF16), v_ref[:, hh * VW:(hh + 1) * VW], preferred_element_type=F32)
                if diagonal:
                    o_ref[:, hh * VW:(hh + 1) * VW] = acc / l_new
                    lse_ref[hh] = m_cur + jnp.log(l_new)
                else:
                    l_s[hh] = l_new
                    acc_s[hh] = acc
                    m_s[hh] = m_cur

        @pl.when(i != j)
        def _():
            step(False)

        @pl.when(i == j)
        def _():
            step(True)

    return pl.pallas_call(
        body, name=name,
        out_shape=(jax.ShapeDtypeStruct((S, H * VW), F32), jax.ShapeDtypeStruct((H, S, LANES), F32)),
        grid_spec=pltpu.PrefetchScalarGridSpec(
            num_scalar_prefetch=2, grid=(H // HB, qi.shape[0]),
            in_specs=[pl.BlockSpec((t, HB * QW), lambda g, p, qi, kj: (qi[p], g)),
                      pl.BlockSpec((t, HB * QW), lambda g, p, qi, kj: (kj[p], g)),
                      pl.BlockSpec((t, HB * VW), lambda g, p, qi, kj: (kj[p], g))],
            out_specs=(pl.BlockSpec((t, HB * VW), lambda g, p, qi, kj: (qi[p], g)),
                       pl.BlockSpec((HB, t, LANES), lambda g, p, qi, kj: (g, qi[p], 0))),
            scratch_shapes=[pltpu.VMEM((HB, t, LANES), F32), pltpu.VMEM((HB, t, LANES), F32), pltpu.VMEM((HB, t, VW), F32)]),
        compiler_params=_params("parallel", "arbitrary"),
    )(qi, kj, qp, kp, v)


def _flash_bwd(qp, kp, v, o, do, lse, *, name):
    S = qp.shape[0]
    H = v.shape[1] // MLA_V
    t = min(S, 512)
    nb = S // t
    HB = FLASH_HB
    qi, kj = _causal_pairs(nb)
    QW, VW = MLA_QK_PAD, MLA_V
    tn = (((0,), (0,)), ((), ()))
    nt = (((1,), (1,)), ((), ()))

    def body(qi_ref, kj_ref, q_ref, k_ref, v_ref, o_ref, do_ref, lse_ref, dq_ref, dk_ref, dv_ref, dq_s):
        pr = pl.program_id(1)
        i = qi_ref[pr]
        j = kj_ref[pr]
        rows = pl.ds(pl.multiple_of(j * t, t), t)

        @pl.when(pr == 0)
        def _():
            dk_ref[...] = jnp.zeros_like(dk_ref)
            dv_ref[...] = jnp.zeros_like(dv_ref)

        @pl.when(j == 0)
        def _():
            dq_s[...] = jnp.zeros_like(dq_s)

        def step(diagonal):
            for hh in range(HB):
                q = q_ref[:, hh * QW:(hh + 1) * QW]
                k = k_ref[:, hh * QW:(hh + 1) * QW]
                dob = do_ref[:, hh * VW:(hh + 1) * VW]
                p = jnp.exp(_scores(q, k, diagonal, t) - lse_ref[hh][:, :1])
                delta = jnp.sum(dob.astype(F32) * o_ref[:, hh * VW:(hh + 1) * VW], axis=1, keepdims=True)
                dp = lax.dot_general(dob, v_ref[:, hh * VW:(hh + 1) * VW], nt, preferred_element_type=F32)
                dsb = (p * (dp - delta)).astype(BF16)
                dv_ref[rows, hh * VW:(hh + 1) * VW] += lax.dot_general(p.astype(BF16), dob, tn, preferred_element_type=F32)
                dk_ref[rows, hh * QW:(hh + 1) * QW] += lax.dot_general(dsb, q, tn, preferred_element_type=F32)
                dq = dq_s[:, hh * QW:(hh + 1) * QW] + jnp.dot(dsb, k, preferred_element_type=F32)
                if diagonal:
                    dq_ref[:, hh * QW:(hh + 1) * QW] = dq
                else:
                    dq_s[:, hh * QW:(hh + 1) * QW] = dq

        @pl.when(i != j)
        def _():
            step(False)

        @pl.when(i == j)
        def _():
            step(True)

    qside = lambda g, p, qi, kj: (qi[p], g)
    kside = lambda g, p, qi, kj: (kj[p], g)
    whole = lambda g, p, qi, kj: (0, g)
    return pl.pallas_call(
        body, name=name,
        out_shape=(jax.ShapeDtypeStruct((S, H * QW), F32), jax.ShapeDtypeStruct((S, H * QW), F32),
                   jax.ShapeDtypeStruct((S, H * VW), F32)),
        grid_spec=pltpu.PrefetchScalarGridSpec(
            num_scalar_prefetch=2, grid=(H // HB, qi.shape[0]),
            in_specs=[pl.BlockSpec((t, HB * QW), qside), pl.BlockSpec((t, HB * QW), kside), pl.BlockSpec((t, HB * VW), kside),
                      pl.BlockSpec((t, HB * VW), qside), pl.BlockSpec((t, HB * VW), qside),
                      pl.BlockSpec((HB, t, LANES), lambda g, p, qi, kj: (g, qi[p], 0))],
            out_specs=(pl.BlockSpec((t, HB * QW), qside), pl.BlockSpec((S, HB * QW), whole), pl.BlockSpec((S, HB * VW), whole)),
            scratch_shapes=[pltpu.VMEM((t, HB * QW), F32)]),
        compiler_params=_params("parallel", "arbitrary"),
    )(qi, kj, qp, kp, v, o, do, lse)


def _swa_kv_halves(blk, hf, lo):
    if hf == 0:
        a = jnp.where(lo, blk, 0.0)
        b = pltpu.roll(a, 64, 1)
    else:
        b = jnp.where(lo, 0.0, blk)
        a = pltpu.roll(b, 64, 1)
    return a.astype(BF16), b.astype(BF16)


def _swa_softmax(qs, kx, bias, neg0, sk):
    s = lax.dot_general(qs, kx, (((1,), (1,)), ((), ())), preferred_element_type=F32) + bias + neg0
    m = jnp.maximum(jnp.max(s, axis=1, keepdims=True), sk)
    e = jnp.exp(s - m)
    es = jnp.exp(sk - m)
    inv = 1.0 / (jnp.sum(e, axis=1, keepdims=True) + es)
    return e * inv, es * inv


def _swa_stack(ref, kvh, npb, scale=None):
    parts = [ref[:, (kvh * npb + pb) * LANES:(kvh * npb + pb + 1) * LANES] for pb in range(npb)]
    x = jnp.concatenate(parts, axis=0)
    return x if scale is None else x * scale


def _swa_sink_col(sink_ref, kvh, e, npb):
    row = lax.broadcasted_iota(jnp.int32, (npb * SWA_BLOCK, 1), 0)
    col = jnp.zeros((npb * SWA_BLOCK, 1), F32) + sink_ref[2 * (kvh * npb) + e]
    for pb in range(1, npb):
        col = jnp.where(row >= pb * SWA_BLOCK, sink_ref[2 * (kvh * npb + pb) + e], col)
    return col


def _swa_fwd(z_swa, bias_st, sinks, *, name):
    S, W = z_swa.shape
    npb = bias_st.shape[1] // SWA_BLOCK
    NH = 2 * SWA_KVH * npb
    QW = NH * SWA_HD
    KW = SWA_KVH * SWA_HD
    nb = S // SWA_BLOCK
    B = SWA_BLOCK
    assert SWA_KVH % 2 == 0 and W == QW + 2 * KW

    def body(sink_ref, q_ref, kvc_ref, kvp_ref, b_ref, o_ref):
        n = pl.program_id(0)
        lo = lax.broadcasted_iota(jnp.int32, (2 * B, LANES), 1) < 64
        col = lax.broadcasted_iota(jnp.int32, (npb * B, 2 * B), 1)
        neg0 = jnp.where(jnp.logical_and(col < B, n == 0), NEG, 0.0)
        for kb in range(SWA_KVH // 2):
            kblk = jnp.concatenate([kvp_ref[:, kb * LANES:(kb + 1) * LANES], kvc_ref[:, kb * LANES:(kb + 1) * LANES]], axis=0)
            vblk = jnp.concatenate([kvp_ref[:, KW + kb * LANES:KW + (kb + 1) * LANES],
                                    kvc_ref[:, KW + kb * LANES:KW + (kb + 1) * LANES]], axis=0)
            for hf in range(2):
                kvh = 2 * kb + hf
                ks = _swa_kv_halves(kblk, hf, lo)
                vs = _swa_kv_halves(vblk, hf, lo)
                qs = _swa_stack(q_ref, kvh, npb, SWA_HD ** -0.5).astype(BF16)
                acc = jnp.zeros((npb * B, LANES), F32)
                for e in range(2):
                    p, _ = _swa_softmax(qs, ks[e], b_ref[2 * kvh + e], neg0, _swa_sink_col(sink_ref, kvh, e, npb))
                    acc = acc + jnp.dot(p.astype(BF16), vs[e], preferred_element_type=F32)
                for pb in range(npb):
                    P = kvh * npb + pb
                    o_ref[:, P * LANES:(P + 1) * LANES] = acc[pb * B:(pb + 1) * B]

    kvcol = QW // (2 * KW)
    assert QW % (2 * KW) == 0
    return pl.pallas_call(
        body, name=name,
        out_shape=jax.ShapeDtypeStruct((S, QW), F32), grid=(nb,),
        in_specs=[SMEM_FULL, pl.BlockSpec((B, QW), lambda n: (n, 0)), pl.BlockSpec((B, 2 * KW), lambda n: (n, kvcol)),
                  pl.BlockSpec((B, 2 * KW), lambda n: (jnp.maximum(n - 1, 0), kvcol)),
                  pl.BlockSpec(bias_st.shape, lambda n: (0, 0, 0))],
        out_specs=pl.BlockSpec((B, QW), lambda n: (n, 0)),
        compiler_params=_params("parallel"),
    )(sinks, z_swa, z_swa, z_swa, bias_st)


def _swa_bwd(z_swa, bias_st, sinks, o, do, *, name):
    S, W = z_swa.shape
    npb = bias_st.shape[1] // SWA_BLOCK
    NH = 2 * SWA_KVH * npb
    QW = NH * SWA_HD
    KW = SWA_KVH * SWA_HD
    nb = S // SWA_BLOCK
    B = SWA_BLOCK
    scale = SWA_HD ** -0.5
    tn = (((0,), (0,)), ((), ()))
    nt = (((1,), (1,)), ((), ()))

    def fold(x, hf, lo):
        x = x + pltpu.roll(x, 64, 1)
        return jnp.where(lo, x, 0.0) if hf == 0 else jnp.where(lo, 0.0, x)

    def body(sink_ref, q_ref, kvc_ref, kvp_ref, b_ref, o_ref, do_ref, dz_ref, dbias_ref, dsink_ref,
             cq_s, ck_s, cv_s, nq_s, nk_s, nv_s, pk_s, pv_s):
        n = pl.program_id(0)

        @pl.when(n == 0)
        def _():
            dbias_ref[...] = jnp.zeros_like(dbias_ref)
            dsink_ref[...] = jnp.zeros_like(dsink_ref)
            cq_s[...] = jnp.zeros_like(cq_s)
            ck_s[...] = jnp.zeros_like(ck_s)
            cv_s[...] = jnp.zeros_like(cv_s)

        @pl.when(n == nb)
        def _():
            pk_s[...] = jnp.zeros_like(pk_s)
            pv_s[...] = jnp.zeros_like(pv_s)

        @pl.when(n < nb)
        def _():
            lo = lax.broadcasted_iota(jnp.int32, (2 * B, LANES), 1) < 64
            lo1 = lax.broadcasted_iota(jnp.int32, (npb * B, LANES), 1) < 64
            lane1 = lax.broadcasted_iota(jnp.int32, (1, LANES), 1)
            col = lax.broadcasted_iota(jnp.int32, (npb * B, 2 * B), 1)
            neg0 = jnp.where(jnp.logical_and(col < B, n == 0), NEG, 0.0)
            dsink = jnp.zeros((1, LANES), F32)
            for kb in range(SWA_KVH // 2):
                kblk = jnp.concatenate([kvp_ref[:, kb * LANES:(kb + 1) * LANES], kvc_ref[:, kb * LANES:(kb + 1) * LANES]], axis=0)
                vblk = jnp.concatenate([kvp_ref[:, KW + kb * LANES:KW + (kb + 1) * LANES],
                                        kvc_ref[:, KW + kb * LANES:KW + (kb + 1) * LANES]], axis=0)
                dkblk = jnp.zeros((2 * B, LANES), F32)
                dvblk = jnp.zeros((2 * B, LANES), F32)
                for hf in range(2):
                    kvh = 2 * kb + hf
                    ks = _swa_kv_halves(kblk, hf, lo)
                    vs = _swa_kv_halves(vblk, hf, lo)
                    qs = _swa_stack(q_ref, kvh, npb, scale).astype(BF16)
                    dos = _swa_stack(do_ref, kvh, npb)
                    prod = dos * _swa_stack(o_ref, kvh, npb)
                    dob = dos.astype(BF16)
                    dkj = jnp.zeros((2 * B, LANES), F32)
                    dvj = jnp.zeros((2 * B, LANES), F32)
                    dqs = jnp.zeros((npb * B, LANES), F32)
                    for e in range(2):
                        keep = lo1 if e == 0 else jnp.logical_not(lo1)
                        p, psink = _swa_softmax(qs, ks[e], b_ref[2 * kvh + e], neg0, _swa_sink_col(sink_ref, kvh, e, npb))
                        delta = jnp.sum(jnp.where(keep, prod, 0.0), axis=1, keepdims=True)
                        dp = lax.dot_general(dob, vs[e], nt, preferred_element_type=F32)
                        ds = p * (dp - delta)
                        dbias_ref[2 * kvh + e] += ds
                        pd = psink * delta
                        for pb in range(npb):
                            dsh = -jnp.sum(pd[pb * B:(pb + 1) * B], axis=0, keepdims=True)
                            dsink = dsink + jnp.where(lane1 == 2 * (kvh * npb + pb) + e, dsh, 0.0)
                        dsb = ds.astype(BF16)
                        dqs = dqs + jnp.dot(dsb, ks[e], preferred_element_type=F32)
                        keep2 = lo if e == 0 else jnp.logical_not(lo)
                        dkj = dkj + jnp.where(keep2, lax.dot_general(dsb, qs, tn, preferred_element_type=F32), 0.0)
                        dvj = dvj + jnp.where(keep2, lax.dot_general(p.astype(BF16), dob, tn, preferred_element_type=F32), 0.0)
                    for pb in range(npb):
                        P = kvh * npb + pb
                        nq_s[:, P * LANES:(P + 1) * LANES] = dqs[pb * B:(pb + 1) * B] * scale
                    dkblk = dkblk + fold(dkj, hf, lo)
                    dvblk = dvblk + fold(dvj, hf, lo)
                pk_s[:, kb * LANES:(kb + 1) * LANES] = dkblk[:B]
                nk_s[:, kb * LANES:(kb + 1) * LANES] = dkblk[B:]
                pv_s[:, kb * LANES:(kb + 1) * LANES] = dvblk[:B]
                nv_s[:, kb * LANES:(kb + 1) * LANES] = dvblk[B:]
            dsink_ref[...] += dsink

        dz_ref[:, :QW] = cq_s[...].astype(BF16)
        dz_ref[:, QW:QW + KW] = (ck_s[...] + pk_s[...]).astype(BF16)
        dz_ref[:, QW + KW:] = (cv_s[...] + pv_s[...]).astype(BF16)

        @pl.when(n < nb)
        def _():
            cq_s[...] = nq_s[...]
            ck_s[...] = nk_s[...]
            cv_s[...] = nv_s[...]

    kvcol = QW // (2 * KW)
    cur = lambda n: (jnp.minimum(n, nb - 1), 0)
    return pl.pallas_call(
        body, name=name,
        out_shape=(jax.ShapeDtypeStruct((S, W), BF16), jax.ShapeDtypeStruct(bias_st.shape, F32),
                   jax.ShapeDtypeStruct((1, LANES), F32)),
        grid=(nb + 1,),
        in_specs=[SMEM_FULL, pl.BlockSpec((B, QW), cur), pl.BlockSpec((B, 2 * KW), lambda n: (jnp.minimum(n, nb - 1), kvcol)),
                  pl.BlockSpec((B, 2 * KW), lambda n: (jnp.maximum(jnp.minimum(n, nb - 1) - 1, 0), kvcol)),
                  pl.BlockSpec(bias_st.shape, lambda n: (0, 0, 0)), pl.BlockSpec((B, QW), cur), pl.BlockSpec((B, QW), cur)],
        out_specs=(pl.BlockSpec((B, W), lambda n: (jnp.maximum(n - 1, 0), 0)),
                   pl.BlockSpec(bias_st.shape, lambda n: (0, 0, 0)), pl.BlockSpec((1, LANES), lambda n: (0, 0))),
        scratch_shapes=[pltpu.VMEM((B, QW), F32), pltpu.VMEM((B, KW), F32), pltpu.VMEM((B, KW), F32),
                        pltpu.VMEM((B, QW), F32), pltpu.VMEM((B, KW), F32), pltpu.VMEM((B, KW), F32),
                        pltpu.VMEM((B, KW), F32), pltpu.VMEM((B, KW), F32)],
        compiler_params=_params("arbitrary"),
    )(sinks, z_swa, z_swa, z_swa, bias_st, o, do)


def _gate_fwd(zg, o_a, o_b, *, name):
    S, D = o_a.shape
    tr = min(S, 512)

    def body(z_ref, a_ref, b_ref, m_ref):
        ga = jax.nn.sigmoid(z_ref[:, :PAIR])
        gb = jax.nn.sigmoid(z_ref[:, PAIR:])
        m_ref[...] = (ga * a_ref[...] + gb * b_ref[...]).astype(BF16)

    col = pl.BlockSpec((tr, PAIR), lambda i, j: (i, j))
    return pl.pallas_call(
        body, name=name, out_shape=jax.ShapeDtypeStruct((S, D), BF16), grid=(S // tr, D // PAIR),
        in_specs=[pl.BlockSpec((tr, 2 * PAIR), lambda i, j: (i, j)), col, col], out_specs=col,
        compiler_params=_params("parallel", "parallel"),
    )(zg, o_a, o_b)


def _gate_bwd(dmix, zg, o_a, o_b, *, name):
    S, D = o_a.shape
    tr = min(S, 512)

    def body(d_ref, z_ref, a_ref, b_ref, da_ref, db_ref, dz_ref):
        d = d_ref[...]
        ga = jax.nn.sigmoid(z_ref[:, :PAIR])
        gb = jax.nn.sigmoid(z_ref[:, PAIR:])
        da_ref[...] = (d * ga).astype(BF16)
        db_ref[...] = d * gb
        dz_ref[:, :PAIR] = (d * a_ref[...] * (ga * (1.0 - ga))).astype(BF16)
        dz_ref[:, PAIR:] = (d * b_ref[...] * (gb * (1.0 - gb))).astype(BF16)

    col = pl.BlockSpec((tr, PAIR), lambda i, j: (i, j))
    wide = pl.BlockSpec((tr, 2 * PAIR), lambda i, j: (i, j))
    return pl.pallas_call(
        body, name=name,
        out_shape=(jax.ShapeDtypeStruct((S, D), BF16), jax.ShapeDtypeStruct((S, D), F32), jax.ShapeDtypeStruct((S, 2 * D), BF16)),
        grid=(S // tr, D // PAIR), in_specs=[col, wide, col, col], out_specs=(col, col, wide),
        compiler_params=_params("parallel", "parallel"),
    )(dmix, zg, o_a, o_b)


def _conv_taps(t_ref, prev_ref, i, tr):
    cur = t_ref[...].astype(F32)
    live = (i > 0).astype(F32)
    p6 = prev_ref[14:15, :].astype(F32) * live
    p7 = prev_ref[15:16, :].astype(F32) * live
    row = lax.broadcasted_iota(jnp.int32, cur.shape, 0)
    t1 = jnp.where(row == 0, p7, pltpu.roll(cur, 1, 0))
    t2 = jnp.where(row == 0, p6, jnp.where(row == 1, p7, pltpu.roll(cur, 2, 0)))
    return cur, t1, t2


def _conv_u(t_ref, prev_ref, w_ref, b_ref, i, tr):
    cur, t1, t2 = _conv_taps(t_ref, prev_ref, i, tr)
    u = ((b_ref[...] + w_ref[0:1, :] * t2) + w_ref[1:2, :] * t1) + w_ref[2:3, :] * cur
    return u, cur, t1, t2


def _conv_specs(tr, S):
    blk = pl.BlockSpec((tr, 2 * PAIR), lambda j, i: (i, j))
    prev = pl.BlockSpec((16, 2 * PAIR), lambda j, i: (jnp.maximum(i * (tr // 16) - 1, 0), j))
    w3 = pl.BlockSpec((3, 2 * PAIR), lambda j, i: (0, j))
    w1 = pl.BlockSpec((1, 2 * PAIR), lambda j, i: (0, j))
    return blk, prev, w3, w1


def _conv_gate_fwd(t, cw, cb, *, name):
    S, F2 = t.shape
    tr = min(S, 512)
    blk, prev, w3, w1 = _conv_specs(tr, S)

    def body(t_ref, prev_ref, w_ref, b_ref, a_ref):
        u, _, _, _ = _conv_u(t_ref, prev_ref, w_ref, b_ref, pl.program_id(1), tr)
        a_ref[...] = (jax.nn.silu(u[:, :PAIR]) * u[:, PAIR:]).astype(BF16)

    return pl.pallas_call(
        body, name=name, out_shape=jax.ShapeDtypeStruct((S, F2 // 2), BF16), grid=(F2 // (2 * PAIR), S // tr),
        in_specs=[blk, prev, w3, w1], out_specs=pl.BlockSpec((tr, PAIR), lambda j, i: (i, j)),
        compiler_params=_params("parallel", "parallel"),
    )(t, t, cw, cb)


def _conv_gate_bwd(t, da, cw, cb, *, name):
    S, F2 = t.shape
    tr = min(S, 256)
    blk, prev, w3, w1 = _conv_specs(tr, S)

    def body(t_ref, prev_ref, da_ref, w_ref, b_ref, du_ref, dw_ref, db_ref):
        i = pl.program_id(1)
        u, cur, t1, t2 = _conv_u(t_ref, prev_ref, w_ref, b_ref, i, tr)
        u1 = u[:, :PAIR]
        u2 = u[:, PAIR:]
        d = da_ref[...].astype(F32)
        sg = jax.nn.sigmoid(u1)
        du1 = d * u2 * (sg * (1.0 + u1 * (1.0 - sg)))
        du2 = d * (u1 * sg)
        du = jnp.concatenate([du1, du2], axis=1)
        du_ref[...] = du.astype(BF16)
        first = i == 0
        _acc_rows(db_ref, du, first)
        dw = jnp.concatenate([jnp.sum(du * t2, axis=0, keepdims=True), jnp.sum(du * t1, axis=0, keepdims=True),
                              jnp.sum(du * cur, axis=0, keepdims=True)], axis=0)

        @pl.when(first)
        def _():
            dw_ref[...] = dw

        @pl.when(jnp.logical_not(first))
        def _():
            dw_ref[...] += dw

    return pl.pallas_call(
        body, name=name,
        out_shape=(jax.ShapeDtypeStruct((S, F2), BF16), jax.ShapeDtypeStruct((3, F2), F32), jax.ShapeDtypeStruct((1, F2), F32)),
        grid=(F2 // (2 * PAIR), S // tr),
        in_specs=[blk, prev, pl.BlockSpec((tr, PAIR), lambda j, i: (i, j)), w3, w1], out_specs=(blk, w3, w1),
        compiler_params=_params("parallel", "arbitrary"),
    )(t, t, da, cw, cb)


def _conv_bwd_dt(du, cw, *, name):
    S, F2 = du.shape
    tr = min(S, 512)
    nrow = S // tr
    blk, _, w3, _ = _conv_specs(tr, S)
    nxt = pl.BlockSpec((16, 2 * PAIR), lambda j, i: (jnp.minimum((i + 1) * (tr // 16), S // 16 - 1), j))

    def body(d_ref, next_ref, w_ref, dt_ref):
        i = pl.program_id(1)
        cur = d_ref[...].astype(F32)
        live = (i < nrow - 1).astype(F32)
        n0 = next_ref[0:1, :].astype(F32) * live
        n1 = next_ref[1:2, :].astype(F32) * live
        row = lax.broadcasted_iota(jnp.int32, cur.shape, 0)
        d1 = jnp.where(row == tr - 1, n0, pltpu.roll(cur, tr - 1, 0))
        d2 = jnp.where(row == tr - 1, n1, jnp.where(row == tr - 2, n0, pltpu.roll(cur, tr - 2, 0)))
        dt_ref[...] = ((w_ref[2:3, :] * cur + w_ref[1:2, :] * d1) + w_ref[0:1, :] * d2).astype(BF16)

    return pl.pallas_call(
        body, name=name, out_shape=jax.ShapeDtypeStruct((S, F2), BF16), grid=(F2 // (2 * PAIR), nrow),
        in_specs=[blk, nxt, w3], out_specs=blk, compiler_params=_params("parallel", "parallel"),
    )(du, du, cw)


def _ada_fwd(c_all, w, b, *, name):
    Bn, D = c_all.shape
    N = w.shape[1]
    tn = _tile(N, 512)

    def body(c_ref, w_ref, b_ref, o_ref):
        o_ref[...] = jnp.dot(jax.nn.silu(c_ref[...]), w_ref[...], preferred_element_type=F32, precision=HIGHEST) + b_ref[...]

    return pl.pallas_call(
        body, name=name, out_shape=jax.ShapeDtypeStruct((Bn, N), F32), grid=(N // tn,),
        in_specs=[pl.BlockSpec((Bn, D), lambda j: (0, 0)), pl.BlockSpec((D, tn), lambda j: (0, j)),
                  pl.BlockSpec((1, tn), lambda j: (0, j))],
        out_specs=pl.BlockSpec((Bn, tn), lambda j: (0, j)), compiler_params=_params("parallel"),
    )(c_all, w, b)


def _ada_bwd(c_all_t, dmod, *, name):
    D, Bn = c_all_t.shape
    N = dmod.shape[1]
    tm = _tile(D, 512, 8)
    tn = _tile(N, 1536)

    def body(c_ref, d_ref, o_ref):
        o_ref[...] = jnp.dot(jax.nn.silu(c_ref[...]), d_ref[...], preferred_element_type=F32, precision=HIGHEST)

    return pl.pallas_call(
        body, name=name, out_shape=jax.ShapeDtypeStruct((D, N), F32), grid=(D // tm, N // tn),
        in_specs=[pl.BlockSpec((tm, Bn), lambda i, j: (i, 0)), pl.BlockSpec((Bn, tn), lambda i, j: (0, j))],
        out_specs=pl.BlockSpec((tm, tn), lambda i, j: (i, j)), compiler_params=_params("parallel", "parallel"),
    )(c_all_t, dmod)


def _adamw(w, g, m, v, *, name):
    R, C = w.shape
    tr = R if R * C <= (1 << 18) else _tile(R, max(8, (1 << 18) // C), 8)

    def body(w_ref, g_ref, m_ref, v_ref, d_ref, nm_ref, nv_ref):
        gv = g_ref[...]
        nm = ADAM_B1 * m_ref[...] + (1.0 - ADAM_B1) * gv
        nv = ADAM_B2 * v_ref[...] + (1.0 - ADAM_B2) * (gv * gv)
        m_hat = nm / (1.0 - ADAM_B1 ** ADAM_STEP)
        v_hat = nv / (1.0 - ADAM_B2 ** ADAM_STEP)
        d_ref[...] = -ADAM_LR * (m_hat / (jnp.sqrt(v_hat) + ADAM_EPS) + ADAM_WD * w_ref[...])
        nm_ref[...] = nm
        nv_ref[...] = nv

    blk = pl.BlockSpec((tr, C), lambda i: (i, 0))
    shp = jax.ShapeDtypeStruct((R, C), F32)
    return pl.pallas_call(
        body, name=name, out_shape=(shp, shp, shp), grid=(R // tr,), in_specs=[blk] * 4, out_specs=(blk,) * 3,
        compiler_params=_params("parallel"),
    )(w, g, m, v)


def _place():
    x, y, c = lax.axis_index("x"), lax.axis_index("y"), lax.axis_index("c")
    return x, y, c, [(1 - x, y), (x, 1 - y), (1 - x, 1 - y)]


def _remote(src, dst, send_sem, recv_sem, dev):
    return pltpu.make_async_remote_copy(src_ref=src, dst_ref=dst, send_sem=send_sem, recv_sem=recv_sem,
                                        device_id=dev, device_id_type=MESH)


def _allgather8(v, *, name):
    R, C = v.shape

    def body(v_ref, out_ref, send_sems, recv_sems, local_sem):
        x, y, c, chips = _place()
        me, sibling = (x, y, c), (x, y, 1 - c)

        def rows(px, py, pc):
            return out_ref.at[pl.ds((4 * px + 2 * py + pc) * R, R), :]

        def copy(k, block, to, src=None):
            return _remote(rows(*block) if src is None else src, rows(*block), send_sems.at[k], recv_sems.at[k], to)

        mine = pltpu.make_async_copy(v_ref, rows(*me), local_sem)
        mine.start()
        first = [copy(0, me, sibling, src=v_ref)]
        first += [copy(1 + j, me, (*chip, c), src=v_ref) for j, chip in enumerate(chips)]
        for cp in first:
            cp.start()
        passed = [copy(4 + j, (*chip, c), sibling) for j, chip in enumerate(chips)]
        for j, chip in enumerate(chips):
            copy(1 + j, (*chip, c), me).wait_recv()
            passed[j].start()
        copy(0, sibling, me).wait_recv()
        for j, chip in enumerate(chips):
            copy(4 + j, (*chip, 1 - c), me).wait_recv()
        for cp in first + passed:
            cp.wait_send()
        mine.wait()

    out = pl.pallas_call(
        body, name=name, out_shape=jax.ShapeDtypeStruct((N_DEV * R, C), v.dtype),
        in_specs=[VMEM_FULL], out_specs=VMEM_FULL,
        scratch_shapes=[pltpu.SemaphoreType.DMA((7,)), pltpu.SemaphoreType.DMA((7,)), pltpu.SemaphoreType.DMA],
    )(v)
    return out.reshape(N_DEV, R, C)


SEM = pl.BlockSpec(memory_space=pltpu.SEMAPHORE)
HBM = pl.BlockSpec(memory_space=pltpu.HBM)
EFFECT = pltpu.SideEffectType.DATAFLOW_SIDE_EFFECTING
DMA_SEM = pltpu.SemaphoreType.DMA(())


def _in_hbm(a):
    return pltpu.with_memory_space_constraint(a, pltpu.HBM)


def _three_halves(land, r2):
    return land.at[pl.ds(0, N_CHIP - 1), pl.ds(0, r2)]


def _gather_start(ws, after, *, name):
    n = len(ws)
    na = len(after)
    lands = [lax.empty((N_CHIP,) + w.shape, w.dtype) for w in ws]

    def body(*refs):
        w_refs, land_refs = refs[:n], refs[n:2 * n]
        send, recv = refs[2 * n + na:3 * n + na], refs[3 * n + na:4 * n + na]
        token = refs[6 * n + na]
        x, y, c, chips = _place()
        k = 2 * x + y
        for i in range(n):
            r2 = ws[i].shape[0] // 2
            for cx, cy in chips:
                _remote(w_refs[i].at[pl.ds(c * r2, r2)], land_refs[i].at[k, pl.ds(c * r2, r2)], send[i], recv[i],
                        (cx, cy, c)).start()
        token[...] = jnp.zeros_like(token)

    outs = pl.pallas_call(
        body, name=name,
        out_shape=[DMA_SEM] * (2 * n) + [pltpu.HBM(w.shape, w.dtype) for w in ws] + [pltpu.HBM(l.shape, l.dtype) for l in lands]
        + [jax.ShapeDtypeStruct((8, LANES), F32)],
        in_specs=[HBM] * (2 * n) + [ANY] * na, out_specs=[SEM] * (2 * n) + [HBM] * (2 * n) + [VMEM_FULL],
        input_output_aliases={i: 2 * n + i for i in range(2 * n)},
        compiler_params=pltpu.CompilerParams(has_side_effects=EFFECT),
    )(*[_in_hbm(w) for w in ws], *[_in_hbm(l) for l in lands], *after)
    return outs[:n], outs[n:2 * n], outs[2 * n:3 * n], outs[3 * n:4 * n], outs[4 * n]


def _gather_forward(send, recv, ws, lands, after, *, name):
    n = len(ws)

    def body(*refs):
        w_refs, land_refs = refs[:n], refs[n:2 * n]
        send1, recv1 = refs[2 * n:3 * n], refs[3 * n:4 * n]
        send2, recv2 = refs[4 * n + 1 + 2 * n:4 * n + 1 + 3 * n], refs[4 * n + 1 + 3 * n:4 * n + 1 + 4 * n]
        x, y, c, chips = _place()
        sibling = (x, y, 1 - c)
        for i in range(n):
            r2 = ws[i].shape[0] // 2
            win = _three_halves(land_refs[i], r2)
            done = _remote(win, win, send1[i], recv1[i], sibling)
            done.wait_send()
            done.wait_recv()
            for cx, cy in chips:
                got = land_refs[i].at[2 * cx + cy, pl.ds(c * r2, r2)]
                _remote(got, got, send2[i], recv2[i], sibling).start()

    outs = pl.pallas_call(
        body, name=name,
        out_shape=[pltpu.HBM(w.shape, w.dtype) for w in ws] + [pltpu.HBM(l.shape, l.dtype) for l in lands] + [DMA_SEM] * (2 * n),
        in_specs=[HBM] * (2 * n) + [SEM] * (2 * n) + [ANY], out_specs=[HBM] * (2 * n) + [SEM] * (2 * n),
        input_output_aliases={i: i for i in range(2 * n)},
        compiler_params=pltpu.CompilerParams(has_side_effects=EFFECT),
    )(*ws, *lands, *send, *recv, after)
    return outs[2 * n:3 * n], outs[3 * n:4 * n], outs[n:2 * n]


def _gather_finish(send, recv, lands, after, *, name):
    n = len(lands)

    def body(*refs):
        land_refs = refs[:n]
        send2, recv2 = refs[n:2 * n], refs[2 * n:3 * n]
        x, y, c, _ = _place()
        for i in range(n):
            win = _three_halves(land_refs[i], lands[i].shape[1] // 2)
            done = _remote(win, win, send2[i], recv2[i], (x, y, 1 - c))
            done.wait_send()
            done.wait_recv()

    return pl.pallas_call(
        body, name=name,
        out_shape=[pltpu.HBM(l.shape, l.dtype) for l in lands],
        in_specs=[HBM] * n + [SEM] * (2 * n) + [ANY], out_specs=[HBM] * n,
        input_output_aliases={i: i for i in range(n)},
        compiler_params=pltpu.CompilerParams(has_side_effects=EFFECT),
    )(*lands, *send, *recv, after)


def _scatter_start(gs, *, name):
    n = len(gs)
    lands = [lax.empty((N_DEV, g.shape[1] // 2, g.shape[2]), g.dtype) for g in gs]

    def body(*refs):
        g_refs, land_refs = refs[:n], refs[n:2 * n]
        send, recv = refs[2 * n:3 * n], refs[3 * n:4 * n]
        token = refs[6 * n]
        x, y, c, chips = _place()
        k = 2 * x + y
        me = 2 * k + c
        for i in range(n):
            r2 = gs[i].shape[1] // 2
            for cx, cy in chips:
                for cc in range(2):
                    _remote(g_refs[i].at[2 * cx + cy, pl.ds(cc * r2, r2)], land_refs[i].at[me], send[i], recv[i],
                            (cx, cy, cc)).start()
            _remote(g_refs[i].at[k, pl.ds((1 - c) * r2, r2)], land_refs[i].at[me], send[i], recv[i], (x, y, 1 - c)).start()
        token[...] = jnp.zeros_like(token)

    outs = pl.pallas_call(
        body, name=name,
        out_shape=[DMA_SEM] * (2 * n) + [pltpu.HBM(g.shape, g.dtype) for g in gs] + [pltpu.HBM(l.shape, l.dtype) for l in lands]
        + [jax.ShapeDtypeStruct((8, LANES), F32)],
        in_specs=[HBM] * (2 * n), out_specs=[SEM] * (2 * n) + [HBM] * (2 * n) + [VMEM_FULL],
        input_output_aliases={i: 2 * n + i for i in range(2 * n)},
        compiler_params=pltpu.CompilerParams(has_side_effects=EFFECT),
    )(*[_in_hbm(g) for g in gs], *[_in_hbm(l) for l in lands])
    return outs[:n], outs[n:2 * n], outs[2 * n:3 * n], outs[3 * n:4 * n], outs[4 * n]


def _scatter_wait(send, recv, gs, lands, after, *, name):
    n = len(gs)

    def body(*refs):
        land_refs = refs[n:2 * n]
        send1, recv1 = refs[2 * n:3 * n], refs[3 * n:4 * n]
        x, y, c, _ = _place()
        for i in range(n):
            win = land_refs[i].at[pl.ds(0, N_DEV - 1)]
            done = _remote(win, win, send1[i], recv1[i], (x, y, 1 - c))
            done.wait_send()
            done.wait_recv()

    outs = pl.pallas_call(
        body, name=name,
        out_shape=[pltpu.HBM(g.shape, g.dtype) for g in gs] + [pltpu.HBM(l.shape, l.dtype) for l in lands],
        in_specs=[HBM] * (2 * n) + [SEM] * (2 * n) + [ANY], out_specs=[HBM] * (2 * n),
        input_output_aliases={i: i for i in range(2 * n)},
        compiler_params=pltpu.CompilerParams(has_side_effects=EFFECT),
    )(*gs, *lands, *send, *recv, after)
    return outs[:n], outs[n:]


def _share_halves(ts, *, name):
    n = len(ts)

    def body(*refs):
        outs = refs[n:2 * n]
        send_sems, recv_sems = refs[2 * n:]
        x, y, c, _ = _place()
        sibling = (x, y, 1 - c)
        cps = []
        for i in range(n):
            r2 = ts[i].shape[0] // 2
            mine = outs[i].at[pl.ds(c * r2, r2)]
            cps.append(_remote(mine, mine, send_sems.at[i], recv_sems.at[i], sibling))
            cps[-1].start()
        for i in range(n):
            r2 = ts[i].shape[0] // 2
            got = outs[i].at[pl.ds((1 - c) * r2, r2)]
            _remote(got, got, send_sems.at[i], recv_sems.at[i], sibling).wait_recv()
        for cp in cps:
            cp.wait_send()

    return pl.pallas_call(
        body, name=name,
        out_shape=[jax.ShapeDtypeStruct(t.shape, t.dtype) for t in ts],
        in_specs=[ANY] * n, out_specs=[ANY] * n, input_output_aliases={i: i for i in range(n)},
        scratch_shapes=[pltpu.SemaphoreType.DMA((n,)), pltpu.SemaphoreType.DMA((n,))],
    )(*ts)


def _sum_pieces(land, g, idx, *, name):
    _, r2, C = land.shape
    tr = _tile(r2, max(16, (1 << 19) // C), 16)
    nr = r2 // tr

    def body(idx_ref, land_ref, own_ref, o_ref, acc_ref):
        d = pl.program_id(1)
        mine = d == idx_ref[0]

        @pl.when(d == 0)
        def _():
            acc_ref[...] = jnp.zeros_like(acc_ref)

        @pl.when(mine)
        def _():
            acc_ref[...] += own_ref[...].astype(F32)

        @pl.when(jnp.logical_not(mine))
        def _():
            acc_ref[...] += land_ref[...].astype(F32)

        @pl.when(d == N_DEV - 1)
        def _():
            o_ref[...] = acc_ref[...]

    return pl.pallas_call(
        body, name=name, out_shape=jax.ShapeDtypeStruct((2 * r2, C), F32),
        grid_spec=pltpu.PrefetchScalarGridSpec(
            num_scalar_prefetch=1, grid=(nr, N_DEV),
            in_specs=[pl.BlockSpec((None, tr, C), lambda i, d, ix: (jnp.where(d == ix[0], (d + 1) % N_DEV, d), i, 0)),
                      pl.BlockSpec((None, tr, C), lambda i, d, ix: (ix[1], ix[2] * nr + i, 0))],
            out_specs=pl.BlockSpec((tr, C), lambda i, d, ix: (ix[2] * nr + i, 0)),
            scratch_shapes=[pltpu.VMEM((tr, C), F32)]),
        compiler_params=_params("parallel", "arbitrary"),
    )(idx, land, g)


def _sum_devices(v, *, name):
    n, R, C = v.shape

    def body(v_ref, o_ref):
        acc = v_ref[0]
        for j in range(1, n):
            acc = acc + v_ref[j]
        o_ref[...] = acc

    return pl.pallas_call(body, name=name, out_shape=jax.ShapeDtypeStruct((R, C), F32),
                          in_specs=[VMEM_FULL], out_specs=VMEM_FULL)(v)


def _pair(a, b):
    n = a.shape[1]
    parts = []
    for j in range(n // PAIR):
        parts += [a[:, j * PAIR:(j + 1) * PAIR], b[:, j * PAIR:(j + 1) * PAIR]]
    return jnp.concatenate(parts, axis=1)


def _unpair(p):
    nt = p.shape[1] // (2 * PAIR)
    a = jnp.concatenate([p[:, 2 * j * PAIR:(2 * j + 1) * PAIR] for j in range(nt)], axis=1)
    b = jnp.concatenate([p[:, (2 * j + 1) * PAIR:(2 * j + 2) * PAIR] for j in range(nt)], axis=1)
    return a, b


def _from_col_shards(g):
    return jnp.transpose(g, (1, 0, 2)).reshape(g.shape[1], N_CHIP * g.shape[2])


def _to_col_shards(w):
    R, N = w.shape
    return jnp.transpose(w.reshape(R, N_CHIP, N // N_CHIP), (1, 0, 2))


def _split_heads(w, widths):
    R, N = w.shape
    per = sum(widths)
    w3 = w.reshape(R, N // per, per)
    lo = w3[:, :, :widths[0]].reshape(R, -1)
    hi = w3[:, :, widths[0]:].reshape(R, -1)
    return jnp.concatenate([lo, hi], axis=1)


def _merge_heads(w, widths):
    R, N = w.shape
    H = N // sum(widths)
    lo = w[:, :H * widths[0]].reshape(R, H, widths[0])
    hi = w[:, H * widths[0]:].reshape(R, H, widths[1])
    return jnp.concatenate([lo, hi], axis=2).reshape(R, N)


def _t5_bucket(dist):
    max_exact = REL_BUCKETS // 2
    n = jnp.maximum(dist, 0)
    large = max_exact + (jnp.log(jnp.maximum(n, 1).astype(F32) / max_exact)
                         / jnp.log(jnp.asarray(REL_MAX_DIST / max_exact, F32))
                         * (REL_BUCKETS - max_exact)).astype(jnp.int32)
    large = jnp.minimum(large, REL_BUCKETS - 1)
    return jnp.where(n < max_exact, n, large)


def _rel_tables():
    a = jnp.arange(SWA_BLOCK)
    b = jnp.arange(2 * SWA_BLOCK)
    dist = SWA_BLOCK + a[:, None] - b[None, :]
    valid = jnp.logical_and(dist >= 0, dist < SWA_BLOCK)
    onehot = jnp.logical_and(_t5_bucket(dist)[..., None] == jnp.arange(REL_BUCKETS), valid[..., None])
    onehot = onehot.astype(F32).reshape(2 * SWA_BLOCK * SWA_BLOCK, REL_BUCKETS)
    negmask = jnp.where(valid, 0.0, NEG).astype(F32).reshape(1, -1)
    return onehot, negmask


def _rope_tables(S):
    pos = jnp.arange(S, dtype=F32)
    inv = ROPE_THETA ** (-jnp.arange(0, MLA_ROPE, 2, dtype=F32) / MLA_ROPE)
    ang = pos[:, None] * inv[None, :]
    ang = jnp.concatenate([ang, ang, ang, ang], axis=-1)
    return jnp.cos(ang), jnp.sin(ang)


def _flat_pad(parts, rows=8):
    flat = jnp.concatenate([p.reshape(1, -1) for p in parts], axis=1)
    n = flat.shape[1]
    width = -(-n // (rows * LANES)) * LANES
    return jnp.pad(flat, ((0, 0), (0, rows * width - n))).reshape(rows, width)


def _unflat(vec, shapes):
    flat = vec.reshape(-1)
    out, off = [], 0
    for s in shapes:
        n = 1
        for d in s:
            n *= d
        out.append(flat[off:off + n].reshape(s))
        off += n
    return out


def kernel(x, c, w_ada, b_ada, g_pre_mix, g_post_mix, w_in, g_q_lat, w_uq, g_kv_lat, w_ukv, rel_bias, sinks, w_o, g_pre_ffn, g_post_ffn, w_up, conv_w, conv_b, w_down, loss_target, m_w_ada, m_b_ada, m_g_pre_mix, m_g_post_mix, m_w_in, m_g_q_lat, m_w_uq, m_g_kv_lat, m_w_ukv, m_rel_bias, m_sinks, m_w_o, m_g_pre_ffn, m_g_post_ffn, m_w_up, m_conv_w, m_conv_b, m_w_down, v_w_ada, v_b_ada, v_g_pre_mix, v_g_post_mix, v_w_in, v_g_q_lat, v_w_uq, v_g_kv_lat, v_w_ukv, v_rel_bias, v_sinks, v_w_o, v_g_pre_ffn, v_g_post_ffn, v_w_up, v_conv_w, v_conv_b, v_w_down):
    S, D = x.shape[1], x.shape[2]
    Rq, Rkv = g_q_lat.shape[1], g_kv_lat.shape[1]
    H = D // MLA_V
    NH = D // SWA_HD
    KW = SWA_KVH * SWA_HD
    F = w_down.shape[1] * N_CHIP
    xi, yi, ci = lax.axis_index("x"), lax.axis_index("y"), lax.axis_index("c")
    chip = 2 * xi + yi
    me = 2 * chip + ci
    x2, tgt = x[0], loss_target[0]

    c_all = _allgather8(jnp.broadcast_to(c, (8, D)), name="gather_c")[:, 0, :]
    n3 = w_ada.shape[2]
    mod_part = _ada_fwd(c_all, w_ada[0], lax.dynamic_slice(b_ada, (0, chip * n3), (1, n3)), name="ada_fwd")
    mod_all = _allgather8(mod_part, name="gather_mod")
    mod_me = lax.dynamic_index_in_dim(mod_all[0::2], me, axis=1, keepdims=False).reshape(1, 6 * D)
    sh1, sc1, gt1, sh2, sc2, gt2 = [mod_me[:, k * D:(k + 1) * D] for k in range(6)]
    cw_all = _allgather8(jnp.pad(conv_w[0], ((0, 5), (0, 0))), name="gather_conv_w")[0::2, :3]

    big = [w_in[0], w_uq[0], w_ukv[0], w_o[0], w_up[0], w_down[0]]
    local = [w.astype(BF16) for w in big]
    send1, recv1, srcs, lands, token = _gather_start(local, (mod_all, cw_all), name="gather_start")
    onehot, negmask = _rel_tables()
    npb = NH // (2 * SWA_KVH)
    rb_st = jnp.transpose(rel_bias.T.reshape(SWA_KVH, npb, 2, REL_BUCKETS), (0, 2, 1, 3)).reshape(NH, REL_BUCKETS)
    bias_m = (_matmul(rb_st, onehot.T, tie=token, name="rel_bias_table") + negmask).reshape(
        2 * SWA_KVH, npb * SWA_BLOCK, 2 * SWA_BLOCK)
    h = _modnorm_fwd(x2, g_pre_mix, sc1, sh1, name="pre_mix_norm")

    def whole(land, i):
        return lax.dynamic_update_index_in_dim(land, local[i], chip, 0)

    s2, r2, l_in = _gather_forward(send1[:1], recv1[:1], srcs[:1], lands[:1], h, name="gather_forward_in")
    (l_in,) = _gather_finish(s2, r2, l_in, h, name="gather_finish_in")
    s2b, r2b, l_b = _gather_forward(send1[1:4], recv1[1:4], srcs[1:4], lands[1:4], l_in, name="gather_forward_attn")
    gin = whole(l_in, 0)
    win = _from_col_shards(gin)
    o_kr = Rq + Rkv
    o_q = o_kr + MLA_ROPE
    o_g = o_q + NH * SWA_HD + 2 * KW
    w_lat = jnp.concatenate([win[:, :o_q], win[:, o_kr:o_q]], axis=1)
    w_swa = win[:, o_q:o_g]
    w_gate = _pair(win[:, o_g:o_g + D], win[:, o_g + D:])
    w_in_all = jnp.concatenate([w_lat, w_swa, w_gate], axis=1)
    n_lat, n_swa = w_lat.shape[1], w_swa.shape[1]
    cw_full = _from_col_shards(cw_all)
    cw = _pair(cw_full[:, :F], cw_full[:, F:])
    cb = _pair(conv_b[:, :F], conv_b[:, F:])
    cos, sin = _rope_tables(S)
    sink_v = sinks.reshape(NH)

    z_lat = _matmul(h, w_lat, name="in_proj_lat")
    z_swa = _matmul(h, w_swa, name="in_proj_swa")
    zg = _matmul(h, w_gate, name="in_proj_gate")
    nq, nkv = _lat_norm_fwd(z_lat, g_q_lat, g_kv_lat, name="lat_norm")
    l_uq, l_ukv, l_o = _gather_finish(s2b, r2b, l_b, nq, name="gather_finish_attn")
    wuq = _split_heads(_from_col_shards(whole(l_uq, 1)), (MLA_NOPE, MLA_ROPE))
    wukv = _split_heads(_from_col_shards(whole(l_ukv, 2)), (MLA_NOPE, MLA_V))
    wo = whole(l_o, 3).reshape(D, D)
    q_raw = _matmul(nq, wuq, name="uq_proj")
    kv_raw = _matmul(nkv, wukv, name="ukv_proj")
    qp, kp, vv = _mla_pack_fwd(q_raw, kv_raw, z_lat, cos, sin, o_kr, name="mla_pack")
    o_a, lse = _flash_fwd(qp, kp, vv, name="mla_attn")
    s2c, r2c, l_c = _gather_forward(send1[4:], recv1[4:], srcs[4:], lands[4:], o_a, name="gather_forward_ffn")
    o_b = _swa_fwd(z_swa, bias_m, sink_v, name="swa_attn")
    mixin = _gate_fwd(zg, o_a, o_b, name="gate_mix")
    mix = _matmul(mixin, wo, name="o_proj")
    x1 = _resnorm_fwd(x2, mix, g_post_mix, gt1, name="post_mix_norm")
    h2 = _modnorm_fwd(x1, g_pre_ffn, sc2, sh2, name="pre_ffn_norm")
    l_up, l_down = _gather_finish(s2c, r2c, l_c, h2, name="gather_finish_ffn")
    wup_full = _from_col_shards(whole(l_up, 4))
    wup = _pair(wup_full[:, :F], wup_full[:, F:])
    wdown = whole(l_down, 5).reshape(F, D)
    t = _matmul(h2, wup, out_dtype=BF16, name="up_proj")
    a = _conv_gate_fwd(t, cw, cb, name="conv_gate")
    yv = _matmul(a, wdown, name="down_proj")
    dout, loss_tile = _resnorm_loss(x1, yv, g_post_ffn, gt2, tgt, name="post_ffn_norm_loss")

    idx = jnp.stack([me, chip, ci]).astype(jnp.int32)
    big_params = dict(w_in=(w_in, m_w_in, v_w_in), w_uq=(w_uq, m_w_uq, v_w_uq), w_ukv=(w_ukv, m_w_ukv, v_w_ukv),
                      w_o=(w_o, m_w_o, v_w_o), w_up=(w_up, m_w_up, v_w_up), w_down=(w_down, m_w_down, v_w_down))
    res = {}

    def start(nms, gs):
        send, recv, gsrc, glands, tok = _scatter_start(gs, name="grads_start_" + nms[0])
        return (nms, send, recv, gsrc, glands), tok

    def finish(pending, after):
        nms, send, recv, gsrc, glands = pending
        gsrc, glands = _scatter_wait(send, recv, gsrc, glands, after, name="grads_wait_" + nms[0])
        halves = [_sum_pieces(l, g, idx, name="grad_sum_" + nm) for l, g, nm in zip(glands, gsrc, nms)]
        for nm, g in zip(nms, _share_halves(halves, name="grads_share_" + nms[0])):
            w, m, v = big_params[nm]
            res[nm] = (g,) + tuple(_adamw(w[0], g, m[0], v[0], name="adamw_" + nm))

    dy, dg_post_ffn, dgt2 = _resnorm_bwd(dout, yv, g_post_ffn, gt2, name="post_ffn_norm_bwd")
    dw_down = _matmul(a, dy, ta=True, out_dtype=BF16, name="down_proj_dw")
    p_down, tok = start(["w_down"], [dw_down.reshape(N_CHIP, F // N_CHIP, D)])
    da = _matmul(dy, wdown, tb=True, out_dtype=BF16, tie=tok, name="down_proj_dx")
    du, dcw_p, dcb_p = _conv_gate_bwd(t, da, cw, cb, name="conv_gate_bwd")
    dt = _conv_bwd_dt(du, cw, name="conv_bwd_dt")
    dw_up_p = _matmul(h2, dt, ta=True, out_dtype=BF16, name="up_proj_dw")
    p_up, tok = start(["w_up"], [_to_col_shards(jnp.concatenate(_unpair(dw_up_p), axis=1))])
    dh2 = _matmul(dt, wup, tb=True, tie=tok, name="up_proj_dx")
    finish(p_down, dh2)
    dx1, dg_pre_ffn, dsc2, dsh2 = _modnorm_bwd(dh2, x1, g_pre_ffn, sc2, dout, name="pre_ffn_norm_bwd")
    dmix, dg_post_mix, dgt1 = _resnorm_bwd(dx1, mix, g_post_mix, gt1, name="post_mix_norm_bwd")
    dw_o = _matmul(mixin, dmix, ta=True, out_dtype=BF16, name="o_proj_dw")
    p_o, tok = start(["w_o"], [dw_o.reshape(N_CHIP, D // N_CHIP, D)])
    dmixin = _matmul(dmix, wo, tb=True, tie=tok, name="o_proj_dx")
    do_a, do_b, dzg = _gate_bwd(dmixin, zg, o_a, o_b, name="gate_mix_bwd")
    dqp, dkp, dvv = _flash_bwd(qp, kp, vv, o_a, do_a, lse, name="mla_attn_bwd")
    dq_raw, dkv_raw, dkr = _mla_pack_bwd(dqp, dkp, dvv, cos, sin, name="mla_pack_bwd")
    finish(p_up, dq_raw)
    dw_uq_p = _matmul(nq, dq_raw, ta=True, out_dtype=BF16, name="uq_proj_dw")
    dw_ukv_p = _matmul(nkv, dkv_raw, ta=True, out_dtype=BF16, name="ukv_proj_dw")
    p_qkv, tok = start(["w_uq", "w_ukv"], [_to_col_shards(_merge_heads(dw_uq_p, (MLA_NOPE, MLA_ROPE))),
                                           _to_col_shards(_merge_heads(dw_ukv_p, (MLA_NOPE, MLA_V)))])
    dnq = _matmul(dq_raw, wuq, tb=True, tie=tok, name="uq_proj_dx")
    dnkv = _matmul(dkv_raw, wukv, tb=True, name="ukv_proj_dx")
    finish(p_o, dnkv)
    dz_lat, dg_q, dg_kv = _lat_norm_bwd(z_lat, dnq, dnkv, dkr, g_q_lat, g_kv_lat, name="lat_norm_bwd")
    dz_swa, dbias, dsink = _swa_bwd(z_swa, bias_m, sink_v, o_b, do_b, name="swa_attn_bwd")
    dz = jnp.concatenate([dz_lat, dz_swa, dzg], axis=1)
    dw_in_p = _matmul(h, dz, ta=True, out_dtype=BF16, name="in_proj_dw")
    dga, dgb = _unpair(dw_in_p[:, n_lat + n_swa:])
    dw_in = jnp.concatenate([dw_in_p[:, :o_q], dw_in_p[:, n_lat:n_lat + n_swa], dga, dgb], axis=1)
    p_in, tok = start(["w_in"], [_to_col_shards(dw_in)])
    dh = _matmul(dz, w_in_all, tb=True, tie=tok, name="in_proj_dx")
    finish(p_qkv, dh)
    grad_x, dg_pre_mix, dsc1, dsh1 = _modnorm_bwd(dh, x2, g_pre_mix, sc1, dx1, name="pre_mix_norm_bwd")
    drel_st = _matmul(dbias.reshape(NH, -1), onehot, name="rel_bias_bwd")
    drel = jnp.transpose(drel_st.reshape(SWA_KVH, 2, npb, REL_BUCKETS), (0, 2, 1, 3)).reshape(NH, REL_BUCKETS).T

    dcw = jnp.concatenate(_unpair(dcw_p), axis=1)
    dcb = jnp.concatenate(_unpair(dcb_p), axis=1)
    dmod = jnp.concatenate([dsh1, dsc1, dgt1, dsh2, dsc2, dgt2], axis=1)
    small = [dmod, dg_pre_mix, dg_post_mix, dg_pre_ffn, dg_post_ffn, dg_q, dg_kv, drel, dsink[:, :NH], dcb, dcw]
    shapes = [p.shape for p in small]
    small_all = _allgather8(_flat_pad(small), name="gather_small_grads")
    tot = _unflat(_sum_devices(small_all, name="sum_small_grads"), shapes)
    g_b_ada, g_pre_mix_g, g_post_mix_g, g_pre_ffn_g, g_post_ffn_g, g_q_g, g_kv_g, g_rel, g_sinks, g_cb, g_cw_full = tot
    dmod_all = small_all.reshape(N_DEV, -1)[:, :6 * D]
    g_w_ada = _ada_bwd(c_all.T, lax.dynamic_slice(dmod_all, (0, chip * n3), (N_DEV, n3)), name="ada_bwd")
    ncw = conv_w.shape[2]
    g_cw = lax.dynamic_slice(g_cw_full, (0, chip * ncw), (3, ncw))

    res["w_ada"] = (g_w_ada,) + tuple(_adamw(w_ada[0], g_w_ada, m_w_ada[0], v_w_ada[0], name="adamw_w_ada"))
    finish(p_in, g_w_ada)
    snames = ["b_ada", "g_pre_mix", "g_post_mix", "g_pre_ffn", "g_post_ffn", "g_q_lat", "g_kv_lat", "rel_bias", "sinks",
              "conv_b", "conv_w"]
    sw = [b_ada, g_pre_mix, g_post_mix, g_pre_ffn, g_post_ffn, g_q_lat, g_kv_lat, rel_bias, sinks, conv_b, conv_w]
    sm = [m_b_ada, m_g_pre_mix, m_g_post_mix, m_g_pre_ffn, m_g_post_ffn, m_g_q_lat, m_g_kv_lat, m_rel_bias, m_sinks,
          m_conv_b, m_conv_w]
    sv = [v_b_ada, v_g_pre_mix, v_g_post_mix, v_g_pre_ffn, v_g_post_ffn, v_g_q_lat, v_g_kv_lat, v_rel_bias, v_sinks,
          v_conv_b, v_conv_w]
    sg = [g_b_ada, g_pre_mix_g, g_post_mix_g, g_pre_ffn_g, g_post_ffn_g, g_q_g, g_kv_g, g_rel, g_sinks, g_cb, g_cw]
    sshapes = [w.shape for w in sw]
    sd, snm, snv = _adamw(_flat_pad(sw), _flat_pad(sg), _flat_pad(sm), _flat_pad(sv), name="adamw_small")
    sd, snm, snv = _unflat(sd, sshapes), _unflat(snm, sshapes), _unflat(snv, sshapes)
    for k, nm in enumerate(snames):
        res[nm] = (sg[k].reshape(sshapes[k]), sd[k], snm[k], snv[k])

    order = ["w_ada", "b_ada", "g_pre_mix", "g_post_mix", "w_in", "g_q_lat", "w_uq", "g_kv_lat", "w_ukv", "rel_bias", "sinks",
             "w_o", "g_pre_ffn", "g_post_ffn", "w_up", "conv_w", "conv_b", "w_down"]
    ref_shapes = dict(w_ada=w_ada.shape, w_in=w_in.shape, w_uq=w_uq.shape, w_ukv=w_ukv.shape, w_o=w_o.shape,
                      w_up=w_up.shape, w_down=w_down.shape)
    outs = []
    for k in range(4):
        for nm in order:
            arr = res[nm][k]
            outs.append(arr.reshape(ref_shapes[nm]) if nm in ref_shapes else arr)
    loss = lax.psum(loss_tile[0, 0], ("x", "y", "c"))
    return (loss, grad_x[None], *outs)
```

```python
import functools

import jax
import jax.numpy as jnp
from jax import lax
from jax.experimental import pallas as pl
from jax.experimental.pallas import tpu as pltpu

F32 = jnp.float32
BF16 = jnp.bfloat16
MESH = pl.DeviceIdType.MESH
HIGHEST = lax.Precision.HIGHEST

N_DEV = 8
N_CHIP = 4
LANES = 128
MLA_NOPE = 128
MLA_ROPE = 64
MLA_V = 128
MLA_QK = MLA_NOPE + MLA_ROPE
MLA_QK_PAD = 256
ROPE_THETA = 10000.0
SWA_HD = 64
SWA_KVH = 4
SWA_BLOCK = 128
REL_BUCKETS = 32
REL_MAX_DIST = 128
PAIR = 512
EPS = 1e-6
NEG = -1e30
ADAM_LR = 0.001
ADAM_B1 = 0.9
ADAM_B2 = 0.999
ADAM_EPS = 1e-08
ADAM_WD = 0.01
ADAM_STEP = 10

ANY = pl.BlockSpec(memory_space=pl.ANY)
VMEM_FULL = pl.BlockSpec(memory_space=pltpu.VMEM)
SMEM_FULL = pl.BlockSpec(memory_space=pltpu.SMEM)


def _params(*sem):
    return pltpu.CompilerParams(dimension_semantics=sem if sem else None)


def _tied(body, tie):
    if tie is None:
        return body, [], []

    def tied_body(tie_ref, *refs):
        body(*refs)

    return tied_body, [ANY], [tie]


def _tile(n, pref, unit=LANES):
    best = None
    for t in range(unit, min(n, pref) + 1, unit):
        if n % t == 0:
            best = t
    return n if best is None else best


def _matmul(a, b, *, ta=False, tb=False, out_dtype=F32, tie=None, name):
    if ta:
        K, M = a.shape
    else:
        M, K = a.shape
    if tb:
        N, K2 = b.shape
    else:
        K2, N = b.shape
    assert K == K2, (a.shape, b.shape, ta, tb)
    exact = a.dtype == F32
    tn = _tile(N, 1536)
    tk = _tile(K, 2048)
    nk = K // tk
    tm = M if M < 8 else _tile(M, 1024, LANES if ta else 8)
    dn = (((0 if ta else 1,), (1 if tb else 0,)), ((), ()))

    def product(a_ref, b_ref):
        return lax.dot_general(a_ref[...], b_ref[...], dn, preferred_element_type=F32,
                               precision=HIGHEST if exact else None)

    def body_acc(a_ref, b_ref, o_ref, acc_ref):
        k = pl.program_id(2)

        @pl.when(k == 0)
        def _():
            acc_ref[...] = product(a_ref, b_ref)

        @pl.when(jnp.logical_and(k > 0, k < nk - 1))
        def _():
            acc_ref[...] += product(a_ref, b_ref)

        @pl.when(k == nk - 1)
        def _():
            o_ref[...] = (acc_ref[...] + product(a_ref, b_ref)).astype(o_ref.dtype)

    def body_one(a_ref, b_ref, o_ref):
        o_ref[...] = product(a_ref, b_ref).astype(o_ref.dtype)

    body, tspec, targ = _tied(body_one if nk == 1 else body_acc, tie)
    a_spec = pl.BlockSpec((tk, tm), lambda i, j, k: (k, i)) if ta else pl.BlockSpec((tm, tk), lambda i, j, k: (i, k))
    b_spec = pl.BlockSpec((tn, tk), lambda i, j, k: (j, k)) if tb else pl.BlockSpec((tk, tn), lambda i, j, k: (k, j))
    return pl.pallas_call(
        body, name=name,
        out_shape=jax.ShapeDtypeStruct((M, N), out_dtype),
        grid=(M // tm, N // tn, nk),
        in_specs=tspec + [a_spec, b_spec],
        out_specs=pl.BlockSpec((tm, tn), lambda i, j, k: (i, j)),
        scratch_shapes=[] if nk == 1 else [pltpu.VMEM((tm, tn), F32)],
        compiler_params=_params("parallel", "parallel", "arbitrary"),
    )(*targ, a, b)


def _row_tile(S, width):
    return _tile(S, max(8, (1 << 19) // width), 8)


def _rstd(x):
    return lax.rsqrt(jnp.mean(x * x, axis=-1, keepdims=True) + EPS)


def _acc_rows(ref, val, first):
    s = jnp.sum(val, axis=0, keepdims=True)

    @pl.when(first)
    def _():
        ref[...] = s

    @pl.when(jnp.logical_not(first))
    def _():
        ref[...] += s


def _modnorm_fwd(x, g, sc, sh, *, name):
    S, D = x.shape
    tr = _row_tile(S, D)

    def body(x_ref, g_ref, sc_ref, sh_ref, h_ref):
        xv = x_ref[...]
        n = (xv * _rstd(xv)) * g_ref[...]
        h_ref[...] = (n * (1.0 + sc_ref[...]) + sh_ref[...]).astype(BF16)

    row = pl.BlockSpec((tr, D), lambda i: (i, 0))
    vec = pl.BlockSpec((1, D), lambda i: (0, 0))
    return pl.pallas_call(
        body, name=name, out_shape=jax.ShapeDtypeStruct((S, D), BF16), grid=(S // tr,),
        in_specs=[row, vec, vec, vec], out_specs=row, compiler_params=_params("parallel"),
    )(x, g, sc, sh)


def _modnorm_bwd(dh, x, g, sc, dres, *, name):
    S, D = x.shape
    tr = _row_tile(S, D)

    def body(dh_ref, x_ref, g_ref, sc_ref, dres_ref, dx_ref, dg_ref, dsc_ref, dsh_ref):
        first = pl.program_id(0) == 0
        xv = x_ref[...]
        dhv = dh_ref[...]
        gv = g_ref[...]
        r = _rstd(xv)
        xhat = xv * r
        _acc_rows(dsh_ref, dhv, first)
        _acc_rows(dsc_ref, dhv * (xhat * gv), first)
        dn = dhv * (1.0 + sc_ref[...])
        _acc_rows(dg_ref, dn * xhat, first)
        dxhat = dn * gv
        proj = jnp.mean(dxhat * xhat, axis=-1, keepdims=True)
        dx_ref[...] = r * (dxhat - xhat * proj) + dres_ref[...]

    row = pl.BlockSpec((tr, D), lambda i: (i, 0))
    vec = pl.BlockSpec((1, D), lambda i: (0, 0))
    vshape = jax.ShapeDtypeStruct((1, D), F32)
    return pl.pallas_call(
        body, name=name,
        out_shape=(jax.ShapeDtypeStruct((S, D), F32), vshape, vshape, vshape), grid=(S // tr,),
        in_specs=[row, row, vec, vec, row], out_specs=(row, vec, vec, vec),
        compiler_params=_params("arbitrary"),
    )(dh, x, g, sc, dres)


def _resnorm_fwd(xres, m, g, gt, *, name):
    S, D = xres.shape
    tr = _row_tile(S, D)

    def body(x_ref, m_ref, g_ref, gt_ref, o_ref):
        mv = m_ref[...]
        o_ref[...] = x_ref[...] + gt_ref[...] * ((mv * _rstd(mv)) * g_ref[...])

    row = pl.BlockSpec((tr, D), lambda i: (i, 0))
    vec = pl.BlockSpec((1, D), lambda i: (0, 0))
    return pl.pallas_call(
        body, name=name, out_shape=jax.ShapeDtypeStruct((S, D), F32), grid=(S // tr,),
        in_specs=[row, row, vec, vec], out_specs=row, compiler_params=_params("parallel"),
    )(xres, m, g, gt)


def _resnorm_loss(xres, m, g, gt, target, *, name):
    S, D = xres.shape
    tr = _row_tile(S, D)

    def body(x_ref, m_ref, g_ref, gt_ref, t_ref, d_ref, loss_ref):
        mv = m_ref[...]
        out = x_ref[...] + gt_ref[...] * ((mv * _rstd(mv)) * g_ref[...])
        err = out - t_ref[...]
        d_ref[...] = err * (1.0 / D)
        part = 0.5 * jnp.sum(jnp.mean(err * err, axis=-1, keepdims=True), axis=0, keepdims=True)
        part = jnp.broadcast_to(part, loss_ref.shape)

        @pl.when(pl.program_id(0) == 0)
        def _():
            loss_ref[...] = part

        @pl.when(pl.program_id(0) != 0)
        def _():
            loss_ref[...] += part

    row = pl.BlockSpec((tr, D), lambda i: (i, 0))
    vec = pl.BlockSpec((1, D), lambda i: (0, 0))
    return pl.pallas_call(
        body, name=name,
        out_shape=(jax.ShapeDtypeStruct((S, D), F32), jax.ShapeDtypeStruct((8, LANES), F32)), grid=(S // tr,),
        in_specs=[row, row, vec, vec, row], out_specs=(row, pl.BlockSpec((8, LANES), lambda i: (0, 0))),
        compiler_params=_params("arbitrary"),
    )(xres, m, g, gt, target)


def _resnorm_bwd(dout, m, g, gt, *, name):
    S, D = m.shape
    tr = _row_tile(S, D)

    def body(d_ref, m_ref, g_ref, gt_ref, dm_ref, dg_ref, dgt_ref):
        first = pl.program_id(0) == 0
        mv = m_ref[...]
        dv = d_ref[...]
        gv = g_ref[...]
        r = _rstd(mv)
        mhat = mv * r
        _acc_rows(dgt_ref, dv * (mhat * gv), first)
        dn = dv * gt_ref[...]
        _acc_rows(dg_ref, dn * mhat, first)
        dmhat = dn * gv
        proj = jnp.mean(dmhat * mhat, axis=-1, keepdims=True)
        dm_ref[...] = (r * (dmhat - mhat * proj)).astype(BF16)

    row = pl.BlockSpec((tr, D), lambda i: (i, 0))
    vec = pl.BlockSpec((1, D), lambda i: (0, 0))
    vshape = jax.ShapeDtypeStruct((1, D), F32)
    return pl.pallas_call(
        body, name=name, out_shape=(jax.ShapeDtypeStruct((S, D), BF16), vshape, vshape), grid=(S // tr,),
        in_specs=[row, row, vec, vec], out_specs=(row, vec, vec), compiler_params=_params("arbitrary"),
    )(dout, m, g, gt)


def _lat_norm_fwd(z_lat, g_q, g_kv, *, name):
    S, W = z_lat.shape
    Rq, Rkv = g_q.shape[1], g_kv.shape[1]
    tr = _row_tile(S, W)

    def body(z_ref, gq_ref, gkv_ref, nq_ref, nkv_ref):
        cq = z_ref[:, :Rq]
        ckv = z_ref[:, Rq:Rq + Rkv]
        nq_ref[...] = ((cq * _rstd(cq)) * gq_ref[...]).astype(BF16)
        nkv_ref[...] = ((ckv * _rstd(ckv)) * gkv_ref[...]).astype(BF16)

    return pl.pallas_call(
        body, name=name,
        out_shape=(jax.ShapeDtypeStruct((S, Rq), BF16), jax.ShapeDtypeStruct((S, Rkv), BF16)), grid=(S // tr,),
        in_specs=[pl.BlockSpec((tr, W), lambda i: (i, 0)), pl.BlockSpec((1, Rq), lambda i: (0, 0)),
                  pl.BlockSpec((1, Rkv), lambda i: (0, 0))],
        out_specs=(pl.BlockSpec((tr, Rq), lambda i: (i, 0)), pl.BlockSpec((tr, Rkv), lambda i: (i, 0))),
        compiler_params=_params("parallel"),
    )(z_lat, g_q, g_kv)


def _lat_norm_bwd(z_lat, dnq, dnkv, dkr, g_q, g_kv, *, name):
    S, W = z_lat.shape
    Rq, Rkv = g_q.shape[1], g_kv.shape[1]
    tr = _row_tile(S, W)

    def one(c, dn, gv):
        r = _rstd(c)
        chat = c * r
        dchat = dn * gv
        proj = jnp.mean(dchat * chat, axis=-1, keepdims=True)
        return r * (dchat - chat * proj), dn * chat

    def body(z_ref, dnq_ref, dnkv_ref, dkr_ref, gq_ref, gkv_ref, dz_ref, dgq_ref, dgkv_ref):
        first = pl.program_id(0) == 0
        dcq, pq = one(z_ref[:, :Rq], dnq_ref[...], gq_ref[...])
        dckv, pkv = one(z_ref[:, Rq:Rq + Rkv], dnkv_ref[...], gkv_ref[...])
        _acc_rows(dgq_ref, pq, first)
        _acc_rows(dgkv_ref, pkv, first)
        dz_ref[:, :Rq] = dcq.astype(BF16)
        dz_ref[:, Rq:Rq + Rkv] = dckv.astype(BF16)
        dz_ref[:, Rq + Rkv:] = dkr_ref[...].astype(BF16)

    return pl.pallas_call(
        body, name=name,
        out_shape=(jax.ShapeDtypeStruct((S, W), BF16), jax.ShapeDtypeStruct((1, Rq), F32),
                   jax.ShapeDtypeStruct((1, Rkv), F32)), grid=(S // tr,),
        in_specs=[pl.BlockSpec((tr, W), lambda i: (i, 0)), pl.BlockSpec((tr, Rq), lambda i: (i, 0)),
                  pl.BlockSpec((tr, Rkv), lambda i: (i, 0)), pl.BlockSpec((tr, LANES), lambda i: (i, 0)),
                  pl.BlockSpec((1, Rq), lambda i: (0, 0)), pl.BlockSpec((1, Rkv), lambda i: (0, 0))],
        out_specs=(pl.BlockSpec((tr, W), lambda i: (i, 0)), pl.BlockSpec((1, Rq), lambda i: (0, 0)),
                   pl.BlockSpec((1, Rkv), lambda i: (0, 0))),
        compiler_params=_params("arbitrary"),
    )(z_lat, dnq, dnkv, dkr, g_q, g_kv)


def _rot(x, lo32):
    a = pltpu.roll(x, 32, 1)
    b = pltpu.roll(x, LANES - 32, 1)
    return jnp.where(lo32, -b, a)


def _rot_t(g, lo32):
    a = pltpu.roll(g, 32, 1)
    b = pltpu.roll(g, LANES - 32, 1)
    return jnp.where(lo32, b, -a)


def _mla_pack_fwd(q_raw, kv_raw, z_lat, cos, sin, kr_off, *, name):
    S = q_raw.shape[0]
    H = kv_raw.shape[1] // (MLA_NOPE + MLA_V)
    W = z_lat.shape[1]
    scale = MLA_QK ** -0.5
    tr = min(S, 128)
    nope_w = H * MLA_NOPE

    def body(q_ref, kv_ref, z_ref, cos_ref, sin_ref, qp_ref, kp_ref, v_ref):
        lane = lax.broadcasted_iota(jnp.int32, (tr, LANES), 1)
        lo32 = (lane % 64) < 32
        lo64 = lane < 64
        c = cos_ref[...]
        s = sin_ref[...]
        kr = z_ref[:, kr_off:kr_off + LANES]
        kr = (kr * c + _rot(kr, lo32) * s).astype(BF16)
        for hp in range(H // 2):
            xb = q_ref[:, nope_w + hp * LANES:nope_w + (hp + 1) * LANES]
            rb = (xb * c + _rot(xb, lo32) * s) * scale
            for e in range(2):
                h = 2 * hp + e
                base = h * MLA_QK_PAD
                qp_ref[:, base:base + LANES] = (q_ref[:, h * LANES:(h + 1) * LANES] * scale).astype(BF16)
                keep = lo64 if e == 0 else jnp.logical_not(lo64)
                qp_ref[:, base + LANES:base + 2 * LANES] = jnp.where(keep, rb, 0.0).astype(BF16)
                kp_ref[:, base:base + LANES] = kv_ref[:, h * LANES:(h + 1) * LANES].astype(BF16)
                kp_ref[:, base + LANES:base + 2 * LANES] = kr
        v_ref[...] = kv_ref[:, nope_w:].astype(BF16)

    return pl.pallas_call(
        body, name=name,
        out_shape=(jax.ShapeDtypeStruct((S, H * MLA_QK_PAD), BF16), jax.ShapeDtypeStruct((S, H * MLA_QK_PAD), BF16),
                   jax.ShapeDtypeStruct((S, H * MLA_V), BF16)), grid=(S // tr,),
        in_specs=[pl.BlockSpec((tr, q_raw.shape[1]), lambda i: (i, 0)), pl.BlockSpec((tr, kv_raw.shape[1]), lambda i: (i, 0)),
                  pl.BlockSpec((tr, W), lambda i: (i, 0)), pl.BlockSpec((tr, LANES), lambda i: (i, 0)),
                  pl.BlockSpec((tr, LANES), lambda i: (i, 0))],
        out_specs=(pl.BlockSpec((tr, H * MLA_QK_PAD), lambda i: (i, 0)), pl.BlockSpec((tr, H * MLA_QK_PAD), lambda i: (i, 0)),
                   pl.BlockSpec((tr, H * MLA_V), lambda i: (i, 0))),
        compiler_params=_params("parallel"),
    )(q_raw, kv_raw, z_lat, cos, sin)


def _mla_pack_bwd(dqp, dkp, dv, cos, sin, *, name):
    S = dqp.shape[0]
    H = dv.shape[1] // MLA_V
    scale = MLA_QK ** -0.5
    tr = min(S, 128)
    nope_w = H * MLA_NOPE

    def body(dqp_ref, dkp_ref, dv_ref, cos_ref, sin_ref, dq_ref, dkv_ref, dkr_ref):
        lane = lax.broadcasted_iota(jnp.int32, (tr, LANES), 1)
        lo32 = (lane % 64) < 32
        lo64 = lane < 64
        c = cos_ref[...]
        s = sin_ref[...]
        dkr2 = jnp.zeros((tr, LANES), F32)
        for hp in range(H // 2):
            be = (2 * hp) * MLA_QK_PAD
            bo = (2 * hp + 1) * MLA_QK_PAD
            g = jnp.where(lo64, dqp_ref[:, be + LANES:be + 2 * LANES], dqp_ref[:, bo + LANES:bo + 2 * LANES]) * scale
            dq_ref[:, nope_w + hp * LANES:nope_w + (hp + 1) * LANES] = (g * c + _rot_t(g * s, lo32)).astype(BF16)
            for h, base in ((2 * hp, be), (2 * hp + 1, bo)):
                dq_ref[:, h * LANES:(h + 1) * LANES] = (dqp_ref[:, base:base + LANES] * scale).astype(BF16)
                dkv_ref[:, h * LANES:(h + 1) * LANES] = dkp_ref[:, base:base + LANES].astype(BF16)
                dkr2 = dkr2 + dkp_ref[:, base + LANES:base + 2 * LANES]
        dkr2 = dkr2 * c + _rot_t(dkr2 * s, lo32)
        dkr2 = dkr2 + pltpu.roll(dkr2, 64, 1)
        dkr_ref[...] = jnp.where(lo64, dkr2, 0.0)
        dkv_ref[:, nope_w:] = dv_ref[...].astype(BF16)

    return pl.pallas_call(
        body, name=name,
        out_shape=(jax.ShapeDtypeStruct((S, nope_w + H * MLA_ROPE), BF16), jax.ShapeDtypeStruct((S, 2 * nope_w), BF16),
                   jax.ShapeDtypeStruct((S, LANES), F32)), grid=(S // tr,),
        in_specs=[pl.BlockSpec((tr, H * MLA_QK_PAD), lambda i: (i, 0)), pl.BlockSpec((tr, H * MLA_QK_PAD), lambda i: (i, 0)),
                  pl.BlockSpec((tr, H * MLA_V), lambda i: (i, 0)), pl.BlockSpec((tr, LANES), lambda i: (i, 0)),
                  pl.BlockSpec((tr, LANES), lambda i: (i, 0))],
        out_specs=(pl.BlockSpec((tr, nope_w + H * MLA_ROPE), lambda i: (i, 0)), pl.BlockSpec((tr, 2 * nope_w), lambda i: (i, 0)),
                   pl.BlockSpec((tr, LANES), lambda i: (i, 0))),
        compiler_params=_params("parallel"),
    )(dqp, dkp, dv, cos, sin)


FLASH_HB = 2


def _causal_pairs(nb):
    qi = [i for i in range(nb) for j in range(i + 1)]
    kj = [j for i in range(nb) for j in range(i + 1)]
    return jnp.asarray(qi, jnp.int32), jnp.asarray(kj, jnp.int32)


def _scores(q, k, diagonal, t):
    s = lax.dot_general(q, k, (((1,), (1,)), ((), ())), preferred_element_type=F32)
    if diagonal:
        row = lax.broadcasted_iota(jnp.int32, (t, t), 0)
        col = lax.broadcasted_iota(jnp.int32, (t, t), 1)
        s = jnp.where(col <= row, s, NEG)
    return s


def _flash_fwd(qp, kp, v, *, name):
    S = qp.shape[0]
    H = v.shape[1] // MLA_V
    t = min(S, 512)
    nb = S // t
    HB = 2 * FLASH_HB
    qi, kj = _causal_pairs(nb)
    QW, VW = MLA_QK_PAD, MLA_V

    def body(qi_ref, kj_ref, q_ref, k_ref, v_ref, o_ref, lse_ref, m_s, l_s, acc_s):
        pr = pl.program_id(1)
        i = qi_ref[pr]
        j = kj_ref[pr]

        @pl.when(j == 0)
        def _():
            m_s[...] = jnp.full_like(m_s, NEG)
            l_s[...] = jnp.zeros_like(l_s)
            acc_s[...] = jnp.zeros_like(acc_s)

        def step(diagonal):
            for hh in range(HB):
                s = _scores(q_ref[:, hh * QW:(hh + 1) * QW], k_ref[:, hh * QW:(hh + 1) * QW], diagonal, t)
                m_prev = m_s[hh]
                m_cur = jnp.maximum(m_prev, jnp.max(s, axis=1, keepdims=True))
                alpha = jnp.exp(m_prev - m_cur)
                p = jnp.exp(s - m_cur[:, :1])
                l_new = alpha * l_s[hh] + jnp.sum(p, axis=1, keepdims=True)
                acc = alpha * acc_s[hh] + jnp.dot(p.astype(BF16), v_ref[:, hh * VW:(hh + 1) * VW], preferred_element_type=F32)
                if diagonal:
                    o_ref[:, hh * VW:(hh + 1) * VW] = acc / l_new
                    lse_ref[hh] = m_cur + jnp.log(l_new)
                else:
                    l_s[hh] = l_new
                    acc_s[hh] = acc
                    m_s[hh] = m_cur

        @pl.when(i != j)
        def _():
            step(False)

        @pl.when(i == j)
        def _():
            step(True)

    return pl.pallas_call(
        body, name=name,
        out_shape=(jax.ShapeDtypeStruct((S, H * VW), F32), jax.ShapeDtypeStruct((H, S, LANES), F32)),
        grid_spec=pltpu.PrefetchScalarGridSpec(
            num_scalar_prefetch=2, grid=(H // HB, qi.shape[0]),
            in_specs=[pl.BlockSpec((t, HB * QW), lambda g, p, qi, kj: (qi[p], g)),
                      pl.BlockSpec((t, HB * QW), lambda g, p, qi, kj: (kj[p], g)),
                      pl.BlockSpec((t, HB * VW), lambda g, p, qi, kj: (kj[p], g))],
            out_specs=(pl.BlockSpec((t, HB * VW), lambda g, p, qi, kj: (qi[p], g)),
                       pl.BlockSpec((HB, t, LANES), lambda g, p, qi, kj: (g, qi[p], 0))),
            scratch_shapes=[pltpu.VMEM((HB, t, LANES), F32), pltpu.VMEM((HB, t, LANES), F32), pltpu.VMEM((HB, t, VW), F32)]),
        compiler_params=_params("parallel", "arbitrary"),
    )(qi, kj, qp, kp, v)


def _flash_bwd(qp, kp, v, o, do, lse, *, name):
    S = qp.shape[0]
    H = v.shape[1] // MLA_V
    t = min(S, 512)
    nb = S // t
    HB = FLASH_HB
    qi, kj = _causal_pairs(nb)
    QW, VW = MLA_QK_PAD, MLA_V
    tn = (((0,), (0,)), ((), ()))
    nt = (((1,), (1,)), ((), ()))

    def body(qi_ref, kj_ref, q_ref, k_ref, v_ref, o_ref, do_ref, lse_ref, dq_ref, dk_ref, dv_ref, dq_s):
        pr = pl.program_id(1)
        i = qi_ref[pr]
        j = kj_ref[pr]
        rows = pl.ds(pl.multiple_of(j * t, t), t)

        @pl.when(pr == 0)
        def _():
            dk_ref[...] = jnp.zeros_like(dk_ref)
            dv_ref[...] = jnp.zeros_like(dv_ref)

        @pl.when(j == 0)
        def _():
            dq_s[...] = jnp.zeros_like(dq_s)

        def step(diagonal):
            for hh in range(HB):
                q = q_ref[:, hh * QW:(hh + 1) * QW]
                k = k_ref[:, hh * QW:(hh + 1) * QW]
                dob = do_ref[:, hh * VW:(hh + 1) * VW]
                p = jnp.exp(_scores(q, k, diagonal, t) - lse_ref[hh][:, :1])
                delta = jnp.sum(dob.astype(F32) * o_ref[:, hh * VW:(hh + 1) * VW], axis=1, keepdims=True)
                dp = lax.dot_general(dob, v_ref[:, hh * VW:(hh + 1) * VW], nt, preferred_element_type=F32)
                dsb = (p * (dp - delta)).astype(BF16)
                dv_ref[rows, hh * VW:(hh + 1) * VW] += lax.dot_general(p.astype(BF16), dob, tn, preferred_element_type=F32)
                dk_ref[rows, hh * QW:(hh + 1) * QW] += lax.dot_general(dsb, q, tn, preferred_element_type=F32)
                dq = dq_s[:, hh * QW:(hh + 1) * QW] + jnp.dot(dsb, k, preferred_element_type=F32)
                if diagonal:
                    dq_ref[:, hh * QW:(hh + 1) * QW] = dq
                else:
                    dq_s[:, hh * QW:(hh + 1) * QW] = dq

        @pl.when(i != j)
        def _():
            step(False)

        @pl.when(i == j)
        def _():
            step(True)

    qside = lambda g, p, qi, kj: (qi[p], g)
    kside = lambda g, p, qi, kj: (kj[p], g)
    whole = lambda g, p, qi, kj: (0, g)
    return pl.pallas_call(
        body, name=name,
        out_shape=(jax.ShapeDtypeStruct((S, H * QW), F32), jax.ShapeDtypeStruct((S, H * QW), F32),
                   jax.ShapeDtypeStruct((S, H * VW), F32)),
        grid_spec=pltpu.PrefetchScalarGridSpec(
            num_scalar_prefetch=2, grid=(H // HB, qi.shape[0]),
            in_specs=[pl.BlockSpec((t, HB * QW), qside), pl.BlockSpec((t, HB * QW), kside), pl.BlockSpec((t, HB * VW), kside),
                      pl.BlockSpec((t, HB * VW), qside), pl.BlockSpec((t, HB * VW), qside),
                      pl.BlockSpec((HB, t, LANES), lambda g, p, qi, kj: (g, qi[p], 0))],
            out_specs=(pl.BlockSpec((t, HB * QW), qside), pl.BlockSpec((S, HB * QW), whole), pl.BlockSpec((S, HB * VW), whole)),
            scratch_shapes=[pltpu.VMEM((t, HB * QW), F32)]),
        compiler_params=_params("parallel", "arbitrary"),
    )(qi, kj, qp, kp, v, o, do, lse)


def _swa_kv_halves(blk, hf, lo):
    if hf == 0:
        a = jnp.where(lo, blk, 0.0)
        b = pltpu.roll(a, 64, 1)
    else:
        b = jnp.where(lo, 0.0, blk)
        a = pltpu.roll(b, 64, 1)
    return a.astype(BF16), b.astype(BF16)


def _swa_softmax(qs, kx, bias, neg0, sk):
    s = lax.dot_general(qs, kx, (((1,), (1,)), ((), ())), preferred_element_type=F32) + bias + neg0
    m = jnp.maximum(jnp.max(s, axis=1, keepdims=True), sk)
    e = jnp.exp(s - m)
    es = jnp.exp(sk - m)
    inv = 1.0 / (jnp.sum(e, axis=1, keepdims=True) + es)
    return e * inv, es * inv


def _swa_stack(ref, kvh, npb, scale=None):
    parts = [ref[:, (kvh * npb + pb) * LANES:(kvh * npb + pb + 1) * LANES] for pb in range(npb)]
    x = jnp.concatenate(parts, axis=0)
    return x if scale is None else x * scale


def _swa_sink_col(sink_ref, kvh, e, npb):
    row = lax.broadcasted_iota(jnp.int32, (npb * SWA_BLOCK, 1), 0)
    col = jnp.zeros((npb * SWA_BLOCK, 1), F32) + sink_ref[2 * (kvh * npb) + e]
    for pb in range(1, npb):
        col = jnp.where(row >= pb * SWA_BLOCK, sink_ref[2 * (kvh * npb + pb) + e], col)
    return col


def _swa_fwd(z_swa, bias_st, sinks, *, name):
    S, W = z_swa.shape
    npb = bias_st.shape[1] // SWA_BLOCK
    NH = 2 * SWA_KVH * npb
    QW = NH * SWA_HD
    KW = SWA_KVH * SWA_HD
    nb = S // SWA_BLOCK
    B = SWA_BLOCK
    assert SWA_KVH % 2 == 0 and W == QW + 2 * KW

    def body(sink_ref, q_ref, kvc_ref, kvp_ref, b_ref, o_ref):
        n = pl.program_id(0)
        lo = lax.broadcasted_iota(jnp.int32, (2 * B, LANES), 1) < 64
        col = lax.broadcasted_iota(jnp.int32, (npb * B, 2 * B), 1)
        neg0 = jnp.where(jnp.logical_and(col < B, n == 0), NEG, 0.0)
        for kb in range(SWA_KVH // 2):
            kblk = jnp.concatenate([kvp_ref[:, kb * LANES:(kb + 1) * LANES], kvc_ref[:, kb * LANES:(kb + 1) * LANES]], axis=0)
            vblk = jnp.concatenate([kvp_ref[:, KW + kb * LANES:KW + (kb + 1) * LANES],
                                    kvc_ref[:, KW + kb * LANES:KW + (kb + 1) * LANES]], axis=0)
            for hf in range(2):
                kvh = 2 * kb + hf
                ks = _swa_kv_halves(kblk, hf, lo)
                vs = _swa_kv_halves(vblk, hf, lo)
                qs = _swa_stack(q_ref, kvh, npb, SWA_HD ** -0.5).astype(BF16)
                acc = jnp.zeros((npb * B, LANES), F32)
                for e in range(2):
                    p, _ = _swa_softmax(qs, ks[e], b_ref[2 * kvh + e], neg0, _swa_sink_col(sink_ref, kvh, e, npb))
                    acc = acc + jnp.dot(p.astype(BF16), vs[e], preferred_element_type=F32)
                for pb in range(npb):
                    P = kvh * npb + pb
                    o_ref[:, P * LANES:(P + 1) * LANES] = acc[pb * B:(pb + 1) * B]

    kvcol = QW // (2 * KW)
    assert QW % (2 * KW) == 0
    return pl.pallas_call(
        body, name=name,
        out_shape=jax.ShapeDtypeStruct((S, QW), F32), grid=(nb,),
        in_specs=[SMEM_FULL, pl.BlockSpec((B, QW), lambda n: (n, 0)), pl.BlockSpec((B, 2 * KW), lambda n: (n, kvcol)),
                  pl.BlockSpec((B, 2 * KW), lambda n: (jnp.maximum(n - 1, 0), kvcol)),
                  pl.BlockSpec(bias_st.shape, lambda n: (0, 0, 0))],
        out_specs=pl.BlockSpec((B, QW), lambda n: (n, 0)),
        compiler_params=_params("parallel"),
    )(sinks, z_swa, z_swa, z_swa, bias_st)


def _swa_bwd(z_swa, bias_st, sinks, o, do, *, name):
    S, W = z_swa.shape
    npb = bias_st.shape[1] // SWA_BLOCK
    NH = 2 * SWA_KVH * npb
    QW = NH * SWA_HD
    KW = SWA_KVH * SWA_HD
    nb = S // SWA_BLOCK
    B = SWA_BLOCK
    scale = SWA_HD ** -0.5
    tn = (((0,), (0,)), ((), ()))
    nt = (((1,), (1,)), ((), ()))

    def fold(x, hf, lo):
        x = x + pltpu.roll(x, 64, 1)
        return jnp.where(lo, x, 0.0) if hf == 0 else jnp.where(lo, 0.0, x)

    def body(sink_ref, q_ref, kvc_ref, kvp_ref, b_ref, o_ref, do_ref, dz_ref, dbias_ref, dsink_ref,
             cq_s, ck_s, cv_s, nq_s, nk_s, nv_s, pk_s, pv_s):
        n = pl.program_id(0)

        @pl.when(n == 0)
        def _():
            dbias_ref[...] = jnp.zeros_like(dbias_ref)
            dsink_ref[...] = jnp.zeros_like(dsink_ref)
            cq_s[...] = jnp.zeros_like(cq_s)
            ck_s[...] = jnp.zeros_like(ck_s)
            cv_s[...] = jnp.zeros_like(cv_s)

        @pl.when(n == nb)
        def _():
            pk_s[...] = jnp.zeros_like(pk_s)
            pv_s[...] = jnp.zeros_like(pv_s)

        @pl.when(n < nb)
        def _():
            lo = lax.broadcasted_iota(jnp.int32, (2 * B, LANES), 1) < 64
            lo1 = lax.broadcasted_iota(jnp.int32, (npb * B, LANES), 1) < 64
            lane1 = lax.broadcasted_iota(jnp.int32, (1, LANES), 1)
            col = lax.broadcasted_iota(jnp.int32, (npb * B, 2 * B), 1)
            neg0 = jnp.where(jnp.logical_and(col < B, n == 0), NEG, 0.0)
            dsink = jnp.zeros((1, LANES), F32)
            for kb in range(SWA_KVH // 2):
                kblk = jnp.concatenate([kvp_ref[:, kb * LANES:(kb + 1) * LANES], kvc_ref[:, kb * LANES:(kb + 1) * LANES]], axis=0)
                vblk = jnp.concatenate([kvp_ref[:, KW + kb * LANES:KW + (kb + 1) * LANES],
                                        kvc_ref[:, KW + kb * LANES:KW + (kb + 1) * LANES]], axis=0)
                dkblk = jnp.zeros((2 * B, LANES), F32)
                dvblk = jnp.zeros((2 * B, LANES), F32)
                for hf in range(2):
                    kvh = 2 * kb + hf
                    ks = _swa_kv_halves(kblk, hf, lo)
                    vs = _swa_kv_halves(vblk, hf, lo)
                    qs = _swa_stack(q_ref, kvh, npb, scale).astype(BF16)
                    dos = _swa_stack(do_ref, kvh, npb)
                    prod = dos * _swa_stack(o_ref, kvh, npb)
                    dob = dos.astype(BF16)
                    dkj = jnp.zeros((2 * B, LANES), F32)
                    dvj = jnp.zeros((2 * B, LANES), F32)
                    dqs = jnp.zeros((npb * B, LANES), F32)
                    for e in range(2):
                        keep = lo1 if e == 0 else jnp.logical_not(lo1)
                        p, psink = _swa_softmax(qs, ks[e], b_ref[2 * kvh + e], neg0, _swa_sink_col(sink_ref, kvh, e, npb))
                        delta = jnp.sum(jnp.where(keep, prod, 0.0), axis=1, keepdims=True)
                        dp = lax.dot_general(dob, vs[e], nt, preferred_element_type=F32)
                        ds = p * (dp - delta)
                        dbias_ref[2 * kvh + e] += ds
                        pd = psink * delta
                        for pb in range(npb):
                            dsh = -jnp.sum(pd[pb * B:(pb + 1) * B], axis=0, keepdims=True)
                            dsink = dsink + jnp.where(lane1 == 2 * (kvh * npb + pb) + e, dsh, 0.0)
                        dsb = ds.astype(BF16)
                        dqs = dqs + jnp.dot(dsb, ks[e], preferred_element_type=F32)
                        keep2 = lo if e == 0 else jnp.logical_not(lo)
                        dkj = dkj + jnp.where(keep2, lax.dot_general(dsb, qs, tn, preferred_element_type=F32), 0.0)
                        dvj = dvj + jnp.where(keep2, lax.dot_general(p.astype(BF16), dob, tn, preferred_element_type=F32), 0.0)
                    for pb in range(npb):
                        P = kvh * npb + pb
                        nq_s[:, P * LANES:(P + 1) * LANES] = dqs[pb * B:(pb + 1) * B] * scale
                    dkblk = dkblk + fold(dkj, hf, lo)
                    dvblk = dvblk + fold(dvj, hf, lo)
                pk_s[:, kb * LANES:(kb + 1) * LANES] = dkblk[:B]
                nk_s[:, kb * LANES:(kb + 1) * LANES] = dkblk[B:]
                pv_s[:, kb * LANES:(kb + 1) * LANES] = dvblk[:B]
                nv_s[:, kb * LANES:(kb + 1) * LANES] = dvblk[B:]
            dsink_ref[...] += dsink

        dz_ref[:, :QW] = cq_s[...].astype(BF16)
        dz_ref[:, QW:QW + KW] = (ck_s[...] + pk_s[...]).astype(BF16)
        dz_ref[:, QW + KW:] = (cv_s[...] + pv_s[...]).astype(BF16)

        @pl.when(n < nb)
        def _():
            cq_s[...] = nq_s[...]
            ck_s[...] = nk_s[...]
            cv_s[...] = nv_s[...]

    kvcol = QW // (2 * KW)
    cur = lambda n: (jnp.minimum(n, nb - 1), 0)
    return pl.pallas_call(
        body, name=name,
        out_shape=(jax.ShapeDtypeStruct((S, W), BF16), jax.ShapeDtypeStruct(bias_st.shape, F32),
                   jax.ShapeDtypeStruct((1, LANES), F32)),
        grid=(nb + 1,),
        in_specs=[SMEM_FULL, pl.BlockSpec((B, QW), cur), pl.BlockSpec((B, 2 * KW), lambda n: (jnp.minimum(n, nb - 1), kvcol)),
                  pl.BlockSpec((B, 2 * KW), lambda n: (jnp.maximum(jnp.minimum(n, nb - 1) - 1, 0), kvcol)),
                  pl.BlockSpec(bias_st.shape, lambda n: (0, 0, 0)), pl.BlockSpec((B, QW), cur), pl.BlockSpec((B, QW), cur)],
        out_specs=(pl.BlockSpec((B, W), lambda n: (jnp.maximum(n - 1, 0), 0)),
                   pl.BlockSpec(bias_st.shape, lambda n: (0, 0, 0)), pl.BlockSpec((1, LANES), lambda n: (0, 0))),
        scratch_shapes=[pltpu.VMEM((B, QW), F32), pltpu.VMEM((B, KW), F32), pltpu.VMEM((B, KW), F32),
                        pltpu.VMEM((B, QW), F32), pltpu.VMEM((B, KW), F32), pltpu.VMEM((B, KW), F32),
                        pltpu.VMEM((B, KW), F32), pltpu.VMEM((B, KW), F32)],
        compiler_params=_params("arbitrary"),
    )(sinks, z_swa, z_swa, z_swa, bias_st, o, do)


def _gate_fwd(zg, o_a, o_b, *, name):
    S, D = o_a.shape
    tr = min(S, 512)

    def body(z_ref, a_ref, b_ref, m_ref):
        ga = jax.nn.sigmoid(z_ref[:, :PAIR])
        gb = jax.nn.sigmoid(z_ref[:, PAIR:])
        m_ref[...] = (ga * a_ref[...] + gb * b_ref[...]).astype(BF16)

    col = pl.BlockSpec((tr, PAIR), lambda i, j: (i, j))
    return pl.pallas_call(
        body, name=name, out_shape=jax.ShapeDtypeStruct((S, D), BF16), grid=(S // tr, D // PAIR),
        in_specs=[pl.BlockSpec((tr, 2 * PAIR), lambda i, j: (i, j)), col, col], out_specs=col,
        compiler_params=_params("parallel", "parallel"),
    )(zg, o_a, o_b)


def _gate_bwd(dmix, zg, o_a, o_b, *, name):
    S, D = o_a.shape
    tr = min(S, 512)

    def body(d_ref, z_ref, a_ref, b_ref, da_ref, db_ref, dz_ref):
        d = d_ref[...]
        ga = jax.nn.sigmoid(z_ref[:, :PAIR])
        gb = jax.nn.sigmoid(z_ref[:, PAIR:])
        da_ref[...] = (d * ga).astype(BF16)
        db_ref[...] = d * gb
        dz_ref[:, :PAIR] = (d * a_ref[...] * (ga * (1.0 - ga))).astype(BF16)
        dz_ref[:, PAIR:] = (d * b_ref[...] * (gb * (1.0 - gb))).astype(BF16)

    col = pl.BlockSpec((tr, PAIR), lambda i, j: (i, j))
    wide = pl.BlockSpec((tr, 2 * PAIR), lambda i, j: (i, j))
    return pl.pallas_call(
        body, name=name,
        out_shape=(jax.ShapeDtypeStruct((S, D), BF16), jax.ShapeDtypeStruct((S, D), F32), jax.ShapeDtypeStruct((S, 2 * D), BF16)),
        grid=(S // tr, D // PAIR), in_specs=[col, wide, col, col], out_specs=(col, col, wide),
        compiler_params=_params("parallel", "parallel"),
    )(dmix, zg, o_a, o_b)


def _conv_taps(t_ref, prev_ref, i, tr):
    cur = t_ref[...].astype(F32)
    live = (i > 0).astype(F32)
    p6 = prev_ref[14:15, :].astype(F32) * live
    p7 = prev_ref[15:16, :].astype(F32) * live
    row = lax.broadcasted_iota(jnp.int32, cur.shape, 0)
    t1 = jnp.where(row == 0, p7, pltpu.roll(cur, 1, 0))
    t2 = jnp.where(row == 0, p6, jnp.where(row == 1, p7, pltpu.roll(cur, 2, 0)))
    return cur, t1, t2


def _conv_u(t_ref, prev_ref, w_ref, b_ref, i, tr):
    cur, t1, t2 = _conv_taps(t_ref, prev_ref, i, tr)
    u = ((b_ref[...] + w_ref[0:1, :] * t2) + w_ref[1:2, :] * t1) + w_ref[2:3, :] * cur
    return u, cur, t1, t2


def _conv_specs(tr, S):
    blk = pl.BlockSpec((tr, 2 * PAIR), lambda j, i: (i, j))
    prev = pl.BlockSpec((16, 2 * PAIR), lambda j, i: (jnp.maximum(i * (tr // 16) - 1, 0), j))
    w3 = pl.BlockSpec((3, 2 * PAIR), lambda j, i: (0, j))
    w1 = pl.BlockSpec((1, 2 * PAIR), lambda j, i: (0, j))
    return blk, prev, w3, w1


def _conv_gate_fwd(t, cw, cb, *, name):
    S, F2 = t.shape
    tr = min(S, 512)
    blk, prev, w3, w1 = _conv_specs(tr, S)

    def body(t_ref, prev_ref, w_ref, b_ref, a_ref):
        u, _, _, _ = _conv_u(t_ref, prev_ref, w_ref, b_ref, pl.program_id(1), tr)
        a_ref[...] = (jax.nn.silu(u[:, :PAIR]) * u[:, PAIR:]).astype(BF16)

    return pl.pallas_call(
        body, name=name, out_shape=jax.ShapeDtypeStruct((S, F2 // 2), BF16), grid=(F2 // (2 * PAIR), S // tr),
        in_specs=[blk, prev, w3, w1], out_specs=pl.BlockSpec((tr, PAIR), lambda j, i: (i, j)),
        compiler_params=_params("parallel", "parallel"),
    )(t, t, cw, cb)


def _conv_gate_bwd(t, da, cw, cb, *, name):
    S, F2 = t.shape
    tr = min(S, 256)
    blk, prev, w3, w1 = _conv_specs(tr, S)

    def body(t_ref, prev_ref, da_ref, w_ref, b_ref, du_ref, dw_ref, db_ref):
        i = pl.program_id(1)
        u, cur, t1, t2 = _conv_u(t_ref, prev_ref, w_ref, b_ref, i, tr)
        u1 = u[:, :PAIR]
        u2 = u[:, PAIR:]
        d = da_ref[...].astype(F32)
        sg = jax.nn.sigmoid(u1)
        du1 = d * u2 * (sg * (1.0 + u1 * (1.0 - sg)))
        du2 = d * (u1 * sg)
        du = jnp.concatenate([du1, du2], axis=1)
        du_ref[...] = du.astype(BF16)
        first = i == 0
        _acc_rows(db_ref, du, first)
        dw = jnp.concatenate([jnp.sum(du * t2, axis=0, keepdims=True), jnp.sum(du * t1, axis=0, keepdims=True),
                              jnp.sum(du * cur, axis=0, keepdims=True)], axis=0)

        @pl.when(first)
        def _():
            dw_ref[...] = dw

        @pl.when(jnp.logical_not(first))
        def _():
            dw_ref[...] += dw

    return pl.pallas_call(
        body, name=name,
        out_shape=(jax.ShapeDtypeStruct((S, F2), BF16), jax.ShapeDtypeStruct((3, F2), F32), jax.ShapeDtypeStruct((1, F2), F32)),
        grid=(F2 // (2 * PAIR), S // tr),
        in_specs=[blk, prev, pl.BlockSpec((tr, PAIR), lambda j, i: (i, j)), w3, w1], out_specs=(blk, w3, w1),
        compiler_params=_params("parallel", "arbitrary"),
    )(t, t, da, cw, cb)


def _conv_bwd_dt(du, cw, *, name):
    S, F2 = du.shape
    tr = min(S, 512)
    nrow = S // tr
    blk, _, w3, _ = _conv_specs(tr, S)
    nxt = pl.BlockSpec((16, 2 * PAIR), lambda j, i: (jnp.minimum((i + 1) * (tr // 16), S // 16 - 1), j))

    def body(d_ref, next_ref, w_ref, dt_ref):
        i = pl.program_id(1)
        cur = d_ref[...].astype(F32)
        live = (i < nrow - 1).astype(F32)
        n0 = next_ref[0:1, :].astype(F32) * live
        n1 = next_ref[1:2, :].astype(F32) * live
        row = lax.broadcasted_iota(jnp.int32, cur.shape, 0)
        d1 = jnp.where(row == tr - 1, n0, pltpu.roll(cur, tr - 1, 0))
        d2 = jnp.where(row == tr - 1, n1, jnp.where(row == tr - 2, n0, pltpu.roll(cur, tr - 2, 0)))
        dt_ref[...] = ((w_ref[2:3, :] * cur + w_ref[1:2, :] * d1) + w_ref[0:1, :] * d2).astype(BF16)

    return pl.pallas_call(
        body, name=name, out_shape=jax.ShapeDtypeStruct((S, F2), BF16), grid=(F2 // (2 * PAIR), nrow),
        in_specs=[blk, nxt, w3], out_specs=blk, compiler_params=_params("parallel", "parallel"),
    )(du, du, cw)


def _ada_fwd(c_all, w, b, *, name):
    Bn, D = c_all.shape
    N = w.shape[1]
    tn = _tile(N, 512)

    def body(c_ref, w_ref, b_ref, o_ref):
        o_ref[...] = jnp.dot(jax.nn.silu(c_ref[...]), w_ref[...], preferred_element_type=F32, precision=HIGHEST) + b_ref[...]

    return pl.pallas_call(
        body, name=name, out_shape=jax.ShapeDtypeStruct((Bn, N), F32), grid=(N // tn,),
        in_specs=[pl.BlockSpec((Bn, D), lambda j: (0, 0)), pl.BlockSpec((D, tn), lambda j: (0, j)),
                  pl.BlockSpec((1, tn), lambda j: (0, j))],
        out_specs=pl.BlockSpec((Bn, tn), lambda j: (0, j)), compiler_params=_params("parallel"),
    )(c_all, w, b)


def _ada_bwd(c_all_t, dmod, *, name):
    D, Bn = c_all_t.shape
    N = dmod.shape[1]
    tm = _tile(D, 512, 8)
    tn = _tile(N, 1536)

    def body(c_ref, d_ref, o_ref):
        o_ref[...] = jnp.dot(jax.nn.silu(c_ref[...]), d_ref[...], preferred_element_type=F32, precision=HIGHEST)

    return pl.pallas_call(
        body, name=name, out_shape=jax.ShapeDtypeStruct((D, N), F32), grid=(D // tm, N // tn),
        in_specs=[pl.BlockSpec((tm, Bn), lambda i, j: (i, 0)), pl.BlockSpec((Bn, tn), lambda i, j: (0, j))],
        out_specs=pl.BlockSpec((tm, tn), lambda i, j: (i, j)), compiler_params=_params("parallel", "parallel"),
    )(c_all_t, dmod)


def _adamw(w, g, m, v, *, name):
    R, C = w.shape
    tr = R if R * C <= (1 << 18) else _tile(R, max(8, (1 << 18) // C), 8)

    def body(w_ref, g_ref, m_ref, v_ref, d_ref, nm_ref, nv_ref):
        gv = g_ref[...]
        nm = ADAM_B1 * m_ref[...] + (1.0 - ADAM_B1) * gv
        nv = ADAM_B2 * v_ref[...] + (1.0 - ADAM_B2) * (gv * gv)
        m_hat = nm / (1.0 - ADAM_B1 ** ADAM_STEP)
        v_hat = nv / (1.0 - ADAM_B2 ** ADAM_STEP)
        d_ref[...] = -ADAM_LR * (m_hat / (jnp.sqrt(v_hat) + ADAM_EPS) + ADAM_WD * w_ref[...])
        nm_ref[...] = nm
        nv_ref[...] = nv

    blk = pl.BlockSpec((tr, C), lambda i: (i, 0))
    shp = jax.ShapeDtypeStruct((R, C), F32)
    return pl.pallas_call(
        body, name=name, out_shape=(shp, shp, shp), grid=(R // tr,), in_specs=[blk] * 4, out_specs=(blk,) * 3,
        compiler_params=_params("parallel"),
    )(w, g, m, v)


def _place():
    x, y, c = lax.axis_index("x"), lax.axis_index("y"), lax.axis_index("c")
    return x, y, c, [(1 - x, y), (x, 1 - y), (1 - x, 1 - y)]


def _remote(src, dst, send_sem, recv_sem, dev):
    return pltpu.make_async_remote_copy(src_ref=src, dst_ref=dst, send_sem=send_sem, recv_sem=recv_sem,
                                        device_id=dev, device_id_type=MESH)


def _allgather8(v, *, tie=None, name):
    R, C = v.shape

    def body(v_ref, out_ref, send_sems, recv_sems, local_sem):
        x, y, c, chips = _place()
        me, sibling = (x, y, c), (x, y, 1 - c)

        def rows(px, py, pc):
            return out_ref.at[pl.ds((4 * px + 2 * py + pc) * R, R), :]

        def copy(k, block, to, src=None):
            return _remote(rows(*block) if src is None else src, rows(*block), send_sems.at[k], recv_sems.at[k], to)

        mine = pltpu.make_async_copy(v_ref, rows(*me), local_sem)
        mine.start()
        first = [copy(0, me, sibling, src=v_ref)]
        first += [copy(1 + j, me, (*chip, c), src=v_ref) for j, chip in enumerate(chips)]
        for cp in first:
            cp.start()
        passed = [copy(4 + j, (*chip, c), sibling) for j, chip in enumerate(chips)]
        for j, chip in enumerate(chips):
            copy(1 + j, (*chip, c), me).wait_recv()
            passed[j].start()
        copy(0, sibling, me).wait_recv()
        for j, chip in enumerate(chips):
            copy(4 + j, (*chip, 1 - c), me).wait_recv()
        for cp in first + passed:
            cp.wait_send()
        mine.wait()

    body, tspec, targ = _tied(body, tie)
    out = pl.pallas_call(
        body, name=name, out_shape=jax.ShapeDtypeStruct((N_DEV * R, C), v.dtype),
        in_specs=tspec + [VMEM_FULL], out_specs=VMEM_FULL,
        scratch_shapes=[pltpu.SemaphoreType.DMA((7,)), pltpu.SemaphoreType.DMA((7,)), pltpu.SemaphoreType.DMA],
    )(*targ, v)
    return out.reshape(N_DEV, R, C)


SEM = pl.BlockSpec(memory_space=pltpu.SEMAPHORE)
HBM = pl.BlockSpec(memory_space=pltpu.HBM)
EFFECT = pltpu.SideEffectType.DATAFLOW_SIDE_EFFECTING
DMA_SEM = pltpu.SemaphoreType.DMA(())


def _in_hbm(a):
    return pltpu.with_memory_space_constraint(a, pltpu.HBM)


def _three_halves(land, r2):
    return land.at[pl.ds(0, N_CHIP - 1), pl.ds(0, r2)]


def _gather_start(ws, after, *, name):
    n = len(ws)
    na = len(after)
    lands = [lax.empty((N_CHIP,) + w.shape, w.dtype) for w in ws]

    def body(*refs):
        w_refs, land_refs = refs[:n], refs[n:2 * n]
        send, recv = refs[2 * n + na:3 * n + na], refs[3 * n + na:4 * n + na]
        token = refs[6 * n + na]
        x, y, c, chips = _place()
        k = 2 * x + y
        for i in range(n):
            r2 = ws[i].shape[0] // 2
            for cx, cy in chips:
                _remote(w_refs[i].at[pl.ds(c * r2, r2)], land_refs[i].at[k, pl.ds(c * r2, r2)], send[i], recv[i],
                        (cx, cy, c)).start()
        token[...] = jnp.zeros_like(token)

    outs = pl.pallas_call(
        body, name=name,
        out_shape=[DMA_SEM] * (2 * n) + [pltpu.HBM(w.shape, w.dtype) for w in ws] + [pltpu.HBM(l.shape, l.dtype) for l in lands]
        + [jax.ShapeDtypeStruct((8, LANES), F32)],
        in_specs=[HBM] * (2 * n) + [ANY] * na, out_specs=[SEM] * (2 * n) + [HBM] * (2 * n) + [VMEM_FULL],
        input_output_aliases={i: 2 * n + i for i in range(2 * n)},
        compiler_params=pltpu.CompilerParams(has_side_effects=EFFECT),
    )(*[_in_hbm(w) for w in ws], *[_in_hbm(l) for l in lands], *after)
    return outs[:n], outs[n:2 * n], outs[2 * n:3 * n], outs[3 * n:4 * n], outs[4 * n]


def _gather_forward(send, recv, ws, lands, after, *, name):
    n = len(ws)

    def body(*refs):
        w_refs, land_refs = refs[:n], refs[n:2 * n]
        send1, recv1 = refs[2 * n:3 * n], refs[3 * n:4 * n]
        send2, recv2 = refs[4 * n + 1 + 2 * n:4 * n + 1 + 3 * n], refs[4 * n + 1 + 3 * n:4 * n + 1 + 4 * n]
        x, y, c, chips = _place()
        sibling = (x, y, 1 - c)
        for i in range(n):
            r2 = ws[i].shape[0] // 2
            win = _three_halves(land_refs[i], r2)
            done = _remote(win, win, send1[i], recv1[i], sibling)
            done.wait_send()
            done.wait_recv()
            for cx, cy in chips:
                got = land_refs[i].at[2 * cx + cy, pl.ds(c * r2, r2)]
                _remote(got, got, send2[i], recv2[i], sibling).start()
        token = refs[8 * n + 1]
        token[...] = jnp.zeros_like(token)

    outs = pl.pallas_call(
        body, name=name,
        out_shape=[pltpu.HBM(w.shape, w.dtype) for w in ws] + [pltpu.HBM(l.shape, l.dtype) for l in lands] + [DMA_SEM] * (2 * n)
        + [jax.ShapeDtypeStruct((8, LANES), F32)],
        in_specs=[HBM] * (2 * n) + [SEM] * (2 * n) + [ANY], out_specs=[HBM] * (2 * n) + [SEM] * (2 * n) + [VMEM_FULL],
        input_output_aliases={i: i for i in range(2 * n)},
        compiler_params=pltpu.CompilerParams(has_side_effects=EFFECT),
    )(*ws, *lands, *send, *recv, after)
    return outs[2 * n:3 * n], outs[3 * n:4 * n], outs[n:2 * n], outs[4 * n]


def _gather_finish(send, recv, lands, after, *, name):
    n = len(lands)

    def body(*refs):
        land_refs = refs[:n]
        send2, recv2 = refs[n:2 * n], refs[2 * n:3 * n]
        x, y, c, _ = _place()
        for i in range(n):
            win = _three_halves(land_refs[i], lands[i].shape[1] // 2)
            done = _remote(win, win, send2[i], recv2[i], (x, y, 1 - c))
            done.wait_send()
            done.wait_recv()

    return pl.pallas_call(
        body, name=name,
        out_shape=[pltpu.HBM(l.shape, l.dtype) for l in lands],
        in_specs=[HBM] * n + [SEM] * (2 * n) + [ANY], out_specs=[HBM] * n,
        input_output_aliases={i: i for i in range(n)},
        compiler_params=pltpu.CompilerParams(has_side_effects=EFFECT),
    )(*lands, *send, *recv, after)


def _scatter_start(gs, *, name):
    n = len(gs)
    lands = [lax.empty((N_DEV, g.shape[1] // 2, g.shape[2]), g.dtype) for g in gs]

    def body(*refs):
        g_refs, land_refs = refs[:n], refs[n:2 * n]
        send, recv = refs[2 * n:3 * n], refs[3 * n:4 * n]
        token = refs[6 * n]
        x, y, c, chips = _place()
        k = 2 * x + y
        me = 2 * k + c
        for i in range(n):
            r2 = gs[i].shape[1] // 2
            for cx, cy in chips:
                for cc in range(2):
                    _remote(g_refs[i].at[2 * cx + cy, pl.ds(cc * r2, r2)], land_refs[i].at[me], send[i], recv[i],
                            (cx, cy, cc)).start()
            _remote(g_refs[i].at[k, pl.ds((1 - c) * r2, r2)], land_refs[i].at[me], send[i], recv[i], (x, y, 1 - c)).start()
        token[...] = jnp.zeros_like(token)

    outs = pl.pallas_call(
        body, name=name,
        out_shape=[DMA_SEM] * (2 * n) + [pltpu.HBM(g.shape, g.dtype) for g in gs] + [pltpu.HBM(l.shape, l.dtype) for l in lands]
        + [jax.ShapeDtypeStruct((8, LANES), F32)],
        in_specs=[HBM] * (2 * n), out_specs=[SEM] * (2 * n) + [HBM] * (2 * n) + [VMEM_FULL],
        input_output_aliases={i: 2 * n + i for i in range(2 * n)},
        compiler_params=pltpu.CompilerParams(has_side_effects=EFFECT),
    )(*[_in_hbm(g) for g in gs], *[_in_hbm(l) for l in lands])
    return outs[:n], outs[n:2 * n], outs[2 * n:3 * n], outs[3 * n:4 * n], outs[4 * n]


def _scatter_wait(send, recv, gs, lands, after, *, name):
    n = len(gs)

    def body(*refs):
        land_refs = refs[n:2 * n]
        send1, recv1 = refs[2 * n:3 * n], refs[3 * n:4 * n]
        x, y, c, _ = _place()
        for i in range(n):
            win = land_refs[i].at[pl.ds(0, N_DEV - 1)]
            done = _remote(win, win, send1[i], recv1[i], (x, y, 1 - c))
            done.wait_send()
            done.wait_recv()

    outs = pl.pallas_call(
        body, name=name,
        out_shape=[pltpu.HBM(g.shape, g.dtype) for g in gs] + [pltpu.HBM(l.shape, l.dtype) for l in lands],
        in_specs=[HBM] * (2 * n) + [SEM] * (2 * n) + [ANY], out_specs=[HBM] * (2 * n),
        input_output_aliases={i: i for i in range(2 * n)},
        compiler_params=pltpu.CompilerParams(has_side_effects=EFFECT),
    )(*gs, *lands, *send, *recv, after)
    return outs[:n], outs[n:]


def _share_halves(ts, *, name):
    n = len(ts)

    def body(*refs):
        outs = refs[n:2 * n]
        send_sems, recv_sems = refs[2 * n:]
        x, y, c, _ = _place()
        sibling = (x, y, 1 - c)
        cps = []
        for i in range(n):
            r2 = ts[i].shape[0] // 2
            mine = outs[i].at[pl.ds(c * r2, r2)]
            cps.append(_remote(mine, mine, send_sems.at[i], recv_sems.at[i], sibling))
            cps[-1].start()
        for i in range(n):
            r2 = ts[i].shape[0] // 2
            got = outs[i].at[pl.ds((1 - c) * r2, r2)]
            _remote(got, got, send_sems.at[i], recv_sems.at[i], sibling).wait_recv()
        for cp in cps:
            cp.wait_send()

    return pl.pallas_call(
        body, name=name,
        out_shape=[jax.ShapeDtypeStruct(t.shape, t.dtype) for t in ts],
        in_specs=[ANY] * n, out_specs=[ANY] * n, input_output_aliases={i: i for i in range(n)},
        scratch_shapes=[pltpu.SemaphoreType.DMA((n,)), pltpu.SemaphoreType.DMA((n,))],
    )(*ts)


def _sum_pieces(land, g, idx, *, name):
    _, r2, C = land.shape
    tr = _tile(r2, max(16, (1 << 19) // C), 16)
    nr = r2 // tr

    def body(idx_ref, land_ref, own_ref, o_ref, acc_ref):
        d = pl.program_id(1)
        mine = d == idx_ref[0]

        @pl.when(d == 0)
        def _():
            acc_ref[...] = jnp.zeros_like(acc_ref)

        @pl.when(mine)
        def _():
            acc_ref[...] += own_ref[...].astype(F32)

        @pl.when(jnp.logical_not(mine))
        def _():
            acc_ref[...] += land_ref[...].astype(F32)

        @pl.when(d == N_DEV - 1)
        def _():
            o_ref[...] = acc_ref[...]

    return pl.pallas_call(
        body, name=name, out_shape=jax.ShapeDtypeStruct((2 * r2, C), F32),
        grid_spec=pltpu.PrefetchScalarGridSpec(
            num_scalar_prefetch=1, grid=(nr, N_DEV),
            in_specs=[pl.BlockSpec((None, tr, C), lambda i, d, ix: (jnp.where(d == ix[0], (d + 1) % N_DEV, d), i, 0)),
                      pl.BlockSpec((None, tr, C), lambda i, d, ix: (ix[1], ix[2] * nr + i, 0))],
            out_specs=pl.BlockSpec((tr, C), lambda i, d, ix: (ix[2] * nr + i, 0)),
            scratch_shapes=[pltpu.VMEM((tr, C), F32)]),
        compiler_params=_params("parallel", "arbitrary"),
    )(idx, land, g)


def _sum_devices(v, *, name):
    n, R, C = v.shape

    def body(v_ref, o_ref):
        acc = v_ref[0]
        for j in range(1, n):
            acc = acc + v_ref[j]
        o_ref[...] = acc

    return pl.pallas_call(body, name=name, out_shape=jax.ShapeDtypeStruct((R, C), F32),
                          in_specs=[VMEM_FULL], out_specs=VMEM_FULL)(v)


def _pair(a, b):
    n = a.shape[1]
    parts = []
    for j in range(n // PAIR):
        parts += [a[:, j * PAIR:(j + 1) * PAIR], b[:, j * PAIR:(j + 1) * PAIR]]
    return jnp.concatenate(parts, axis=1)


def _unpair(p):
    nt = p.shape[1] // (2 * PAIR)
    a = jnp.concatenate([p[:, 2 * j * PAIR:(2 * j + 1) * PAIR] for j in range(nt)], axis=1)
    b = jnp.concatenate([p[:, (2 * j + 1) * PAIR:(2 * j + 2) * PAIR] for j in range(nt)], axis=1)
    return a, b


def _from_col_shards(g):
    return jnp.transpose(g, (1, 0, 2)).reshape(g.shape[1], N_CHIP * g.shape[2])


def _to_col_shards(w):
    R, N = w.shape
    return jnp.transpose(w.reshape(R, N_CHIP, N // N_CHIP), (1, 0, 2))


def _split_heads(w, widths):
    R, N = w.shape
    per = sum(widths)
    w3 = w.reshape(R, N // per, per)
    lo = w3[:, :, :widths[0]].reshape(R, -1)
    hi = w3[:, :, widths[0]:].reshape(R, -1)
    return jnp.concatenate([lo, hi], axis=1)


def _merge_heads(w, widths):
    R, N = w.shape
    H = N // sum(widths)
    lo = w[:, :H * widths[0]].reshape(R, H, widths[0])
    hi = w[:, H * widths[0]:].reshape(R, H, widths[1])
    return jnp.concatenate([lo, hi], axis=2).reshape(R, N)


def _t5_bucket(dist):
    max_exact = REL_BUCKETS // 2
    n = jnp.maximum(dist, 0)
    large = max_exact + (jnp.log(jnp.maximum(n, 1).astype(F32) / max_exact)
                         / jnp.log(jnp.asarray(REL_MAX_DIST / max_exact, F32))
                         * (REL_BUCKETS - max_exact)).astype(jnp.int32)
    large = jnp.minimum(large, REL_BUCKETS - 1)
    return jnp.where(n < max_exact, n, large)


def _rel_tables():
    a = jnp.arange(SWA_BLOCK)
    b = jnp.arange(2 * SWA_BLOCK)
    dist = SWA_BLOCK + a[:, None] - b[None, :]
    valid = jnp.logical_and(dist >= 0, dist < SWA_BLOCK)
    onehot = jnp.logical_and(_t5_bucket(dist)[..., None] == jnp.arange(REL_BUCKETS), valid[..., None])
    onehot = onehot.astype(F32).reshape(2 * SWA_BLOCK * SWA_BLOCK, REL_BUCKETS)
    negmask = jnp.where(valid, 0.0, NEG).astype(F32).reshape(1, -1)
    return onehot, negmask


def _rope_tables(S):
    pos = jnp.arange(S, dtype=F32)
    inv = ROPE_THETA ** (-jnp.arange(0, MLA_ROPE, 2, dtype=F32) / MLA_ROPE)
    ang = pos[:, None] * inv[None, :]
    ang = jnp.concatenate([ang, ang, ang, ang], axis=-1)
    return jnp.cos(ang), jnp.sin(ang)


def _flat_pad(parts, rows=8):
    flat = jnp.concatenate([p.reshape(1, -1) for p in parts], axis=1)
    n = flat.shape[1]
    width = -(-n // (rows * LANES)) * LANES
    return jnp.pad(flat, ((0, 0), (0, rows * width - n))).reshape(rows, width)


def _unflat(vec, shapes):
    flat = vec.reshape(-1)
    out, off = [], 0
    for s in shapes:
        n = 1
        for d in s:
            n *= d
        out.append(flat[off:off + n].reshape(s))
        off += n
    return out


def kernel(x, c, w_ada, b_ada, g_pre_mix, g_post_mix, w_in, g_q_lat, w_uq, g_kv_lat, w_ukv, rel_bias, sinks, w_o, g_pre_ffn, g_post_ffn, w_up, conv_w, conv_b, w_down, loss_target, m_w_ada, m_b_ada, m_g_pre_mix, m_g_post_mix, m_w_in, m_g_q_lat, m_w_uq, m_g_kv_lat, m_w_ukv, m_rel_bias, m_sinks, m_w_o, m_g_pre_ffn, m_g_post_ffn, m_w_up, m_conv_w, m_conv_b, m_w_down, v_w_ada, v_b_ada, v_g_pre_mix, v_g_post_mix, v_w_in, v_g_q_lat, v_w_uq, v_g_kv_lat, v_w_ukv, v_rel_bias, v_sinks, v_w_o, v_g_pre_ffn, v_g_post_ffn, v_w_up, v_conv_w, v_conv_b, v_w_down):
    S, D = x.shape[1], x.shape[2]
    Rq, Rkv = g_q_lat.shape[1], g_kv_lat.shape[1]
    H = D // MLA_V
    NH = D // SWA_HD
    KW = SWA_KVH * SWA_HD
    F = w_down.shape[1] * N_CHIP
    xi, yi, ci = lax.axis_index("x"), lax.axis_index("y"), lax.axis_index("c")
    chip = 2 * xi + yi
    me = 2 * chip + ci
    x2, tgt = x[0], loss_target[0]

    c_all = _allgather8(jnp.broadcast_to(c, (8, D)), name="gather_c")[:, 0, :]
    n3 = w_ada.shape[2]
    mod_part = _ada_fwd(c_all, w_ada[0], lax.dynamic_slice(b_ada, (0, chip * n3), (1, n3)), name="ada_fwd")
    mod_all = _allgather8(mod_part, name="gather_mod")
    mod_me = lax.dynamic_index_in_dim(mod_all[0::2], me, axis=1, keepdims=False).reshape(1, 6 * D)
    sh1, sc1, gt1, sh2, sc2, gt2 = [mod_me[:, k * D:(k + 1) * D] for k in range(6)]
    cw_all = _allgather8(jnp.pad(conv_w[0], ((0, 5), (0, 0))), name="gather_conv_w")[0::2, :3]

    big = [w_in[0], w_uq[0], w_ukv[0], w_o[0], w_up[0], w_down[0]]
    local = [w.astype(BF16) for w in big]
    send_a, recv_a, srcs_a, lands_a, token = _gather_start(local[:1], (mod_all, cw_all), name="gather_start_in")
    send_b, recv_b, srcs_b, lands_b, token = _gather_start(local[1:], (token,), name="gather_start_rest")
    send1, recv1, srcs, lands = send_a + send_b, recv_a + recv_b, srcs_a + srcs_b, lands_a + lands_b
    onehot, negmask = _rel_tables()
    npb = NH // (2 * SWA_KVH)
    rb_st = jnp.transpose(rel_bias.T.reshape(SWA_KVH, npb, 2, REL_BUCKETS), (0, 2, 1, 3)).reshape(NH, REL_BUCKETS)
    bias_m = (_matmul(rb_st, onehot.T, tie=token, name="rel_bias_table") + negmask).reshape(
        2 * SWA_KVH, npb * SWA_BLOCK, 2 * SWA_BLOCK)
    h = _modnorm_fwd(x2, g_pre_mix, sc1, sh1, name="pre_mix_norm")

    def whole(land, i):
        return lax.dynamic_update_index_in_dim(land, local[i], chip, 0)

    s2, r2, l_in, _ = _gather_forward(send1[:1], recv1[:1], srcs[:1], lands[:1], h, name="gather_forward_in")
    (l_in,) = _gather_finish(s2, r2, l_in, h, name="gather_finish_in")
    gin = whole(l_in, 0)
    win = _from_col_shards(gin)
    o_kr = Rq + Rkv
    o_q = o_kr + MLA_ROPE
    o_g = o_q + NH * SWA_HD + 2 * KW
    w_lat = jnp.concatenate([win[:, :o_q], win[:, o_kr:o_q]], axis=1)
    w_swa = win[:, o_q:o_g]
    w_gate = _pair(win[:, o_g:o_g + D], win[:, o_g + D:])
    w_in_all = jnp.concatenate([w_lat, w_swa, w_gate], axis=1)
    n_lat, n_swa = w_lat.shape[1], w_swa.shape[1]
    cw_full = _from_col_shards(cw_all)
    cw = _pair(cw_full[:, :F], cw_full[:, F:])
    cb = _pair(conv_b[:, :F], conv_b[:, F:])
    cos, sin = _rope_tables(S)
    sink_v = sinks.reshape(NH)

    z_lat = _matmul(h, w_lat, name="in_proj_lat")
    z_swa = _matmul(h, w_swa, name="in_proj_swa")
    zg = _matmul(h, w_gate, name="in_proj_gate")
    s2b, r2b, l_b, _ = _gather_forward(send1[1:4], recv1[1:4], srcs[1:4], lands[1:4], zg, name="gather_forward_attn")
    nq, nkv = _lat_norm_fwd(z_lat, g_q_lat, g_kv_lat, name="lat_norm")
    l_uq, l_ukv, l_o = _gather_finish(s2b, r2b, l_b, nq, name="gather_finish_attn")
    wuq = _split_heads(_from_col_shards(whole(l_uq, 1)), (MLA_NOPE, MLA_ROPE))
    wukv = _split_heads(_from_col_shards(whole(l_ukv, 2)), (MLA_NOPE, MLA_V))
    wo = whole(l_o, 3).reshape(D, D)
    q_raw = _matmul(nq, wuq, name="uq_proj")
    kv_raw = _matmul(nkv, wukv, name="ukv_proj")
    qp, kp, vv = _mla_pack_fwd(q_raw, kv_raw, z_lat, cos, sin, o_kr, name="mla_pack")
    o_a, lse = _flash_fwd(qp, kp, vv, name="mla_attn")
    s2c, r2c, l_c, tok_c = _gather_forward(send1[4:], recv1[4:], srcs[4:], lands[4:], o_a, name="gather_forward_ffn")
    o_b = _swa_fwd(z_swa, bias_m, sink_v, name="swa_attn")
    mixin = _gate_fwd(zg, o_a, o_b, name="gate_mix")
    mix = _matmul(mixin, wo, tie=tok_c, name="o_proj")
    x1 = _resnorm_fwd(x2, mix, g_post_mix, gt1, name="post_mix_norm")
    h2 = _modnorm_fwd(x1, g_pre_ffn, sc2, sh2, name="pre_ffn_norm")
    l_up, l_down = _gather_finish(s2c, r2c, l_c, h2, name="gather_finish_ffn")
    wup_full = _from_col_shards(whole(l_up, 4))
    wup = _pair(wup_full[:, :F], wup_full[:, F:])
    wdown = whole(l_down, 5).reshape(F, D)
    t = _matmul(h2, wup, out_dtype=BF16, name="up_proj")
    a = _conv_gate_fwd(t, cw, cb, name="conv_gate")
    yv = _matmul(a, wdown, name="down_proj")
    dout, loss_tile = _resnorm_loss(x1, yv, g_post_ffn, gt2, tgt, name="post_ffn_norm_loss")

    idx = jnp.stack([me, chip, ci]).astype(jnp.int32)
    big_params = dict(w_in=(w_in, m_w_in, v_w_in), w_uq=(w_uq, m_w_uq, v_w_uq), w_ukv=(w_ukv, m_w_ukv, v_w_ukv),
                      w_o=(w_o, m_w_o, v_w_o), w_up=(w_up, m_w_up, v_w_up), w_down=(w_down, m_w_down, v_w_down))
    res = {}

    def start(nms, gs):
        send, recv, gsrc, glands, tok = _scatter_start(gs, name="grads_start_" + nms[0])
        return (nms, send, recv, gsrc, glands), tok

    def finish(pending, after):
        nms, send, recv, gsrc, glands = pending
        gsrc, glands = _scatter_wait(send, recv, gsrc, glands, after, name="grads_wait_" + nms[0])
        halves = [_sum_pieces(l, g, idx, name="grad_sum_" + nm) for l, g, nm in zip(glands, gsrc, nms)]
        for nm, g in zip(nms, _share_halves(halves, name="grads_share_" + nms[0])):
            w, m, v = big_params[nm]
            res[nm] = (g,) + tuple(_adamw(w[0], g, m[0], v[0], name="adamw_" + nm))

    dy, dg_post_ffn, dgt2 = _resnorm_bwd(dout, yv, g_post_ffn, gt2, name="post_ffn_norm_bwd")
    dw_down = _matmul(a, dy, ta=True, out_dtype=BF16, name="down_proj_dw")
    p_down, tok = start(["w_down"], [dw_down.reshape(N_CHIP, F // N_CHIP, D)])
    da = _matmul(dy, wdown, tb=True, out_dtype=BF16, tie=tok, name="down_proj_dx")
    du, dcw_p, dcb_p = _conv_gate_bwd(t, da, cw, cb, name="conv_gate_bwd")
    dt = _conv_bwd_dt(du, cw, name="conv_bwd_dt")
    dw_up_p = _matmul(h2, dt, ta=True, out_dtype=BF16, name="up_proj_dw")
    p_up, tok = start(["w_up"], [_to_col_shards(jnp.concatenate(_unpair(dw_up_p), axis=1))])
    dh2 = _matmul(dt, wup, tb=True, tie=tok, name="up_proj_dx")
    dx1, dg_pre_ffn, dsc2, dsh2 = _modnorm_bwd(dh2, x1, g_pre_ffn, sc2, dout, name="pre_ffn_norm_bwd")
    dmix, dg_post_mix, dgt1 = _resnorm_bwd(dx1, mix, g_post_mix, gt1, name="post_mix_norm_bwd")
    dw_o = _matmul(mixin, dmix, ta=True, out_dtype=BF16, name="o_proj_dw")
    p_o, tok = start(["w_o"], [dw_o.reshape(N_CHIP, D // N_CHIP, D)])
    dmixin = _matmul(dmix, wo, tb=True, tie=tok, name="o_proj_dx")
    do_a, do_b, dzg = _gate_bwd(dmixin, zg, o_a, o_b, name="gate_mix_bwd")
    dqp, dkp, dvv = _flash_bwd(qp, kp, vv, o_a, do_a, lse, name="mla_attn_bwd")
    dq_raw, dkv_raw, dkr = _mla_pack_bwd(dqp, dkp, dvv, cos, sin, name="mla_pack_bwd")
    dw_uq_p = _matmul(nq, dq_raw, ta=True, out_dtype=BF16, name="uq_proj_dw")
    dw_ukv_p = _matmul(nkv, dkv_raw, ta=True, out_dtype=BF16, name="ukv_proj_dw")
    p_qkv, tok = start(["w_uq", "w_ukv"], [_to_col_shards(_merge_heads(dw_uq_p, (MLA_NOPE, MLA_ROPE))),
                                           _to_col_shards(_merge_heads(dw_ukv_p, (MLA_NOPE, MLA_V)))])
    dnq = _matmul(dq_raw, wuq, tb=True, tie=tok, name="uq_proj_dx")
    dnkv = _matmul(dkv_raw, wukv, tb=True, name="ukv_proj_dx")
    dz_lat, dg_q, dg_kv = _lat_norm_bwd(z_lat, dnq, dnkv, dkr, g_q_lat, g_kv_lat, name="lat_norm_bwd")
    dz_swa, dbias, dsink = _swa_bwd(z_swa, bias_m, sink_v, o_b, do_b, name="swa_attn_bwd")
    dz = jnp.concatenate([dz_lat, dz_swa, dzg], axis=1)
    dw_in_p = _matmul(h, dz, ta=True, out_dtype=BF16, name="in_proj_dw")
    dga, dgb = _unpair(dw_in_p[:, n_lat + n_swa:])
    dw_in = jnp.concatenate([dw_in_p[:, :o_q], dw_in_p[:, n_lat:n_lat + n_swa], dga, dgb], axis=1)
    p_in, tok = start(["w_in"], [_to_col_shards(dw_in)])
    dh = _matmul(dz, w_in_all, tb=True, tie=tok, name="in_proj_dx")
    grad_x, dg_pre_mix, dsc1, dsh1 = _modnorm_bwd(dh, x2, g_pre_mix, sc1, dx1, name="pre_mix_norm_bwd")
    drel_st = _matmul(dbias.reshape(NH, -1), onehot, tie=grad_x, name="rel_bias_bwd")
    for pending in (p_down, p_up, p_o, p_qkv):
        finish(pending, drel_st)
    drel = jnp.transpose(drel_st.reshape(SWA_KVH, 2, npb, REL_BUCKETS), (0, 2, 1, 3)).reshape(NH, REL_BUCKETS).T

    dcw = jnp.concatenate(_unpair(dcw_p), axis=1)
    dcb = jnp.concatenate(_unpair(dcb_p), axis=1)
    dmod = jnp.concatenate([dsh1, dsc1, dgt1, dsh2, dsc2, dgt2], axis=1)
    small = [dmod, dg_pre_mix, dg_post_mix, dg_pre_ffn, dg_post_ffn, dg_q, dg_kv, drel, dsink[:, :NH], dcb, dcw]
    shapes = [p.shape for p in small]
    small_all = _allgather8(_flat_pad(small), tie=res["w_ukv"][1], name="gather_small_grads")
    tot = _unflat(_sum_devices(small_all, name="sum_small_grads"), shapes)
    g_b_ada, g_pre_mix_g, g_post_mix_g, g_pre_ffn_g, g_post_ffn_g, g_q_g, g_kv_g, g_rel, g_sinks, g_cb, g_cw_full = tot
    dmod_all = small_all.reshape(N_DEV, -1)[:, :6 * D]
    g_w_ada = _ada_bwd(c_all.T, lax.dynamic_slice(dmod_all, (0, chip * n3), (N_DEV, n3)), name="ada_bwd")
    ncw = conv_w.shape[2]
    g_cw = lax.dynamic_slice(g_cw_full, (0, chip * ncw), (3, ncw))

    res["w_ada"] = (g_w_ada,) + tuple(_adamw(w_ada[0], g_w_ada, m_w_ada[0], v_w_ada[0], name="adamw_w_ada"))
    finish(p_in, g_w_ada)
    snames = ["b_ada", "g_pre_mix", "g_post_mix", "g_pre_ffn", "g_post_ffn", "g_q_lat", "g_kv_lat", "rel_bias", "sinks",
              "conv_b", "conv_w"]
    sw = [b_ada, g_pre_mix, g_post_mix, g_pre_ffn, g_post_ffn, g_q_lat, g_kv_lat, rel_bias, sinks, conv_b, conv_w]
    sm = [m_b_ada, m_g_pre_mix, m_g_post_mix, m_g_pre_ffn, m_g_post_ffn, m_g_q_lat, m_g_kv_lat, m_rel_bias, m_sinks,
          m_conv_b, m_conv_w]
    sv = [v_b_ada, v_g_pre_mix, v_g_post_mix, v_g_pre_ffn, v_g_post_ffn, v_g_q_lat, v_g_kv_lat, v_rel_bias, v_sinks,
          v_conv_b, v_conv_w]
    sg = [g_b_ada, g_pre_mix_g, g_post_mix_g, g_pre_ffn_g, g_post_ffn_g, g_q_g, g_kv_g, g_rel, g_sinks, g_cb, g_cw]
    sshapes = [w.shape for w in sw]
    sd, snm, snv = _adamw(_flat_pad(sw), _flat_pad(sg), _flat_pad(sm), _flat_pad(sv), name="adamw_small")
    sd, snm, snv = _unflat(sd, sshapes), _unflat(snm, sshapes), _unflat(snv, sshapes)
    for k, nm in enumerate(snames):
        res[nm] = (sg[k].reshape(sshapes[k]), sd[k], snm[k], snv[k])

    order = ["w_ada", "b_ada", "g_pre_mix", "g_post_mix", "w_in", "g_q_lat", "w_uq", "g_kv_lat", "w_ukv", "rel_bias", "sinks",
             "w_o", "g_pre_ffn", "g_post_ffn", "w_up", "conv_w", "conv_b", "w_down"]
    ref_shapes = dict(w_ada=w_ada.shape, w_in=w_in.shape, w_uq=w_uq.shape, w_ukv=w_ukv.shape, w_o=w_o.shape,
                      w_up=w_up.shape, w_down=w_down.shape)
    outs = []
    for k in range(4):
        for nm in order:
            arr = res[nm][k]
            outs.append(arr.reshape(ref_shapes[nm]) if nm in ref_shapes else arr)
    loss = lax.psum(loss_tile[0, 0], ("x", "y", "c"))
    return (loss, grad_x[None], *outs)
```

```python
import functools

import jax
import jax.numpy as jnp
from jax import lax
from jax.experimental import pallas as pl
from jax.experimental.pallas import tpu as pltpu

F32 = jnp.float32
BF16 = jnp.bfloat16
MESH = pl.DeviceIdType.MESH
HIGHEST = lax.Precision.HIGHEST

N_DEV = 8
N_CHIP = 4
LANES = 128
MLA_NOPE = 128
MLA_ROPE = 64
MLA_V = 128
MLA_QK = MLA_NOPE + MLA_ROPE
MLA_QK_PAD = 256
ROPE_THETA = 10000.0
SWA_HD = 64
SWA_KVH = 4
SWA_BLOCK = 128
REL_BUCKETS = 32
REL_MAX_DIST = 128
PAIR = 512
EPS = 1e-6
NEG = -1e30
ADAM_LR = 0.001
ADAM_B1 = 0.9
ADAM_B2 = 0.999
ADAM_EPS = 1e-08
ADAM_WD = 0.01
ADAM_STEP = 10

ANY = pl.BlockSpec(memory_space=pl.ANY)
VMEM_FULL = pl.BlockSpec(memory_space=pltpu.VMEM)
SMEM_FULL = pl.BlockSpec(memory_space=pltpu.SMEM)


def _params(*sem):
    return pltpu.CompilerParams(dimension_semantics=sem if sem else None)


def _tied(body, tie):
    if tie is None:
        return body, [], []
    ties = list(tie) if isinstance(tie, (list, tuple)) else [tie]

    def tied_body(*refs):
        body(*refs[len(ties):])

    return tied_body, [ANY] * len(ties), ties


def _tile(n, pref, unit=LANES):
    best = None
    for t in range(unit, min(n, pref) + 1, unit):
        if n % t == 0:
            best = t
    return n if best is None else best


def _matmul(a, b, *, ta=False, tb=False, out_dtype=F32, tie=None, shards=None, name):
    a2 = a.shape[1:] if shards == "k" else a.shape
    b2 = b.shape[1:] if shards else b.shape
    nsh = b.shape[0] if shards else 1
    K, M = a2 if ta else a2[::-1]
    N, K2 = b2 if tb else b2[::-1]
    assert K == K2, (a.shape, b.shape, ta, tb)
    exact = a.dtype == F32
    tn = _tile(N, 1536)
    tk = _tile(K, 2048)
    nkc = K // tk
    nk = nkc * (nsh if shards == "k" else 1)
    tm = M if M < 8 else _tile(M, 1024, LANES if ta else 8)
    dn = (((0 if ta else 1,), (1 if tb else 0,)), ((), ()))
    kax = 3 if shards == "out" else 2

    def product(a_ref, b_ref):
        return lax.dot_general(a_ref[...], b_ref[...], dn, preferred_element_type=F32,
                               precision=HIGHEST if exact else None)

    def body_acc(a_ref, b_ref, o_ref, acc_ref):
        k = pl.program_id(kax)

        @pl.when(k == 0)
        def _():
            acc_ref[...] = product(a_ref, b_ref)

        @pl.when(jnp.logical_and(k > 0, k < nk - 1))
        def _():
            acc_ref[...] += product(a_ref, b_ref)

        @pl.when(k == nk - 1)
        def _():
            o_ref[...] = (acc_ref[...] + product(a_ref, b_ref)).astype(o_ref.dtype)

    def body_one(a_ref, b_ref, o_ref):
        o_ref[...] = product(a_ref, b_ref).astype(o_ref.dtype)

    a_blk, b_blk = ((tk, tm) if ta else (tm, tk)), ((tn, tk) if tb else (tk, tn))
    a_at = (lambda i, k: (k, i)) if ta else (lambda i, k: (i, k))
    b_at = (lambda j, k: (j, k)) if tb else (lambda j, k: (k, j))
    if shards == "out":
        grid = (nsh, M // tm, N // tn, nk)
        a_spec = pl.BlockSpec(a_blk, lambda s, i, j, k: a_at(i, k))
        b_spec = pl.BlockSpec((None,) + b_blk, lambda s, i, j, k: (s,) + b_at(j, k))
        o_spec = pl.BlockSpec((None, tm, tn), lambda s, i, j, k: (s, i, j))
        out_shape = jax.ShapeDtypeStruct((nsh, M, N), out_dtype)
        sem = ("parallel", "parallel", "parallel", "arbitrary")
    elif shards == "k":
        grid = (M // tm, N // tn, nk)
        a_spec = pl.BlockSpec((None,) + a_blk, lambda i, j, k: (k // nkc,) + a_at(i, k % nkc))
        b_spec = pl.BlockSpec((None,) + b_blk, lambda i, j, k: (k // nkc,) + b_at(j, k % nkc))
        o_spec = pl.BlockSpec((tm, tn), lambda i, j, k: (i, j))
        out_shape = jax.ShapeDtypeStruct((M, N), out_dtype)
        sem = ("parallel", "parallel", "arbitrary")
    else:
        grid = (M // tm, N // tn, nk)
        a_spec = pl.BlockSpec(a_blk, lambda i, j, k: a_at(i, k))
        b_spec = pl.BlockSpec(b_blk, lambda i, j, k: b_at(j, k))
        o_spec = pl.BlockSpec((tm, tn), lambda i, j, k: (i, j))
        out_shape = jax.ShapeDtypeStruct((M, N), out_dtype)
        sem = ("parallel", "parallel", "arbitrary")
    body, tspec, targ = _tied(body_one if nk == 1 else body_acc, tie)
    return pl.pallas_call(
        body, name=name, out_shape=out_shape, grid=grid, in_specs=tspec + [a_spec, b_spec], out_specs=o_spec,
        scratch_shapes=[] if nk == 1 else [pltpu.VMEM((tm, tn), F32)],
        compiler_params=_params(*sem),
    )(*targ, a, b)


def _row_tile(S, width):
    return _tile(S, max(8, (1 << 19) // width), 8)


def _rstd(x):
    return lax.rsqrt(jnp.mean(x * x, axis=-1, keepdims=True) + EPS)


def _acc_rows(ref, val, first):
    s = jnp.sum(val, axis=0, keepdims=True)

    @pl.when(first)
    def _():
        ref[...] = s

    @pl.when(jnp.logical_not(first))
    def _():
        ref[...] += s


def _modnorm_fwd(x, g, sc, sh, *, name):
    S, D = x.shape
    tr = _row_tile(S, D)

    def body(x_ref, g_ref, sc_ref, sh_ref, h_ref):
        xv = x_ref[...]
        n = (xv * _rstd(xv)) * g_ref[...]
        h_ref[...] = (n * (1.0 + sc_ref[...]) + sh_ref[...]).astype(BF16)

    row = pl.BlockSpec((tr, D), lambda i: (i, 0))
    vec = pl.BlockSpec((1, D), lambda i: (0, 0))
    return pl.pallas_call(
        body, name=name, out_shape=jax.ShapeDtypeStruct((S, D), BF16), grid=(S // tr,),
        in_specs=[row, vec, vec, vec], out_specs=row, compiler_params=_params("parallel"),
    )(x, g, sc, sh)


def _modnorm_bwd(dh, x, g, sc, dres, *, name):
    S, D = x.shape
    tr = _row_tile(S, D)

    def body(dh_ref, x_ref, g_ref, sc_ref, dres_ref, dx_ref, dg_ref, dsc_ref, dsh_ref):
        first = pl.program_id(0) == 0
        xv = x_ref[...]
        dhv = dh_ref[...]
        gv = g_ref[...]
        r = _rstd(xv)
        xhat = xv * r
        _acc_rows(dsh_ref, dhv, first)
        _acc_rows(dsc_ref, dhv * (xhat * gv), first)
        dn = dhv * (1.0 + sc_ref[...])
        _acc_rows(dg_ref, dn * xhat, first)
        dxhat = dn * gv
        proj = jnp.mean(dxhat * xhat, axis=-1, keepdims=True)
        dx_ref[...] = r * (dxhat - xhat * proj) + dres_ref[...]

    row = pl.BlockSpec((tr, D), lambda i: (i, 0))
    vec = pl.BlockSpec((1, D), lambda i: (0, 0))
    vshape = jax.ShapeDtypeStruct((1, D), F32)
    return pl.pallas_call(
        body, name=name,
        out_shape=(jax.ShapeDtypeStruct((S, D), F32), vshape, vshape, vshape), grid=(S // tr,),
        in_specs=[row, row, vec, vec, row], out_specs=(row, vec, vec, vec),
        compiler_params=_params("arbitrary"),
    )(dh, x, g, sc, dres)


def _resnorm_fwd(xres, m, g, gt, *, name):
    S, D = xres.shape
    tr = _row_tile(S, D)

    def body(x_ref, m_ref, g_ref, gt_ref, o_ref):
        mv = m_ref[...]
        o_ref[...] = x_ref[...] + gt_ref[...] * ((mv * _rstd(mv)) * g_ref[...])

    row = pl.BlockSpec((tr, D), lambda i: (i, 0))
    vec = pl.BlockSpec((1, D), lambda i: (0, 0))
    return pl.pallas_call(
        body, name=name, out_shape=jax.ShapeDtypeStruct((S, D), F32), grid=(S // tr,),
        in_specs=[row, row, vec, vec], out_specs=row, compiler_params=_params("parallel"),
    )(xres, m, g, gt)


def _resnorm_loss(xres, m, g, gt, target, *, name):
    S, D = xres.shape
    tr = _row_tile(S, D)

    def body(x_ref, m_ref, g_ref, gt_ref, t_ref, d_ref, loss_ref):
        mv = m_ref[...]
        out = x_ref[...] + gt_ref[...] * ((mv * _rstd(mv)) * g_ref[...])
        err = out - t_ref[...]
        d_ref[...] = err * (1.0 / D)
        part = 0.5 * jnp.sum(jnp.mean(err * err, axis=-1, keepdims=True), axis=0, keepdims=True)
        part = jnp.broadcast_to(part, loss_ref.shape)

        @pl.when(pl.program_id(0) == 0)
        def _():
            loss_ref[...] = part

        @pl.when(pl.program_id(0) != 0)
        def _():
            loss_ref[...] += part

    row = pl.BlockSpec((tr, D), lambda i: (i, 0))
    vec = pl.BlockSpec((1, D), lambda i: (0, 0))
    return pl.pallas_call(
        body, name=name,
        out_shape=(jax.ShapeDtypeStruct((S, D), F32), jax.ShapeDtypeStruct((8, LANES), F32)), grid=(S // tr,),
        in_specs=[row, row, vec, vec, row], out_specs=(row, pl.BlockSpec((8, LANES), lambda i: (0, 0))),
        compiler_params=_params("arbitrary"),
    )(xres, m, g, gt, target)


def _resnorm_bwd(dout, m, g, gt, *, name):
    S, D = m.shape
    tr = _row_tile(S, D)

    def body(d_ref, m_ref, g_ref, gt_ref, dm_ref, dg_ref, dgt_ref):
        first = pl.program_id(0) == 0
        mv = m_ref[...]
        dv = d_ref[...]
        gv = g_ref[...]
        r = _rstd(mv)
        mhat = mv * r
        _acc_rows(dgt_ref, dv * (mhat * gv), first)
        dn = dv * gt_ref[...]
        _acc_rows(dg_ref, dn * mhat, first)
        dmhat = dn * gv
        proj = jnp.mean(dmhat * mhat, axis=-1, keepdims=True)
        dm_ref[...] = (r * (dmhat - mhat * proj)).astype(BF16)

    row = pl.BlockSpec((tr, D), lambda i: (i, 0))
    vec = pl.BlockSpec((1, D), lambda i: (0, 0))
    vshape = jax.ShapeDtypeStruct((1, D), F32)
    return pl.pallas_call(
        body, name=name, out_shape=(jax.ShapeDtypeStruct((S, D), BF16), vshape, vshape), grid=(S // tr,),
        in_specs=[row, row, vec, vec], out_specs=(row, vec, vec), compiler_params=_params("arbitrary"),
    )(dout, m, g, gt)


def _lat_norm_fwd(z_lat, g_q, g_kv, *, name):
    S, W = z_lat.shape
    Rq, Rkv = g_q.shape[1], g_kv.shape[1]
    tr = _row_tile(S, W)

    def body(z_ref, gq_ref, gkv_ref, nq_ref, nkv_ref):
        cq = z_ref[:, :Rq]
        ckv = z_ref[:, Rq:Rq + Rkv]
        nq_ref[...] = ((cq * _rstd(cq)) * gq_ref[...]).astype(BF16)
        nkv_ref[...] = ((ckv * _rstd(ckv)) * gkv_ref[...]).astype(BF16)

    return pl.pallas_call(
        body, name=name,
        out_shape=(jax.ShapeDtypeStruct((S, Rq), BF16), jax.ShapeDtypeStruct((S, Rkv), BF16)), grid=(S // tr,),
        in_specs=[pl.BlockSpec((tr, W), lambda i: (i, 0)), pl.BlockSpec((1, Rq), lambda i: (0, 0)),
                  pl.BlockSpec((1, Rkv), lambda i: (0, 0))],
        out_specs=(pl.BlockSpec((tr, Rq), lambda i: (i, 0)), pl.BlockSpec((tr, Rkv), lambda i: (i, 0))),
        compiler_params=_params("parallel"),
    )(z_lat, g_q, g_kv)


def _lat_norm_bwd(z_lat, dnq, dnkv, dkr, g_q, g_kv, *, name):
    S, W = z_lat.shape
    Rq, Rkv = g_q.shape[1], g_kv.shape[1]
    tr = _row_tile(S, W)

    def one(c, dn, gv):
        r = _rstd(c)
        chat = c * r
        dchat = dn * gv
        proj = jnp.mean(dchat * chat, axis=-1, keepdims=True)
        return r * (dchat - chat * proj), dn * chat

    def body(z_ref, dnq_ref, dnkv_ref, dkr_ref, gq_ref, gkv_ref, dz_ref, dgq_ref, dgkv_ref):
        first = pl.program_id(0) == 0
        dcq, pq = one(z_ref[:, :Rq], dnq_ref[...], gq_ref[...])
        dckv, pkv = one(z_ref[:, Rq:Rq + Rkv], dnkv_ref[...], gkv_ref[...])
        _acc_rows(dgq_ref, pq, first)
        _acc_rows(dgkv_ref, pkv, first)
        dz_ref[:, :Rq] = dcq.astype(BF16)
        dz_ref[:, Rq:Rq + Rkv] = dckv.astype(BF16)
        dz_ref[:, Rq + Rkv:] = dkr_ref[...].astype(BF16)

    return pl.pallas_call(
        body, name=name,
        out_shape=(jax.ShapeDtypeStruct((S, W), BF16), jax.ShapeDtypeStruct((1, Rq), F32),
                   jax.ShapeDtypeStruct((1, Rkv), F32)), grid=(S // tr,),
        in_specs=[pl.BlockSpec((tr, W), lambda i: (i, 0)), pl.BlockSpec((tr, Rq), lambda i: (i, 0)),
                  pl.BlockSpec((tr, Rkv), lambda i: (i, 0)), pl.BlockSpec((tr, LANES), lambda i: (i, 0)),
                  pl.BlockSpec((1, Rq), lambda i: (0, 0)), pl.BlockSpec((1, Rkv), lambda i: (0, 0))],
        out_specs=(pl.BlockSpec((tr, W), lambda i: (i, 0)), pl.BlockSpec((1, Rq), lambda i: (0, 0)),
                   pl.BlockSpec((1, Rkv), lambda i: (0, 0))),
        compiler_params=_params("arbitrary"),
    )(z_lat, dnq, dnkv, dkr, g_q, g_kv)


def _rot(x, lo32):
    a = pltpu.roll(x, 32, 1)
    b = pltpu.roll(x, LANES - 32, 1)
    return jnp.where(lo32, -b, a)


def _rot_t(g, lo32):
    a = pltpu.roll(g, 32, 1)
    b = pltpu.roll(g, LANES - 32, 1)
    return jnp.where(lo32, b, -a)


def _mla_pack_fwd(q_raw, kv_raw, z_lat, cos, sin, kr_off, *, name):
    S = q_raw.shape[0]
    H = kv_raw.shape[1] // (MLA_NOPE + MLA_V)
    W = z_lat.shape[1]
    scale = MLA_QK ** -0.5
    tr = min(S, 128)
    nope_w = H * MLA_NOPE

    def body(q_ref, kv_ref, z_ref, cos_ref, sin_ref, qp_ref, kp_ref, v_ref):
        lane = lax.broadcasted_iota(jnp.int32, (tr, LANES), 1)
        lo32 = (lane % 64) < 32
        lo64 = lane < 64
        c = cos_ref[...]
        s = sin_ref[...]
        kr = z_ref[:, kr_off:kr_off + LANES]
        kr = (kr * c + _rot(kr, lo32) * s).astype(BF16)
        for hp in range(H // 2):
            xb = q_ref[:, nope_w + hp * LANES:nope_w + (hp + 1) * LANES]
            rb = (xb * c + _rot(xb, lo32) * s) * scale
            for e in range(2):
                h = 2 * hp + e
                base = h * MLA_QK_PAD
                qp_ref[:, base:base + LANES] = (q_ref[:, h * LANES:(h + 1) * LANES] * scale).astype(BF16)
                keep = lo64 if e == 0 else jnp.logical_not(lo64)
                qp_ref[:, base + LANES:base + 2 * LANES] = jnp.where(keep, rb, 0.0).astype(BF16)
                kp_ref[:, base:base + LANES] = kv_ref[:, h * LANES:(h + 1) * LANES].astype(BF16)
                kp_ref[:, base + LANES:base + 2 * LANES] = kr
        v_ref[...] = kv_ref[:, nope_w:].astype(BF16)

    return pl.pallas_call(
        body, name=name,
        out_shape=(jax.ShapeDtypeStruct((S, H * MLA_QK_PAD), BF16), jax.ShapeDtypeStruct((S, H * MLA_QK_PAD), BF16),
                   jax.ShapeDtypeStruct((S, H * MLA_V), BF16)), grid=(S // tr,),
        in_specs=[pl.BlockSpec((tr, q_raw.shape[1]), lambda i: (i, 0)), pl.BlockSpec((tr, kv_raw.shape[1]), lambda i: (i, 0)),
                  pl.BlockSpec((tr, W), lambda i: (i, 0)), pl.BlockSpec((tr, LANES), lambda i: (i, 0)),
                  pl.BlockSpec((tr, LANES), lambda i: (i, 0))],
        out_specs=(pl.BlockSpec((tr, H * MLA_QK_PAD), lambda i: (i, 0)), pl.BlockSpec((tr, H * MLA_QK_PAD), lambda i: (i, 0)),
                   pl.BlockSpec((tr, H * MLA_V), lambda i: (i, 0))),
        compiler_params=_params("parallel"),
    )(q_raw, kv_raw, z_lat, cos, sin)


def _mla_pack_bwd(dqp, dkp, dv, cos, sin, *, name):
    S = dqp.shape[0]
    H = dv.shape[1] // MLA_V
    scale = MLA_QK ** -0.5
    tr = min(S, 128)
    nope_w = H * MLA_NOPE

    def body(dqp_ref, dkp_ref, dv_ref, cos_ref, sin_ref, dq_ref, dkv_ref, dkr_ref):
        lane = lax.broadcasted_iota(jnp.int32, (tr, LANES), 1)
        lo32 = (lane % 64) < 32
        lo64 = lane < 64
        c = cos_ref[...]
        s = sin_ref[...]
        dkr2 = jnp.zeros((tr, LANES), F32)
        for hp in range(H // 2):
            be = (2 * hp) * MLA_QK_PAD
            bo = (2 * hp + 1) * MLA_QK_PAD
            g = jnp.where(lo64, dqp_ref[:, be + LANES:be + 2 * LANES], dqp_ref[:, bo + LANES:bo + 2 * LANES]) * scale
            dq_ref[:, nope_w + hp * LANES:nope_w + (hp + 1) * LANES] = (g * c + _rot_t(g * s, lo32)).astype(BF16)
            for h, base in ((2 * hp, be), (2 * hp + 1, bo)):
                dq_ref[:, h * LANES:(h + 1) * LANES] = (dqp_ref[:, base:base + LANES] * scale).astype(BF16)
                dkv_ref[:, h * LANES:(h + 1) * LANES] = dkp_ref[:, base:base + LANES].astype(BF16)
                dkr2 = dkr2 + dkp_ref[:, base + LANES:base + 2 * LANES]
        dkr2 = dkr2 * c + _rot_t(dkr2 * s, lo32)
        dkr2 = dkr2 + pltpu.roll(dkr2, 64, 1)
        dkr_ref[...] = jnp.where(lo64, dkr2, 0.0)
        dkv_ref[:, nope_w:] = dv_ref[...].astype(BF16)

    return pl.pallas_call(
        body, name=name,
        out_shape=(jax.ShapeDtypeStruct((S, nope_w + H * MLA_ROPE), BF16), jax.ShapeDtypeStruct((S, 2 * nope_w), BF16),
                   jax.ShapeDtypeStruct((S, LANES), F32)), grid=(S // tr,),
        in_specs=[pl.BlockSpec((tr, H * MLA_QK_PAD), lambda i: (i, 0)), pl.BlockSpec((tr, H * MLA_QK_PAD), lambda i: (i, 0)),
                  pl.BlockSpec((tr, H * MLA_V), lambda i: (i, 0)), pl.BlockSpec((tr, LANES), lambda i: (i, 0)),
                  pl.BlockSpec((tr, LANES), lambda i: (i, 0))],
        out_specs=(pl.BlockSpec((tr, nope_w + H * MLA_ROPE), lambda i: (i, 0)), pl.BlockSpec((tr, 2 * nope_w), lambda i: (i, 0)),
                   pl.BlockSpec((tr, LANES), lambda i: (i, 0))),
        compiler_params=_params("parallel"),
    )(dqp, dkp, dv, cos, sin)


FLASH_HB = 2


def _causal_pairs(nb):
    qi = [i for i in range(nb) for j in range(i + 1)]
    kj = [j for i in range(nb) for j in range(i + 1)]
    return jnp.asarray(qi, jnp.int32), jnp.asarray(kj, jnp.int32)


def _scores(q, k, diagonal, t):
    s = lax.dot_general(q, k, (((1,), (1,)), ((), ())), preferred_element_type=F32)
    if diagonal:
        row = lax.broadcasted_iota(jnp.int32, (t, t), 0)
        col = lax.broadcasted_iota(jnp.int32, (t, t), 1)
        s = jnp.where(col <= row, s, NEG)
    return s


def _flash_fwd(qp, kp, v, *, name):
    S = qp.shape[0]
    H = v.shape[1] // MLA_V
    t = min(S, 512)
    nb = S // t
    HB = 2 * FLASH_HB
    qi, kj = _causal_pairs(nb)
    QW, VW = MLA_QK_PAD, MLA_V

    def body(qi_ref, kj_ref, q_ref, k_ref, v_ref, o_ref, lse_ref, m_s, l_s, acc_s):
        pr = pl.program_id(1)
        i = qi_ref[pr]
        j = kj_ref[pr]

        @pl.when(j == 0)
        def _():
            m_s[...] = jnp.full_like(m_s, NEG)
            l_s[...] = jnp.zeros_like(l_s)
            acc_s[...] = jnp.zeros_like(acc_s)

        def step(diagonal):
            for hh in range(HB):
                s = _scores(q_ref[:, hh * QW:(hh + 1) * QW], k_ref[:, hh * QW:(hh + 1) * QW], diagonal, t)
                m_prev = m_s[hh]
                m_cur = jnp.maximum(m_prev, jnp.max(s, axis=1, keepdims=True))
                alpha = jnp.exp(m_prev - m_cur)
                p = jnp.exp(s - m_cur[:, :1])
                l_new = alpha * l_s[hh] + jnp.sum(p, axis=1, keepdims=True)
                acc = alpha * acc_s[hh] + jnp.dot(p.astype(BF16), v_ref[:, hh * VW:(hh + 1) * VW], preferred_element_type=F32)
                if diagonal:
                    o_ref[:, hh * VW:(hh + 1) * VW] = acc / l_new
                    lse_ref[hh] = m_cur + jnp.log(l_new)
                else:
                    l_s[hh] = l_new
                    acc_s[hh] = acc
                    m_s[hh] = m_cur

        @pl.when(i != j)
        def _():
            step(False)

        @pl.when(i == j)
        def _():
            step(True)

    return pl.pallas_call(
        body, name=name,
        out_shape=(jax.ShapeDtypeStruct((S, H * VW), F32), jax.ShapeDtypeStruct((H, S, LANES), F32)),
        grid_spec=pltpu.PrefetchScalarGridSpec(
            num_scalar_prefetch=2, grid=(H // HB, qi.shape[0]),
            in_specs=[pl.BlockSpec((t, HB * QW), lambda g, p, qi, kj: (qi[p], g)),
                      pl.BlockSpec((t, HB * QW), lambda g, p, qi, kj: (kj[p], g)),
                      pl.BlockSpec((t, HB * VW), lambda g, p, qi, kj: (kj[p], g))],
            out_specs=(pl.BlockSpec((t, HB * VW), lambda g, p, qi, kj: (qi[p], g)),
                       pl.BlockSpec((HB, t, LANES), lambda g, p, qi, kj: (g, qi[p], 0))),
            scratch_shapes=[pltpu.VMEM((HB, t, LANES), F32), pltpu.VMEM((HB, t, LANES), F32), pltpu.VMEM((HB, t, VW), F32)]),
        compiler_params=_params("parallel", "arbitrary"),
    )(qi, kj, qp, kp, v)


def _flash_bwd(qp, kp, v, o, do, lse, *, name):
    S = qp.shape[0]
    H = v.shape[1] // MLA_V
    t = min(S, 512)
    nb = S // t
    HB = FLASH_HB
    qi, kj = _causal_pairs(nb)
    QW, VW = MLA_QK_PAD, MLA_V
    tn = (((0,), (0,)), ((), ()))
    nt = (((1,), (1,)), ((), ()))

    def body(qi_ref, kj_ref, q_ref, k_ref, v_ref, o_ref, do_ref, lse_ref, dq_ref, dk_ref, dv_ref, dq_s):
        pr = pl.program_id(1)
        i = qi_ref[pr]
        j = kj_ref[pr]
        rows = pl.ds(pl.multiple_of(j * t, t), t)

        @pl.when(pr == 0)
        def _():
            dk_ref[...] = jnp.zeros_like(dk_ref)
            dv_ref[...] = jnp.zeros_like(dv_ref)

        @pl.when(j == 0)
        def _():
            dq_s[...] = jnp.zeros_like(dq_s)

        def step(diagonal):
            for hh in range(HB):
                q = q_ref[:, hh * QW:(hh + 1) * QW]
                k = k_ref[:, hh * QW:(hh + 1) * QW]
                dob = do_ref[:, hh * VW:(hh + 1) * VW]
                p = jnp.exp(_scores(q, k, diagonal, t) - lse_ref[hh][:, :1])
                delta = jnp.sum(dob.astype(F32) * o_ref[:, hh * VW:(hh + 1) * VW], axis=1, keepdims=True)
                dp = lax.dot_general(dob, v_ref[:, hh * VW:(hh + 1) * VW], nt, preferred_element_type=F32)
                dsb = (p * (dp - delta)).astype(BF16)
                dv_ref[rows, hh * VW:(hh + 1) * VW] += lax.dot_general(p.astype(BF16), dob, tn, preferred_element_type=F32)
                dk_ref[rows, hh * QW:(hh + 1) * QW] += lax.dot_general(dsb, q, tn, preferred_element_type=F32)
                dq = dq_s[:, hh * QW:(hh + 1) * QW] + jnp.dot(dsb, k, preferred_element_type=F32)
                if diagonal:
                    dq_ref[:, hh * QW:(hh + 1) * QW] = dq
                else:
                    dq_s[:, hh * QW:(hh + 1) * QW] = dq

        @pl.when(i != j)
        def _():
            step(False)

        @pl.when(i == j)
        def _():
            step(True)

    qside = lambda g, p, qi, kj: (qi[p], g)
    kside = lambda g, p, qi, kj: (kj[p], g)
    whole = lambda g, p, qi, kj: (0, g)
    return pl.pallas_call(
        body, name=name,
        out_shape=(jax.ShapeDtypeStruct((S, H * QW), F32), jax.ShapeDtypeStruct((S, H * QW), F32),
                   jax.ShapeDtypeStruct((S, H * VW), F32)),
        grid_spec=pltpu.PrefetchScalarGridSpec(
            num_scalar_prefetch=2, grid=(H // HB, qi.shape[0]),
            in_specs=[pl.BlockSpec((t, HB * QW), qside), pl.BlockSpec((t, HB * QW), kside), pl.BlockSpec((t, HB * VW), kside),
                      pl.BlockSpec((t, HB * VW), qside), pl.BlockSpec((t, HB * VW), qside),
                      pl.BlockSpec((HB, t, LANES), lambda g, p, qi, kj: (g, qi[p], 0))],
            out_specs=(pl.BlockSpec((t, HB * QW), qside), pl.BlockSpec((S, HB * QW), whole), pl.BlockSpec((S, HB * VW), whole)),
            scratch_shapes=[pltpu.VMEM((t, HB * QW), F32)]),
        compiler_params=_params("parallel", "arbitrary"),
    )(qi, kj, qp, kp, v, o, do, lse)


def _swa_kv_halves(blk, hf, lo):
    if hf == 0:
        a = jnp.where(lo, blk, 0.0)
        b = pltpu.roll(a, 64, 1)
    else:
        b = jnp.where(lo, 0.0, blk)
        a = pltpu.roll(b, 64, 1)
    return a.astype(BF16), b.astype(BF16)


def _swa_softmax(qs, kx, bias, neg0, sk):
    s = lax.dot_general(qs, kx, (((1,), (1,)), ((), ())), preferred_element_type=F32) + bias + neg0
    m = jnp.maximum(jnp.max(s, axis=1, keepdims=True), sk)
    e = jnp.exp(s - m)
    es = jnp.exp(sk - m)
    inv = 1.0 / (jnp.sum(e, axis=1, keepdims=True) + es)
    return e * inv, es * inv


def _swa_stack(ref, kvh, npb, scale=None):
    parts = [ref[:, (kvh * npb + pb) * LANES:(kvh * npb + pb + 1) * LANES] for pb in range(npb)]
    x = jnp.concatenate(parts, axis=0)
    return x if scale is None else x * scale


def _swa_sink_col(sink_ref, kvh, e, npb):
    row = lax.broadcasted_iota(jnp.int32, (npb * SWA_BLOCK, 1), 0)
    col = jnp.zeros((npb * SWA_BLOCK, 1), F32) + sink_ref[2 * (kvh * npb) + e]
    for pb in range(1, npb):
        col = jnp.where(row >= pb * SWA_BLOCK, sink_ref[2 * (kvh * npb + pb) + e], col)
    return col


def _swa_fwd(z_swa, bias_st, sinks, *, name):
    S, W = z_swa.shape
    npb = bias_st.shape[1] // SWA_BLOCK
    NH = 2 * SWA_KVH * npb
    QW = NH * SWA_HD
    KW = SWA_KVH * SWA_HD
    nb = S // SWA_BLOCK
    B = SWA_BLOCK
    assert SWA_KVH % 2 == 0 and W == QW + 2 * KW

    def body(sink_ref, q_ref, kvc_ref, kvp_ref, b_ref, o_ref):
        n = pl.program_id(0)
        lo = lax.broadcasted_iota(jnp.int32, (2 * B, LANES), 1) < 64
        col = lax.broadcasted_iota(jnp.int32, (npb * B, 2 * B), 1)
        neg0 = jnp.where(jnp.logical_and(col < B, n == 0), NEG, 0.0)
        for kb in range(SWA_KVH // 2):
            kblk = jnp.concatenate([kvp_ref[:, kb * LANES:(kb + 1) * LANES], kvc_ref[:, kb * LANES:(kb + 1) * LANES]], axis=0)
            vblk = jnp.concatenate([kvp_ref[:, KW + kb * LANES:KW + (kb + 1) * LANES],
                                    kvc_ref[:, KW + kb * LANES:KW + (kb + 1) * LANES]], axis=0)
            for hf in range(2):
                kvh = 2 * kb + hf
                ks = _swa_kv_halves(kblk, hf, lo)
                vs = _swa_kv_halves(vblk, hf, lo)
                qs = _swa_stack(q_ref, kvh, npb, SWA_HD ** -0.5).astype(BF16)
                acc = jnp.zeros((npb * B, LANES), F32)
                for e in range(2):
                    p, _ = _swa_softmax(qs, ks[e], b_ref[2 * kvh + e], neg0, _swa_sink_col(sink_ref, kvh, e, npb))
                    acc = acc + jnp.dot(p.astype(BF16), vs[e], preferred_element_type=F32)
                for pb in range(npb):
                    P = kvh * npb + pb
                    o_ref[:, P * LANES:(P + 1) * LANES] = acc[pb * B:(pb + 1) * B]

    kvcol = QW // (2 * KW)
    assert QW % (2 * KW) == 0
    return pl.pallas_call(
        body, name=name,
        out_shape=jax.ShapeDtypeStruct((S, QW), F32), grid=(nb,),
        in_specs=[SMEM_FULL, pl.BlockSpec((B, QW), lambda n: (n, 0)), pl.BlockSpec((B, 2 * KW), lambda n: (n, kvcol)),
                  pl.BlockSpec((B, 2 * KW), lambda n: (jnp.maximum(n - 1, 0), kvcol)),
                  pl.BlockSpec(bias_st.shape, lambda n: (0, 0, 0))],
        out_specs=pl.BlockSpec((B, QW), lambda n: (n, 0)),
        compiler_params=_params("parallel"),
    )(sinks, z_swa, z_swa, z_swa, bias_st)


def _swa_bwd(z_swa, bias_st, sinks, o, do, *, name):
    S, W = z_swa.shape
    npb = bias_st.shape[1] // SWA_BLOCK
    NH = 2 * SWA_KVH * npb
    QW = NH * SWA_HD
    KW = SWA_KVH * SWA_HD
    nb = S // SWA_BLOCK
    B = SWA_BLOCK
    scale = SWA_HD ** -0.5
    tn = (((0,), (0,)), ((), ()))
    nt = (((1,), (1,)), ((), ()))

    def fold(x, hf, lo):
        x = x + pltpu.roll(x, 64, 1)
        return jnp.where(lo, x, 0.0) if hf == 0 else jnp.where(lo, 0.0, x)

    def body(sink_ref, q_ref, kvc_ref, kvp_ref, b_ref, o_ref, do_ref, dz_ref, dbias_ref, dsink_ref,
             cq_s, ck_s, cv_s, nq_s, nk_s, nv_s, pk_s, pv_s):
        n = pl.program_id(0)

        @pl.when(n == 0)
        def _():
            dbias_ref[...] = jnp.zeros_like(dbias_ref)
            dsink_ref[...] = jnp.zeros_like(dsink_ref)
            cq_s[...] = jnp.zeros_like(cq_s)
            ck_s[...] = jnp.zeros_like(ck_s)
            cv_s[...] = jnp.zeros_like(cv_s)

        @pl.when(n == nb)
        def _():
            pk_s[...] = jnp.zeros_like(pk_s)
            pv_s[...] = jnp.zeros_like(pv_s)

        @pl.when(n < nb)
        def _():
            lo = lax.broadcasted_iota(jnp.int32, (2 * B, LANES), 1) < 64
            lo1 = lax.broadcasted_iota(jnp.int32, (npb * B, LANES), 1) < 64
            lane1 = lax.broadcasted_iota(jnp.int32, (1, LANES), 1)
            col = lax.broadcasted_iota(jnp.int32, (npb * B, 2 * B), 1)
            neg0 = jnp.where(jnp.logical_and(col < B, n == 0), NEG, 0.0)
            dsink = jnp.zeros((1, LANES), F32)
            for kb in range(SWA_KVH // 2):
                kblk = jnp.concatenate([kvp_ref[:, kb * LANES:(kb + 1) * LANES], kvc_ref[:, kb * LANES:(kb + 1) * LANES]], axis=0)
                vblk = jnp.concatenate([kvp_ref[:, KW + kb * LANES:KW + (kb + 1) * LANES],
                                        kvc_ref[:, KW + kb * LANES:KW + (kb + 1) * LANES]], axis=0)
                dkblk = jnp.zeros((2 * B, LANES), F32)
                dvblk = jnp.zeros((2 * B, LANES), F32)
                for hf in range(2):
                    kvh = 2 * kb + hf
                    ks = _swa_kv_halves(kblk, hf, lo)
                    vs = _swa_kv_halves(vblk, hf, lo)
                    qs = _swa_stack(q_ref, kvh, npb, scale).astype(BF16)
                    dos = _swa_stack(do_ref, kvh, npb)
                    prod = dos * _swa_stack(o_ref, kvh, npb)
                    dob = dos.astype(BF16)
                    dkj = jnp.zeros((2 * B, LANES), F32)
                    dvj = jnp.zeros((2 * B, LANES), F32)
                    dqs = jnp.zeros((npb * B, LANES), F32)
                    for e in range(2):
                        keep = lo1 if e == 0 else jnp.logical_not(lo1)
                        p, psink = _swa_softmax(qs, ks[e], b_ref[2 * kvh + e], neg0, _swa_sink_col(sink_ref, kvh, e, npb))
                        delta = jnp.sum(jnp.where(keep, prod, 0.0), axis=1, keepdims=True)
                        dp = lax.dot_general(dob, vs[e], nt, preferred_element_type=F32)
                        ds = p * (dp - delta)
                        dbias_ref[2 * kvh + e] += ds
                        pd = psink * delta
                        for pb in range(npb):
                            dsh = -jnp.sum(pd[pb * B:(pb + 1) * B], axis=0, keepdims=True)
                            dsink = dsink + jnp.where(lane1 == 2 * (kvh * npb + pb) + e, dsh, 0.0)
                        dsb = ds.astype(BF16)
                        dqs = dqs + jnp.dot(dsb, ks[e], preferred_element_type=F32)
                        keep2 = lo if e == 0 else jnp.logical_not(lo)
                        dkj = dkj + jnp.where(keep2, lax.dot_general(dsb, qs, tn, preferred_element_type=F32), 0.0)
                        dvj = dvj + jnp.where(keep2, lax.dot_general(p.astype(BF16), dob, tn, preferred_element_type=F32), 0.0)
                    for pb in range(npb):
                        P = kvh * npb + pb
                        nq_s[:, P * LANES:(P + 1) * LANES] = dqs[pb * B:(pb + 1) * B] * scale
                    dkblk = dkblk + fold(dkj, hf, lo)
                    dvblk = dvblk + fold(dvj, hf, lo)
                pk_s[:, kb * LANES:(kb + 1) * LANES] = dkblk[:B]
                nk_s[:, kb * LANES:(kb + 1) * LANES] = dkblk[B:]
                pv_s[:, kb * LANES:(kb + 1) * LANES] = dvblk[:B]
                nv_s[:, kb * LANES:(kb + 1) * LANES] = dvblk[B:]
            dsink_ref[...] += dsink

        dz_ref[:, :QW] = cq_s[...].astype(BF16)
        dz_ref[:, QW:QW + KW] = (ck_s[...] + pk_s[...]).astype(BF16)
        dz_ref[:, QW + KW:] = (cv_s[...] + pv_s[...]).astype(BF16)

        @pl.when(n < nb)
        def _():
            cq_s[...] = nq_s[...]
            ck_s[...] = nk_s[...]
            cv_s[...] = nv_s[...]

    kvcol = QW // (2 * KW)
    cur = lambda n: (jnp.minimum(n, nb - 1), 0)
    return pl.pallas_call(
        body, name=name,
        out_shape=(jax.ShapeDtypeStruct((S, W), BF16), jax.ShapeDtypeStruct(bias_st.shape, F32),
                   jax.ShapeDtypeStruct((1, LANES), F32)),
        grid=(nb + 1,),
        in_specs=[SMEM_FULL, pl.BlockSpec((B, QW), cur), pl.BlockSpec((B, 2 * KW), lambda n: (jnp.minimum(n, nb - 1), kvcol)),
                  pl.BlockSpec((B, 2 * KW), lambda n: (jnp.maximum(jnp.minimum(n, nb - 1) - 1, 0), kvcol)),
                  pl.BlockSpec(bias_st.shape, lambda n: (0, 0, 0)), pl.BlockSpec((B, QW), cur), pl.BlockSpec((B, QW), cur)],
        out_specs=(pl.BlockSpec((B, W), lambda n: (jnp.maximum(n - 1, 0), 0)),
                   pl.BlockSpec(bias_st.shape, lambda n: (0, 0, 0)), pl.BlockSpec((1, LANES), lambda n: (0, 0))),
        scratch_shapes=[pltpu.VMEM((B, QW), F32), pltpu.VMEM((B, KW), F32), pltpu.VMEM((B, KW), F32),
                        pltpu.VMEM((B, QW), F32), pltpu.VMEM((B, KW), F32), pltpu.VMEM((B, KW), F32),
                        pltpu.VMEM((B, KW), F32), pltpu.VMEM((B, KW), F32)],
        compiler_params=_params("arbitrary"),
    )(sinks, z_swa, z_swa, z_swa, bias_st, o, do)


def _gate_fwd(zg, o_a, o_b, *, name):
    S, D = o_a.shape
    tr = min(S, 512)

    def body(z_ref, a_ref, b_ref, m_ref):
        ga = jax.nn.sigmoid(z_ref[:, :PAIR])
        gb = jax.nn.sigmoid(z_ref[:, PAIR:])
        m_ref[...] = (ga * a_ref[...] + gb * b_ref[...]).astype(BF16)

    col = pl.BlockSpec((tr, PAIR), lambda i, j: (i, j))
    return pl.pallas_call(
        body, name=name, out_shape=jax.ShapeDtypeStruct((S, D), BF16), grid=(S // tr, D // PAIR),
        in_specs=[pl.BlockSpec((tr, 2 * PAIR), lambda i, j: (i, j)), col, col], out_specs=col,
        compiler_params=_params("parallel", "parallel"),
    )(zg, o_a, o_b)


def _gate_bwd(dmix, zg, o_a, o_b, *, name):
    S, D = o_a.shape
    tr = min(S, 512)

    def body(d_ref, z_ref, a_ref, b_ref, da_ref, db_ref, dz_ref):
        d = d_ref[...]
        ga = jax.nn.sigmoid(z_ref[:, :PAIR])
        gb = jax.nn.sigmoid(z_ref[:, PAIR:])
        da_ref[...] = (d * ga).astype(BF16)
        db_ref[...] = d * gb
        dz_ref[:, :PAIR] = (d * a_ref[...] * (ga * (1.0 - ga))).astype(BF16)
        dz_ref[:, PAIR:] = (d * b_ref[...] * (gb * (1.0 - gb))).astype(BF16)

    col = pl.BlockSpec((tr, PAIR), lambda i, j: (i, j))
    wide = pl.BlockSpec((tr, 2 * PAIR), lambda i, j: (i, j))
    return pl.pallas_call(
        body, name=name,
        out_shape=(jax.ShapeDtypeStruct((S, D), BF16), jax.ShapeDtypeStruct((S, D), F32), jax.ShapeDtypeStruct((S, 2 * D), BF16)),
        grid=(S // tr, D // PAIR), in_specs=[col, wide, col, col], out_specs=(col, col, wide),
        compiler_params=_params("parallel", "parallel"),
    )(dmix, zg, o_a, o_b)


def _conv_u(t_ref, prev_ref, w_ref, b_ref, m, i):
    cur = t_ref[m].astype(F32)
    live = (i > 0).astype(F32)
    p6 = prev_ref[m, 14:15, :].astype(F32) * live
    p7 = prev_ref[m, 15:16, :].astype(F32) * live
    row = lax.broadcasted_iota(jnp.int32, cur.shape, 0)
    t1 = jnp.where(row == 0, p7, pltpu.roll(cur, 1, 0))
    t2 = jnp.where(row == 0, p6, jnp.where(row == 1, p7, pltpu.roll(cur, 2, 0)))
    u = ((b_ref[m] + w_ref[m, 0:1, :] * t2) + w_ref[m, 1:2, :] * t1) + w_ref[m, 2:3, :] * cur
    return u, cur, t1, t2


def _conv_specs(tr, tc):
    blk = pl.BlockSpec((2, tr, tc), lambda p, j, i: (p, i, j))
    prev = pl.BlockSpec((2, 16, tc), lambda p, j, i: (p, jnp.maximum(i * (tr // 16) - 1, 0), j))
    w3 = pl.BlockSpec((2, 3, tc), lambda p, j, i: (p, 0, j))
    w1 = pl.BlockSpec((2, 1, tc), lambda p, j, i: (p, 0, j))
    return blk, prev, w3, w1


def _conv_gate_fwd(t, cw, cb, *, name):
    _, S, C = t.shape
    tr, tc = min(S, 512), _tile(C, 1536)
    ncol = C // tc
    blk, prev, w3, w1 = _conv_specs(tr, tc)

    def body(t_ref, prev_ref, w_ref, b_ref, a_ref):
        i = pl.program_id(2)
        u1 = _conv_u(t_ref, prev_ref, w_ref, b_ref, 0, i)[0]
        u2 = _conv_u(t_ref, prev_ref, w_ref, b_ref, 1, i)[0]
        a_ref[...] = (jax.nn.silu(u1) * u2).astype(BF16)

    return pl.pallas_call(
        body, name=name, out_shape=jax.ShapeDtypeStruct((S, 2 * C), BF16), grid=(2, ncol, S // tr),
        in_specs=[blk, prev, w3, w1], out_specs=pl.BlockSpec((tr, tc), lambda p, j, i: (i, p * ncol + j)),
        compiler_params=_params("parallel", "parallel", "parallel"),
    )(t, t, cw, cb)


def _conv_gate_bwd(t, da, cw, cb, *, name):
    _, S, C = t.shape
    tr, tc = min(S, 256), _tile(C, 1536)
    ncol = C // tc
    blk, prev, w3, w1 = _conv_specs(tr, tc)

    def body(t_ref, prev_ref, da_ref, w_ref, b_ref, du_ref, dw_ref, db_ref):
        i = pl.program_id(2)
        first = i == 0
        u1, c1, a1, b1 = _conv_u(t_ref, prev_ref, w_ref, b_ref, 0, i)
        u2, c2, a2, b2 = _conv_u(t_ref, prev_ref, w_ref, b_ref, 1, i)
        d = da_ref[...].astype(F32)
        sg = jax.nn.sigmoid(u1)
        du1 = d * u2 * (sg * (1.0 + u1 * (1.0 - sg)))
        du2 = d * (u1 * sg)
        for m, (du, cur, t1, t2) in enumerate(((du1, c1, a1, b1), (du2, c2, a2, b2))):
            du_ref[m] = du.astype(BF16)
            dw = jnp.concatenate([jnp.sum(du * t2, axis=0, keepdims=True), jnp.sum(du * t1, axis=0, keepdims=True),
                                  jnp.sum(du * cur, axis=0, keepdims=True)], axis=0)
            db = jnp.sum(du, axis=0, keepdims=True)

            @pl.when(first)
            def _():
                dw_ref[m] = dw
                db_ref[m] = db

            @pl.when(jnp.logical_not(first))
            def _():
                dw_ref[m] += dw
                db_ref[m] += db

    return pl.pallas_call(
        body, name=name,
        out_shape=(jax.ShapeDtypeStruct(t.shape, BF16), jax.ShapeDtypeStruct(cw.shape, F32), jax.ShapeDtypeStruct(cb.shape, F32)),
        grid=(2, ncol, S // tr),
        in_specs=[blk, prev, pl.BlockSpec((tr, tc), lambda p, j, i: (i, p * ncol + j)), w3, w1], out_specs=(blk, w3, w1),
        compiler_params=_params("parallel", "parallel", "arbitrary"),
    )(t, t, da, cw, cb)


def _conv_bwd_dt(du, cw, *, name):
    _, S, C = du.shape
    tr, tc = min(S, 512), _tile(C, 1536)
    nrow = S // tr
    blk, _, w3, _ = _conv_specs(tr, tc)
    nxt = pl.BlockSpec((2, 16, tc), lambda p, j, i: (p, jnp.minimum((i + 1) * (tr // 16), S // 16 - 1), j))

    def body(d_ref, next_ref, w_ref, dt_ref):
        i = pl.program_id(2)
        live = (i < nrow - 1).astype(F32)
        for m in range(2):
            cur = d_ref[m].astype(F32)
            n0 = next_ref[m, 0:1, :].astype(F32) * live
            n1 = next_ref[m, 1:2, :].astype(F32) * live
            row = lax.broadcasted_iota(jnp.int32, cur.shape, 0)
            d1 = jnp.where(row == tr - 1, n0, pltpu.roll(cur, tr - 1, 0))
            d2 = jnp.where(row == tr - 1, n1, jnp.where(row == tr - 2, n0, pltpu.roll(cur, tr - 2, 0)))
            dt_ref[m] = ((w_ref[m, 2:3, :] * cur + w_ref[m, 1:2, :] * d1) + w_ref[m, 0:1, :] * d2).astype(BF16)

    return pl.pallas_call(
        body, name=name, out_shape=jax.ShapeDtypeStruct(du.shape, BF16), grid=(2, C // tc, nrow),
        in_specs=[blk, nxt, w3], out_specs=blk, compiler_params=_params("parallel", "parallel", "parallel"),
    )(du, du, cw)


def _ada_fwd(c_all, w, b, *, name):
    Bn, D = c_all.shape
    N = w.shape[1]
    tn = _tile(N, 512)

    def body(c_ref, w_ref, b_ref, o_ref):
        o_ref[...] = jnp.dot(jax.nn.silu(c_ref[...]), w_ref[...], preferred_element_type=F32, precision=HIGHEST) + b_ref[...]

    return pl.pallas_call(
        body, name=name, out_shape=jax.ShapeDtypeStruct((Bn, N), F32), grid=(N // tn,),
        in_specs=[pl.BlockSpec((Bn, D), lambda j: (0, 0)), pl.BlockSpec((D, tn), lambda j: (0, j)),
                  pl.BlockSpec((1, tn), lambda j: (0, j))],
        out_specs=pl.BlockSpec((Bn, tn), lambda j: (0, j)), compiler_params=_params("parallel"),
    )(c_all, w, b)


def _ada_bwd(c_all_t, dmod, *, name):
    D, Bn = c_all_t.shape
    N = dmod.shape[1]
    tm = _tile(D, 512, 8)
    tn = _tile(N, 1536)

    def body(c_ref, d_ref, o_ref):
        o_ref[...] = jnp.dot(jax.nn.silu(c_ref[...]), d_ref[...], preferred_element_type=F32, precision=HIGHEST)

    return pl.pallas_call(
        body, name=name, out_shape=jax.ShapeDtypeStruct((D, N), F32), grid=(D // tm, N // tn),
        in_specs=[pl.BlockSpec((tm, Bn), lambda i, j: (i, 0)), pl.BlockSpec((Bn, tn), lambda i, j: (0, j))],
        out_specs=pl.BlockSpec((tm, tn), lambda i, j: (i, j)), compiler_params=_params("parallel", "parallel"),
    )(c_all_t, dmod)


def _adamw(w, g, m, v, *, name):
    R, C = w.shape
    tr = R if R * C <= (1 << 18) else _tile(R, max(8, (1 << 18) // C), 8)

    def body(w_ref, g_ref, m_ref, v_ref, d_ref, nm_ref, nv_ref):
        gv = g_ref[...]
        nm = ADAM_B1 * m_ref[...] + (1.0 - ADAM_B1) * gv
        nv = ADAM_B2 * v_ref[...] + (1.0 - ADAM_B2) * (gv * gv)
        m_hat = nm / (1.0 - ADAM_B1 ** ADAM_STEP)
        v_hat = nv / (1.0 - ADAM_B2 ** ADAM_STEP)
        d_ref[...] = -ADAM_LR * (m_hat / (jnp.sqrt(v_hat) + ADAM_EPS) + ADAM_WD * w_ref[...])
        nm_ref[...] = nm
        nv_ref[...] = nv

    blk = pl.BlockSpec((tr, C), lambda i: (i, 0))
    shp = jax.ShapeDtypeStruct((R, C), F32)
    return pl.pallas_call(
        body, name=name, out_shape=(shp, shp, shp), grid=(R // tr,), in_specs=[blk] * 4, out_specs=(blk,) * 3,
        compiler_params=_params("parallel"),
    )(w, g, m, v)


def _place():
    x, y, c = lax.axis_index("x"), lax.axis_index("y"), lax.axis_index("c")
    return x, y, c, [(1 - x, y), (x, 1 - y), (1 - x, 1 - y)]


def _remote(src, dst, send_sem, recv_sem, dev):
    return pltpu.make_async_remote_copy(src_ref=src, dst_ref=dst, send_sem=send_sem, recv_sem=recv_sem,
                                        device_id=dev, device_id_type=MESH)


def _allgather8(v, *, tie=None, name):
    R, C = v.shape

    def body(v_ref, out_ref, send_sems, recv_sems, local_sem):
        x, y, c, chips = _place()
        me, sibling = (x, y, c), (x, y, 1 - c)

        def rows(px, py, pc):
            return out_ref.at[pl.ds((4 * px + 2 * py + pc) * R, R), :]

        def copy(k, block, to, src=None):
            return _remote(rows(*block) if src is None else src, rows(*block), send_sems.at[k], recv_sems.at[k], to)

        mine = pltpu.make_async_copy(v_ref, rows(*me), local_sem)
        mine.start()
        first = [copy(0, me, sibling, src=v_ref)]
        first += [copy(1 + j, me, (*chip, c), src=v_ref) for j, chip in enumerate(chips)]
        for cp in first:
            cp.start()
        passed = [copy(4 + j, (*chip, c), sibling) for j, chip in enumerate(chips)]
        for j, chip in enumerate(chips):
            copy(1 + j, (*chip, c), me).wait_recv()
            passed[j].start()
        copy(0, sibling, me).wait_recv()
        for j, chip in enumerate(chips):
            copy(4 + j, (*chip, 1 - c), me).wait_recv()
        for cp in first + passed:
            cp.wait_send()
        mine.wait()

    body, tspec, targ = _tied(body, tie)
    out = pl.pallas_call(
        body, name=name, out_shape=jax.ShapeDtypeStruct((N_DEV * R, C), v.dtype),
        in_specs=tspec + [VMEM_FULL], out_specs=VMEM_FULL,
        scratch_shapes=[pltpu.SemaphoreType.DMA((7,)), pltpu.SemaphoreType.DMA((7,)), pltpu.SemaphoreType.DMA],
    )(*targ, v)
    return out.reshape(N_DEV, R, C)


SEM = pl.BlockSpec(memory_space=pltpu.SEMAPHORE)
HBM = pl.BlockSpec(memory_space=pltpu.HBM)
EFFECT = pltpu.SideEffectType.DATAFLOW_SIDE_EFFECTING
DMA_SEM = pltpu.SemaphoreType.DMA(())


def _in_hbm(a):
    return pltpu.with_memory_space_constraint(a, pltpu.HBM)


def _three_halves(land, r2):
    return land.at[pl.ds(0, N_CHIP - 1), pl.ds(0, r2)]


def _slot(chip, swap):
    return (chip % 2) * 2 + chip // 2 if swap else chip


def _gather_start(ws, after, swaps, *, name):
    n = len(ws)
    na = len(after)
    lands = [lax.empty((N_CHIP,) + w.shape, w.dtype) for w in ws]

    def body(*refs):
        w_refs, land_refs = refs[:n], refs[n:2 * n]
        send, recv = refs[2 * n + na:3 * n + na], refs[3 * n + na:4 * n + na]
        token = refs[6 * n + na]
        x, y, c, chips = _place()
        k = 2 * x + y
        for i in range(n):
            r2 = ws[i].shape[0] // 2
            for cx, cy in chips:
                _remote(w_refs[i].at[pl.ds(c * r2, r2)], land_refs[i].at[_slot(k, swaps[i]), pl.ds(c * r2, r2)], send[i], recv[i],
                        (cx, cy, c)).start()
        token[...] = jnp.zeros_like(token)

    outs = pl.pallas_call(
        body, name=name,
        out_shape=[DMA_SEM] * (2 * n) + [pltpu.HBM(w.shape, w.dtype) for w in ws] + [pltpu.HBM(l.shape, l.dtype) for l in lands]
        + [jax.ShapeDtypeStruct((8, LANES), F32)],
        in_specs=[HBM] * (2 * n) + [ANY] * na, out_specs=[SEM] * (2 * n) + [HBM] * (2 * n) + [VMEM_FULL],
        input_output_aliases={i: 2 * n + i for i in range(2 * n)},
        compiler_params=pltpu.CompilerParams(has_side_effects=EFFECT),
    )(*[_in_hbm(w) for w in ws], *[_in_hbm(l) for l in lands], *after)
    return outs[:n], outs[n:2 * n], outs[2 * n:3 * n], outs[3 * n:4 * n], outs[4 * n]


def _gather_forward(send, recv, ws, lands, after, swaps, *, name):
    n = len(ws)

    def body(*refs):
        w_refs, land_refs = refs[:n], refs[n:2 * n]
        send1, recv1 = refs[2 * n:3 * n], refs[3 * n:4 * n]
        send2, recv2 = refs[4 * n + 1 + 2 * n:4 * n + 1 + 3 * n], refs[4 * n + 1 + 3 * n:4 * n + 1 + 4 * n]
        x, y, c, chips = _place()
        sibling = (x, y, 1 - c)
        for i in range(n):
            r2 = ws[i].shape[0] // 2
            win = _three_halves(land_refs[i], r2)
            done = _remote(win, win, send1[i], recv1[i], sibling)
            done.wait_send()
            done.wait_recv()
            for cx, cy in chips:
                got = land_refs[i].at[_slot(2 * cx + cy, swaps[i]), pl.ds(c * r2, r2)]
                _remote(got, got, send2[i], recv2[i], sibling).start()
        token = refs[8 * n + 1]
        token[...] = jnp.zeros_like(token)

    outs = pl.pallas_call(
        body, name=name,
        out_shape=[pltpu.HBM(w.shape, w.dtype) for w in ws] + [pltpu.HBM(l.shape, l.dtype) for l in lands] + [DMA_SEM] * (2 * n)
        + [jax.ShapeDtypeStruct((8, LANES), F32)],
        in_specs=[HBM] * (2 * n) + [SEM] * (2 * n) + [ANY], out_specs=[HBM] * (2 * n) + [SEM] * (2 * n) + [VMEM_FULL],
        input_output_aliases={i: i for i in range(2 * n)},
        compiler_params=pltpu.CompilerParams(has_side_effects=EFFECT),
    )(*ws, *lands, *send, *recv, after)
    return outs[2 * n:3 * n], outs[3 * n:4 * n], outs[n:2 * n], outs[4 * n]


def _gather_finish(send, recv, lands, after, *, name):
    n = len(lands)

    def body(*refs):
        land_refs = refs[:n]
        send2, recv2 = refs[n:2 * n], refs[2 * n:3 * n]
        x, y, c, _ = _place()
        for i in range(n):
            win = _three_halves(land_refs[i], lands[i].shape[1] // 2)
            done = _remote(win, win, send2[i], recv2[i], (x, y, 1 - c))
            done.wait_send()
            done.wait_recv()

    return pl.pallas_call(
        body, name=name,
        out_shape=[pltpu.HBM(l.shape, l.dtype) for l in lands],
        in_specs=[HBM] * n + [SEM] * (2 * n) + [ANY], out_specs=[HBM] * n,
        input_output_aliases={i: i for i in range(n)},
        compiler_params=pltpu.CompilerParams(has_side_effects=EFFECT),
    )(*lands, *send, *recv, after)


def _scatter_start(gs, swaps, *, name):
    n = len(gs)
    lands = [lax.empty((N_DEV, g.shape[1] // 2, g.shape[2]), g.dtype) for g in gs]

    def body(*refs):
        g_refs, land_refs = refs[:n], refs[n:2 * n]
        send, recv = refs[2 * n:3 * n], refs[3 * n:4 * n]
        token = refs[6 * n]
        x, y, c, chips = _place()
        k = 2 * x + y
        me = 2 * k + c
        for i in range(n):
            r2 = gs[i].shape[1] // 2
            for cx, cy in chips:
                for cc in range(2):
                    _remote(g_refs[i].at[_slot(2 * cx + cy, swaps[i]), pl.ds(cc * r2, r2)], land_refs[i].at[me], send[i], recv[i],
                            (cx, cy, cc)).start()
            _remote(g_refs[i].at[_slot(k, swaps[i]), pl.ds((1 - c) * r2, r2)], land_refs[i].at[me], send[i], recv[i],
                    (x, y, 1 - c)).start()
        token[...] = jnp.zeros_like(token)

    outs = pl.pallas_call(
        body, name=name,
        out_shape=[DMA_SEM] * (2 * n) + [pltpu.HBM(g.shape, g.dtype) for g in gs] + [pltpu.HBM(l.shape, l.dtype) for l in lands]
        + [jax.ShapeDtypeStruct((8, LANES), F32)],
        in_specs=[HBM] * (2 * n), out_specs=[SEM] * (2 * n) + [HBM] * (2 * n) + [VMEM_FULL],
        input_output_aliases={i: 2 * n + i for i in range(2 * n)},
        compiler_params=pltpu.CompilerParams(has_side_effects=EFFECT),
    )(*[_in_hbm(g) for g in gs], *[_in_hbm(l) for l in lands])
    return outs[:n], outs[n:2 * n], outs[2 * n:3 * n], outs[3 * n:4 * n], outs[4 * n]


def _scatter_wait(send, recv, gs, lands, after, *, name):
    n = len(gs)

    def body(*refs):
        land_refs = refs[n:2 * n]
        send1, recv1 = refs[2 * n:3 * n], refs[3 * n:4 * n]
        x, y, c, _ = _place()
        for i in range(n):
            win = land_refs[i].at[pl.ds(0, N_DEV - 1)]
            done = _remote(win, win, send1[i], recv1[i], (x, y, 1 - c))
            done.wait_send()
            done.wait_recv()

    outs = pl.pallas_call(
        body, name=name,
        out_shape=[pltpu.HBM(g.shape, g.dtype) for g in gs] + [pltpu.HBM(l.shape, l.dtype) for l in lands],
        in_specs=[HBM] * (2 * n) + [SEM] * (2 * n) + [ANY], out_specs=[HBM] * (2 * n),
        input_output_aliases={i: i for i in range(2 * n)},
        compiler_params=pltpu.CompilerParams(has_side_effects=EFFECT),
    )(*gs, *lands, *send, *recv, after)
    return outs[:n], outs[n:]


def _share_halves(ts, *, name):
    n = len(ts)

    def body(*refs):
        outs = refs[n:2 * n]
        send_sems, recv_sems = refs[2 * n:]
        x, y, c, _ = _place()
        sibling = (x, y, 1 - c)
        cps = []
        for i in range(n):
            r2 = ts[i].shape[0] // 2
            mine = outs[i].at[pl.ds(c * r2, r2)]
            cps.append(_remote(mine, mine, send_sems.at[i], recv_sems.at[i], sibling))
            cps[-1].start()
        for i in range(n):
            r2 = ts[i].shape[0] // 2
            got = outs[i].at[pl.ds((1 - c) * r2, r2)]
            _remote(got, got, send_sems.at[i], recv_sems.at[i], sibling).wait_recv()
        for cp in cps:
            cp.wait_send()

    return pl.pallas_call(
        body, name=name,
        out_shape=[jax.ShapeDtypeStruct(t.shape, t.dtype) for t in ts],
        in_specs=[ANY] * n, out_specs=[ANY] * n, input_output_aliases={i: i for i in range(n)},
        scratch_shapes=[pltpu.SemaphoreType.DMA((n,)), pltpu.SemaphoreType.DMA((n,))],
    )(*ts)


def _sum_pieces(land, g, idx, *, name):
    _, r2, C = land.shape
    tr = _tile(r2, max(16, (1 << 19) // C), 16)
    nr = r2 // tr

    def body(idx_ref, land_ref, own_ref, o_ref, acc_ref):
        d = pl.program_id(1)
        mine = d == idx_ref[0]

        @pl.when(d == 0)
        def _():
            acc_ref[...] = jnp.zeros_like(acc_ref)

        @pl.when(mine)
        def _():
            acc_ref[...] += own_ref[...].astype(F32)

        @pl.when(jnp.logical_not(mine))
        def _():
            acc_ref[...] += land_ref[...].astype(F32)

        @pl.when(d == N_DEV - 1)
        def _():
            o_ref[...] = acc_ref[...]

    return pl.pallas_call(
        body, name=name, out_shape=jax.ShapeDtypeStruct((2 * r2, C), F32),
        grid_spec=pltpu.PrefetchScalarGridSpec(
            num_scalar_prefetch=1, grid=(nr, N_DEV),
            in_specs=[pl.BlockSpec((None, tr, C), lambda i, d, ix: (jnp.where(d == ix[0], (d + 1) % N_DEV, d), i, 0)),
                      pl.BlockSpec((None, tr, C), lambda i, d, ix: (ix[1], ix[2] * nr + i, 0))],
            out_specs=pl.BlockSpec((tr, C), lambda i, d, ix: (ix[2] * nr + i, 0)),
            scratch_shapes=[pltpu.VMEM((tr, C), F32)]),
        compiler_params=_params("parallel", "arbitrary"),
    )(idx, land, g)


def _sum_devices(v, *, name):
    n, R, C = v.shape

    def body(v_ref, o_ref):
        acc = v_ref[0]
        for j in range(1, n):
            acc = acc + v_ref[j]
        o_ref[...] = acc

    return pl.pallas_call(body, name=name, out_shape=jax.ShapeDtypeStruct((R, C), F32),
                          in_specs=[VMEM_FULL], out_specs=VMEM_FULL)(v)


def _pair(a, b):
    n = a.shape[1]
    parts = []
    for j in range(n // PAIR):
        parts += [a[:, j * PAIR:(j + 1) * PAIR], b[:, j * PAIR:(j + 1) * PAIR]]
    return jnp.concatenate(parts, axis=1)


def _unpair(p):
    nt = p.shape[1] // (2 * PAIR)
    a = jnp.concatenate([p[:, 2 * j * PAIR:(2 * j + 1) * PAIR] for j in range(nt)], axis=1)
    b = jnp.concatenate([p[:, (2 * j + 1) * PAIR:(2 * j + 2) * PAIR] for j in range(nt)], axis=1)
    return a, b


def _from_col_shards(g):
    return jnp.transpose(g, (1, 0, 2)).reshape(g.shape[1], N_CHIP * g.shape[2])


def _to_col_shards(w):
    R, N = w.shape
    return jnp.transpose(w.reshape(R, N_CHIP, N // N_CHIP), (1, 0, 2))


def _split_heads(w, widths):
    R, N = w.shape
    per = sum(widths)
    w3 = w.reshape(R, N // per, per)
    lo = w3[:, :, :widths[0]].reshape(R, -1)
    hi = w3[:, :, widths[0]:].reshape(R, -1)
    return jnp.concatenate([lo, hi], axis=1)


def _merge_heads(w, widths):
    R, N = w.shape
    H = N // sum(widths)
    lo = w[:, :H * widths[0]].reshape(R, H, widths[0])
    hi = w[:, H * widths[0]:].reshape(R, H, widths[1])
    return jnp.concatenate([lo, hi], axis=2).reshape(R, N)


def _t5_bucket(dist):
    max_exact = REL_BUCKETS // 2
    n = jnp.maximum(dist, 0)
    large = max_exact + (jnp.log(jnp.maximum(n, 1).astype(F32) / max_exact)
                         / jnp.log(jnp.asarray(REL_MAX_DIST / max_exact, F32))
                         * (REL_BUCKETS - max_exact)).astype(jnp.int32)
    large = jnp.minimum(large, REL_BUCKETS - 1)
    return jnp.where(n < max_exact, n, large)


def _rel_tables():
    a = jnp.arange(SWA_BLOCK)
    b = jnp.arange(2 * SWA_BLOCK)
    dist = SWA_BLOCK + a[:, None] - b[None, :]
    valid = jnp.logical_and(dist >= 0, dist < SWA_BLOCK)
    onehot = jnp.logical_and(_t5_bucket(dist)[..., None] == jnp.arange(REL_BUCKETS), valid[..., None])
    onehot = onehot.astype(F32).reshape(2 * SWA_BLOCK * SWA_BLOCK, REL_BUCKETS)
    negmask = jnp.where(valid, 0.0, NEG).astype(F32).reshape(1, -1)
    return onehot, negmask


def _rope_tables(S):
    pos = jnp.arange(S, dtype=F32)
    inv = ROPE_THETA ** (-jnp.arange(0, MLA_ROPE, 2, dtype=F32) / MLA_ROPE)
    ang = pos[:, None] * inv[None, :]
    ang = jnp.concatenate([ang, ang, ang, ang], axis=-1)
    return jnp.cos(ang), jnp.sin(ang)


def _flat_pad(parts, rows=8):
    flat = jnp.concatenate([p.reshape(1, -1) for p in parts], axis=1)
    n = flat.shape[1]
    width = -(-n // (rows * LANES)) * LANES
    return jnp.pad(flat, ((0, 0), (0, rows * width - n))).reshape(rows, width)


def _unflat(vec, shapes):
    flat = vec.reshape(-1)
    out, off = [], 0
    for s in shapes:
        n = 1
        for d in s:
            n *= d
        out.append(flat[off:off + n].reshape(s))
        off += n
    return out


def kernel(x, c, w_ada, b_ada, g_pre_mix, g_post_mix, w_in, g_q_lat, w_uq, g_kv_lat, w_ukv, rel_bias, sinks, w_o, g_pre_ffn, g_post_ffn, w_up, conv_w, conv_b, w_down, loss_target, m_w_ada, m_b_ada, m_g_pre_mix, m_g_post_mix, m_w_in, m_g_q_lat, m_w_uq, m_g_kv_lat, m_w_ukv, m_rel_bias, m_sinks, m_w_o, m_g_pre_ffn, m_g_post_ffn, m_w_up, m_conv_w, m_conv_b, m_w_down, v_w_ada, v_b_ada, v_g_pre_mix, v_g_post_mix, v_w_in, v_g_q_lat, v_w_uq, v_g_kv_lat, v_w_ukv, v_rel_bias, v_sinks, v_w_o, v_g_pre_ffn, v_g_post_ffn, v_w_up, v_conv_w, v_conv_b, v_w_down):
    S, D = x.shape[1], x.shape[2]
    Rq, Rkv = g_q_lat.shape[1], g_kv_lat.shape[1]
    H = D // MLA_V
    NH = D // SWA_HD
    KW = SWA_KVH * SWA_HD
    F = w_down.shape[1] * N_CHIP
    xi, yi, ci = lax.axis_index("x"), lax.axis_index("y"), lax.axis_index("c")
    chip = 2 * xi + yi
    me = 2 * chip + ci
    x2, tgt = x[0], loss_target[0]

    c_all = _allgather8(jnp.broadcast_to(c, (8, D)), name="gather_c")[:, 0, :]
    n3 = w_ada.shape[2]
    mod_part = _ada_fwd(c_all, w_ada[0], lax.dynamic_slice(b_ada, (0, chip * n3), (1, n3)), name="ada_fwd")
    mod_all = _allgather8(mod_part, name="gather_mod")
    mod_me = lax.dynamic_index_in_dim(mod_all[0::2], me, axis=1, keepdims=False).reshape(1, 6 * D)
    sh1, sc1, gt1, sh2, sc2, gt2 = [mod_me[:, k * D:(k + 1) * D] for k in range(6)]
    cw_all = _allgather8(jnp.pad(conv_w[0], ((0, 5), (0, 0))), name="gather_conv_w")[0::2, :3]

    big = [w_in[0], w_uq[0], w_ukv[0], w_o[0], w_up[0], w_down[0]]
    local = [w.astype(BF16) for w in big]
    swaps = [False, False, False, False, True, False]
    send_a, recv_a, srcs_a, lands_a, token = _gather_start(local[:1], (mod_all, cw_all), swaps[:1], name="gather_start_in")
    send_b, recv_b, srcs_b, lands_b, token = _gather_start(local[1:], (token,), swaps[1:], name="gather_start_rest")
    send1, recv1, srcs, lands = send_a + send_b, recv_a + recv_b, srcs_a + srcs_b, lands_a + lands_b
    onehot, negmask = _rel_tables()
    npb = NH // (2 * SWA_KVH)
    rb_st = jnp.transpose(rel_bias.T.reshape(SWA_KVH, npb, 2, REL_BUCKETS), (0, 2, 1, 3)).reshape(NH, REL_BUCKETS)
    bias_m = (_matmul(rb_st, onehot.T, tie=token, name="rel_bias_table") + negmask).reshape(
        2 * SWA_KVH, npb * SWA_BLOCK, 2 * SWA_BLOCK)
    h = _modnorm_fwd(x2, g_pre_mix, sc1, sh1, name="pre_mix_norm")

    def whole(land, i):
        return lax.dynamic_update_index_in_dim(land, local[i], _slot(chip, swaps[i]), 0)

    def conv_slots(v):
        return jnp.stack([v[0], v[2], v[1], v[3]])

    s2, r2, l_in, _ = _gather_forward(send1[:1], recv1[:1], srcs[:1], lands[:1], h, swaps[:1], name="gather_forward_in")
    (l_in,) = _gather_finish(s2, r2, l_in, h, name="gather_finish_in")
    gin = whole(l_in, 0)
    win = _from_col_shards(gin)
    o_kr = Rq + Rkv
    o_q = o_kr + MLA_ROPE
    o_g = o_q + NH * SWA_HD + 2 * KW
    w_lat = jnp.concatenate([win[:, :o_q], win[:, o_kr:o_q]], axis=1)
    w_swa = win[:, o_q:o_g]
    w_gate = _pair(win[:, o_g:o_g + D], win[:, o_g + D:])
    w_in_all = jnp.concatenate([w_lat, w_swa, w_gate], axis=1)
    n_lat, n_swa = w_lat.shape[1], w_swa.shape[1]
    cw = conv_slots(cw_all)
    cb = conv_slots(conv_b.reshape(N_CHIP, 1, -1))
    cos, sin = _rope_tables(S)
    sink_v = sinks.reshape(NH)

    z_lat = _matmul(h, w_lat, name="in_proj_lat")
    z_swa = _matmul(h, w_swa, name="in_proj_swa")
    zg = _matmul(h, w_gate, name="in_proj_gate")
    s2b, r2b, l_b, _ = _gather_forward(send1[1:4], recv1[1:4], srcs[1:4], lands[1:4], zg, swaps[1:4],
                                       name="gather_forward_attn")
    nq, nkv = _lat_norm_fwd(z_lat, g_q_lat, g_kv_lat, name="lat_norm")
    l_uq, l_ukv, l_o = _gather_finish(s2b, r2b, l_b, nq, name="gather_finish_attn")
    wuq = _split_heads(_from_col_shards(whole(l_uq, 1)), (MLA_NOPE, MLA_ROPE))
    wukv = _split_heads(_from_col_shards(whole(l_ukv, 2)), (MLA_NOPE, MLA_V))
    wo = whole(l_o, 3).reshape(D, D)
    q_raw = _matmul(nq, wuq, name="uq_proj")
    kv_raw = _matmul(nkv, wukv, name="ukv_proj")
    qp, kp, vv = _mla_pack_fwd(q_raw, kv_raw, z_lat, cos, sin, o_kr, name="mla_pack")
    o_a, lse = _flash_fwd(qp, kp, vv, name="mla_attn")
    s2c, r2c, l_c, tok_c = _gather_forward(send1[4:], recv1[4:], srcs[4:], lands[4:], o_a, swaps[4:],
                                           name="gather_forward_ffn")
    o_b = _swa_fwd(z_swa, bias_m, sink_v, name="swa_attn")
    mixin = _gate_fwd(zg, o_a, o_b, name="gate_mix")
    mix = _matmul(mixin, wo, tie=tok_c, name="o_proj")
    x1 = _resnorm_fwd(x2, mix, g_post_mix, gt1, name="post_mix_norm")
    h2 = _modnorm_fwd(x1, g_pre_ffn, sc2, sh2, name="pre_ffn_norm")
    l_up, l_down = _gather_finish(s2c, r2c, l_c, h2, name="gather_finish_ffn")
    wup = whole(l_up, 4)
    wdown = whole(l_down, 5).reshape(F, D)
    t = _matmul(h2, wup, out_dtype=BF16, shards="out", name="up_proj")
    a = _conv_gate_fwd(t, cw, cb, name="conv_gate")
    yv = _matmul(a, wdown, name="down_proj")
    dout, loss_tile = _resnorm_loss(x1, yv, g_post_ffn, gt2, tgt, name="post_ffn_norm_loss")

    big_params = dict(w_in=(w_in, m_w_in, v_w_in), w_uq=(w_uq, m_w_uq, v_w_uq), w_ukv=(w_ukv, m_w_ukv, v_w_ukv),
                      w_o=(w_o, m_w_o, v_w_o), w_up=(w_up, m_w_up, v_w_up), w_down=(w_down, m_w_down, v_w_down))
    res = {}

    def start(nms, gs):
        sw = [nm == "w_up" for nm in nms]
        send, recv, gsrc, glands, tok = _scatter_start(gs, sw, name="grads_start_" + nms[0])
        return (nms, send, recv, gsrc, glands), tok

    def finish(pending, after):
        nms, send, recv, gsrc, glands = pending
        gsrc, glands = _scatter_wait(send, recv, gsrc, glands, after, name="grads_wait_" + nms[0])
        halves = [_sum_pieces(l, g, jnp.stack([me, _slot(chip, nm == "w_up"), ci]).astype(jnp.int32), name="grad_sum_" + nm)
                  for l, g, nm in zip(glands, gsrc, nms)]
        for nm, g in zip(nms, _share_halves(halves, name="grads_share_" + nms[0])):
            w, m, v = big_params[nm]
            res[nm] = (g,) + tuple(_adamw(w[0], g, m[0], v[0], name="adamw_" + nm))

    dy, dg_post_ffn, dgt2 = _resnorm_bwd(dout, yv, g_post_ffn, gt2, name="post_ffn_norm_bwd")
    dw_down = _matmul(a, dy, ta=True, out_dtype=BF16, name="down_proj_dw")
    p_down, tok = start(["w_down"], [dw_down.reshape(N_CHIP, F // N_CHIP, D)])
    da = _matmul(dy, wdown, tb=True, out_dtype=BF16, tie=tok, name="down_proj_dx")
    du, dcw_s, dcb_s = _conv_gate_bwd(t, da, cw, cb, name="conv_gate_bwd")
    dt = _conv_bwd_dt(du, cw, name="conv_bwd_dt")
    dw_up = _matmul(h2, dt, ta=True, out_dtype=BF16, shards="out", name="up_proj_dw")
    p_up, tok = start(["w_up"], [dw_up])
    dh2 = _matmul(dt, wup, tb=True, tie=tok, shards="k", name="up_proj_dx")
    dx1, dg_pre_ffn, dsc2, dsh2 = _modnorm_bwd(dh2, x1, g_pre_ffn, sc2, dout, name="pre_ffn_norm_bwd")
    dmix, dg_post_mix, dgt1 = _resnorm_bwd(dx1, mix, g_post_mix, gt1, name="post_mix_norm_bwd")
    dw_o = _matmul(mixin, dmix, ta=True, out_dtype=BF16, name="o_proj_dw")
    p_o, tok = start(["w_o"], [dw_o.reshape(N_CHIP, D // N_CHIP, D)])
    dmixin = _matmul(dmix, wo, tb=True, tie=tok, name="o_proj_dx")
    do_a, do_b, dzg = _gate_bwd(dmixin, zg, o_a, o_b, name="gate_mix_bwd")
    dqp, dkp, dvv = _flash_bwd(qp, kp, vv, o_a, do_a, lse, name="mla_attn_bwd")
    dq_raw, dkv_raw, dkr = _mla_pack_bwd(dqp, dkp, dvv, cos, sin, name="mla_pack_bwd")
    dw_uq_p = _matmul(nq, dq_raw, ta=True, out_dtype=BF16, name="uq_proj_dw")
    dw_ukv_p = _matmul(nkv, dkv_raw, ta=True, out_dtype=BF16, name="ukv_proj_dw")
    p_qkv, tok = start(["w_uq", "w_ukv"], [_to_col_shards(_merge_heads(dw_uq_p, (MLA_NOPE, MLA_ROPE))),
                                           _to_col_shards(_merge_heads(dw_ukv_p, (MLA_NOPE, MLA_V)))])
    dnq = _matmul(dq_raw, wuq, tb=True, tie=tok, name="uq_proj_dx")
    dnkv = _matmul(dkv_raw, wukv, tb=True, name="ukv_proj_dx")
    dz_lat, dg_q, dg_kv = _lat_norm_bwd(z_lat, dnq, dnkv, dkr, g_q_lat, g_kv_lat, name="lat_norm_bwd")
    dz_swa, dbias, dsink = _swa_bwd(z_swa, bias_m, sink_v, o_b, do_b, name="swa_attn_bwd")
    dz = jnp.concatenate([dz_lat, dz_swa, dzg], axis=1)
    dw_in_p = _matmul(h, dz, ta=True, out_dtype=BF16, name="in_proj_dw")
    dga, dgb = _unpair(dw_in_p[:, n_lat + n_swa:])
    dw_in = jnp.concatenate([dw_in_p[:, :o_q], dw_in_p[:, n_lat:n_lat + n_swa], dga, dgb], axis=1)
    p_in, tok = start(["w_in"], [_to_col_shards(dw_in)])
    dh = _matmul(dz, w_in_all, tb=True, tie=tok, name="in_proj_dx")
    grad_x, dg_pre_mix, dsc1, dsh1 = _modnorm_bwd(dh, x2, g_pre_mix, sc1, dx1, name="pre_mix_norm_bwd")
    drel_st = _matmul(dbias.reshape(NH, -1), onehot, tie=grad_x, name="rel_bias_bwd")
    for pending in (p_down, p_up, p_o, p_qkv):
        finish(pending, drel_st)
    drel = jnp.transpose(drel_st.reshape(SWA_KVH, 2, npb, REL_BUCKETS), (0, 2, 1, 3)).reshape(NH, REL_BUCKETS).T

    dcw = _from_col_shards(conv_slots(dcw_s))
    dcb = conv_slots(dcb_s).reshape(1, -1)
    dmod = jnp.concatenate([dsh1, dsc1, dgt1, dsh2, dsc2, dgt2], axis=1)
    small = [dmod, dg_pre_mix, dg_post_mix, dg_pre_ffn, dg_post_ffn, dg_q, dg_kv, drel, dsink[:, :NH], dcb, dcw]
    shapes = [p.shape for p in small]
    done = [res[nm][1] for nm in ("w_down", "w_up", "w_o", "w_uq", "w_ukv")]
    small_all = _allgather8(_flat_pad(small), tie=done, name="gather_small_grads")
    tot = _unflat(_sum_devices(small_all, name="sum_small_grads"), shapes)
    g_b_ada, g_pre_mix_g, g_post_mix_g, g_pre_ffn_g, g_post_ffn_g, g_q_g, g_kv_g, g_rel, g_sinks, g_cb, g_cw_full = tot
    dmod_all = small_all.reshape(N_DEV, -1)[:, :6 * D]
    g_w_ada = _ada_bwd(c_all.T, lax.dynamic_slice(dmod_all, (0, chip * n3), (N_DEV, n3)), name="ada_bwd")
    ncw = conv_w.shape[2]
    g_cw = lax.dynamic_slice(g_cw_full, (0, chip * ncw), (3, ncw))

    res["w_ada"] = (g_w_ada,) + tuple(_adamw(w_ada[0], g_w_ada, m_w_ada[0], v_w_ada[0], name="adamw_w_ada"))
    finish(p_in, g_w_ada)
    snames = ["b_ada", "g_pre_mix", "g_post_mix", "g_pre_ffn", "g_post_ffn", "g_q_lat", "g_kv_lat", "rel_bias", "sinks",
              "conv_b", "conv_w"]
    sw = [b_ada, g_pre_mix, g_post_mix, g_pre_ffn, g_post_ffn, g_q_lat, g_kv_lat, rel_bias, sinks, conv_b, conv_w]
    sm = [m_b_ada, m_g_pre_mix, m_g_post_mix, m_g_pre_ffn, m_g_post_ffn, m_g_q_lat, m_g_kv_lat, m_rel_bias, m_sinks,
          m_conv_b, m_conv_w]
    sv = [v_b_ada, v_g_pre_mix, v_g_post_mix, v_g_pre_ffn, v_g_post_ffn, v_g_q_lat, v_g_kv_lat, v_rel_bias, v_sinks,
          v_conv_b, v_conv_w]
    sg = [g_b_ada, g_pre_mix_g, g_post_mix_g, g_pre_ffn_g, g_post_ffn_g, g_q_g, g_kv_g, g_rel, g_sinks, g_cb, g_cw]
    sshapes = [w.shape for w in sw]
    sd, snm, snv = _adamw(_flat_pad(sw), _flat_pad(sg), _flat_pad(sm), _flat_pad(sv), name="adamw_small")
    sd, snm, snv = _unflat(sd, sshapes), _unflat(snm, sshapes), _unflat(snv, sshapes)
    for k, nm in enumerate(snames):
        res[nm] = (sg[k].reshape(sshapes[k]), sd[k], snm[k], snv[k])

    order = ["w_ada", "b_ada", "g_pre_mix", "g_post_mix", "w_in", "g_q_lat", "w_uq", "g_kv_lat", "w_ukv", "rel_bias", "sinks",
             "w_o", "g_pre_ffn", "g_post_ffn", "w_up", "conv_w", "conv_b", "w_down"]
    ref_shapes = dict(w_ada=w_ada.shape, w_in=w_in.shape, w_uq=w_uq.shape, w_ukv=w_ukv.shape, w_o=w_o.shape,
                      w_up=w_up.shape, w_down=w_down.shape)
    outs = []
    for k in range(4):
        for nm in order:
            arr = res[nm][k]
            outs.append(arr.reshape(ref_shapes[nm]) if nm in ref_shapes else arr)
    loss = lax.psum(loss_tile[0, 0], ("x", "y", "c"))
    return (loss, grad_x[None], *outs)
```

```python
import math

import jax
import jax.numpy as jnp
from jax import lax
from jax.experimental import pallas as pl
from jax.experimental.pallas import tpu as pltpu

F32 = jnp.float32
BF16 = jnp.bfloat16
MESH = pl.DeviceIdType.MESH
HIGHEST = lax.Precision.HIGHEST

N_DEV = 8
N_CHIP = 4
LANES = 128
MLA_NOPE = 128
MLA_ROPE = 64
MLA_V = 128
MLA_QK = MLA_NOPE + MLA_ROPE
MLA_QK_PAD = 256
ROPE_THETA = 10000.0
SWA_HD = 64
SWA_KVH = 4
SWA_BLOCK = 128
REL_BUCKETS = 32
REL_MAX_DIST = 128
PAIR = 512
EPS = 1e-6
NEG = -1e30
ADAM_LR = 0.001
ADAM_B1 = 0.9
ADAM_B2 = 0.999
ADAM_EPS = 1e-08
ADAM_WD = 0.01
ADAM_STEP = 10

ANY = pl.BlockSpec(memory_space=pl.ANY)
VMEM_FULL = pl.BlockSpec(memory_space=pltpu.VMEM)
SMEM_FULL = pl.BlockSpec(memory_space=pltpu.SMEM)


def _params(*sem):
    return pltpu.CompilerParams(dimension_semantics=sem if sem else None)


def _tied(body, tie):
    if tie is None:
        return body, [], []
    ties = list(tie) if isinstance(tie, (list, tuple)) else [tie]

    def tied_body(*refs):
        body(*refs[len(ties):])

    return tied_body, [ANY] * len(ties), ties


def _tile(n, pref, unit=LANES):
    best = None
    for t in range(unit, min(n, pref) + 1, unit):
        if n % t == 0:
            best = t
    return n if best is None else best


def _matmul(a, b, *, ta=False, tb=False, out_dtype=F32, tie=None, shards=None, bcols=None, name):
    a2 = a.shape[1:] if shards == "k" else a.shape
    b2 = b.shape[1:] if shards else b.shape
    nsh = b.shape[0] if shards else 1
    K, M = a2 if ta else a2[::-1]
    N, K2 = b2 if tb else b2[::-1]
    assert K == K2, (a.shape, b.shape, ta, tb)
    exact = a.dtype == F32
    col0 = 0
    if bcols is not None:
        assert not tb and shards is None
        col0, N = bcols
    tn = _tile(math.gcd(N, col0) if col0 else N, 1536)
    col0 //= tn
    tk = _tile(K, 2048)
    nkc = K // tk
    nk = nkc * (nsh if shards == "k" else 1)
    tm = M if M < 8 else _tile(M, 1024, LANES if ta else 8)
    dn = (((0 if ta else 1,), (1 if tb else 0,)), ((), ()))
    kax = 3 if shards == "out" else 2

    def product(a_ref, b_ref):
        return lax.dot_general(a_ref[...], b_ref[...], dn, preferred_element_type=F32,
                               precision=HIGHEST if exact else None)

    def body_acc(a_ref, b_ref, o_ref, acc_ref):
        k = pl.program_id(kax)

        @pl.when(k == 0)
        def _():
            acc_ref[...] = product(a_ref, b_ref)

        @pl.when(jnp.logical_and(k > 0, k < nk - 1))
        def _():
            acc_ref[...] += product(a_ref, b_ref)

        @pl.when(k == nk - 1)
        def _():
            o_ref[...] = (acc_ref[...] + product(a_ref, b_ref)).astype(o_ref.dtype)

    def body_one(a_ref, b_ref, o_ref):
        o_ref[...] = product(a_ref, b_ref).astype(o_ref.dtype)

    a_blk, b_blk = ((tk, tm) if ta else (tm, tk)), ((tn, tk) if tb else (tk, tn))
    a_at = (lambda i, k: (k, i)) if ta else (lambda i, k: (i, k))
    b_at = (lambda j, k: (j, k)) if tb else (lambda j, k: (k, j + col0))
    if shards == "out":
        grid = (nsh, M // tm, N // tn, nk)
        a_spec = pl.BlockSpec(a_blk, lambda s, i, j, k: a_at(i, k))
        b_spec = pl.BlockSpec((None,) + b_blk, lambda s, i, j, k: (s,) + b_at(j, k))
        o_spec = pl.BlockSpec((None, tm, tn), lambda s, i, j, k: (s, i, j))
        out_shape = jax.ShapeDtypeStruct((nsh, M, N), out_dtype)
        sem = ("parallel", "parallel", "parallel", "arbitrary")
    elif shards == "k":
        grid = (M // tm, N // tn, nk)
        a_spec = pl.BlockSpec((None,) + a_blk, lambda i, j, k: (k // nkc,) + a_at(i, k % nkc))
        b_spec = pl.BlockSpec((None,) + b_blk, lambda i, j, k: (k // nkc,) + b_at(j, k % nkc))
        o_spec = pl.BlockSpec((tm, tn), lambda i, j, k: (i, j))
        out_shape = jax.ShapeDtypeStruct((M, N), out_dtype)
        sem = ("parallel", "parallel", "arbitrary")
    else:
        grid = (M // tm, N // tn, nk)
        a_spec = pl.BlockSpec(a_blk, lambda i, j, k: a_at(i, k))
        b_spec = pl.BlockSpec(b_blk, lambda i, j, k: b_at(j, k))
        o_spec = pl.BlockSpec((tm, tn), lambda i, j, k: (i, j))
        out_shape = jax.ShapeDtypeStruct((M, N), out_dtype)
        sem = ("parallel", "parallel", "arbitrary")
    body, tspec, targ = _tied(body_one if nk == 1 else body_acc, tie)
    return pl.pallas_call(
        body, name=name, out_shape=out_shape, grid=grid, in_specs=tspec + [a_spec, b_spec], out_specs=o_spec,
        scratch_shapes=[] if nk == 1 else [pltpu.VMEM((tm, tn), F32)],
        compiler_params=_params(*sem),
    )(*targ, a, b)


def _row_tile(S, width):
    return _tile(S, max(8, (1 << 19) // width), 8)


def _rstd(x):
    return lax.rsqrt(jnp.mean(x * x, axis=-1, keepdims=True) + EPS)


def _acc_rows(ref, val, first):
    s = jnp.sum(val, axis=0, keepdims=True)

    @pl.when(first)
    def _():
        ref[...] = s

    @pl.when(jnp.logical_not(first))
    def _():
        ref[...] += s


def _modnorm_fwd(x, g, sc, sh, *, name):
    S, D = x.shape
    tr = _row_tile(S, D)

    def body(x_ref, g_ref, sc_ref, sh_ref, h_ref):
        xv = x_ref[...]
        n = (xv * _rstd(xv)) * g_ref[...]
        h_ref[...] = (n * (1.0 + sc_ref[...]) + sh_ref[...]).astype(BF16)

    row = pl.BlockSpec((tr, D), lambda i: (i, 0))
    vec = pl.BlockSpec((1, D), lambda i: (0, 0))
    return pl.pallas_call(
        body, name=name, out_shape=jax.ShapeDtypeStruct((S, D), BF16), grid=(S // tr,),
        in_specs=[row, vec, vec, vec], out_specs=row, compiler_params=_params("parallel"),
    )(x, g, sc, sh)


def _modnorm_bwd(dh, x, g, sc, dres, *, name):
    S, D = x.shape
    tr = _row_tile(S, D)

    def body(dh_ref, x_ref, g_ref, sc_ref, dres_ref, dx_ref, dg_ref, dsc_ref, dsh_ref):
        first = pl.program_id(0) == 0
        xv = x_ref[...]
        dhv = dh_ref[...]
        gv = g_ref[...]
        r = _rstd(xv)
        xhat = xv * r
        _acc_rows(dsh_ref, dhv, first)
        _acc_rows(dsc_ref, dhv * (xhat * gv), first)
        dn = dhv * (1.0 + sc_ref[...])
        _acc_rows(dg_ref, dn * xhat, first)
        dxhat = dn * gv
        proj = jnp.mean(dxhat * xhat, axis=-1, keepdims=True)
        dx_ref[...] = r * (dxhat - xhat * proj) + dres_ref[...]

    row = pl.BlockSpec((tr, D), lambda i: (i, 0))
    vec = pl.BlockSpec((1, D), lambda i: (0, 0))
    vshape = jax.ShapeDtypeStruct((1, D), F32)
    return pl.pallas_call(
        body, name=name,
        out_shape=(jax.ShapeDtypeStruct((S, D), F32), vshape, vshape, vshape), grid=(S // tr,),
        in_specs=[row, row, vec, vec, row], out_specs=(row, vec, vec, vec),
        compiler_params=_params("arbitrary"),
    )(dh, x, g, sc, dres)


def _resnorm_fwd(xres, m, g, gt, *, name):
    S, D = xres.shape
    tr = _row_tile(S, D)

    def body(x_ref, m_ref, g_ref, gt_ref, o_ref):
        mv = m_ref[...]
        o_ref[...] = x_ref[...] + gt_ref[...] * ((mv * _rstd(mv)) * g_ref[...])

    row = pl.BlockSpec((tr, D), lambda i: (i, 0))
    vec = pl.BlockSpec((1, D), lambda i: (0, 0))
    return pl.pallas_call(
        body, name=name, out_shape=jax.ShapeDtypeStruct((S, D), F32), grid=(S // tr,),
        in_specs=[row, row, vec, vec], out_specs=row, compiler_params=_params("parallel"),
    )(xres, m, g, gt)


def _resnorm_loss(xres, m, g, gt, target, *, name):
    S, D = xres.shape
    tr = _row_tile(S, D)

    def body(x_ref, m_ref, g_ref, gt_ref, t_ref, d_ref, loss_ref):
        mv = m_ref[...]
        out = x_ref[...] + gt_ref[...] * ((mv * _rstd(mv)) * g_ref[...])
        err = out - t_ref[...]
        d_ref[...] = err * (1.0 / D)
        part = 0.5 * jnp.sum(jnp.mean(err * err, axis=-1, keepdims=True), axis=0, keepdims=True)
        part = jnp.broadcast_to(part, loss_ref.shape)

        @pl.when(pl.program_id(0) == 0)
        def _():
            loss_ref[...] = part

        @pl.when(pl.program_id(0) != 0)
        def _():
            loss_ref[...] += part

    row = pl.BlockSpec((tr, D), lambda i: (i, 0))
    vec = pl.BlockSpec((1, D), lambda i: (0, 0))
    return pl.pallas_call(
        body, name=name,
        out_shape=(jax.ShapeDtypeStruct((S, D), F32), jax.ShapeDtypeStruct((8, LANES), F32)), grid=(S // tr,),
        in_specs=[row, row, vec, vec, row], out_specs=(row, pl.BlockSpec((8, LANES), lambda i: (0, 0))),
        compiler_params=_params("arbitrary"),
    )(xres, m, g, gt, target)


def _resnorm_bwd(dout, m, g, gt, *, name):
    S, D = m.shape
    tr = _row_tile(S, D)

    def body(d_ref, m_ref, g_ref, gt_ref, dm_ref, dg_ref, dgt_ref):
        first = pl.program_id(0) == 0
        mv = m_ref[...]
        dv = d_ref[...]
        gv = g_ref[...]
        r = _rstd(mv)
        mhat = mv * r
        _acc_rows(dgt_ref, dv * (mhat * gv), first)
        dn = dv * gt_ref[...]
        _acc_rows(dg_ref, dn * mhat, first)
        dmhat = dn * gv
        proj = jnp.mean(dmhat * mhat, axis=-1, keepdims=True)
        dm_ref[...] = (r * (dmhat - mhat * proj)).astype(BF16)

    row = pl.BlockSpec((tr, D), lambda i: (i, 0))
    vec = pl.BlockSpec((1, D), lambda i: (0, 0))
    vshape = jax.ShapeDtypeStruct((1, D), F32)
    return pl.pallas_call(
        body, name=name, out_shape=(jax.ShapeDtypeStruct((S, D), BF16), vshape, vshape), grid=(S // tr,),
        in_specs=[row, row, vec, vec], out_specs=(row, vec, vec), compiler_params=_params("arbitrary"),
    )(dout, m, g, gt)


def _lat_norm_fwd(z_lat, g_q, g_kv, *, name):
    S, W = z_lat.shape
    Rq, Rkv = g_q.shape[1], g_kv.shape[1]
    tr = _row_tile(S, W)

    def body(z_ref, gq_ref, gkv_ref, nq_ref, nkv_ref):
        cq = z_ref[:, :Rq]
        ckv = z_ref[:, Rq:Rq + Rkv]
        nq_ref[...] = ((cq * _rstd(cq)) * gq_ref[...]).astype(BF16)
        nkv_ref[...] = ((ckv * _rstd(ckv)) * gkv_ref[...]).astype(BF16)

    return pl.pallas_call(
        body, name=name,
        out_shape=(jax.ShapeDtypeStruct((S, Rq), BF16), jax.ShapeDtypeStruct((S, Rkv), BF16)), grid=(S // tr,),
        in_specs=[pl.BlockSpec((tr, W), lambda i: (i, 0)), pl.BlockSpec((1, Rq), lambda i: (0, 0)),
                  pl.BlockSpec((1, Rkv), lambda i: (0, 0))],
        out_specs=(pl.BlockSpec((tr, Rq), lambda i: (i, 0)), pl.BlockSpec((tr, Rkv), lambda i: (i, 0))),
        compiler_params=_params("parallel"),
    )(z_lat, g_q, g_kv)


def _lat_norm_bwd(z_lat, dnq, dnkv, dkr, g_q, g_kv, *, name):
    S, W = z_lat.shape
    Rq, Rkv = g_q.shape[1], g_kv.shape[1]
    tr = _row_tile(S, W)

    def one(c, dn, gv):
        r = _rstd(c)
        chat = c * r
        dchat = dn * gv
        proj = jnp.mean(dchat * chat, axis=-1, keepdims=True)
        return r * (dchat - chat * proj), dn * chat

    def body(z_ref, dnq_ref, dnkv_ref, dkr_ref, gq_ref, gkv_ref, dz_ref, dgq_ref, dgkv_ref):
        first = pl.program_id(0) == 0
        dcq, pq = one(z_ref[:, :Rq], dnq_ref[...], gq_ref[...])
        dckv, pkv = one(z_ref[:, Rq:Rq + Rkv], dnkv_ref[...], gkv_ref[...])
        _acc_rows(dgq_ref, pq, first)
        _acc_rows(dgkv_ref, pkv, first)
        dz_ref[:, :Rq] = dcq.astype(BF16)
        dz_ref[:, Rq:Rq + Rkv] = dckv.astype(BF16)
        dz_ref[:, Rq + Rkv:Rq + Rkv + LANES] = dkr_ref[...].astype(BF16)
        if W > Rq + Rkv + LANES:
            dz_ref[:, Rq + Rkv + LANES:] = jnp.zeros((tr, W - Rq - Rkv - LANES), BF16)

    return pl.pallas_call(
        body, name=name,
        out_shape=(jax.ShapeDtypeStruct((S, W), BF16), jax.ShapeDtypeStruct((1, Rq), F32),
                   jax.ShapeDtypeStruct((1, Rkv), F32)), grid=(S // tr,),
        in_specs=[pl.BlockSpec((tr, W), lambda i: (i, 0)), pl.BlockSpec((tr, Rq), lambda i: (i, 0)),
                  pl.BlockSpec((tr, Rkv), lambda i: (i, 0)), pl.BlockSpec((tr, LANES), lambda i: (i, 0)),
                  pl.BlockSpec((1, Rq), lambda i: (0, 0)), pl.BlockSpec((1, Rkv), lambda i: (0, 0))],
        out_specs=(pl.BlockSpec((tr, W), lambda i: (i, 0)), pl.BlockSpec((1, Rq), lambda i: (0, 0)),
                   pl.BlockSpec((1, Rkv), lambda i: (0, 0))),
        compiler_params=_params("arbitrary"),
    )(z_lat, dnq, dnkv, dkr, g_q, g_kv)


def _rot(x, lo32):
    a = pltpu.roll(x, 32, 1)
    b = pltpu.roll(x, LANES - 32, 1)
    return jnp.where(lo32, -b, a)


def _rot_t(g, lo32):
    a = pltpu.roll(g, 32, 1)
    b = pltpu.roll(g, LANES - 32, 1)
    return jnp.where(lo32, b, -a)


def _mla_pack_fwd(q_raw, kv_raw, z_lat, cos, sin, kr_off, *, name):
    S = q_raw.shape[0]
    H = kv_raw.shape[1] // (MLA_NOPE + MLA_V)
    W = z_lat.shape[1]
    scale = MLA_QK ** -0.5
    tr = min(S, 128)
    nope_w = H * MLA_NOPE

    def body(q_ref, kv_ref, z_ref, cos_ref, sin_ref, qp_ref, kp_ref, v_ref):
        lane = lax.broadcasted_iota(jnp.int32, (tr, LANES), 1)
        lo32 = (lane % 64) < 32
        lo64 = lane < 64
        c = cos_ref[...]
        s = sin_ref[...]
        kr = z_ref[:, kr_off:kr_off + LANES]
        kr = (kr * c + _rot(kr, lo32) * s).astype(BF16)
        for hp in range(H // 2):
            xb = q_ref[:, nope_w + hp * LANES:nope_w + (hp + 1) * LANES]
            rb = (xb * c + _rot(xb, lo32) * s) * scale
            for e in range(2):
                h = 2 * hp + e
                base = h * MLA_QK_PAD
                qp_ref[:, base:base + LANES] = (q_ref[:, h * LANES:(h + 1) * LANES] * scale).astype(BF16)
                keep = lo64 if e == 0 else jnp.logical_not(lo64)
                qp_ref[:, base + LANES:base + 2 * LANES] = jnp.where(keep, rb, 0.0).astype(BF16)
                kp_ref[:, base:base + LANES] = kv_ref[:, h * LANES:(h + 1) * LANES].astype(BF16)
                kp_ref[:, base + LANES:base + 2 * LANES] = kr
        v_ref[...] = kv_ref[:, nope_w:].astype(BF16)

    return pl.pallas_call(
        body, name=name,
        out_shape=(jax.ShapeDtypeStruct((S, H * MLA_QK_PAD), BF16), jax.ShapeDtypeStruct((S, H * MLA_QK_PAD), BF16),
                   jax.ShapeDtypeStruct((S, H * MLA_V), BF16)), grid=(S // tr,),
        in_specs=[pl.BlockSpec((tr, q_raw.shape[1]), lambda i: (i, 0)), pl.BlockSpec((tr, kv_raw.shape[1]), lambda i: (i, 0)),
                  pl.BlockSpec((tr, W), lambda i: (i, 0)), pl.BlockSpec((tr, LANES), lambda i: (i, 0)),
                  pl.BlockSpec((tr, LANES), lambda i: (i, 0))],
        out_specs=(pl.BlockSpec((tr, H * MLA_QK_PAD), lambda i: (i, 0)), pl.BlockSpec((tr, H * MLA_QK_PAD), lambda i: (i, 0)),
                   pl.BlockSpec((tr, H * MLA_V), lambda i: (i, 0))),
        compiler_params=_params("parallel"),
    )(q_raw, kv_raw, z_lat, cos, sin)


def _mla_pack_bwd(dqp, dkp, dv, cos, sin, *, name):
    S = dqp.shape[0]
    H = dv.shape[1] // MLA_V
    scale = MLA_QK ** -0.5
    tr = min(S, 128)
    nope_w = H * MLA_NOPE

    def body(dqp_ref, dkp_ref, dv_ref, cos_ref, sin_ref, dq_ref, dkv_ref, dkr_ref):
        lane = lax.broadcasted_iota(jnp.int32, (tr, LANES), 1)
        lo32 = (lane % 64) < 32
        lo64 = lane < 64
        c = cos_ref[...]
        s = sin_ref[...]
        dkr2 = jnp.zeros((tr, LANES), F32)
        for hp in range(H // 2):
            be = (2 * hp) * MLA_QK_PAD
            bo = (2 * hp + 1) * MLA_QK_PAD
            g = jnp.where(lo64, dqp_ref[:, be + LANES:be + 2 * LANES], dqp_ref[:, bo + LANES:bo + 2 * LANES]) * scale
            dq_ref[:, nope_w + hp * LANES:nope_w + (hp + 1) * LANES] = (g * c + _rot_t(g * s, lo32)).astype(BF16)
            for h, base in ((2 * hp, be), (2 * hp + 1, bo)):
                dq_ref[:, h * LANES:(h + 1) * LANES] = (dqp_ref[:, base:base + LANES] * scale).astype(BF16)
                dkv_ref[:, h * LANES:(h + 1) * LANES] = dkp_ref[:, base:base + LANES].astype(BF16)
                dkr2 = dkr2 + dkp_ref[:, base + LANES:base + 2 * LANES]
        dkr2 = dkr2 * c + _rot_t(dkr2 * s, lo32)
        dkr2 = dkr2 + pltpu.roll(dkr2, 64, 1)
        dkr_ref[...] = jnp.where(lo64, dkr2, 0.0)
        dkv_ref[:, nope_w:] = dv_ref[...].astype(BF16)

    return pl.pallas_call(
        body, name=name,
        out_shape=(jax.ShapeDtypeStruct((S, nope_w + H * MLA_ROPE), BF16), jax.ShapeDtypeStruct((S, 2 * nope_w), BF16),
                   jax.ShapeDtypeStruct((S, LANES), F32)), grid=(S // tr,),
        in_specs=[pl.BlockSpec((tr, H * MLA_QK_PAD), lambda i: (i, 0)), pl.BlockSpec((tr, H * MLA_QK_PAD), lambda i: (i, 0)),
                  pl.BlockSpec((tr, H * MLA_V), lambda i: (i, 0)), pl.BlockSpec((tr, LANES), lambda i: (i, 0)),
                  pl.BlockSpec((tr, LANES), lambda i: (i, 0))],
        out_specs=(pl.BlockSpec((tr, nope_w + H * MLA_ROPE), lambda i: (i, 0)), pl.BlockSpec((tr, 2 * nope_w), lambda i: (i, 0)),
                   pl.BlockSpec((tr, LANES), lambda i: (i, 0))),
        compiler_params=_params("parallel"),
    )(dqp, dkp, dv, cos, sin)


FLASH_HB = 2


def _causal_pairs(nb):
    qi = [i for i in range(nb) for j in range(i + 1)]
    kj = [j for i in range(nb) for j in range(i + 1)]
    return jnp.asarray(qi, jnp.int32), jnp.asarray(kj, jnp.int32)


def _scores(q, k, diagonal, t):
    s = lax.dot_general(q, k, (((1,), (1,)), ((), ())), preferred_element_type=F32)
    if diagonal:
        row = lax.broadcasted_iota(jnp.int32, (t, t), 0)
        col = lax.broadcasted_iota(jnp.int32, (t, t), 1)
        s = jnp.where(col <= row, s, NEG)
    return s


def _flash_fwd(qp, kp, v, *, name):
    S = qp.shape[0]
    H = v.shape[1] // MLA_V
    t = min(S, 512)
    nb = S // t
    HB = 2 * FLASH_HB
    qi, kj = _causal_pairs(nb)
    QW, VW = MLA_QK_PAD, MLA_V

    def body(qi_ref, kj_ref, q_ref, k_ref, v_ref, o_ref, lse_ref, m_s, l_s, acc_s):
        pr = pl.program_id(1)
        i = qi_ref[pr]
        j = kj_ref[pr]

        @pl.when(j == 0)
        def _():
            m_s[...] = jnp.full_like(m_s, NEG)
            l_s[...] = jnp.zeros_like(l_s)
            acc_s[...] = jnp.zeros_like(acc_s)

        def step(diagonal):
            for hh in range(HB):
                s = _scores(q_ref[:, hh * QW:(hh + 1) * QW], k_ref[:, hh * QW:(hh + 1) * QW], diagonal, t)
                m_prev = m_s[hh]
                m_cur = jnp.maximum(m_prev, jnp.max(s, axis=1, keepdims=True))
                alpha = jnp.exp(m_prev - m_cur)
                p = jnp.exp(s - m_cur[:, :1])
                l_new = alpha * l_s[hh] + jnp.sum(p, axis=1, keepdims=True)
                acc = alpha * acc_s[hh] + jnp.dot(p.astype(BF16), v_ref[:, hh * VW:(hh + 1) * VW], preferred_element_type=F32)
                if diagonal:
                    o_ref[:, hh * VW:(hh + 1) * VW] = acc / l_new
                    lse_ref[hh] = m_cur + jnp.log(l_new)
                else:
                    l_s[hh] = l_new
                    acc_s[hh] = acc
                    m_s[hh] = m_cur

        @pl.when(i != j)
        def _():
            step(False)

        @pl.when(i == j)
        def _():
            step(True)

    return pl.pallas_call(
        body, name=name,
        out_shape=(jax.ShapeDtypeStruct((S, H * VW), F32), jax.ShapeDtypeStruct((H, S, LANES), F32)),
        grid_spec=pltpu.PrefetchScalarGridSpec(
            num_scalar_prefetch=2, grid=(H // HB, qi.shape[0]),
            in_specs=[pl.BlockSpec((t, HB * QW), lambda g, p, qi, kj: (qi[p], g)),
                      pl.BlockSpec((t, HB * QW), lambda g, p, qi, kj: (kj[p], g)),
                      pl.BlockSpec((t, HB * VW), lambda g, p, qi, kj: (kj[p], g))],
            out_specs=(pl.BlockSpec((t, HB * VW), lambda g, p, qi, kj: (qi[p], g)),
                       pl.BlockSpec((HB, t, LANES), lambda g, p, qi, kj: (g, qi[p], 0))),
            scratch_shapes=[pltpu.VMEM((HB, t, LANES), F32), pltpu.VMEM((HB, t, LANES), F32), pltpu.VMEM((HB, t, VW), F32)]),
        compiler_params=_params("parallel", "arbitrary"),
    )(qi, kj, qp, kp, v)


def _flash_bwd(qp, kp, v, o, do, lse, *, name):
    S = qp.shape[0]
    H = v.shape[1] // MLA_V
    t = min(S, 512)
    nb = S // t
    HB = FLASH_HB
    qi, kj = _causal_pairs(nb)
    QW, VW = MLA_QK_PAD, MLA_V
    tn = (((0,), (0,)), ((), ()))
    nt = (((1,), (1,)), ((), ()))

    def body(qi_ref, kj_ref, q_ref, k_ref, v_ref, o_ref, do_ref, lse_ref, dq_ref, dk_ref, dv_ref, dq_s):
        pr = pl.program_id(1)
        i = qi_ref[pr]
        j = kj_ref[pr]
        rows = pl.ds(pl.multiple_of(j * t, t), t)

        @pl.when(pr == 0)
        def _():
            dk_ref[...] = jnp.zeros_like(dk_ref)
            dv_ref[...] = jnp.zeros_like(dv_ref)

        @pl.when(j == 0)
        def _():
            dq_s[...] = jnp.zeros_like(dq_s)

        def step(diagonal):
            for hh in range(HB):
                q = q_ref[:, hh * QW:(hh + 1) * QW]
                k = k_ref[:, hh * QW:(hh + 1) * QW]
                dob = do_ref[:, hh * VW:(hh + 1) * VW]
                p = jnp.exp(_scores(q, k, diagonal, t) - lse_ref[hh][:, :1])
                delta = jnp.sum(dob.astype(F32) * o_ref[:, hh * VW:(hh + 1) * VW], axis=1, keepdims=True)
                dp = lax.dot_general(dob, v_ref[:, hh * VW:(hh + 1) * VW], nt, preferred_element_type=F32)
                dsb = (p * (dp - delta)).astype(BF16)
                dv_ref[rows, hh * VW:(hh + 1) * VW] += lax.dot_general(p.astype(BF16), dob, tn, preferred_element_type=F32)
                dk_ref[rows, hh * QW:(hh + 1) * QW] += lax.dot_general(dsb, q, tn, preferred_element_type=F32)
                dq = dq_s[:, hh * QW:(hh + 1) * QW] + jnp.dot(dsb, k, preferred_element_type=F32)
                if diagonal:
                    dq_ref[:, hh * QW:(hh + 1) * QW] = dq
                else:
                    dq_s[:, hh * QW:(hh + 1) * QW] = dq

        @pl.when(i != j)
        def _():
            step(False)

        @pl.when(i == j)
        def _():
            step(True)

    qside = lambda g, p, qi, kj: (qi[p], g)
    kside = lambda g, p, qi, kj: (kj[p], g)
    whole = lambda g, p, qi, kj: (0, g)
    return pl.pallas_call(
        body, name=name,
        out_shape=(jax.ShapeDtypeStruct((S, H * QW), F32), jax.ShapeDtypeStruct((S, H * QW), F32),
                   jax.ShapeDtypeStruct((S, H * VW), F32)),
        grid_spec=pltpu.PrefetchScalarGridSpec(
            num_scalar_prefetch=2, grid=(H // HB, qi.shape[0]),
            in_specs=[pl.BlockSpec((t, HB * QW), qside), pl.BlockSpec((t, HB * QW), kside), pl.BlockSpec((t, HB * VW), kside),
                      pl.BlockSpec((t, HB * VW), qside), pl.BlockSpec((t, HB * VW), qside),
                      pl.BlockSpec((HB, t, LANES), lambda g, p, qi, kj: (g, qi[p], 0))],
            out_specs=(pl.BlockSpec((t, HB * QW), qside), pl.BlockSpec((S, HB * QW), whole), pl.BlockSpec((S, HB * VW), whole)),
            scratch_shapes=[pltpu.VMEM((t, HB * QW), F32)]),
        compiler_params=_params("parallel", "arbitrary"),
    )(qi, kj, qp, kp, v, o, do, lse)


def _swa_kv_halves(blk, hf, lo):
    if hf == 0:
        a = jnp.where(lo, blk, 0.0)
        b = pltpu.roll(a, 64, 1)
    else:
        b = jnp.where(lo, 0.0, blk)
        a = pltpu.roll(b, 64, 1)
    return a.astype(BF16), b.astype(BF16)


def _swa_softmax(qs, kx, bias, neg0, sk):
    s = lax.dot_general(qs, kx, (((1,), (1,)), ((), ())), preferred_element_type=F32) + bias + neg0
    m = jnp.maximum(jnp.max(s, axis=1, keepdims=True), sk)
    e = jnp.exp(s - m)
    es = jnp.exp(sk - m)
    inv = 1.0 / (jnp.sum(e, axis=1, keepdims=True) + es)
    return e * inv, es * inv


def _swa_stack(ref, kvh, npb, scale=None):
    parts = [ref[:, (kvh * npb + pb) * LANES:(kvh * npb + pb + 1) * LANES] for pb in range(npb)]
    x = jnp.concatenate(parts, axis=0)
    return x if scale is None else x * scale


def _swa_sink_col(sink_ref, kvh, e, npb):
    row = lax.broadcasted_iota(jnp.int32, (npb * SWA_BLOCK, 1), 0)
    col = jnp.zeros((npb * SWA_BLOCK, 1), F32) + sink_ref[2 * (kvh * npb) + e]
    for pb in range(1, npb):
        col = jnp.where(row >= pb * SWA_BLOCK, sink_ref[2 * (kvh * npb + pb) + e], col)
    return col


def _swa_fwd(z_swa, bias_st, sinks, *, name):
    S, W = z_swa.shape
    npb = bias_st.shape[1] // SWA_BLOCK
    NH = 2 * SWA_KVH * npb
    QW = NH * SWA_HD
    KW = SWA_KVH * SWA_HD
    nb = S // SWA_BLOCK
    B = SWA_BLOCK
    assert SWA_KVH % 2 == 0 and W == QW + 2 * KW

    def body(sink_ref, q_ref, kvc_ref, kvp_ref, b_ref, o_ref):
        n = pl.program_id(0)
        lo = lax.broadcasted_iota(jnp.int32, (2 * B, LANES), 1) < 64
        col = lax.broadcasted_iota(jnp.int32, (npb * B, 2 * B), 1)
        neg0 = jnp.where(jnp.logical_and(col < B, n == 0), NEG, 0.0)
        for kb in range(SWA_KVH // 2):
            kblk = jnp.concatenate([kvp_ref[:, kb * LANES:(kb + 1) * LANES], kvc_ref[:, kb * LANES:(kb + 1) * LANES]], axis=0)
            vblk = jnp.concatenate([kvp_ref[:, KW + kb * LANES:KW + (kb + 1) * LANES],
                                    kvc_ref[:, KW + kb * LANES:KW + (kb + 1) * LANES]], axis=0)
            for hf in range(2):
                kvh = 2 * kb + hf
                ks = _swa_kv_halves(kblk, hf, lo)
                vs = _swa_kv_halves(vblk, hf, lo)
                qs = _swa_stack(q_ref, kvh, npb, SWA_HD ** -0.5).astype(BF16)
                acc = jnp.zeros((npb * B, LANES), F32)
                for e in range(2):
                    p, _ = _swa_softmax(qs, ks[e], b_ref[2 * kvh + e], neg0, _swa_sink_col(sink_ref, kvh, e, npb))
                    acc = acc + jnp.dot(p.astype(BF16), vs[e], preferred_element_type=F32)
                for pb in range(npb):
                    P = kvh * npb + pb
                    o_ref[:, P * LANES:(P + 1) * LANES] = acc[pb * B:(pb + 1) * B]

    kvcol = QW // (2 * KW)
    assert QW % (2 * KW) == 0
    return pl.pallas_call(
        body, name=name,
        out_shape=jax.ShapeDtypeStruct((S, QW), F32), grid=(nb,),
        in_specs=[SMEM_FULL, pl.BlockSpec((B, QW), lambda n: (n, 0)), pl.BlockSpec((B, 2 * KW), lambda n: (n, kvcol)),
                  pl.BlockSpec((B, 2 * KW), lambda n: (jnp.maximum(n - 1, 0), kvcol)),
                  pl.BlockSpec(bias_st.shape, lambda n: (0, 0, 0))],
        out_specs=pl.BlockSpec((B, QW), lambda n: (n, 0)),
        compiler_params=_params("parallel"),
    )(sinks, z_swa, z_swa, z_swa, bias_st)


def _swa_bwd(z_swa, bias_st, sinks, o, do, *, name):
    S, W = z_swa.shape
    npb = bias_st.shape[1] // SWA_BLOCK
    NH = 2 * SWA_KVH * npb
    QW = NH * SWA_HD
    KW = SWA_KVH * SWA_HD
    nb = S // SWA_BLOCK
    B = SWA_BLOCK
    scale = SWA_HD ** -0.5
    tn = (((0,), (0,)), ((), ()))
    nt = (((1,), (1,)), ((), ()))

    def fold(x, hf, lo):
        x = x + pltpu.roll(x, 64, 1)
        return jnp.where(lo, x, 0.0) if hf == 0 else jnp.where(lo, 0.0, x)

    def body(sink_ref, q_ref, kvc_ref, kvp_ref, b_ref, o_ref, do_ref, dz_ref, dbias_ref, dsink_ref,
             cq_s, ck_s, cv_s, nq_s, nk_s, nv_s, pk_s, pv_s):
        n = pl.program_id(0)

        @pl.when(n == 0)
        def _():
            dbias_ref[...] = jnp.zeros_like(dbias_ref)
            dsink_ref[...] = jnp.zeros_like(dsink_ref)
            cq_s[...] = jnp.zeros_like(cq_s)
            ck_s[...] = jnp.zeros_like(ck_s)
            cv_s[...] = jnp.zeros_like(cv_s)

        @pl.when(n == nb)
        def _():
            pk_s[...] = jnp.zeros_like(pk_s)
            pv_s[...] = jnp.zeros_like(pv_s)

        @pl.when(n < nb)
        def _():
            lo = lax.broadcasted_iota(jnp.int32, (2 * B, LANES), 1) < 64
            lo1 = lax.broadcasted_iota(jnp.int32, (npb * B, LANES), 1) < 64
            lane1 = lax.broadcasted_iota(jnp.int32, (1, LANES), 1)
            col = lax.broadcasted_iota(jnp.int32, (npb * B, 2 * B), 1)
            neg0 = jnp.where(jnp.logical_and(col < B, n == 0), NEG, 0.0)
            dsink = jnp.zeros((1, LANES), F32)
            for kb in range(SWA_KVH // 2):
                kblk = jnp.concatenate([kvp_ref[:, kb * LANES:(kb + 1) * LANES], kvc_ref[:, kb * LANES:(kb + 1) * LANES]], axis=0)
                vblk = jnp.concatenate([kvp_ref[:, KW + kb * LANES:KW + (kb + 1) * LANES],
                                        kvc_ref[:, KW + kb * LANES:KW + (kb + 1) * LANES]], axis=0)
                dkblk = jnp.zeros((2 * B, LANES), F32)
                dvblk = jnp.zeros((2 * B, LANES), F32)
                for hf in range(2):
                    kvh = 2 * kb + hf
                    ks = _swa_kv_halves(kblk, hf, lo)
                    vs = _swa_kv_halves(vblk, hf, lo)
                    qs = _swa_stack(q_ref, kvh, npb, scale).astype(BF16)
                    dos = _swa_stack(do_ref, kvh, npb)
                    prod = dos * _swa_stack(o_ref, kvh, npb)
                    dob = dos.astype(BF16)
                    dkj = jnp.zeros((2 * B, LANES), F32)
                    dvj = jnp.zeros((2 * B, LANES), F32)
                    dqs = jnp.zeros((npb * B, LANES), F32)
                    for e in range(2):
                        keep = lo1 if e == 0 else jnp.logical_not(lo1)
                        p, psink = _swa_softmax(qs, ks[e], b_ref[2 * kvh + e], neg0, _swa_sink_col(sink_ref, kvh, e, npb))
                        delta = jnp.sum(jnp.where(keep, prod, 0.0), axis=1, keepdims=True)
                        dp = lax.dot_general(dob, vs[e], nt, preferred_element_type=F32)
                        ds = p * (dp - delta)
                        dbias_ref[2 * kvh + e] += ds
                        pd = psink * delta
                        for pb in range(npb):
                            dsh = -jnp.sum(pd[pb * B:(pb + 1) * B], axis=0, keepdims=True)
                            dsink = dsink + jnp.where(lane1 == 2 * (kvh * npb + pb) + e, dsh, 0.0)
                        dsb = ds.astype(BF16)
                        dqs = dqs + jnp.dot(dsb, ks[e], preferred_element_type=F32)
                        keep2 = lo if e == 0 else jnp.logical_not(lo)
                        dkj = dkj + jnp.where(keep2, lax.dot_general(dsb, qs, tn, preferred_element_type=F32), 0.0)
                        dvj = dvj + jnp.where(keep2, lax.dot_general(p.astype(BF16), dob, tn, preferred_element_type=F32), 0.0)
                    for pb in range(npb):
                        P = kvh * npb + pb
                        nq_s[:, P * LANES:(P + 1) * LANES] = dqs[pb * B:(pb + 1) * B] * scale
                    dkblk = dkblk + fold(dkj, hf, lo)
                    dvblk = dvblk + fold(dvj, hf, lo)
                pk_s[:, kb * LANES:(kb + 1) * LANES] = dkblk[:B]
                nk_s[:, kb * LANES:(kb + 1) * LANES] = dkblk[B:]
                pv_s[:, kb * LANES:(kb + 1) * LANES] = dvblk[:B]
                nv_s[:, kb * LANES:(kb + 1) * LANES] = dvblk[B:]
            dsink_ref[...] += dsink

        dz_ref[:, :QW] = cq_s[...].astype(BF16)
        dz_ref[:, QW:QW + KW] = (ck_s[...] + pk_s[...]).astype(BF16)
        dz_ref[:, QW + KW:] = (cv_s[...] + pv_s[...]).astype(BF16)

        @pl.when(n < nb)
        def _():
            cq_s[...] = nq_s[...]
            ck_s[...] = nk_s[...]
            cv_s[...] = nv_s[...]

    kvcol = QW // (2 * KW)
    cur = lambda n: (jnp.minimum(n, nb - 1), 0)
    return pl.pallas_call(
        body, name=name,
        out_shape=(jax.ShapeDtypeStruct((S, W), BF16), jax.ShapeDtypeStruct(bias_st.shape, F32),
                   jax.ShapeDtypeStruct((1, LANES), F32)),
        grid=(nb + 1,),
        in_specs=[SMEM_FULL, pl.BlockSpec((B, QW), cur), pl.BlockSpec((B, 2 * KW), lambda n: (jnp.minimum(n, nb - 1), kvcol)),
                  pl.BlockSpec((B, 2 * KW), lambda n: (jnp.maximum(jnp.minimum(n, nb - 1) - 1, 0), kvcol)),
                  pl.BlockSpec(bias_st.shape, lambda n: (0, 0, 0)), pl.BlockSpec((B, QW), cur), pl.BlockSpec((B, QW), cur)],
        out_specs=(pl.BlockSpec((B, W), lambda n: (jnp.maximum(n - 1, 0), 0)),
                   pl.BlockSpec(bias_st.shape, lambda n: (0, 0, 0)), pl.BlockSpec((1, LANES), lambda n: (0, 0))),
        scratch_shapes=[pltpu.VMEM((B, QW), F32), pltpu.VMEM((B, KW), F32), pltpu.VMEM((B, KW), F32),
                        pltpu.VMEM((B, QW), F32), pltpu.VMEM((B, KW), F32), pltpu.VMEM((B, KW), F32),
                        pltpu.VMEM((B, KW), F32), pltpu.VMEM((B, KW), F32)],
        compiler_params=_params("arbitrary"),
    )(sinks, z_swa, z_swa, z_swa, bias_st, o, do)


def _gate_fwd(zg, o_a, o_b, *, name):
    S, D = o_a.shape
    tr = min(S, 512)

    def body(z_ref, a_ref, b_ref, m_ref):
        ga = jax.nn.sigmoid(z_ref[:, :PAIR])
        gb = jax.nn.sigmoid(z_ref[:, PAIR:])
        m_ref[...] = (ga * a_ref[...] + gb * b_ref[...]).astype(BF16)

    col = pl.BlockSpec((tr, PAIR), lambda i, j: (i, j))
    return pl.pallas_call(
        body, name=name, out_shape=jax.ShapeDtypeStruct((S, D), BF16), grid=(S // tr, D // PAIR),
        in_specs=[pl.BlockSpec((tr, 2 * PAIR), lambda i, j: (i, j)), col, col], out_specs=col,
        compiler_params=_params("parallel", "parallel"),
    )(zg, o_a, o_b)


def _gate_bwd(dmix, zg, o_a, o_b, *, name):
    S, D = o_a.shape
    tr = min(S, 512)

    def body(d_ref, z_ref, a_ref, b_ref, da_ref, db_ref, dz_ref):
        d = d_ref[...]
        ga = jax.nn.sigmoid(z_ref[:, :PAIR])
        gb = jax.nn.sigmoid(z_ref[:, PAIR:])
        da_ref[...] = (d * ga).astype(BF16)
        db_ref[...] = d * gb
        dz_ref[:, :PAIR] = (d * a_ref[...] * (ga * (1.0 - ga))).astype(BF16)
        dz_ref[:, PAIR:] = (d * b_ref[...] * (gb * (1.0 - gb))).astype(BF16)

    col = pl.BlockSpec((tr, PAIR), lambda i, j: (i, j))
    wide = pl.BlockSpec((tr, 2 * PAIR), lambda i, j: (i, j))
    return pl.pallas_call(
        body, name=name,
        out_shape=(jax.ShapeDtypeStruct((S, D), BF16), jax.ShapeDtypeStruct((S, D), F32), jax.ShapeDtypeStruct((S, 2 * D), BF16)),
        grid=(S // tr, D // PAIR), in_specs=[col, wide, col, col], out_specs=(col, col, wide),
        compiler_params=_params("parallel", "parallel"),
    )(dmix, zg, o_a, o_b)


def _conv_u(t_ref, prev_ref, w_ref, b_ref, m, i):
    cur = t_ref[m].astype(F32)
    live = (i > 0).astype(F32)
    p6 = prev_ref[m, 14:15, :].astype(F32) * live
    p7 = prev_ref[m, 15:16, :].astype(F32) * live
    row = lax.broadcasted_iota(jnp.int32, cur.shape, 0)
    t1 = jnp.where(row == 0, p7, pltpu.roll(cur, 1, 0))
    t2 = jnp.where(row == 0, p6, jnp.where(row == 1, p7, pltpu.roll(cur, 2, 0)))
    u = ((b_ref[m] + w_ref[m, 0:1, :] * t2) + w_ref[m, 1:2, :] * t1) + w_ref[m, 2:3, :] * cur
    return u, cur, t1, t2


def _conv_specs(tr, tc):
    blk = pl.BlockSpec((2, tr, tc), lambda p, j, i: (p, i, j))
    prev = pl.BlockSpec((2, 16, tc), lambda p, j, i: (p, jnp.maximum(i * (tr // 16) - 1, 0), j))
    w3 = pl.BlockSpec((2, 3, tc), lambda p, j, i: (p, 0, j))
    w1 = pl.BlockSpec((2, 1, tc), lambda p, j, i: (p, 0, j))
    return blk, prev, w3, w1


def _conv_gate_fwd(t, cw, cb, *, name):
    _, S, C = t.shape
    tr, tc = min(S, 512), _tile(C, 1536)
    ncol = C // tc
    blk, prev, w3, w1 = _conv_specs(tr, tc)

    def body(t_ref, prev_ref, w_ref, b_ref, a_ref):
        i = pl.program_id(2)
        u1 = _conv_u(t_ref, prev_ref, w_ref, b_ref, 0, i)[0]
        u2 = _conv_u(t_ref, prev_ref, w_ref, b_ref, 1, i)[0]
        a_ref[...] = (jax.nn.silu(u1) * u2).astype(BF16)

    return pl.pallas_call(
        body, name=name, out_shape=jax.ShapeDtypeStruct((S, 2 * C), BF16), grid=(2, ncol, S // tr),
        in_specs=[blk, prev, w3, w1], out_specs=pl.BlockSpec((tr, tc), lambda p, j, i: (i, p * ncol + j)),
        compiler_params=_params("parallel", "parallel", "parallel"),
    )(t, t, cw, cb)


def _conv_gate_bwd(t, da, cw, cb, *, name):
    _, S, C = t.shape
    tr, tc = min(S, 256), _tile(C, 1536)
    ncol = C // tc
    blk, prev, w3, w1 = _conv_specs(tr, tc)

    def body(t_ref, prev_ref, da_ref, w_ref, b_ref, du_ref, dw_ref, db_ref):
        i = pl.program_id(2)
        first = i == 0
        u1, c1, a1, b1 = _conv_u(t_ref, prev_ref, w_ref, b_ref, 0, i)
        u2, c2, a2, b2 = _conv_u(t_ref, prev_ref, w_ref, b_ref, 1, i)
        d = da_ref[...].astype(F32)
        sg = jax.nn.sigmoid(u1)
        du1 = d * u2 * (sg * (1.0 + u1 * (1.0 - sg)))
        du2 = d * (u1 * sg)
        for m, (du, cur, t1, t2) in enumerate(((du1, c1, a1, b1), (du2, c2, a2, b2))):
            du_ref[m] = du.astype(BF16)
            dw = jnp.concatenate([jnp.sum(du * t2, axis=0, keepdims=True), jnp.sum(du * t1, axis=0, keepdims=True),
                                  jnp.sum(du * cur, axis=0, keepdims=True)], axis=0)
            db = jnp.sum(du, axis=0, keepdims=True)

            @pl.when(first)
            def _():
                dw_ref[m] = dw
                db_ref[m] = db

            @pl.when(jnp.logical_not(first))
            def _():
                dw_ref[m] += dw
                db_ref[m] += db

    return pl.pallas_call(
        body, name=name,
        out_shape=(jax.ShapeDtypeStruct(t.shape, BF16), jax.ShapeDtypeStruct(cw.shape, F32), jax.ShapeDtypeStruct(cb.shape, F32)),
        grid=(2, ncol, S // tr),
        in_specs=[blk, prev, pl.BlockSpec((tr, tc), lambda p, j, i: (i, p * ncol + j)), w3, w1], out_specs=(blk, w3, w1),
        compiler_params=_params("parallel", "parallel", "arbitrary"),
    )(t, t, da, cw, cb)


def _conv_bwd_dt(du, cw, *, name):
    _, S, C = du.shape
    tr, tc = min(S, 512), _tile(C, 1536)
    nrow = S // tr
    blk, _, w3, _ = _conv_specs(tr, tc)
    nxt = pl.BlockSpec((2, 16, tc), lambda p, j, i: (p, jnp.minimum((i + 1) * (tr // 16), S // 16 - 1), j))

    def body(d_ref, next_ref, w_ref, dt_ref):
        i = pl.program_id(2)
        live = (i < nrow - 1).astype(F32)
        for m in range(2):
            cur = d_ref[m].astype(F32)
            n0 = next_ref[m, 0:1, :].astype(F32) * live
            n1 = next_ref[m, 1:2, :].astype(F32) * live
            row = lax.broadcasted_iota(jnp.int32, cur.shape, 0)
            d1 = jnp.where(row == tr - 1, n0, pltpu.roll(cur, tr - 1, 0))
            d2 = jnp.where(row == tr - 1, n1, jnp.where(row == tr - 2, n0, pltpu.roll(cur, tr - 2, 0)))
            dt_ref[m] = ((w_ref[m, 2:3, :] * cur + w_ref[m, 1:2, :] * d1) + w_ref[m, 0:1, :] * d2).astype(BF16)

    return pl.pallas_call(
        body, name=name, out_shape=jax.ShapeDtypeStruct(du.shape, BF16), grid=(2, C // tc, nrow),
        in_specs=[blk, nxt, w3], out_specs=blk, compiler_params=_params("parallel", "parallel", "parallel"),
    )(du, du, cw)


def _ada_fwd(c_all, w, b, *, name):
    Bn, D = c_all.shape
    N = w.shape[1]
    tn = _tile(N, 512)

    def body(c_ref, w_ref, b_ref, o_ref):
        o_ref[...] = jnp.dot(jax.nn.silu(c_ref[...]), w_ref[...], preferred_element_type=F32, precision=HIGHEST) + b_ref[...]

    return pl.pallas_call(
        body, name=name, out_shape=jax.ShapeDtypeStruct((Bn, N), F32), grid=(N // tn,),
        in_specs=[pl.BlockSpec((Bn, D), lambda j: (0, 0)), pl.BlockSpec((D, tn), lambda j: (0, j)),
                  pl.BlockSpec((1, tn), lambda j: (0, j))],
        out_specs=pl.BlockSpec((Bn, tn), lambda j: (0, j)), compiler_params=_params("parallel"),
    )(c_all, w, b)


def _ada_bwd(c_all_t, dmod, *, name):
    D, Bn = c_all_t.shape
    N = dmod.shape[1]
    tm = _tile(D, 512, 8)
    tn = _tile(N, 1536)

    def body(c_ref, d_ref, o_ref):
        o_ref[...] = jnp.dot(jax.nn.silu(c_ref[...]), d_ref[...], preferred_element_type=F32, precision=HIGHEST)

    return pl.pallas_call(
        body, name=name, out_shape=jax.ShapeDtypeStruct((D, N), F32), grid=(D // tm, N // tn),
        in_specs=[pl.BlockSpec((tm, Bn), lambda i, j: (i, 0)), pl.BlockSpec((Bn, tn), lambda i, j: (0, j))],
        out_specs=pl.BlockSpec((tm, tn), lambda i, j: (i, j)), compiler_params=_params("parallel", "parallel"),
    )(c_all_t, dmod)


def _adamw(w, g, m, v, *, name):
    R, C = w.shape
    tr = R if R * C <= (1 << 18) else _tile(R, max(8, (1 << 18) // C), 8)

    def body(w_ref, g_ref, m_ref, v_ref, d_ref, nm_ref, nv_ref):
        gv = g_ref[...]
        nm = ADAM_B1 * m_ref[...] + (1.0 - ADAM_B1) * gv
        nv = ADAM_B2 * v_ref[...] + (1.0 - ADAM_B2) * (gv * gv)
        m_hat = nm / (1.0 - ADAM_B1 ** ADAM_STEP)
        v_hat = nv / (1.0 - ADAM_B2 ** ADAM_STEP)
        d_ref[...] = -ADAM_LR * (m_hat / (jnp.sqrt(v_hat) + ADAM_EPS) + ADAM_WD * w_ref[...])
        nm_ref[...] = nm
        nv_ref[...] = nv

    blk = pl.BlockSpec((tr, C), lambda i: (i, 0))
    shp = jax.ShapeDtypeStruct((R, C), F32)
    return pl.pallas_call(
        body, name=name, out_shape=(shp, shp, shp), grid=(R // tr,), in_specs=[blk] * 4, out_specs=(blk,) * 3,
        compiler_params=_params("parallel"),
    )(w, g, m, v)


def _place():
    x, y, c = lax.axis_index("x"), lax.axis_index("y"), lax.axis_index("c")
    return x, y, c, [(1 - x, y), (x, 1 - y), (1 - x, 1 - y)]


def _remote(src, dst, send_sem, recv_sem, dev):
    return pltpu.make_async_remote_copy(src_ref=src, dst_ref=dst, send_sem=send_sem, recv_sem=recv_sem,
                                        device_id=dev, device_id_type=MESH)


def _allgather8(v, *, tie=None, name):
    R, C = v.shape

    def body(v_ref, out_ref, send_sems, recv_sems, local_sem):
        x, y, c, chips = _place()
        me, sibling = (x, y, c), (x, y, 1 - c)

        def rows(px, py, pc):
            return out_ref.at[pl.ds((4 * px + 2 * py + pc) * R, R), :]

        def copy(k, block, to, src=None):
            return _remote(rows(*block) if src is None else src, rows(*block), send_sems.at[k], recv_sems.at[k], to)

        mine = pltpu.make_async_copy(v_ref, rows(*me), local_sem)
        mine.start()
        first = [copy(0, me, sibling, src=v_ref)]
        first += [copy(1 + j, me, (*chip, c), src=v_ref) for j, chip in enumerate(chips)]
        for cp in first:
            cp.start()
        passed = [copy(4 + j, (*chip, c), sibling) for j, chip in enumerate(chips)]
        for j, chip in enumerate(chips):
            copy(1 + j, (*chip, c), me).wait_recv()
            passed[j].start()
        copy(0, sibling, me).wait_recv()
        for j, chip in enumerate(chips):
            copy(4 + j, (*chip, 1 - c), me).wait_recv()
        for cp in first + passed:
            cp.wait_send()
        mine.wait()

    body, tspec, targ = _tied(body, tie)
    out = pl.pallas_call(
        body, name=name, out_shape=jax.ShapeDtypeStruct((N_DEV * R, C), v.dtype),
        in_specs=tspec + [VMEM_FULL], out_specs=VMEM_FULL,
        scratch_shapes=[pltpu.SemaphoreType.DMA((7,)), pltpu.SemaphoreType.DMA((7,)), pltpu.SemaphoreType.DMA],
    )(*targ, v)
    return out.reshape(N_DEV, R, C)


SEM = pl.BlockSpec(memory_space=pltpu.SEMAPHORE)
HBM = pl.BlockSpec(memory_space=pltpu.HBM)
EFFECT = pltpu.SideEffectType.DATAFLOW_SIDE_EFFECTING
DMA_SEM = pltpu.SemaphoreType.DMA(())


def _in_hbm(a):
    return pltpu.with_memory_space_constraint(a, pltpu.HBM)


def _three_halves(land, r2):
    return land.at[pl.ds(0, N_CHIP - 1), pl.ds(0, r2)]


def _slot(chip, swap):
    return (chip % 2) * 2 + chip // 2 if swap else chip


def _gather_start(ws, after, swaps, *, name):
    n = len(ws)
    na = len(after)
    lands = [lax.empty((N_CHIP,) + w.shape, w.dtype) for w in ws]

    def body(*refs):
        w_refs, land_refs = refs[:n], refs[n:2 * n]
        send, recv = refs[2 * n + na:3 * n + na], refs[3 * n + na:4 * n + na]
        token = refs[6 * n + na]
        x, y, c, chips = _place()
        k = 2 * x + y
        for i in range(n):
            r2 = ws[i].shape[0] // 2
            for cx, cy in chips:
                _remote(w_refs[i].at[pl.ds(c * r2, r2)], land_refs[i].at[_slot(k, swaps[i]), pl.ds(c * r2, r2)], send[i], recv[i],
                        (cx, cy, c)).start()
        token[...] = jnp.zeros_like(token)

    outs = pl.pallas_call(
        body, name=name,
        out_shape=[DMA_SEM] * (2 * n) + [pltpu.HBM(w.shape, w.dtype) for w in ws] + [pltpu.HBM(l.shape, l.dtype) for l in lands]
        + [jax.ShapeDtypeStruct((8, LANES), F32)],
        in_specs=[HBM] * (2 * n) + [ANY] * na, out_specs=[SEM] * (2 * n) + [HBM] * (2 * n) + [VMEM_FULL],
        input_output_aliases={i: 2 * n + i for i in range(2 * n)},
        compiler_params=pltpu.CompilerParams(has_side_effects=EFFECT),
    )(*[_in_hbm(w) for w in ws], *[_in_hbm(l) for l in lands], *after)
    return outs[:n], outs[n:2 * n], outs[2 * n:3 * n], outs[3 * n:4 * n], outs[4 * n]


def _gather_forward(send, recv, ws, lands, after, swaps, *, name):
    n = len(ws)

    def body(*refs):
        w_refs, land_refs = refs[:n], refs[n:2 * n]
        send1, recv1 = refs[2 * n:3 * n], refs[3 * n:4 * n]
        send2, recv2 = refs[4 * n + 1 + 2 * n:4 * n + 1 + 3 * n], refs[4 * n + 1 + 3 * n:4 * n + 1 + 4 * n]
        x, y, c, chips = _place()
        sibling = (x, y, 1 - c)
        for i in range(n):
            r2 = ws[i].shape[0] // 2
            win = _three_halves(land_refs[i], r2)
            done = _remote(win, win, send1[i], recv1[i], sibling)
            done.wait_send()
            done.wait_recv()
            for cx, cy in chips:
                got = land_refs[i].at[_slot(2 * cx + cy, swaps[i]), pl.ds(c * r2, r2)]
                _remote(got, got, send2[i], recv2[i], sibling).start()
        token = refs[8 * n + 1]
        token[...] = jnp.zeros_like(token)

    outs = pl.pallas_call(
        body, name=name,
        out_shape=[pltpu.HBM(w.shape, w.dtype) for w in ws] + [pltpu.HBM(l.shape, l.dtype) for l in lands] + [DMA_SEM] * (2 * n)
        + [jax.ShapeDtypeStruct((8, LANES), F32)],
        in_specs=[HBM] * (2 * n) + [SEM] * (2 * n) + [ANY], out_specs=[HBM] * (2 * n) + [SEM] * (2 * n) + [VMEM_FULL],
        input_output_aliases={i: i for i in range(2 * n)},
        compiler_params=pltpu.CompilerParams(has_side_effects=EFFECT),
    )(*ws, *lands, *send, *recv, after)
    return outs[2 * n:3 * n], outs[3 * n:4 * n], outs[n:2 * n], outs[4 * n]


def _gather_finish(send, recv, lands, after, *, name):
    n = len(lands)

    def body(*refs):
        land_refs = refs[:n]
        send2, recv2 = refs[n:2 * n], refs[2 * n:3 * n]
        x, y, c, _ = _place()
        for i in range(n):
            win = _three_halves(land_refs[i], lands[i].shape[1] // 2)
            done = _remote(win, win, send2[i], recv2[i], (x, y, 1 - c))
            done.wait_send()
            done.wait_recv()

    return pl.pallas_call(
        body, name=name,
        out_shape=[pltpu.HBM(l.shape, l.dtype) for l in lands],
        in_specs=[HBM] * n + [SEM] * (2 * n) + [ANY], out_specs=[HBM] * n,
        input_output_aliases={i: i for i in range(n)},
        compiler_params=pltpu.CompilerParams(has_side_effects=EFFECT),
    )(*lands, *send, *recv, after)


def _scatter_start(gs, swaps, *, name):
    n = len(gs)
    lands = [lax.empty((N_DEV, g.shape[1] // 2, g.shape[2]), g.dtype) for g in gs]

    def body(*refs):
        g_refs, land_refs = refs[:n], refs[n:2 * n]
        send, recv = refs[2 * n:3 * n], refs[3 * n:4 * n]
        token = refs[6 * n]
        x, y, c, chips = _place()
        k = 2 * x + y
        me = 2 * k + c
        for i in range(n):
            r2 = gs[i].shape[1] // 2
            for cx, cy in chips:
                for cc in range(2):
                    _remote(g_refs[i].at[_slot(2 * cx + cy, swaps[i]), pl.ds(cc * r2, r2)], land_refs[i].at[me], send[i], recv[i],
                            (cx, cy, cc)).start()
            _remote(g_refs[i].at[_slot(k, swaps[i]), pl.ds((1 - c) * r2, r2)], land_refs[i].at[me], send[i], recv[i],
                    (x, y, 1 - c)).start()
        token[...] = jnp.zeros_like(token)

    outs = pl.pallas_call(
        body, name=name,
        out_shape=[DMA_SEM] * (2 * n) + [pltpu.HBM(g.shape, g.dtype) for g in gs] + [pltpu.HBM(l.shape, l.dtype) for l in lands]
        + [jax.ShapeDtypeStruct((8, LANES), F32)],
        in_specs=[HBM] * (2 * n), out_specs=[SEM] * (2 * n) + [HBM] * (2 * n) + [VMEM_FULL],
        input_output_aliases={i: 2 * n + i for i in range(2 * n)},
        compiler_params=pltpu.CompilerParams(has_side_effects=EFFECT),
    )(*[_in_hbm(g) for g in gs], *[_in_hbm(l) for l in lands])
    return outs[:n], outs[n:2 * n], outs[2 * n:3 * n], outs[3 * n:4 * n], outs[4 * n]


def _scatter_wait(send, recv, gs, lands, after, *, name):
    n = len(gs)

    def body(*refs):
        land_refs = refs[n:2 * n]
        send1, recv1 = refs[2 * n:3 * n], refs[3 * n:4 * n]
        x, y, c, _ = _place()
        for i in range(n):
            win = land_refs[i].at[pl.ds(0, N_DEV - 1)]
            done = _remote(win, win, send1[i], recv1[i], (x, y, 1 - c))
            done.wait_send()
            done.wait_recv()

    outs = pl.pallas_call(
        body, name=name,
        out_shape=[pltpu.HBM(g.shape, g.dtype) for g in gs] + [pltpu.HBM(l.shape, l.dtype) for l in lands],
        in_specs=[HBM] * (2 * n) + [SEM] * (2 * n) + [ANY], out_specs=[HBM] * (2 * n),
        input_output_aliases={i: i for i in range(2 * n)},
        compiler_params=pltpu.CompilerParams(has_side_effects=EFFECT),
    )(*gs, *lands, *send, *recv, after)
    return outs[:n], outs[n:]


def _share_halves(ts, *, name):
    n = len(ts)

    def body(*refs):
        outs = refs[n:2 * n]
        send_sems, recv_sems = refs[2 * n:]
        x, y, c, _ = _place()
        sibling = (x, y, 1 - c)
        cps = []
        for i in range(n):
            r2 = ts[i].shape[0] // 2
            mine = outs[i].at[pl.ds(c * r2, r2)]
            cps.append(_remote(mine, mine, send_sems.at[i], recv_sems.at[i], sibling))
            cps[-1].start()
        for i in range(n):
            r2 = ts[i].shape[0] // 2
            got = outs[i].at[pl.ds((1 - c) * r2, r2)]
            _remote(got, got, send_sems.at[i], recv_sems.at[i], sibling).wait_recv()
        for cp in cps:
            cp.wait_send()

    return pl.pallas_call(
        body, name=name,
        out_shape=[jax.ShapeDtypeStruct(t.shape, t.dtype) for t in ts],
        in_specs=[ANY] * n, out_specs=[ANY] * n, input_output_aliases={i: i for i in range(n)},
        scratch_shapes=[pltpu.SemaphoreType.DMA((n,)), pltpu.SemaphoreType.DMA((n,))],
    )(*ts)


def _sum_pieces(land, g, idx, *, name):
    _, r2, C = land.shape
    tr = _tile(r2, max(16, (1 << 19) // C), 16)
    nr = r2 // tr

    def body(idx_ref, land_ref, own_ref, o_ref, acc_ref):
        d = pl.program_id(1)
        mine = d == idx_ref[0]

        @pl.when(d == 0)
        def _():
            acc_ref[...] = jnp.zeros_like(acc_ref)

        @pl.when(mine)
        def _():
            acc_ref[...] += own_ref[...].astype(F32)

        @pl.when(jnp.logical_not(mine))
        def _():
            acc_ref[...] += land_ref[...].astype(F32)

        @pl.when(d == N_DEV - 1)
        def _():
            o_ref[...] = acc_ref[...]

    return pl.pallas_call(
        body, name=name, out_shape=jax.ShapeDtypeStruct((2 * r2, C), F32),
        grid_spec=pltpu.PrefetchScalarGridSpec(
            num_scalar_prefetch=1, grid=(nr, N_DEV),
            in_specs=[pl.BlockSpec((None, tr, C), lambda i, d, ix: (jnp.where(d == ix[0], (d + 1) % N_DEV, d), i, 0)),
                      pl.BlockSpec((None, tr, C), lambda i, d, ix: (ix[1], ix[2] * nr + i, 0))],
            out_specs=pl.BlockSpec((tr, C), lambda i, d, ix: (ix[2] * nr + i, 0)),
            scratch_shapes=[pltpu.VMEM((tr, C), F32)]),
        compiler_params=_params("parallel", "arbitrary"),
    )(idx, land, g)


def _sum_devices(v, *, name):
    n, R, C = v.shape

    def body(v_ref, o_ref):
        acc = v_ref[0]
        for j in range(1, n):
            acc = acc + v_ref[j]
        o_ref[...] = acc

    return pl.pallas_call(body, name=name, out_shape=jax.ShapeDtypeStruct((R, C), F32),
                          in_specs=[VMEM_FULL], out_specs=VMEM_FULL)(v)


def _shard_cols(shards, lo, hi, width):
    out = []
    while lo < hi:
        j = lo // width
        end = min(hi, (j + 1) * width)
        out.append(shards[j][:, lo - j * width:end - j * width])
        lo = end
    return out


def _from_col_shards(g):
    return jnp.transpose(g, (1, 0, 2)).reshape(g.shape[1], N_CHIP * g.shape[2])


def _to_col_shards(w):
    R, N = w.shape
    return jnp.transpose(w.reshape(R, N_CHIP, N // N_CHIP), (1, 0, 2))


def _split_heads(w, widths):
    R, N = w.shape
    per = sum(widths)
    w3 = w.reshape(R, N // per, per)
    lo = w3[:, :, :widths[0]].reshape(R, -1)
    hi = w3[:, :, widths[0]:].reshape(R, -1)
    return jnp.concatenate([lo, hi], axis=1)


def _merge_heads(w, widths):
    R, N = w.shape
    H = N // sum(widths)
    lo = w[:, :H * widths[0]].reshape(R, H, widths[0])
    hi = w[:, H * widths[0]:].reshape(R, H, widths[1])
    return jnp.concatenate([lo, hi], axis=2).reshape(R, N)


def _t5_bucket(dist):
    max_exact = REL_BUCKETS // 2
    n = jnp.maximum(dist, 0)
    large = max_exact + (jnp.log(jnp.maximum(n, 1).astype(F32) / max_exact)
                         / jnp.log(jnp.asarray(REL_MAX_DIST / max_exact, F32))
                         * (REL_BUCKETS - max_exact)).astype(jnp.int32)
    large = jnp.minimum(large, REL_BUCKETS - 1)
    return jnp.where(n < max_exact, n, large)


def _rel_tables():
    a = jnp.arange(SWA_BLOCK)
    b = jnp.arange(2 * SWA_BLOCK)
    dist = SWA_BLOCK + a[:, None] - b[None, :]
    valid = jnp.logical_and(dist >= 0, dist < SWA_BLOCK)
    onehot = jnp.logical_and(_t5_bucket(dist)[..., None] == jnp.arange(REL_BUCKETS), valid[..., None])
    onehot = onehot.astype(F32).reshape(2 * SWA_BLOCK * SWA_BLOCK, REL_BUCKETS)
    negmask = jnp.where(valid, 0.0, NEG).astype(F32).reshape(1, -1)
    return onehot, negmask


def _rope_tables(S):
    pos = jnp.arange(S, dtype=F32)
    inv = ROPE_THETA ** (-jnp.arange(0, MLA_ROPE, 2, dtype=F32) / MLA_ROPE)
    ang = pos[:, None] * inv[None, :]
    ang = jnp.concatenate([ang, ang, ang, ang], axis=-1)
    return jnp.cos(ang), jnp.sin(ang)


def _flat_pad(parts, rows=8):
    flat = jnp.concatenate([p.reshape(1, -1) for p in parts], axis=1)
    n = flat.shape[1]
    width = -(-n // (rows * LANES)) * LANES
    return jnp.pad(flat, ((0, 0), (0, rows * width - n))).reshape(rows, width)


def _unflat(vec, shapes):
    flat = vec.reshape(-1)
    out, off = [], 0
    for s in shapes:
        n = 1
        for d in s:
            n *= d
        out.append(flat[off:off + n].reshape(s))
        off += n
    return out


def kernel(x, c, w_ada, b_ada, g_pre_mix, g_post_mix, w_in, g_q_lat, w_uq, g_kv_lat, w_ukv, rel_bias, sinks, w_o, g_pre_ffn, g_post_ffn, w_up, conv_w, conv_b, w_down, loss_target, m_w_ada, m_b_ada, m_g_pre_mix, m_g_post_mix, m_w_in, m_g_q_lat, m_w_uq, m_g_kv_lat, m_w_ukv, m_rel_bias, m_sinks, m_w_o, m_g_pre_ffn, m_g_post_ffn, m_w_up, m_conv_w, m_conv_b, m_w_down, v_w_ada, v_b_ada, v_g_pre_mix, v_g_post_mix, v_w_in, v_g_q_lat, v_w_uq, v_g_kv_lat, v_w_ukv, v_rel_bias, v_sinks, v_w_o, v_g_pre_ffn, v_g_post_ffn, v_w_up, v_conv_w, v_conv_b, v_w_down):
    S, D = x.shape[1], x.shape[2]
    Rq, Rkv = g_q_lat.shape[1], g_kv_lat.shape[1]
    H = D // MLA_V
    NH = D // SWA_HD
    KW = SWA_KVH * SWA_HD
    F = w_down.shape[1] * N_CHIP
    xi, yi, ci = lax.axis_index("x"), lax.axis_index("y"), lax.axis_index("c")
    chip = 2 * xi + yi
    me = 2 * chip + ci
    x2, tgt = x[0], loss_target[0]

    c_all = _allgather8(jnp.broadcast_to(c, (8, D)), name="gather_c")[:, 0, :]
    n3 = w_ada.shape[2]
    mod_part = _ada_fwd(c_all, w_ada[0], lax.dynamic_slice(b_ada, (0, chip * n3), (1, n3)), name="ada_fwd")
    mod_all = _allgather8(mod_part, name="gather_mod")
    mod_me = lax.dynamic_index_in_dim(mod_all[0::2], me, axis=1, keepdims=False).reshape(1, 6 * D)
    sh1, sc1, gt1, sh2, sc2, gt2 = [mod_me[:, k * D:(k + 1) * D] for k in range(6)]
    cw_all = _allgather8(jnp.pad(conv_w[0], ((0, 5), (0, 0))), name="gather_conv_w")[0::2, :3]

    big = [w_in[0], w_uq[0], w_ukv[0], w_o[0], w_up[0], w_down[0]]
    local = [w.astype(BF16) for w in big]
    swaps = [False, False, False, False, True, False]
    send_a, recv_a, srcs_a, lands_a, token = _gather_start(local[:1], (mod_all, cw_all), swaps[:1], name="gather_start_in")
    send_b, recv_b, srcs_b, lands_b, token = _gather_start(local[1:], (token,), swaps[1:], name="gather_start_rest")
    send1, recv1, srcs, lands = send_a + send_b, recv_a + recv_b, srcs_a + srcs_b, lands_a + lands_b
    onehot, negmask = _rel_tables()
    npb = NH // (2 * SWA_KVH)
    rb_st = jnp.transpose(rel_bias.T.reshape(SWA_KVH, npb, 2, REL_BUCKETS), (0, 2, 1, 3)).reshape(NH, REL_BUCKETS)
    bias_m = (_matmul(rb_st, onehot.T, tie=token, name="rel_bias_table") + negmask).reshape(
        2 * SWA_KVH, npb * SWA_BLOCK, 2 * SWA_BLOCK)
    h = _modnorm_fwd(x2, g_pre_mix, sc1, sh1, name="pre_mix_norm")

    def whole(land, i):
        return lax.dynamic_update_index_in_dim(land, local[i], _slot(chip, swaps[i]), 0)

    def conv_slots(v):
        return jnp.stack([v[0], v[2], v[1], v[3]])

    s2, r2, l_in, _ = _gather_forward(send1[:1], recv1[:1], srcs[:1], lands[:1], h, swaps[:1], name="gather_forward_in")
    (l_in,) = _gather_finish(s2, r2, l_in, h, name="gather_finish_in")
    gin = whole(l_in, 0)
    o_kr = Rq + Rkv
    o_q = o_kr + MLA_ROPE
    o_g = o_q + NH * SWA_HD + 2 * KW
    n_gate, n_swa = 2 * D, o_g - o_q
    n_lat = -(-(o_q + MLA_ROPE) // PAIR) * PAIR
    runs = []
    for tl in range(D // PAIR):
        runs.append((o_g + tl * PAIR, o_g + (tl + 1) * PAIR, 2 * tl * PAIR))
        runs.append((o_g + D + tl * PAIR, o_g + D + (tl + 1) * PAIR, (2 * tl + 1) * PAIR))
    runs.append((o_q, o_g, n_gate))
    runs.append((0, o_q, n_gate + n_swa))
    csh = gin.shape[2]
    parts = []
    for lo, hi, _ in runs + [(o_kr, o_q, 0)]:
        parts += _shard_cols([gin[j] for j in range(N_CHIP)], lo, hi, csh)
    parts.append(jnp.zeros((D, n_lat - o_q - MLA_ROPE), BF16))
    w_in_all = jnp.concatenate(parts, axis=1)
    cw = conv_slots(cw_all)
    cb = conv_slots(conv_b.reshape(N_CHIP, 1, -1))
    cos, sin = _rope_tables(S)
    sink_v = sinks.reshape(NH)

    z_lat = _matmul(h, w_in_all, bcols=(n_gate + n_swa, n_lat), name="in_proj_lat")
    z_swa = _matmul(h, w_in_all, bcols=(n_gate, n_swa), name="in_proj_swa")
    zg = _matmul(h, w_in_all, bcols=(0, n_gate), name="in_proj_gate")
    s2b, r2b, l_b, _ = _gather_forward(send1[1:4], recv1[1:4], srcs[1:4], lands[1:4], zg, swaps[1:4],
                                       name="gather_forward_attn")
    nq, nkv = _lat_norm_fwd(z_lat, g_q_lat, g_kv_lat, name="lat_norm")
    l_uq, l_ukv, l_o = _gather_finish(s2b, r2b, l_b, nq, name="gather_finish_attn")
    wuq = _split_heads(_from_col_shards(whole(l_uq, 1)), (MLA_NOPE, MLA_ROPE))
    wukv = _split_heads(_from_col_shards(whole(l_ukv, 2)), (MLA_NOPE, MLA_V))
    wo = whole(l_o, 3).reshape(D, D)
    q_raw = _matmul(nq, wuq, name="uq_proj")
    kv_raw = _matmul(nkv, wukv, name="ukv_proj")
    qp, kp, vv = _mla_pack_fwd(q_raw, kv_raw, z_lat, cos, sin, o_kr, name="mla_pack")
    o_a, lse = _flash_fwd(qp, kp, vv, name="mla_attn")
    s2c, r2c, l_c, tok_c = _gather_forward(send1[4:], recv1[4:], srcs[4:], lands[4:], o_a, swaps[4:],
                                           name="gather_forward_ffn")
    o_b = _swa_fwd(z_swa, bias_m, sink_v, name="swa_attn")
    mixin = _gate_fwd(zg, o_a, o_b, name="gate_mix")
    mix = _matmul(mixin, wo, tie=tok_c, name="o_proj")
    x1 = _resnorm_fwd(x2, mix, g_post_mix, gt1, name="post_mix_norm")
    h2 = _modnorm_fwd(x1, g_pre_ffn, sc2, sh2, name="pre_ffn_norm")
    l_up, l_down = _gather_finish(s2c, r2c, l_c, h2, name="gather_finish_ffn")
    wup = whole(l_up, 4)
    wdown = whole(l_down, 5).reshape(F, D)
    t = _matmul(h2, wup, out_dtype=BF16, shards="out", name="up_proj")
    a = _conv_gate_fwd(t, cw, cb, name="conv_gate")
    yv = _matmul(a, wdown, name="down_proj")
    dout, loss_tile = _resnorm_loss(x1, yv, g_post_ffn, gt2, tgt, name="post_ffn_norm_loss")

    big_params = dict(w_in=(w_in, m_w_in, v_w_in), w_uq=(w_uq, m_w_uq, v_w_uq), w_ukv=(w_ukv, m_w_ukv, v_w_ukv),
                      w_o=(w_o, m_w_o, v_w_o), w_up=(w_up, m_w_up, v_w_up), w_down=(w_down, m_w_down, v_w_down))
    res = {}

    def start(nms, gs):
        sw = [nm == "w_up" for nm in nms]
        send, recv, gsrc, glands, tok = _scatter_start(gs, sw, name="grads_start_" + nms[0])
        return (nms, send, recv, gsrc, glands), tok

    def finish(pending, after):
        nms, send, recv, gsrc, glands = pending
        gsrc, glands = _scatter_wait(send, recv, gsrc, glands, after, name="grads_wait_" + nms[0])
        halves = [_sum_pieces(l, g, jnp.stack([me, _slot(chip, nm == "w_up"), ci]).astype(jnp.int32), name="grad_sum_" + nm)
                  for l, g, nm in zip(glands, gsrc, nms)]
        for nm, g in zip(nms, _share_halves(halves, name="grads_share_" + nms[0])):
            w, m, v = big_params[nm]
            res[nm] = (g,) + tuple(_adamw(w[0], g, m[0], v[0], name="adamw_" + nm))

    dy, dg_post_ffn, dgt2 = _resnorm_bwd(dout, yv, g_post_ffn, gt2, name="post_ffn_norm_bwd")
    dw_down = _matmul(a, dy, ta=True, out_dtype=BF16, name="down_proj_dw")
    p_down, tok = start(["w_down"], [dw_down.reshape(N_CHIP, F // N_CHIP, D)])
    da = _matmul(dy, wdown, tb=True, out_dtype=BF16, tie=tok, name="down_proj_dx")
    du, dcw_s, dcb_s = _conv_gate_bwd(t, da, cw, cb, name="conv_gate_bwd")
    dt = _conv_bwd_dt(du, cw, name="conv_bwd_dt")
    dw_up = _matmul(h2, dt, ta=True, out_dtype=BF16, shards="out", name="up_proj_dw")
    p_up, tok = start(["w_up"], [dw_up])
    dh2 = _matmul(dt, wup, tb=True, tie=tok, shards="k", name="up_proj_dx")
    dx1, dg_pre_ffn, dsc2, dsh2 = _modnorm_bwd(dh2, x1, g_pre_ffn, sc2, dout, name="pre_ffn_norm_bwd")
    dmix, dg_post_mix, dgt1 = _resnorm_bwd(dx1, mix, g_post_mix, gt1, name="post_mix_norm_bwd")
    dw_o = _matmul(mixin, dmix, ta=True, out_dtype=BF16, name="o_proj_dw")
    p_o, tok = start(["w_o"], [dw_o.reshape(N_CHIP, D // N_CHIP, D)])
    dmixin = _matmul(dmix, wo, tb=True, tie=tok, name="o_proj_dx")
    do_a, do_b, dzg = _gate_bwd(dmixin, zg, o_a, o_b, name="gate_mix_bwd")
    dqp, dkp, dvv = _flash_bwd(qp, kp, vv, o_a, do_a, lse, name="mla_attn_bwd")
    dq_raw, dkv_raw, dkr = _mla_pack_bwd(dqp, dkp, dvv, cos, sin, name="mla_pack_bwd")
    dw_uq_p = _matmul(nq, dq_raw, ta=True, out_dtype=BF16, name="uq_proj_dw")
    dw_ukv_p = _matmul(nkv, dkv_raw, ta=True, out_dtype=BF16, name="ukv_proj_dw")
    p_qkv, tok = start(["w_uq", "w_ukv"], [_to_col_shards(_merge_heads(dw_uq_p, (MLA_NOPE, MLA_ROPE))),
                                           _to_col_shards(_merge_heads(dw_ukv_p, (MLA_NOPE, MLA_V)))])
    dnq = _matmul(dq_raw, wuq, tb=True, tie=tok, name="uq_proj_dx")
    dnkv = _matmul(dkv_raw, wukv, tb=True, name="ukv_proj_dx")
    dz_lat, dg_q, dg_kv = _lat_norm_bwd(z_lat, dnq, dnkv, dkr, g_q_lat, g_kv_lat, name="lat_norm_bwd")
    dz_swa, dbias, dsink = _swa_bwd(z_swa, bias_m, sink_v, o_b, do_b, name="swa_attn_bwd")
    dz = jnp.concatenate([dzg, dz_swa, dz_lat], axis=1)
    dw_in_p = _matmul(h, dz, ta=True, out_dtype=BF16, name="in_proj_dw")
    dw_shards = []
    for j in range(N_CHIP):
        cols = []
        for lo, hi, at in sorted(runs):
            a0, a1 = max(lo, j * csh), min(hi, (j + 1) * csh)
            if a0 < a1:
                cols.append(dw_in_p[:, at + a0 - lo:at + a1 - lo])
        dw_shards.append(jnp.concatenate(cols, axis=1))
    p_in, tok = start(["w_in"], [jnp.stack(dw_shards)])
    dh = _matmul(dz, w_in_all, tb=True, tie=tok, name="in_proj_dx")
    grad_x, dg_pre_mix, dsc1, dsh1 = _modnorm_bwd(dh, x2, g_pre_mix, sc1, dx1, name="pre_mix_norm_bwd")
    drel_st = _matmul(dbias.reshape(NH, -1), onehot, tie=grad_x, name="rel_bias_bwd")
    for pending in (p_down, p_up, p_o, p_qkv):
        finish(pending, drel_st)
    drel = jnp.transpose(drel_st.reshape(SWA_KVH, 2, npb, REL_BUCKETS), (0, 2, 1, 3)).reshape(NH, REL_BUCKETS).T

    dcw = _from_col_shards(conv_slots(dcw_s))
    dcb = conv_slots(dcb_s).reshape(1, -1)
    dmod = jnp.concatenate([dsh1, dsc1, dgt1, dsh2, dsc2, dgt2], axis=1)
    small = [dmod, dg_pre_mix, dg_post_mix, dg_pre_ffn, dg_post_ffn, dg_q, dg_kv, drel, dsink[:, :NH], dcb, dcw]
    shapes = [p.shape for p in small]
    done = [res[nm][1] for nm in ("w_down", "w_up", "w_o", "w_uq", "w_ukv")]
    small_all = _allgather8(_flat_pad(small), tie=done, name="gather_small_grads")
    tot = _unflat(_sum_devices(small_all, name="sum_small_grads"), shapes)
    g_b_ada, g_pre_mix_g, g_post_mix_g, g_pre_ffn_g, g_post_ffn_g, g_q_g, g_kv_g, g_rel, g_sinks, g_cb, g_cw_full = tot
    dmod_all = small_all.reshape(N_DEV, -1)[:, :6 * D]
    g_w_ada = _ada_bwd(c_all.T, lax.dynamic_slice(dmod_all, (0, chip * n3), (N_DEV, n3)), name="ada_bwd")
    ncw = conv_w.shape[2]
    g_cw = lax.dynamic_slice(g_cw_full, (0, chip * ncw), (3, ncw))

    res["w_ada"] = (g_w_ada,) + tuple(_adamw(w_ada[0], g_w_ada, m_w_ada[0], v_w_ada[0], name="adamw_w_ada"))
    finish(p_in, g_w_ada)
    snames = ["b_ada", "g_pre_mix", "g_post_mix", "g_pre_ffn", "g_post_ffn", "g_q_lat", "g_kv_lat", "rel_bias", "sinks",
              "conv_b", "conv_w"]
    sw = [b_ada, g_pre_mix, g_post_mix, g_pre_ffn, g_post_ffn, g_q_lat, g_kv_lat, rel_bias, sinks, conv_b, conv_w]
    sm = [m_b_ada, m_g_pre_mix, m_g_post_mix, m_g_pre_ffn, m_g_post_ffn, m_g_q_lat, m_g_kv_lat, m_rel_bias, m_sinks,
          m_conv_b, m_conv_w]
    sv = [v_b_ada, v_g_pre_mix, v_g_post_mix, v_g_pre_ffn, v_g_post_ffn, v_g_q_lat, v_g_kv_lat, v_rel_bias, v_sinks,
          v_conv_b, v_conv_w]
    sg = [g_b_ada, g_pre_mix_g, g_post_mix_g, g_pre_ffn_g, g_post_ffn_g, g_q_g, g_kv_g, g_rel, g_sinks, g_cb, g_cw]
    sshapes = [w.shape for w in sw]
    sd, snm, snv = _adamw(_flat_pad(sw), _flat_pad(sg), _flat_pad(sm), _flat_pad(sv), name="adamw_small")
    sd, snm, snv = _unflat(sd, sshapes), _unflat(snm, sshapes), _unflat(snv, sshapes)
    for k, nm in enumerate(snames):
        res[nm] = (sg[k].reshape(sshapes[k]), sd[k], snm[k], snv[k])

    order = ["w_ada", "b_ada", "g_pre_mix", "g_post_mix", "w_in", "g_q_lat", "w_uq", "g_kv_lat", "w_ukv", "rel_bias", "sinks",
             "w_o", "g_pre_ffn", "g_post_ffn", "w_up", "conv_w", "conv_b", "w_down"]
    ref_shapes = dict(w_ada=w_ada.shape, w_in=w_in.shape, w_uq=w_uq.shape, w_ukv=w_ukv.shape, w_o=w_o.shape,
                      w_up=w_up.shape, w_down=w_down.shape)
    outs = []
    for k in range(4):
        for nm in order:
            arr = res[nm][k]
            outs.append(arr.reshape(ref_shapes[nm]) if nm in ref_shapes else arr)
    loss = lax.psum(loss_tile[0, 0], ("x", "y", "c"))
    return (loss, grad_x[None], *outs)
```

```python
import math

import jax
import jax.numpy as jnp
from jax import lax
from jax.experimental import pallas as pl
from jax.experimental.pallas import tpu as pltpu

F32 = jnp.float32
BF16 = jnp.bfloat16
MESH = pl.DeviceIdType.MESH
HIGHEST = lax.Precision.HIGHEST

N_DEV = 8
N_CHIP = 4
LANES = 128
MLA_NOPE = 128
MLA_ROPE = 64
MLA_V = 128
MLA_QK = MLA_NOPE + MLA_ROPE
MLA_QK_PAD = 256
ROPE_THETA = 10000.0
SWA_HD = 64
SWA_KVH = 4
SWA_BLOCK = 128
REL_BUCKETS = 32
REL_MAX_DIST = 128
PAIR = 512
EPS = 1e-6
NEG = -1e30
ADAM_LR = 0.001
ADAM_B1 = 0.9
ADAM_B2 = 0.999
ADAM_EPS = 1e-08
ADAM_WD = 0.01
ADAM_STEP = 10

ANY = pl.BlockSpec(memory_space=pl.ANY)
VMEM_FULL = pl.BlockSpec(memory_space=pltpu.VMEM)
SMEM_FULL = pl.BlockSpec(memory_space=pltpu.SMEM)


def _params(*sem):
    return pltpu.CompilerParams(dimension_semantics=sem if sem else None)


def _tied(body, tie):
    if tie is None:
        return body, [], []
    ties = list(tie) if isinstance(tie, (list, tuple)) else [tie]

    def tied_body(*refs):
        body(*refs[len(ties):])

    return tied_body, [ANY] * len(ties), ties


def _tile(n, pref, unit=LANES):
    best = None
    for t in range(unit, min(n, pref) + 1, unit):
        if n % t == 0:
            best = t
    return n if best is None else best


def _matmul(a, b, *, ta=False, tb=False, out_dtype=F32, tie=None, shards=None, bcols=None, name):
    a2 = a.shape[1:] if shards == "k" else a.shape
    b2 = b.shape[1:] if shards else b.shape
    nsh = b.shape[0] if shards else 1
    K, M = a2 if ta else a2[::-1]
    N, K2 = b2 if tb else b2[::-1]
    assert K == K2, (a.shape, b.shape, ta, tb)
    exact = a.dtype == F32
    col0 = 0
    if bcols is not None:
        assert not tb and shards is None
        col0, N = bcols
    tn = _tile(math.gcd(N, col0) if col0 else N, 1536)
    col0 //= tn
    tk = _tile(K, 2048)
    nkc = K // tk
    nk = nkc * (nsh if shards == "k" else 1)
    tm = M if M < 8 else _tile(M, 1024, LANES if ta else 8)
    dn = (((0 if ta else 1,), (1 if tb else 0,)), ((), ()))
    kax = 3 if shards == "out" else 2

    def product(a_ref, b_ref):
        return lax.dot_general(a_ref[...], b_ref[...], dn, preferred_element_type=F32,
                               precision=HIGHEST if exact else None)

    def body_acc(a_ref, b_ref, o_ref, acc_ref):
        k = pl.program_id(kax)

        @pl.when(k == 0)
        def _():
            acc_ref[...] = product(a_ref, b_ref)

        @pl.when(jnp.logical_and(k > 0, k < nk - 1))
        def _():
            acc_ref[...] += product(a_ref, b_ref)

        @pl.when(k == nk - 1)
        def _():
            o_ref[...] = (acc_ref[...] + product(a_ref, b_ref)).astype(o_ref.dtype)

    def body_one(a_ref, b_ref, o_ref):
        o_ref[...] = product(a_ref, b_ref).astype(o_ref.dtype)

    a_blk, b_blk = ((tk, tm) if ta else (tm, tk)), ((tn, tk) if tb else (tk, tn))
    a_at = (lambda i, k: (k, i)) if ta else (lambda i, k: (i, k))
    b_at = (lambda j, k: (j, k)) if tb else (lambda j, k: (k, j + col0))
    if shards == "out":
        grid = (nsh, M // tm, N // tn, nk)
        a_spec = pl.BlockSpec(a_blk, lambda s, i, j, k: a_at(i, k))
        b_spec = pl.BlockSpec((None,) + b_blk, lambda s, i, j, k: (s,) + b_at(j, k))
        o_spec = pl.BlockSpec((None, tm, tn), lambda s, i, j, k: (s, i, j))
        out_shape = jax.ShapeDtypeStruct((nsh, M, N), out_dtype)
        sem = ("parallel", "parallel", "parallel", "arbitrary")
    elif shards == "k":
        grid = (M // tm, N // tn, nk)
        a_spec = pl.BlockSpec((None,) + a_blk, lambda i, j, k: (k // nkc,) + a_at(i, k % nkc))
        b_spec = pl.BlockSpec((None,) + b_blk, lambda i, j, k: (k // nkc,) + b_at(j, k % nkc))
        o_spec = pl.BlockSpec((tm, tn), lambda i, j, k: (i, j))
        out_shape = jax.ShapeDtypeStruct((M, N), out_dtype)
        sem = ("parallel", "parallel", "arbitrary")
    else:
        grid = (M // tm, N // tn, nk)
        a_spec = pl.BlockSpec(a_blk, lambda i, j, k: a_at(i, k))
        b_spec = pl.BlockSpec(b_blk, lambda i, j, k: b_at(j, k))
        o_spec = pl.BlockSpec((tm, tn), lambda i, j, k: (i, j))
        out_shape = jax.ShapeDtypeStruct((M, N), out_dtype)
        sem = ("parallel", "parallel", "arbitrary")
    body, tspec, targ = _tied(body_one if nk == 1 else body_acc, tie)
    return pl.pallas_call(
        body, name=name, out_shape=out_shape, grid=grid, in_specs=tspec + [a_spec, b_spec], out_specs=o_spec,
        scratch_shapes=[] if nk == 1 else [pltpu.VMEM((tm, tn), F32)],
        compiler_params=_params(*sem),
    )(*targ, a, b)


def _row_tile(S, width):
    return _tile(S, max(8, (1 << 19) // width), 8)


def _rstd(x):
    return lax.rsqrt(jnp.mean(x * x, axis=-1, keepdims=True) + EPS)


def _acc_rows(ref, val, first):
    s = jnp.sum(val, axis=0, keepdims=True)

    @pl.when(first)
    def _():
        ref[...] = s

    @pl.when(jnp.logical_not(first))
    def _():
        ref[...] += s


def _modnorm_fwd(x, g, sc, sh, *, name):
    S, D = x.shape
    tr = _row_tile(S, D)

    def body(x_ref, g_ref, sc_ref, sh_ref, h_ref):
        xv = x_ref[...]
        n = (xv * _rstd(xv)) * g_ref[...]
        h_ref[...] = (n * (1.0 + sc_ref[...]) + sh_ref[...]).astype(BF16)

    row = pl.BlockSpec((tr, D), lambda i: (i, 0))
    vec = pl.BlockSpec((1, D), lambda i: (0, 0))
    return pl.pallas_call(
        body, name=name, out_shape=jax.ShapeDtypeStruct((S, D), BF16), grid=(S // tr,),
        in_specs=[row, vec, vec, vec], out_specs=row, compiler_params=_params("parallel"),
    )(x, g, sc, sh)


def _modnorm_bwd(dh, x, g, sc, dres, *, name):
    S, D = x.shape
    tr = _row_tile(S, D)

    def body(dh_ref, x_ref, g_ref, sc_ref, dres_ref, dx_ref, dg_ref, dsc_ref, dsh_ref):
        first = pl.program_id(0) == 0
        xv = x_ref[...]
        dhv = dh_ref[...]
        gv = g_ref[...]
        r = _rstd(xv)
        xhat = xv * r
        _acc_rows(dsh_ref, dhv, first)
        _acc_rows(dsc_ref, dhv * (xhat * gv), first)
        dn = dhv * (1.0 + sc_ref[...])
        _acc_rows(dg_ref, dn * xhat, first)
        dxhat = dn * gv
        proj = jnp.mean(dxhat * xhat, axis=-1, keepdims=True)
        dx_ref[...] = r * (dxhat - xhat * proj) + dres_ref[...]

    row = pl.BlockSpec((tr, D), lambda i: (i, 0))
    vec = pl.BlockSpec((1, D), lambda i: (0, 0))
    vshape = jax.ShapeDtypeStruct((1, D), F32)
    return pl.pallas_call(
        body, name=name,
        out_shape=(jax.ShapeDtypeStruct((S, D), F32), vshape, vshape, vshape), grid=(S // tr,),
        in_specs=[row, row, vec, vec, row], out_specs=(row, vec, vec, vec),
        compiler_params=_params("arbitrary"),
    )(dh, x, g, sc, dres)


def _resnorm_fwd(xres, m, g, gt, *, name):
    S, D = xres.shape
    tr = _row_tile(S, D)

    def body(x_ref, m_ref, g_ref, gt_ref, o_ref):
        mv = m_ref[...]
        o_ref[...] = x_ref[...] + gt_ref[...] * ((mv * _rstd(mv)) * g_ref[...])

    row = pl.BlockSpec((tr, D), lambda i: (i, 0))
    vec = pl.BlockSpec((1, D), lambda i: (0, 0))
    return pl.pallas_call(
        body, name=name, out_shape=jax.ShapeDtypeStruct((S, D), F32), grid=(S // tr,),
        in_specs=[row, row, vec, vec], out_specs=row, compiler_params=_params("parallel"),
    )(xres, m, g, gt)


def _resnorm_loss(xres, m, g, gt, target, *, name):
    S, D = xres.shape
    tr = _row_tile(S, D)

    def body(x_ref, m_ref, g_ref, gt_ref, t_ref, d_ref, dm_ref, dg_ref, dgt_ref, loss_ref):
        first = pl.program_id(0) == 0
        mv = m_ref[...]
        gv = g_ref[...]
        r = _rstd(mv)
        mhat = mv * r
        n = mhat * gv
        err = (x_ref[...] + gt_ref[...] * n) - t_ref[...]
        dv = err * (1.0 / D)
        d_ref[...] = dv
        part = 0.5 * jnp.sum(jnp.mean(err * err, axis=-1, keepdims=True), axis=0, keepdims=True)
        part = jnp.broadcast_to(part, loss_ref.shape)

        @pl.when(first)
        def _():
            loss_ref[...] = part

        @pl.when(jnp.logical_not(first))
        def _():
            loss_ref[...] += part

        _acc_rows(dgt_ref, dv * n, first)
        dn = dv * gt_ref[...]
        _acc_rows(dg_ref, dn * mhat, first)
        dmhat = dn * gv
        proj = jnp.mean(dmhat * mhat, axis=-1, keepdims=True)
        dm_ref[...] = (r * (dmhat - mhat * proj)).astype(BF16)

    row = pl.BlockSpec((tr, D), lambda i: (i, 0))
    vec = pl.BlockSpec((1, D), lambda i: (0, 0))
    vshape = jax.ShapeDtypeStruct((1, D), F32)
    return pl.pallas_call(
        body, name=name,
        out_shape=(jax.ShapeDtypeStruct((S, D), F32), jax.ShapeDtypeStruct((S, D), BF16), vshape, vshape,
                   jax.ShapeDtypeStruct((8, LANES), F32)), grid=(S // tr,),
        in_specs=[row, row, vec, vec, row], out_specs=(row, row, vec, vec, pl.BlockSpec((8, LANES), lambda i: (0, 0))),
        compiler_params=_params("arbitrary"),
    )(xres, m, g, gt, target)


def _resnorm_bwd(dout, m, g, gt, *, name):
    S, D = m.shape
    tr = _row_tile(S, D)

    def body(d_ref, m_ref, g_ref, gt_ref, dm_ref, dg_ref, dgt_ref):
        first = pl.program_id(0) == 0
        mv = m_ref[...]
        dv = d_ref[...]
        gv = g_ref[...]
        r = _rstd(mv)
        mhat = mv * r
        _acc_rows(dgt_ref, dv * (mhat * gv), first)
        dn = dv * gt_ref[...]
        _acc_rows(dg_ref, dn * mhat, first)
        dmhat = dn * gv
        proj = jnp.mean(dmhat * mhat, axis=-1, keepdims=True)
        dm_ref[...] = (r * (dmhat - mhat * proj)).astype(BF16)

    row = pl.BlockSpec((tr, D), lambda i: (i, 0))
    vec = pl.BlockSpec((1, D), lambda i: (0, 0))
    vshape = jax.ShapeDtypeStruct((1, D), F32)
    return pl.pallas_call(
        body, name=name, out_shape=(jax.ShapeDtypeStruct((S, D), BF16), vshape, vshape), grid=(S // tr,),
        in_specs=[row, row, vec, vec], out_specs=(row, vec, vec), compiler_params=_params("arbitrary"),
    )(dout, m, g, gt)


def _lat_norm_fwd(z_lat, g_q, g_kv, *, name):
    S, W = z_lat.shape
    Rq, Rkv = g_q.shape[1], g_kv.shape[1]
    tr = _row_tile(S, W)

    def body(z_ref, gq_ref, gkv_ref, nq_ref, nkv_ref):
        cq = z_ref[:, :Rq]
        ckv = z_ref[:, Rq:Rq + Rkv]
        nq_ref[...] = ((cq * _rstd(cq)) * gq_ref[...]).astype(BF16)
        nkv_ref[...] = ((ckv * _rstd(ckv)) * gkv_ref[...]).astype(BF16)

    return pl.pallas_call(
        body, name=name,
        out_shape=(jax.ShapeDtypeStruct((S, Rq), BF16), jax.ShapeDtypeStruct((S, Rkv), BF16)), grid=(S // tr,),
        in_specs=[pl.BlockSpec((tr, W), lambda i: (i, 0)), pl.BlockSpec((1, Rq), lambda i: (0, 0)),
                  pl.BlockSpec((1, Rkv), lambda i: (0, 0))],
        out_specs=(pl.BlockSpec((tr, Rq), lambda i: (i, 0)), pl.BlockSpec((tr, Rkv), lambda i: (i, 0))),
        compiler_params=_params("parallel"),
    )(z_lat, g_q, g_kv)


def _lat_norm_bwd(z_lat, dnq, dnkv, dkr, g_q, g_kv, *, name):
    S, W = z_lat.shape
    Rq, Rkv = g_q.shape[1], g_kv.shape[1]
    tr = _row_tile(S, W)

    def one(c, dn, gv):
        r = _rstd(c)
        chat = c * r
        dchat = dn * gv
        proj = jnp.mean(dchat * chat, axis=-1, keepdims=True)
        return r * (dchat - chat * proj), dn * chat

    def body(z_ref, dnq_ref, dnkv_ref, dkr_ref, gq_ref, gkv_ref, dz_ref, dgq_ref, dgkv_ref):
        first = pl.program_id(0) == 0
        dcq, pq = one(z_ref[:, :Rq], dnq_ref[...], gq_ref[...])
        dckv, pkv = one(z_ref[:, Rq:Rq + Rkv], dnkv_ref[...], gkv_ref[...])
        _acc_rows(dgq_ref, pq, first)
        _acc_rows(dgkv_ref, pkv, first)
        dz_ref[:, :Rq] = dcq.astype(BF16)
        dz_ref[:, Rq:Rq + Rkv] = dckv.astype(BF16)
        dz_ref[:, Rq + Rkv:Rq + Rkv + LANES] = dkr_ref[...].astype(BF16)
        if W > Rq + Rkv + LANES:
            dz_ref[:, Rq + Rkv + LANES:] = jnp.zeros((tr, W - Rq - Rkv - LANES), BF16)

    return pl.pallas_call(
        body, name=name,
        out_shape=(jax.ShapeDtypeStruct((S, W), BF16), jax.ShapeDtypeStruct((1, Rq), F32),
                   jax.ShapeDtypeStruct((1, Rkv), F32)), grid=(S // tr,),
        in_specs=[pl.BlockSpec((tr, W), lambda i: (i, 0)), pl.BlockSpec((tr, Rq), lambda i: (i, 0)),
                  pl.BlockSpec((tr, Rkv), lambda i: (i, 0)), pl.BlockSpec((tr, LANES), lambda i: (i, 0)),
                  pl.BlockSpec((1, Rq), lambda i: (0, 0)), pl.BlockSpec((1, Rkv), lambda i: (0, 0))],
        out_specs=(pl.BlockSpec((tr, W), lambda i: (i, 0)), pl.BlockSpec((1, Rq), lambda i: (0, 0)),
                   pl.BlockSpec((1, Rkv), lambda i: (0, 0))),
        compiler_params=_params("arbitrary"),
    )(z_lat, dnq, dnkv, dkr, g_q, g_kv)


def _rot(x, lo32):
    a = pltpu.roll(x, 32, 1)
    b = pltpu.roll(x, LANES - 32, 1)
    return jnp.where(lo32, -b, a)


def _rot_t(g, lo32):
    a = pltpu.roll(g, 32, 1)
    b = pltpu.roll(g, LANES - 32, 1)
    return jnp.where(lo32, b, -a)


def _mla_pack_fwd(q_raw, kv_raw, z_lat, cos, sin, kr_off, *, name):
    S = q_raw.shape[0]
    H = kv_raw.shape[1] // (MLA_NOPE + MLA_V)
    W = z_lat.shape[1]
    scale = MLA_QK ** -0.5
    tr = min(S, 128)
    nope_w = H * MLA_NOPE

    def body(q_ref, kv_ref, z_ref, cos_ref, sin_ref, qp_ref, kp_ref, v_ref):
        lane = lax.broadcasted_iota(jnp.int32, (tr, LANES), 1)
        lo32 = (lane % 64) < 32
        lo64 = lane < 64
        c = cos_ref[...]
        s = sin_ref[...]
        kr = z_ref[:, kr_off:kr_off + LANES]
        kr = (kr * c + _rot(kr, lo32) * s).astype(BF16)
        for hp in range(H // 2):
            xb = q_ref[:, nope_w + hp * LANES:nope_w + (hp + 1) * LANES].astype(F32)
            rb = (xb * c + _rot(xb, lo32) * s) * scale
            for e in range(2):
                h = 2 * hp + e
                base = h * MLA_QK_PAD
                qp_ref[:, base:base + LANES] = (q_ref[:, h * LANES:(h + 1) * LANES].astype(F32) * scale).astype(BF16)
                keep = lo64 if e == 0 else jnp.logical_not(lo64)
                qp_ref[:, base + LANES:base + 2 * LANES] = jnp.where(keep, rb, 0.0).astype(BF16)
                kp_ref[:, base:base + LANES] = kv_ref[:, h * LANES:(h + 1) * LANES].astype(BF16)
                kp_ref[:, base + LANES:base + 2 * LANES] = kr
        v_ref[...] = kv_ref[:, nope_w:].astype(BF16)

    return pl.pallas_call(
        body, name=name,
        out_shape=(jax.ShapeDtypeStruct((S, H * MLA_QK_PAD), BF16), jax.ShapeDtypeStruct((S, H * MLA_QK_PAD), BF16),
                   jax.ShapeDtypeStruct((S, H * MLA_V), BF16)), grid=(S // tr,),
        in_specs=[pl.BlockSpec((tr, q_raw.shape[1]), lambda i: (i, 0)), pl.BlockSpec((tr, kv_raw.shape[1]), lambda i: (i, 0)),
                  pl.BlockSpec((tr, W), lambda i: (i, 0)), pl.BlockSpec((tr, LANES), lambda i: (i, 0)),
                  pl.BlockSpec((tr, LANES), lambda i: (i, 0))],
        out_specs=(pl.BlockSpec((tr, H * MLA_QK_PAD), lambda i: (i, 0)), pl.BlockSpec((tr, H * MLA_QK_PAD), lambda i: (i, 0)),
                   pl.BlockSpec((tr, H * MLA_V), lambda i: (i, 0))),
        compiler_params=_params("parallel"),
    )(q_raw, kv_raw, z_lat, cos, sin)


def _mla_pack_bwd(dqp, dkp, dv, cos, sin, *, name):
    S = dqp.shape[0]
    H = dv.shape[1] // MLA_V
    scale = MLA_QK ** -0.5
    tr = min(S, 128)
    nope_w = H * MLA_NOPE

    def body(dqp_ref, dkp_ref, dv_ref, cos_ref, sin_ref, dq_ref, dkv_ref, dkr_ref):
        lane = lax.broadcasted_iota(jnp.int32, (tr, LANES), 1)
        lo32 = (lane % 64) < 32
        lo64 = lane < 64
        c = cos_ref[...]
        s = sin_ref[...]
        dkr2 = jnp.zeros((tr, LANES), F32)
        for hp in range(H // 2):
            be = (2 * hp) * MLA_QK_PAD
            bo = (2 * hp + 1) * MLA_QK_PAD
            g = jnp.where(lo64, dqp_ref[:, be + LANES:be + 2 * LANES], dqp_ref[:, bo + LANES:bo + 2 * LANES]) * scale
            dq_ref[:, nope_w + hp * LANES:nope_w + (hp + 1) * LANES] = (g * c + _rot_t(g * s, lo32)).astype(BF16)
            for h, base in ((2 * hp, be), (2 * hp + 1, bo)):
                dq_ref[:, h * LANES:(h + 1) * LANES] = (dqp_ref[:, base:base + LANES] * scale).astype(BF16)
                dkv_ref[:, h * LANES:(h + 1) * LANES] = dkp_ref[:, base:base + LANES].astype(BF16)
                dkr2 = dkr2 + dkp_ref[:, base + LANES:base + 2 * LANES]
        dkr2 = dkr2 * c + _rot_t(dkr2 * s, lo32)
        dkr2 = dkr2 + pltpu.roll(dkr2, 64, 1)
        dkr_ref[...] = jnp.where(lo64, dkr2, 0.0)
        dkv_ref[:, nope_w:] = dv_ref[...].astype(BF16)

    return pl.pallas_call(
        body, name=name,
        out_shape=(jax.ShapeDtypeStruct((S, nope_w + H * MLA_ROPE), BF16), jax.ShapeDtypeStruct((S, 2 * nope_w), BF16),
                   jax.ShapeDtypeStruct((S, LANES), F32)), grid=(S // tr,),
        in_specs=[pl.BlockSpec((tr, H * MLA_QK_PAD), lambda i: (i, 0)), pl.BlockSpec((tr, H * MLA_QK_PAD), lambda i: (i, 0)),
                  pl.BlockSpec((tr, H * MLA_V), lambda i: (i, 0)), pl.BlockSpec((tr, LANES), lambda i: (i, 0)),
                  pl.BlockSpec((tr, LANES), lambda i: (i, 0))],
        out_specs=(pl.BlockSpec((tr, nope_w + H * MLA_ROPE), lambda i: (i, 0)), pl.BlockSpec((tr, 2 * nope_w), lambda i: (i, 0)),
                   pl.BlockSpec((tr, LANES), lambda i: (i, 0))),
        compiler_params=_params("parallel"),
    )(dqp, dkp, dv, cos, sin)


FLASH_HB = 2


def _causal_pairs(nb):
    qi = [i for i in range(nb) for j in range(i + 1)]
    kj = [j for i in range(nb) for j in range(i + 1)]
    return jnp.asarray(qi, jnp.int32), jnp.asarray(kj, jnp.int32)


def _scores(q, k, diagonal, t):
    s = lax.dot_general(q, k, (((1,), (1,)), ((), ())), preferred_element_type=F32)
    if diagonal:
        row = lax.broadcasted_iota(jnp.int32, (t, t), 0)
        col = lax.broadcasted_iota(jnp.int32, (t, t), 1)
        s = jnp.where(col <= row, s, NEG)
    return s


def _flash_fwd(qp, kp, v, *, name):
    S = qp.shape[0]
    H = v.shape[1] // MLA_V
    t = min(S, 512)
    nb = S // t
    HB = 2 * FLASH_HB
    qi, kj = _causal_pairs(nb)
    QW, VW = MLA_QK_PAD, MLA_V

    def body(qi_ref, kj_ref, q_ref, k_ref, v_ref, o_ref, lse_ref, m_s, l_s, acc_s):
        pr = pl.program_id(1)
        i = qi_ref[pr]
        j = kj_ref[pr]

        @pl.when(j == 0)
        def _():
            m_s[...] = jnp.full_like(m_s, NEG)
            l_s[...] = jnp.zeros_like(l_s)
            acc_s[...] = jnp.zeros_like(acc_s)

        def step(diagonal):
            state = [(m_s[hh], l_s[hh], acc_s[hh]) for hh in range(HB)]
            new = []
            for hh, (m_prev, l_prev, acc_prev) in enumerate(state):
                s = _scores(q_ref[:, hh * QW:(hh + 1) * QW], k_ref[:, hh * QW:(hh + 1) * QW], diagonal, t)
                m_cur = jnp.maximum(m_prev, jnp.max(s, axis=1, keepdims=True))
                alpha = jnp.exp(m_prev - m_cur)
                p = jnp.exp(s - m_cur[:, :1])
                l_new = alpha * l_prev + jnp.sum(p, axis=1, keepdims=True)
                acc = alpha * acc_prev + jnp.dot(p.astype(BF16), v_ref[:, hh * VW:(hh + 1) * VW], preferred_element_type=F32)
                new.append((m_cur, l_new, acc))
            for hh, (m_cur, l_new, acc) in enumerate(new):
                if diagonal:
                    o_ref[:, hh * VW:(hh + 1) * VW] = acc / l_new
                    lse_ref[hh] = m_cur + jnp.log(l_new)
                else:
                    l_s[hh] = l_new
                    acc_s[hh] = acc
                    m_s[hh] = m_cur

        @pl.when(i != j)
        def _():
            step(False)

        @pl.when(i == j)
        def _():
            step(True)

    return pl.pallas_call(
        body, name=name,
        out_shape=(jax.ShapeDtypeStruct((S, H * VW), F32), jax.ShapeDtypeStruct((H, S, LANES), F32)),
        grid_spec=pltpu.PrefetchScalarGridSpec(
            num_scalar_prefetch=2, grid=(H // HB, qi.shape[0]),
            in_specs=[pl.BlockSpec((t, HB * QW), lambda g, p, qi, kj: (qi[p], g)),
                      pl.BlockSpec((t, HB * QW), lambda g, p, qi, kj: (kj[p], g)),
                      pl.BlockSpec((t, HB * VW), lambda g, p, qi, kj: (kj[p], g))],
            out_specs=(pl.BlockSpec((t, HB * VW), lambda g, p, qi, kj: (qi[p], g)),
                       pl.BlockSpec((HB, t, LANES), lambda g, p, qi, kj: (g, qi[p], 0))),
            scratch_shapes=[pltpu.VMEM((HB, t, LANES), F32), pltpu.VMEM((HB, t, LANES), F32), pltpu.VMEM((HB, t, VW), F32)]),
        compiler_params=_params("parallel", "arbitrary"),
    )(qi, kj, qp, kp, v)


def _flash_bwd(qp, kp, v, o, do, lse, *, name):
    S = qp.shape[0]
    H = v.shape[1] // MLA_V
    t = min(S, 512)
    nb = S // t
    HB = FLASH_HB
    qi, kj = _causal_pairs(nb)
    QW, VW = MLA_QK_PAD, MLA_V
    tn = (((0,), (0,)), ((), ()))
    nt = (((1,), (1,)), ((), ()))

    def body(qi_ref, kj_ref, q_ref, k_ref, v_ref, o_ref, do_ref, lse_ref, dq_ref, dk_ref, dv_ref, dq_s):
        pr = pl.program_id(1)
        i = qi_ref[pr]
        j = kj_ref[pr]
        rows = pl.ds(pl.multiple_of(j * t, t), t)

        @pl.when(pr == 0)
        def _():
            dk_ref[...] = jnp.zeros_like(dk_ref)
            dv_ref[...] = jnp.zeros_like(dv_ref)

        @pl.when(j == 0)
        def _():
            dq_s[...] = jnp.zeros_like(dq_s)

        def step(diagonal):
            old = [(dv_ref[rows, hh * VW:(hh + 1) * VW], dk_ref[rows, hh * QW:(hh + 1) * QW], dq_s[:, hh * QW:(hh + 1) * QW])
                   for hh in range(HB)]
            new = []
            for hh, (dv_old, dk_old, dq_old) in enumerate(old):
                q = q_ref[:, hh * QW:(hh + 1) * QW]
                k = k_ref[:, hh * QW:(hh + 1) * QW]
                dob = do_ref[:, hh * VW:(hh + 1) * VW]
                p = jnp.exp(_scores(q, k, diagonal, t) - lse_ref[hh][:, :1])
                delta = jnp.sum(dob.astype(F32) * o_ref[:, hh * VW:(hh + 1) * VW], axis=1, keepdims=True)
                dp = lax.dot_general(dob, v_ref[:, hh * VW:(hh + 1) * VW], nt, preferred_element_type=F32)
                dsb = (p * (dp - delta)).astype(BF16)
                new.append((dv_old + lax.dot_general(p.astype(BF16), dob, tn, preferred_element_type=F32),
                            dk_old + lax.dot_general(dsb, q, tn, preferred_element_type=F32),
                            dq_old + jnp.dot(dsb, k, preferred_element_type=F32)))
            for hh, (dv_new, dk_new, dq_new) in enumerate(new):
                dv_ref[rows, hh * VW:(hh + 1) * VW] = dv_new
                dk_ref[rows, hh * QW:(hh + 1) * QW] = dk_new
                if diagonal:
                    dq_ref[:, hh * QW:(hh + 1) * QW] = dq_new
                else:
                    dq_s[:, hh * QW:(hh + 1) * QW] = dq_new

        @pl.when(i != j)
        def _():
            step(False)

        @pl.when(i == j)
        def _():
            step(True)

    qside = lambda g, p, qi, kj: (qi[p], g)
    kside = lambda g, p, qi, kj: (kj[p], g)
    whole = lambda g, p, qi, kj: (0, g)
    return pl.pallas_call(
        body, name=name,
        out_shape=(jax.ShapeDtypeStruct((S, H * QW), F32), jax.ShapeDtypeStruct((S, H * QW), F32),
                   jax.ShapeDtypeStruct((S, H * VW), F32)),
        grid_spec=pltpu.PrefetchScalarGridSpec(
            num_scalar_prefetch=2, grid=(H // HB, qi.shape[0]),
            in_specs=[pl.BlockSpec((t, HB * QW), qside), pl.BlockSpec((t, HB * QW), kside), pl.BlockSpec((t, HB * VW), kside),
                      pl.BlockSpec((t, HB * VW), qside), pl.BlockSpec((t, HB * VW), qside),
                      pl.BlockSpec((HB, t, LANES), lambda g, p, qi, kj: (g, qi[p], 0))],
            out_specs=(pl.BlockSpec((t, HB * QW), qside), pl.BlockSpec((S, HB * QW), whole), pl.BlockSpec((S, HB * VW), whole)),
            scratch_shapes=[pltpu.VMEM((t, HB * QW), F32)]),
        compiler_params=_params("parallel", "arbitrary"),
    )(qi, kj, qp, kp, v, o, do, lse)


def _swa_kv_halves(blk, hf, lo):
    if hf == 0:
        a = jnp.where(lo, blk, 0.0)
        b = pltpu.roll(a, 64, 1)
    else:
        b = jnp.where(lo, 0.0, blk)
        a = pltpu.roll(b, 64, 1)
    return a.astype(BF16), b.astype(BF16)


def _swa_softmax(qs, kx, bias, neg0, sk):
    s = lax.dot_general(qs, kx, (((1,), (1,)), ((), ())), preferred_element_type=F32) + bias + neg0
    m = jnp.maximum(jnp.max(s, axis=1, keepdims=True), sk)
    e = jnp.exp(s - m)
    es = jnp.exp(sk - m)
    inv = 1.0 / (jnp.sum(e, axis=1, keepdims=True) + es)
    return e * inv, es * inv


def _swa_stack(ref, kvh, npb, scale=None):
    parts = [ref[:, (kvh * npb + pb) * LANES:(kvh * npb + pb + 1) * LANES] for pb in range(npb)]
    x = jnp.concatenate(parts, axis=0)
    return x if scale is None else x * scale


def _swa_sink_col(sink_ref, kvh, e, npb):
    row = lax.broadcasted_iota(jnp.int32, (npb * SWA_BLOCK, 1), 0)
    col = jnp.zeros((npb * SWA_BLOCK, 1), F32) + sink_ref[2 * (kvh * npb) + e]
    for pb in range(1, npb):
        col = jnp.where(row >= pb * SWA_BLOCK, sink_ref[2 * (kvh * npb + pb) + e], col)
    return col


def _swa_fwd(z_swa, bias_st, sinks, *, name):
    S, W = z_swa.shape
    npb = bias_st.shape[1] // SWA_BLOCK
    NH = 2 * SWA_KVH * npb
    QW = NH * SWA_HD
    KW = SWA_KVH * SWA_HD
    nb = S // SWA_BLOCK
    B = SWA_BLOCK
    assert SWA_KVH % 2 == 0 and W == QW + 2 * KW

    def body(sink_ref, q_ref, kvc_ref, kvp_ref, b_ref, o_ref):
        n = pl.program_id(0)
        lo = lax.broadcasted_iota(jnp.int32, (2 * B, LANES), 1) < 64
        col = lax.broadcasted_iota(jnp.int32, (npb * B, 2 * B), 1)
        neg0 = jnp.where(jnp.logical_and(col < B, n == 0), NEG, 0.0)
        for kb in range(SWA_KVH // 2):
            kblk = jnp.concatenate([kvp_ref[:, kb * LANES:(kb + 1) * LANES], kvc_ref[:, kb * LANES:(kb + 1) * LANES]], axis=0)
            vblk = jnp.concatenate([kvp_ref[:, KW + kb * LANES:KW + (kb + 1) * LANES],
                                    kvc_ref[:, KW + kb * LANES:KW + (kb + 1) * LANES]], axis=0)
            for hf in range(2):
                kvh = 2 * kb + hf
                ks = _swa_kv_halves(kblk, hf, lo)
                vs = _swa_kv_halves(vblk, hf, lo)
                qs = _swa_stack(q_ref, kvh, npb, SWA_HD ** -0.5).astype(BF16)
                acc = jnp.zeros((npb * B, LANES), F32)
                for e in range(2):
                    p, _ = _swa_softmax(qs, ks[e], b_ref[2 * kvh + e], neg0, _swa_sink_col(sink_ref, kvh, e, npb))
                    acc = acc + jnp.dot(p.astype(BF16), vs[e], preferred_element_type=F32)
                for pb in range(npb):
                    P = kvh * npb + pb
                    o_ref[:, P * LANES:(P + 1) * LANES] = acc[pb * B:(pb + 1) * B]

    kvcol = QW // (2 * KW)
    assert QW % (2 * KW) == 0
    return pl.pallas_call(
        body, name=name,
        out_shape=jax.ShapeDtypeStruct((S, QW), F32), grid=(nb,),
        in_specs=[SMEM_FULL, pl.BlockSpec((B, QW), lambda n: (n, 0)), pl.BlockSpec((B, 2 * KW), lambda n: (n, kvcol)),
                  pl.BlockSpec((B, 2 * KW), lambda n: (jnp.maximum(n - 1, 0), kvcol)),
                  pl.BlockSpec(bias_st.shape, lambda n: (0, 0, 0))],
        out_specs=pl.BlockSpec((B, QW), lambda n: (n, 0)),
        compiler_params=_params("parallel"),
    )(sinks, z_swa, z_swa, z_swa, bias_st)


def _swa_bwd(z_swa, bias_st, sinks, o, do, *, name):
    S, W = z_swa.shape
    npb = bias_st.shape[1] // SWA_BLOCK
    NH = 2 * SWA_KVH * npb
    QW = NH * SWA_HD
    KW = SWA_KVH * SWA_HD
    nb = S // SWA_BLOCK
    B = SWA_BLOCK
    scale = SWA_HD ** -0.5
    tn = (((0,), (0,)), ((), ()))
    nt = (((1,), (1,)), ((), ()))

    def fold(x, hf, lo):
        x = x + pltpu.roll(x, 64, 1)
        return jnp.where(lo, x, 0.0) if hf == 0 else jnp.where(lo, 0.0, x)

    def body(sink_ref, q_ref, kvc_ref, kvp_ref, b_ref, o_ref, do_ref, dz_ref, dbias_ref, dsink_ref,
             cq_s, ck_s, cv_s, nq_s, nk_s, nv_s, pk_s, pv_s):
        n = pl.program_id(0)

        @pl.when(n == 0)
        def _():
            dbias_ref[...] = jnp.zeros_like(dbias_ref)
            dsink_ref[...] = jnp.zeros_like(dsink_ref)
            cq_s[...] = jnp.zeros_like(cq_s)
            ck_s[...] = jnp.zeros_like(ck_s)
            cv_s[...] = jnp.zeros_like(cv_s)

        @pl.when(n == nb)
        def _():
            pk_s[...] = jnp.zeros_like(pk_s)
            pv_s[...] = jnp.zeros_like(pv_s)

        @pl.when(n < nb)
        def _():
            lo = lax.broadcasted_iota(jnp.int32, (2 * B, LANES), 1) < 64
            lo1 = lax.broadcasted_iota(jnp.int32, (npb * B, LANES), 1) < 64
            lane1 = lax.broadcasted_iota(jnp.int32, (1, LANES), 1)
            col = lax.broadcasted_iota(jnp.int32, (npb * B, 2 * B), 1)
            neg0 = jnp.where(jnp.logical_and(col < B, n == 0), NEG, 0.0)
            dsink = jnp.zeros((1, LANES), F32)
            for kb in range(SWA_KVH // 2):
                kblk = jnp.concatenate([kvp_ref[:, kb * LANES:(kb + 1) * LANES], kvc_ref[:, kb * LANES:(kb + 1) * LANES]], axis=0)
                vblk = jnp.concatenate([kvp_ref[:, KW + kb * LANES:KW + (kb + 1) * LANES],
                                        kvc_ref[:, KW + kb * LANES:KW + (kb + 1) * LANES]], axis=0)
                dkblk = jnp.zeros((2 * B, LANES), F32)
                dvblk = jnp.zeros((2 * B, LANES), F32)
                for hf in range(2):
                    kvh = 2 * kb + hf
                    ks = _swa_kv_halves(kblk, hf, lo)
                    vs = _swa_kv_halves(vblk, hf, lo)
                    qs = _swa_stack(q_ref, kvh, npb, scale).astype(BF16)
                    dos = _swa_stack(do_ref, kvh, npb)
                    prod = dos * _swa_stack(o_ref, kvh, npb)
                    dob = dos.astype(BF16)
                    dkj = jnp.zeros((2 * B, LANES), F32)
                    dvj = jnp.zeros((2 * B, LANES), F32)
                    dqs = jnp.zeros((npb * B, LANES), F32)
                    for e in range(2):
                        keep = lo1 if e == 0 else jnp.logical_not(lo1)
                        p, psink = _swa_softmax(qs, ks[e], b_ref[2 * kvh + e], neg0, _swa_sink_col(sink_ref, kvh, e, npb))
                        delta = jnp.sum(jnp.where(keep, prod, 0.0), axis=1, keepdims=True)
                        dp = lax.dot_general(dob, vs[e], nt, preferred_element_type=F32)
                        ds = p * (dp - delta)
                        dbias_ref[2 * kvh + e] += ds
                        pd = psink * delta
                        for pb in range(npb):
                            dsh = -jnp.sum(pd[pb * B:(pb + 1) * B], axis=0, keepdims=True)
                            dsink = dsink + jnp.where(lane1 == 2 * (kvh * npb + pb) + e, dsh, 0.0)
                        dsb = ds.astype(BF16)
                        dqs = dqs + jnp.dot(dsb, ks[e], preferred_element_type=F32)
                        keep2 = lo if e == 0 else jnp.logical_not(lo)
                        dkj = dkj + jnp.where(keep2, lax.dot_general(dsb, qs, tn, preferred_element_type=F32), 0.0)
                        dvj = dvj + jnp.where(keep2, lax.dot_general(p.astype(BF16), dob, tn, preferred_element_type=F32), 0.0)
                    for pb in range(npb):
                        P = kvh * npb + pb
                        nq_s[:, P * LANES:(P + 1) * LANES] = dqs[pb * B:(pb + 1) * B] * scale
                    dkblk = dkblk + fold(dkj, hf, lo)
                    dvblk = dvblk + fold(dvj, hf, lo)
                pk_s[:, kb * LANES:(kb + 1) * LANES] = dkblk[:B]
                nk_s[:, kb * LANES:(kb + 1) * LANES] = dkblk[B:]
                pv_s[:, kb * LANES:(kb + 1) * LANES] = dvblk[:B]
                nv_s[:, kb * LANES:(kb + 1) * LANES] = dvblk[B:]
            dsink_ref[...] += dsink

        dz_ref[:, :QW] = cq_s[...].astype(BF16)
        dz_ref[:, QW:QW + KW] = (ck_s[...] + pk_s[...]).astype(BF16)
        dz_ref[:, QW + KW:] = (cv_s[...] + pv_s[...]).astype(BF16)

        @pl.when(n < nb)
        def _():
            cq_s[...] = nq_s[...]
            ck_s[...] = nk_s[...]
            cv_s[...] = nv_s[...]

    kvcol = QW // (2 * KW)
    cur = lambda n: (jnp.minimum(n, nb - 1), 0)
    return pl.pallas_call(
        body, name=name,
        out_shape=(jax.ShapeDtypeStruct((S, W), BF16), jax.ShapeDtypeStruct(bias_st.shape, F32),
                   jax.ShapeDtypeStruct((1, LANES), F32)),
        grid=(nb + 1,),
        in_specs=[SMEM_FULL, pl.BlockSpec((B, QW), cur), pl.BlockSpec((B, 2 * KW), lambda n: (jnp.minimum(n, nb - 1), kvcol)),
                  pl.BlockSpec((B, 2 * KW), lambda n: (jnp.maximum(jnp.minimum(n, nb - 1) - 1, 0), kvcol)),
                  pl.BlockSpec(bias_st.shape, lambda n: (0, 0, 0)), pl.BlockSpec((B, QW), cur), pl.BlockSpec((B, QW), cur)],
        out_specs=(pl.BlockSpec((B, W), lambda n: (jnp.maximum(n - 1, 0), 0)),
                   pl.BlockSpec(bias_st.shape, lambda n: (0, 0, 0)), pl.BlockSpec((1, LANES), lambda n: (0, 0))),
        scratch_shapes=[pltpu.VMEM((B, QW), F32), pltpu.VMEM((B, KW), F32), pltpu.VMEM((B, KW), F32),
                        pltpu.VMEM((B, QW), F32), pltpu.VMEM((B, KW), F32), pltpu.VMEM((B, KW), F32),
                        pltpu.VMEM((B, KW), F32), pltpu.VMEM((B, KW), F32)],
        compiler_params=_params("arbitrary"),
    )(sinks, z_swa, z_swa, z_swa, bias_st, o, do)


def _gate_fwd(zg, o_a, o_b, *, name):
    S, D = o_a.shape
    tr = min(S, 512)

    def body(z_ref, a_ref, b_ref, m_ref):
        ga = jax.nn.sigmoid(z_ref[:, :PAIR])
        gb = jax.nn.sigmoid(z_ref[:, PAIR:])
        m_ref[...] = (ga * a_ref[...] + gb * b_ref[...]).astype(BF16)

    col = pl.BlockSpec((tr, PAIR), lambda i, j: (i, j))
    return pl.pallas_call(
        body, name=name, out_shape=jax.ShapeDtypeStruct((S, D), BF16), grid=(S // tr, D // PAIR),
        in_specs=[pl.BlockSpec((tr, 2 * PAIR), lambda i, j: (i, j)), col, col], out_specs=col,
        compiler_params=_params("parallel", "parallel"),
    )(zg, o_a, o_b)


def _gate_bwd(dmix, zg, o_a, o_b, *, name):
    S, D = o_a.shape
    tr = min(S, 512)

    def body(d_ref, z_ref, a_ref, b_ref, da_ref, db_ref, dz_ref):
        d = d_ref[...]
        ga = jax.nn.sigmoid(z_ref[:, :PAIR])
        gb = jax.nn.sigmoid(z_ref[:, PAIR:])
        da_ref[...] = (d * ga).astype(BF16)
        db_ref[...] = d * gb
        dz_ref[:, :PAIR] = (d * a_ref[...] * (ga * (1.0 - ga))).astype(BF16)
        dz_ref[:, PAIR:] = (d * b_ref[...] * (gb * (1.0 - gb))).astype(BF16)

    col = pl.BlockSpec((tr, PAIR), lambda i, j: (i, j))
    wide = pl.BlockSpec((tr, 2 * PAIR), lambda i, j: (i, j))
    return pl.pallas_call(
        body, name=name,
        out_shape=(jax.ShapeDtypeStruct((S, D), BF16), jax.ShapeDtypeStruct((S, D), F32), jax.ShapeDtypeStruct((S, 2 * D), BF16)),
        grid=(S // tr, D // PAIR), in_specs=[col, wide, col, col], out_specs=(col, col, wide),
        compiler_params=_params("parallel", "parallel"),
    )(dmix, zg, o_a, o_b)


def _conv_u(t_ref, prev_ref, w_ref, b_ref, m, i):
    cur = t_ref[m].astype(F32)
    live = (i > 0).astype(F32)
    p6 = prev_ref[m, 14:15, :].astype(F32) * live
    p7 = prev_ref[m, 15:16, :].astype(F32) * live
    row = lax.broadcasted_iota(jnp.int32, cur.shape, 0)
    t1 = jnp.where(row == 0, p7, pltpu.roll(cur, 1, 0))
    t2 = jnp.where(row == 0, p6, jnp.where(row == 1, p7, pltpu.roll(cur, 2, 0)))
    u = ((b_ref[m] + w_ref[m, 0:1, :] * t2) + w_ref[m, 1:2, :] * t1) + w_ref[m, 2:3, :] * cur
    return u, cur, t1, t2


def _conv_specs(tr, tc):
    blk = pl.BlockSpec((2, tr, tc), lambda p, j, i: (p, i, j))
    prev = pl.BlockSpec((2, 16, tc), lambda p, j, i: (p, jnp.maximum(i * (tr // 16) - 1, 0), j))
    w3 = pl.BlockSpec((2, 3, tc), lambda p, j, i: (p, 0, j))
    w1 = pl.BlockSpec((2, 1, tc), lambda p, j, i: (p, 0, j))
    return blk, prev, w3, w1


def _conv_gate_fwd(t, cw, cb, *, name):
    _, S, C = t.shape
    tr, tc = min(S, 512), _tile(C, 1536)
    ncol = C // tc
    blk, prev, w3, w1 = _conv_specs(tr, tc)

    def body(t_ref, prev_ref, w_ref, b_ref, a_ref):
        i = pl.program_id(2)
        u1 = _conv_u(t_ref, prev_ref, w_ref, b_ref, 0, i)[0]
        u2 = _conv_u(t_ref, prev_ref, w_ref, b_ref, 1, i)[0]
        a_ref[...] = (jax.nn.silu(u1) * u2).astype(BF16)

    return pl.pallas_call(
        body, name=name, out_shape=jax.ShapeDtypeStruct((S, 2 * C), BF16), grid=(2, ncol, S // tr),
        in_specs=[blk, prev, w3, w1], out_specs=pl.BlockSpec((tr, tc), lambda p, j, i: (i, p * ncol + j)),
        compiler_params=_params("parallel", "parallel", "parallel"),
    )(t, t, cw, cb)


def _conv_gate_bwd(t, da, cw, cb, *, name):
    _, S, C = t.shape
    tr, tc = min(S, 256), _tile(C, 1536)
    ncol = C // tc
    blk, prev, w3, w1 = _conv_specs(tr, tc)

    def body(t_ref, prev_ref, da_ref, w_ref, b_ref, du_ref, dw_ref, db_ref):
        i = pl.program_id(2)
        first = i == 0
        u1, c1, a1, b1 = _conv_u(t_ref, prev_ref, w_ref, b_ref, 0, i)
        u2, c2, a2, b2 = _conv_u(t_ref, prev_ref, w_ref, b_ref, 1, i)
        d = da_ref[...].astype(F32)
        sg = jax.nn.sigmoid(u1)
        du1 = d * u2 * (sg * (1.0 + u1 * (1.0 - sg)))
        du2 = d * (u1 * sg)
        for m, (du, cur, t1, t2) in enumerate(((du1, c1, a1, b1), (du2, c2, a2, b2))):
            du_ref[m] = du.astype(BF16)
            dw = jnp.concatenate([jnp.sum(du * t2, axis=0, keepdims=True), jnp.sum(du * t1, axis=0, keepdims=True),
                                  jnp.sum(du * cur, axis=0, keepdims=True)], axis=0)
            db = jnp.sum(du, axis=0, keepdims=True)

            @pl.when(first)
            def _():
                dw_ref[m] = dw
                db_ref[m] = db

            @pl.when(jnp.logical_not(first))
            def _():
                dw_ref[m] += dw
                db_ref[m] += db

    return pl.pallas_call(
        body, name=name,
        out_shape=(jax.ShapeDtypeStruct(t.shape, BF16), jax.ShapeDtypeStruct(cw.shape, F32), jax.ShapeDtypeStruct(cb.shape, F32)),
        grid=(2, ncol, S // tr),
        in_specs=[blk, prev, pl.BlockSpec((tr, tc), lambda p, j, i: (i, p * ncol + j)), w3, w1], out_specs=(blk, w3, w1),
        compiler_params=_params("parallel", "parallel", "arbitrary"),
    )(t, t, da, cw, cb)


def _conv_bwd_dt(du, cw, *, name):
    _, S, C = du.shape
    tr, tc = min(S, 512), _tile(C, 1536)
    nrow = S // tr
    blk, _, w3, _ = _conv_specs(tr, tc)
    nxt = pl.BlockSpec((2, 16, tc), lambda p, j, i: (p, jnp.minimum((i + 1) * (tr // 16), S // 16 - 1), j))

    def body(d_ref, next_ref, w_ref, dt_ref):
        i = pl.program_id(2)
        live = (i < nrow - 1).astype(F32)
        for m in range(2):
            cur = d_ref[m].astype(F32)
            n0 = next_ref[m, 0:1, :].astype(F32) * live
            n1 = next_ref[m, 1:2, :].astype(F32) * live
            row = lax.broadcasted_iota(jnp.int32, cur.shape, 0)
            d1 = jnp.where(row == tr - 1, n0, pltpu.roll(cur, tr - 1, 0))
            d2 = jnp.where(row == tr - 1, n1, jnp.where(row == tr - 2, n0, pltpu.roll(cur, tr - 2, 0)))
            dt_ref[m] = ((w_ref[m, 2:3, :] * cur + w_ref[m, 1:2, :] * d1) + w_ref[m, 0:1, :] * d2).astype(BF16)

    return pl.pallas_call(
        body, name=name, out_shape=jax.ShapeDtypeStruct(du.shape, BF16), grid=(2, C // tc, nrow),
        in_specs=[blk, nxt, w3], out_specs=blk, compiler_params=_params("parallel", "parallel", "parallel"),
    )(du, du, cw)


def _ada_fwd(c_all, w, b, *, name):
    Bn, D = c_all.shape
    N = w.shape[1]
    tn = _tile(N, 512)

    def body(c_ref, w_ref, b_ref, o_ref):
        o_ref[...] = jnp.dot(jax.nn.silu(c_ref[...]), w_ref[...], preferred_element_type=F32, precision=HIGHEST) + b_ref[...]

    return pl.pallas_call(
        body, name=name, out_shape=jax.ShapeDtypeStruct((Bn, N), F32), grid=(N // tn,),
        in_specs=[pl.BlockSpec((Bn, D), lambda j: (0, 0)), pl.BlockSpec((D, tn), lambda j: (0, j)),
                  pl.BlockSpec((1, tn), lambda j: (0, j))],
        out_specs=pl.BlockSpec((Bn, tn), lambda j: (0, j)), compiler_params=_params("parallel"),
    )(c_all, w, b)


def _ada_bwd(c_all_t, dmod, *, name):
    D, Bn = c_all_t.shape
    N = dmod.shape[1]
    tm = _tile(D, 512, 8)
    tn = _tile(N, 1536)

    def body(c_ref, d_ref, o_ref):
        o_ref[...] = jnp.dot(jax.nn.silu(c_ref[...]), d_ref[...], preferred_element_type=F32, precision=HIGHEST)

    return pl.pallas_call(
        body, name=name, out_shape=jax.ShapeDtypeStruct((D, N), F32), grid=(D // tm, N // tn),
        in_specs=[pl.BlockSpec((tm, Bn), lambda i, j: (i, 0)), pl.BlockSpec((Bn, tn), lambda i, j: (0, j))],
        out_specs=pl.BlockSpec((tm, tn), lambda i, j: (i, j)), compiler_params=_params("parallel", "parallel"),
    )(c_all_t, dmod)


def _adamw(w, g, m, v, *, name):
    R, C = w.shape
    tr = R if R * C <= (1 << 18) else _tile(R, max(8, (1 << 18) // C), 8)

    def body(w_ref, g_ref, m_ref, v_ref, d_ref, nm_ref, nv_ref):
        gv = g_ref[...]
        nm = ADAM_B1 * m_ref[...] + (1.0 - ADAM_B1) * gv
        nv = ADAM_B2 * v_ref[...] + (1.0 - ADAM_B2) * (gv * gv)
        m_hat = nm / (1.0 - ADAM_B1 ** ADAM_STEP)
        v_hat = nv / (1.0 - ADAM_B2 ** ADAM_STEP)
        d_ref[...] = -ADAM_LR * (m_hat / (jnp.sqrt(v_hat) + ADAM_EPS) + ADAM_WD * w_ref[...])
        nm_ref[...] = nm
        nv_ref[...] = nv

    blk = pl.BlockSpec((tr, C), lambda i: (i, 0))
    shp = jax.ShapeDtypeStruct((R, C), F32)
    return pl.pallas_call(
        body, name=name, out_shape=(shp, shp, shp), grid=(R // tr,), in_specs=[blk] * 4, out_specs=(blk,) * 3,
        compiler_params=_params("parallel"),
    )(w, g, m, v)


def _place():
    x, y, c = lax.axis_index("x"), lax.axis_index("y"), lax.axis_index("c")
    return x, y, c, [(1 - x, y), (x, 1 - y), (1 - x, 1 - y)]


def _remote(src, dst, send_sem, recv_sem, dev):
    return pltpu.make_async_remote_copy(src_ref=src, dst_ref=dst, send_sem=send_sem, recv_sem=recv_sem,
                                        device_id=dev, device_id_type=MESH)


def _allgather8(v, *, tie=None, name):
    R, C = v.shape

    def body(v_ref, out_ref, send_sems, recv_sems, local_sem):
        x, y, c, chips = _place()
        me, sibling = (x, y, c), (x, y, 1 - c)

        def rows(px, py, pc):
            return out_ref.at[pl.ds((4 * px + 2 * py + pc) * R, R), :]

        def copy(k, block, to, src=None):
            return _remote(rows(*block) if src is None else src, rows(*block), send_sems.at[k], recv_sems.at[k], to)

        mine = pltpu.make_async_copy(v_ref, rows(*me), local_sem)
        mine.start()
        first = [copy(0, me, sibling, src=v_ref)]
        first += [copy(1 + j, me, (*chip, c), src=v_ref) for j, chip in enumerate(chips)]
        for cp in first:
            cp.start()
        passed = [copy(4 + j, (*chip, c), sibling) for j, chip in enumerate(chips)]
        for j, chip in enumerate(chips):
            copy(1 + j, (*chip, c), me).wait_recv()
            passed[j].start()
        copy(0, sibling, me).wait_recv()
        for j, chip in enumerate(chips):
            copy(4 + j, (*chip, 1 - c), me).wait_recv()
        for cp in first + passed:
            cp.wait_send()
        mine.wait()

    body, tspec, targ = _tied(body, tie)
    out = pl.pallas_call(
        body, name=name, out_shape=jax.ShapeDtypeStruct((N_DEV * R, C), v.dtype),
        in_specs=tspec + [VMEM_FULL], out_specs=VMEM_FULL,
        scratch_shapes=[pltpu.SemaphoreType.DMA((7,)), pltpu.SemaphoreType.DMA((7,)), pltpu.SemaphoreType.DMA],
    )(*targ, v)
    return out.reshape(N_DEV, R, C)


SEM = pl.BlockSpec(memory_space=pltpu.SEMAPHORE)
HBM = pl.BlockSpec(memory_space=pltpu.HBM)
EFFECT = pltpu.SideEffectType.DATAFLOW_SIDE_EFFECTING
DMA_SEM = pltpu.SemaphoreType.DMA(())


def _in_hbm(a):
    return pltpu.with_memory_space_constraint(a, pltpu.HBM)


def _three_halves(land, r2):
    return land.at[pl.ds(0, N_CHIP - 1), pl.ds(0, r2)]


def _slot(chip, swap):
    return (chip % 2) * 2 + chip // 2 if swap else chip


def _gather_start(ws, after, swaps, *, name):
    n = len(ws)
    na = len(after)
    lands = [lax.empty((N_CHIP,) + w.shape, w.dtype) for w in ws]

    def body(*refs):
        w_refs, land_refs = refs[:n], refs[n:2 * n]
        send, recv = refs[2 * n + na:3 * n + na], refs[3 * n + na:4 * n + na]
        token = refs[6 * n + na]
        x, y, c, chips = _place()
        k = 2 * x + y
        for i in range(n):
            r2 = ws[i].shape[0] // 2
            for cx, cy in chips:
                _remote(w_refs[i].at[pl.ds(c * r2, r2)], land_refs[i].at[_slot(k, swaps[i]), pl.ds(c * r2, r2)], send[i], recv[i],
                        (cx, cy, c)).start()
        token[...] = jnp.zeros_like(token)

    outs = pl.pallas_call(
        body, name=name,
        out_shape=[DMA_SEM] * (2 * n) + [pltpu.HBM(w.shape, w.dtype) for w in ws] + [pltpu.HBM(l.shape, l.dtype) for l in lands]
        + [jax.ShapeDtypeStruct((8, LANES), F32)],
        in_specs=[HBM] * (2 * n) + [ANY] * na, out_specs=[SEM] * (2 * n) + [HBM] * (2 * n) + [VMEM_FULL],
        input_output_aliases={i: 2 * n + i for i in range(2 * n)},
        compiler_params=pltpu.CompilerParams(has_side_effects=EFFECT),
    )(*[_in_hbm(w) for w in ws], *[_in_hbm(l) for l in lands], *after)
    return outs[:n], outs[n:2 * n], outs[2 * n:3 * n], outs[3 * n:4 * n], outs[4 * n]


def _gather_forward(send, recv, ws, lands, after, swaps, *, name):
    n = len(ws)

    def body(*refs):
        w_refs, land_refs = refs[:n], refs[n:2 * n]
        send1, recv1 = refs[2 * n:3 * n], refs[3 * n:4 * n]
        send2, recv2 = refs[4 * n + 1 + 2 * n:4 * n + 1 + 3 * n], refs[4 * n + 1 + 3 * n:4 * n + 1 + 4 * n]
        x, y, c, chips = _place()
        sibling = (x, y, 1 - c)
        for i in range(n):
            r2 = ws[i].shape[0] // 2
            win = _three_halves(land_refs[i], r2)
            done = _remote(win, win, send1[i], recv1[i], sibling)
            done.wait_send()
            done.wait_recv()
            for cx, cy in chips:
                got = land_refs[i].at[_slot(2 * cx + cy, swaps[i]), pl.ds(c * r2, r2)]
                _remote(got, got, send2[i], recv2[i], sibling).start()
        token = refs[8 * n + 1]
        token[...] = jnp.zeros_like(token)

    outs = pl.pallas_call(
        body, name=name,
        out_shape=[pltpu.HBM(w.shape, w.dtype) for w in ws] + [pltpu.HBM(l.shape, l.dtype) for l in lands] + [DMA_SEM] * (2 * n)
        + [jax.ShapeDtypeStruct((8, LANES), F32)],
        in_specs=[HBM] * (2 * n) + [SEM] * (2 * n) + [ANY], out_specs=[HBM] * (2 * n) + [SEM] * (2 * n) + [VMEM_FULL],
        input_output_aliases={i: i for i in range(2 * n)},
        compiler_params=pltpu.CompilerParams(has_side_effects=EFFECT),
    )(*ws, *lands, *send, *recv, after)
    return outs[2 * n:3 * n], outs[3 * n:4 * n], outs[n:2 * n], outs[4 * n]


def _gather_finish(send, recv, lands, after, *, name):
    n = len(lands)

    def body(*refs):
        land_refs = refs[:n]
        send2, recv2 = refs[n:2 * n], refs[2 * n:3 * n]
        x, y, c, _ = _place()
        for i in range(n):
            win = _three_halves(land_refs[i], lands[i].shape[1] // 2)
            done = _remote(win, win, send2[i], recv2[i], (x, y, 1 - c))
            done.wait_send()
            done.wait_recv()

    return pl.pallas_call(
        body, name=name,
        out_shape=[pltpu.HBM(l.shape, l.dtype) for l in lands],
        in_specs=[HBM] * n + [SEM] * (2 * n) + [ANY], out_specs=[HBM] * n,
        input_output_aliases={i: i for i in range(n)},
        compiler_params=pltpu.CompilerParams(has_side_effects=EFFECT),
    )(*lands, *send, *recv, after)


def _scatter_start(gs, swaps, *, name):
    n = len(gs)
    lands = [lax.empty((N_DEV, g.shape[1] // 2, g.shape[2]), g.dtype) for g in gs]

    def body(*refs):
        g_refs, land_refs = refs[:n], refs[n:2 * n]
        send, recv = refs[2 * n:3 * n], refs[3 * n:4 * n]
        token = refs[6 * n]
        x, y, c, chips = _place()
        k = 2 * x + y
        me = 2 * k + c
        for i in range(n):
            r2 = gs[i].shape[1] // 2
            for cx, cy in chips:
                for cc in range(2):
                    _remote(g_refs[i].at[_slot(2 * cx + cy, swaps[i]), pl.ds(cc * r2, r2)], land_refs[i].at[me], send[i], recv[i],
                            (cx, cy, cc)).start()
            _remote(g_refs[i].at[_slot(k, swaps[i]), pl.ds((1 - c) * r2, r2)], land_refs[i].at[me], send[i], recv[i],
                    (x, y, 1 - c)).start()
        token[...] = jnp.zeros_like(token)

    outs = pl.pallas_call(
        body, name=name,
        out_shape=[DMA_SEM] * (2 * n) + [pltpu.HBM(g.shape, g.dtype) for g in gs] + [pltpu.HBM(l.shape, l.dtype) for l in lands]
        + [jax.ShapeDtypeStruct((8, LANES), F32)],
        in_specs=[HBM] * (2 * n), out_specs=[SEM] * (2 * n) + [HBM] * (2 * n) + [VMEM_FULL],
        input_output_aliases={i: 2 * n + i for i in range(2 * n)},
        compiler_params=pltpu.CompilerParams(has_side_effects=EFFECT),
    )(*[_in_hbm(g) for g in gs], *[_in_hbm(l) for l in lands])
    return outs[:n], outs[n:2 * n], outs[2 * n:3 * n], outs[3 * n:4 * n], outs[4 * n]


def _scatter_wait(send, recv, gs, lands, after, *, name):
    n = len(gs)

    def body(*refs):
        land_refs = refs[n:2 * n]
        send1, recv1 = refs[2 * n:3 * n], refs[3 * n:4 * n]
        x, y, c, _ = _place()
        for i in range(n):
            win = land_refs[i].at[pl.ds(0, N_DEV - 1)]
            done = _remote(win, win, send1[i], recv1[i], (x, y, 1 - c))
            done.wait_send()
            done.wait_recv()

    outs = pl.pallas_call(
        body, name=name,
        out_shape=[pltpu.HBM(g.shape, g.dtype) for g in gs] + [pltpu.HBM(l.shape, l.dtype) for l in lands],
        in_specs=[HBM] * (2 * n) + [SEM] * (2 * n) + [ANY], out_specs=[HBM] * (2 * n),
        input_output_aliases={i: i for i in range(2 * n)},
        compiler_params=pltpu.CompilerParams(has_side_effects=EFFECT),
    )(*gs, *lands, *send, *recv, after)
    return outs[:n], outs[n:]


def _share_halves(ts, *, name):
    n = len(ts)

    def body(*refs):
        outs = refs[n:2 * n]
        send_sems, recv_sems = refs[2 * n:]
        x, y, c, _ = _place()
        sibling = (x, y, 1 - c)
        cps = []
        for i in range(n):
            r2 = ts[i].shape[0] // 2
            mine = outs[i].at[pl.ds(c * r2, r2)]
            cps.append(_remote(mine, mine, send_sems.at[i], recv_sems.at[i], sibling))
            cps[-1].start()
        for i in range(n):
            r2 = ts[i].shape[0] // 2
            got = outs[i].at[pl.ds((1 - c) * r2, r2)]
            _remote(got, got, send_sems.at[i], recv_sems.at[i], sibling).wait_recv()
        for cp in cps:
            cp.wait_send()

    return pl.pallas_call(
        body, name=name,
        out_shape=[jax.ShapeDtypeStruct(t.shape, t.dtype) for t in ts],
        in_specs=[ANY] * n, out_specs=[ANY] * n, input_output_aliases={i: i for i in range(n)},
        scratch_shapes=[pltpu.SemaphoreType.DMA((n,)), pltpu.SemaphoreType.DMA((n,))],
    )(*ts)


def _sum_pieces(land, g, idx, *, name):
    _, r2, C = land.shape
    tr = _tile(r2, max(16, (1 << 20) // C), 16)
    nr = r2 // tr

    def body(idx_ref, land_ref, own_ref, o_ref, acc_ref):
        d = pl.program_id(1)
        mine = d == idx_ref[0]

        @pl.when(d == 0)
        def _():
            acc_ref[...] = jnp.zeros_like(acc_ref)

        @pl.when(mine)
        def _():
            acc_ref[...] += own_ref[...].astype(F32)

        @pl.when(jnp.logical_not(mine))
        def _():
            acc_ref[...] += land_ref[...].astype(F32)

        @pl.when(d == N_DEV - 1)
        def _():
            o_ref[...] = acc_ref[...]

    return pl.pallas_call(
        body, name=name, out_shape=jax.ShapeDtypeStruct((2 * r2, C), F32),
        grid_spec=pltpu.PrefetchScalarGridSpec(
            num_scalar_prefetch=1, grid=(nr, N_DEV),
            in_specs=[pl.BlockSpec((None, tr, C), lambda i, d, ix: (jnp.where(d == ix[0], (d + 1) % N_DEV, d), i, 0)),
                      pl.BlockSpec((None, tr, C), lambda i, d, ix: (ix[1], ix[2] * nr + i, 0))],
            out_specs=pl.BlockSpec((tr, C), lambda i, d, ix: (ix[2] * nr + i, 0)),
            scratch_shapes=[pltpu.VMEM((tr, C), F32)]),
        compiler_params=_params("parallel", "arbitrary"),
    )(idx, land, g)


def _sum_devices(v, *, name):
    n, R, C = v.shape

    def body(v_ref, o_ref):
        acc = v_ref[0]
        for j in range(1, n):
            acc = acc + v_ref[j]
        o_ref[...] = acc

    return pl.pallas_call(body, name=name, out_shape=jax.ShapeDtypeStruct((R, C), F32),
                          in_specs=[VMEM_FULL], out_specs=VMEM_FULL)(v)


def _shard_cols(shards, lo, hi, width):
    out = []
    while lo < hi:
        j = lo // width
        end = min(hi, (j + 1) * width)
        out.append(shards[j][:, lo - j * width:end - j * width])
        lo = end
    return out


def _from_col_shards(g):
    return jnp.transpose(g, (1, 0, 2)).reshape(g.shape[1], N_CHIP * g.shape[2])


def _to_col_shards(w):
    R, N = w.shape
    return jnp.transpose(w.reshape(R, N_CHIP, N // N_CHIP), (1, 0, 2))


def _split_heads(w, widths):
    R, N = w.shape
    per = sum(widths)
    w3 = w.reshape(R, N // per, per)
    lo = w3[:, :, :widths[0]].reshape(R, -1)
    hi = w3[:, :, widths[0]:].reshape(R, -1)
    return jnp.concatenate([lo, hi], axis=1)


def _merge_heads(w, widths):
    R, N = w.shape
    H = N // sum(widths)
    lo = w[:, :H * widths[0]].reshape(R, H, widths[0])
    hi = w[:, H * widths[0]:].reshape(R, H, widths[1])
    return jnp.concatenate([lo, hi], axis=2).reshape(R, N)


def _t5_bucket(dist):
    max_exact = REL_BUCKETS // 2
    n = jnp.maximum(dist, 0)
    large = max_exact + (jnp.log(jnp.maximum(n, 1).astype(F32) / max_exact)
                         / jnp.log(jnp.asarray(REL_MAX_DIST / max_exact, F32))
                         * (REL_BUCKETS - max_exact)).astype(jnp.int32)
    large = jnp.minimum(large, REL_BUCKETS - 1)
    return jnp.where(n < max_exact, n, large)


def _rel_tables():
    a = jnp.arange(SWA_BLOCK)
    b = jnp.arange(2 * SWA_BLOCK)
    dist = SWA_BLOCK + a[:, None] - b[None, :]
    valid = jnp.logical_and(dist >= 0, dist < SWA_BLOCK)
    onehot = jnp.logical_and(_t5_bucket(dist)[..., None] == jnp.arange(REL_BUCKETS), valid[..., None])
    onehot = onehot.astype(F32).reshape(2 * SWA_BLOCK * SWA_BLOCK, REL_BUCKETS)
    negmask = jnp.where(valid, 0.0, NEG).astype(F32).reshape(1, -1)
    return onehot, negmask


def _rope_tables(S):
    pos = jnp.arange(S, dtype=F32)
    inv = ROPE_THETA ** (-jnp.arange(0, MLA_ROPE, 2, dtype=F32) / MLA_ROPE)
    ang = pos[:, None] * inv[None, :]
    ang = jnp.concatenate([ang, ang, ang, ang], axis=-1)
    return jnp.cos(ang), jnp.sin(ang)


def _flat_pad(parts, rows=8):
    flat = jnp.concatenate([p.reshape(1, -1) for p in parts], axis=1)
    n = flat.shape[1]
    width = -(-n // (rows * LANES)) * LANES
    return jnp.pad(flat, ((0, 0), (0, rows * width - n))).reshape(rows, width)


def _unflat(vec, shapes):
    flat = vec.reshape(-1)
    out, off = [], 0
    for s in shapes:
        n = 1
        for d in s:
            n *= d
        out.append(flat[off:off + n].reshape(s))
        off += n
    return out


def kernel(x, c, w_ada, b_ada, g_pre_mix, g_post_mix, w_in, g_q_lat, w_uq, g_kv_lat, w_ukv, rel_bias, sinks, w_o, g_pre_ffn, g_post_ffn, w_up, conv_w, conv_b, w_down, loss_target, m_w_ada, m_b_ada, m_g_pre_mix, m_g_post_mix, m_w_in, m_g_q_lat, m_w_uq, m_g_kv_lat, m_w_ukv, m_rel_bias, m_sinks, m_w_o, m_g_pre_ffn, m_g_post_ffn, m_w_up, m_conv_w, m_conv_b, m_w_down, v_w_ada, v_b_ada, v_g_pre_mix, v_g_post_mix, v_w_in, v_g_q_lat, v_w_uq, v_g_kv_lat, v_w_ukv, v_rel_bias, v_sinks, v_w_o, v_g_pre_ffn, v_g_post_ffn, v_w_up, v_conv_w, v_conv_b, v_w_down):
    S, D = x.shape[1], x.shape[2]
    Rq, Rkv = g_q_lat.shape[1], g_kv_lat.shape[1]
    H = D // MLA_V
    NH = D // SWA_HD
    KW = SWA_KVH * SWA_HD
    F = w_down.shape[1] * N_CHIP
    xi, yi, ci = lax.axis_index("x"), lax.axis_index("y"), lax.axis_index("c")
    chip = 2 * xi + yi
    me = 2 * chip + ci
    x2, tgt = x[0], loss_target[0]

    c_all = _allgather8(jnp.broadcast_to(c, (8, D)), name="gather_c")[:, 0, :]
    n3 = w_ada.shape[2]
    mod_part = _ada_fwd(c_all, w_ada[0], lax.dynamic_slice(b_ada, (0, chip * n3), (1, n3)), name="ada_fwd")
    mod_all = _allgather8(mod_part, name="gather_mod")
    mod_me = lax.dynamic_index_in_dim(mod_all[0::2], me, axis=1, keepdims=False).reshape(1, 6 * D)
    sh1, sc1, gt1, sh2, sc2, gt2 = [mod_me[:, k * D:(k + 1) * D] for k in range(6)]

    big = [w_in[0], w_uq[0], w_ukv[0], w_o[0], w_up[0], w_down[0]]
    local = [w.astype(BF16) for w in big]
    swaps = [False, False, False, False, True, False]
    send_a, recv_a, srcs_a, lands_a, token = _gather_start(local[:1], (mod_all,), swaps[:1], name="gather_start_in")
    send_b, recv_b, srcs_b, lands_b, token = _gather_start(local[1:], (token,), swaps[1:], name="gather_start_rest")
    send1, recv1, srcs, lands = send_a + send_b, recv_a + recv_b, srcs_a + srcs_b, lands_a + lands_b
    onehot, negmask = _rel_tables()
    npb = NH // (2 * SWA_KVH)
    rb_st = jnp.transpose(rel_bias.T.reshape(SWA_KVH, npb, 2, REL_BUCKETS), (0, 2, 1, 3)).reshape(NH, REL_BUCKETS)
    bias_m = (_matmul(rb_st, onehot.T, tie=token, name="rel_bias_table") + negmask).reshape(
        2 * SWA_KVH, npb * SWA_BLOCK, 2 * SWA_BLOCK)
    h = _modnorm_fwd(x2, g_pre_mix, sc1, sh1, name="pre_mix_norm")

    def whole(land, i):
        return lax.dynamic_update_index_in_dim(land, local[i], _slot(chip, swaps[i]), 0)

    def conv_slots(v):
        return jnp.stack([v[0], v[2], v[1], v[3]])

    s2, r2, l_in, _ = _gather_forward(send1[:1], recv1[:1], srcs[:1], lands[:1], h, swaps[:1], name="gather_forward_in")
    (l_in,) = _gather_finish(s2, r2, l_in, h, name="gather_finish_in")
    gin = whole(l_in, 0)
    o_kr = Rq + Rkv
    o_q = o_kr + MLA_ROPE
    o_g = o_q + NH * SWA_HD + 2 * KW
    n_gate, n_swa = 2 * D, o_g - o_q
    n_lat = -(-(o_q + MLA_ROPE) // PAIR) * PAIR
    runs = []
    for tl in range(D // PAIR):
        runs.append((o_g + tl * PAIR, o_g + (tl + 1) * PAIR, 2 * tl * PAIR))
        runs.append((o_g + D + tl * PAIR, o_g + D + (tl + 1) * PAIR, (2 * tl + 1) * PAIR))
    runs.append((o_q, o_g, n_gate))
    runs.append((0, o_q, n_gate + n_swa))
    csh = gin.shape[2]
    parts = []
    for lo, hi, _ in runs + [(o_kr, o_q, 0)]:
        parts += _shard_cols([gin[j] for j in range(N_CHIP)], lo, hi, csh)
    parts.append(jnp.zeros((D, n_lat - o_q - MLA_ROPE), BF16))
    w_in_all = jnp.concatenate(parts, axis=1)
    cos, sin = _rope_tables(S)
    sink_v = sinks.reshape(NH)

    z_lat = _matmul(h, w_in_all, bcols=(n_gate + n_swa, n_lat), name="in_proj_lat")
    z_swa = _matmul(h, w_in_all, bcols=(n_gate, n_swa), name="in_proj_swa")
    zg = _matmul(h, w_in_all, bcols=(0, n_gate), name="in_proj_gate")
    s2b, r2b, l_b, _ = _gather_forward(send1[1:4], recv1[1:4], srcs[1:4], lands[1:4], zg, swaps[1:4],
                                       name="gather_forward_attn")
    nq, nkv = _lat_norm_fwd(z_lat, g_q_lat, g_kv_lat, name="lat_norm")
    l_uq, l_ukv, l_o = _gather_finish(s2b, r2b, l_b, nq, name="gather_finish_attn")
    wuq = _split_heads(_from_col_shards(whole(l_uq, 1)), (MLA_NOPE, MLA_ROPE))
    wukv = _split_heads(_from_col_shards(whole(l_ukv, 2)), (MLA_NOPE, MLA_V))
    wo = whole(l_o, 3).reshape(D, D)
    q_raw = _matmul(nq, wuq, out_dtype=BF16, name="uq_proj")
    kv_raw = _matmul(nkv, wukv, out_dtype=BF16, name="ukv_proj")
    qp, kp, vv = _mla_pack_fwd(q_raw, kv_raw, z_lat, cos, sin, o_kr, name="mla_pack")
    o_a, lse = _flash_fwd(qp, kp, vv, name="mla_attn")
    s2c, r2c, l_c, tok_c = _gather_forward(send1[4:], recv1[4:], srcs[4:], lands[4:], o_a, swaps[4:],
                                           name="gather_forward_ffn")
    o_b = _swa_fwd(z_swa, bias_m, sink_v, name="swa_attn")
    mixin = _gate_fwd(zg, o_a, o_b, name="gate_mix")
    mix = _matmul(mixin, wo, tie=tok_c, name="o_proj")
    x1 = _resnorm_fwd(x2, mix, g_post_mix, gt1, name="post_mix_norm")
    h2 = _modnorm_fwd(x1, g_pre_ffn, sc2, sh2, name="pre_ffn_norm")
    l_up, l_down = _gather_finish(s2c, r2c, l_c, h2, name="gather_finish_ffn")
    cw_all = _allgather8(jnp.pad(conv_w[0], ((0, 5), (0, 0))), tie=l_down, name="gather_conv_w")[0::2, :3]
    cw = conv_slots(cw_all)
    cb = conv_slots(conv_b.reshape(N_CHIP, 1, -1))
    wup = whole(l_up, 4)
    wdown = whole(l_down, 5).reshape(F, D)
    t = _matmul(h2, wup, out_dtype=BF16, shards="out", name="up_proj")
    a = _conv_gate_fwd(t, cw, cb, name="conv_gate")
    yv = _matmul(a, wdown, name="down_proj")
    dout, dy, dg_post_ffn, dgt2, loss_tile = _resnorm_loss(x1, yv, g_post_ffn, gt2, tgt, name="post_ffn_norm_loss")

    big_params = dict(w_in=(w_in, m_w_in, v_w_in), w_uq=(w_uq, m_w_uq, v_w_uq), w_ukv=(w_ukv, m_w_ukv, v_w_ukv),
                      w_o=(w_o, m_w_o, v_w_o), w_up=(w_up, m_w_up, v_w_up), w_down=(w_down, m_w_down, v_w_down))
    res = {}

    def start(nms, gs):
        sw = [nm == "w_up" for nm in nms]
        send, recv, gsrc, glands, tok = _scatter_start(gs, sw, name="grads_start_" + nms[0])
        return (nms, send, recv, gsrc, glands), tok

    def finish(pending, after):
        nms, send, recv, gsrc, glands = pending
        gsrc, glands = _scatter_wait(send, recv, gsrc, glands, after, name="grads_wait_" + nms[0])
        halves = [_sum_pieces(l, g, jnp.stack([me, _slot(chip, nm == "w_up"), ci]).astype(jnp.int32), name="grad_sum_" + nm)
                  for l, g, nm in zip(glands, gsrc, nms)]
        for nm, g in zip(nms, _share_halves(halves, name="grads_share_" + nms[0])):
            w, m, v = big_params[nm]
            res[nm] = (g,) + tuple(_adamw(w[0], g, m[0], v[0], name="adamw_" + nm))

    dw_down = _matmul(a, dy, ta=True, out_dtype=BF16, name="down_proj_dw")
    p_down, tok = start(["w_down"], [dw_down.reshape(N_CHIP, F // N_CHIP, D)])
    da = _matmul(dy, wdown, tb=True, out_dtype=BF16, tie=tok, name="down_proj_dx")
    du, dcw_s, dcb_s = _conv_gate_bwd(t, da, cw, cb, name="conv_gate_bwd")
    dt = _conv_bwd_dt(du, cw, name="conv_bwd_dt")
    dw_up = _matmul(h2, dt, ta=True, out_dtype=BF16, shards="out", name="up_proj_dw")
    p_up, tok = start(["w_up"], [dw_up])
    dh2 = _matmul(dt, wup, tb=True, tie=tok, shards="k", name="up_proj_dx")
    dx1, dg_pre_ffn, dsc2, dsh2 = _modnorm_bwd(dh2, x1, g_pre_ffn, sc2, dout, name="pre_ffn_norm_bwd")
    dmix, dg_post_mix, dgt1 = _resnorm_bwd(dx1, mix, g_post_mix, gt1, name="post_mix_norm_bwd")
    dw_o = _matmul(mixin, dmix, ta=True, out_dtype=BF16, name="o_proj_dw")
    p_o, tok = start(["w_o"], [dw_o.reshape(N_CHIP, D // N_CHIP, D)])
    dmixin = _matmul(dmix, wo, tb=True, tie=tok, name="o_proj_dx")
    do_a, do_b, dzg = _gate_bwd(dmixin, zg, o_a, o_b, name="gate_mix_bwd")
    dqp, dkp, dvv = _flash_bwd(qp, kp, vv, o_a, do_a, lse, name="mla_attn_bwd")
    dq_raw, dkv_raw, dkr = _mla_pack_bwd(dqp, dkp, dvv, cos, sin, name="mla_pack_bwd")
    dw_uq_p = _matmul(nq, dq_raw, ta=True, out_dtype=BF16, name="uq_proj_dw")
    dw_ukv_p = _matmul(nkv, dkv_raw, ta=True, out_dtype=BF16, name="ukv_proj_dw")
    p_qkv, tok = start(["w_uq", "w_ukv"], [_to_col_shards(_merge_heads(dw_uq_p, (MLA_NOPE, MLA_ROPE))),
                                           _to_col_shards(_merge_heads(dw_ukv_p, (MLA_NOPE, MLA_V)))])
    dnq = _matmul(dq_raw, wuq, tb=True, tie=tok, name="uq_proj_dx")
    dnkv = _matmul(dkv_raw, wukv, tb=True, name="ukv_proj_dx")
    dz_lat, dg_q, dg_kv = _lat_norm_bwd(z_lat, dnq, dnkv, dkr, g_q_lat, g_kv_lat, name="lat_norm_bwd")
    dz_swa, dbias, dsink = _swa_bwd(z_swa, bias_m, sink_v, o_b, do_b, name="swa_attn_bwd")
    dz = jnp.concatenate([dzg, dz_swa, dz_lat], axis=1)
    dw_in_p = _matmul(h, dz, ta=True, out_dtype=BF16, name="in_proj_dw")
    dw_shards = []
    for j in range(N_CHIP):
        cols = []
        for lo, hi, at in sorted(runs):
            a0, a1 = max(lo, j * csh), min(hi, (j + 1) * csh)
            if a0 < a1:
                cols.append(dw_in_p[:, at + a0 - lo:at + a1 - lo])
        dw_shards.append(jnp.concatenate(cols, axis=1))
    p_in, tok = start(["w_in"], [jnp.stack(dw_shards)])
    dh = _matmul(dz, w_in_all, tb=True, tie=tok, name="in_proj_dx")
    grad_x, dg_pre_mix, dsc1, dsh1 = _modnorm_bwd(dh, x2, g_pre_mix, sc1, dx1, name="pre_mix_norm_bwd")
    drel_st = _matmul(dbias.reshape(NH, -1), onehot, tie=grad_x, name="rel_bias_bwd")
    for pending in (p_down, p_up, p_o, p_qkv):
        finish(pending, drel_st)
    drel = jnp.transpose(drel_st.reshape(SWA_KVH, 2, npb, REL_BUCKETS), (0, 2, 1, 3)).reshape(NH, REL_BUCKETS).T

    dcw = _from_col_shards(conv_slots(dcw_s))
    dcb = conv_slots(dcb_s).reshape(1, -1)
    dmod = jnp.concatenate([dsh1, dsc1, dgt1, dsh2, dsc2, dgt2], axis=1)
    small = [dmod, dg_pre_mix, dg_post_mix, dg_pre_ffn, dg_post_ffn, dg_q, dg_kv, drel, dsink[:, :NH], dcb, dcw]
    shapes = [p.shape for p in small]
    done = [res[nm][1] for nm in ("w_down", "w_up", "w_o", "w_uq", "w_ukv")]
    small_all = _allgather8(_flat_pad(small), tie=done, name="gather_small_grads")
    tot = _unflat(_sum_devices(small_all, name="sum_small_grads"), shapes)
    g_b_ada, g_pre_mix_g, g_post_mix_g, g_pre_ffn_g, g_post_ffn_g, g_q_g, g_kv_g, g_rel, g_sinks, g_cb, g_cw_full = tot
    dmod_all = small_all.reshape(N_DEV, -1)[:, :6 * D]
    g_w_ada = _ada_bwd(c_all.T, lax.dynamic_slice(dmod_all, (0, chip * n3), (N_DEV, n3)), name="ada_bwd")
    ncw = conv_w.shape[2]
    g_cw = lax.dynamic_slice(g_cw_full, (0, chip * ncw), (3, ncw))

    res["w_ada"] = (g_w_ada,) + tuple(_adamw(w_ada[0], g_w_ada, m_w_ada[0], v_w_ada[0], name="adamw_w_ada"))
    finish(p_in, g_w_ada)
    snames = ["b_ada", "g_pre_mix", "g_post_mix", "g_pre_ffn", "g_post_ffn", "g_q_lat", "g_kv_lat", "rel_bias", "sinks",
              "conv_b", "conv_w"]
    sw = [b_ada, g_pre_mix, g_post_mix, g_pre_ffn, g_post_ffn, g_q_lat, g_kv_lat, rel_bias, sinks, conv_b, conv_w]
    sm = [m_b_ada, m_g_pre_mix, m_g_post_mix, m_g_pre_ffn, m_g_post_ffn, m_g_q_lat, m_g_kv_lat, m_rel_bias, m_sinks,
          m_conv_b, m_conv_w]
    sv = [v_b_ada, v_g_pre_mix, v_g_post_mix, v_g_pre_ffn, v_g_post_ffn, v_g_q_lat, v_g_kv_lat, v_rel_bias, v_sinks,
          v_conv_b, v_conv_w]
    sg = [g_b_ada, g_pre_mix_g, g_post_mix_g, g_pre_ffn_g, g_post_ffn_g, g_q_g, g_kv_g, g_rel, g_sinks, g_cb, g_cw]
    sshapes = [w.shape for w in sw]
    sd, snm, snv = _adamw(_flat_pad(sw), _flat_pad(sg), _flat_pad(sm), _flat_pad(sv), name="adamw_small")
    sd, snm, snv = _unflat(sd, sshapes), _unflat(snm, sshapes), _unflat(snv, sshapes)
    for k, nm in enumerate(snames):
        res[nm] = (sg[k].reshape(sshapes[k]), sd[k], snm[k], snv[k])

    order = ["w_ada", "b_ada", "g_pre_mix", "g_post_mix", "w_in", "g_q_lat", "w_uq", "g_kv_lat", "w_ukv", "rel_bias", "sinks",
             "w_o", "g_pre_ffn", "g_post_ffn", "w_up", "conv_w", "conv_b", "w_down"]
    ref_shapes = dict(w_ada=w_ada.shape, w_in=w_in.shape, w_uq=w_uq.shape, w_ukv=w_ukv.shape, w_o=w_o.shape,
                      w_up=w_up.shape, w_down=w_down.shape)
    outs = []
    for k in range(4):
        for nm in order:
            arr = res[nm][k]
            outs.append(arr.reshape(ref_shapes[nm]) if nm in ref_shapes else arr)
    loss = lax.psum(loss_tile[0, 0], ("x", "y", "c"))
    return (loss, grad_x[None], *outs)
```

```python
import math

import jax
import jax.numpy as jnp
from jax import lax
from jax.experimental import pallas as pl
from jax.experimental.pallas import tpu as pltpu

F32 = jnp.float32
BF16 = jnp.bfloat16
MESH = pl.DeviceIdType.MESH
HIGHEST = lax.Precision.HIGHEST

N_DEV = 8
N_CHIP = 4
LANES = 128
MLA_NOPE = 128
MLA_ROPE = 64
MLA_V = 128
MLA_QK = MLA_NOPE + MLA_ROPE
MLA_QK_PAD = 256
ROPE_THETA = 10000.0
SWA_HD = 64
SWA_KVH = 4
SWA_BLOCK = 128
REL_BUCKETS = 32
REL_MAX_DIST = 128
PAIR = 512
EPS = 1e-6
NEG = -1e30
ADAM_LR = 0.001
ADAM_B1 = 0.9
ADAM_B2 = 0.999
ADAM_EPS = 1e-08
ADAM_WD = 0.01
ADAM_STEP = 10

ANY = pl.BlockSpec(memory_space=pl.ANY)
VMEM_FULL = pl.BlockSpec(memory_space=pltpu.VMEM)
SMEM_FULL = pl.BlockSpec(memory_space=pltpu.SMEM)


def _params(*sem):
    return pltpu.CompilerParams(dimension_semantics=sem if sem else None)


def _tied(body, tie):
    if tie is None:
        return body, [], []
    ties = list(tie) if isinstance(tie, (list, tuple)) else [tie]

    def tied_body(*refs):
        body(*refs[len(ties):])

    return tied_body, [ANY] * len(ties), ties


def _tile(n, pref, unit=LANES):
    best = None
    for t in range(unit, min(n, pref) + 1, unit):
        if n % t == 0:
            best = t
    return n if best is None else best


def _matmul(a, b, *, ta=False, tb=False, out_dtype=F32, tie=None, shards=None, bcols=None, name):
    a2 = a.shape[1:] if shards == "k" else a.shape
    b2 = b.shape[1:] if shards else b.shape
    nsh = b.shape[0] if shards else 1
    K, M = a2 if ta else a2[::-1]
    N, K2 = b2 if tb else b2[::-1]
    assert K == K2, (a.shape, b.shape, ta, tb)
    exact = a.dtype == F32
    col0 = 0
    if bcols is not None:
        assert not tb and shards is None
        col0, N = bcols
    tn = _tile(math.gcd(N, col0) if col0 else N, 1536)
    col0 //= tn
    tk = _tile(K, 2048)
    nkc = K // tk
    nk = nkc * (nsh if shards == "k" else 1)
    tm = M if M < 8 else _tile(M, 1024, LANES if ta else 8)
    dn = (((0 if ta else 1,), (1 if tb else 0,)), ((), ()))
    kax = 3 if shards == "out" else 2

    def product(a_ref, b_ref):
        return lax.dot_general(a_ref[...], b_ref[...], dn, preferred_element_type=F32,
                               precision=HIGHEST if exact else None)

    def body_acc(a_ref, b_ref, o_ref, acc_ref):
        k = pl.program_id(kax)

        @pl.when(k == 0)
        def _():
            acc_ref[...] = product(a_ref, b_ref)

        @pl.when(jnp.logical_and(k > 0, k < nk - 1))
        def _():
            acc_ref[...] += product(a_ref, b_ref)

        @pl.when(k == nk - 1)
        def _():
            o_ref[...] = (acc_ref[...] + product(a_ref, b_ref)).astype(o_ref.dtype)

    def body_one(a_ref, b_ref, o_ref):
        o_ref[...] = product(a_ref, b_ref).astype(o_ref.dtype)

    a_blk, b_blk = ((tk, tm) if ta else (tm, tk)), ((tn, tk) if tb else (tk, tn))
    a_at = (lambda i, k: (k, i)) if ta else (lambda i, k: (i, k))
    b_at = (lambda j, k: (j, k)) if tb else (lambda j, k: (k, j + col0))
    if shards == "out":
        grid = (nsh, M // tm, N // tn, nk)
        a_spec = pl.BlockSpec(a_blk, lambda s, i, j, k: a_at(i, k))
        b_spec = pl.BlockSpec((None,) + b_blk, lambda s, i, j, k: (s,) + b_at(j, k))
        o_spec = pl.BlockSpec((None, tm, tn), lambda s, i, j, k: (s, i, j))
        out_shape = jax.ShapeDtypeStruct((nsh, M, N), out_dtype)
        sem = ("parallel", "parallel", "parallel", "arbitrary")
    elif shards == "k":
        grid = (M // tm, N // tn, nk)
        a_spec = pl.BlockSpec((None,) + a_blk, lambda i, j, k: (k // nkc,) + a_at(i, k % nkc))
        b_spec = pl.BlockSpec((None,) + b_blk, lambda i, j, k: (k // nkc,) + b_at(j, k % nkc))
        o_spec = pl.BlockSpec((tm, tn), lambda i, j, k: (i, j))
        out_shape = jax.ShapeDtypeStruct((M, N), out_dtype)
        sem = ("parallel", "parallel", "arbitrary")
    else:
        grid = (M // tm, N // tn, nk)
        a_spec = pl.BlockSpec(a_blk, lambda i, j, k: a_at(i, k))
        b_spec = pl.BlockSpec(b_blk, lambda i, j, k: b_at(j, k))
        o_spec = pl.BlockSpec((tm, tn), lambda i, j, k: (i, j))
        out_shape = jax.ShapeDtypeStruct((M, N), out_dtype)
        sem = ("parallel", "parallel", "arbitrary")
    body, tspec, targ = _tied(body_one if nk == 1 else body_acc, tie)
    return pl.pallas_call(
        body, name=name, out_shape=out_shape, grid=grid, in_specs=tspec + [a_spec, b_spec], out_specs=o_spec,
        scratch_shapes=[] if nk == 1 else [pltpu.VMEM((tm, tn), F32)],
        compiler_params=_params(*sem),
    )(*targ, a, b)


def _row_tile(S, width):
    return _tile(S, max(8, (1 << 19) // width), 8)


def _rstd(x):
    return lax.rsqrt(jnp.mean(x * x, axis=-1, keepdims=True) + EPS)


def _acc_rows(ref, val, first):
    s = jnp.sum(val, axis=0, keepdims=True)

    @pl.when(first)
    def _():
        ref[...] = s

    @pl.when(jnp.logical_not(first))
    def _():
        ref[...] += s


def _modnorm_fwd(x, g, sc, sh, *, name):
    S, D = x.shape
    tr = _row_tile(S, D)

    def body(x_ref, g_ref, sc_ref, sh_ref, h_ref):
        xv = x_ref[...]
        n = (xv * _rstd(xv)) * g_ref[...]
        h_ref[...] = (n * (1.0 + sc_ref[...]) + sh_ref[...]).astype(BF16)

    row = pl.BlockSpec((tr, D), lambda i: (i, 0))
    vec = pl.BlockSpec((1, D), lambda i: (0, 0))
    return pl.pallas_call(
        body, name=name, out_shape=jax.ShapeDtypeStruct((S, D), BF16), grid=(S // tr,),
        in_specs=[row, vec, vec, vec], out_specs=row, compiler_params=_params("parallel"),
    )(x, g, sc, sh)


def _modnorm_bwd(dh, x, g, sc, dres, *, name):
    S, D = x.shape
    tr = _row_tile(S, D)

    def body(dh_ref, x_ref, g_ref, sc_ref, dres_ref, dx_ref, dg_ref, dsc_ref, dsh_ref):
        first = pl.program_id(0) == 0
        xv = x_ref[...]
        dhv = dh_ref[...]
        gv = g_ref[...]
        r = _rstd(xv)
        xhat = xv * r
        _acc_rows(dsh_ref, dhv, first)
        _acc_rows(dsc_ref, dhv * (xhat * gv), first)
        dn = dhv * (1.0 + sc_ref[...])
        _acc_rows(dg_ref, dn * xhat, first)
        dxhat = dn * gv
        proj = jnp.mean(dxhat * xhat, axis=-1, keepdims=True)
        dx_ref[...] = r * (dxhat - xhat * proj) + dres_ref[...]

    row = pl.BlockSpec((tr, D), lambda i: (i, 0))
    vec = pl.BlockSpec((1, D), lambda i: (0, 0))
    vshape = jax.ShapeDtypeStruct((1, D), F32)
    return pl.pallas_call(
        body, name=name,
        out_shape=(jax.ShapeDtypeStruct((S, D), F32), vshape, vshape, vshape), grid=(S // tr,),
        in_specs=[row, row, vec, vec, row], out_specs=(row, vec, vec, vec),
        compiler_params=_params("arbitrary"),
    )(dh, x, g, sc, dres)


def _resnorm_fwd(xres, m, g, gt, *, name):
    S, D = xres.shape
    tr = _row_tile(S, D)

    def body(x_ref, m_ref, g_ref, gt_ref, o_ref):
        mv = m_ref[...]
        o_ref[...] = x_ref[...] + gt_ref[...] * ((mv * _rstd(mv)) * g_ref[...])

    row = pl.BlockSpec((tr, D), lambda i: (i, 0))
    vec = pl.BlockSpec((1, D), lambda i: (0, 0))
    return pl.pallas_call(
        body, name=name, out_shape=jax.ShapeDtypeStruct((S, D), F32), grid=(S // tr,),
        in_specs=[row, row, vec, vec], out_specs=row, compiler_params=_params("parallel"),
    )(xres, m, g, gt)


def _resnorm_loss(xres, m, g, gt, target, *, name):
    S, D = xres.shape
    tr = _row_tile(S, D)

    def body(x_ref, m_ref, g_ref, gt_ref, t_ref, d_ref, dm_ref, dg_ref, dgt_ref, loss_ref):
        first = pl.program_id(0) == 0
        mv = m_ref[...]
        gv = g_ref[...]
        r = _rstd(mv)
        mhat = mv * r
        n = mhat * gv
        err = (x_ref[...] + gt_ref[...] * n) - t_ref[...]
        dv = err * (1.0 / D)
        d_ref[...] = dv
        part = 0.5 * jnp.sum(jnp.mean(err * err, axis=-1, keepdims=True), axis=0, keepdims=True)
        part = jnp.broadcast_to(part, loss_ref.shape)

        @pl.when(first)
        def _():
            loss_ref[...] = part

        @pl.when(jnp.logical_not(first))
        def _():
            loss_ref[...] += part

        _acc_rows(dgt_ref, dv * n, first)
        dn = dv * gt_ref[...]
        _acc_rows(dg_ref, dn * mhat, first)
        dmhat = dn * gv
        proj = jnp.mean(dmhat * mhat, axis=-1, keepdims=True)
        dm_ref[...] = (r * (dmhat - mhat * proj)).astype(BF16)

    row = pl.BlockSpec((tr, D), lambda i: (i, 0))
    vec = pl.BlockSpec((1, D), lambda i: (0, 0))
    vshape = jax.ShapeDtypeStruct((1, D), F32)
    return pl.pallas_call(
        body, name=name,
        out_shape=(jax.ShapeDtypeStruct((S, D), F32), jax.ShapeDtypeStruct((S, D), BF16), vshape, vshape,
                   jax.ShapeDtypeStruct((8, LANES), F32)), grid=(S // tr,),
        in_specs=[row, row, vec, vec, row], out_specs=(row, row, vec, vec, pl.BlockSpec((8, LANES), lambda i: (0, 0))),
        compiler_params=_params("arbitrary"),
    )(xres, m, g, gt, target)


def _resnorm_bwd(dout, m, g, gt, *, name):
    S, D = m.shape
    tr = _row_tile(S, D)

    def body(d_ref, m_ref, g_ref, gt_ref, dm_ref, dg_ref, dgt_ref):
        first = pl.program_id(0) == 0
        mv = m_ref[...]
        dv = d_ref[...]
        gv = g_ref[...]
        r = _rstd(mv)
        mhat = mv * r
        _acc_rows(dgt_ref, dv * (mhat * gv), first)
        dn = dv * gt_ref[...]
        _acc_rows(dg_ref, dn * mhat, first)
        dmhat = dn * gv
        proj = jnp.mean(dmhat * mhat, axis=-1, keepdims=True)
        dm_ref[...] = (r * (dmhat - mhat * proj)).astype(BF16)

    row = pl.BlockSpec((tr, D), lambda i: (i, 0))
    vec = pl.BlockSpec((1, D), lambda i: (0, 0))
    vshape = jax.ShapeDtypeStruct((1, D), F32)
    return pl.pallas_call(
        body, name=name, out_shape=(jax.ShapeDtypeStruct((S, D), BF16), vshape, vshape), grid=(S // tr,),
        in_specs=[row, row, vec, vec], out_specs=(row, vec, vec), compiler_params=_params("arbitrary"),
    )(dout, m, g, gt)


def _lat_norm_fwd(z_lat, g_q, g_kv, *, name):
    S, W = z_lat.shape
    Rq, Rkv = g_q.shape[1], g_kv.shape[1]
    tr = _row_tile(S, W)

    def body(z_ref, gq_ref, gkv_ref, nq_ref, nkv_ref):
        cq = z_ref[:, :Rq]
        ckv = z_ref[:, Rq:Rq + Rkv]
        nq_ref[...] = ((cq * _rstd(cq)) * gq_ref[...]).astype(BF16)
        nkv_ref[...] = ((ckv * _rstd(ckv)) * gkv_ref[...]).astype(BF16)

    return pl.pallas_call(
        body, name=name,
        out_shape=(jax.ShapeDtypeStruct((S, Rq), BF16), jax.ShapeDtypeStruct((S, Rkv), BF16)), grid=(S // tr,),
        in_specs=[pl.BlockSpec((tr, W), lambda i: (i, 0)), pl.BlockSpec((1, Rq), lambda i: (0, 0)),
                  pl.BlockSpec((1, Rkv), lambda i: (0, 0))],
        out_specs=(pl.BlockSpec((tr, Rq), lambda i: (i, 0)), pl.BlockSpec((tr, Rkv), lambda i: (i, 0))),
        compiler_params=_params("parallel"),
    )(z_lat, g_q, g_kv)


def _lat_norm_bwd(z_lat, dnq, dnkv, dkr, g_q, g_kv, *, name):
    S, W = z_lat.shape
    Rq, Rkv = g_q.shape[1], g_kv.shape[1]
    tr = _row_tile(S, W)

    def one(c, dn, gv):
        r = _rstd(c)
        chat = c * r
        dchat = dn * gv
        proj = jnp.mean(dchat * chat, axis=-1, keepdims=True)
        return r * (dchat - chat * proj), dn * chat

    def body(z_ref, dnq_ref, dnkv_ref, dkr_ref, gq_ref, gkv_ref, dz_ref, dgq_ref, dgkv_ref):
        first = pl.program_id(0) == 0
        dcq, pq = one(z_ref[:, :Rq], dnq_ref[...], gq_ref[...])
        dckv, pkv = one(z_ref[:, Rq:Rq + Rkv], dnkv_ref[...], gkv_ref[...])
        _acc_rows(dgq_ref, pq, first)
        _acc_rows(dgkv_ref, pkv, first)
        dz_ref[:, :Rq] = dcq.astype(BF16)
        dz_ref[:, Rq:Rq + Rkv] = dckv.astype(BF16)
        dz_ref[:, Rq + Rkv:Rq + Rkv + LANES] = dkr_ref[...].astype(BF16)
        if W > Rq + Rkv + LANES:
            dz_ref[:, Rq + Rkv + LANES:] = jnp.zeros((tr, W - Rq - Rkv - LANES), BF16)

    return pl.pallas_call(
        body, name=name,
        out_shape=(jax.ShapeDtypeStruct((S, W), BF16), jax.ShapeDtypeStruct((1, Rq), F32),
                   jax.ShapeDtypeStruct((1, Rkv), F32)), grid=(S // tr,),
        in_specs=[pl.BlockSpec((tr, W), lambda i: (i, 0)), pl.BlockSpec((tr, Rq), lambda i: (i, 0)),
                  pl.BlockSpec((tr, Rkv), lambda i: (i, 0)), pl.BlockSpec((tr, LANES), lambda i: (i, 0)),
                  pl.BlockSpec((1, Rq), lambda i: (0, 0)), pl.BlockSpec((1, Rkv), lambda i: (0, 0))],
        out_specs=(pl.BlockSpec((tr, W), lambda i: (i, 0)), pl.BlockSpec((1, Rq), lambda i: (0, 0)),
                   pl.BlockSpec((1, Rkv), lambda i: (0, 0))),
        compiler_params=_params("arbitrary"),
    )(z_lat, dnq, dnkv, dkr, g_q, g_kv)


def _rot(x, lo32):
    a = pltpu.roll(x, 32, 1)
    b = pltpu.roll(x, LANES - 32, 1)
    return jnp.where(lo32, -b, a)


def _rot_t(g, lo32):
    a = pltpu.roll(g, 32, 1)
    b = pltpu.roll(g, LANES - 32, 1)
    return jnp.where(lo32, b, -a)


def _mla_pack_fwd(q_raw, kv_raw, z_lat, cos, sin, kr_off, *, name):
    S = q_raw.shape[0]
    H = kv_raw.shape[1] // (MLA_NOPE + MLA_V)
    W = z_lat.shape[1]
    scale = MLA_QK ** -0.5
    tr = min(S, 128)
    nope_w = H * MLA_NOPE

    def body(q_ref, kv_ref, z_ref, cos_ref, sin_ref, qp_ref, kp_ref, v_ref):
        lane = lax.broadcasted_iota(jnp.int32, (tr, LANES), 1)
        lo32 = (lane % 64) < 32
        lo64 = lane < 64
        c = cos_ref[...]
        s = sin_ref[...]
        kr = z_ref[:, kr_off:kr_off + LANES]
        kr = (kr * c + _rot(kr, lo32) * s).astype(BF16)
        for hp in range(H // 2):
            xb = q_ref[:, nope_w + hp * LANES:nope_w + (hp + 1) * LANES].astype(F32)
            rb = (xb * c + _rot(xb, lo32) * s) * scale
            for e in range(2):
                h = 2 * hp + e
                base = h * MLA_QK_PAD
                qp_ref[:, base:base + LANES] = (q_ref[:, h * LANES:(h + 1) * LANES].astype(F32) * scale).astype(BF16)
                keep = lo64 if e == 0 else jnp.logical_not(lo64)
                qp_ref[:, base + LANES:base + 2 * LANES] = jnp.where(keep, rb, 0.0).astype(BF16)
                kp_ref[:, base:base + LANES] = kv_ref[:, h * LANES:(h + 1) * LANES].astype(BF16)
                kp_ref[:, base + LANES:base + 2 * LANES] = kr
        v_ref[...] = kv_ref[:, nope_w:].astype(BF16)

    return pl.pallas_call(
        body, name=name,
        out_shape=(jax.ShapeDtypeStruct((S, H * MLA_QK_PAD), BF16), jax.ShapeDtypeStruct((S, H * MLA_QK_PAD), BF16),
                   jax.ShapeDtypeStruct((S, H * MLA_V), BF16)), grid=(S // tr,),
        in_specs=[pl.BlockSpec((tr, q_raw.shape[1]), lambda i: (i, 0)), pl.BlockSpec((tr, kv_raw.shape[1]), lambda i: (i, 0)),
                  pl.BlockSpec((tr, W), lambda i: (i, 0)), pl.BlockSpec((tr, LANES), lambda i: (i, 0)),
                  pl.BlockSpec((tr, LANES), lambda i: (i, 0))],
        out_specs=(pl.BlockSpec((tr, H * MLA_QK_PAD), lambda i: (i, 0)), pl.BlockSpec((tr, H * MLA_QK_PAD), lambda i: (i, 0)),
                   pl.BlockSpec((tr, H * MLA_V), lambda i: (i, 0))),
        compiler_params=_params("parallel"),
    )(q_raw, kv_raw, z_lat, cos, sin)


def _mla_pack_bwd(dqp, dkp, dv, cos, sin, *, name):
    S = dqp.shape[0]
    H = dv.shape[1] // MLA_V
    scale = MLA_QK ** -0.5
    tr = min(S, 128)
    nope_w = H * MLA_NOPE

    def body(dqp_ref, dkp_ref, dv_ref, cos_ref, sin_ref, dq_ref, dkv_ref, dkr_ref):
        lane = lax.broadcasted_iota(jnp.int32, (tr, LANES), 1)
        lo32 = (lane % 64) < 32
        lo64 = lane < 64
        c = cos_ref[...]
        s = sin_ref[...]
        dkr2 = jnp.zeros((tr, LANES), F32)
        for hp in range(H // 2):
            be = (2 * hp) * MLA_QK_PAD
            bo = (2 * hp + 1) * MLA_QK_PAD
            g = jnp.where(lo64, dqp_ref[:, be + LANES:be + 2 * LANES].astype(F32),
                          dqp_ref[:, bo + LANES:bo + 2 * LANES].astype(F32)) * scale
            dq_ref[:, nope_w + hp * LANES:nope_w + (hp + 1) * LANES] = (g * c + _rot_t(g * s, lo32)).astype(BF16)
            for h, base in ((2 * hp, be), (2 * hp + 1, bo)):
                dq_ref[:, h * LANES:(h + 1) * LANES] = (dqp_ref[:, base:base + LANES].astype(F32) * scale).astype(BF16)
                dkv_ref[:, h * LANES:(h + 1) * LANES] = dkp_ref[:, base:base + LANES].astype(BF16)
                dkr2 = dkr2 + dkp_ref[:, base + LANES:base + 2 * LANES].astype(F32)
        dkr2 = dkr2 * c + _rot_t(dkr2 * s, lo32)
        dkr2 = dkr2 + pltpu.roll(dkr2, 64, 1)
        dkr_ref[...] = jnp.where(lo64, dkr2, 0.0)
        dkv_ref[:, nope_w:] = dv_ref[...].astype(BF16)

    return pl.pallas_call(
        body, name=name,
        out_shape=(jax.ShapeDtypeStruct((S, nope_w + H * MLA_ROPE), BF16), jax.ShapeDtypeStruct((S, 2 * nope_w), BF16),
                   jax.ShapeDtypeStruct((S, LANES), F32)), grid=(S // tr,),
        in_specs=[pl.BlockSpec((tr, H * MLA_QK_PAD), lambda i: (i, 0)), pl.BlockSpec((tr, H * MLA_QK_PAD), lambda i: (i, 0)),
                  pl.BlockSpec((tr, H * MLA_V), lambda i: (i, 0)), pl.BlockSpec((tr, LANES), lambda i: (i, 0)),
                  pl.BlockSpec((tr, LANES), lambda i: (i, 0))],
        out_specs=(pl.BlockSpec((tr, nope_w + H * MLA_ROPE), lambda i: (i, 0)), pl.BlockSpec((tr, 2 * nope_w), lambda i: (i, 0)),
                   pl.BlockSpec((tr, LANES), lambda i: (i, 0))),
        compiler_params=_params("parallel"),
    )(dqp, dkp, dv, cos, sin)


FLASH_HB = 2


def _causal_pairs(nb):
    qi = [i for i in range(nb) for j in range(i + 1)]
    kj = [j for i in range(nb) for j in range(i + 1)]
    return jnp.asarray(qi, jnp.int32), jnp.asarray(kj, jnp.int32)


def _scores(q, k, diagonal, t):
    s = lax.dot_general(q, k, (((1,), (1,)), ((), ())), preferred_element_type=F32)
    if diagonal:
        row = lax.broadcasted_iota(jnp.int32, (t, t), 0)
        col = lax.broadcasted_iota(jnp.int32, (t, t), 1)
        s = jnp.where(col <= row, s, NEG)
    return s


def _flash_fwd(qp, kp, v, *, name):
    S = qp.shape[0]
    H = v.shape[1] // MLA_V
    t = min(S, 512)
    nb = S // t
    HB = 2 * FLASH_HB
    qi, kj = _causal_pairs(nb)
    QW, VW = MLA_QK_PAD, MLA_V

    def body(qi_ref, kj_ref, q_ref, k_ref, v_ref, o_ref, lse_ref, m_s, l_s, acc_s):
        pr = pl.program_id(1)
        i = qi_ref[pr]
        j = kj_ref[pr]

        @pl.when(j == 0)
        def _():
            m_s[...] = jnp.full_like(m_s, NEG)
            l_s[...] = jnp.zeros_like(l_s)
            acc_s[...] = jnp.zeros_like(acc_s)

        def step(diagonal):
            state = [(m_s[hh], l_s[hh], acc_s[hh]) for hh in range(HB)]
            new = []
            for hh, (m_prev, l_prev, acc_prev) in enumerate(state):
                s = _scores(q_ref[:, hh * QW:(hh + 1) * QW], k_ref[:, hh * QW:(hh + 1) * QW], diagonal, t)
                m_cur = jnp.maximum(m_prev, jnp.max(s, axis=1, keepdims=True))
                alpha = jnp.exp(m_prev - m_cur)
                p = jnp.exp(s - m_cur[:, :1])
                l_new = alpha * l_prev + jnp.sum(p, axis=1, keepdims=True)
                acc = alpha * acc_prev + jnp.dot(p.astype(BF16), v_ref[:, hh * VW:(hh + 1) * VW], preferred_element_type=F32)
                new.append((m_cur, l_new, acc))
            for hh, (m_cur, l_new, acc) in enumerate(new):
                if diagonal:
                    o_ref[:, hh * VW:(hh + 1) * VW] = acc / l_new
                    lse_ref[hh] = m_cur + jnp.log(l_new)
                else:
                    l_s[hh] = l_new
                    acc_s[hh] = acc
                    m_s[hh] = m_cur

        @pl.when(i != j)
        def _():
            step(False)

        @pl.when(i == j)
        def _():
            step(True)

    return pl.pallas_call(
        body, name=name,
        out_shape=(jax.ShapeDtypeStruct((S, H * VW), F32), jax.ShapeDtypeStruct((H, S, LANES), F32)),
        grid_spec=pltpu.PrefetchScalarGridSpec(
            num_scalar_prefetch=2, grid=(H // HB, qi.shape[0]),
            in_specs=[pl.BlockSpec((t, HB * QW), lambda g, p, qi, kj: (qi[p], g)),
                      pl.BlockSpec((t, HB * QW), lambda g, p, qi, kj: (kj[p], g)),
                      pl.BlockSpec((t, HB * VW), lambda g, p, qi, kj: (kj[p], g))],
            out_specs=(pl.BlockSpec((t, HB * VW), lambda g, p, qi, kj: (qi[p], g)),
                       pl.BlockSpec((HB, t, LANES), lambda g, p, qi, kj: (g, qi[p], 0))),
            scratch_shapes=[pltpu.VMEM((HB, t, LANES), F32), pltpu.VMEM((HB, t, LANES), F32), pltpu.VMEM((HB, t, VW), F32)]),
        compiler_params=_params("parallel", "arbitrary"),
    )(qi, kj, qp, kp, v)


def _flash_bwd(qp, kp, v, o, do, lse, *, name):
    S = qp.shape[0]
    H = v.shape[1] // MLA_V
    t = min(S, 512)
    nb = S // t
    HB = FLASH_HB
    qi, kj = _causal_pairs(nb)
    npairs = qi.shape[0]
    QW, VW = MLA_QK_PAD, MLA_V
    tn = (((0,), (0,)), ((), ()))
    nt = (((1,), (1,)), ((), ()))

    def body(qi_ref, kj_ref, q_ref, k_ref, v_ref, o_ref, do_ref, lse_ref, dq_ref, dk_ref, dv_ref, dq_s, dk_s, dv_s):
        pr = pl.program_id(1)
        i = qi_ref[pr]
        j = kj_ref[pr]
        rows = pl.ds(pl.multiple_of(j * t, t), t)

        @pl.when(pr == 0)
        def _():
            dk_s[...] = jnp.zeros_like(dk_s)
            dv_s[...] = jnp.zeros_like(dv_s)

        @pl.when(j == 0)
        def _():
            dq_s[...] = jnp.zeros_like(dq_s)

        def step(diagonal):
            for hh in range(HB):
                q = q_ref[:, hh * QW:(hh + 1) * QW]
                k = k_ref[:, hh * QW:(hh + 1) * QW]
                dob = do_ref[:, hh * VW:(hh + 1) * VW]
                p = jnp.exp(_scores(q, k, diagonal, t) - lse_ref[hh][:, :1])
                delta = jnp.sum(dob.astype(F32) * o_ref[:, hh * VW:(hh + 1) * VW], axis=1, keepdims=True)
                dp = lax.dot_general(dob, v_ref[:, hh * VW:(hh + 1) * VW], nt, preferred_element_type=F32)
                dsb = (p * (dp - delta)).astype(BF16)
                dv_s[rows, hh * VW:(hh + 1) * VW] += lax.dot_general(p.astype(BF16), dob, tn, preferred_element_type=F32)
                dk_s[rows, hh * QW:(hh + 1) * QW] += lax.dot_general(dsb, q, tn, preferred_element_type=F32)
                dq = dq_s[:, hh * QW:(hh + 1) * QW] + jnp.dot(dsb, k, preferred_element_type=F32)
                if diagonal:
                    dq_ref[:, hh * QW:(hh + 1) * QW] = dq.astype(BF16)
                else:
                    dq_s[:, hh * QW:(hh + 1) * QW] = dq

        @pl.when(i != j)
        def _():
            step(False)

        @pl.when(i == j)
        def _():
            step(True)

        @pl.when(pr == npairs - 1)
        def _():
            dk_ref[...] = dk_s[...].astype(BF16)
            dv_ref[...] = dv_s[...].astype(BF16)

    qside = lambda g, p, qi, kj: (qi[p], g)
    kside = lambda g, p, qi, kj: (kj[p], g)
    whole = lambda g, p, qi, kj: (0, g)
    return pl.pallas_call(
        body, name=name,
        out_shape=(jax.ShapeDtypeStruct((S, H * QW), BF16), jax.ShapeDtypeStruct((S, H * QW), BF16),
                   jax.ShapeDtypeStruct((S, H * VW), BF16)),
        grid_spec=pltpu.PrefetchScalarGridSpec(
            num_scalar_prefetch=2, grid=(H // HB, npairs),
            in_specs=[pl.BlockSpec((t, HB * QW), qside), pl.BlockSpec((t, HB * QW), kside), pl.BlockSpec((t, HB * VW), kside),
                      pl.BlockSpec((t, HB * VW), qside), pl.BlockSpec((t, HB * VW), qside),
                      pl.BlockSpec((HB, t, LANES), lambda g, p, qi, kj: (g, qi[p], 0))],
            out_specs=(pl.BlockSpec((t, HB * QW), qside), pl.BlockSpec((S, HB * QW), whole), pl.BlockSpec((S, HB * VW), whole)),
            scratch_shapes=[pltpu.VMEM((t, HB * QW), F32), pltpu.VMEM((S, HB * QW), F32), pltpu.VMEM((S, HB * VW), F32)]),
        compiler_params=_params("parallel", "arbitrary"),
    )(qi, kj, qp, kp, v, o, do, lse)


def _swa_kv_halves(blk, hf, lo):
    if hf == 0:
        a = jnp.where(lo, blk, 0.0)
        b = pltpu.roll(a, 64, 1)
    else:
        b = jnp.where(lo, 0.0, blk)
        a = pltpu.roll(b, 64, 1)
    return a.astype(BF16), b.astype(BF16)


def _swa_softmax(qs, kx, bias, neg0, sk):
    s = lax.dot_general(qs, kx, (((1,), (1,)), ((), ())), preferred_element_type=F32) + bias + neg0
    m = jnp.maximum(jnp.max(s, axis=1, keepdims=True), sk)
    e = jnp.exp(s - m)
    es = jnp.exp(sk - m)
    inv = 1.0 / (jnp.sum(e, axis=1, keepdims=True) + es)
    return e * inv, es * inv


def _swa_stack(ref, kvh, npb, scale=None):
    parts = [ref[:, (kvh * npb + pb) * LANES:(kvh * npb + pb + 1) * LANES] for pb in range(npb)]
    x = jnp.concatenate(parts, axis=0)
    return x if scale is None else x * scale


def _swa_sink_col(sink_ref, kvh, e, npb):
    row = lax.broadcasted_iota(jnp.int32, (npb * SWA_BLOCK, 1), 0)
    col = jnp.zeros((npb * SWA_BLOCK, 1), F32) + sink_ref[2 * (kvh * npb) + e]
    for pb in range(1, npb):
        col = jnp.where(row >= pb * SWA_BLOCK, sink_ref[2 * (kvh * npb + pb) + e], col)
    return col


def _swa_fwd(z_swa, bias_st, sinks, *, name):
    S, W = z_swa.shape
    npb = bias_st.shape[1] // SWA_BLOCK
    NH = 2 * SWA_KVH * npb
    QW = NH * SWA_HD
    KW = SWA_KVH * SWA_HD
    nb = S // SWA_BLOCK
    B = SWA_BLOCK
    assert SWA_KVH % 2 == 0 and W == QW + 2 * KW

    def body(sink_ref, q_ref, kvc_ref, kvp_ref, b_ref, o_ref):
        n = pl.program_id(0)
        lo = lax.broadcasted_iota(jnp.int32, (2 * B, LANES), 1) < 64
        col = lax.broadcasted_iota(jnp.int32, (npb * B, 2 * B), 1)
        neg0 = jnp.where(jnp.logical_and(col < B, n == 0), NEG, 0.0)
        for kb in range(SWA_KVH // 2):
            kblk = jnp.concatenate([kvp_ref[:, kb * LANES:(kb + 1) * LANES], kvc_ref[:, kb * LANES:(kb + 1) * LANES]], axis=0)
            vblk = jnp.concatenate([kvp_ref[:, KW + kb * LANES:KW + (kb + 1) * LANES],
                                    kvc_ref[:, KW + kb * LANES:KW + (kb + 1) * LANES]], axis=0)
            for hf in range(2):
                kvh = 2 * kb + hf
                ks = _swa_kv_halves(kblk, hf, lo)
                vs = _swa_kv_halves(vblk, hf, lo)
                qs = _swa_stack(q_ref, kvh, npb, SWA_HD ** -0.5).astype(BF16)
                acc = jnp.zeros((npb * B, LANES), F32)
                for e in range(2):
                    p, _ = _swa_softmax(qs, ks[e], b_ref[2 * kvh + e], neg0, _swa_sink_col(sink_ref, kvh, e, npb))
                    acc = acc + jnp.dot(p.astype(BF16), vs[e], preferred_element_type=F32)
                for pb in range(npb):
                    P = kvh * npb + pb
                    o_ref[:, P * LANES:(P + 1) * LANES] = acc[pb * B:(pb + 1) * B]

    kvcol = QW // (2 * KW)
    assert QW % (2 * KW) == 0
    return pl.pallas_call(
        body, name=name,
        out_shape=jax.ShapeDtypeStruct((S, QW), F32), grid=(nb,),
        in_specs=[SMEM_FULL, pl.BlockSpec((B, QW), lambda n: (n, 0)), pl.BlockSpec((B, 2 * KW), lambda n: (n, kvcol)),
                  pl.BlockSpec((B, 2 * KW), lambda n: (jnp.maximum(n - 1, 0), kvcol)),
                  pl.BlockSpec(bias_st.shape, lambda n: (0, 0, 0))],
        out_specs=pl.BlockSpec((B, QW), lambda n: (n, 0)),
        compiler_params=_params("parallel"),
    )(sinks, z_swa, z_swa, z_swa, bias_st)


def _swa_bwd(z_swa, bias_st, sinks, o, do, *, name):
    S, W = z_swa.shape
    npb = bias_st.shape[1] // SWA_BLOCK
    NH = 2 * SWA_KVH * npb
    QW = NH * SWA_HD
    KW = SWA_KVH * SWA_HD
    nb = S // SWA_BLOCK
    B = SWA_BLOCK
    scale = SWA_HD ** -0.5
    tn = (((0,), (0,)), ((), ()))
    nt = (((1,), (1,)), ((), ()))

    def fold(x, hf, lo):
        x = x + pltpu.roll(x, 64, 1)
        return jnp.where(lo, x, 0.0) if hf == 0 else jnp.where(lo, 0.0, x)

    def body(sink_ref, q_ref, kvc_ref, kvp_ref, b_ref, o_ref, do_ref, dz_ref, dbias_ref, dsink_ref,
             cq_s, ck_s, cv_s, nq_s, nk_s, nv_s, pk_s, pv_s):
        n = pl.program_id(0)

        @pl.when(n == 0)
        def _():
            dbias_ref[...] = jnp.zeros_like(dbias_ref)
            dsink_ref[...] = jnp.zeros_like(dsink_ref)
            cq_s[...] = jnp.zeros_like(cq_s)
            ck_s[...] = jnp.zeros_like(ck_s)
            cv_s[...] = jnp.zeros_like(cv_s)

        @pl.when(n == nb)
        def _():
            pk_s[...] = jnp.zeros_like(pk_s)
            pv_s[...] = jnp.zeros_like(pv_s)

        @pl.when(n < nb)
        def _():
            lo = lax.broadcasted_iota(jnp.int32, (2 * B, LANES), 1) < 64
            lo1 = lax.broadcasted_iota(jnp.int32, (npb * B, LANES), 1) < 64
            lane1 = lax.broadcasted_iota(jnp.int32, (1, LANES), 1)
            col = lax.broadcasted_iota(jnp.int32, (npb * B, 2 * B), 1)
            neg0 = jnp.where(jnp.logical_and(col < B, n == 0), NEG, 0.0)
            dsink = jnp.zeros((1, LANES), F32)
            for kb in range(SWA_KVH // 2):
                kblk = jnp.concatenate([kvp_ref[:, kb * LANES:(kb + 1) * LANES], kvc_ref[:, kb * LANES:(kb + 1) * LANES]], axis=0)
                vblk = jnp.concatenate([kvp_ref[:, KW + kb * LANES:KW + (kb + 1) * LANES],
                                        kvc_ref[:, KW + kb * LANES:KW + (kb + 1) * LANES]], axis=0)
                dkblk = jnp.zeros((2 * B, LANES), F32)
                dvblk = jnp.zeros((2 * B, LANES), F32)
                for hf in range(2):
                    kvh = 2 * kb + hf
                    ks = _swa_kv_halves(kblk, hf, lo)
                    vs = _swa_kv_halves(vblk, hf, lo)
                    qs = _swa_stack(q_ref, kvh, npb, scale).astype(BF16)
                    dos = _swa_stack(do_ref, kvh, npb)
                    prod = dos * _swa_stack(o_ref, kvh, npb)
                    dob = dos.astype(BF16)
                    dkj = jnp.zeros((2 * B, LANES), F32)
                    dvj = jnp.zeros((2 * B, LANES), F32)
                    dqs = jnp.zeros((npb * B, LANES), F32)
                    for e in range(2):
                        keep = lo1 if e == 0 else jnp.logical_not(lo1)
                        p, psink = _swa_softmax(qs, ks[e], b_ref[2 * kvh + e], neg0, _swa_sink_col(sink_ref, kvh, e, npb))
                        delta = jnp.sum(jnp.where(keep, prod, 0.0), axis=1, keepdims=True)
                        dp = lax.dot_general(dob, vs[e], nt, preferred_element_type=F32)
                        ds = p * (dp - delta)
                        dbias_ref[2 * kvh + e] += ds
                        pd = psink * delta
                        for pb in range(npb):
                            dsh = -jnp.sum(pd[pb * B:(pb + 1) * B], axis=0, keepdims=True)
                            dsink = dsink + jnp.where(lane1 == 2 * (kvh * npb + pb) + e, dsh, 0.0)
                        dsb = ds.astype(BF16)
                        dqs = dqs + jnp.dot(dsb, ks[e], preferred_element_type=F32)
                        keep2 = lo if e == 0 else jnp.logical_not(lo)
                        dkj = dkj + jnp.where(keep2, lax.dot_general(dsb, qs, tn, preferred_element_type=F32), 0.0)
                        dvj = dvj + jnp.where(keep2, lax.dot_general(p.astype(BF16), dob, tn, preferred_element_type=F32), 0.0)
                    for pb in range(npb):
                        P = kvh * npb + pb
                        nq_s[:, P * LANES:(P + 1) * LANES] = dqs[pb * B:(pb + 1) * B] * scale
                    dkblk = dkblk + fold(dkj, hf, lo)
                    dvblk = dvblk + fold(dvj, hf, lo)
                pk_s[:, kb * LANES:(kb + 1) * LANES] = dkblk[:B]
                nk_s[:, kb * LANES:(kb + 1) * LANES] = dkblk[B:]
                pv_s[:, kb * LANES:(kb + 1) * LANES] = dvblk[:B]
                nv_s[:, kb * LANES:(kb + 1) * LANES] = dvblk[B:]
            dsink_ref[...] += dsink

        dz_ref[:, :QW] = cq_s[...].astype(BF16)
        dz_ref[:, QW:QW + KW] = (ck_s[...] + pk_s[...]).astype(BF16)
        dz_ref[:, QW + KW:] = (cv_s[...] + pv_s[...]).astype(BF16)

        @pl.when(n < nb)
        def _():
            cq_s[...] = nq_s[...]
            ck_s[...] = nk_s[...]
            cv_s[...] = nv_s[...]

    kvcol = QW // (2 * KW)
    cur = lambda n: (jnp.minimum(n, nb - 1), 0)
    return pl.pallas_call(
        body, name=name,
        out_shape=(jax.ShapeDtypeStruct((S, W), BF16), jax.ShapeDtypeStruct(bias_st.shape, F32),
                   jax.ShapeDtypeStruct((1, LANES), F32)),
        grid=(nb + 1,),
        in_specs=[SMEM_FULL, pl.BlockSpec((B, QW), cur), pl.BlockSpec((B, 2 * KW), lambda n: (jnp.minimum(n, nb - 1), kvcol)),
                  pl.BlockSpec((B, 2 * KW), lambda n: (jnp.maximum(jnp.minimum(n, nb - 1) - 1, 0), kvcol)),
                  pl.BlockSpec(bias_st.shape, lambda n: (0, 0, 0)), pl.BlockSpec((B, QW), cur), pl.BlockSpec((B, QW), cur)],
        out_specs=(pl.BlockSpec((B, W), lambda n: (jnp.maximum(n - 1, 0), 0)),
                   pl.BlockSpec(bias_st.shape, lambda n: (0, 0, 0)), pl.BlockSpec((1, LANES), lambda n: (0, 0))),
        scratch_shapes=[pltpu.VMEM((B, QW), F32), pltpu.VMEM((B, KW), F32), pltpu.VMEM((B, KW), F32),
                        pltpu.VMEM((B, QW), F32), pltpu.VMEM((B, KW), F32), pltpu.VMEM((B, KW), F32),
                        pltpu.VMEM((B, KW), F32), pltpu.VMEM((B, KW), F32)],
        compiler_params=_params("arbitrary"),
    )(sinks, z_swa, z_swa, z_swa, bias_st, o, do)


def _gate_fwd(zg, o_a, o_b, *, name):
    S, D = o_a.shape
    tr = min(S, 512)

    def body(z_ref, a_ref, b_ref, m_ref):
        ga = jax.nn.sigmoid(z_ref[:, :PAIR].astype(F32))
        gb = jax.nn.sigmoid(z_ref[:, PAIR:].astype(F32))
        m_ref[...] = (ga * a_ref[...] + gb * b_ref[...]).astype(BF16)

    col = pl.BlockSpec((tr, PAIR), lambda i, j: (i, j))
    return pl.pallas_call(
        body, name=name, out_shape=jax.ShapeDtypeStruct((S, D), BF16), grid=(S // tr, D // PAIR),
        in_specs=[pl.BlockSpec((tr, 2 * PAIR), lambda i, j: (i, j)), col, col], out_specs=col,
        compiler_params=_params("parallel", "parallel"),
    )(zg, o_a, o_b)


def _gate_bwd(dmix, zg, o_a, o_b, *, name):
    S, D = o_a.shape
    tr = min(S, 512)

    def body(d_ref, z_ref, a_ref, b_ref, da_ref, db_ref, dz_ref):
        d = d_ref[...]
        ga = jax.nn.sigmoid(z_ref[:, :PAIR].astype(F32))
        gb = jax.nn.sigmoid(z_ref[:, PAIR:].astype(F32))
        da_ref[...] = (d * ga).astype(BF16)
        db_ref[...] = d * gb
        dz_ref[:, :PAIR] = (d * a_ref[...] * (ga * (1.0 - ga))).astype(BF16)
        dz_ref[:, PAIR:] = (d * b_ref[...] * (gb * (1.0 - gb))).astype(BF16)

    col = pl.BlockSpec((tr, PAIR), lambda i, j: (i, j))
    wide = pl.BlockSpec((tr, 2 * PAIR), lambda i, j: (i, j))
    return pl.pallas_call(
        body, name=name,
        out_shape=(jax.ShapeDtypeStruct((S, D), BF16), jax.ShapeDtypeStruct((S, D), F32), jax.ShapeDtypeStruct((S, 2 * D), BF16)),
        grid=(S // tr, D // PAIR), in_specs=[col, wide, col, col], out_specs=(col, col, wide),
        compiler_params=_params("parallel", "parallel"),
    )(dmix, zg, o_a, o_b)


def _conv_u(t_ref, prev_ref, w_ref, b_ref, m, i):
    cur = t_ref[m].astype(F32)
    live = (i > 0).astype(F32)
    p6 = prev_ref[m, 14:15, :].astype(F32) * live
    p7 = prev_ref[m, 15:16, :].astype(F32) * live
    row = lax.broadcasted_iota(jnp.int32, cur.shape, 0)
    t1 = jnp.where(row == 0, p7, pltpu.roll(cur, 1, 0))
    t2 = jnp.where(row == 0, p6, jnp.where(row == 1, p7, pltpu.roll(cur, 2, 0)))
    u = ((b_ref[m] + w_ref[m, 0:1, :] * t2) + w_ref[m, 1:2, :] * t1) + w_ref[m, 2:3, :] * cur
    return u, cur, t1, t2


def _conv_specs(tr, tc):
    blk = pl.BlockSpec((2, tr, tc), lambda p, j, i: (p, i, j))
    prev = pl.BlockSpec((2, 16, tc), lambda p, j, i: (p, jnp.maximum(i * (tr // 16) - 1, 0), j))
    w3 = pl.BlockSpec((2, 3, tc), lambda p, j, i: (p, 0, j))
    w1 = pl.BlockSpec((2, 1, tc), lambda p, j, i: (p, 0, j))
    return blk, prev, w3, w1


def _conv_gate_fwd(t, cw, cb, *, name):
    _, S, C = t.shape
    tr, tc = min(S, 512), _tile(C, 1536)
    ncol = C // tc
    blk, prev, w3, w1 = _conv_specs(tr, tc)

    def body(t_ref, prev_ref, w_ref, b_ref, a_ref):
        i = pl.program_id(2)
        u1 = _conv_u(t_ref, prev_ref, w_ref, b_ref, 0, i)[0]
        u2 = _conv_u(t_ref, prev_ref, w_ref, b_ref, 1, i)[0]
        a_ref[...] = (jax.nn.silu(u1) * u2).astype(BF16)

    return pl.pallas_call(
        body, name=name, out_shape=jax.ShapeDtypeStruct((S, 2 * C), BF16), grid=(2, ncol, S // tr),
        in_specs=[blk, prev, w3, w1], out_specs=pl.BlockSpec((tr, tc), lambda p, j, i: (i, p * ncol + j)),
        compiler_params=_params("parallel", "parallel", "parallel"),
    )(t, t, cw, cb)


def _conv_gate_bwd(t, da, cw, cb, *, name):
    _, S, C = t.shape
    tr, tc = min(S, 256), _tile(C, 1536)
    ncol = C // tc
    blk, prev, w3, w1 = _conv_specs(tr, tc)

    def body(t_ref, prev_ref, da_ref, w_ref, b_ref, du_ref, dw_ref, db_ref):
        i = pl.program_id(2)
        first = i == 0
        u1, c1, a1, b1 = _conv_u(t_ref, prev_ref, w_ref, b_ref, 0, i)
        u2, c2, a2, b2 = _conv_u(t_ref, prev_ref, w_ref, b_ref, 1, i)
        d = da_ref[...].astype(F32)
        sg = jax.nn.sigmoid(u1)
        du1 = d * u2 * (sg * (1.0 + u1 * (1.0 - sg)))
        du2 = d * (u1 * sg)
        for m, (du, cur, t1, t2) in enumerate(((du1, c1, a1, b1), (du2, c2, a2, b2))):
            du_ref[m] = du.astype(BF16)
            dw = jnp.concatenate([jnp.sum(du * t2, axis=0, keepdims=True), jnp.sum(du * t1, axis=0, keepdims=True),
                                  jnp.sum(du * cur, axis=0, keepdims=True)], axis=0)
            db = jnp.sum(du, axis=0, keepdims=True)

            @pl.when(first)
            def _():
                dw_ref[m] = dw
                db_ref[m] = db

            @pl.when(jnp.logical_not(first))
            def _():
                dw_ref[m] += dw
                db_ref[m] += db

    return pl.pallas_call(
        body, name=name,
        out_shape=(jax.ShapeDtypeStruct(t.shape, BF16), jax.ShapeDtypeStruct(cw.shape, F32), jax.ShapeDtypeStruct(cb.shape, F32)),
        grid=(2, ncol, S // tr),
        in_specs=[blk, prev, pl.BlockSpec((tr, tc), lambda p, j, i: (i, p * ncol + j)), w3, w1], out_specs=(blk, w3, w1),
        compiler_params=_params("parallel", "parallel", "arbitrary"),
    )(t, t, da, cw, cb)


def _conv_bwd_dt(du, cw, *, name):
    _, S, C = du.shape
    tr, tc = min(S, 512), _tile(C, 1536)
    nrow = S // tr
    blk, _, w3, _ = _conv_specs(tr, tc)
    nxt = pl.BlockSpec((2, 16, tc), lambda p, j, i: (p, jnp.minimum((i + 1) * (tr // 16), S // 16 - 1), j))

    def body(d_ref, next_ref, w_ref, dt_ref):
        i = pl.program_id(2)
        live = (i < nrow - 1).astype(F32)
        for m in range(2):
            cur = d_ref[m].astype(F32)
            n0 = next_ref[m, 0:1, :].astype(F32) * live
            n1 = next_ref[m, 1:2, :].astype(F32) * live
            row = lax.broadcasted_iota(jnp.int32, cur.shape, 0)
            d1 = jnp.where(row == tr - 1, n0, pltpu.roll(cur, tr - 1, 0))
            d2 = jnp.where(row == tr - 1, n1, jnp.where(row == tr - 2, n0, pltpu.roll(cur, tr - 2, 0)))
            dt_ref[m] = ((w_ref[m, 2:3, :] * cur + w_ref[m, 1:2, :] * d1) + w_ref[m, 0:1, :] * d2).astype(BF16)

    return pl.pallas_call(
        body, name=name, out_shape=jax.ShapeDtypeStruct(du.shape, BF16), grid=(2, C // tc, nrow),
        in_specs=[blk, nxt, w3], out_specs=blk, compiler_params=_params("parallel", "parallel", "parallel"),
    )(du, du, cw)


def _ada_fwd(c_all, w, b, *, name):
    Bn, D = c_all.shape
    N = w.shape[1]
    tn = _tile(N, 512)

    def body(c_ref, w_ref, b_ref, o_ref):
        o_ref[...] = jnp.dot(jax.nn.silu(c_ref[...]), w_ref[...], preferred_element_type=F32, precision=HIGHEST) + b_ref[...]

    return pl.pallas_call(
        body, name=name, out_shape=jax.ShapeDtypeStruct((Bn, N), F32), grid=(N // tn,),
        in_specs=[pl.BlockSpec((Bn, D), lambda j: (0, 0)), pl.BlockSpec((D, tn), lambda j: (0, j)),
                  pl.BlockSpec((1, tn), lambda j: (0, j))],
        out_specs=pl.BlockSpec((Bn, tn), lambda j: (0, j)), compiler_params=_params("parallel"),
    )(c_all, w, b)


def _ada_bwd(c_all_t, dmod, *, name):
    D, Bn = c_all_t.shape
    N = dmod.shape[1]
    tm = _tile(D, 512, 8)
    tn = _tile(N, 1536)

    def body(c_ref, d_ref, o_ref):
        o_ref[...] = jnp.dot(jax.nn.silu(c_ref[...]), d_ref[...], preferred_element_type=F32, precision=HIGHEST)

    return pl.pallas_call(
        body, name=name, out_shape=jax.ShapeDtypeStruct((D, N), F32), grid=(D // tm, N // tn),
        in_specs=[pl.BlockSpec((tm, Bn), lambda i, j: (i, 0)), pl.BlockSpec((Bn, tn), lambda i, j: (0, j))],
        out_specs=pl.BlockSpec((tm, tn), lambda i, j: (i, j)), compiler_params=_params("parallel", "parallel"),
    )(c_all_t, dmod)


def _adamw(w, g, m, v, *, name):
    R, C = w.shape
    tr = R if R * C <= (1 << 19) else _tile(R, max(8, (1 << 19) // C), 8)

    def body(w_ref, g_ref, m_ref, v_ref, d_ref, nm_ref, nv_ref):
        gv = g_ref[...]
        nm = ADAM_B1 * m_ref[...] + (1.0 - ADAM_B1) * gv
        nv = ADAM_B2 * v_ref[...] + (1.0 - ADAM_B2) * (gv * gv)
        m_hat = nm / (1.0 - ADAM_B1 ** ADAM_STEP)
        v_hat = nv / (1.0 - ADAM_B2 ** ADAM_STEP)
        d_ref[...] = -ADAM_LR * (m_hat / (jnp.sqrt(v_hat) + ADAM_EPS) + ADAM_WD * w_ref[...])
        nm_ref[...] = nm
        nv_ref[...] = nv

    blk = pl.BlockSpec((tr, C), lambda i: (i, 0))
    shp = jax.ShapeDtypeStruct((R, C), F32)
    return pl.pallas_call(
        body, name=name, out_shape=(shp, shp, shp), grid=(R // tr,), in_specs=[blk] * 4, out_specs=(blk,) * 3,
        compiler_params=_params("parallel"),
    )(w, g, m, v)


def _place():
    x, y, c = lax.axis_index("x"), lax.axis_index("y"), lax.axis_index("c")
    return x, y, c, [(1 - x, y), (x, 1 - y), (1 - x, 1 - y)]


def _remote(src, dst, send_sem, recv_sem, dev):
    return pltpu.make_async_remote_copy(src_ref=src, dst_ref=dst, send_sem=send_sem, recv_sem=recv_sem,
                                        device_id=dev, device_id_type=MESH)


def _allgather8(v, *, tie=None, name):
    R, C = v.shape

    def body(v_ref, out_ref, send_sems, recv_sems, local_sem):
        x, y, c, chips = _place()
        me, sibling = (x, y, c), (x, y, 1 - c)

        def rows(px, py, pc):
            return out_ref.at[pl.ds((4 * px + 2 * py + pc) * R, R), :]

        def copy(k, block, to, src=None):
            return _remote(rows(*block) if src is None else src, rows(*block), send_sems.at[k], recv_sems.at[k], to)

        mine = pltpu.make_async_copy(v_ref, rows(*me), local_sem)
        mine.start()
        first = [copy(0, me, sibling, src=v_ref)]
        first += [copy(1 + j, me, (*chip, c), src=v_ref) for j, chip in enumerate(chips)]
        for cp in first:
            cp.start()
        passed = [copy(4 + j, (*chip, c), sibling) for j, chip in enumerate(chips)]
        for j, chip in enumerate(chips):
            copy(1 + j, (*chip, c), me).wait_recv()
            passed[j].start()
        copy(0, sibling, me).wait_recv()
        for j, chip in enumerate(chips):
            copy(4 + j, (*chip, 1 - c), me).wait_recv()
        for cp in first + passed:
            cp.wait_send()
        mine.wait()

    body, tspec, targ = _tied(body, tie)
    out = pl.pallas_call(
        body, name=name, out_shape=jax.ShapeDtypeStruct((N_DEV * R, C), v.dtype),
        in_specs=tspec + [VMEM_FULL], out_specs=VMEM_FULL,
        scratch_shapes=[pltpu.SemaphoreType.DMA((7,)), pltpu.SemaphoreType.DMA((7,)), pltpu.SemaphoreType.DMA],
    )(*targ, v)
    return out.reshape(N_DEV, R, C)


SEM = pl.BlockSpec(memory_space=pltpu.SEMAPHORE)
HBM = pl.BlockSpec(memory_space=pltpu.HBM)
EFFECT = pltpu.SideEffectType.DATAFLOW_SIDE_EFFECTING
DMA_SEM = pltpu.SemaphoreType.DMA(())


def _in_hbm(a):
    return pltpu.with_memory_space_constraint(a, pltpu.HBM)


def _three_halves(land, r2):
    return land.at[pl.ds(0, N_CHIP - 1), pl.ds(0, r2)]


def _slot(chip, swap):
    return (chip % 2) * 2 + chip // 2 if swap else chip


def _gather_start(ws, after, swaps, *, name):
    n = len(ws)
    na = len(after)
    lands = [lax.empty((N_CHIP,) + w.shape, w.dtype) for w in ws]

    def body(*refs):
        w_refs, land_refs = refs[:n], refs[n:2 * n]
        send, recv = refs[2 * n + na:3 * n + na], refs[3 * n + na:4 * n + na]
        token = refs[6 * n + na]
        x, y, c, chips = _place()
        k = 2 * x + y
        for i in range(n):
            r2 = ws[i].shape[0] // 2
            for cx, cy in chips:
                _remote(w_refs[i].at[pl.ds(c * r2, r2)], land_refs[i].at[_slot(k, swaps[i]), pl.ds(c * r2, r2)], send[i], recv[i],
                        (cx, cy, c)).start()
        token[...] = jnp.zeros_like(token)

    outs = pl.pallas_call(
        body, name=name,
        out_shape=[DMA_SEM] * (2 * n) + [pltpu.HBM(w.shape, w.dtype) for w in ws] + [pltpu.HBM(l.shape, l.dtype) for l in lands]
        + [jax.ShapeDtypeStruct((8, LANES), F32)],
        in_specs=[HBM] * (2 * n) + [ANY] * na, out_specs=[SEM] * (2 * n) + [HBM] * (2 * n) + [VMEM_FULL],
        input_output_aliases={i: 2 * n + i for i in range(2 * n)},
        compiler_params=pltpu.CompilerParams(has_side_effects=EFFECT),
    )(*[_in_hbm(w) for w in ws], *[_in_hbm(l) for l in lands], *after)
    return outs[:n], outs[n:2 * n], outs[2 * n:3 * n], outs[3 * n:4 * n], outs[4 * n]


def _gather_forward(send, recv, ws, lands, after, swaps, *, name):
    n = len(ws)

    def body(*refs):
        w_refs, land_refs = refs[:n], refs[n:2 * n]
        send1, recv1 = refs[2 * n:3 * n], refs[3 * n:4 * n]
        send2, recv2 = refs[4 * n + 1 + 2 * n:4 * n + 1 + 3 * n], refs[4 * n + 1 + 3 * n:4 * n + 1 + 4 * n]
        x, y, c, chips = _place()
        sibling = (x, y, 1 - c)
        for i in range(n):
            r2 = ws[i].shape[0] // 2
            win = _three_halves(land_refs[i], r2)
            done = _remote(win, win, send1[i], recv1[i], sibling)
            done.wait_send()
            done.wait_recv()
            for cx, cy in chips:
                got = land_refs[i].at[_slot(2 * cx + cy, swaps[i]), pl.ds(c * r2, r2)]
                _remote(got, got, send2[i], recv2[i], sibling).start()
        token = refs[8 * n + 1]
        token[...] = jnp.zeros_like(token)

    outs = pl.pallas_call(
        body, name=name,
        out_shape=[pltpu.HBM(w.shape, w.dtype) for w in ws] + [pltpu.HBM(l.shape, l.dtype) for l in lands] + [DMA_SEM] * (2 * n)
        + [jax.ShapeDtypeStruct((8, LANES), F32)],
        in_specs=[HBM] * (2 * n) + [SEM] * (2 * n) + [ANY], out_specs=[HBM] * (2 * n) + [SEM] * (2 * n) + [VMEM_FULL],
        input_output_aliases={i: i for i in range(2 * n)},
        compiler_params=pltpu.CompilerParams(has_side_effects=EFFECT),
    )(*ws, *lands, *send, *recv, after)
    return outs[2 * n:3 * n], outs[3 * n:4 * n], outs[n:2 * n], outs[4 * n]


def _gather_finish(send, recv, lands, after, *, name):
    n = len(lands)

    def body(*refs):
        land_refs = refs[:n]
        send2, recv2 = refs[n:2 * n], refs[2 * n:3 * n]
        x, y, c, _ = _place()
        for i in range(n):
            win = _three_halves(land_refs[i], lands[i].shape[1] // 2)
            done = _remote(win, win, send2[i], recv2[i], (x, y, 1 - c))
            done.wait_send()
            done.wait_recv()

    return pl.pallas_call(
        body, name=name,
        out_shape=[pltpu.HBM(l.shape, l.dtype) for l in lands],
        in_specs=[HBM] * n + [SEM] * (2 * n) + [ANY], out_specs=[HBM] * n,
        input_output_aliases={i: i for i in range(n)},
        compiler_params=pltpu.CompilerParams(has_side_effects=EFFECT),
    )(*lands, *send, *recv, after)


def _scatter_start(gs, swaps, *, name):
    n = len(gs)
    lands = [lax.empty((N_DEV, g.shape[1] // 2, g.shape[2]), g.dtype) for g in gs]

    def body(*refs):
        g_refs, land_refs = refs[:n], refs[n:2 * n]
        send, recv = refs[2 * n:3 * n], refs[3 * n:4 * n]
        token = refs[6 * n]
        x, y, c, chips = _place()
        k = 2 * x + y
        me = 2 * k + c
        for i in range(n):
            r2 = gs[i].shape[1] // 2
            for cx, cy in chips:
                for cc in range(2):
                    _remote(g_refs[i].at[_slot(2 * cx + cy, swaps[i]), pl.ds(cc * r2, r2)], land_refs[i].at[me], send[i], recv[i],
                            (cx, cy, cc)).start()
            _remote(g_refs[i].at[_slot(k, swaps[i]), pl.ds((1 - c) * r2, r2)], land_refs[i].at[me], send[i], recv[i],
                    (x, y, 1 - c)).start()
        token[...] = jnp.zeros_like(token)

    outs = pl.pallas_call(
        body, name=name,
        out_shape=[DMA_SEM] * (2 * n) + [pltpu.HBM(g.shape, g.dtype) for g in gs] + [pltpu.HBM(l.shape, l.dtype) for l in lands]
        + [jax.ShapeDtypeStruct((8, LANES), F32)],
        in_specs=[HBM] * (2 * n), out_specs=[SEM] * (2 * n) + [HBM] * (2 * n) + [VMEM_FULL],
        input_output_aliases={i: 2 * n + i for i in range(2 * n)},
        compiler_params=pltpu.CompilerParams(has_side_effects=EFFECT),
    )(*[_in_hbm(g) for g in gs], *[_in_hbm(l) for l in lands])
    return outs[:n], outs[n:2 * n], outs[2 * n:3 * n], outs[3 * n:4 * n], outs[4 * n]


def _scatter_wait(send, recv, gs, lands, after, *, name):
    n = len(gs)

    def body(*refs):
        land_refs = refs[n:2 * n]
        send1, recv1 = refs[2 * n:3 * n], refs[3 * n:4 * n]
        x, y, c, _ = _place()
        for i in range(n):
            win = land_refs[i].at[pl.ds(0, N_DEV - 1)]
            done = _remote(win, win, send1[i], recv1[i], (x, y, 1 - c))
            done.wait_send()
            done.wait_recv()

    outs = pl.pallas_call(
        body, name=name,
        out_shape=[pltpu.HBM(g.shape, g.dtype) for g in gs] + [pltpu.HBM(l.shape, l.dtype) for l in lands],
        in_specs=[HBM] * (2 * n) + [SEM] * (2 * n) + [ANY], out_specs=[HBM] * (2 * n),
        input_output_aliases={i: i for i in range(2 * n)},
        compiler_params=pltpu.CompilerParams(has_side_effects=EFFECT),
    )(*gs, *lands, *send, *recv, after)
    return outs[:n], outs[n:]


def _share_halves(ts, *, name):
    n = len(ts)

    def body(*refs):
        outs = refs[n:2 * n]
        send_sems, recv_sems = refs[2 * n:]
        x, y, c, _ = _place()
        sibling = (x, y, 1 - c)
        cps = []
        for i in range(n):
            r2 = ts[i].shape[0] // 2
            mine = outs[i].at[pl.ds(c * r2, r2)]
            cps.append(_remote(mine, mine, send_sems.at[i], recv_sems.at[i], sibling))
            cps[-1].start()
        for i in range(n):
            r2 = ts[i].shape[0] // 2
            got = outs[i].at[pl.ds((1 - c) * r2, r2)]
            _remote(got, got, send_sems.at[i], recv_sems.at[i], sibling).wait_recv()
        for cp in cps:
            cp.wait_send()

    return pl.pallas_call(
        body, name=name,
        out_shape=[jax.ShapeDtypeStruct(t.shape, t.dtype) for t in ts],
        in_specs=[ANY] * n, out_specs=[ANY] * n, input_output_aliases={i: i for i in range(n)},
        scratch_shapes=[pltpu.SemaphoreType.DMA((n,)), pltpu.SemaphoreType.DMA((n,))],
    )(*ts)


def _sum_pieces(land, g, idx, *, name):
    _, r2, C = land.shape
    tr = _tile(r2, max(16, (1 << 20) // C), 16)
    nr = r2 // tr

    def body(idx_ref, land_ref, own_ref, o_ref, acc_ref):
        d = pl.program_id(1)
        mine = d == idx_ref[0]

        @pl.when(d == 0)
        def _():
            acc_ref[...] = jnp.zeros_like(acc_ref)

        @pl.when(mine)
        def _():
            acc_ref[...] += own_ref[...].astype(F32)

        @pl.when(jnp.logical_not(mine))
        def _():
            acc_ref[...] += land_ref[...].astype(F32)

        @pl.when(d == N_DEV - 1)
        def _():
            o_ref[...] = acc_ref[...]

    return pl.pallas_call(
        body, name=name, out_shape=jax.ShapeDtypeStruct((2 * r2, C), F32),
        grid_spec=pltpu.PrefetchScalarGridSpec(
            num_scalar_prefetch=1, grid=(nr, N_DEV),
            in_specs=[pl.BlockSpec((None, tr, C), lambda i, d, ix: (jnp.where(d == ix[0], (d + 1) % N_DEV, d), i, 0)),
                      pl.BlockSpec((None, tr, C), lambda i, d, ix: (ix[1], ix[2] * nr + i, 0))],
            out_specs=pl.BlockSpec((tr, C), lambda i, d, ix: (ix[2] * nr + i, 0)),
            scratch_shapes=[pltpu.VMEM((tr, C), F32)]),
        compiler_params=_params("parallel", "arbitrary"),
    )(idx, land, g)


def _sum_devices(v, *, name):
    n, R, C = v.shape

    def body(v_ref, o_ref):
        acc = v_ref[0]
        for j in range(1, n):
            acc = acc + v_ref[j]
        o_ref[...] = acc

    return pl.pallas_call(body, name=name, out_shape=jax.ShapeDtypeStruct((R, C), F32),
                          in_specs=[VMEM_FULL], out_specs=VMEM_FULL)(v)


def _shard_cols(shards, lo, hi, width):
    out = []
    while lo < hi:
        j = lo // width
        end = min(hi, (j + 1) * width)
        out.append(shards[j][:, lo - j * width:end - j * width])
        lo = end
    return out


def _from_col_shards(g):
    return jnp.transpose(g, (1, 0, 2)).reshape(g.shape[1], N_CHIP * g.shape[2])


def _to_col_shards(w):
    R, N = w.shape
    return jnp.transpose(w.reshape(R, N_CHIP, N // N_CHIP), (1, 0, 2))


def _split_heads(w, widths):
    R, N = w.shape
    per = sum(widths)
    w3 = w.reshape(R, N // per, per)
    lo = w3[:, :, :widths[0]].reshape(R, -1)
    hi = w3[:, :, widths[0]:].reshape(R, -1)
    return jnp.concatenate([lo, hi], axis=1)


def _merge_heads(w, widths):
    R, N = w.shape
    H = N // sum(widths)
    lo = w[:, :H * widths[0]].reshape(R, H, widths[0])
    hi = w[:, H * widths[0]:].reshape(R, H, widths[1])
    return jnp.concatenate([lo, hi], axis=2).reshape(R, N)


def _t5_bucket(dist):
    max_exact = REL_BUCKETS // 2
    n = jnp.maximum(dist, 0)
    large = max_exact + (jnp.log(jnp.maximum(n, 1).astype(F32) / max_exact)
                         / jnp.log(jnp.asarray(REL_MAX_DIST / max_exact, F32))
                         * (REL_BUCKETS - max_exact)).astype(jnp.int32)
    large = jnp.minimum(large, REL_BUCKETS - 1)
    return jnp.where(n < max_exact, n, large)


def _rel_tables():
    a = jnp.arange(SWA_BLOCK)
    b = jnp.arange(2 * SWA_BLOCK)
    dist = SWA_BLOCK + a[:, None] - b[None, :]
    valid = jnp.logical_and(dist >= 0, dist < SWA_BLOCK)
    onehot = jnp.logical_and(_t5_bucket(dist)[..., None] == jnp.arange(REL_BUCKETS), valid[..., None])
    onehot = onehot.astype(F32).reshape(2 * SWA_BLOCK * SWA_BLOCK, REL_BUCKETS)
    negmask = jnp.where(valid, 0.0, NEG).astype(F32).reshape(1, -1)
    return onehot, negmask


def _rope_tables(S):
    pos = jnp.arange(S, dtype=F32)
    inv = ROPE_THETA ** (-jnp.arange(0, MLA_ROPE, 2, dtype=F32) / MLA_ROPE)
    ang = pos[:, None] * inv[None, :]
    ang = jnp.concatenate([ang, ang, ang, ang], axis=-1)
    return jnp.cos(ang), jnp.sin(ang)


def _flat_pad(parts, rows=8):
    flat = jnp.concatenate([p.reshape(1, -1) for p in parts], axis=1)
    n = flat.shape[1]
    width = -(-n // (rows * LANES)) * LANES
    return jnp.pad(flat, ((0, 0), (0, rows * width - n))).reshape(rows, width)


def _unflat(vec, shapes):
    flat = vec.reshape(-1)
    out, off = [], 0
    for s in shapes:
        n = 1
        for d in s:
            n *= d
        out.append(flat[off:off + n].reshape(s))
        off += n
    return out


def kernel(x, c, w_ada, b_ada, g_pre_mix, g_post_mix, w_in, g_q_lat, w_uq, g_kv_lat, w_ukv, rel_bias, sinks, w_o, g_pre_ffn, g_post_ffn, w_up, conv_w, conv_b, w_down, loss_target, m_w_ada, m_b_ada, m_g_pre_mix, m_g_post_mix, m_w_in, m_g_q_lat, m_w_uq, m_g_kv_lat, m_w_ukv, m_rel_bias, m_sinks, m_w_o, m_g_pre_ffn, m_g_post_ffn, m_w_up, m_conv_w, m_conv_b, m_w_down, v_w_ada, v_b_ada, v_g_pre_mix, v_g_post_mix, v_w_in, v_g_q_lat, v_w_uq, v_g_kv_lat, v_w_ukv, v_rel_bias, v_sinks, v_w_o, v_g_pre_ffn, v_g_post_ffn, v_w_up, v_conv_w, v_conv_b, v_w_down):
    S, D = x.shape[1], x.shape[2]
    Rq, Rkv = g_q_lat.shape[1], g_kv_lat.shape[1]
    H = D // MLA_V
    NH = D // SWA_HD
    KW = SWA_KVH * SWA_HD
    F = w_down.shape[1] * N_CHIP
    xi, yi, ci = lax.axis_index("x"), lax.axis_index("y"), lax.axis_index("c")
    chip = 2 * xi + yi
    me = 2 * chip + ci
    x2, tgt = x[0], loss_target[0]

    c_all = _allgather8(jnp.broadcast_to(c, (8, D)), name="gather_c")[:, 0, :]
    n3 = w_ada.shape[2]
    mod_part = _ada_fwd(c_all, w_ada[0], lax.dynamic_slice(b_ada, (0, chip * n3), (1, n3)), name="ada_fwd")
    mod_all = _allgather8(mod_part, name="gather_mod")
    mod_me = lax.dynamic_index_in_dim(mod_all[0::2], me, axis=1, keepdims=False).reshape(1, 6 * D)
    sh1, sc1, gt1, sh2, sc2, gt2 = [mod_me[:, k * D:(k + 1) * D] for k in range(6)]

    swaps = [False, False, False, False, True, False]
    local = [w_in[0].astype(BF16)]
    send_a, recv_a, srcs_a, lands_a, token = _gather_start(local, (mod_all,), swaps[:1], name="gather_start_in")
    rest, token = lax.optimization_barrier(((w_uq[0], w_ukv[0], w_o[0], w_up[0], w_down[0]), token))
    local += [w.astype(BF16) for w in rest]
    send_b, recv_b, srcs_b, lands_b, token = _gather_start(local[1:], (token,), swaps[1:], name="gather_start_rest")
    send1, recv1, srcs, lands = send_a + send_b, recv_a + recv_b, srcs_a + srcs_b, lands_a + lands_b
    onehot, negmask = _rel_tables()
    npb = NH // (2 * SWA_KVH)
    rb_st = jnp.transpose(rel_bias.T.reshape(SWA_KVH, npb, 2, REL_BUCKETS), (0, 2, 1, 3)).reshape(NH, REL_BUCKETS)
    bias_m = (_matmul(rb_st, onehot.T, tie=token, name="rel_bias_table") + negmask).reshape(
        2 * SWA_KVH, npb * SWA_BLOCK, 2 * SWA_BLOCK)
    h = _modnorm_fwd(x2, g_pre_mix, sc1, sh1, name="pre_mix_norm")

    def whole(land, i):
        return lax.dynamic_update_index_in_dim(land, local[i], _slot(chip, swaps[i]), 0)

    def conv_slots(v):
        return jnp.stack([v[0], v[2], v[1], v[3]])

    s2, r2, l_in, _ = _gather_forward(send1[:1], recv1[:1], srcs[:1], lands[:1], h, swaps[:1], name="gather_forward_in")
    (l_in,) = _gather_finish(s2, r2, l_in, h, name="gather_finish_in")
    gin = whole(l_in, 0)
    o_kr = Rq + Rkv
    o_q = o_kr + MLA_ROPE
    o_g = o_q + NH * SWA_HD + 2 * KW
    n_gate, n_swa = 2 * D, o_g - o_q
    n_lat = -(-(o_q + MLA_ROPE) // PAIR) * PAIR
    runs = []
    for tl in range(D // PAIR):
        runs.append((o_g + tl * PAIR, o_g + (tl + 1) * PAIR, 2 * tl * PAIR))
        runs.append((o_g + D + tl * PAIR, o_g + D + (tl + 1) * PAIR, (2 * tl + 1) * PAIR))
    runs.append((o_q, o_g, n_gate))
    runs.append((0, o_q, n_gate + n_swa))
    csh = gin.shape[2]
    parts = []
    for lo, hi, _ in runs + [(o_kr, o_q, 0)]:
        parts += _shard_cols([gin[j] for j in range(N_CHIP)], lo, hi, csh)
    parts.append(jnp.zeros((D, n_lat - o_q - MLA_ROPE), BF16))
    w_in_all = jnp.concatenate(parts, axis=1)
    cos, sin = _rope_tables(S)
    sink_v = sinks.reshape(NH)

    z_lat = _matmul(h, w_in_all, bcols=(n_gate + n_swa, n_lat), name="in_proj_lat")
    z_swa = _matmul(h, w_in_all, bcols=(n_gate, n_swa), name="in_proj_swa")
    zg = _matmul(h, w_in_all, bcols=(0, n_gate), out_dtype=BF16, name="in_proj_gate")
    s2b, r2b, l_b, _ = _gather_forward(send1[1:4], recv1[1:4], srcs[1:4], lands[1:4], zg, swaps[1:4],
                                       name="gather_forward_attn")
    nq, nkv = _lat_norm_fwd(z_lat, g_q_lat, g_kv_lat, name="lat_norm")
    l_uq, l_ukv, l_o = _gather_finish(s2b, r2b, l_b, nq, name="gather_finish_attn")
    wuq = _split_heads(_from_col_shards(whole(l_uq, 1)), (MLA_NOPE, MLA_ROPE))
    wukv = _split_heads(_from_col_shards(whole(l_ukv, 2)), (MLA_NOPE, MLA_V))
    wo = whole(l_o, 3).reshape(D, D)
    q_raw = _matmul(nq, wuq, out_dtype=BF16, name="uq_proj")
    kv_raw = _matmul(nkv, wukv, out_dtype=BF16, name="ukv_proj")
    qp, kp, vv = _mla_pack_fwd(q_raw, kv_raw, z_lat, cos, sin, o_kr, name="mla_pack")
    o_a, lse = _flash_fwd(qp, kp, vv, name="mla_attn")
    s2c, r2c, l_c, tok_c = _gather_forward(send1[4:], recv1[4:], srcs[4:], lands[4:], o_a, swaps[4:],
                                           name="gather_forward_ffn")
    o_b = _swa_fwd(z_swa, bias_m, sink_v, name="swa_attn")
    mixin = _gate_fwd(zg, o_a, o_b, name="gate_mix")
    mix = _matmul(mixin, wo, tie=tok_c, name="o_proj")
    x1 = _resnorm_fwd(x2, mix, g_post_mix, gt1, name="post_mix_norm")
    h2 = _modnorm_fwd(x1, g_pre_ffn, sc2, sh2, name="pre_ffn_norm")
    l_up, l_down = _gather_finish(s2c, r2c, l_c, h2, name="gather_finish_ffn")
    cw_all = _allgather8(jnp.pad(conv_w[0], ((0, 5), (0, 0))), tie=l_down, name="gather_conv_w")[0::2, :3]
    cw = conv_slots(cw_all)
    cb = conv_slots(conv_b.reshape(N_CHIP, 1, -1))
    wup = whole(l_up, 4)
    wdown = whole(l_down, 5).reshape(F, D)
    t = _matmul(h2, wup, out_dtype=BF16, shards="out", name="up_proj")
    a = _conv_gate_fwd(t, cw, cb, name="conv_gate")
    yv = _matmul(a, wdown, name="down_proj")
    dout, dy, dg_post_ffn, dgt2, loss_tile = _resnorm_loss(x1, yv, g_post_ffn, gt2, tgt, name="post_ffn_norm_loss")

    big_params = dict(w_in=(w_in, m_w_in, v_w_in), w_uq=(w_uq, m_w_uq, v_w_uq), w_ukv=(w_ukv, m_w_ukv, v_w_ukv),
                      w_o=(w_o, m_w_o, v_w_o), w_up=(w_up, m_w_up, v_w_up), w_down=(w_down, m_w_down, v_w_down))
    res = {}

    def start(nms, gs):
        sw = [nm == "w_up" for nm in nms]
        send, recv, gsrc, glands, tok = _scatter_start(gs, sw, name="grads_start_" + nms[0])
        return (nms, send, recv, gsrc, glands), tok

    def finish(pending, after):
        nms, send, recv, gsrc, glands = pending
        gsrc, glands = _scatter_wait(send, recv, gsrc, glands, after, name="grads_wait_" + nms[0])
        halves = [_sum_pieces(l, g, jnp.stack([me, _slot(chip, nm == "w_up"), ci]).astype(jnp.int32), name="grad_sum_" + nm)
                  for l, g, nm in zip(glands, gsrc, nms)]
        for nm, g in zip(nms, _share_halves(halves, name="grads_share_" + nms[0])):
            w, m, v = big_params[nm]
            res[nm] = (g,) + tuple(_adamw(w[0], g, m[0], v[0], name="adamw_" + nm))

    dw_down = _matmul(a, dy, ta=True, out_dtype=BF16, name="down_proj_dw")
    p_down, tok = start(["w_down"], [dw_down.reshape(N_CHIP, F // N_CHIP, D)])
    da = _matmul(dy, wdown, tb=True, out_dtype=BF16, tie=tok, name="down_proj_dx")
    du, dcw_s, dcb_s = _conv_gate_bwd(t, da, cw, cb, name="conv_gate_bwd")
    dt = _conv_bwd_dt(du, cw, name="conv_bwd_dt")
    dw_up = _matmul(h2, dt, ta=True, out_dtype=BF16, shards="out", name="up_proj_dw")
    p_up, tok = start(["w_up"], [dw_up])
    dh2 = _matmul(dt, wup, tb=True, tie=tok, shards="k", name="up_proj_dx")
    dx1, dg_pre_ffn, dsc2, dsh2 = _modnorm_bwd(dh2, x1, g_pre_ffn, sc2, dout, name="pre_ffn_norm_bwd")
    dmix, dg_post_mix, dgt1 = _resnorm_bwd(dx1, mix, g_post_mix, gt1, name="post_mix_norm_bwd")
    dw_o = _matmul(mixin, dmix, ta=True, out_dtype=BF16, name="o_proj_dw")
    p_o, tok = start(["w_o"], [dw_o.reshape(N_CHIP, D // N_CHIP, D)])
    dmixin = _matmul(dmix, wo, tb=True, tie=tok, name="o_proj_dx")
    do_a, do_b, dzg = _gate_bwd(dmixin, zg, o_a, o_b, name="gate_mix_bwd")
    dqp, dkp, dvv = _flash_bwd(qp, kp, vv, o_a, do_a, lse, name="mla_attn_bwd")
    dq_raw, dkv_raw, dkr = _mla_pack_bwd(dqp, dkp, dvv, cos, sin, name="mla_pack_bwd")
    dw_uq_p = _matmul(nq, dq_raw, ta=True, out_dtype=BF16, name="uq_proj_dw")
    dw_ukv_p = _matmul(nkv, dkv_raw, ta=True, out_dtype=BF16, name="ukv_proj_dw")
    p_qkv, tok = start(["w_uq", "w_ukv"], [_to_col_shards(_merge_heads(dw_uq_p, (MLA_NOPE, MLA_ROPE))),
                                           _to_col_shards(_merge_heads(dw_ukv_p, (MLA_NOPE, MLA_V)))])
    dnq = _matmul(dq_raw, wuq, tb=True, tie=tok, name="uq_proj_dx")
    dnkv = _matmul(dkv_raw, wukv, tb=True, name="ukv_proj_dx")
    dz_lat, dg_q, dg_kv = _lat_norm_bwd(z_lat, dnq, dnkv, dkr, g_q_lat, g_kv_lat, name="lat_norm_bwd")
    dz_swa, dbias, dsink = _swa_bwd(z_swa, bias_m, sink_v, o_b, do_b, name="swa_attn_bwd")
    dz = jnp.concatenate([dzg, dz_swa, dz_lat], axis=1)
    dw_in_p = _matmul(h, dz, ta=True, out_dtype=BF16, name="in_proj_dw")
    dw_shards = []
    for j in range(N_CHIP):
        cols = []
        for lo, hi, at in sorted(runs):
            a0, a1 = max(lo, j * csh), min(hi, (j + 1) * csh)
            if a0 < a1:
                cols.append(dw_in_p[:, at + a0 - lo:at + a1 - lo])
        dw_shards.append(jnp.concatenate(cols, axis=1))
    p_in, tok = start(["w_in"], [jnp.stack(dw_shards)])
    dh = _matmul(dz, w_in_all, tb=True, tie=tok, name="in_proj_dx")
    grad_x, dg_pre_mix, dsc1, dsh1 = _modnorm_bwd(dh, x2, g_pre_mix, sc1, dx1, name="pre_mix_norm_bwd")
    drel_st = _matmul(dbias.reshape(NH, -1), onehot, tie=grad_x, name="rel_bias_bwd")
    for pending in (p_down, p_up, p_o, p_qkv):
        finish(pending, drel_st)
    drel = jnp.transpose(drel_st.reshape(SWA_KVH, 2, npb, REL_BUCKETS), (0, 2, 1, 3)).reshape(NH, REL_BUCKETS).T

    dcw = _from_col_shards(conv_slots(dcw_s))
    dcb = conv_slots(dcb_s).reshape(1, -1)
    dmod = jnp.concatenate([dsh1, dsc1, dgt1, dsh2, dsc2, dgt2], axis=1)
    small = [dmod, dg_pre_mix, dg_post_mix, dg_pre_ffn, dg_post_ffn, dg_q, dg_kv, drel, dsink[:, :NH], dcb, dcw]
    shapes = [p.shape for p in small]
    done = [res[nm][1] for nm in ("w_down", "w_up", "w_o", "w_uq", "w_ukv")]
    small_all = _allgather8(_flat_pad(small), tie=done, name="gather_small_grads")
    tot = _unflat(_sum_devices(small_all, name="sum_small_grads"), shapes)
    g_b_ada, g_pre_mix_g, g_post_mix_g, g_pre_ffn_g, g_post_ffn_g, g_q_g, g_kv_g, g_rel, g_sinks, g_cb, g_cw_full = tot
    dmod_all = small_all.reshape(N_DEV, -1)[:, :6 * D]
    g_w_ada = _ada_bwd(c_all.T, lax.dynamic_slice(dmod_all, (0, chip * n3), (N_DEV, n3)), name="ada_bwd")
    ncw = conv_w.shape[2]
    g_cw = lax.dynamic_slice(g_cw_full, (0, chip * ncw), (3, ncw))

    res["w_ada"] = (g_w_ada,) + tuple(_adamw(w_ada[0], g_w_ada, m_w_ada[0], v_w_ada[0], name="adamw_w_ada"))
    finish(p_in, g_w_ada)
    snames = ["b_ada", "g_pre_mix", "g_post_mix", "g_pre_ffn", "g_post_ffn", "g_q_lat", "g_kv_lat", "rel_bias", "sinks",
              "conv_b", "conv_w"]
    sw = [b_ada, g_pre_mix, g_post_mix, g_pre_ffn, g_post_ffn, g_q_lat, g_kv_lat, rel_bias, sinks, conv_b, conv_w]
    sm = [m_b_ada, m_g_pre_mix, m_g_post_mix, m_g_pre_ffn, m_g_post_ffn, m_g_q_lat, m_g_kv_lat, m_rel_bias, m_sinks,
          m_conv_b, m_conv_w]
    sv = [v_b_ada, v_g_pre_mix, v_g_post_mix, v_g_pre_ffn, v_g_post_ffn, v_g_q_lat, v_g_kv_lat, v_rel_bias, v_sinks,
          v_conv_b, v_conv_w]
    sg = [g_b_ada, g_pre_mix_g, g_post_mix_g, g_pre_ffn_g, g_post_ffn_g, g_q_g, g_kv_g, g_rel, g_sinks, g_cb, g_cw]
    sshapes = [w.shape for w in sw]
    sd, snm, snv = _adamw(_flat_pad(sw), _flat_pad(sg), _flat_pad(sm), _flat_pad(sv), name="adamw_small")
    sd, snm, snv = _unflat(sd, sshapes), _unflat(snm, sshapes), _unflat(snv, sshapes)
    for k, nm in enumerate(snames):
        res[nm] = (sg[k].reshape(sshapes[k]), sd[k], snm[k], snv[k])

    order = ["w_ada", "b_ada", "g_pre_mix", "g_post_mix", "w_in", "g_q_lat", "w_uq", "g_kv_lat", "w_ukv", "rel_bias", "sinks",
             "w_o", "g_pre_ffn", "g_post_ffn", "w_up", "conv_w", "conv_b", "w_down"]
    ref_shapes = dict(w_ada=w_ada.shape, w_in=w_in.shape, w_uq=w_uq.shape, w_ukv=w_ukv.shape, w_o=w_o.shape,
                      w_up=w_up.shape, w_down=w_down.shape)
    outs = []
    for k in range(4):
        for nm in order:
            arr = res[nm][k]
            outs.append(arr.reshape(ref_shapes[nm]) if nm in ref_shapes else arr)
    loss = lax.psum(loss_tile[0, 0], ("x", "y", "c"))
    return (loss, grad_x[None], *outs)
```

```python
import math

import jax
import jax.numpy as jnp
from jax import lax
from jax.experimental import pallas as pl
from jax.experimental.pallas import tpu as pltpu

F32 = jnp.float32
BF16 = jnp.bfloat16
MESH = pl.DeviceIdType.MESH
HIGHEST = lax.Precision.HIGHEST

N_DEV = 8
N_CHIP = 4
LANES = 128
MLA_NOPE = 128
MLA_ROPE = 64
MLA_V = 128
MLA_QK = MLA_NOPE + MLA_ROPE
MLA_QK_PAD = 256
ROPE_THETA = 10000.0
SWA_HD = 64
SWA_KVH = 4
SWA_BLOCK = 128
REL_BUCKETS = 32
REL_MAX_DIST = 128
PAIR = 512
EPS = 1e-6
NEG = -1e30
ADAM_LR = 0.001
ADAM_B1 = 0.9
ADAM_B2 = 0.999
ADAM_EPS = 1e-08
ADAM_WD = 0.01
ADAM_STEP = 10

ANY = pl.BlockSpec(memory_space=pl.ANY)
VMEM_FULL = pl.BlockSpec(memory_space=pltpu.VMEM)
SMEM_FULL = pl.BlockSpec(memory_space=pltpu.SMEM)


def _params(*sem):
    return pltpu.CompilerParams(dimension_semantics=sem if sem else None)


def _tied(body, tie):
    if tie is None:
        return body, [], []
    ties = list(tie) if isinstance(tie, (list, tuple)) else [tie]

    def tied_body(*refs):
        body(*refs[len(ties):])

    return tied_body, [ANY] * len(ties), ties


def _tile(n, pref, unit=LANES):
    best = None
    for t in range(unit, min(n, pref) + 1, unit):
        if n % t == 0:
            best = t
    return n if best is None else best


def _matmul(a, b, *, ta=False, tb=False, out_dtype=F32, tie=None, shards=None, bcols=None, name):
    a2 = a.shape[1:] if shards == "k" else a.shape
    b2 = b.shape[1:] if shards else b.shape
    nsh = b.shape[0] if shards else 1
    K, M = a2 if ta else a2[::-1]
    N, K2 = b2 if tb else b2[::-1]
    assert K == K2, (a.shape, b.shape, ta, tb)
    exact = a.dtype == F32
    col0 = 0
    if bcols is not None:
        assert not tb and shards is None
        col0, N = bcols
    tn = _tile(math.gcd(N, col0) if col0 else N, 2048)
    col0 //= tn
    tk = _tile(K, 2048)
    nkc = K // tk
    nk = nkc * (nsh if shards == "k" else 1)
    tm = M if M < 8 else _tile(M, 1024, LANES if ta else 8)
    dn = (((0 if ta else 1,), (1 if tb else 0,)), ((), ()))
    kax = 3 if shards == "out" else 2

    def product(a_ref, b_ref):
        return lax.dot_general(a_ref[...], b_ref[...], dn, preferred_element_type=F32,
                               precision=HIGHEST if exact else None)

    def body_acc(a_ref, b_ref, o_ref, acc_ref):
        k = pl.program_id(kax)

        @pl.when(k == 0)
        def _():
            acc_ref[...] = product(a_ref, b_ref)

        @pl.when(jnp.logical_and(k > 0, k < nk - 1))
        def _():
            acc_ref[...] += product(a_ref, b_ref)

        @pl.when(k == nk - 1)
        def _():
            o_ref[...] = (acc_ref[...] + product(a_ref, b_ref)).astype(o_ref.dtype)

    def body_one(a_ref, b_ref, o_ref):
        o_ref[...] = product(a_ref, b_ref).astype(o_ref.dtype)

    a_blk, b_blk = ((tk, tm) if ta else (tm, tk)), ((tn, tk) if tb else (tk, tn))
    a_at = (lambda i, k: (k, i)) if ta else (lambda i, k: (i, k))
    b_at = (lambda j, k: (j, k)) if tb else (lambda j, k: (k, j + col0))
    if shards == "out":
        grid = (nsh, M // tm, N // tn, nk)
        a_spec = pl.BlockSpec(a_blk, lambda s, i, j, k: a_at(i, k))
        b_spec = pl.BlockSpec((None,) + b_blk, lambda s, i, j, k: (s,) + b_at(j, k))
        o_spec = pl.BlockSpec((None, tm, tn), lambda s, i, j, k: (s, i, j))
        out_shape = jax.ShapeDtypeStruct((nsh, M, N), out_dtype)
        sem = ("parallel", "parallel", "parallel", "arbitrary")
    elif shards == "k":
        grid = (M // tm, N // tn, nk)
        a_spec = pl.BlockSpec((None,) + a_blk, lambda i, j, k: (k // nkc,) + a_at(i, k % nkc))
        b_spec = pl.BlockSpec((None,) + b_blk, lambda i, j, k: (k // nkc,) + b_at(j, k % nkc))
        o_spec = pl.BlockSpec((tm, tn), lambda i, j, k: (i, j))
        out_shape = jax.ShapeDtypeStruct((M, N), out_dtype)
        sem = ("parallel", "parallel", "arbitrary")
    else:
        grid = (M // tm, N // tn, nk)
        a_spec = pl.BlockSpec(a_blk, lambda i, j, k: a_at(i, k))
        b_spec = pl.BlockSpec(b_blk, lambda i, j, k: b_at(j, k))
        o_spec = pl.BlockSpec((tm, tn), lambda i, j, k: (i, j))
        out_shape = jax.ShapeDtypeStruct((M, N), out_dtype)
        sem = ("parallel", "parallel", "arbitrary")
    body, tspec, targ = _tied(body_one if nk == 1 else body_acc, tie)
    return pl.pallas_call(
        body, name=name, out_shape=out_shape, grid=grid, in_specs=tspec + [a_spec, b_spec], out_specs=o_spec,
        scratch_shapes=[] if nk == 1 else [pltpu.VMEM((tm, tn), F32)],
        compiler_params=_params(*sem),
    )(*targ, a, b)


def _row_tile(S, width):
    return _tile(S, max(8, (1 << 19) // width), 8)


def _rstd(x):
    return lax.rsqrt(jnp.mean(x * x, axis=-1, keepdims=True) + EPS)


def _acc_rows(ref, val, first):
    s = jnp.sum(val, axis=0, keepdims=True)

    @pl.when(first)
    def _():
        ref[...] = s

    @pl.when(jnp.logical_not(first))
    def _():
        ref[...] += s


def _modnorm_fwd(x, g, sc, sh, *, name):
    S, D = x.shape
    tr = _row_tile(S, D)

    def body(x_ref, g_ref, sc_ref, sh_ref, h_ref):
        xv = x_ref[...]
        n = (xv * _rstd(xv)) * g_ref[...]
        h_ref[...] = (n * (1.0 + sc_ref[...]) + sh_ref[...]).astype(BF16)

    row = pl.BlockSpec((tr, D), lambda i: (i, 0))
    vec = pl.BlockSpec((1, D), lambda i: (0, 0))
    return pl.pallas_call(
        body, name=name, out_shape=jax.ShapeDtypeStruct((S, D), BF16), grid=(S // tr,),
        in_specs=[row, vec, vec, vec], out_specs=row, compiler_params=_params("parallel"),
    )(x, g, sc, sh)


def _modnorm_bwd(dh, x, g, sc, dres, *, name):
    S, D = x.shape
    tr = _row_tile(S, D)

    def body(dh_ref, x_ref, g_ref, sc_ref, dres_ref, dx_ref, dg_ref, dsc_ref, dsh_ref):
        first = pl.program_id(0) == 0
        xv = x_ref[...]
        dhv = dh_ref[...]
        gv = g_ref[...]
        r = _rstd(xv)
        xhat = xv * r
        _acc_rows(dsh_ref, dhv, first)
        _acc_rows(dsc_ref, dhv * (xhat * gv), first)
        dn = dhv * (1.0 + sc_ref[...])
        _acc_rows(dg_ref, dn * xhat, first)
        dxhat = dn * gv
        proj = jnp.mean(dxhat * xhat, axis=-1, keepdims=True)
        dx_ref[...] = r * (dxhat - xhat * proj) + dres_ref[...]

    row = pl.BlockSpec((tr, D), lambda i: (i, 0))
    vec = pl.BlockSpec((1, D), lambda i: (0, 0))
    vshape = jax.ShapeDtypeStruct((1, D), F32)
    return pl.pallas_call(
        body, name=name,
        out_shape=(jax.ShapeDtypeStruct((S, D), F32), vshape, vshape, vshape), grid=(S // tr,),
        in_specs=[row, row, vec, vec, row], out_specs=(row, vec, vec, vec),
        compiler_params=_params("arbitrary"),
    )(dh, x, g, sc, dres)


def _resnorm_fwd(xres, m, g, gt, *, name):
    S, D = xres.shape
    tr = _row_tile(S, D)

    def body(x_ref, m_ref, g_ref, gt_ref, o_ref):
        mv = m_ref[...]
        o_ref[...] = x_ref[...] + gt_ref[...] * ((mv * _rstd(mv)) * g_ref[...])

    row = pl.BlockSpec((tr, D), lambda i: (i, 0))
    vec = pl.BlockSpec((1, D), lambda i: (0, 0))
    return pl.pallas_call(
        body, name=name, out_shape=jax.ShapeDtypeStruct((S, D), F32), grid=(S // tr,),
        in_specs=[row, row, vec, vec], out_specs=row, compiler_params=_params("parallel"),
    )(xres, m, g, gt)


def _resnorm_loss(xres, m, g, gt, target, *, name):
    S, D = xres.shape
    tr = _row_tile(S, D)

    def body(x_ref, m_ref, g_ref, gt_ref, t_ref, d_ref, dm_ref, dg_ref, dgt_ref, loss_ref):
        first = pl.program_id(0) == 0
        mv = m_ref[...]
        gv = g_ref[...]
        r = _rstd(mv)
        mhat = mv * r
        n = mhat * gv
        err = (x_ref[...] + gt_ref[...] * n) - t_ref[...]
        dv = err * (1.0 / D)
        d_ref[...] = dv
        part = 0.5 * jnp.sum(jnp.mean(err * err, axis=-1, keepdims=True), axis=0, keepdims=True)
        part = jnp.broadcast_to(part, loss_ref.shape)

        @pl.when(first)
        def _():
            loss_ref[...] = part

        @pl.when(jnp.logical_not(first))
        def _():
            loss_ref[...] += part

        _acc_rows(dgt_ref, dv * n, first)
        dn = dv * gt_ref[...]
        _acc_rows(dg_ref, dn * mhat, first)
        dmhat = dn * gv
        proj = jnp.mean(dmhat * mhat, axis=-1, keepdims=True)
        dm_ref[...] = (r * (dmhat - mhat * proj)).astype(BF16)

    row = pl.BlockSpec((tr, D), lambda i: (i, 0))
    vec = pl.BlockSpec((1, D), lambda i: (0, 0))
    vshape = jax.ShapeDtypeStruct((1, D), F32)
    return pl.pallas_call(
        body, name=name,
        out_shape=(jax.ShapeDtypeStruct((S, D), F32), jax.ShapeDtypeStruct((S, D), BF16), vshape, vshape,
                   jax.ShapeDtypeStruct((8, LANES), F32)), grid=(S // tr,),
        in_specs=[row, row, vec, vec, row], out_specs=(row, row, vec, vec, pl.BlockSpec((8, LANES), lambda i: (0, 0))),
        compiler_params=_params("arbitrary"),
    )(xres, m, g, gt, target)


def _resnorm_bwd(dout, m, g, gt, *, name):
    S, D = m.shape
    tr = _row_tile(S, D)

    def body(d_ref, m_ref, g_ref, gt_ref, dm_ref, dg_ref, dgt_ref):
        first = pl.program_id(0) == 0
        mv = m_ref[...]
        dv = d_ref[...]
        gv = g_ref[...]
        r = _rstd(mv)
        mhat = mv * r
        _acc_rows(dgt_ref, dv * (mhat * gv), first)
        dn = dv * gt_ref[...]
        _acc_rows(dg_ref, dn * mhat, first)
        dmhat = dn * gv
        proj = jnp.mean(dmhat * mhat, axis=-1, keepdims=True)
        dm_ref[...] = (r * (dmhat - mhat * proj)).astype(BF16)

    row = pl.BlockSpec((tr, D), lambda i: (i, 0))
    vec = pl.BlockSpec((1, D), lambda i: (0, 0))
    vshape = jax.ShapeDtypeStruct((1, D), F32)
    return pl.pallas_call(
        body, name=name, out_shape=(jax.ShapeDtypeStruct((S, D), BF16), vshape, vshape), grid=(S // tr,),
        in_specs=[row, row, vec, vec], out_specs=(row, vec, vec), compiler_params=_params("arbitrary"),
    )(dout, m, g, gt)


def _lat_norm_fwd(z_lat, g_q, g_kv, *, name):
    S, W = z_lat.shape
    Rq, Rkv = g_q.shape[1], g_kv.shape[1]
    tr = _row_tile(S, W)

    def body(z_ref, gq_ref, gkv_ref, nq_ref, nkv_ref):
        cq = z_ref[:, :Rq]
        ckv = z_ref[:, Rq:Rq + Rkv]
        nq_ref[...] = ((cq * _rstd(cq)) * gq_ref[...]).astype(BF16)
        nkv_ref[...] = ((ckv * _rstd(ckv)) * gkv_ref[...]).astype(BF16)

    return pl.pallas_call(
        body, name=name,
        out_shape=(jax.ShapeDtypeStruct((S, Rq), BF16), jax.ShapeDtypeStruct((S, Rkv), BF16)), grid=(S // tr,),
        in_specs=[pl.BlockSpec((tr, W), lambda i: (i, 0)), pl.BlockSpec((1, Rq), lambda i: (0, 0)),
                  pl.BlockSpec((1, Rkv), lambda i: (0, 0))],
        out_specs=(pl.BlockSpec((tr, Rq), lambda i: (i, 0)), pl.BlockSpec((tr, Rkv), lambda i: (i, 0))),
        compiler_params=_params("parallel"),
    )(z_lat, g_q, g_kv)


def _lat_norm_bwd(z_lat, dnq, dnkv, dkr, g_q, g_kv, *, name):
    S, W = z_lat.shape
    Rq, Rkv = g_q.shape[1], g_kv.shape[1]
    tr = _row_tile(S, W)

    def one(c, dn, gv):
        r = _rstd(c)
        chat = c * r
        dchat = dn * gv
        proj = jnp.mean(dchat * chat, axis=-1, keepdims=True)
        return r * (dchat - chat * proj), dn * chat

    def body(z_ref, dnq_ref, dnkv_ref, dkr_ref, gq_ref, gkv_ref, dz_ref, dgq_ref, dgkv_ref):
        first = pl.program_id(0) == 0
        dcq, pq = one(z_ref[:, :Rq], dnq_ref[...], gq_ref[...])
        dckv, pkv = one(z_ref[:, Rq:Rq + Rkv], dnkv_ref[...], gkv_ref[...])
        _acc_rows(dgq_ref, pq, first)
        _acc_rows(dgkv_ref, pkv, first)
        dz_ref[:, :Rq] = dcq.astype(BF16)
        dz_ref[:, Rq:Rq + Rkv] = dckv.astype(BF16)
        dz_ref[:, Rq + Rkv:Rq + Rkv + LANES] = dkr_ref[...].astype(BF16)
        if W > Rq + Rkv + LANES:
            dz_ref[:, Rq + Rkv + LANES:] = jnp.zeros((tr, W - Rq - Rkv - LANES), BF16)

    return pl.pallas_call(
        body, name=name,
        out_shape=(jax.ShapeDtypeStruct((S, W), BF16), jax.ShapeDtypeStruct((1, Rq), F32),
                   jax.ShapeDtypeStruct((1, Rkv), F32)), grid=(S // tr,),
        in_specs=[pl.BlockSpec((tr, W), lambda i: (i, 0)), pl.BlockSpec((tr, Rq), lambda i: (i, 0)),
                  pl.BlockSpec((tr, Rkv), lambda i: (i, 0)), pl.BlockSpec((tr, LANES), lambda i: (i, 0)),
                  pl.BlockSpec((1, Rq), lambda i: (0, 0)), pl.BlockSpec((1, Rkv), lambda i: (0, 0))],
        out_specs=(pl.BlockSpec((tr, W), lambda i: (i, 0)), pl.BlockSpec((1, Rq), lambda i: (0, 0)),
                   pl.BlockSpec((1, Rkv), lambda i: (0, 0))),
        compiler_params=_params("arbitrary"),
    )(z_lat, dnq, dnkv, dkr, g_q, g_kv)


def _rot(x, lo32):
    a = pltpu.roll(x, 32, 1)
    b = pltpu.roll(x, LANES - 32, 1)
    return jnp.where(lo32, -b, a)


def _rot_t(g, lo32):
    a = pltpu.roll(g, 32, 1)
    b = pltpu.roll(g, LANES - 32, 1)
    return jnp.where(lo32, b, -a)


def _mla_pack_fwd(q_raw, kv_raw, z_lat, cos, sin, kr_off, *, name):
    S = q_raw.shape[0]
    H = kv_raw.shape[1] // (MLA_NOPE + MLA_V)
    W = z_lat.shape[1]
    scale = MLA_QK ** -0.5
    tr = min(S, 128)
    nope_w = H * MLA_NOPE

    def body(q_ref, kv_ref, z_ref, cos_ref, sin_ref, qp_ref, kp_ref, v_ref):
        lane = lax.broadcasted_iota(jnp.int32, (tr, LANES), 1)
        lo32 = (lane % 64) < 32
        lo64 = lane < 64
        c = cos_ref[...]
        s = sin_ref[...]
        kr = z_ref[:, kr_off:kr_off + LANES]
        kr = (kr * c + _rot(kr, lo32) * s).astype(BF16)
        for hp in range(H // 2):
            xb = q_ref[:, nope_w + hp * LANES:nope_w + (hp + 1) * LANES].astype(F32)
            rb = (xb * c + _rot(xb, lo32) * s) * scale
            for e in range(2):
                h = 2 * hp + e
                base = h * MLA_QK_PAD
                qp_ref[:, base:base + LANES] = (q_ref[:, h * LANES:(h + 1) * LANES].astype(F32) * scale).astype(BF16)
                keep = lo64 if e == 0 else jnp.logical_not(lo64)
                qp_ref[:, base + LANES:base + 2 * LANES] = jnp.where(keep, rb, 0.0).astype(BF16)
                kp_ref[:, base:base + LANES] = kv_ref[:, h * LANES:(h + 1) * LANES].astype(BF16)
                kp_ref[:, base + LANES:base + 2 * LANES] = kr
        v_ref[...] = kv_ref[:, nope_w:].astype(BF16)

    return pl.pallas_call(
        body, name=name,
        out_shape=(jax.ShapeDtypeStruct((S, H * MLA_QK_PAD), BF16), jax.ShapeDtypeStruct((S, H * MLA_QK_PAD), BF16),
                   jax.ShapeDtypeStruct((S, H * MLA_V), BF16)), grid=(S // tr,),
        in_specs=[pl.BlockSpec((tr, q_raw.shape[1]), lambda i: (i, 0)), pl.BlockSpec((tr, kv_raw.shape[1]), lambda i: (i, 0)),
                  pl.BlockSpec((tr, W), lambda i: (i, 0)), pl.BlockSpec((tr, LANES), lambda i: (i, 0)),
                  pl.BlockSpec((tr, LANES), lambda i: (i, 0))],
        out_specs=(pl.BlockSpec((tr, H * MLA_QK_PAD), lambda i: (i, 0)), pl.BlockSpec((tr, H * MLA_QK_PAD), lambda i: (i, 0)),
                   pl.BlockSpec((tr, H * MLA_V), lambda i: (i, 0))),
        compiler_params=_params("parallel"),
    )(q_raw, kv_raw, z_lat, cos, sin)


def _mla_pack_bwd(dqp, dkp, dv, cos, sin, *, name):
    S = dqp.shape[0]
    H = dv.shape[1] // MLA_V
    scale = MLA_QK ** -0.5
    tr = min(S, 128)
    nope_w = H * MLA_NOPE

    def body(dqp_ref, dkp_ref, dv_ref, cos_ref, sin_ref, dq_ref, dkv_ref, dkr_ref):
        lane = lax.broadcasted_iota(jnp.int32, (tr, LANES), 1)
        lo32 = (lane % 64) < 32
        lo64 = lane < 64
        c = cos_ref[...]
        s = sin_ref[...]
        dkr2 = jnp.zeros((tr, LANES), F32)
        for hp in range(H // 2):
            be = (2 * hp) * MLA_QK_PAD
            bo = (2 * hp + 1) * MLA_QK_PAD
            g = jnp.where(lo64, dqp_ref[:, be + LANES:be + 2 * LANES].astype(F32),
                          dqp_ref[:, bo + LANES:bo + 2 * LANES].astype(F32)) * scale
            dq_ref[:, nope_w + hp * LANES:nope_w + (hp + 1) * LANES] = (g * c + _rot_t(g * s, lo32)).astype(BF16)
            for h, base in ((2 * hp, be), (2 * hp + 1, bo)):
                dq_ref[:, h * LANES:(h + 1) * LANES] = (dqp_ref[:, base:base + LANES].astype(F32) * scale).astype(BF16)
                dkv_ref[:, h * LANES:(h + 1) * LANES] = dkp_ref[:, base:base + LANES].astype(BF16)
                dkr2 = dkr2 + dkp_ref[:, base + LANES:base + 2 * LANES].astype(F32)
        dkr2 = dkr2 * c + _rot_t(dkr2 * s, lo32)
        dkr2 = dkr2 + pltpu.roll(dkr2, 64, 1)
        dkr_ref[...] = jnp.where(lo64, dkr2, 0.0)
        dkv_ref[:, nope_w:] = dv_ref[...].astype(BF16)

    return pl.pallas_call(
        body, name=name,
        out_shape=(jax.ShapeDtypeStruct((S, nope_w + H * MLA_ROPE), BF16), jax.ShapeDtypeStruct((S, 2 * nope_w), BF16),
                   jax.ShapeDtypeStruct((S, LANES), F32)), grid=(S // tr,),
        in_specs=[pl.BlockSpec((tr, H * MLA_QK_PAD), lambda i: (i, 0)), pl.BlockSpec((tr, H * MLA_QK_PAD), lambda i: (i, 0)),
                  pl.BlockSpec((tr, H * MLA_V), lambda i: (i, 0)), pl.BlockSpec((tr, LANES), lambda i: (i, 0)),
                  pl.BlockSpec((tr, LANES), lambda i: (i, 0))],
        out_specs=(pl.BlockSpec((tr, nope_w + H * MLA_ROPE), lambda i: (i, 0)), pl.BlockSpec((tr, 2 * nope_w), lambda i: (i, 0)),
                   pl.BlockSpec((tr, LANES), lambda i: (i, 0))),
        compiler_params=_params("parallel"),
    )(dqp, dkp, dv, cos, sin)


FLASH_HB = 2


def _causal_pairs(nb):
    qi = [i for i in range(nb) for j in range(i + 1)]
    kj = [j for i in range(nb) for j in range(i + 1)]
    return jnp.asarray(qi, jnp.int32), jnp.asarray(kj, jnp.int32)


def _scores(q, k, diagonal, t):
    s = lax.dot_general(q, k, (((1,), (1,)), ((), ())), preferred_element_type=F32)
    if diagonal:
        row = lax.broadcasted_iota(jnp.int32, (t, t), 0)
        col = lax.broadcasted_iota(jnp.int32, (t, t), 1)
        s = jnp.where(col <= row, s, NEG)
    return s


def _flash_fwd(qp, kp, v, *, name):
    S = qp.shape[0]
    H = v.shape[1] // MLA_V
    t = min(S, 512)
    nb = S // t
    HB = 2 * FLASH_HB
    qi, kj = _causal_pairs(nb)
    QW, VW = MLA_QK_PAD, MLA_V

    def body(qi_ref, kj_ref, q_ref, k_ref, v_ref, o_ref, lse_ref, m_s, l_s, acc_s):
        pr = pl.program_id(1)
        i = qi_ref[pr]
        j = kj_ref[pr]

        @pl.when(j == 0)
        def _():
            m_s[...] = jnp.full_like(m_s, NEG)
            l_s[...] = jnp.zeros_like(l_s)
            acc_s[...] = jnp.zeros_like(acc_s)

        def step(diagonal):
            state = [(m_s[hh], l_s[hh], acc_s[hh]) for hh in range(HB)]
            new = []
            for hh, (m_prev, l_prev, acc_prev) in enumerate(state):
                s = _scores(q_ref[:, hh * QW:(hh + 1) * QW], k_ref[:, hh * QW:(hh + 1) * QW], diagonal, t)
                m_cur = jnp.maximum(m_prev, jnp.max(s, axis=1, keepdims=True))
                alpha = jnp.exp(m_prev - m_cur)
                p = jnp.exp(s - m_cur[:, :1])
                l_new = alpha * l_prev + jnp.sum(p, axis=1, keepdims=True)
                acc = alpha * acc_prev + jnp.dot(p.astype(BF16), v_ref[:, hh * VW:(hh + 1) * VW], preferred_element_type=F32)
                new.append((m_cur, l_new, acc))
            for hh, (m_cur, l_new, acc) in enumerate(new):
                if diagonal:
                    o_ref[:, hh * VW:(hh + 1) * VW] = acc / l_new
                    lse_ref[hh] = m_cur + jnp.log(l_new)
                else:
                    l_s[hh] = l_new
                    acc_s[hh] = acc
                    m_s[hh] = m_cur

        @pl.when(i != j)
        def _():
            step(False)

        @pl.when(i == j)
        def _():
            step(True)

    return pl.pallas_call(
        body, name=name,
        out_shape=(jax.ShapeDtypeStruct((S, H * VW), F32), jax.ShapeDtypeStruct((H, S, LANES), F32)),
        grid_spec=pltpu.PrefetchScalarGridSpec(
            num_scalar_prefetch=2, grid=(H // HB, qi.shape[0]),
            in_specs=[pl.BlockSpec((t, HB * QW), lambda g, p, qi, kj: (qi[p], g)),
                      pl.BlockSpec((t, HB * QW), lambda g, p, qi, kj: (kj[p], g)),
                      pl.BlockSpec((t, HB * VW), lambda g, p, qi, kj: (kj[p], g))],
            out_specs=(pl.BlockSpec((t, HB * VW), lambda g, p, qi, kj: (qi[p], g)),
                       pl.BlockSpec((HB, t, LANES), lambda g, p, qi, kj: (g, qi[p], 0))),
            scratch_shapes=[pltpu.VMEM((HB, t, LANES), F32), pltpu.VMEM((HB, t, LANES), F32), pltpu.VMEM((HB, t, VW), F32)]),
        compiler_params=_params("parallel", "arbitrary"),
    )(qi, kj, qp, kp, v)


def _flash_bwd(qp, kp, v, o, do, lse, *, name):
    S = qp.shape[0]
    H = v.shape[1] // MLA_V
    t = min(S, 512)
    nb = S // t
    HB = FLASH_HB
    qi, kj = _causal_pairs(nb)
    npairs = qi.shape[0]
    QW, VW = MLA_QK_PAD, MLA_V
    tn = (((0,), (0,)), ((), ()))
    nt = (((1,), (1,)), ((), ()))

    def body(qi_ref, kj_ref, q_ref, k_ref, v_ref, o_ref, do_ref, lse_ref, dq_ref, dk_ref, dv_ref, dq_s, dk_s, dv_s):
        pr = pl.program_id(1)
        i = qi_ref[pr]
        j = kj_ref[pr]
        rows = pl.ds(pl.multiple_of(j * t, t), t)

        @pl.when(pr == 0)
        def _():
            dk_s[...] = jnp.zeros_like(dk_s)
            dv_s[...] = jnp.zeros_like(dv_s)

        @pl.when(j == 0)
        def _():
            dq_s[...] = jnp.zeros_like(dq_s)

        def step(diagonal):
            for hh in range(HB):
                q = q_ref[:, hh * QW:(hh + 1) * QW]
                k = k_ref[:, hh * QW:(hh + 1) * QW]
                dob = do_ref[:, hh * VW:(hh + 1) * VW]
                p = jnp.exp(_scores(q, k, diagonal, t) - lse_ref[hh][:, :1])
                delta = jnp.sum(dob.astype(F32) * o_ref[:, hh * VW:(hh + 1) * VW], axis=1, keepdims=True)
                dp = lax.dot_general(dob, v_ref[:, hh * VW:(hh + 1) * VW], nt, preferred_element_type=F32)
                dsb = (p * (dp - delta)).astype(BF16)
                dv_s[rows, hh * VW:(hh + 1) * VW] += lax.dot_general(p.astype(BF16), dob, tn, preferred_element_type=F32)
                dk_s[rows, hh * QW:(hh + 1) * QW] += lax.dot_general(dsb, q, tn, preferred_element_type=F32)
                dq = dq_s[:, hh * QW:(hh + 1) * QW] + jnp.dot(dsb, k, preferred_element_type=F32)
                if diagonal:
                    dq_ref[:, hh * QW:(hh + 1) * QW] = dq.astype(BF16)
                else:
                    dq_s[:, hh * QW:(hh + 1) * QW] = dq

        @pl.when(i != j)
        def _():
            step(False)

        @pl.when(i == j)
        def _():
            step(True)

        @pl.when(pr == npairs - 1)
        def _():
            dk_ref[...] = dk_s[...].astype(BF16)
            dv_ref[...] = dv_s[...].astype(BF16)

    qside = lambda g, p, qi, kj: (qi[p], g)
    kside = lambda g, p, qi, kj: (kj[p], g)
    whole = lambda g, p, qi, kj: (0, g)
    return pl.pallas_call(
        body, name=name,
        out_shape=(jax.ShapeDtypeStruct((S, H * QW), BF16), jax.ShapeDtypeStruct((S, H * QW), BF16),
                   jax.ShapeDtypeStruct((S, H * VW), BF16)),
        grid_spec=pltpu.PrefetchScalarGridSpec(
            num_scalar_prefetch=2, grid=(H // HB, npairs),
            in_specs=[pl.BlockSpec((t, HB * QW), qside), pl.BlockSpec((t, HB * QW), kside), pl.BlockSpec((t, HB * VW), kside),
                      pl.BlockSpec((t, HB * VW), qside), pl.BlockSpec((t, HB * VW), qside),
                      pl.BlockSpec((HB, t, LANES), lambda g, p, qi, kj: (g, qi[p], 0))],
            out_specs=(pl.BlockSpec((t, HB * QW), qside), pl.BlockSpec((S, HB * QW), whole), pl.BlockSpec((S, HB * VW), whole)),
            scratch_shapes=[pltpu.VMEM((t, HB * QW), F32), pltpu.VMEM((S, HB * QW), F32), pltpu.VMEM((S, HB * VW), F32)]),
        compiler_params=_params("parallel", "arbitrary"),
    )(qi, kj, qp, kp, v, o, do, lse)


def _swa_kv_halves(blk, hf, lo):
    if hf == 0:
        a = jnp.where(lo, blk, 0.0)
        b = pltpu.roll(a, 64, 1)
    else:
        b = jnp.where(lo, 0.0, blk)
        a = pltpu.roll(b, 64, 1)
    return a.astype(BF16), b.astype(BF16)


def _swa_softmax(qs, kx, bias, neg0, sk):
    s = lax.dot_general(qs, kx, (((1,), (1,)), ((), ())), preferred_element_type=F32) + bias + neg0
    m = jnp.maximum(jnp.max(s, axis=1, keepdims=True), sk)
    e = jnp.exp(s - m)
    es = jnp.exp(sk - m)
    inv = 1.0 / (jnp.sum(e, axis=1, keepdims=True) + es)
    return e * inv, es * inv


def _swa_stack(ref, kvh, npb, scale=None):
    parts = [ref[:, (kvh * npb + pb) * LANES:(kvh * npb + pb + 1) * LANES] for pb in range(npb)]
    x = jnp.concatenate(parts, axis=0)
    return x if scale is None else x * scale


def _swa_sink_col(sink_ref, kvh, e, npb):
    row = lax.broadcasted_iota(jnp.int32, (npb * SWA_BLOCK, 1), 0)
    col = jnp.zeros((npb * SWA_BLOCK, 1), F32) + sink_ref[2 * (kvh * npb) + e]
    for pb in range(1, npb):
        col = jnp.where(row >= pb * SWA_BLOCK, sink_ref[2 * (kvh * npb + pb) + e], col)
    return col


def _swa_fwd(z_swa, bias_st, sinks, *, name):
    S, W = z_swa.shape
    npb = bias_st.shape[1] // SWA_BLOCK
    NH = 2 * SWA_KVH * npb
    QW = NH * SWA_HD
    KW = SWA_KVH * SWA_HD
    nb = S // SWA_BLOCK
    B = SWA_BLOCK
    assert SWA_KVH % 2 == 0 and W == QW + 2 * KW

    def body(sink_ref, q_ref, kvc_ref, kvp_ref, b_ref, o_ref):
        n = pl.program_id(0)
        lo = lax.broadcasted_iota(jnp.int32, (2 * B, LANES), 1) < 64
        col = lax.broadcasted_iota(jnp.int32, (npb * B, 2 * B), 1)
        neg0 = jnp.where(jnp.logical_and(col < B, n == 0), NEG, 0.0)
        for kb in range(SWA_KVH // 2):
            kblk = jnp.concatenate([kvp_ref[:, kb * LANES:(kb + 1) * LANES], kvc_ref[:, kb * LANES:(kb + 1) * LANES]], axis=0)
            vblk = jnp.concatenate([kvp_ref[:, KW + kb * LANES:KW + (kb + 1) * LANES],
                                    kvc_ref[:, KW + kb * LANES:KW + (kb + 1) * LANES]], axis=0)
            for hf in range(2):
                kvh = 2 * kb + hf
                ks = _swa_kv_halves(kblk, hf, lo)
                vs = _swa_kv_halves(vblk, hf, lo)
                qs = _swa_stack(q_ref, kvh, npb, SWA_HD ** -0.5).astype(BF16)
                acc = jnp.zeros((npb * B, LANES), F32)
                for e in range(2):
                    p, _ = _swa_softmax(qs, ks[e], b_ref[2 * kvh + e], neg0, _swa_sink_col(sink_ref, kvh, e, npb))
                    acc = acc + jnp.dot(p.astype(BF16), vs[e], preferred_element_type=F32)
                for pb in range(npb):
                    P = kvh * npb + pb
                    o_ref[:, P * LANES:(P + 1) * LANES] = acc[pb * B:(pb + 1) * B]

    kvcol = QW // (2 * KW)
    assert QW % (2 * KW) == 0
    return pl.pallas_call(
        body, name=name,
        out_shape=jax.ShapeDtypeStruct((S, QW), F32), grid=(nb,),
        in_specs=[SMEM_FULL, pl.BlockSpec((B, QW), lambda n: (n, 0)), pl.BlockSpec((B, 2 * KW), lambda n: (n, kvcol)),
                  pl.BlockSpec((B, 2 * KW), lambda n: (jnp.maximum(n - 1, 0), kvcol)),
                  pl.BlockSpec(bias_st.shape, lambda n: (0, 0, 0))],
        out_specs=pl.BlockSpec((B, QW), lambda n: (n, 0)),
        compiler_params=_params("parallel"),
    )(sinks, z_swa, z_swa, z_swa, bias_st)


def _swa_bwd(z_swa, bias_st, sinks, o, do, *, name):
    S, W = z_swa.shape
    npb = bias_st.shape[1] // SWA_BLOCK
    NH = 2 * SWA_KVH * npb
    QW = NH * SWA_HD
    KW = SWA_KVH * SWA_HD
    nb = S // SWA_BLOCK
    B = SWA_BLOCK
    scale = SWA_HD ** -0.5
    tn = (((0,), (0,)), ((), ()))
    nt = (((1,), (1,)), ((), ()))

    def fold(x, hf, lo):
        x = x + pltpu.roll(x, 64, 1)
        return jnp.where(lo, x, 0.0) if hf == 0 else jnp.where(lo, 0.0, x)

    def body(sink_ref, q_ref, kvc_ref, kvp_ref, b_ref, o_ref, do_ref, dz_ref, dbias_ref, dsink_ref,
             cq_s, ck_s, cv_s, nq_s, nk_s, nv_s, pk_s, pv_s):
        n = pl.program_id(0)

        @pl.when(n == 0)
        def _():
            dbias_ref[...] = jnp.zeros_like(dbias_ref)
            dsink_ref[...] = jnp.zeros_like(dsink_ref)
            cq_s[...] = jnp.zeros_like(cq_s)
            ck_s[...] = jnp.zeros_like(ck_s)
            cv_s[...] = jnp.zeros_like(cv_s)

        @pl.when(n == nb)
        def _():
            pk_s[...] = jnp.zeros_like(pk_s)
            pv_s[...] = jnp.zeros_like(pv_s)

        @pl.when(n < nb)
        def _():
            lo = lax.broadcasted_iota(jnp.int32, (2 * B, LANES), 1) < 64
            lo1 = lax.broadcasted_iota(jnp.int32, (npb * B, LANES), 1) < 64
            lane1 = lax.broadcasted_iota(jnp.int32, (1, LANES), 1)
            col = lax.broadcasted_iota(jnp.int32, (npb * B, 2 * B), 1)
            neg0 = jnp.where(jnp.logical_and(col < B, n == 0), NEG, 0.0)
            dsink = jnp.zeros((1, LANES), F32)
            for kb in range(SWA_KVH // 2):
                kblk = jnp.concatenate([kvp_ref[:, kb * LANES:(kb + 1) * LANES], kvc_ref[:, kb * LANES:(kb + 1) * LANES]], axis=0)
                vblk = jnp.concatenate([kvp_ref[:, KW + kb * LANES:KW + (kb + 1) * LANES],
                                        kvc_ref[:, KW + kb * LANES:KW + (kb + 1) * LANES]], axis=0)
                dkblk = jnp.zeros((2 * B, LANES), F32)
                dvblk = jnp.zeros((2 * B, LANES), F32)
                for hf in range(2):
                    kvh = 2 * kb + hf
                    ks = _swa_kv_halves(kblk, hf, lo)
                    vs = _swa_kv_halves(vblk, hf, lo)
                    qs = _swa_stack(q_ref, kvh, npb, scale).astype(BF16)
                    dos = _swa_stack(do_ref, kvh, npb)
                    prod = dos * _swa_stack(o_ref, kvh, npb)
                    dob = dos.astype(BF16)
                    dkj = jnp.zeros((2 * B, LANES), F32)
                    dvj = jnp.zeros((2 * B, LANES), F32)
                    dqs = jnp.zeros((npb * B, LANES), F32)
                    for e in range(2):
                        keep = lo1 if e == 0 else jnp.logical_not(lo1)
                        p, psink = _swa_softmax(qs, ks[e], b_ref[2 * kvh + e], neg0, _swa_sink_col(sink_ref, kvh, e, npb))
                        delta = jnp.sum(jnp.where(keep, prod, 0.0), axis=1, keepdims=True)
                        dp = lax.dot_general(dob, vs[e], nt, preferred_element_type=F32)
                        ds = p * (dp - delta)
                        dbias_ref[2 * kvh + e] += ds
                        pd = psink * delta
                        for pb in range(npb):
                            dsh = -jnp.sum(pd[pb * B:(pb + 1) * B], axis=0, keepdims=True)
                            dsink = dsink + jnp.where(lane1 == 2 * (kvh * npb + pb) + e, dsh, 0.0)
                        dsb = ds.astype(BF16)
                        dqs = dqs + jnp.dot(dsb, ks[e], preferred_element_type=F32)
                        keep2 = lo if e == 0 else jnp.logical_not(lo)
                        dkj = dkj + jnp.where(keep2, lax.dot_general(dsb, qs, tn, preferred_element_type=F32), 0.0)
                        dvj = dvj + jnp.where(keep2, lax.dot_general(p.astype(BF16), dob, tn, preferred_element_type=F32), 0.0)
                    for pb in range(npb):
                        P = kvh * npb + pb
                        nq_s[:, P * LANES:(P + 1) * LANES] = dqs[pb * B:(pb + 1) * B] * scale
                    dkblk = dkblk + fold(dkj, hf, lo)
                    dvblk = dvblk + fold(dvj, hf, lo)
                pk_s[:, kb * LANES:(kb + 1) * LANES] = dkblk[:B]
                nk_s[:, kb * LANES:(kb + 1) * LANES] = dkblk[B:]
                pv_s[:, kb * LANES:(kb + 1) * LANES] = dvblk[:B]
                nv_s[:, kb * LANES:(kb + 1) * LANES] = dvblk[B:]
            dsink_ref[...] += dsink

        dz_ref[:, :QW] = cq_s[...].astype(BF16)
        dz_ref[:, QW:QW + KW] = (ck_s[...] + pk_s[...]).astype(BF16)
        dz_ref[:, QW + KW:] = (cv_s[...] + pv_s[...]).astype(BF16)

        @pl.when(n < nb)
        def _():
            cq_s[...] = nq_s[...]
            ck_s[...] = nk_s[...]
            cv_s[...] = nv_s[...]

    kvcol = QW // (2 * KW)
    cur = lambda n: (jnp.minimum(n, nb - 1), 0)
    return pl.pallas_call(
        body, name=name,
        out_shape=(jax.ShapeDtypeStruct((S, W), BF16), jax.ShapeDtypeStruct(bias_st.shape, F32),
                   jax.ShapeDtypeStruct((1, LANES), F32)),
        grid=(nb + 1,),
        in_specs=[SMEM_FULL, pl.BlockSpec((B, QW), cur), pl.BlockSpec((B, 2 * KW), lambda n: (jnp.minimum(n, nb - 1), kvcol)),
                  pl.BlockSpec((B, 2 * KW), lambda n: (jnp.maximum(jnp.minimum(n, nb - 1) - 1, 0), kvcol)),
                  pl.BlockSpec(bias_st.shape, lambda n: (0, 0, 0)), pl.BlockSpec((B, QW), cur), pl.BlockSpec((B, QW), cur)],
        out_specs=(pl.BlockSpec((B, W), lambda n: (jnp.maximum(n - 1, 0), 0)),
                   pl.BlockSpec(bias_st.shape, lambda n: (0, 0, 0)), pl.BlockSpec((1, LANES), lambda n: (0, 0))),
        scratch_shapes=[pltpu.VMEM((B, QW), F32), pltpu.VMEM((B, KW), F32), pltpu.VMEM((B, KW), F32),
                        pltpu.VMEM((B, QW), F32), pltpu.VMEM((B, KW), F32), pltpu.VMEM((B, KW), F32),
                        pltpu.VMEM((B, KW), F32), pltpu.VMEM((B, KW), F32)],
        compiler_params=_params("arbitrary"),
    )(sinks, z_swa, z_swa, z_swa, bias_st, o, do)


def _gate_fwd(zg, o_a, o_b, *, name):
    S, D = o_a.shape
    tr = min(S, 512)

    def body(z_ref, a_ref, b_ref, m_ref):
        ga = jax.nn.sigmoid(z_ref[:, :PAIR].astype(F32))
        gb = jax.nn.sigmoid(z_ref[:, PAIR:].astype(F32))
        m_ref[...] = (ga * a_ref[...] + gb * b_ref[...]).astype(BF16)

    col = pl.BlockSpec((tr, PAIR), lambda i, j: (i, j))
    return pl.pallas_call(
        body, name=name, out_shape=jax.ShapeDtypeStruct((S, D), BF16), grid=(S // tr, D // PAIR),
        in_specs=[pl.BlockSpec((tr, 2 * PAIR), lambda i, j: (i, j)), col, col], out_specs=col,
        compiler_params=_params("parallel", "parallel"),
    )(zg, o_a, o_b)


def _gate_bwd(dmix, zg, o_a, o_b, *, name):
    S, D = o_a.shape
    tr = min(S, 512)

    def body(d_ref, z_ref, a_ref, b_ref, da_ref, db_ref, dz_ref):
        d = d_ref[...]
        ga = jax.nn.sigmoid(z_ref[:, :PAIR].astype(F32))
        gb = jax.nn.sigmoid(z_ref[:, PAIR:].astype(F32))
        da_ref[...] = (d * ga).astype(BF16)
        db_ref[...] = d * gb
        dz_ref[:, :PAIR] = (d * a_ref[...] * (ga * (1.0 - ga))).astype(BF16)
        dz_ref[:, PAIR:] = (d * b_ref[...] * (gb * (1.0 - gb))).astype(BF16)

    col = pl.BlockSpec((tr, PAIR), lambda i, j: (i, j))
    wide = pl.BlockSpec((tr, 2 * PAIR), lambda i, j: (i, j))
    return pl.pallas_call(
        body, name=name,
        out_shape=(jax.ShapeDtypeStruct((S, D), BF16), jax.ShapeDtypeStruct((S, D), F32), jax.ShapeDtypeStruct((S, 2 * D), BF16)),
        grid=(S // tr, D // PAIR), in_specs=[col, wide, col, col], out_specs=(col, col, wide),
        compiler_params=_params("parallel", "parallel"),
    )(dmix, zg, o_a, o_b)


def _conv_u(t_ref, prev_ref, w_ref, b_ref, m, i):
    cur = t_ref[m].astype(F32)
    live = (i > 0).astype(F32)
    p6 = prev_ref[m, 14:15, :].astype(F32) * live
    p7 = prev_ref[m, 15:16, :].astype(F32) * live
    row = lax.broadcasted_iota(jnp.int32, cur.shape, 0)
    t1 = jnp.where(row == 0, p7, pltpu.roll(cur, 1, 0))
    t2 = jnp.where(row == 0, p6, jnp.where(row == 1, p7, pltpu.roll(cur, 2, 0)))
    u = ((b_ref[m] + w_ref[m, 0:1, :] * t2) + w_ref[m, 1:2, :] * t1) + w_ref[m, 2:3, :] * cur
    return u, cur, t1, t2


def _conv_specs(tr, tc):
    blk = pl.BlockSpec((2, tr, tc), lambda p, j, i: (p, i, j))
    prev = pl.BlockSpec((2, 16, tc), lambda p, j, i: (p, jnp.maximum(i * (tr // 16) - 1, 0), j))
    w3 = pl.BlockSpec((2, 3, tc), lambda p, j, i: (p, 0, j))
    w1 = pl.BlockSpec((2, 1, tc), lambda p, j, i: (p, 0, j))
    return blk, prev, w3, w1


def _conv_gate_fwd(t, cw, cb, *, name):
    _, S, C = t.shape
    tr, tc = min(S, 512), _tile(C, 1536)
    ncol = C // tc
    blk, prev, w3, w1 = _conv_specs(tr, tc)

    def body(t_ref, prev_ref, w_ref, b_ref, a_ref):
        i = pl.program_id(2)
        u1 = _conv_u(t_ref, prev_ref, w_ref, b_ref, 0, i)[0]
        u2 = _conv_u(t_ref, prev_ref, w_ref, b_ref, 1, i)[0]
        a_ref[...] = (jax.nn.silu(u1) * u2).astype(BF16)

    return pl.pallas_call(
        body, name=name, out_shape=jax.ShapeDtypeStruct((S, 2 * C), BF16), grid=(2, ncol, S // tr),
        in_specs=[blk, prev, w3, w1], out_specs=pl.BlockSpec((tr, tc), lambda p, j, i: (i, p * ncol + j)),
        compiler_params=_params("parallel", "parallel", "parallel"),
    )(t, t, cw, cb)


def _conv_gate_bwd(t, da, cw, cb, *, name):
    _, S, C = t.shape
    tr, tc = min(S, 256), _tile(C, 1536)
    ncol = C // tc
    blk, prev, w3, w1 = _conv_specs(tr, tc)

    def body(t_ref, prev_ref, da_ref, w_ref, b_ref, du_ref, dw_ref, db_ref):
        i = pl.program_id(2)
        first = i == 0
        u1, c1, a1, b1 = _conv_u(t_ref, prev_ref, w_ref, b_ref, 0, i)
        u2, c2, a2, b2 = _conv_u(t_ref, prev_ref, w_ref, b_ref, 1, i)
        d = da_ref[...].astype(F32)
        sg = jax.nn.sigmoid(u1)
        du1 = d * u2 * (sg * (1.0 + u1 * (1.0 - sg)))
        du2 = d * (u1 * sg)
        for m, (du, cur, t1, t2) in enumerate(((du1, c1, a1, b1), (du2, c2, a2, b2))):
            du_ref[m] = du.astype(BF16)
            dw = jnp.concatenate([jnp.sum(du * t2, axis=0, keepdims=True), jnp.sum(du * t1, axis=0, keepdims=True),
                                  jnp.sum(du * cur, axis=0, keepdims=True)], axis=0)
            db = jnp.sum(du, axis=0, keepdims=True)

            @pl.when(first)
            def _():
                dw_ref[m] = dw
                db_ref[m] = db

            @pl.when(jnp.logical_not(first))
            def _():
                dw_ref[m] += dw
                db_ref[m] += db

    return pl.pallas_call(
        body, name=name,
        out_shape=(jax.ShapeDtypeStruct(t.shape, BF16), jax.ShapeDtypeStruct(cw.shape, F32), jax.ShapeDtypeStruct(cb.shape, F32)),
        grid=(2, ncol, S // tr),
        in_specs=[blk, prev, pl.BlockSpec((tr, tc), lambda p, j, i: (i, p * ncol + j)), w3, w1], out_specs=(blk, w3, w1),
        compiler_params=_params("parallel", "parallel", "arbitrary"),
    )(t, t, da, cw, cb)


def _conv_bwd_dt(du, cw, *, name):
    _, S, C = du.shape
    tr, tc = min(S, 512), _tile(C, 1536)
    nrow = S // tr
    blk, _, w3, _ = _conv_specs(tr, tc)
    nxt = pl.BlockSpec((2, 16, tc), lambda p, j, i: (p, jnp.minimum((i + 1) * (tr // 16), S // 16 - 1), j))

    def body(d_ref, next_ref, w_ref, dt_ref):
        i = pl.program_id(2)
        live = (i < nrow - 1).astype(F32)
        for m in range(2):
            cur = d_ref[m].astype(F32)
            n0 = next_ref[m, 0:1, :].astype(F32) * live
            n1 = next_ref[m, 1:2, :].astype(F32) * live
            row = lax.broadcasted_iota(jnp.int32, cur.shape, 0)
            d1 = jnp.where(row == tr - 1, n0, pltpu.roll(cur, tr - 1, 0))
            d2 = jnp.where(row == tr - 1, n1, jnp.where(row == tr - 2, n0, pltpu.roll(cur, tr - 2, 0)))
            dt_ref[m] = ((w_ref[m, 2:3, :] * cur + w_ref[m, 1:2, :] * d1) + w_ref[m, 0:1, :] * d2).astype(BF16)

    return pl.pallas_call(
        body, name=name, out_shape=jax.ShapeDtypeStruct(du.shape, BF16), grid=(2, C // tc, nrow),
        in_specs=[blk, nxt, w3], out_specs=blk, compiler_params=_params("parallel", "parallel", "parallel"),
    )(du, du, cw)


def _ada_fwd(c_all, w, b, *, name):
    Bn, D = c_all.shape
    N = w.shape[1]
    tn = _tile(N, 512)

    def body(c_ref, w_ref, b_ref, o_ref):
        o_ref[...] = jnp.dot(jax.nn.silu(c_ref[...]), w_ref[...], preferred_element_type=F32, precision=HIGHEST) + b_ref[...]

    return pl.pallas_call(
        body, name=name, out_shape=jax.ShapeDtypeStruct((Bn, N), F32), grid=(N // tn,),
        in_specs=[pl.BlockSpec((Bn, D), lambda j: (0, 0)), pl.BlockSpec((D, tn), lambda j: (0, j)),
                  pl.BlockSpec((1, tn), lambda j: (0, j))],
        out_specs=pl.BlockSpec((Bn, tn), lambda j: (0, j)), compiler_params=_params("parallel"),
    )(c_all, w, b)


def _ada_bwd(c_all_t, dmod, *, name):
    D, Bn = c_all_t.shape
    N = dmod.shape[1]
    tm = _tile(D, 512, 8)
    tn = _tile(N, 1536)

    def body(c_ref, d_ref, o_ref):
        o_ref[...] = jnp.dot(jax.nn.silu(c_ref[...]), d_ref[...], preferred_element_type=F32, precision=HIGHEST)

    return pl.pallas_call(
        body, name=name, out_shape=jax.ShapeDtypeStruct((D, N), F32), grid=(D // tm, N // tn),
        in_specs=[pl.BlockSpec((tm, Bn), lambda i, j: (i, 0)), pl.BlockSpec((Bn, tn), lambda i, j: (0, j))],
        out_specs=pl.BlockSpec((tm, tn), lambda i, j: (i, j)), compiler_params=_params("parallel", "parallel"),
    )(c_all_t, dmod)


def _adamw(w, g, m, v, *, name):
    R, C = w.shape
    tr = R if R * C <= (1 << 19) else _tile(R, max(8, (1 << 19) // C), 8)

    def body(w_ref, g_ref, m_ref, v_ref, d_ref, nm_ref, nv_ref):
        gv = g_ref[...]
        nm = ADAM_B1 * m_ref[...] + (1.0 - ADAM_B1) * gv
        nv = ADAM_B2 * v_ref[...] + (1.0 - ADAM_B2) * (gv * gv)
        m_hat = nm / (1.0 - ADAM_B1 ** ADAM_STEP)
        v_hat = nv / (1.0 - ADAM_B2 ** ADAM_STEP)
        d_ref[...] = -ADAM_LR * (m_hat / (jnp.sqrt(v_hat) + ADAM_EPS) + ADAM_WD * w_ref[...])
        nm_ref[...] = nm
        nv_ref[...] = nv

    blk = pl.BlockSpec((tr, C), lambda i: (i, 0))
    shp = jax.ShapeDtypeStruct((R, C), F32)
    return pl.pallas_call(
        body, name=name, out_shape=(shp, shp, shp), grid=(R // tr,), in_specs=[blk] * 4, out_specs=(blk,) * 3,
        compiler_params=_params("parallel"),
    )(w, g, m, v)


def _place():
    x, y, c = lax.axis_index("x"), lax.axis_index("y"), lax.axis_index("c")
    return x, y, c, [(1 - x, y), (x, 1 - y), (1 - x, 1 - y)]


def _remote(src, dst, send_sem, recv_sem, dev):
    return pltpu.make_async_remote_copy(src_ref=src, dst_ref=dst, send_sem=send_sem, recv_sem=recv_sem,
                                        device_id=dev, device_id_type=MESH)


def _allgather8(v, *, tie=None, name):
    R, C = v.shape

    def body(v_ref, out_ref, send_sems, recv_sems, local_sem):
        x, y, c, chips = _place()
        me, sibling = (x, y, c), (x, y, 1 - c)

        def rows(px, py, pc):
            return out_ref.at[pl.ds((4 * px + 2 * py + pc) * R, R), :]

        def copy(k, block, to, src=None):
            return _remote(rows(*block) if src is None else src, rows(*block), send_sems.at[k], recv_sems.at[k], to)

        mine = pltpu.make_async_copy(v_ref, rows(*me), local_sem)
        mine.start()
        first = [copy(0, me, sibling, src=v_ref)]
        first += [copy(1 + j, me, (*chip, c), src=v_ref) for j, chip in enumerate(chips)]
        for cp in first:
            cp.start()
        passed = [copy(4 + j, (*chip, c), sibling) for j, chip in enumerate(chips)]
        for j, chip in enumerate(chips):
            copy(1 + j, (*chip, c), me).wait_recv()
            passed[j].start()
        copy(0, sibling, me).wait_recv()
        for j, chip in enumerate(chips):
            copy(4 + j, (*chip, 1 - c), me).wait_recv()
        for cp in first + passed:
            cp.wait_send()
        mine.wait()

    body, tspec, targ = _tied(body, tie)
    out = pl.pallas_call(
        body, name=name, out_shape=jax.ShapeDtypeStruct((N_DEV * R, C), v.dtype),
        in_specs=tspec + [VMEM_FULL], out_specs=VMEM_FULL,
        scratch_shapes=[pltpu.SemaphoreType.DMA((7,)), pltpu.SemaphoreType.DMA((7,)), pltpu.SemaphoreType.DMA],
    )(*targ, v)
    return out.reshape(N_DEV, R, C)


SEM = pl.BlockSpec(memory_space=pltpu.SEMAPHORE)
HBM = pl.BlockSpec(memory_space=pltpu.HBM)
EFFECT = pltpu.SideEffectType.DATAFLOW_SIDE_EFFECTING
DMA_SEM = pltpu.SemaphoreType.DMA(())


def _in_hbm(a):
    return pltpu.with_memory_space_constraint(a, pltpu.HBM)


def _three_halves(land, r2):
    return land.at[pl.ds(0, N_CHIP - 1), pl.ds(0, r2)]


def _slot(chip, swap):
    return (chip % 2) * 2 + chip // 2 if swap else chip


def _gather_start(ws, after, swaps, *, name):
    n = len(ws)
    na = len(after)
    lands = [lax.empty((N_CHIP,) + w.shape, w.dtype) for w in ws]

    def body(*refs):
        w_refs, land_refs = refs[:n], refs[n:2 * n]
        send, recv = refs[2 * n + na:3 * n + na], refs[3 * n + na:4 * n + na]
        token = refs[6 * n + na]
        x, y, c, chips = _place()
        k = 2 * x + y
        for i in range(n):
            r2 = ws[i].shape[0] // 2
            for cx, cy in chips:
                _remote(w_refs[i].at[pl.ds(c * r2, r2)], land_refs[i].at[_slot(k, swaps[i]), pl.ds(c * r2, r2)], send[i], recv[i],
                        (cx, cy, c)).start()
        token[...] = jnp.zeros_like(token)

    outs = pl.pallas_call(
        body, name=name,
        out_shape=[DMA_SEM] * (2 * n) + [pltpu.HBM(w.shape, w.dtype) for w in ws] + [pltpu.HBM(l.shape, l.dtype) for l in lands]
        + [jax.ShapeDtypeStruct((8, LANES), F32)],
        in_specs=[HBM] * (2 * n) + [ANY] * na, out_specs=[SEM] * (2 * n) + [HBM] * (2 * n) + [VMEM_FULL],
        input_output_aliases={i: 2 * n + i for i in range(2 * n)},
        compiler_params=pltpu.CompilerParams(has_side_effects=EFFECT),
    )(*[_in_hbm(w) for w in ws], *[_in_hbm(l) for l in lands], *after)
    return outs[:n], outs[n:2 * n], outs[2 * n:3 * n], outs[3 * n:4 * n], outs[4 * n]


def _gather_forward(send, recv, ws, lands, after, swaps, *, name):
    n = len(ws)

    def body(*refs):
        w_refs, land_refs = refs[:n], refs[n:2 * n]
        send1, recv1 = refs[2 * n:3 * n], refs[3 * n:4 * n]
        send2, recv2 = refs[4 * n + 1 + 2 * n:4 * n + 1 + 3 * n], refs[4 * n + 1 + 3 * n:4 * n + 1 + 4 * n]
        x, y, c, chips = _place()
        sibling = (x, y, 1 - c)
        for i in range(n):
            r2 = ws[i].shape[0] // 2
            win = _three_halves(land_refs[i], r2)
            done = _remote(win, win, send1[i], recv1[i], sibling)
            done.wait_send()
            done.wait_recv()
            for cx, cy in chips:
                got = land_refs[i].at[_slot(2 * cx + cy, swaps[i]), pl.ds(c * r2, r2)]
                _remote(got, got, send2[i], recv2[i], sibling).start()
        token = refs[8 * n + 1]
        token[...] = jnp.zeros_like(token)

    outs = pl.pallas_call(
        body, name=name,
        out_shape=[pltpu.HBM(w.shape, w.dtype) for w in ws] + [pltpu.HBM(l.shape, l.dtype) for l in lands] + [DMA_SEM] * (2 * n)
        + [jax.ShapeDtypeStruct((8, LANES), F32)],
        in_specs=[HBM] * (2 * n) + [SEM] * (2 * n) + [ANY], out_specs=[HBM] * (2 * n) + [SEM] * (2 * n) + [VMEM_FULL],
        input_output_aliases={i: i for i in range(2 * n)},
        compiler_params=pltpu.CompilerParams(has_side_effects=EFFECT),
    )(*ws, *lands, *send, *recv, after)
    return outs[2 * n:3 * n], outs[3 * n:4 * n], outs[n:2 * n], outs[4 * n]


def _gather_finish(send, recv, lands, after, *, name):
    n = len(lands)

    def body(*refs):
        land_refs = refs[:n]
        send2, recv2 = refs[n:2 * n], refs[2 * n:3 * n]
        x, y, c, _ = _place()
        for i in range(n):
            win = _three_halves(land_refs[i], lands[i].shape[1] // 2)
            done = _remote(win, win, send2[i], recv2[i], (x, y, 1 - c))
            done.wait_send()
            done.wait_recv()

    return pl.pallas_call(
        body, name=name,
        out_shape=[pltpu.HBM(l.shape, l.dtype) for l in lands],
        in_specs=[HBM] * n + [SEM] * (2 * n) + [ANY], out_specs=[HBM] * n,
        input_output_aliases={i: i for i in range(n)},
        compiler_params=pltpu.CompilerParams(has_side_effects=EFFECT),
    )(*lands, *send, *recv, after)


def _scatter_start(gs, swaps, *, name):
    n = len(gs)
    lands = [lax.empty((N_DEV, g.shape[1] // 2, g.shape[2]), g.dtype) for g in gs]

    def body(*refs):
        g_refs, land_refs = refs[:n], refs[n:2 * n]
        send, recv = refs[2 * n:3 * n], refs[3 * n:4 * n]
        token = refs[6 * n]
        x, y, c, chips = _place()
        k = 2 * x + y
        me = 2 * k + c
        for i in range(n):
            r2 = gs[i].shape[1] // 2
            for cx, cy in chips:
                for cc in range(2):
                    _remote(g_refs[i].at[_slot(2 * cx + cy, swaps[i]), pl.ds(cc * r2, r2)], land_refs[i].at[me], send[i], recv[i],
                            (cx, cy, cc)).start()
            _remote(g_refs[i].at[_slot(k, swaps[i]), pl.ds((1 - c) * r2, r2)], land_refs[i].at[me], send[i], recv[i],
                    (x, y, 1 - c)).start()
        token[...] = jnp.zeros_like(token)

    outs = pl.pallas_call(
        body, name=name,
        out_shape=[DMA_SEM] * (2 * n) + [pltpu.HBM(g.shape, g.dtype) for g in gs] + [pltpu.HBM(l.shape, l.dtype) for l in lands]
        + [jax.ShapeDtypeStruct((8, LANES), F32)],
        in_specs=[HBM] * (2 * n), out_specs=[SEM] * (2 * n) + [HBM] * (2 * n) + [VMEM_FULL],
        input_output_aliases={i: 2 * n + i for i in range(2 * n)},
        compiler_params=pltpu.CompilerParams(has_side_effects=EFFECT),
    )(*[_in_hbm(g) for g in gs], *[_in_hbm(l) for l in lands])
    return outs[:n], outs[n:2 * n], outs[2 * n:3 * n], outs[3 * n:4 * n], outs[4 * n]


def _scatter_wait(send, recv, gs, lands, after, *, name):
    n = len(gs)

    def body(*refs):
        land_refs = refs[n:2 * n]
        send1, recv1 = refs[2 * n:3 * n], refs[3 * n:4 * n]
        x, y, c, _ = _place()
        for i in range(n):
            win = land_refs[i].at[pl.ds(0, N_DEV - 1)]
            done = _remote(win, win, send1[i], recv1[i], (x, y, 1 - c))
            done.wait_send()
            done.wait_recv()

    outs = pl.pallas_call(
        body, name=name,
        out_shape=[pltpu.HBM(g.shape, g.dtype) for g in gs] + [pltpu.HBM(l.shape, l.dtype) for l in lands],
        in_specs=[HBM] * (2 * n) + [SEM] * (2 * n) + [ANY], out_specs=[HBM] * (2 * n),
        input_output_aliases={i: i for i in range(2 * n)},
        compiler_params=pltpu.CompilerParams(has_side_effects=EFFECT),
    )(*gs, *lands, *send, *recv, after)
    return outs[:n], outs[n:]


def _share_halves(ts, *, name):
    n = len(ts)

    def body(*refs):
        outs = refs[n:2 * n]
        send_sems, recv_sems = refs[2 * n:]
        x, y, c, _ = _place()
        sibling = (x, y, 1 - c)
        cps = []
        for i in range(n):
            r2 = ts[i].shape[0] // 2
            mine = outs[i].at[pl.ds(c * r2, r2)]
            cps.append(_remote(mine, mine, send_sems.at[i], recv_sems.at[i], sibling))
            cps[-1].start()
        for i in range(n):
            r2 = ts[i].shape[0] // 2
            got = outs[i].at[pl.ds((1 - c) * r2, r2)]
            _remote(got, got, send_sems.at[i], recv_sems.at[i], sibling).wait_recv()
        for cp in cps:
            cp.wait_send()

    return pl.pallas_call(
        body, name=name,
        out_shape=[jax.ShapeDtypeStruct(t.shape, t.dtype) for t in ts],
        in_specs=[ANY] * n, out_specs=[ANY] * n, input_output_aliases={i: i for i in range(n)},
        scratch_shapes=[pltpu.SemaphoreType.DMA((n,)), pltpu.SemaphoreType.DMA((n,))],
    )(*ts)


def _sum_pieces(land, g, idx, *, name):
    _, r2, C = land.shape
    tr = _tile(r2, max(16, (1 << 20) // C), 16)
    nr = r2 // tr

    def body(idx_ref, land_ref, own_ref, o_ref, acc_ref):
        d = pl.program_id(1)
        mine = d == idx_ref[0]

        @pl.when(d == 0)
        def _():
            acc_ref[...] = jnp.zeros_like(acc_ref)

        @pl.when(mine)
        def _():
            acc_ref[...] += own_ref[...].astype(F32)

        @pl.when(jnp.logical_not(mine))
        def _():
            acc_ref[...] += land_ref[...].astype(F32)

        @pl.when(d == N_DEV - 1)
        def _():
            o_ref[...] = acc_ref[...]

    return pl.pallas_call(
        body, name=name, out_shape=jax.ShapeDtypeStruct((2 * r2, C), F32),
        grid_spec=pltpu.PrefetchScalarGridSpec(
            num_scalar_prefetch=1, grid=(nr, N_DEV),
            in_specs=[pl.BlockSpec((None, tr, C), lambda i, d, ix: (jnp.where(d == ix[0], (d + 1) % N_DEV, d), i, 0)),
                      pl.BlockSpec((None, tr, C), lambda i, d, ix: (ix[1], ix[2] * nr + i, 0))],
            out_specs=pl.BlockSpec((tr, C), lambda i, d, ix: (ix[2] * nr + i, 0)),
            scratch_shapes=[pltpu.VMEM((tr, C), F32)]),
        compiler_params=_params("parallel", "arbitrary"),
    )(idx, land, g)


def _sum_devices(v, *, name):
    n, R, C = v.shape

    def body(v_ref, o_ref):
        acc = v_ref[0]
        for j in range(1, n):
            acc = acc + v_ref[j]
        o_ref[...] = acc

    return pl.pallas_call(body, name=name, out_shape=jax.ShapeDtypeStruct((R, C), F32),
                          in_specs=[VMEM_FULL], out_specs=VMEM_FULL)(v)


def _shard_cols(shards, lo, hi, width):
    out = []
    while lo < hi:
        j = lo // width
        end = min(hi, (j + 1) * width)
        out.append(shards[j][:, lo - j * width:end - j * width])
        lo = end
    return out


def _from_col_shards(g):
    return jnp.transpose(g, (1, 0, 2)).reshape(g.shape[1], N_CHIP * g.shape[2])


def _to_col_shards(w):
    R, N = w.shape
    return jnp.transpose(w.reshape(R, N_CHIP, N // N_CHIP), (1, 0, 2))


def _split_heads(w, widths):
    R, N = w.shape
    per = sum(widths)
    w3 = w.reshape(R, N // per, per)
    lo = w3[:, :, :widths[0]].reshape(R, -1)
    hi = w3[:, :, widths[0]:].reshape(R, -1)
    return jnp.concatenate([lo, hi], axis=1)


def _merge_heads(w, widths):
    R, N = w.shape
    H = N // sum(widths)
    lo = w[:, :H * widths[0]].reshape(R, H, widths[0])
    hi = w[:, H * widths[0]:].reshape(R, H, widths[1])
    return jnp.concatenate([lo, hi], axis=2).reshape(R, N)


def _t5_bucket(dist):
    max_exact = REL_BUCKETS // 2
    n = jnp.maximum(dist, 0)
    large = max_exact + (jnp.log(jnp.maximum(n, 1).astype(F32) / max_exact)
                         / jnp.log(jnp.asarray(REL_MAX_DIST / max_exact, F32))
                         * (REL_BUCKETS - max_exact)).astype(jnp.int32)
    large = jnp.minimum(large, REL_BUCKETS - 1)
    return jnp.where(n < max_exact, n, large)


def _rel_tables():
    a = jnp.arange(SWA_BLOCK)
    b = jnp.arange(2 * SWA_BLOCK)
    dist = SWA_BLOCK + a[:, None] - b[None, :]
    valid = jnp.logical_and(dist >= 0, dist < SWA_BLOCK)
    onehot = jnp.logical_and(_t5_bucket(dist)[..., None] == jnp.arange(REL_BUCKETS), valid[..., None])
    onehot = onehot.astype(F32).reshape(2 * SWA_BLOCK * SWA_BLOCK, REL_BUCKETS)
    negmask = jnp.where(valid, 0.0, NEG).astype(F32).reshape(1, -1)
    return onehot, negmask


def _rope_tables(S):
    pos = jnp.arange(S, dtype=F32)
    inv = ROPE_THETA ** (-jnp.arange(0, MLA_ROPE, 2, dtype=F32) / MLA_ROPE)
    ang = pos[:, None] * inv[None, :]
    ang = jnp.concatenate([ang, ang, ang, ang], axis=-1)
    return jnp.cos(ang), jnp.sin(ang)


def _flat_pad(parts, rows=8):
    flat = jnp.concatenate([p.reshape(1, -1) for p in parts], axis=1)
    n = flat.shape[1]
    width = -(-n // (rows * LANES)) * LANES
    return jnp.pad(flat, ((0, 0), (0, rows * width - n))).reshape(rows, width)


def _unflat(vec, shapes):
    flat = vec.reshape(-1)
    out, off = [], 0
    for s in shapes:
        n = 1
        for d in s:
            n *= d
        out.append(flat[off:off + n].reshape(s))
        off += n
    return out


def kernel(x, c, w_ada, b_ada, g_pre_mix, g_post_mix, w_in, g_q_lat, w_uq, g_kv_lat, w_ukv, rel_bias, sinks, w_o, g_pre_ffn, g_post_ffn, w_up, conv_w, conv_b, w_down, loss_target, m_w_ada, m_b_ada, m_g_pre_mix, m_g_post_mix, m_w_in, m_g_q_lat, m_w_uq, m_g_kv_lat, m_w_ukv, m_rel_bias, m_sinks, m_w_o, m_g_pre_ffn, m_g_post_ffn, m_w_up, m_conv_w, m_conv_b, m_w_down, v_w_ada, v_b_ada, v_g_pre_mix, v_g_post_mix, v_w_in, v_g_q_lat, v_w_uq, v_g_kv_lat, v_w_ukv, v_rel_bias, v_sinks, v_w_o, v_g_pre_ffn, v_g_post_ffn, v_w_up, v_conv_w, v_conv_b, v_w_down):
    S, D = x.shape[1], x.shape[2]
    Rq, Rkv = g_q_lat.shape[1], g_kv_lat.shape[1]
    H = D // MLA_V
    NH = D // SWA_HD
    KW = SWA_KVH * SWA_HD
    F = w_down.shape[1] * N_CHIP
    xi, yi, ci = lax.axis_index("x"), lax.axis_index("y"), lax.axis_index("c")
    chip = 2 * xi + yi
    me = 2 * chip + ci
    x2, tgt = x[0], loss_target[0]

    c_all = _allgather8(jnp.broadcast_to(c, (8, D)), name="gather_c")[:, 0, :]
    n3 = w_ada.shape[2]
    mod_part = _ada_fwd(c_all, w_ada[0], lax.dynamic_slice(b_ada, (0, chip * n3), (1, n3)), name="ada_fwd")
    mod_all = _allgather8(mod_part, name="gather_mod")
    mod_me = lax.dynamic_index_in_dim(mod_all[0::2], me, axis=1, keepdims=False).reshape(1, 6 * D)
    sh1, sc1, gt1, sh2, sc2, gt2 = [mod_me[:, k * D:(k + 1) * D] for k in range(6)]

    swaps = [False, False, False, False, True, False]
    local = [w_in[0].astype(BF16)]
    send_a, recv_a, srcs_a, lands_a, token = _gather_start(local, (mod_all,), swaps[:1], name="gather_start_in")
    rest, token = lax.optimization_barrier(((w_uq[0], w_ukv[0], w_o[0], w_up[0], w_down[0]), token))
    local += [w.astype(BF16) for w in rest]
    send_b, recv_b, srcs_b, lands_b, token = _gather_start(local[1:], (token,), swaps[1:], name="gather_start_rest")
    send1, recv1, srcs, lands = send_a + send_b, recv_a + recv_b, srcs_a + srcs_b, lands_a + lands_b
    onehot, negmask = _rel_tables()
    npb = NH // (2 * SWA_KVH)
    rb_st = jnp.transpose(rel_bias.T.reshape(SWA_KVH, npb, 2, REL_BUCKETS), (0, 2, 1, 3)).reshape(NH, REL_BUCKETS)
    bias_m = (_matmul(rb_st, onehot.T, tie=token, name="rel_bias_table") + negmask).reshape(
        2 * SWA_KVH, npb * SWA_BLOCK, 2 * SWA_BLOCK)
    h = _modnorm_fwd(x2, g_pre_mix, sc1, sh1, name="pre_mix_norm")

    def whole(land, i):
        return lax.dynamic_update_index_in_dim(land, local[i], _slot(chip, swaps[i]), 0)

    def conv_slots(v):
        return jnp.stack([v[0], v[2], v[1], v[3]])

    s2, r2, l_in, _ = _gather_forward(send1[:1], recv1[:1], srcs[:1], lands[:1], h, swaps[:1], name="gather_forward_in")
    (l_in,) = _gather_finish(s2, r2, l_in, h, name="gather_finish_in")
    gin = whole(l_in, 0)
    o_kr = Rq + Rkv
    o_q = o_kr + MLA_ROPE
    o_g = o_q + NH * SWA_HD + 2 * KW
    n_gate, n_swa = 2 * D, o_g - o_q
    n_lat = -(-(o_q + MLA_ROPE) // PAIR) * PAIR
    runs = []
    for tl in range(D // PAIR):
        runs.append((o_g + tl * PAIR, o_g + (tl + 1) * PAIR, 2 * tl * PAIR))
        runs.append((o_g + D + tl * PAIR, o_g + D + (tl + 1) * PAIR, (2 * tl + 1) * PAIR))
    runs.append((o_q, o_g, n_gate))
    runs.append((0, o_q, n_gate + n_swa))
    csh = gin.shape[2]
    parts = []
    for lo, hi, _ in runs + [(o_kr, o_q, 0)]:
        parts += _shard_cols([gin[j] for j in range(N_CHIP)], lo, hi, csh)
    parts.append(jnp.zeros((D, n_lat - o_q - MLA_ROPE), BF16))
    w_in_all = jnp.concatenate(parts, axis=1)
    cos, sin = _rope_tables(S)
    sink_v = sinks.reshape(NH)

    z_lat = _matmul(h, w_in_all, bcols=(n_gate + n_swa, n_lat), name="in_proj_lat")
    z_swa = _matmul(h, w_in_all, bcols=(n_gate, n_swa), name="in_proj_swa")
    zg = _matmul(h, w_in_all, bcols=(0, n_gate), out_dtype=BF16, name="in_proj_gate")
    s2b, r2b, l_b, _ = _gather_forward(send1[1:4], recv1[1:4], srcs[1:4], lands[1:4], zg, swaps[1:4],
                                       name="gather_forward_attn")
    nq, nkv = _lat_norm_fwd(z_lat, g_q_lat, g_kv_lat, name="lat_norm")
    l_uq, l_ukv, l_o = _gather_finish(s2b, r2b, l_b, nq, name="gather_finish_attn")
    wuq = _split_heads(_from_col_shards(whole(l_uq, 1)), (MLA_NOPE, MLA_ROPE))
    wukv = _split_heads(_from_col_shards(whole(l_ukv, 2)), (MLA_NOPE, MLA_V))
    wo = whole(l_o, 3).reshape(D, D)
    q_raw = _matmul(nq, wuq, out_dtype=BF16, name="uq_proj")
    kv_raw = _matmul(nkv, wukv, out_dtype=BF16, name="ukv_proj")
    qp, kp, vv = _mla_pack_fwd(q_raw, kv_raw, z_lat, cos, sin, o_kr, name="mla_pack")
    o_a, lse = _flash_fwd(qp, kp, vv, name="mla_attn")
    s2c, r2c, l_c, tok_c = _gather_forward(send1[4:], recv1[4:], srcs[4:], lands[4:], o_a, swaps[4:],
                                           name="gather_forward_ffn")
    o_b = _swa_fwd(z_swa, bias_m, sink_v, name="swa_attn")
    mixin = _gate_fwd(zg, o_a, o_b, name="gate_mix")
    mix = _matmul(mixin, wo, tie=tok_c, name="o_proj")
    x1 = _resnorm_fwd(x2, mix, g_post_mix, gt1, name="post_mix_norm")
    h2 = _modnorm_fwd(x1, g_pre_ffn, sc2, sh2, name="pre_ffn_norm")
    l_up, l_down = _gather_finish(s2c, r2c, l_c, h2, name="gather_finish_ffn")
    cw_all = _allgather8(jnp.pad(conv_w[0], ((0, 5), (0, 0))), tie=l_down, name="gather_conv_w")[0::2, :3]
    cw = conv_slots(cw_all)
    cb = conv_slots(conv_b.reshape(N_CHIP, 1, -1))
    wup = whole(l_up, 4)
    wdown = whole(l_down, 5).reshape(F, D)
    t = _matmul(h2, wup, out_dtype=BF16, shards="out", name="up_proj")
    a = _conv_gate_fwd(t, cw, cb, name="conv_gate")
    yv = _matmul(a, wdown, name="down_proj")
    dout, dy, dg_post_ffn, dgt2, loss_tile = _resnorm_loss(x1, yv, g_post_ffn, gt2, tgt, name="post_ffn_norm_loss")

    big_params = dict(w_in=(w_in, m_w_in, v_w_in), w_uq=(w_uq, m_w_uq, v_w_uq), w_ukv=(w_ukv, m_w_ukv, v_w_ukv),
                      w_o=(w_o, m_w_o, v_w_o), w_up=(w_up, m_w_up, v_w_up), w_down=(w_down, m_w_down, v_w_down))
    res = {}

    def start(nms, gs):
        sw = [nm == "w_up" for nm in nms]
        send, recv, gsrc, glands, tok = _scatter_start(gs, sw, name="grads_start_" + nms[0])
        return (nms, send, recv, gsrc, glands), tok

    def finish(pendings, after):
        nms, send, recv, gsrc, glands = [sum((list(p[k]) for p in pendings), []) for k in range(5)]
        gsrc, glands = _scatter_wait(send, recv, gsrc, glands, after, name="grads_wait_" + nms[0])
        halves = [_sum_pieces(l, g, jnp.stack([me, _slot(chip, nm == "w_up"), ci]).astype(jnp.int32), name="grad_sum_" + nm)
                  for l, g, nm in zip(glands, gsrc, nms)]
        for nm, g in zip(nms, _share_halves(halves, name="grads_share_" + nms[0])):
            w, m, v = big_params[nm]
            res[nm] = (g,) + tuple(_adamw(w[0], g, m[0], v[0], name="adamw_" + nm))

    dw_down = _matmul(a, dy, ta=True, out_dtype=BF16, name="down_proj_dw")
    p_down, tok = start(["w_down"], [dw_down.reshape(N_CHIP, F // N_CHIP, D)])
    da = _matmul(dy, wdown, tb=True, out_dtype=BF16, tie=tok, name="down_proj_dx")
    du, dcw_s, dcb_s = _conv_gate_bwd(t, da, cw, cb, name="conv_gate_bwd")
    dt = _conv_bwd_dt(du, cw, name="conv_bwd_dt")
    dw_up = _matmul(h2, dt, ta=True, out_dtype=BF16, shards="out", name="up_proj_dw")
    p_up, tok = start(["w_up"], [dw_up])
    dh2 = _matmul(dt, wup, tb=True, tie=tok, shards="k", name="up_proj_dx")
    dx1, dg_pre_ffn, dsc2, dsh2 = _modnorm_bwd(dh2, x1, g_pre_ffn, sc2, dout, name="pre_ffn_norm_bwd")
    dmix, dg_post_mix, dgt1 = _resnorm_bwd(dx1, mix, g_post_mix, gt1, name="post_mix_norm_bwd")
    dw_o = _matmul(mixin, dmix, ta=True, out_dtype=BF16, name="o_proj_dw")
    p_o, tok = start(["w_o"], [dw_o.reshape(N_CHIP, D // N_CHIP, D)])
    dmixin = _matmul(dmix, wo, tb=True, tie=tok, name="o_proj_dx")
    do_a, do_b, dzg = _gate_bwd(dmixin, zg, o_a, o_b, name="gate_mix_bwd")
    dqp, dkp, dvv = _flash_bwd(qp, kp, vv, o_a, do_a, lse, name="mla_attn_bwd")
    dq_raw, dkv_raw, dkr = _mla_pack_bwd(dqp, dkp, dvv, cos, sin, name="mla_pack_bwd")
    dw_uq_p = _matmul(nq, dq_raw, ta=True, out_dtype=BF16, name="uq_proj_dw")
    dw_ukv_p = _matmul(nkv, dkv_raw, ta=True, out_dtype=BF16, name="ukv_proj_dw")
    p_qkv, tok = start(["w_uq", "w_ukv"], [_to_col_shards(_merge_heads(dw_uq_p, (MLA_NOPE, MLA_ROPE))),
                                           _to_col_shards(_merge_heads(dw_ukv_p, (MLA_NOPE, MLA_V)))])
    dnq = _matmul(dq_raw, wuq, tb=True, tie=tok, name="uq_proj_dx")
    dnkv = _matmul(dkv_raw, wukv, tb=True, name="ukv_proj_dx")
    dz_lat, dg_q, dg_kv = _lat_norm_bwd(z_lat, dnq, dnkv, dkr, g_q_lat, g_kv_lat, name="lat_norm_bwd")
    dz_swa, dbias, dsink = _swa_bwd(z_swa, bias_m, sink_v, o_b, do_b, name="swa_attn_bwd")
    dz = jnp.concatenate([dzg, dz_swa, dz_lat], axis=1)
    dw_in_p = _matmul(h, dz, ta=True, out_dtype=BF16, name="in_proj_dw")
    dw_shards = []
    for j in range(N_CHIP):
        cols = []
        for lo, hi, at in sorted(runs):
            a0, a1 = max(lo, j * csh), min(hi, (j + 1) * csh)
            if a0 < a1:
                cols.append(dw_in_p[:, at + a0 - lo:at + a1 - lo])
        dw_shards.append(jnp.concatenate(cols, axis=1))
    p_in, tok = start(["w_in"], [jnp.stack(dw_shards)])
    dh = _matmul(dz, w_in_all, tb=True, tie=tok, name="in_proj_dx")
    grad_x, dg_pre_mix, dsc1, dsh1 = _modnorm_bwd(dh, x2, g_pre_mix, sc1, dx1, name="pre_mix_norm_bwd")
    drel_st = _matmul(dbias.reshape(NH, -1), onehot, tie=grad_x, name="rel_bias_bwd")
    finish((p_down, p_up, p_o, p_qkv), drel_st)
    drel = jnp.transpose(drel_st.reshape(SWA_KVH, 2, npb, REL_BUCKETS), (0, 2, 1, 3)).reshape(NH, REL_BUCKETS).T

    dcw = _from_col_shards(conv_slots(dcw_s))
    dcb = conv_slots(dcb_s).reshape(1, -1)
    dmod = jnp.concatenate([dsh1, dsc1, dgt1, dsh2, dsc2, dgt2], axis=1)
    small = [dmod, dg_pre_mix, dg_post_mix, dg_pre_ffn, dg_post_ffn, dg_q, dg_kv, drel, dsink[:, :NH], dcb, dcw]
    shapes = [p.shape for p in small]
    done = [res[nm][1] for nm in ("w_down", "w_up", "w_o", "w_uq", "w_ukv")]
    small_all = _allgather8(_flat_pad(small), tie=done, name="gather_small_grads")
    tot = _unflat(_sum_devices(small_all, name="sum_small_grads"), shapes)
    g_b_ada, g_pre_mix_g, g_post_mix_g, g_pre_ffn_g, g_post_ffn_g, g_q_g, g_kv_g, g_rel, g_sinks, g_cb, g_cw_full = tot
    dmod_all = small_all.reshape(N_DEV, -1)[:, :6 * D]
    g_w_ada = _ada_bwd(c_all.T, lax.dynamic_slice(dmod_all, (0, chip * n3), (N_DEV, n3)), name="ada_bwd")
    ncw = conv_w.shape[2]
    g_cw = lax.dynamic_slice(g_cw_full, (0, chip * ncw), (3, ncw))

    res["w_ada"] = (g_w_ada,) + tuple(_adamw(w_ada[0], g_w_ada, m_w_ada[0], v_w_ada[0], name="adamw_w_ada"))
    finish((p_in,), g_w_ada)
    snames = ["b_ada", "g_pre_mix", "g_post_mix", "g_pre_ffn", "g_post_ffn", "g_q_lat", "g_kv_lat", "rel_bias", "sinks",
              "conv_b", "conv_w"]
    sw = [b_ada, g_pre_mix, g_post_mix, g_pre_ffn, g_post_ffn, g_q_lat, g_kv_lat, rel_bias, sinks, conv_b, conv_w]
    sm = [m_b_ada, m_g_pre_mix, m_g_post_mix, m_g_pre_ffn, m_g_post_ffn, m_g_q_lat, m_g_kv_lat, m_rel_bias, m_sinks,
          m_conv_b, m_conv_w]
    sv = [v_b_ada, v_g_pre_mix, v_g_post_mix, v_g_pre_ffn, v_g_post_ffn, v_g_q_lat, v_g_kv_lat, v_rel_bias, v_sinks,
          v_conv_b, v_conv_w]
    sg = [g_b_ada, g_pre_mix_g, g_post_mix_g, g_pre_ffn_g, g_post_ffn_g, g_q_g, g_kv_g, g_rel, g_sinks, g_cb, g_cw]
    sshapes = [w.shape for w in sw]
    sd, snm, snv = _adamw(_flat_pad(sw), _flat_pad(sg), _flat_pad(sm), _flat_pad(sv), name="adamw_small")
    sd, snm, snv = _unflat(sd, sshapes), _unflat(snm, sshapes), _unflat(snv, sshapes)
    for k, nm in enumerate(snames):
        res[nm] = (sg[k].reshape(sshapes[k]), sd[k], snm[k], snv[k])

    order = ["w_ada", "b_ada", "g_pre_mix", "g_post_mix", "w_in", "g_q_lat", "w_uq", "g_kv_lat", "w_ukv", "rel_bias", "sinks",
             "w_o", "g_pre_ffn", "g_post_ffn", "w_up", "conv_w", "conv_b", "w_down"]
    ref_shapes = dict(w_ada=w_ada.shape, w_in=w_in.shape, w_uq=w_uq.shape, w_ukv=w_ukv.shape, w_o=w_o.shape,
                      w_up=w_up.shape, w_down=w_down.shape)
    outs = []
    for k in range(4):
        for nm in order:
            arr = res[nm][k]
            outs.append(arr.reshape(ref_shapes[nm]) if nm in ref_shapes else arr)
    loss = lax.psum(loss_tile[0, 0], ("x", "y", "c"))
    return (loss, grad_x[None], *outs)
```

```python
import math

import jax
import jax.numpy as jnp
from jax import lax
from jax.experimental import pallas as pl
from jax.experimental.pallas import tpu as pltpu

F32 = jnp.float32
BF16 = jnp.bfloat16
MESH = pl.DeviceIdType.MESH
HIGHEST = lax.Precision.HIGHEST

N_DEV = 8
N_CHIP = 4
LANES = 128
MLA_NOPE = 128
MLA_ROPE = 64
MLA_V = 128
MLA_QK = MLA_NOPE + MLA_ROPE
MLA_QK_PAD = 256
ROPE_THETA = 10000.0
SWA_HD = 64
SWA_KVH = 4
SWA_BLOCK = 128
REL_BUCKETS = 32
REL_MAX_DIST = 128
PAIR = 512
EPS = 1e-6
NEG = -1e30
ADAM_LR = 0.001
ADAM_B1 = 0.9
ADAM_B2 = 0.999
ADAM_EPS = 1e-08
ADAM_WD = 0.01
ADAM_STEP = 10

ANY = pl.BlockSpec(memory_space=pl.ANY)
VMEM_FULL = pl.BlockSpec(memory_space=pltpu.VMEM)
SMEM_FULL = pl.BlockSpec(memory_space=pltpu.SMEM)


def _params(*sem):
    return pltpu.CompilerParams(dimension_semantics=sem if sem else None)


def _tied(body, tie):
    if tie is None:
        return body, [], []
    ties = list(tie) if isinstance(tie, (list, tuple)) else [tie]

    def tied_body(*refs):
        body(*refs[len(ties):])

    return tied_body, [ANY] * len(ties), ties


def _tile(n, pref, unit=LANES):
    best = None
    for t in range(unit, min(n, pref) + 1, unit):
        if n % t == 0:
            best = t
    return n if best is None else best


def _matmul(a, b, *, ta=False, tb=False, out_dtype=F32, tie=None, shards=None, bcols=None, name):
    a2 = a.shape[1:] if shards == "k" else a.shape
    b2 = b.shape[1:] if shards else b.shape
    nsh = b.shape[0] if shards else 1
    K, M = a2 if ta else a2[::-1]
    N, K2 = b2 if tb else b2[::-1]
    assert K == K2, (a.shape, b.shape, ta, tb)
    exact = a.dtype == F32
    col0 = 0
    if bcols is not None:
        assert not tb and shards is None
        col0, N = bcols
    tn = _tile(math.gcd(N, col0) if col0 else N, 2048)
    col0 //= tn
    tk = _tile(K, 2048)
    nkc = K // tk
    nk = nkc * (nsh if shards == "k" else 1)
    tm = M if M < 8 else _tile(M, 1024, LANES if ta else 8)
    dn = (((0 if ta else 1,), (1 if tb else 0,)), ((), ()))
    kax = 3 if shards == "out" else 2

    def product(a_ref, b_ref):
        return lax.dot_general(a_ref[...], b_ref[...], dn, preferred_element_type=F32,
                               precision=HIGHEST if exact else None)

    def body_acc(a_ref, b_ref, o_ref, acc_ref):
        k = pl.program_id(kax)

        @pl.when(k == 0)
        def _():
            acc_ref[...] = product(a_ref, b_ref)

        @pl.when(jnp.logical_and(k > 0, k < nk - 1))
        def _():
            acc_ref[...] += product(a_ref, b_ref)

        @pl.when(k == nk - 1)
        def _():
            o_ref[...] = (acc_ref[...] + product(a_ref, b_ref)).astype(o_ref.dtype)

    def body_one(a_ref, b_ref, o_ref):
        o_ref[...] = product(a_ref, b_ref).astype(o_ref.dtype)

    a_blk, b_blk = ((tk, tm) if ta else (tm, tk)), ((tn, tk) if tb else (tk, tn))
    a_at = (lambda i, k: (k, i)) if ta else (lambda i, k: (i, k))
    b_at = (lambda j, k: (j, k)) if tb else (lambda j, k: (k, j + col0))
    if shards == "out":
        grid = (nsh, M // tm, N // tn, nk)
        a_spec = pl.BlockSpec(a_blk, lambda s, i, j, k: a_at(i, k))
        b_spec = pl.BlockSpec((None,) + b_blk, lambda s, i, j, k: (s,) + b_at(j, k))
        o_spec = pl.BlockSpec((None, tm, tn), lambda s, i, j, k: (s, i, j))
        out_shape = jax.ShapeDtypeStruct((nsh, M, N), out_dtype)
        sem = ("parallel", "parallel", "parallel", "arbitrary")
    elif shards == "k":
        grid = (M // tm, N // tn, nk)
        a_spec = pl.BlockSpec((None,) + a_blk, lambda i, j, k: (k // nkc,) + a_at(i, k % nkc))
        b_spec = pl.BlockSpec((None,) + b_blk, lambda i, j, k: (k // nkc,) + b_at(j, k % nkc))
        o_spec = pl.BlockSpec((tm, tn), lambda i, j, k: (i, j))
        out_shape = jax.ShapeDtypeStruct((M, N), out_dtype)
        sem = ("parallel", "parallel", "arbitrary")
    else:
        grid = (M // tm, N // tn, nk)
        a_spec = pl.BlockSpec(a_blk, lambda i, j, k: a_at(i, k))
        b_spec = pl.BlockSpec(b_blk, lambda i, j, k: b_at(j, k))
        o_spec = pl.BlockSpec((tm, tn), lambda i, j, k: (i, j))
        out_shape = jax.ShapeDtypeStruct((M, N), out_dtype)
        sem = ("parallel", "parallel", "arbitrary")
    body, tspec, targ = _tied(body_one if nk == 1 else body_acc, tie)
    return pl.pallas_call(
        body, name=name, out_shape=out_shape, grid=grid, in_specs=tspec + [a_spec, b_spec], out_specs=o_spec,
        scratch_shapes=[] if nk == 1 else [pltpu.VMEM((tm, tn), F32)],
        compiler_params=_params(*sem),
    )(*targ, a, b)


def _row_tile(S, width):
    return _tile(S, max(8, (1 << 19) // width), 8)


def _rstd(x):
    return lax.rsqrt(jnp.mean(x * x, axis=-1, keepdims=True) + EPS)


def _acc_rows(ref, val, first):
    s = jnp.sum(val, axis=0, keepdims=True)

    @pl.when(first)
    def _():
        ref[...] = s

    @pl.when(jnp.logical_not(first))
    def _():
        ref[...] += s


def _modnorm_fwd(x, g, sc, sh, *, name):
    S, D = x.shape
    tr = _row_tile(S, D)

    def body(x_ref, g_ref, sc_ref, sh_ref, h_ref):
        xv = x_ref[...]
        n = (xv * _rstd(xv)) * g_ref[...]
        h_ref[...] = (n * (1.0 + sc_ref[...]) + sh_ref[...]).astype(BF16)

    row = pl.BlockSpec((tr, D), lambda i: (i, 0))
    vec = pl.BlockSpec((1, D), lambda i: (0, 0))
    return pl.pallas_call(
        body, name=name, out_shape=jax.ShapeDtypeStruct((S, D), BF16), grid=(S // tr,),
        in_specs=[row, vec, vec, vec], out_specs=row, compiler_params=_params("parallel"),
    )(x, g, sc, sh)


def _modnorm_bwd(dh, x, g, sc, dres, *, name):
    S, D = x.shape
    tr = _row_tile(S, D)

    def body(dh_ref, x_ref, g_ref, sc_ref, dres_ref, dx_ref, dg_ref, dsc_ref, dsh_ref):
        first = pl.program_id(0) == 0
        xv = x_ref[...]
        dhv = dh_ref[...]
        gv = g_ref[...]
        r = _rstd(xv)
        xhat = xv * r
        _acc_rows(dsh_ref, dhv, first)
        _acc_rows(dsc_ref, dhv * (xhat * gv), first)
        dn = dhv * (1.0 + sc_ref[...])
        _acc_rows(dg_ref, dn * xhat, first)
        dxhat = dn * gv
        proj = jnp.mean(dxhat * xhat, axis=-1, keepdims=True)
        dx_ref[...] = r * (dxhat - xhat * proj) + dres_ref[...]

    row = pl.BlockSpec((tr, D), lambda i: (i, 0))
    vec = pl.BlockSpec((1, D), lambda i: (0, 0))
    vshape = jax.ShapeDtypeStruct((1, D), F32)
    return pl.pallas_call(
        body, name=name,
        out_shape=(jax.ShapeDtypeStruct((S, D), F32), vshape, vshape, vshape), grid=(S // tr,),
        in_specs=[row, row, vec, vec, row], out_specs=(row, vec, vec, vec),
        compiler_params=_params("arbitrary"),
    )(dh, x, g, sc, dres)


def _resnorm_modnorm_fwd(xres, m, g, gt, g2, sc2, sh2, *, name):
    S, D = xres.shape
    tr = _row_tile(S, D)

    def body(x_ref, m_ref, g_ref, gt_ref, g2_ref, sc_ref, sh_ref, o_ref, h_ref):
        mv = m_ref[...]
        x1 = x_ref[...] + gt_ref[...] * ((mv * _rstd(mv)) * g_ref[...])
        o_ref[...] = x1
        n = (x1 * _rstd(x1)) * g2_ref[...]
        h_ref[...] = (n * (1.0 + sc_ref[...]) + sh_ref[...]).astype(BF16)

    row = pl.BlockSpec((tr, D), lambda i: (i, 0))
    vec = pl.BlockSpec((1, D), lambda i: (0, 0))
    return pl.pallas_call(
        body, name=name, out_shape=(jax.ShapeDtypeStruct((S, D), F32), jax.ShapeDtypeStruct((S, D), BF16)), grid=(S // tr,),
        in_specs=[row, row, vec, vec, vec, vec, vec], out_specs=(row, row), compiler_params=_params("parallel"),
    )(xres, m, g, gt, g2, sc2, sh2)


def _modnorm_resnorm_bwd(dh, x, g, sc, dres, m, g1, gt1, *, name):
    S, D = x.shape
    tr = _row_tile(S, D)

    def body(dh_ref, x_ref, g_ref, sc_ref, dres_ref, m_ref, g1_ref, gt1_ref,
             dx_ref, dg_ref, dsc_ref, dsh_ref, dm_ref, dg1_ref, dgt1_ref):
        first = pl.program_id(0) == 0
        xv = x_ref[...]
        dhv = dh_ref[...]
        gv = g_ref[...]
        r = _rstd(xv)
        xhat = xv * r
        _acc_rows(dsh_ref, dhv, first)
        _acc_rows(dsc_ref, dhv * (xhat * gv), first)
        dn = dhv * (1.0 + sc_ref[...])
        _acc_rows(dg_ref, dn * xhat, first)
        dxhat = dn * gv
        proj = jnp.mean(dxhat * xhat, axis=-1, keepdims=True)
        dx1 = r * (dxhat - xhat * proj) + dres_ref[...]
        dx_ref[...] = dx1
        mv = m_ref[...]
        g1v = g1_ref[...]
        r1 = _rstd(mv)
        mhat = mv * r1
        _acc_rows(dgt1_ref, dx1 * (mhat * g1v), first)
        dn1 = dx1 * gt1_ref[...]
        _acc_rows(dg1_ref, dn1 * mhat, first)
        dmhat = dn1 * g1v
        proj1 = jnp.mean(dmhat * mhat, axis=-1, keepdims=True)
        dm_ref[...] = (r1 * (dmhat - mhat * proj1)).astype(BF16)

    row = pl.BlockSpec((tr, D), lambda i: (i, 0))
    vec = pl.BlockSpec((1, D), lambda i: (0, 0))
    vshape = jax.ShapeDtypeStruct((1, D), F32)
    return pl.pallas_call(
        body, name=name,
        out_shape=(jax.ShapeDtypeStruct((S, D), F32), vshape, vshape, vshape, jax.ShapeDtypeStruct((S, D), BF16), vshape, vshape),
        grid=(S // tr,),
        in_specs=[row, row, vec, vec, row, row, vec, vec], out_specs=(row, vec, vec, vec, row, vec, vec),
        compiler_params=_params("arbitrary"),
    )(dh, x, g, sc, dres, m, g1, gt1)


def _resnorm_loss(xres, m, g, gt, target, *, name):
    S, D = xres.shape
    tr = _row_tile(S, D)

    def body(x_ref, m_ref, g_ref, gt_ref, t_ref, d_ref, dm_ref, dg_ref, dgt_ref, loss_ref):
        first = pl.program_id(0) == 0
        mv = m_ref[...]
        gv = g_ref[...]
        r = _rstd(mv)
        mhat = mv * r
        n = mhat * gv
        err = (x_ref[...] + gt_ref[...] * n) - t_ref[...]
        dv = err * (1.0 / D)
        d_ref[...] = dv
        part = 0.5 * jnp.sum(jnp.mean(err * err, axis=-1, keepdims=True), axis=0, keepdims=True)
        part = jnp.broadcast_to(part, loss_ref.shape)

        @pl.when(first)
        def _():
            loss_ref[...] = part

        @pl.when(jnp.logical_not(first))
        def _():
            loss_ref[...] += part

        _acc_rows(dgt_ref, dv * n, first)
        dn = dv * gt_ref[...]
        _acc_rows(dg_ref, dn * mhat, first)
        dmhat = dn * gv
        proj = jnp.mean(dmhat * mhat, axis=-1, keepdims=True)
        dm_ref[...] = (r * (dmhat - mhat * proj)).astype(BF16)

    row = pl.BlockSpec((tr, D), lambda i: (i, 0))
    vec = pl.BlockSpec((1, D), lambda i: (0, 0))
    vshape = jax.ShapeDtypeStruct((1, D), F32)
    return pl.pallas_call(
        body, name=name,
        out_shape=(jax.ShapeDtypeStruct((S, D), F32), jax.ShapeDtypeStruct((S, D), BF16), vshape, vshape,
                   jax.ShapeDtypeStruct((8, LANES), F32)), grid=(S // tr,),
        in_specs=[row, row, vec, vec, row], out_specs=(row, row, vec, vec, pl.BlockSpec((8, LANES), lambda i: (0, 0))),
        compiler_params=_params("arbitrary"),
    )(xres, m, g, gt, target)


def _lat_norm_fwd(z_lat, g_q, g_kv, *, name):
    S, W = z_lat.shape
    Rq, Rkv = g_q.shape[1], g_kv.shape[1]
    tr = _row_tile(S, W)

    def body(z_ref, gq_ref, gkv_ref, nq_ref, nkv_ref):
        cq = z_ref[:, :Rq]
        ckv = z_ref[:, Rq:Rq + Rkv]
        nq_ref[...] = ((cq * _rstd(cq)) * gq_ref[...]).astype(BF16)
        nkv_ref[...] = ((ckv * _rstd(ckv)) * gkv_ref[...]).astype(BF16)

    return pl.pallas_call(
        body, name=name,
        out_shape=(jax.ShapeDtypeStruct((S, Rq), BF16), jax.ShapeDtypeStruct((S, Rkv), BF16)), grid=(S // tr,),
        in_specs=[pl.BlockSpec((tr, W), lambda i: (i, 0)), pl.BlockSpec((1, Rq), lambda i: (0, 0)),
                  pl.BlockSpec((1, Rkv), lambda i: (0, 0))],
        out_specs=(pl.BlockSpec((tr, Rq), lambda i: (i, 0)), pl.BlockSpec((tr, Rkv), lambda i: (i, 0))),
        compiler_params=_params("parallel"),
    )(z_lat, g_q, g_kv)


def _lat_norm_bwd(z_lat, dnq, dnkv, dkr, g_q, g_kv, *, name):
    S, W = z_lat.shape
    Rq, Rkv = g_q.shape[1], g_kv.shape[1]
    tr = _row_tile(S, W)

    def one(c, dn, gv):
        r = _rstd(c)
        chat = c * r
        dchat = dn * gv
        proj = jnp.mean(dchat * chat, axis=-1, keepdims=True)
        return r * (dchat - chat * proj), dn * chat

    def body(z_ref, dnq_ref, dnkv_ref, dkr_ref, gq_ref, gkv_ref, dz_ref, dgq_ref, dgkv_ref):
        first = pl.program_id(0) == 0
        dcq, pq = one(z_ref[:, :Rq], dnq_ref[...], gq_ref[...])
        dckv, pkv = one(z_ref[:, Rq:Rq + Rkv], dnkv_ref[...], gkv_ref[...])
        _acc_rows(dgq_ref, pq, first)
        _acc_rows(dgkv_ref, pkv, first)
        dz_ref[:, :Rq] = dcq.astype(BF16)
        dz_ref[:, Rq:Rq + Rkv] = dckv.astype(BF16)
        dz_ref[:, Rq + Rkv:Rq + Rkv + LANES] = dkr_ref[...].astype(BF16)
        if W > Rq + Rkv + LANES:
            dz_ref[:, Rq + Rkv + LANES:] = jnp.zeros((tr, W - Rq - Rkv - LANES), BF16)

    return pl.pallas_call(
        body, name=name,
        out_shape=(jax.ShapeDtypeStruct((S, W), BF16), jax.ShapeDtypeStruct((1, Rq), F32),
                   jax.ShapeDtypeStruct((1, Rkv), F32)), grid=(S // tr,),
        in_specs=[pl.BlockSpec((tr, W), lambda i: (i, 0)), pl.BlockSpec((tr, Rq), lambda i: (i, 0)),
                  pl.BlockSpec((tr, Rkv), lambda i: (i, 0)), pl.BlockSpec((tr, LANES), lambda i: (i, 0)),
                  pl.BlockSpec((1, Rq), lambda i: (0, 0)), pl.BlockSpec((1, Rkv), lambda i: (0, 0))],
        out_specs=(pl.BlockSpec((tr, W), lambda i: (i, 0)), pl.BlockSpec((1, Rq), lambda i: (0, 0)),
                   pl.BlockSpec((1, Rkv), lambda i: (0, 0))),
        compiler_params=_params("arbitrary"),
    )(z_lat, dnq, dnkv, dkr, g_q, g_kv)


def _rot(x, lo32):
    a = pltpu.roll(x, 32, 1)
    b = pltpu.roll(x, LANES - 32, 1)
    return jnp.where(lo32, -b, a)


def _rot_t(g, lo32):
    a = pltpu.roll(g, 32, 1)
    b = pltpu.roll(g, LANES - 32, 1)
    return jnp.where(lo32, b, -a)


def _mla_pack_fwd(q_raw, kv_raw, z_lat, cos, sin, kr_off, *, name):
    S = q_raw.shape[0]
    H = kv_raw.shape[1] // (MLA_NOPE + MLA_V)
    W = z_lat.shape[1]
    scale = MLA_QK ** -0.5
    tr = min(S, 128)
    nope_w = H * MLA_NOPE

    def body(q_ref, kv_ref, z_ref, cos_ref, sin_ref, qp_ref, kp_ref, v_ref):
        lane = lax.broadcasted_iota(jnp.int32, (tr, LANES), 1)
        lo32 = (lane % 64) < 32
        lo64 = lane < 64
        c = cos_ref[...]
        s = sin_ref[...]
        kr = z_ref[:, kr_off:kr_off + LANES]
        kr = (kr * c + _rot(kr, lo32) * s).astype(BF16)
        for hp in range(H // 2):
            xb = q_ref[:, nope_w + hp * LANES:nope_w + (hp + 1) * LANES].astype(F32)
            rb = (xb * c + _rot(xb, lo32) * s) * scale
            for e in range(2):
                h = 2 * hp + e
                base = h * MLA_QK_PAD
                qp_ref[:, base:base + LANES] = (q_ref[:, h * LANES:(h + 1) * LANES].astype(F32) * scale).astype(BF16)
                keep = lo64 if e == 0 else jnp.logical_not(lo64)
                qp_ref[:, base + LANES:base + 2 * LANES] = jnp.where(keep, rb, 0.0).astype(BF16)
                kp_ref[:, base:base + LANES] = kv_ref[:, h * LANES:(h + 1) * LANES].astype(BF16)
                kp_ref[:, base + LANES:base + 2 * LANES] = kr
        v_ref[...] = kv_ref[:, nope_w:].astype(BF16)

    return pl.pallas_call(
        body, name=name,
        out_shape=(jax.ShapeDtypeStruct((S, H * MLA_QK_PAD), BF16), jax.ShapeDtypeStruct((S, H * MLA_QK_PAD), BF16),
                   jax.ShapeDtypeStruct((S, H * MLA_V), BF16)), grid=(S // tr,),
        in_specs=[pl.BlockSpec((tr, q_raw.shape[1]), lambda i: (i, 0)), pl.BlockSpec((tr, kv_raw.shape[1]), lambda i: (i, 0)),
                  pl.BlockSpec((tr, W), lambda i: (i, 0)), pl.BlockSpec((tr, LANES), lambda i: (i, 0)),
                  pl.BlockSpec((tr, LANES), lambda i: (i, 0))],
        out_specs=(pl.BlockSpec((tr, H * MLA_QK_PAD), lambda i: (i, 0)), pl.BlockSpec((tr, H * MLA_QK_PAD), lambda i: (i, 0)),
                   pl.BlockSpec((tr, H * MLA_V), lambda i: (i, 0))),
        compiler_params=_params("parallel"),
    )(q_raw, kv_raw, z_lat, cos, sin)


def _mla_pack_bwd(dqp, dkp, dv, cos, sin, *, name):
    S = dqp.shape[0]
    H = dv.shape[1] // MLA_V
    scale = MLA_QK ** -0.5
    tr = min(S, 128)
    nope_w = H * MLA_NOPE

    def body(dqp_ref, dkp_ref, dv_ref, cos_ref, sin_ref, dq_ref, dkv_ref, dkr_ref):
        lane = lax.broadcasted_iota(jnp.int32, (tr, LANES), 1)
        lo32 = (lane % 64) < 32
        lo64 = lane < 64
        c = cos_ref[...]
        s = sin_ref[...]
        dkr2 = jnp.zeros((tr, LANES), F32)
        for hp in range(H // 2):
            be = (2 * hp) * MLA_QK_PAD
            bo = (2 * hp + 1) * MLA_QK_PAD
            g = jnp.where(lo64, dqp_ref[:, be + LANES:be + 2 * LANES].astype(F32),
                          dqp_ref[:, bo + LANES:bo + 2 * LANES].astype(F32)) * scale
            dq_ref[:, nope_w + hp * LANES:nope_w + (hp + 1) * LANES] = (g * c + _rot_t(g * s, lo32)).astype(BF16)
            for h, base in ((2 * hp, be), (2 * hp + 1, bo)):
                dq_ref[:, h * LANES:(h + 1) * LANES] = (dqp_ref[:, base:base + LANES].astype(F32) * scale).astype(BF16)
                dkv_ref[:, h * LANES:(h + 1) * LANES] = dkp_ref[:, base:base + LANES].astype(BF16)
                dkr2 = dkr2 + dkp_ref[:, base + LANES:base + 2 * LANES].astype(F32)
        dkr2 = dkr2 * c + _rot_t(dkr2 * s, lo32)
        dkr2 = dkr2 + pltpu.roll(dkr2, 64, 1)
        dkr_ref[...] = jnp.where(lo64, dkr2, 0.0)
        dkv_ref[:, nope_w:] = dv_ref[...].astype(BF16)

    return pl.pallas_call(
        body, name=name,
        out_shape=(jax.ShapeDtypeStruct((S, nope_w + H * MLA_ROPE), BF16), jax.ShapeDtypeStruct((S, 2 * nope_w), BF16),
                   jax.ShapeDtypeStruct((S, LANES), F32)), grid=(S // tr,),
        in_specs=[pl.BlockSpec((tr, H * MLA_QK_PAD), lambda i: (i, 0)), pl.BlockSpec((tr, H * MLA_QK_PAD), lambda i: (i, 0)),
                  pl.BlockSpec((tr, H * MLA_V), lambda i: (i, 0)), pl.BlockSpec((tr, LANES), lambda i: (i, 0)),
                  pl.BlockSpec((tr, LANES), lambda i: (i, 0))],
        out_specs=(pl.BlockSpec((tr, nope_w + H * MLA_ROPE), lambda i: (i, 0)), pl.BlockSpec((tr, 2 * nope_w), lambda i: (i, 0)),
                   pl.BlockSpec((tr, LANES), lambda i: (i, 0))),
        compiler_params=_params("parallel"),
    )(dqp, dkp, dv, cos, sin)


FLASH_HB = 2


def _causal_pairs(nb):
    qi = [i for i in range(nb) for j in range(i + 1)]
    kj = [j for i in range(nb) for j in range(i + 1)]
    return jnp.asarray(qi, jnp.int32), jnp.asarray(kj, jnp.int32)


def _scores(q, k, diagonal, t):
    s = lax.dot_general(q, k, (((1,), (1,)), ((), ())), preferred_element_type=F32)
    if diagonal:
        row = lax.broadcasted_iota(jnp.int32, (t, t), 0)
        col = lax.broadcasted_iota(jnp.int32, (t, t), 1)
        s = jnp.where(col <= row, s, NEG)
    return s


def _flash_fwd(qp, kp, v, *, name):
    S = qp.shape[0]
    H = v.shape[1] // MLA_V
    t = min(S, 512)
    nb = S // t
    HB = 2 * FLASH_HB
    qi, kj = _causal_pairs(nb)
    QW, VW = MLA_QK_PAD, MLA_V

    def body(qi_ref, kj_ref, q_ref, k_ref, v_ref, o_ref, lse_ref, m_s, l_s, acc_s):
        pr = pl.program_id(1)
        i = qi_ref[pr]
        j = kj_ref[pr]

        @pl.when(j == 0)
        def _():
            m_s[...] = jnp.full_like(m_s, NEG)
            l_s[...] = jnp.zeros_like(l_s)
            acc_s[...] = jnp.zeros_like(acc_s)

        def step(diagonal):
            state = [(m_s[hh], l_s[hh], acc_s[hh]) for hh in range(HB)]
            new = []
            for hh, (m_prev, l_prev, acc_prev) in enumerate(state):
                s = _scores(q_ref[:, hh * QW:(hh + 1) * QW], k_ref[:, hh * QW:(hh + 1) * QW], diagonal, t)
                m_cur = jnp.maximum(m_prev, jnp.max(s, axis=1, keepdims=True))
                alpha = jnp.exp(m_prev - m_cur)
                p = jnp.exp(s - m_cur[:, :1])
                l_new = alpha * l_prev + jnp.sum(p, axis=1, keepdims=True)
                acc = alpha * acc_prev + jnp.dot(p.astype(BF16), v_ref[:, hh * VW:(hh + 1) * VW], preferred_element_type=F32)
                new.append((m_cur, l_new, acc))
            for hh, (m_cur, l_new, acc) in enumerate(new):
                if diagonal:
                    o_ref[:, hh * VW:(hh + 1) * VW] = acc / l_new
                    lse_ref[hh] = m_cur + jnp.log(l_new)
                else:
                    l_s[hh] = l_new
                    acc_s[hh] = acc
                    m_s[hh] = m_cur

        @pl.when(i != j)
        def _():
            step(False)

        @pl.when(i == j)
        def _():
            step(True)

    return pl.pallas_call(
        body, name=name,
        out_shape=(jax.ShapeDtypeStruct((S, H * VW), F32), jax.ShapeDtypeStruct((H, S, LANES), F32)),
        grid_spec=pltpu.PrefetchScalarGridSpec(
            num_scalar_prefetch=2, grid=(H // HB, qi.shape[0]),
            in_specs=[pl.BlockSpec((t, HB * QW), lambda g, p, qi, kj: (qi[p], g)),
                      pl.BlockSpec((t, HB * QW), lambda g, p, qi, kj: (kj[p], g)),
                      pl.BlockSpec((t, HB * VW), lambda g, p, qi, kj: (kj[p], g))],
            out_specs=(pl.BlockSpec((t, HB * VW), lambda g, p, qi, kj: (qi[p], g)),
                       pl.BlockSpec((HB, t, LANES), lambda g, p, qi, kj: (g, qi[p], 0))),
            scratch_shapes=[pltpu.VMEM((HB, t, LANES), F32), pltpu.VMEM((HB, t, LANES), F32), pltpu.VMEM((HB, t, VW), F32)]),
        compiler_params=_params("parallel", "arbitrary"),
    )(qi, kj, qp, kp, v)


def _flash_bwd(qp, kp, v, o, do, lse, *, name):
    S = qp.shape[0]
    H = v.shape[1] // MLA_V
    t = min(S, 512)
    nb = S // t
    HB = FLASH_HB
    qi, kj = _causal_pairs(nb)
    npairs = qi.shape[0]
    QW, VW = MLA_QK_PAD, MLA_V
    tn = (((0,), (0,)), ((), ()))
    nt = (((1,), (1,)), ((), ()))

    def body(qi_ref, kj_ref, q_ref, k_ref, v_ref, o_ref, do_ref, lse_ref, dq_ref, dk_ref, dv_ref, dq_s, dk_s, dv_s):
        pr = pl.program_id(1)
        i = qi_ref[pr]
        j = kj_ref[pr]
        rows = pl.ds(pl.multiple_of(j * t, t), t)

        @pl.when(pr == 0)
        def _():
            dk_s[...] = jnp.zeros_like(dk_s)
            dv_s[...] = jnp.zeros_like(dv_s)

        @pl.when(j == 0)
        def _():
            dq_s[...] = jnp.zeros_like(dq_s)

        def step(diagonal):
            for hh in range(HB):
                q = q_ref[:, hh * QW:(hh + 1) * QW]
                k = k_ref[:, hh * QW:(hh + 1) * QW]
                dob = do_ref[:, hh * VW:(hh + 1) * VW]
                p = jnp.exp(_scores(q, k, diagonal, t) - lse_ref[hh][:, :1])
                delta = jnp.sum(dob.astype(F32) * o_ref[:, hh * VW:(hh + 1) * VW], axis=1, keepdims=True)
                dp = lax.dot_general(dob, v_ref[:, hh * VW:(hh + 1) * VW], nt, preferred_element_type=F32)
                dsb = (p * (dp - delta)).astype(BF16)
                dv_s[rows, hh * VW:(hh + 1) * VW] += lax.dot_general(p.astype(BF16), dob, tn, preferred_element_type=F32)
                dk_s[rows, hh * QW:(hh + 1) * QW] += lax.dot_general(dsb, q, tn, preferred_element_type=F32)
                dq = dq_s[:, hh * QW:(hh + 1) * QW] + jnp.dot(dsb, k, preferred_element_type=F32)
                if diagonal:
                    dq_ref[:, hh * QW:(hh + 1) * QW] = dq.astype(BF16)
                else:
                    dq_s[:, hh * QW:(hh + 1) * QW] = dq

        @pl.when(i != j)
        def _():
            step(False)

        @pl.when(i == j)
        def _():
            step(True)

        @pl.when(pr == npairs - 1)
        def _():
            dk_ref[...] = dk_s[...].astype(BF16)
            dv_ref[...] = dv_s[...].astype(BF16)

    qside = lambda g, p, qi, kj: (qi[p], g)
    kside = lambda g, p, qi, kj: (kj[p], g)
    whole = lambda g, p, qi, kj: (0, g)
    return pl.pallas_call(
        body, name=name,
        out_shape=(jax.ShapeDtypeStruct((S, H * QW), BF16), jax.ShapeDtypeStruct((S, H * QW), BF16),
                   jax.ShapeDtypeStruct((S, H * VW), BF16)),
        grid_spec=pltpu.PrefetchScalarGridSpec(
            num_scalar_prefetch=2, grid=(H // HB, npairs),
            in_specs=[pl.BlockSpec((t, HB * QW), qside), pl.BlockSpec((t, HB * QW), kside), pl.BlockSpec((t, HB * VW), kside),
                      pl.BlockSpec((t, HB * VW), qside), pl.BlockSpec((t, HB * VW), qside),
                      pl.BlockSpec((HB, t, LANES), lambda g, p, qi, kj: (g, qi[p], 0))],
            out_specs=(pl.BlockSpec((t, HB * QW), qside), pl.BlockSpec((S, HB * QW), whole), pl.BlockSpec((S, HB * VW), whole)),
            scratch_shapes=[pltpu.VMEM((t, HB * QW), F32), pltpu.VMEM((S, HB * QW), F32), pltpu.VMEM((S, HB * VW), F32)]),
        compiler_params=_params("parallel", "arbitrary"),
    )(qi, kj, qp, kp, v, o, do, lse)


def _swa_kv_halves(blk, hf, lo):
    if hf == 0:
        a = jnp.where(lo, blk, 0.0)
        b = pltpu.roll(a, 64, 1)
    else:
        b = jnp.where(lo, 0.0, blk)
        a = pltpu.roll(b, 64, 1)
    return a.astype(BF16), b.astype(BF16)


def _swa_softmax(qs, kx, bias, neg0, sk):
    s = lax.dot_general(qs, kx, (((1,), (1,)), ((), ())), preferred_element_type=F32) + bias + neg0
    m = jnp.maximum(jnp.max(s, axis=1, keepdims=True), sk)
    e = jnp.exp(s - m)
    es = jnp.exp(sk - m)
    inv = 1.0 / (jnp.sum(e, axis=1, keepdims=True) + es)
    return e * inv, es * inv


def _swa_stack(ref, kvh, npb, scale=None):
    parts = [ref[:, (kvh * npb + pb) * LANES:(kvh * npb + pb + 1) * LANES] for pb in range(npb)]
    x = jnp.concatenate(parts, axis=0)
    return x if scale is None else x * scale


def _swa_sink_col(sink_ref, kvh, e, npb):
    row = lax.broadcasted_iota(jnp.int32, (npb * SWA_BLOCK, 1), 0)
    col = jnp.zeros((npb * SWA_BLOCK, 1), F32) + sink_ref[2 * (kvh * npb) + e]
    for pb in range(1, npb):
        col = jnp.where(row >= pb * SWA_BLOCK, sink_ref[2 * (kvh * npb + pb) + e], col)
    return col


def _swa_fwd(z_swa, bias_st, sinks, *, name):
    S, W = z_swa.shape
    npb = bias_st.shape[1] // SWA_BLOCK
    NH = 2 * SWA_KVH * npb
    QW = NH * SWA_HD
    KW = SWA_KVH * SWA_HD
    nb = S // SWA_BLOCK
    B = SWA_BLOCK
    assert SWA_KVH % 2 == 0 and W == QW + 2 * KW

    def body(sink_ref, q_ref, kvc_ref, kvp_ref, b_ref, o_ref):
        n = pl.program_id(0)
        lo = lax.broadcasted_iota(jnp.int32, (2 * B, LANES), 1) < 64
        col = lax.broadcasted_iota(jnp.int32, (npb * B, 2 * B), 1)
        neg0 = jnp.where(jnp.logical_and(col < B, n == 0), NEG, 0.0)
        for kb in range(SWA_KVH // 2):
            kblk = jnp.concatenate([kvp_ref[:, kb * LANES:(kb + 1) * LANES], kvc_ref[:, kb * LANES:(kb + 1) * LANES]], axis=0)
            vblk = jnp.concatenate([kvp_ref[:, KW + kb * LANES:KW + (kb + 1) * LANES],
                                    kvc_ref[:, KW + kb * LANES:KW + (kb + 1) * LANES]], axis=0)
            for hf in range(2):
                kvh = 2 * kb + hf
                ks = _swa_kv_halves(kblk, hf, lo)
                vs = _swa_kv_halves(vblk, hf, lo)
                qs = _swa_stack(q_ref, kvh, npb, SWA_HD ** -0.5).astype(BF16)
                acc = jnp.zeros((npb * B, LANES), F32)
                for e in range(2):
                    p, _ = _swa_softmax(qs, ks[e], b_ref[2 * kvh + e], neg0, _swa_sink_col(sink_ref, kvh, e, npb))
                    acc = acc + jnp.dot(p.astype(BF16), vs[e], preferred_element_type=F32)
                for pb in range(npb):
                    P = kvh * npb + pb
                    o_ref[:, P * LANES:(P + 1) * LANES] = acc[pb * B:(pb + 1) * B]

    kvcol = QW // (2 * KW)
    assert QW % (2 * KW) == 0
    return pl.pallas_call(
        body, name=name,
        out_shape=jax.ShapeDtypeStruct((S, QW), F32), grid=(nb,),
        in_specs=[SMEM_FULL, pl.BlockSpec((B, QW), lambda n: (n, 0)), pl.BlockSpec((B, 2 * KW), lambda n: (n, kvcol)),
                  pl.BlockSpec((B, 2 * KW), lambda n: (jnp.maximum(n - 1, 0), kvcol)),
                  pl.BlockSpec(bias_st.shape, lambda n: (0, 0, 0))],
        out_specs=pl.BlockSpec((B, QW), lambda n: (n, 0)),
        compiler_params=_params("parallel"),
    )(sinks, z_swa, z_swa, z_swa, bias_st)


def _swa_bwd(z_swa, bias_st, sinks, o, do, *, name):
    S, W = z_swa.shape
    npb = bias_st.shape[1] // SWA_BLOCK
    NH = 2 * SWA_KVH * npb
    QW = NH * SWA_HD
    KW = SWA_KVH * SWA_HD
    nb = S // SWA_BLOCK
    B = SWA_BLOCK
    scale = SWA_HD ** -0.5
    tn = (((0,), (0,)), ((), ()))
    nt = (((1,), (1,)), ((), ()))

    def fold(x, hf, lo):
        x = x + pltpu.roll(x, 64, 1)
        return jnp.where(lo, x, 0.0) if hf == 0 else jnp.where(lo, 0.0, x)

    def body(sink_ref, q_ref, kvc_ref, kvp_ref, b_ref, o_ref, do_ref, dz_ref, dbias_ref, dsink_ref,
             cq_s, ck_s, cv_s, nq_s, nk_s, nv_s, pk_s, pv_s):
        n = pl.program_id(0)

        @pl.when(n == 0)
        def _():
            dbias_ref[...] = jnp.zeros_like(dbias_ref)
            dsink_ref[...] = jnp.zeros_like(dsink_ref)
            cq_s[...] = jnp.zeros_like(cq_s)
            ck_s[...] = jnp.zeros_like(ck_s)
            cv_s[...] = jnp.zeros_like(cv_s)

        @pl.when(n == nb)
        def _():
            pk_s[...] = jnp.zeros_like(pk_s)
            pv_s[...] = jnp.zeros_like(pv_s)

        @pl.when(n < nb)
        def _():
            lo = lax.broadcasted_iota(jnp.int32, (2 * B, LANES), 1) < 64
            lo1 = lax.broadcasted_iota(jnp.int32, (npb * B, LANES), 1) < 64
            lane1 = lax.broadcasted_iota(jnp.int32, (1, LANES), 1)
            col = lax.broadcasted_iota(jnp.int32, (npb * B, 2 * B), 1)
            neg0 = jnp.where(jnp.logical_and(col < B, n == 0), NEG, 0.0)
            dsink = jnp.zeros((1, LANES), F32)
            for kb in range(SWA_KVH // 2):
                kblk = jnp.concatenate([kvp_ref[:, kb * LANES:(kb + 1) * LANES], kvc_ref[:, kb * LANES:(kb + 1) * LANES]], axis=0)
                vblk = jnp.concatenate([kvp_ref[:, KW + kb * LANES:KW + (kb + 1) * LANES],
                                        kvc_ref[:, KW + kb * LANES:KW + (kb + 1) * LANES]], axis=0)
                dkblk = jnp.zeros((2 * B, LANES), F32)
                dvblk = jnp.zeros((2 * B, LANES), F32)
                for hf in range(2):
                    kvh = 2 * kb + hf
                    ks = _swa_kv_halves(kblk, hf, lo)
                    vs = _swa_kv_halves(vblk, hf, lo)
                    qs = _swa_stack(q_ref, kvh, npb, scale).astype(BF16)
                    dob = _swa_stack(do_ref, kvh, npb)
                    prod = dob.astype(F32) * _swa_stack(o_ref, kvh, npb)
                    dkj = jnp.zeros((2 * B, LANES), F32)
                    dvj = jnp.zeros((2 * B, LANES), F32)
                    dqs = jnp.zeros((npb * B, LANES), F32)
                    for e in range(2):
                        keep = lo1 if e == 0 else jnp.logical_not(lo1)
                        p, psink = _swa_softmax(qs, ks[e], b_ref[2 * kvh + e], neg0, _swa_sink_col(sink_ref, kvh, e, npb))
                        delta = jnp.sum(jnp.where(keep, prod, 0.0), axis=1, keepdims=True)
                        dp = lax.dot_general(dob, vs[e], nt, preferred_element_type=F32)
                        ds = p * (dp - delta)
                        dbias_ref[2 * kvh + e] += ds
                        pd = psink * delta
                        for pb in range(npb):
                            dsh = -jnp.sum(pd[pb * B:(pb + 1) * B], axis=0, keepdims=True)
                            dsink = dsink + jnp.where(lane1 == 2 * (kvh * npb + pb) + e, dsh, 0.0)
                        dsb = ds.astype(BF16)
                        dqs = dqs + jnp.dot(dsb, ks[e], preferred_element_type=F32)
                        keep2 = lo if e == 0 else jnp.logical_not(lo)
                        dkj = dkj + jnp.where(keep2, lax.dot_general(dsb, qs, tn, preferred_element_type=F32), 0.0)
                        dvj = dvj + jnp.where(keep2, lax.dot_general(p.astype(BF16), dob, tn, preferred_element_type=F32), 0.0)
                    for pb in range(npb):
                        P = kvh * npb + pb
                        nq_s[:, P * LANES:(P + 1) * LANES] = dqs[pb * B:(pb + 1) * B] * scale
                    dkblk = dkblk + fold(dkj, hf, lo)
                    dvblk = dvblk + fold(dvj, hf, lo)
                pk_s[:, kb * LANES:(kb + 1) * LANES] = dkblk[:B]
                nk_s[:, kb * LANES:(kb + 1) * LANES] = dkblk[B:]
                pv_s[:, kb * LANES:(kb + 1) * LANES] = dvblk[:B]
                nv_s[:, kb * LANES:(kb + 1) * LANES] = dvblk[B:]
            dsink_ref[...] += dsink

        dz_ref[:, :QW] = cq_s[...].astype(BF16)
        dz_ref[:, QW:QW + KW] = (ck_s[...] + pk_s[...]).astype(BF16)
        dz_ref[:, QW + KW:] = (cv_s[...] + pv_s[...]).astype(BF16)

        @pl.when(n < nb)
        def _():
            cq_s[...] = nq_s[...]
            ck_s[...] = nk_s[...]
            cv_s[...] = nv_s[...]

    kvcol = QW // (2 * KW)
    cur = lambda n: (jnp.minimum(n, nb - 1), 0)
    return pl.pallas_call(
        body, name=name,
        out_shape=(jax.ShapeDtypeStruct((S, W), BF16), jax.ShapeDtypeStruct(bias_st.shape, F32),
                   jax.ShapeDtypeStruct((1, LANES), F32)),
        grid=(nb + 1,),
        in_specs=[SMEM_FULL, pl.BlockSpec((B, QW), cur), pl.BlockSpec((B, 2 * KW), lambda n: (jnp.minimum(n, nb - 1), kvcol)),
                  pl.BlockSpec((B, 2 * KW), lambda n: (jnp.maximum(jnp.minimum(n, nb - 1) - 1, 0), kvcol)),
                  pl.BlockSpec(bias_st.shape, lambda n: (0, 0, 0)), pl.BlockSpec((B, QW), cur), pl.BlockSpec((B, QW), cur)],
        out_specs=(pl.BlockSpec((B, W), lambda n: (jnp.maximum(n - 1, 0), 0)),
                   pl.BlockSpec(bias_st.shape, lambda n: (0, 0, 0)), pl.BlockSpec((1, LANES), lambda n: (0, 0))),
        scratch_shapes=[pltpu.VMEM((B, QW), F32), pltpu.VMEM((B, KW), F32), pltpu.VMEM((B, KW), F32),
                        pltpu.VMEM((B, QW), F32), pltpu.VMEM((B, KW), F32), pltpu.VMEM((B, KW), F32),
                        pltpu.VMEM((B, KW), F32), pltpu.VMEM((B, KW), F32)],
        compiler_params=_params("arbitrary"),
    )(sinks, z_swa, z_swa, z_swa, bias_st, o, do)


def _gate_fwd(zg, o_a, o_b, *, name):
    S, D = o_a.shape
    tr = min(S, 512)

    def body(z_ref, a_ref, b_ref, m_ref):
        ga = jax.nn.sigmoid(z_ref[:, :PAIR].astype(F32))
        gb = jax.nn.sigmoid(z_ref[:, PAIR:].astype(F32))
        m_ref[...] = (ga * a_ref[...] + gb * b_ref[...]).astype(BF16)

    col = pl.BlockSpec((tr, PAIR), lambda i, j: (i, j))
    return pl.pallas_call(
        body, name=name, out_shape=jax.ShapeDtypeStruct((S, D), BF16), grid=(S // tr, D // PAIR),
        in_specs=[pl.BlockSpec((tr, 2 * PAIR), lambda i, j: (i, j)), col, col], out_specs=col,
        compiler_params=_params("parallel", "parallel"),
    )(zg, o_a, o_b)


def _gate_bwd(dmix, zg, o_a, o_b, *, name):
    S, D = o_a.shape
    tr = min(S, 512)

    def body(d_ref, z_ref, a_ref, b_ref, da_ref, db_ref, dz_ref):
        d = d_ref[...].astype(F32)
        ga = jax.nn.sigmoid(z_ref[:, :PAIR].astype(F32))
        gb = jax.nn.sigmoid(z_ref[:, PAIR:].astype(F32))
        da_ref[...] = (d * ga).astype(BF16)
        db_ref[...] = (d * gb).astype(BF16)
        dz_ref[:, :PAIR] = (d * a_ref[...] * (ga * (1.0 - ga))).astype(BF16)
        dz_ref[:, PAIR:] = (d * b_ref[...] * (gb * (1.0 - gb))).astype(BF16)

    col = pl.BlockSpec((tr, PAIR), lambda i, j: (i, j))
    wide = pl.BlockSpec((tr, 2 * PAIR), lambda i, j: (i, j))
    return pl.pallas_call(
        body, name=name,
        out_shape=(jax.ShapeDtypeStruct((S, D), BF16), jax.ShapeDtypeStruct((S, D), BF16), jax.ShapeDtypeStruct((S, 2 * D), BF16)),
        grid=(S // tr, D // PAIR), in_specs=[col, wide, col, col], out_specs=(col, col, wide),
        compiler_params=_params("parallel", "parallel"),
    )(dmix, zg, o_a, o_b)


def _conv_u(t_ref, prev_ref, w_ref, b_ref, m, i):
    cur = t_ref[m].astype(F32)
    live = (i > 0).astype(F32)
    p6 = prev_ref[m, 14:15, :].astype(F32) * live
    p7 = prev_ref[m, 15:16, :].astype(F32) * live
    row = lax.broadcasted_iota(jnp.int32, cur.shape, 0)
    t1 = jnp.where(row == 0, p7, pltpu.roll(cur, 1, 0))
    t2 = jnp.where(row == 0, p6, jnp.where(row == 1, p7, pltpu.roll(cur, 2, 0)))
    u = ((b_ref[m] + w_ref[m, 0:1, :] * t2) + w_ref[m, 1:2, :] * t1) + w_ref[m, 2:3, :] * cur
    return u, cur, t1, t2


def _conv_specs(tr, tc):
    blk = pl.BlockSpec((2, tr, tc), lambda p, j, i: (p, i, j))
    prev = pl.BlockSpec((2, 16, tc), lambda p, j, i: (p, jnp.maximum(i * (tr // 16) - 1, 0), j))
    w3 = pl.BlockSpec((2, 3, tc), lambda p, j, i: (p, 0, j))
    w1 = pl.BlockSpec((2, 1, tc), lambda p, j, i: (p, 0, j))
    return blk, prev, w3, w1


def _conv_gate_fwd(t, cw, cb, *, name):
    _, S, C = t.shape
    tr, tc = min(S, 512), _tile(C, 1536)
    ncol = C // tc
    blk, prev, w3, w1 = _conv_specs(tr, tc)

    def body(t_ref, prev_ref, w_ref, b_ref, a_ref):
        i = pl.program_id(2)
        u1 = _conv_u(t_ref, prev_ref, w_ref, b_ref, 0, i)[0]
        u2 = _conv_u(t_ref, prev_ref, w_ref, b_ref, 1, i)[0]
        a_ref[...] = (jax.nn.silu(u1) * u2).astype(BF16)

    return pl.pallas_call(
        body, name=name, out_shape=jax.ShapeDtypeStruct((S, 2 * C), BF16), grid=(2, ncol, S // tr),
        in_specs=[blk, prev, w3, w1], out_specs=pl.BlockSpec((tr, tc), lambda p, j, i: (i, p * ncol + j)),
        compiler_params=_params("parallel", "parallel", "parallel"),
    )(t, t, cw, cb)


def _conv_gate_bwd(t, da, cw, cb, *, name):
    _, S, C = t.shape
    tr, tc = min(S, 256), _tile(C, 1536)
    ncol = C // tc
    blk, prev, w3, w1 = _conv_specs(tr, tc)

    def body(t_ref, prev_ref, da_ref, w_ref, b_ref, du_ref, dw_ref, db_ref):
        i = pl.program_id(2)
        first = i == 0
        u1, c1, a1, b1 = _conv_u(t_ref, prev_ref, w_ref, b_ref, 0, i)
        u2, c2, a2, b2 = _conv_u(t_ref, prev_ref, w_ref, b_ref, 1, i)
        d = da_ref[...].astype(F32)
        sg = jax.nn.sigmoid(u1)
        du1 = d * u2 * (sg * (1.0 + u1 * (1.0 - sg)))
        du2 = d * (u1 * sg)
        for m, (du, cur, t1, t2) in enumerate(((du1, c1, a1, b1), (du2, c2, a2, b2))):
            du_ref[m] = du.astype(BF16)
            dw = jnp.concatenate([jnp.sum(du * t2, axis=0, keepdims=True), jnp.sum(du * t1, axis=0, keepdims=True),
                                  jnp.sum(du * cur, axis=0, keepdims=True)], axis=0)
            db = jnp.sum(du, axis=0, keepdims=True)

            @pl.when(first)
            def _():
                dw_ref[m] = dw
                db_ref[m] = db

            @pl.when(jnp.logical_not(first))
            def _():
                dw_ref[m] += dw
                db_ref[m] += db

    return pl.pallas_call(
        body, name=name,
        out_shape=(jax.ShapeDtypeStruct(t.shape, BF16), jax.ShapeDtypeStruct(cw.shape, F32), jax.ShapeDtypeStruct(cb.shape, F32)),
        grid=(2, ncol, S // tr),
        in_specs=[blk, prev, pl.BlockSpec((tr, tc), lambda p, j, i: (i, p * ncol + j)), w3, w1], out_specs=(blk, w3, w1),
        compiler_params=_params("parallel", "parallel", "arbitrary"),
    )(t, t, da, cw, cb)


def _conv_bwd_dt(du, cw, *, name):
    _, S, C = du.shape
    tr, tc = min(S, 512), _tile(C, 1536)
    nrow = S // tr
    blk, _, w3, _ = _conv_specs(tr, tc)
    nxt = pl.BlockSpec((2, 16, tc), lambda p, j, i: (p, jnp.minimum((i + 1) * (tr // 16), S // 16 - 1), j))

    def body(d_ref, next_ref, w_ref, dt_ref):
        i = pl.program_id(2)
        live = (i < nrow - 1).astype(F32)
        for m in range(2):
            cur = d_ref[m].astype(F32)
            n0 = next_ref[m, 0:1, :].astype(F32) * live
            n1 = next_ref[m, 1:2, :].astype(F32) * live
            row = lax.broadcasted_iota(jnp.int32, cur.shape, 0)
            d1 = jnp.where(row == tr - 1, n0, pltpu.roll(cur, tr - 1, 0))
            d2 = jnp.where(row == tr - 1, n1, jnp.where(row == tr - 2, n0, pltpu.roll(cur, tr - 2, 0)))
            dt_ref[m] = ((w_ref[m, 2:3, :] * cur + w_ref[m, 1:2, :] * d1) + w_ref[m, 0:1, :] * d2).astype(BF16)

    return pl.pallas_call(
        body, name=name, out_shape=jax.ShapeDtypeStruct(du.shape, BF16), grid=(2, C // tc, nrow),
        in_specs=[blk, nxt, w3], out_specs=blk, compiler_params=_params("parallel", "parallel", "parallel"),
    )(du, du, cw)


def _ada_fwd(c_all, w, b, *, name):
    Bn, D = c_all.shape
    N = w.shape[1]
    tn = _tile(N, 512)

    def body(c_ref, w_ref, b_ref, o_ref):
        o_ref[...] = jnp.dot(jax.nn.silu(c_ref[...]), w_ref[...], preferred_element_type=F32, precision=HIGHEST) + b_ref[...]

    return pl.pallas_call(
        body, name=name, out_shape=jax.ShapeDtypeStruct((Bn, N), F32), grid=(N // tn,),
        in_specs=[pl.BlockSpec((Bn, D), lambda j: (0, 0)), pl.BlockSpec((D, tn), lambda j: (0, j)),
                  pl.BlockSpec((1, tn), lambda j: (0, j))],
        out_specs=pl.BlockSpec((Bn, tn), lambda j: (0, j)), compiler_params=_params("parallel"),
    )(c_all, w, b)


def _ada_bwd(c_all_t, dmod, *, name):
    D, Bn = c_all_t.shape
    N = dmod.shape[1]
    tm = _tile(D, 512, 8)
    tn = _tile(N, 1536)

    def body(c_ref, d_ref, o_ref):
        o_ref[...] = jnp.dot(jax.nn.silu(c_ref[...]), d_ref[...], preferred_element_type=F32, precision=HIGHEST)

    return pl.pallas_call(
        body, name=name, out_shape=jax.ShapeDtypeStruct((D, N), F32), grid=(D // tm, N // tn),
        in_specs=[pl.BlockSpec((tm, Bn), lambda i, j: (i, 0)), pl.BlockSpec((Bn, tn), lambda i, j: (0, j))],
        out_specs=pl.BlockSpec((tm, tn), lambda i, j: (i, j)), compiler_params=_params("parallel", "parallel"),
    )(c_all_t, dmod)


def _adamw(w, g, m, v, *, name):
    R, C = w.shape
    tr = R if R * C <= (1 << 19) else _tile(R, max(8, (1 << 19) // C), 8)

    def body(w_ref, g_ref, m_ref, v_ref, d_ref, nm_ref, nv_ref):
        gv = g_ref[...]
        nm = ADAM_B1 * m_ref[...] + (1.0 - ADAM_B1) * gv
        nv = ADAM_B2 * v_ref[...] + (1.0 - ADAM_B2) * (gv * gv)
        m_hat = nm / (1.0 - ADAM_B1 ** ADAM_STEP)
        v_hat = nv / (1.0 - ADAM_B2 ** ADAM_STEP)
        d_ref[...] = -ADAM_LR * (m_hat / (jnp.sqrt(v_hat) + ADAM_EPS) + ADAM_WD * w_ref[...])
        nm_ref[...] = nm
        nv_ref[...] = nv

    blk = pl.BlockSpec((tr, C), lambda i: (i, 0))
    shp = jax.ShapeDtypeStruct((R, C), F32)
    return pl.pallas_call(
        body, name=name, out_shape=(shp, shp, shp), grid=(R // tr,), in_specs=[blk] * 4, out_specs=(blk,) * 3,
        compiler_params=_params("parallel"),
    )(w, g, m, v)


def _place():
    x, y, c = lax.axis_index("x"), lax.axis_index("y"), lax.axis_index("c")
    return x, y, c, [(1 - x, y), (x, 1 - y), (1 - x, 1 - y)]


def _remote(src, dst, send_sem, recv_sem, dev):
    return pltpu.make_async_remote_copy(src_ref=src, dst_ref=dst, send_sem=send_sem, recv_sem=recv_sem,
                                        device_id=dev, device_id_type=MESH)


def _allgather8(v, *, tie=None, name):
    R, C = v.shape

    def body(v_ref, out_ref, send_sems, recv_sems, local_sem):
        x, y, c, chips = _place()
        me, sibling = (x, y, c), (x, y, 1 - c)

        def rows(px, py, pc):
            return out_ref.at[pl.ds((4 * px + 2 * py + pc) * R, R), :]

        def copy(k, block, to, src=None):
            return _remote(rows(*block) if src is None else src, rows(*block), send_sems.at[k], recv_sems.at[k], to)

        mine = pltpu.make_async_copy(v_ref, rows(*me), local_sem)
        mine.start()
        first = [copy(0, me, sibling, src=v_ref)]
        first += [copy(1 + j, me, (*chip, c), src=v_ref) for j, chip in enumerate(chips)]
        for cp in first:
            cp.start()
        passed = [copy(4 + j, (*chip, c), sibling) for j, chip in enumerate(chips)]
        for j, chip in enumerate(chips):
            copy(1 + j, (*chip, c), me).wait_recv()
            passed[j].start()
        copy(0, sibling, me).wait_recv()
        for j, chip in enumerate(chips):
            copy(4 + j, (*chip, 1 - c), me).wait_recv()
        for cp in first + passed:
            cp.wait_send()
        mine.wait()

    body, tspec, targ = _tied(body, tie)
    out = pl.pallas_call(
        body, name=name, out_shape=jax.ShapeDtypeStruct((N_DEV * R, C), v.dtype),
        in_specs=tspec + [VMEM_FULL], out_specs=VMEM_FULL,
        scratch_shapes=[pltpu.SemaphoreType.DMA((7,)), pltpu.SemaphoreType.DMA((7,)), pltpu.SemaphoreType.DMA],
    )(*targ, v)
    return out.reshape(N_DEV, R, C)


SEM = pl.BlockSpec(memory_space=pltpu.SEMAPHORE)
HBM = pl.BlockSpec(memory_space=pltpu.HBM)
EFFECT = pltpu.SideEffectType.DATAFLOW_SIDE_EFFECTING
DMA_SEM = pltpu.SemaphoreType.DMA(())


def _in_hbm(a):
    return pltpu.with_memory_space_constraint(a, pltpu.HBM)


def _three_halves(land, r2):
    return land.at[pl.ds(0, N_CHIP - 1), pl.ds(0, r2)]


def _slot(chip, swap):
    return (chip % 2) * 2 + chip // 2 if swap else chip


def _gather_start(ws, after, swaps, *, name):
    n = len(ws)
    na = len(after)
    lands = [lax.empty((N_CHIP,) + w.shape, w.dtype) for w in ws]

    def body(*refs):
        w_refs, land_refs = refs[:n], refs[n:2 * n]
        send, recv = refs[2 * n + na:3 * n + na], refs[3 * n + na:4 * n + na]
        token = refs[6 * n + na]
        x, y, c, chips = _place()
        k = 2 * x + y
        for i in range(n):
            r2 = ws[i].shape[0] // 2
            for cx, cy in chips:
                _remote(w_refs[i].at[pl.ds(c * r2, r2)], land_refs[i].at[_slot(k, swaps[i]), pl.ds(c * r2, r2)], send[i], recv[i],
                        (cx, cy, c)).start()
        token[...] = jnp.zeros_like(token)

    outs = pl.pallas_call(
        body, name=name,
        out_shape=[DMA_SEM] * (2 * n) + [pltpu.HBM(w.shape, w.dtype) for w in ws] + [pltpu.HBM(l.shape, l.dtype) for l in lands]
        + [jax.ShapeDtypeStruct((8, LANES), F32)],
        in_specs=[HBM] * (2 * n) + [ANY] * na, out_specs=[SEM] * (2 * n) + [HBM] * (2 * n) + [VMEM_FULL],
        input_output_aliases={i: 2 * n + i for i in range(2 * n)},
        compiler_params=pltpu.CompilerParams(has_side_effects=EFFECT),
    )(*[_in_hbm(w) for w in ws], *[_in_hbm(l) for l in lands], *after)
    return outs[:n], outs[n:2 * n], outs[2 * n:3 * n], outs[3 * n:4 * n], outs[4 * n]


def _gather_forward(send, recv, ws, lands, after, swaps, *, name):
    n = len(ws)

    def body(*refs):
        w_refs, land_refs = refs[:n], refs[n:2 * n]
        send1, recv1 = refs[2 * n:3 * n], refs[3 * n:4 * n]
        send2, recv2 = refs[4 * n + 1 + 2 * n:4 * n + 1 + 3 * n], refs[4 * n + 1 + 3 * n:4 * n + 1 + 4 * n]
        x, y, c, chips = _place()
        sibling = (x, y, 1 - c)
        for i in range(n):
            r2 = ws[i].shape[0] // 2
            win = _three_halves(land_refs[i], r2)
            done = _remote(win, win, send1[i], recv1[i], sibling)
            done.wait_send()
            done.wait_recv()
            for cx, cy in chips:
                got = land_refs[i].at[_slot(2 * cx + cy, swaps[i]), pl.ds(c * r2, r2)]
                _remote(got, got, send2[i], recv2[i], sibling).start()
        token = refs[8 * n + 1]
        token[...] = jnp.zeros_like(token)

    outs = pl.pallas_call(
        body, name=name,
        out_shape=[pltpu.HBM(w.shape, w.dtype) for w in ws] + [pltpu.HBM(l.shape, l.dtype) for l in lands] + [DMA_SEM] * (2 * n)
        + [jax.ShapeDtypeStruct((8, LANES), F32)],
        in_specs=[HBM] * (2 * n) + [SEM] * (2 * n) + [ANY], out_specs=[HBM] * (2 * n) + [SEM] * (2 * n) + [VMEM_FULL],
        input_output_aliases={i: i for i in range(2 * n)},
        compiler_params=pltpu.CompilerParams(has_side_effects=EFFECT),
    )(*ws, *lands, *send, *recv, after)
    return outs[2 * n:3 * n], outs[3 * n:4 * n], outs[n:2 * n], outs[4 * n]


def _gather_finish(send, recv, lands, after, *, name):
    n = len(lands)

    def body(*refs):
        land_refs = refs[:n]
        send2, recv2 = refs[n:2 * n], refs[2 * n:3 * n]
        x, y, c, _ = _place()
        for i in range(n):
            win = _three_halves(land_refs[i], lands[i].shape[1] // 2)
            done = _remote(win, win, send2[i], recv2[i], (x, y, 1 - c))
            done.wait_send()
            done.wait_recv()

    return pl.pallas_call(
        body, name=name,
        out_shape=[pltpu.HBM(l.shape, l.dtype) for l in lands],
        in_specs=[HBM] * n + [SEM] * (2 * n) + [ANY], out_specs=[HBM] * n,
        input_output_aliases={i: i for i in range(n)},
        compiler_params=pltpu.CompilerParams(has_side_effects=EFFECT),
    )(*lands, *send, *recv, after)


def _scatter_start(gs, swaps, *, name):
    n = len(gs)
    lands = [lax.empty((N_DEV, g.shape[1] // 2, g.shape[2]), g.dtype) for g in gs]

    def body(*refs):
        g_refs, land_refs = refs[:n], refs[n:2 * n]
        send, recv = refs[2 * n:3 * n], refs[3 * n:4 * n]
        token = refs[6 * n]
        x, y, c, chips = _place()
        k = 2 * x + y
        me = 2 * k + c
        for i in range(n):
            r2 = gs[i].shape[1] // 2
            for cx, cy in chips:
                for cc in range(2):
                    _remote(g_refs[i].at[_slot(2 * cx + cy, swaps[i]), pl.ds(cc * r2, r2)], land_refs[i].at[me], send[i], recv[i],
                            (cx, cy, cc)).start()
            _remote(g_refs[i].at[_slot(k, swaps[i]), pl.ds((1 - c) * r2, r2)], land_refs[i].at[me], send[i], recv[i],
                    (x, y, 1 - c)).start()
        token[...] = jnp.zeros_like(token)

    outs = pl.pallas_call(
        body, name=name,
        out_shape=[DMA_SEM] * (2 * n) + [pltpu.HBM(g.shape, g.dtype) for g in gs] + [pltpu.HBM(l.shape, l.dtype) for l in lands]
        + [jax.ShapeDtypeStruct((8, LANES), F32)],
        in_specs=[HBM] * (2 * n), out_specs=[SEM] * (2 * n) + [HBM] * (2 * n) + [VMEM_FULL],
        input_output_aliases={i: 2 * n + i for i in range(2 * n)},
        compiler_params=pltpu.CompilerParams(has_side_effects=EFFECT),
    )(*[_in_hbm(g) for g in gs], *[_in_hbm(l) for l in lands])
    return outs[:n], outs[n:2 * n], outs[2 * n:3 * n], outs[3 * n:4 * n], outs[4 * n]


def _scatter_wait(send, recv, gs, lands, after, *, name):
    n = len(gs)

    def body(*refs):
        land_refs = refs[n:2 * n]
        send1, recv1 = refs[2 * n:3 * n], refs[3 * n:4 * n]
        x, y, c, _ = _place()
        for i in range(n):
            win = land_refs[i].at[pl.ds(0, N_DEV - 1)]
            done = _remote(win, win, send1[i], recv1[i], (x, y, 1 - c))
            done.wait_send()
            done.wait_recv()

    outs = pl.pallas_call(
        body, name=name,
        out_shape=[pltpu.HBM(g.shape, g.dtype) for g in gs] + [pltpu.HBM(l.shape, l.dtype) for l in lands],
        in_specs=[HBM] * (2 * n) + [SEM] * (2 * n) + [ANY], out_specs=[HBM] * (2 * n),
        input_output_aliases={i: i for i in range(2 * n)},
        compiler_params=pltpu.CompilerParams(has_side_effects=EFFECT),
    )(*gs, *lands, *send, *recv, after)
    return outs[:n], outs[n:]


def _share_halves(ts, *, name):
    n = len(ts)

    def body(*refs):
        outs = refs[n:2 * n]
        send_sems, recv_sems = refs[2 * n:]
        x, y, c, _ = _place()
        sibling = (x, y, 1 - c)
        cps = []
        for i in range(n):
            r2 = ts[i].shape[0] // 2
            mine = outs[i].at[pl.ds(c * r2, r2)]
            cps.append(_remote(mine, mine, send_sems.at[i], recv_sems.at[i], sibling))
            cps[-1].start()
        for i in range(n):
            r2 = ts[i].shape[0] // 2
            got = outs[i].at[pl.ds((1 - c) * r2, r2)]
            _remote(got, got, send_sems.at[i], recv_sems.at[i], sibling).wait_recv()
        for cp in cps:
            cp.wait_send()

    return pl.pallas_call(
        body, name=name,
        out_shape=[jax.ShapeDtypeStruct(t.shape, t.dtype) for t in ts],
        in_specs=[ANY] * n, out_specs=[ANY] * n, input_output_aliases={i: i for i in range(n)},
        scratch_shapes=[pltpu.SemaphoreType.DMA((n,)), pltpu.SemaphoreType.DMA((n,))],
    )(*ts)


def _sum_pieces(land, g, idx, *, name):
    _, r2, C = land.shape
    tr = _tile(r2, max(16, (1 << 20) // C), 16)
    nr = r2 // tr

    def body(idx_ref, land_ref, own_ref, o_ref, acc_ref):
        d = pl.program_id(1)
        mine = d == idx_ref[0]

        @pl.when(d == 0)
        def _():
            acc_ref[...] = jnp.zeros_like(acc_ref)

        @pl.when(mine)
        def _():
            acc_ref[...] += own_ref[...].astype(F32)

        @pl.when(jnp.logical_not(mine))
        def _():
            acc_ref[...] += land_ref[...].astype(F32)

        @pl.when(d == N_DEV - 1)
        def _():
            o_ref[...] = acc_ref[...]

    return pl.pallas_call(
        body, name=name, out_shape=jax.ShapeDtypeStruct((2 * r2, C), F32),
        grid_spec=pltpu.PrefetchScalarGridSpec(
            num_scalar_prefetch=1, grid=(nr, N_DEV),
            in_specs=[pl.BlockSpec((None, tr, C), lambda i, d, ix: (jnp.where(d == ix[0], (d + 1) % N_DEV, d), i, 0)),
                      pl.BlockSpec((None, tr, C), lambda i, d, ix: (ix[1], ix[2] * nr + i, 0))],
            out_specs=pl.BlockSpec((tr, C), lambda i, d, ix: (ix[2] * nr + i, 0)),
            scratch_shapes=[pltpu.VMEM((tr, C), F32)]),
        compiler_params=_params("parallel", "arbitrary"),
    )(idx, land, g)


def _sum_devices(v, *, name):
    n, R, C = v.shape

    def body(v_ref, o_ref):
        acc = v_ref[0]
        for j in range(1, n):
            acc = acc + v_ref[j]
        o_ref[...] = acc

    return pl.pallas_call(body, name=name, out_shape=jax.ShapeDtypeStruct((R, C), F32),
                          in_specs=[VMEM_FULL], out_specs=VMEM_FULL)(v)


def _shard_cols(shards, lo, hi, width):
    out = []
    while lo < hi:
        j = lo // width
        end = min(hi, (j + 1) * width)
        out.append(shards[j][:, lo - j * width:end - j * width])
        lo = end
    return out


def _from_col_shards(g):
    return jnp.transpose(g, (1, 0, 2)).reshape(g.shape[1], N_CHIP * g.shape[2])


def _to_col_shards(w):
    R, N = w.shape
    return jnp.transpose(w.reshape(R, N_CHIP, N // N_CHIP), (1, 0, 2))


def _split_heads(w, widths):
    R, N = w.shape
    per = sum(widths)
    w3 = w.reshape(R, N // per, per)
    lo = w3[:, :, :widths[0]].reshape(R, -1)
    hi = w3[:, :, widths[0]:].reshape(R, -1)
    return jnp.concatenate([lo, hi], axis=1)


def _merge_heads(w, widths):
    R, N = w.shape
    H = N // sum(widths)
    lo = w[:, :H * widths[0]].reshape(R, H, widths[0])
    hi = w[:, H * widths[0]:].reshape(R, H, widths[1])
    return jnp.concatenate([lo, hi], axis=2).reshape(R, N)


def _t5_bucket(dist):
    max_exact = REL_BUCKETS // 2
    n = jnp.maximum(dist, 0)
    large = max_exact + (jnp.log(jnp.maximum(n, 1).astype(F32) / max_exact)
                         / jnp.log(jnp.asarray(REL_MAX_DIST / max_exact, F32))
                         * (REL_BUCKETS - max_exact)).astype(jnp.int32)
    large = jnp.minimum(large, REL_BUCKETS - 1)
    return jnp.where(n < max_exact, n, large)


def _rel_tables():
    a = jnp.arange(SWA_BLOCK)
    b = jnp.arange(2 * SWA_BLOCK)
    dist = SWA_BLOCK + a[:, None] - b[None, :]
    valid = jnp.logical_and(dist >= 0, dist < SWA_BLOCK)
    onehot = jnp.logical_and(_t5_bucket(dist)[..., None] == jnp.arange(REL_BUCKETS), valid[..., None])
    onehot = onehot.astype(F32).reshape(2 * SWA_BLOCK * SWA_BLOCK, REL_BUCKETS)
    negmask = jnp.where(valid, 0.0, NEG).astype(F32).reshape(1, -1)
    return onehot, negmask


def _rope_tables(S):
    pos = jnp.arange(S, dtype=F32)
    inv = ROPE_THETA ** (-jnp.arange(0, MLA_ROPE, 2, dtype=F32) / MLA_ROPE)
    ang = pos[:, None] * inv[None, :]
    ang = jnp.concatenate([ang, ang, ang, ang], axis=-1)
    return jnp.cos(ang), jnp.sin(ang)


def _flat_pad(parts, rows=8):
    flat = jnp.concatenate([p.reshape(1, -1) for p in parts], axis=1)
    n = flat.shape[1]
    width = -(-n // (rows * LANES)) * LANES
    return jnp.pad(flat, ((0, 0), (0, rows * width - n))).reshape(rows, width)


def _unflat(vec, shapes):
    flat = vec.reshape(-1)
    out, off = [], 0
    for s in shapes:
        n = 1
        for d in s:
            n *= d
        out.append(flat[off:off + n].reshape(s))
        off += n
    return out


def kernel(x, c, w_ada, b_ada, g_pre_mix, g_post_mix, w_in, g_q_lat, w_uq, g_kv_lat, w_ukv, rel_bias, sinks, w_o, g_pre_ffn, g_post_ffn, w_up, conv_w, conv_b, w_down, loss_target, m_w_ada, m_b_ada, m_g_pre_mix, m_g_post_mix, m_w_in, m_g_q_lat, m_w_uq, m_g_kv_lat, m_w_ukv, m_rel_bias, m_sinks, m_w_o, m_g_pre_ffn, m_g_post_ffn, m_w_up, m_conv_w, m_conv_b, m_w_down, v_w_ada, v_b_ada, v_g_pre_mix, v_g_post_mix, v_w_in, v_g_q_lat, v_w_uq, v_g_kv_lat, v_w_ukv, v_rel_bias, v_sinks, v_w_o, v_g_pre_ffn, v_g_post_ffn, v_w_up, v_conv_w, v_conv_b, v_w_down):
    S, D = x.shape[1], x.shape[2]
    Rq, Rkv = g_q_lat.shape[1], g_kv_lat.shape[1]
    H = D // MLA_V
    NH = D // SWA_HD
    KW = SWA_KVH * SWA_HD
    F = w_down.shape[1] * N_CHIP
    xi, yi, ci = lax.axis_index("x"), lax.axis_index("y"), lax.axis_index("c")
    chip = 2 * xi + yi
    me = 2 * chip + ci
    x2, tgt = x[0], loss_target[0]

    c_all = _allgather8(jnp.broadcast_to(c, (8, D)), name="gather_c")[:, 0, :]
    n3 = w_ada.shape[2]
    mod_part = _ada_fwd(c_all, w_ada[0], lax.dynamic_slice(b_ada, (0, chip * n3), (1, n3)), name="ada_fwd")
    mod_all = _allgather8(mod_part, name="gather_mod")
    mod_me = lax.dynamic_index_in_dim(mod_all[0::2], me, axis=1, keepdims=False).reshape(1, 6 * D)
    sh1, sc1, gt1, sh2, sc2, gt2 = [mod_me[:, k * D:(k + 1) * D] for k in range(6)]

    swaps = [False, False, False, False, True, False]
    local = [w_in[0].astype(BF16)]
    send_a, recv_a, srcs_a, lands_a, token = _gather_start(local, (mod_all,), swaps[:1], name="gather_start_in")
    rest, token = lax.optimization_barrier(((w_uq[0], w_ukv[0], w_o[0], w_up[0], w_down[0]), token))
    local += [w.astype(BF16) for w in rest]
    send_b, recv_b, srcs_b, lands_b, token = _gather_start(local[1:], (token,), swaps[1:], name="gather_start_rest")
    send1, recv1, srcs, lands = send_a + send_b, recv_a + recv_b, srcs_a + srcs_b, lands_a + lands_b
    onehot, negmask = _rel_tables()
    npb = NH // (2 * SWA_KVH)
    rb_st = jnp.transpose(rel_bias.T.reshape(SWA_KVH, npb, 2, REL_BUCKETS), (0, 2, 1, 3)).reshape(NH, REL_BUCKETS)
    bias_m = (_matmul(rb_st, onehot.T, tie=token, name="rel_bias_table") + negmask).reshape(
        2 * SWA_KVH, npb * SWA_BLOCK, 2 * SWA_BLOCK)
    h = _modnorm_fwd(x2, g_pre_mix, sc1, sh1, name="pre_mix_norm")

    def whole(land, i):
        return lax.dynamic_update_index_in_dim(land, local[i], _slot(chip, swaps[i]), 0)

    def conv_slots(v):
        return jnp.stack([v[0], v[2], v[1], v[3]])

    s2, r2, l_in, _ = _gather_forward(send1[:1], recv1[:1], srcs[:1], lands[:1], h, swaps[:1], name="gather_forward_in")
    (l_in,) = _gather_finish(s2, r2, l_in, h, name="gather_finish_in")
    gin = whole(l_in, 0)
    o_kr = Rq + Rkv
    o_q = o_kr + MLA_ROPE
    o_g = o_q + NH * SWA_HD + 2 * KW
    n_gate, n_swa = 2 * D, o_g - o_q
    n_lat = -(-(o_q + MLA_ROPE) // PAIR) * PAIR
    runs = []
    for tl in range(D // PAIR):
        runs.append((o_g + tl * PAIR, o_g + (tl + 1) * PAIR, 2 * tl * PAIR))
        runs.append((o_g + D + tl * PAIR, o_g + D + (tl + 1) * PAIR, (2 * tl + 1) * PAIR))
    runs.append((o_q, o_g, n_gate))
    runs.append((0, o_q, n_gate + n_swa))
    csh = gin.shape[2]
    parts = []
    for lo, hi, _ in runs + [(o_kr, o_q, 0)]:
        parts += _shard_cols([gin[j] for j in range(N_CHIP)], lo, hi, csh)
    parts.append(jnp.zeros((D, n_lat - o_q - MLA_ROPE), BF16))
    w_in_all = jnp.concatenate(parts, axis=1)
    cos, sin = _rope_tables(S)
    sink_v = sinks.reshape(NH)

    z_lat = _matmul(h, w_in_all, bcols=(n_gate + n_swa, n_lat), name="in_proj_lat")
    z_swa = _matmul(h, w_in_all, bcols=(n_gate, n_swa), name="in_proj_swa")
    zg = _matmul(h, w_in_all, bcols=(0, n_gate), out_dtype=BF16, name="in_proj_gate")
    s2b, r2b, l_b, _ = _gather_forward(send1[1:4], recv1[1:4], srcs[1:4], lands[1:4], zg, swaps[1:4],
                                       name="gather_forward_attn")
    nq, nkv = _lat_norm_fwd(z_lat, g_q_lat, g_kv_lat, name="lat_norm")
    l_uq, l_ukv, l_o = _gather_finish(s2b, r2b, l_b, nq, name="gather_finish_attn")
    wuq = _split_heads(_from_col_shards(whole(l_uq, 1)), (MLA_NOPE, MLA_ROPE))
    wukv = _split_heads(_from_col_shards(whole(l_ukv, 2)), (MLA_NOPE, MLA_V))
    wo = whole(l_o, 3).reshape(D, D)
    q_raw = _matmul(nq, wuq, out_dtype=BF16, name="uq_proj")
    kv_raw = _matmul(nkv, wukv, out_dtype=BF16, name="ukv_proj")
    qp, kp, vv = _mla_pack_fwd(q_raw, kv_raw, z_lat, cos, sin, o_kr, name="mla_pack")
    o_a, lse = _flash_fwd(qp, kp, vv, name="mla_attn")
    s2c, r2c, l_c, tok_c = _gather_forward(send1[4:], recv1[4:], srcs[4:], lands[4:], o_a, swaps[4:],
                                           name="gather_forward_ffn")
    o_b = _swa_fwd(z_swa, bias_m, sink_v, name="swa_attn")
    mixin = _gate_fwd(zg, o_a, o_b, name="gate_mix")
    mix = _matmul(mixin, wo, tie=tok_c, name="o_proj")
    x1, h2 = _resnorm_modnorm_fwd(x2, mix, g_post_mix, gt1, g_pre_ffn, sc2, sh2, name="post_mix_pre_ffn_norm")
    l_up, l_down = _gather_finish(s2c, r2c, l_c, h2, name="gather_finish_ffn")
    cw_all = _allgather8(jnp.pad(conv_w[0], ((0, 5), (0, 0))), tie=l_down, name="gather_conv_w")[0::2, :3]
    cw = conv_slots(cw_all)
    cb = conv_slots(conv_b.reshape(N_CHIP, 1, -1))
    wup = whole(l_up, 4)
    wdown = whole(l_down, 5).reshape(F, D)
    t = _matmul(h2, wup, out_dtype=BF16, shards="out", name="up_proj")
    a = _conv_gate_fwd(t, cw, cb, name="conv_gate")
    yv = _matmul(a, wdown, name="down_proj")
    dout, dy, dg_post_ffn, dgt2, loss_tile = _resnorm_loss(x1, yv, g_post_ffn, gt2, tgt, name="post_ffn_norm_loss")

    big_params = dict(w_in=(w_in, m_w_in, v_w_in), w_uq=(w_uq, m_w_uq, v_w_uq), w_ukv=(w_ukv, m_w_ukv, v_w_ukv),
                      w_o=(w_o, m_w_o, v_w_o), w_up=(w_up, m_w_up, v_w_up), w_down=(w_down, m_w_down, v_w_down))
    res = {}

    def start(nms, gs):
        sw = [nm == "w_up" for nm in nms]
        send, recv, gsrc, glands, tok = _scatter_start(gs, sw, name="grads_start_" + nms[0])
        return (nms, send, recv, gsrc, glands), tok

    def finish(pendings, after):
        nms, send, recv, gsrc, glands = [sum((list(p[k]) for p in pendings), []) for k in range(5)]
        gsrc, glands = _scatter_wait(send, recv, gsrc, glands, after, name="grads_wait_" + nms[0])
        halves = [_sum_pieces(l, g, jnp.stack([me, _slot(chip, nm == "w_up"), ci]).astype(jnp.int32), name="grad_sum_" + nm)
                  for l, g, nm in zip(glands, gsrc, nms)]
        for nm, g in zip(nms, _share_halves(halves, name="grads_share_" + nms[0])):
            w, m, v = big_params[nm]
            res[nm] = (g,) + tuple(_adamw(w[0], g, m[0], v[0], name="adamw_" + nm))

    dw_down = _matmul(a, dy, ta=True, out_dtype=BF16, name="down_proj_dw")
    p_down, tok = start(["w_down"], [dw_down.reshape(N_CHIP, F // N_CHIP, D)])
    da = _matmul(dy, wdown, tb=True, out_dtype=BF16, tie=tok, name="down_proj_dx")
    du, dcw_s, dcb_s = _conv_gate_bwd(t, da, cw, cb, name="conv_gate_bwd")
    dt = _conv_bwd_dt(du, cw, name="conv_bwd_dt")
    dw_up = _matmul(h2, dt, ta=True, out_dtype=BF16, shards="out", name="up_proj_dw")
    p_up, tok = start(["w_up"], [dw_up])
    dh2 = _matmul(dt, wup, tb=True, tie=tok, shards="k", name="up_proj_dx")
    dx1, dg_pre_ffn, dsc2, dsh2, dmix, dg_post_mix, dgt1 = _modnorm_resnorm_bwd(
        dh2, x1, g_pre_ffn, sc2, dout, mix, g_post_mix, gt1, name="pre_ffn_post_mix_norm_bwd")
    dw_o = _matmul(mixin, dmix, ta=True, out_dtype=BF16, name="o_proj_dw")
    p_o, tok = start(["w_o"], [dw_o.reshape(N_CHIP, D // N_CHIP, D)])
    dmixin = _matmul(dmix, wo, tb=True, out_dtype=BF16, tie=tok, name="o_proj_dx")
    do_a, do_b, dzg = _gate_bwd(dmixin, zg, o_a, o_b, name="gate_mix_bwd")
    dqp, dkp, dvv = _flash_bwd(qp, kp, vv, o_a, do_a, lse, name="mla_attn_bwd")
    dq_raw, dkv_raw, dkr = _mla_pack_bwd(dqp, dkp, dvv, cos, sin, name="mla_pack_bwd")
    dw_uq_p = _matmul(nq, dq_raw, ta=True, out_dtype=BF16, name="uq_proj_dw")
    dw_ukv_p = _matmul(nkv, dkv_raw, ta=True, out_dtype=BF16, name="ukv_proj_dw")
    p_qkv, tok = start(["w_uq", "w_ukv"], [_to_col_shards(_merge_heads(dw_uq_p, (MLA_NOPE, MLA_ROPE))),
                                           _to_col_shards(_merge_heads(dw_ukv_p, (MLA_NOPE, MLA_V)))])
    dnq = _matmul(dq_raw, wuq, tb=True, tie=tok, name="uq_proj_dx")
    dnkv = _matmul(dkv_raw, wukv, tb=True, name="ukv_proj_dx")
    dz_lat, dg_q, dg_kv = _lat_norm_bwd(z_lat, dnq, dnkv, dkr, g_q_lat, g_kv_lat, name="lat_norm_bwd")
    dz_swa, dbias, dsink = _swa_bwd(z_swa, bias_m, sink_v, o_b, do_b, name="swa_attn_bwd")
    dz = jnp.concatenate([dzg, dz_swa, dz_lat], axis=1)
    dw_in_p = _matmul(h, dz, ta=True, out_dtype=BF16, name="in_proj_dw")
    dw_shards = []
    for j in range(N_CHIP):
        cols = []
        for lo, hi, at in sorted(runs):
            a0, a1 = max(lo, j * csh), min(hi, (j + 1) * csh)
            if a0 < a1:
                cols.append(dw_in_p[:, at + a0 - lo:at + a1 - lo])
        dw_shards.append(jnp.concatenate(cols, axis=1))
    p_in, tok = start(["w_in"], [jnp.stack(dw_shards)])
    dh = _matmul(dz, w_in_all, tb=True, tie=tok, name="in_proj_dx")
    grad_x, dg_pre_mix, dsc1, dsh1 = _modnorm_bwd(dh, x2, g_pre_mix, sc1, dx1, name="pre_mix_norm_bwd")
    drel_st = _matmul(dbias.reshape(NH, -1), onehot, tie=grad_x, name="rel_bias_bwd")
    finish((p_down, p_up, p_o, p_qkv), drel_st)
    drel = jnp.transpose(drel_st.reshape(SWA_KVH, 2, npb, REL_BUCKETS), (0, 2, 1, 3)).reshape(NH, REL_BUCKETS).T

    dcw = _from_col_shards(conv_slots(dcw_s))
    dcb = conv_slots(dcb_s).reshape(1, -1)
    dmod = jnp.concatenate([dsh1, dsc1, dgt1, dsh2, dsc2, dgt2], axis=1)
    small = [dmod, dg_pre_mix, dg_post_mix, dg_pre_ffn, dg_post_ffn, dg_q, dg_kv, drel, dsink[:, :NH], dcb, dcw]
    shapes = [p.shape for p in small]
    done = [res[nm][1] for nm in ("w_down", "w_up", "w_o", "w_uq", "w_ukv")]
    small_all = _allgather8(_flat_pad(small), tie=done, name="gather_small_grads")
    tot = _unflat(_sum_devices(small_all, name="sum_small_grads"), shapes)
    g_b_ada, g_pre_mix_g, g_post_mix_g, g_pre_ffn_g, g_post_ffn_g, g_q_g, g_kv_g, g_rel, g_sinks, g_cb, g_cw_full = tot
    dmod_all = small_all.reshape(N_DEV, -1)[:, :6 * D]
    g_w_ada = _ada_bwd(c_all.T, lax.dynamic_slice(dmod_all, (0, chip * n3), (N_DEV, n3)), name="ada_bwd")
    ncw = conv_w.shape[2]
    g_cw = lax.dynamic_slice(g_cw_full, (0, chip * ncw), (3, ncw))

    res["w_ada"] = (g_w_ada,) + tuple(_adamw(w_ada[0], g_w_ada, m_w_ada[0], v_w_ada[0], name="adamw_w_ada"))
    finish((p_in,), g_w_ada)
    snames = ["b_ada", "g_pre_mix", "g_post_mix", "g_pre_ffn", "g_post_ffn", "g_q_lat", "g_kv_lat", "rel_bias", "sinks",
              "conv_b", "conv_w"]
    sw = [b_ada, g_pre_mix, g_post_mix, g_pre_ffn, g_post_ffn, g_q_lat, g_kv_lat, rel_bias, sinks, conv_b, conv_w]
    sm = [m_b_ada, m_g_pre_mix, m_g_post_mix, m_g_pre_ffn, m_g_post_ffn, m_g_q_lat, m_g_kv_lat, m_rel_bias, m_sinks,
          m_conv_b, m_conv_w]
    sv = [v_b_ada, v_g_pre_mix, v_g_post_mix, v_g_pre_ffn, v_g_post_ffn, v_g_q_lat, v_g_kv_lat, v_rel_bias, v_sinks,
          v_conv_b, v_conv_w]
    sg = [g_b_ada, g_pre_mix_g, g_post_mix_g, g_pre_ffn_g, g_post_ffn_g, g_q_g, g_kv_g, g_rel, g_sinks, g_cb, g_cw]
    sshapes = [w.shape for w in sw]
    sd, snm, snv = _adamw(_flat_pad(sw), _flat_pad(sg), _flat_pad(sm), _flat_pad(sv), name="adamw_small")
    sd, snm, snv = _unflat(sd, sshapes), _unflat(snm, sshapes), _unflat(snv, sshapes)
    for k, nm in enumerate(snames):
        res[nm] = (sg[k].reshape(sshapes[k]), sd[k], snm[k], snv[k])

    order = ["w_ada", "b_ada", "g_pre_mix", "g_post_mix", "w_in", "g_q_lat", "w_uq", "g_kv_lat", "w_ukv", "rel_bias", "sinks",
             "w_o", "g_pre_ffn", "g_post_ffn", "w_up", "conv_w", "conv_b", "w_down"]
    ref_shapes = dict(w_ada=w_ada.shape, w_in=w_in.shape, w_uq=w_uq.shape, w_ukv=w_ukv.shape, w_o=w_o.shape,
                      w_up=w_up.shape, w_down=w_down.shape)
    outs = []
    for k in range(4):
        for nm in order:
            arr = res[nm][k]
            outs.append(arr.reshape(ref_shapes[nm]) if nm in ref_shapes else arr)
    loss = lax.psum(loss_tile[0, 0], ("x", "y", "c"))
    return (loss, grad_x[None], *outs)
```

```python
import math

import jax
import jax.numpy as jnp
from jax import lax
from jax.experimental import pallas as pl
from jax.experimental.pallas import tpu as pltpu

F32 = jnp.float32
BF16 = jnp.bfloat16
MESH = pl.DeviceIdType.MESH
HIGHEST = lax.Precision.HIGHEST

N_DEV = 8
N_CHIP = 4
LANES = 128
MLA_NOPE = 128
MLA_ROPE = 64
MLA_V = 128
MLA_QK = MLA_NOPE + MLA_ROPE
MLA_QK_PAD = 256
ROPE_THETA = 10000.0
SWA_HD = 64
SWA_KVH = 4
SWA_BLOCK = 128
REL_BUCKETS = 32
REL_MAX_DIST = 128
PAIR = 512
EPS = 1e-6
NEG = -1e30
ADAM_LR = 0.001
ADAM_B1 = 0.9
ADAM_B2 = 0.999
ADAM_EPS = 1e-08
ADAM_WD = 0.01
ADAM_STEP = 10

ANY = pl.BlockSpec(memory_space=pl.ANY)
VMEM_FULL = pl.BlockSpec(memory_space=pltpu.VMEM)
SMEM_FULL = pl.BlockSpec(memory_space=pltpu.SMEM)


def _params(*sem):
    return pltpu.CompilerParams(dimension_semantics=sem if sem else None)


def _tied(body, tie):
    if tie is None:
        return body, [], []
    ties = list(tie) if isinstance(tie, (list, tuple)) else [tie]

    def tied_body(*refs):
        body(*refs[len(ties):])

    return tied_body, [ANY] * len(ties), ties


def _tile(n, pref, unit=LANES):
    best = None
    for t in range(unit, min(n, pref) + 1, unit):
        if n % t == 0:
            best = t
    return n if best is None else best


def _matmul(a, b, *, ta=False, tb=False, out_dtype=F32, tie=None, shards=None, bcols=None, name):
    a2 = a.shape[1:] if shards == "k" else a.shape
    b2 = b.shape[1:] if shards else b.shape
    nsh = b.shape[0] if shards else 1
    K, M = a2 if ta else a2[::-1]
    N, K2 = b2 if tb else b2[::-1]
    assert K == K2, (a.shape, b.shape, ta, tb)
    exact = a.dtype == F32
    col0 = 0
    if bcols is not None:
        assert not tb and shards is None
        col0, N = bcols
    tn = _tile(math.gcd(N, col0) if col0 else N, 2048)
    col0 //= tn
    tk = _tile(K, 2048)
    nkc = K // tk
    nk = nkc * (nsh if shards == "k" else 1)
    tm = M if M < 8 else _tile(M, 1024, LANES if ta else 8)
    dn = (((0 if ta else 1,), (1 if tb else 0,)), ((), ()))
    kax = 3 if shards == "out" else 2

    def product(a_ref, b_ref):
        return lax.dot_general(a_ref[...], b_ref[...], dn, preferred_element_type=F32,
                               precision=HIGHEST if exact else None)

    def body_acc(a_ref, b_ref, o_ref, acc_ref):
        k = pl.program_id(kax)

        @pl.when(k == 0)
        def _():
            acc_ref[...] = product(a_ref, b_ref)

        @pl.when(jnp.logical_and(k > 0, k < nk - 1))
        def _():
            acc_ref[...] += product(a_ref, b_ref)

        @pl.when(k == nk - 1)
        def _():
            o_ref[...] = (acc_ref[...] + product(a_ref, b_ref)).astype(o_ref.dtype)

    def body_one(a_ref, b_ref, o_ref):
        o_ref[...] = product(a_ref, b_ref).astype(o_ref.dtype)

    a_blk, b_blk = ((tk, tm) if ta else (tm, tk)), ((tn, tk) if tb else (tk, tn))
    a_at = (lambda i, k: (k, i)) if ta else (lambda i, k: (i, k))
    b_at = (lambda j, k: (j, k)) if tb else (lambda j, k: (k, j + col0))
    if shards == "out":
        grid = (nsh, M // tm, N // tn, nk)
        a_spec = pl.BlockSpec(a_blk, lambda s, i, j, k: a_at(i, k))
        b_spec = pl.BlockSpec((None,) + b_blk, lambda s, i, j, k: (s,) + b_at(j, k))
        o_spec = pl.BlockSpec((None, tm, tn), lambda s, i, j, k: (s, i, j))
        out_shape = jax.ShapeDtypeStruct((nsh, M, N), out_dtype)
        sem = ("parallel", "parallel", "parallel", "arbitrary")
    elif shards == "k":
        grid = (M // tm, N // tn, nk)
        a_spec = pl.BlockSpec((None,) + a_blk, lambda i, j, k: (k // nkc,) + a_at(i, k % nkc))
        b_spec = pl.BlockSpec((None,) + b_blk, lambda i, j, k: (k // nkc,) + b_at(j, k % nkc))
        o_spec = pl.BlockSpec((tm, tn), lambda i, j, k: (i, j))
        out_shape = jax.ShapeDtypeStruct((M, N), out_dtype)
        sem = ("parallel", "parallel", "arbitrary")
    else:
        grid = (M // tm, N // tn, nk)
        a_spec = pl.BlockSpec(a_blk, lambda i, j, k: a_at(i, k))
        b_spec = pl.BlockSpec(b_blk, lambda i, j, k: b_at(j, k))
        o_spec = pl.BlockSpec((tm, tn), lambda i, j, k: (i, j))
        out_shape = jax.ShapeDtypeStruct((M, N), out_dtype)
        sem = ("parallel", "parallel", "arbitrary")
    body, tspec, targ = _tied(body_one if nk == 1 else body_acc, tie)
    return pl.pallas_call(
        body, name=name, out_shape=out_shape, grid=grid, in_specs=tspec + [a_spec, b_spec], out_specs=o_spec,
        scratch_shapes=[] if nk == 1 else [pltpu.VMEM((tm, tn), F32)],
        compiler_params=_params(*sem),
    )(*targ, a, b)


def _row_tile(S, width):
    return _tile(S, max(8, (1 << 19) // width), 8)


def _rstd(x):
    return lax.rsqrt(jnp.mean(x * x, axis=-1, keepdims=True) + EPS)


def _acc_rows(ref, val, first):
    s = jnp.sum(val, axis=0, keepdims=True)

    @pl.when(first)
    def _():
        ref[...] = s

    @pl.when(jnp.logical_not(first))
    def _():
        ref[...] += s


def _modnorm_fwd(x, g, sc, sh, *, name):
    S, D = x.shape
    tr = _row_tile(S, D)

    def body(x_ref, g_ref, sc_ref, sh_ref, h_ref):
        xv = x_ref[...]
        n = (xv * _rstd(xv)) * g_ref[...]
        h_ref[...] = (n * (1.0 + sc_ref[...]) + sh_ref[...]).astype(BF16)

    row = pl.BlockSpec((tr, D), lambda i: (i, 0))
    vec = pl.BlockSpec((1, D), lambda i: (0, 0))
    return pl.pallas_call(
        body, name=name, out_shape=jax.ShapeDtypeStruct((S, D), BF16), grid=(S // tr,),
        in_specs=[row, vec, vec, vec], out_specs=row, compiler_params=_params("parallel"),
    )(x, g, sc, sh)


def _modnorm_bwd(dh, x, g, sc, dres, *, name):
    S, D = x.shape
    tr = _row_tile(S, D)

    def body(dh_ref, x_ref, g_ref, sc_ref, dres_ref, dx_ref, dg_ref, dsc_ref, dsh_ref):
        first = pl.program_id(0) == 0
        xv = x_ref[...]
        dhv = dh_ref[...]
        gv = g_ref[...]
        r = _rstd(xv)
        xhat = xv * r
        _acc_rows(dsh_ref, dhv, first)
        _acc_rows(dsc_ref, dhv * (xhat * gv), first)
        dn = dhv * (1.0 + sc_ref[...])
        _acc_rows(dg_ref, dn * xhat, first)
        dxhat = dn * gv
        proj = jnp.mean(dxhat * xhat, axis=-1, keepdims=True)
        dx_ref[...] = r * (dxhat - xhat * proj) + dres_ref[...]

    row = pl.BlockSpec((tr, D), lambda i: (i, 0))
    vec = pl.BlockSpec((1, D), lambda i: (0, 0))
    vshape = jax.ShapeDtypeStruct((1, D), F32)
    return pl.pallas_call(
        body, name=name,
        out_shape=(jax.ShapeDtypeStruct((S, D), F32), vshape, vshape, vshape), grid=(S // tr,),
        in_specs=[row, row, vec, vec, row], out_specs=(row, vec, vec, vec),
        compiler_params=_params("arbitrary"),
    )(dh, x, g, sc, dres)


def _resnorm_modnorm_fwd(xres, m, g, gt, g2, sc2, sh2, *, name):
    S, D = xres.shape
    tr = _row_tile(S, D)

    def body(x_ref, m_ref, g_ref, gt_ref, g2_ref, sc_ref, sh_ref, o_ref, h_ref):
        mv = m_ref[...]
        x1 = x_ref[...] + gt_ref[...] * ((mv * _rstd(mv)) * g_ref[...])
        o_ref[...] = x1
        n = (x1 * _rstd(x1)) * g2_ref[...]
        h_ref[...] = (n * (1.0 + sc_ref[...]) + sh_ref[...]).astype(BF16)

    row = pl.BlockSpec((tr, D), lambda i: (i, 0))
    vec = pl.BlockSpec((1, D), lambda i: (0, 0))
    return pl.pallas_call(
        body, name=name, out_shape=(jax.ShapeDtypeStruct((S, D), F32), jax.ShapeDtypeStruct((S, D), BF16)), grid=(S // tr,),
        in_specs=[row, row, vec, vec, vec, vec, vec], out_specs=(row, row), compiler_params=_params("parallel"),
    )(xres, m, g, gt, g2, sc2, sh2)


def _modnorm_resnorm_bwd(dh, x, g, sc, dres, m, g1, gt1, *, name):
    S, D = x.shape
    tr = _row_tile(S, D)

    def body(dh_ref, x_ref, g_ref, sc_ref, dres_ref, m_ref, g1_ref, gt1_ref,
             dx_ref, dg_ref, dsc_ref, dsh_ref, dm_ref, dg1_ref, dgt1_ref):
        first = pl.program_id(0) == 0
        xv = x_ref[...]
        dhv = dh_ref[...]
        gv = g_ref[...]
        r = _rstd(xv)
        xhat = xv * r
        _acc_rows(dsh_ref, dhv, first)
        _acc_rows(dsc_ref, dhv * (xhat * gv), first)
        dn = dhv * (1.0 + sc_ref[...])
        _acc_rows(dg_ref, dn * xhat, first)
        dxhat = dn * gv
        proj = jnp.mean(dxhat * xhat, axis=-1, keepdims=True)
        dx1 = r * (dxhat - xhat * proj) + dres_ref[...]
        dx_ref[...] = dx1
        mv = m_ref[...]
        g1v = g1_ref[...]
        r1 = _rstd(mv)
        mhat = mv * r1
        _acc_rows(dgt1_ref, dx1 * (mhat * g1v), first)
        dn1 = dx1 * gt1_ref[...]
        _acc_rows(dg1_ref, dn1 * mhat, first)
        dmhat = dn1 * g1v
        proj1 = jnp.mean(dmhat * mhat, axis=-1, keepdims=True)
        dm_ref[...] = (r1 * (dmhat - mhat * proj1)).astype(BF16)

    row = pl.BlockSpec((tr, D), lambda i: (i, 0))
    vec = pl.BlockSpec((1, D), lambda i: (0, 0))
    vshape = jax.ShapeDtypeStruct((1, D), F32)
    return pl.pallas_call(
        body, name=name,
        out_shape=(jax.ShapeDtypeStruct((S, D), F32), vshape, vshape, vshape, jax.ShapeDtypeStruct((S, D), BF16), vshape, vshape),
        grid=(S // tr,),
        in_specs=[row, row, vec, vec, row, row, vec, vec], out_specs=(row, vec, vec, vec, row, vec, vec),
        compiler_params=_params("arbitrary"),
    )(dh, x, g, sc, dres, m, g1, gt1)


def _resnorm_loss(xres, m, g, gt, target, *, name):
    S, D = xres.shape
    tr = _row_tile(S, D)

    def body(x_ref, m_ref, g_ref, gt_ref, t_ref, d_ref, dm_ref, dg_ref, dgt_ref, loss_ref):
        first = pl.program_id(0) == 0
        mv = m_ref[...]
        gv = g_ref[...]
        r = _rstd(mv)
        mhat = mv * r
        n = mhat * gv
        err = (x_ref[...] + gt_ref[...] * n) - t_ref[...]
        dv = err * (1.0 / D)
        d_ref[...] = dv
        part = 0.5 * jnp.sum(jnp.mean(err * err, axis=-1, keepdims=True), axis=0, keepdims=True)
        part = jnp.broadcast_to(part, loss_ref.shape)

        @pl.when(first)
        def _():
            loss_ref[...] = part

        @pl.when(jnp.logical_not(first))
        def _():
            loss_ref[...] += part

        _acc_rows(dgt_ref, dv * n, first)
        dn = dv * gt_ref[...]
        _acc_rows(dg_ref, dn * mhat, first)
        dmhat = dn * gv
        proj = jnp.mean(dmhat * mhat, axis=-1, keepdims=True)
        dm_ref[...] = (r * (dmhat - mhat * proj)).astype(BF16)

    row = pl.BlockSpec((tr, D), lambda i: (i, 0))
    vec = pl.BlockSpec((1, D), lambda i: (0, 0))
    vshape = jax.ShapeDtypeStruct((1, D), F32)
    return pl.pallas_call(
        body, name=name,
        out_shape=(jax.ShapeDtypeStruct((S, D), F32), jax.ShapeDtypeStruct((S, D), BF16), vshape, vshape,
                   jax.ShapeDtypeStruct((8, LANES), F32)), grid=(S // tr,),
        in_specs=[row, row, vec, vec, row], out_specs=(row, row, vec, vec, pl.BlockSpec((8, LANES), lambda i: (0, 0))),
        compiler_params=_params("arbitrary"),
    )(xres, m, g, gt, target)


def _lat_norm_fwd(z_lat, g_q, g_kv, *, name):
    S, W = z_lat.shape
    Rq, Rkv = g_q.shape[1], g_kv.shape[1]
    tr = _row_tile(S, W)

    def body(z_ref, gq_ref, gkv_ref, nq_ref, nkv_ref):
        cq = z_ref[:, :Rq]
        ckv = z_ref[:, Rq:Rq + Rkv]
        nq_ref[...] = ((cq * _rstd(cq)) * gq_ref[...]).astype(BF16)
        nkv_ref[...] = ((ckv * _rstd(ckv)) * gkv_ref[...]).astype(BF16)

    return pl.pallas_call(
        body, name=name,
        out_shape=(jax.ShapeDtypeStruct((S, Rq), BF16), jax.ShapeDtypeStruct((S, Rkv), BF16)), grid=(S // tr,),
        in_specs=[pl.BlockSpec((tr, W), lambda i: (i, 0)), pl.BlockSpec((1, Rq), lambda i: (0, 0)),
                  pl.BlockSpec((1, Rkv), lambda i: (0, 0))],
        out_specs=(pl.BlockSpec((tr, Rq), lambda i: (i, 0)), pl.BlockSpec((tr, Rkv), lambda i: (i, 0))),
        compiler_params=_params("parallel"),
    )(z_lat, g_q, g_kv)


def _lat_norm_bwd(z_lat, dnq, dnkv, dkr, g_q, g_kv, *, name):
    S, W = z_lat.shape
    Rq, Rkv = g_q.shape[1], g_kv.shape[1]
    tr = _row_tile(S, W)

    def one(c, dn, gv):
        r = _rstd(c)
        chat = c * r
        dchat = dn * gv
        proj = jnp.mean(dchat * chat, axis=-1, keepdims=True)
        return r * (dchat - chat * proj), dn * chat

    def body(z_ref, dnq_ref, dnkv_ref, dkr_ref, gq_ref, gkv_ref, dz_ref, dgq_ref, dgkv_ref):
        first = pl.program_id(0) == 0
        dcq, pq = one(z_ref[:, :Rq], dnq_ref[...], gq_ref[...])
        dckv, pkv = one(z_ref[:, Rq:Rq + Rkv], dnkv_ref[...], gkv_ref[...])
        _acc_rows(dgq_ref, pq, first)
        _acc_rows(dgkv_ref, pkv, first)
        dz_ref[:, :Rq] = dcq.astype(BF16)
        dz_ref[:, Rq:Rq + Rkv] = dckv.astype(BF16)
        dz_ref[:, Rq + Rkv:Rq + Rkv + LANES] = dkr_ref[...].astype(BF16)
        if W > Rq + Rkv + LANES:
            dz_ref[:, Rq + Rkv + LANES:] = jnp.zeros((tr, W - Rq - Rkv - LANES), BF16)

    return pl.pallas_call(
        body, name=name,
        out_shape=(jax.ShapeDtypeStruct((S, W), BF16), jax.ShapeDtypeStruct((1, Rq), F32),
                   jax.ShapeDtypeStruct((1, Rkv), F32)), grid=(S // tr,),
        in_specs=[pl.BlockSpec((tr, W), lambda i: (i, 0)), pl.BlockSpec((tr, Rq), lambda i: (i, 0)),
                  pl.BlockSpec((tr, Rkv), lambda i: (i, 0)), pl.BlockSpec((tr, LANES), lambda i: (i, 0)),
                  pl.BlockSpec((1, Rq), lambda i: (0, 0)), pl.BlockSpec((1, Rkv), lambda i: (0, 0))],
        out_specs=(pl.BlockSpec((tr, W), lambda i: (i, 0)), pl.BlockSpec((1, Rq), lambda i: (0, 0)),
                   pl.BlockSpec((1, Rkv), lambda i: (0, 0))),
        compiler_params=_params("arbitrary"),
    )(z_lat, dnq, dnkv, dkr, g_q, g_kv)


def _rot(x, lo32):
    a = pltpu.roll(x, 32, 1)
    b = pltpu.roll(x, LANES - 32, 1)
    return jnp.where(lo32, -b, a)


def _rot_t(g, lo32):
    a = pltpu.roll(g, 32, 1)
    b = pltpu.roll(g, LANES - 32, 1)
    return jnp.where(lo32, b, -a)


def _mla_pack_fwd(q_raw, kv_raw, z_lat, cos, sin, kr_off, *, name):
    S = q_raw.shape[0]
    H = kv_raw.shape[1] // (MLA_NOPE + MLA_V)
    W = z_lat.shape[1]
    scale = MLA_QK ** -0.5
    tr = min(S, 128)
    nope_w = H * MLA_NOPE

    def body(q_ref, kv_ref, z_ref, cos_ref, sin_ref, qp_ref, kp_ref, v_ref):
        lane = lax.broadcasted_iota(jnp.int32, (tr, LANES), 1)
        lo32 = (lane % 64) < 32
        lo64 = lane < 64
        c = cos_ref[...]
        s = sin_ref[...]
        kr = z_ref[:, kr_off:kr_off + LANES]
        kr = (kr * c + _rot(kr, lo32) * s).astype(BF16)
        for hp in range(H // 2):
            xb = q_ref[:, nope_w + hp * LANES:nope_w + (hp + 1) * LANES].astype(F32)
            rb = (xb * c + _rot(xb, lo32) * s) * scale
            for e in range(2):
                h = 2 * hp + e
                base = h * MLA_QK_PAD
                qp_ref[:, base:base + LANES] = (q_ref[:, h * LANES:(h + 1) * LANES].astype(F32) * scale).astype(BF16)
                keep = lo64 if e == 0 else jnp.logical_not(lo64)
                qp_ref[:, base + LANES:base + 2 * LANES] = jnp.where(keep, rb, 0.0).astype(BF16)
                kp_ref[:, base:base + LANES] = kv_ref[:, h * LANES:(h + 1) * LANES].astype(BF16)
                kp_ref[:, base + LANES:base + 2 * LANES] = kr
        v_ref[...] = kv_ref[:, nope_w:].astype(BF16)

    return pl.pallas_call(
        body, name=name,
        out_shape=(jax.ShapeDtypeStruct((S, H * MLA_QK_PAD), BF16), jax.ShapeDtypeStruct((S, H * MLA_QK_PAD), BF16),
                   jax.ShapeDtypeStruct((S, H * MLA_V), BF16)), grid=(S // tr,),
        in_specs=[pl.BlockSpec((tr, q_raw.shape[1]), lambda i: (i, 0)), pl.BlockSpec((tr, kv_raw.shape[1]), lambda i: (i, 0)),
                  pl.BlockSpec((tr, W), lambda i: (i, 0)), pl.BlockSpec((tr, LANES), lambda i: (i, 0)),
                  pl.BlockSpec((tr, LANES), lambda i: (i, 0))],
        out_specs=(pl.BlockSpec((tr, H * MLA_QK_PAD), lambda i: (i, 0)), pl.BlockSpec((tr, H * MLA_QK_PAD), lambda i: (i, 0)),
                   pl.BlockSpec((tr, H * MLA_V), lambda i: (i, 0))),
        compiler_params=_params("parallel"),
    )(q_raw, kv_raw, z_lat, cos, sin)


def _mla_pack_bwd(dqp, dkp, dv, cos, sin, *, name):
    S = dqp.shape[0]
    H = dv.shape[1] // MLA_V
    scale = MLA_QK ** -0.5
    tr = min(S, 128)
    nope_w = H * MLA_NOPE

    def body(dqp_ref, dkp_ref, dv_ref, cos_ref, sin_ref, dq_ref, dkv_ref, dkr_ref):
        lane = lax.broadcasted_iota(jnp.int32, (tr, LANES), 1)
        lo32 = (lane % 64) < 32
        lo64 = lane < 64
        c = cos_ref[...]
        s = sin_ref[...]
        dkr2 = jnp.zeros((tr, LANES), F32)
        for hp in range(H // 2):
            be = (2 * hp) * MLA_QK_PAD
            bo = (2 * hp + 1) * MLA_QK_PAD
            g = jnp.where(lo64, dqp_ref[:, be + LANES:be + 2 * LANES].astype(F32),
                          dqp_ref[:, bo + LANES:bo + 2 * LANES].astype(F32)) * scale
            dq_ref[:, nope_w + hp * LANES:nope_w + (hp + 1) * LANES] = (g * c + _rot_t(g * s, lo32)).astype(BF16)
            for h, base in ((2 * hp, be), (2 * hp + 1, bo)):
                dq_ref[:, h * LANES:(h + 1) * LANES] = (dqp_ref[:, base:base + LANES].astype(F32) * scale).astype(BF16)
                dkv_ref[:, h * LANES:(h + 1) * LANES] = dkp_ref[:, base:base + LANES].astype(BF16)
                dkr2 = dkr2 + dkp_ref[:, base + LANES:base + 2 * LANES].astype(F32)
        dkr2 = dkr2 * c + _rot_t(dkr2 * s, lo32)
        dkr2 = dkr2 + pltpu.roll(dkr2, 64, 1)
        dkr_ref[...] = jnp.where(lo64, dkr2, 0.0)
        dkv_ref[:, nope_w:] = dv_ref[...].astype(BF16)

    return pl.pallas_call(
        body, name=name,
        out_shape=(jax.ShapeDtypeStruct((S, nope_w + H * MLA_ROPE), BF16), jax.ShapeDtypeStruct((S, 2 * nope_w), BF16),
                   jax.ShapeDtypeStruct((S, LANES), F32)), grid=(S // tr,),
        in_specs=[pl.BlockSpec((tr, H * MLA_QK_PAD), lambda i: (i, 0)), pl.BlockSpec((tr, H * MLA_QK_PAD), lambda i: (i, 0)),
                  pl.BlockSpec((tr, H * MLA_V), lambda i: (i, 0)), pl.BlockSpec((tr, LANES), lambda i: (i, 0)),
                  pl.BlockSpec((tr, LANES), lambda i: (i, 0))],
        out_specs=(pl.BlockSpec((tr, nope_w + H * MLA_ROPE), lambda i: (i, 0)), pl.BlockSpec((tr, 2 * nope_w), lambda i: (i, 0)),
                   pl.BlockSpec((tr, LANES), lambda i: (i, 0))),
        compiler_params=_params("parallel"),
    )(dqp, dkp, dv, cos, sin)


FLASH_HB = 2


def _causal_pairs(nb):
    qi = [i for i in range(nb) for j in range(i + 1)]
    kj = [j for i in range(nb) for j in range(i + 1)]
    return jnp.asarray(qi, jnp.int32), jnp.asarray(kj, jnp.int32)


def _scores(q, k, diagonal, t):
    s = lax.dot_general(q, k, (((1,), (1,)), ((), ())), preferred_element_type=F32)
    if diagonal:
        row = lax.broadcasted_iota(jnp.int32, (t, t), 0)
        col = lax.broadcasted_iota(jnp.int32, (t, t), 1)
        s = jnp.where(col <= row, s, NEG)
    return s


def _flash_fwd(qp, kp, v, *, name):
    S = qp.shape[0]
    H = v.shape[1] // MLA_V
    t = min(S, 512)
    nb = S // t
    HB = 2 * FLASH_HB
    qi, kj = _causal_pairs(nb)
    QW, VW = MLA_QK_PAD, MLA_V

    def body(qi_ref, kj_ref, q_ref, k_ref, v_ref, o_ref, lse_ref, m_s, l_s, acc_s):
        pr = pl.program_id(1)
        i = qi_ref[pr]
        j = kj_ref[pr]

        @pl.when(j == 0)
        def _():
            m_s[...] = jnp.full_like(m_s, NEG)
            l_s[...] = jnp.zeros_like(l_s)
            acc_s[...] = jnp.zeros_like(acc_s)

        def step(diagonal):
            state = [(m_s[hh], l_s[hh], acc_s[hh]) for hh in range(HB)]
            new = []
            for hh, (m_prev, l_prev, acc_prev) in enumerate(state):
                s = _scores(q_ref[:, hh * QW:(hh + 1) * QW], k_ref[:, hh * QW:(hh + 1) * QW], diagonal, t)
                m_cur = jnp.maximum(m_prev, jnp.max(s, axis=1, keepdims=True))
                alpha = jnp.exp(m_prev - m_cur)
                p = jnp.exp(s - m_cur[:, :1])
                l_new = alpha * l_prev + jnp.sum(p, axis=1, keepdims=True)
                acc = alpha * acc_prev + jnp.dot(p.astype(BF16), v_ref[:, hh * VW:(hh + 1) * VW], preferred_element_type=F32)
                new.append((m_cur, l_new, acc))
            for hh, (m_cur, l_new, acc) in enumerate(new):
                if diagonal:
                    o_ref[:, hh * VW:(hh + 1) * VW] = acc / l_new
                    lse_ref[hh] = m_cur + jnp.log(l_new)
                else:
                    l_s[hh] = l_new
                    acc_s[hh] = acc
                    m_s[hh] = m_cur

        @pl.when(i != j)
        def _():
            step(False)

        @pl.when(i == j)
        def _():
            step(True)

    return pl.pallas_call(
        body, name=name,
        out_shape=(jax.ShapeDtypeStruct((S, H * VW), F32), jax.ShapeDtypeStruct((H, S, LANES), F32)),
        grid_spec=pltpu.PrefetchScalarGridSpec(
            num_scalar_prefetch=2, grid=(H // HB, qi.shape[0]),
            in_specs=[pl.BlockSpec((t, HB * QW), lambda g, p, qi, kj: (qi[p], g)),
                      pl.BlockSpec((t, HB * QW), lambda g, p, qi, kj: (kj[p], g)),
                      pl.BlockSpec((t, HB * VW), lambda g, p, qi, kj: (kj[p], g))],
            out_specs=(pl.BlockSpec((t, HB * VW), lambda g, p, qi, kj: (qi[p], g)),
                       pl.BlockSpec((HB, t, LANES), lambda g, p, qi, kj: (g, qi[p], 0))),
            scratch_shapes=[pltpu.VMEM((HB, t, LANES), F32), pltpu.VMEM((HB, t, LANES), F32), pltpu.VMEM((HB, t, VW), F32)]),
        compiler_params=_params("parallel", "arbitrary"),
    )(qi, kj, qp, kp, v)


def _flash_bwd(qp, kp, v, o, do, lse, *, name):
    S = qp.shape[0]
    H = v.shape[1] // MLA_V
    t = min(S, 512)
    nb = S // t
    HB = 2 * FLASH_HB
    qi = jnp.asarray([i for j in range(nb) for i in range(j, nb)], jnp.int32)
    kj = jnp.asarray([j for j in range(nb) for i in range(j, nb)], jnp.int32)
    npairs = qi.shape[0]
    QW, VW = MLA_QK_PAD, MLA_V
    tn = (((0,), (0,)), ((), ()))
    nt = (((1,), (1,)), ((), ()))

    def body(qi_ref, kj_ref, q_ref, k_ref, v_ref, o_ref, do_ref, lse_ref, dq_ref, dk_ref, dv_ref, dq_s, dk_s, dv_s):
        pr = pl.program_id(1)
        i = qi_ref[pr]
        j = kj_ref[pr]
        rows = pl.ds(pl.multiple_of(i * t, t), t)

        @pl.when(pr == 0)
        def _():
            dq_s[...] = jnp.zeros_like(dq_s)

        @pl.when(i == j)
        def _():
            dk_s[...] = jnp.zeros_like(dk_s)
            dv_s[...] = jnp.zeros_like(dv_s)

        def step(diagonal):
            for hh in range(HB):
                q = q_ref[:, hh * QW:(hh + 1) * QW]
                k = k_ref[:, hh * QW:(hh + 1) * QW]
                dob = do_ref[:, hh * VW:(hh + 1) * VW]
                p = jnp.exp(_scores(q, k, diagonal, t) - lse_ref[hh][:, :1])
                delta = jnp.sum(dob.astype(F32) * o_ref[:, hh * VW:(hh + 1) * VW], axis=1, keepdims=True)
                dp = lax.dot_general(dob, v_ref[:, hh * VW:(hh + 1) * VW], nt, preferred_element_type=F32)
                dsb = (p * (dp - delta)).astype(BF16)
                dv_s[:, hh * VW:(hh + 1) * VW] += lax.dot_general(p.astype(BF16), dob, tn, preferred_element_type=F32)
                dk_s[:, hh * QW:(hh + 1) * QW] += lax.dot_general(dsb, q, tn, preferred_element_type=F32)
                dq_s[rows, hh * QW:(hh + 1) * QW] += jnp.dot(dsb, k, preferred_element_type=F32)

        @pl.when(i != j)
        def _():
            step(False)

        @pl.when(i == j)
        def _():
            step(True)

        @pl.when(i == nb - 1)
        def _():
            dk_ref[...] = dk_s[...].astype(BF16)
            dv_ref[...] = dv_s[...].astype(BF16)

        @pl.when(pr == npairs - 1)
        def _():
            dq_ref[...] = dq_s[...].astype(BF16)

    qside = lambda g, p, qi, kj: (qi[p], g)
    kside = lambda g, p, qi, kj: (kj[p], g)
    whole = lambda g, p, qi, kj: (0, g)
    return pl.pallas_call(
        body, name=name,
        out_shape=(jax.ShapeDtypeStruct((S, H * QW), BF16), jax.ShapeDtypeStruct((S, H * QW), BF16),
                   jax.ShapeDtypeStruct((S, H * VW), BF16)),
        grid_spec=pltpu.PrefetchScalarGridSpec(
            num_scalar_prefetch=2, grid=(H // HB, npairs),
            in_specs=[pl.BlockSpec((t, HB * QW), qside), pl.BlockSpec((t, HB * QW), kside), pl.BlockSpec((t, HB * VW), kside),
                      pl.BlockSpec((t, HB * VW), qside), pl.BlockSpec((t, HB * VW), qside),
                      pl.BlockSpec((HB, t, LANES), lambda g, p, qi, kj: (g, qi[p], 0))],
            out_specs=(pl.BlockSpec((S, HB * QW), whole), pl.BlockSpec((t, HB * QW), kside), pl.BlockSpec((t, HB * VW), kside)),
            scratch_shapes=[pltpu.VMEM((S, HB * QW), F32), pltpu.VMEM((t, HB * QW), F32), pltpu.VMEM((t, HB * VW), F32)]),
        compiler_params=_params("parallel", "arbitrary"),
    )(qi, kj, qp, kp, v, o, do, lse)


def _swa_kv_halves(blk, hf, lo):
    if hf == 0:
        a = jnp.where(lo, blk, 0.0)
        b = pltpu.roll(a, 64, 1)
    else:
        b = jnp.where(lo, 0.0, blk)
        a = pltpu.roll(b, 64, 1)
    return a.astype(BF16), b.astype(BF16)


def _swa_softmax(qs, kx, bias, neg0, sk):
    s = lax.dot_general(qs, kx, (((1,), (1,)), ((), ())), preferred_element_type=F32) + bias + neg0
    m = jnp.maximum(jnp.max(s, axis=1, keepdims=True), sk)
    e = jnp.exp(s - m)
    es = jnp.exp(sk - m)
    inv = 1.0 / (jnp.sum(e, axis=1, keepdims=True) + es)
    return e * inv, es * inv


def _swa_stack(ref, kvh, npb, scale=None):
    parts = [ref[:, (kvh * npb + pb) * LANES:(kvh * npb + pb + 1) * LANES] for pb in range(npb)]
    x = jnp.concatenate(parts, axis=0)
    return x if scale is None else x * scale


def _swa_sink_col(sink_ref, kvh, e, npb):
    row = lax.broadcasted_iota(jnp.int32, (npb * SWA_BLOCK, 1), 0)
    col = jnp.zeros((npb * SWA_BLOCK, 1), F32) + sink_ref[2 * (kvh * npb) + e]
    for pb in range(1, npb):
        col = jnp.where(row >= pb * SWA_BLOCK, sink_ref[2 * (kvh * npb + pb) + e], col)
    return col


def _swa_fwd(z_swa, bias_st, sinks, *, name):
    S, W = z_swa.shape
    npb = bias_st.shape[1] // SWA_BLOCK
    NH = 2 * SWA_KVH * npb
    QW = NH * SWA_HD
    KW = SWA_KVH * SWA_HD
    nb = S // SWA_BLOCK
    B = SWA_BLOCK
    assert SWA_KVH % 2 == 0 and W == QW + 2 * KW

    def body(sink_ref, q_ref, kvc_ref, kvp_ref, b_ref, o_ref):
        n = pl.program_id(0)
        lo = lax.broadcasted_iota(jnp.int32, (2 * B, LANES), 1) < 64
        col = lax.broadcasted_iota(jnp.int32, (npb * B, 2 * B), 1)
        neg0 = jnp.where(jnp.logical_and(col < B, n == 0), NEG, 0.0)
        for kb in range(SWA_KVH // 2):
            kblk = jnp.concatenate([kvp_ref[:, kb * LANES:(kb + 1) * LANES], kvc_ref[:, kb * LANES:(kb + 1) * LANES]], axis=0)
            vblk = jnp.concatenate([kvp_ref[:, KW + kb * LANES:KW + (kb + 1) * LANES],
                                    kvc_ref[:, KW + kb * LANES:KW + (kb + 1) * LANES]], axis=0)
            for hf in range(2):
                kvh = 2 * kb + hf
                ks = _swa_kv_halves(kblk, hf, lo)
                vs = _swa_kv_halves(vblk, hf, lo)
                qs = _swa_stack(q_ref, kvh, npb, SWA_HD ** -0.5).astype(BF16)
                acc = jnp.zeros((npb * B, LANES), F32)
                for e in range(2):
                    p, _ = _swa_softmax(qs, ks[e], b_ref[2 * kvh + e], neg0, _swa_sink_col(sink_ref, kvh, e, npb))
                    acc = acc + jnp.dot(p.astype(BF16), vs[e], preferred_element_type=F32)
                for pb in range(npb):
                    P = kvh * npb + pb
                    o_ref[:, P * LANES:(P + 1) * LANES] = acc[pb * B:(pb + 1) * B]

    kvcol = QW // (2 * KW)
    assert QW % (2 * KW) == 0
    return pl.pallas_call(
        body, name=name,
        out_shape=jax.ShapeDtypeStruct((S, QW), F32), grid=(nb,),
        in_specs=[SMEM_FULL, pl.BlockSpec((B, QW), lambda n: (n, 0)), pl.BlockSpec((B, 2 * KW), lambda n: (n, kvcol)),
                  pl.BlockSpec((B, 2 * KW), lambda n: (jnp.maximum(n - 1, 0), kvcol)),
                  pl.BlockSpec(bias_st.shape, lambda n: (0, 0, 0))],
        out_specs=pl.BlockSpec((B, QW), lambda n: (n, 0)),
        compiler_params=_params("parallel"),
    )(sinks, z_swa, z_swa, z_swa, bias_st)


def _swa_bwd(z_swa, bias_st, sinks, o, do, *, name):
    S, W = z_swa.shape
    npb = bias_st.shape[1] // SWA_BLOCK
    NH = 2 * SWA_KVH * npb
    QW = NH * SWA_HD
    KW = SWA_KVH * SWA_HD
    nb = S // SWA_BLOCK
    B = SWA_BLOCK
    scale = SWA_HD ** -0.5
    tn = (((0,), (0,)), ((), ()))
    nt = (((1,), (1,)), ((), ()))

    def fold(x, hf, lo):
        x = x + pltpu.roll(x, 64, 1)
        return jnp.where(lo, x, 0.0) if hf == 0 else jnp.where(lo, 0.0, x)

    def body(sink_ref, q_ref, kvc_ref, kvp_ref, b_ref, o_ref, do_ref, dz_ref, dbias_ref, dsink_ref,
             cq_s, ck_s, cv_s, nq_s, nk_s, nv_s, pk_s, pv_s):
        n = pl.program_id(0)

        @pl.when(n == 0)
        def _():
            dbias_ref[...] = jnp.zeros_like(dbias_ref)
            dsink_ref[...] = jnp.zeros_like(dsink_ref)
            cq_s[...] = jnp.zeros_like(cq_s)
            ck_s[...] = jnp.zeros_like(ck_s)
            cv_s[...] = jnp.zeros_like(cv_s)

        @pl.when(n == nb)
        def _():
            pk_s[...] = jnp.zeros_like(pk_s)
            pv_s[...] = jnp.zeros_like(pv_s)

        @pl.when(n < nb)
        def _():
            lo = lax.broadcasted_iota(jnp.int32, (2 * B, LANES), 1) < 64
            lo1 = lax.broadcasted_iota(jnp.int32, (npb * B, LANES), 1) < 64
            lane1 = lax.broadcasted_iota(jnp.int32, (1, LANES), 1)
            col = lax.broadcasted_iota(jnp.int32, (npb * B, 2 * B), 1)
            neg0 = jnp.where(jnp.logical_and(col < B, n == 0), NEG, 0.0)
            dsink = jnp.zeros((1, LANES), F32)
            for kb in range(SWA_KVH // 2):
                kblk = jnp.concatenate([kvp_ref[:, kb * LANES:(kb + 1) * LANES], kvc_ref[:, kb * LANES:(kb + 1) * LANES]], axis=0)
                vblk = jnp.concatenate([kvp_ref[:, KW + kb * LANES:KW + (kb + 1) * LANES],
                                        kvc_ref[:, KW + kb * LANES:KW + (kb + 1) * LANES]], axis=0)
                dkblk = jnp.zeros((2 * B, LANES), F32)
                dvblk = jnp.zeros((2 * B, LANES), F32)
                for hf in range(2):
                    kvh = 2 * kb + hf
                    ks = _swa_kv_halves(kblk, hf, lo)
                    vs = _swa_kv_halves(vblk, hf, lo)
                    qs = _swa_stack(q_ref, kvh, npb, scale).astype(BF16)
                    dob = _swa_stack(do_ref, kvh, npb)
                    prod = dob.astype(F32) * _swa_stack(o_ref, kvh, npb)
                    dkj = jnp.zeros((2 * B, LANES), F32)
                    dvj = jnp.zeros((2 * B, LANES), F32)
                    dqs = jnp.zeros((npb * B, LANES), F32)
                    for e in range(2):
                        keep = lo1 if e == 0 else jnp.logical_not(lo1)
                        p, psink = _swa_softmax(qs, ks[e], b_ref[2 * kvh + e], neg0, _swa_sink_col(sink_ref, kvh, e, npb))
                        delta = jnp.sum(jnp.where(keep, prod, 0.0), axis=1, keepdims=True)
                        dp = lax.dot_general(dob, vs[e], nt, preferred_element_type=F32)
                        ds = p * (dp - delta)
                        dbias_ref[2 * kvh + e] += ds
                        pd = psink * delta
                        for pb in range(npb):
                            dsh = -jnp.sum(pd[pb * B:(pb + 1) * B], axis=0, keepdims=True)
                            dsink = dsink + jnp.where(lane1 == 2 * (kvh * npb + pb) + e, dsh, 0.0)
                        dsb = ds.astype(BF16)
                        dqs = dqs + jnp.dot(dsb, ks[e], preferred_element_type=F32)
                        keep2 = lo if e == 0 else jnp.logical_not(lo)
                        dkj = dkj + jnp.where(keep2, lax.dot_general(dsb, qs, tn, preferred_element_type=F32), 0.0)
                        dvj = dvj + jnp.where(keep2, lax.dot_general(p.astype(BF16), dob, tn, preferred_element_type=F32), 0.0)
                    for pb in range(npb):
                        P = kvh * npb + pb
                        nq_s[:, P * LANES:(P + 1) * LANES] = dqs[pb * B:(pb + 1) * B] * scale
                    dkblk = dkblk + fold(dkj, hf, lo)
                    dvblk = dvblk + fold(dvj, hf, lo)
                pk_s[:, kb * LANES:(kb + 1) * LANES] = dkblk[:B]
                nk_s[:, kb * LANES:(kb + 1) * LANES] = dkblk[B:]
                pv_s[:, kb * LANES:(kb + 1) * LANES] = dvblk[:B]
                nv_s[:, kb * LANES:(kb + 1) * LANES] = dvblk[B:]
            dsink_ref[...] += dsink

        dz_ref[:, :QW] = cq_s[...].astype(BF16)
        dz_ref[:, QW:QW + KW] = (ck_s[...] + pk_s[...]).astype(BF16)
        dz_ref[:, QW + KW:] = (cv_s[...] + pv_s[...]).astype(BF16)

        @pl.when(n < nb)
        def _():
            cq_s[...] = nq_s[...]
            ck_s[...] = nk_s[...]
            cv_s[...] = nv_s[...]

    kvcol = QW // (2 * KW)
    cur = lambda n: (jnp.minimum(n, nb - 1), 0)
    return pl.pallas_call(
        body, name=name,
        out_shape=(jax.ShapeDtypeStruct((S, W), BF16), jax.ShapeDtypeStruct(bias_st.shape, F32),
                   jax.ShapeDtypeStruct((1, LANES), F32)),
        grid=(nb + 1,),
        in_specs=[SMEM_FULL, pl.BlockSpec((B, QW), cur), pl.BlockSpec((B, 2 * KW), lambda n: (jnp.minimum(n, nb - 1), kvcol)),
                  pl.BlockSpec((B, 2 * KW), lambda n: (jnp.maximum(jnp.minimum(n, nb - 1) - 1, 0), kvcol)),
                  pl.BlockSpec(bias_st.shape, lambda n: (0, 0, 0)), pl.BlockSpec((B, QW), cur), pl.BlockSpec((B, QW), cur)],
        out_specs=(pl.BlockSpec((B, W), lambda n: (jnp.maximum(n - 1, 0), 0)),
                   pl.BlockSpec(bias_st.shape, lambda n: (0, 0, 0)), pl.BlockSpec((1, LANES), lambda n: (0, 0))),
        scratch_shapes=[pltpu.VMEM((B, QW), F32), pltpu.VMEM((B, KW), F32), pltpu.VMEM((B, KW), F32),
                        pltpu.VMEM((B, QW), F32), pltpu.VMEM((B, KW), F32), pltpu.VMEM((B, KW), F32),
                        pltpu.VMEM((B, KW), F32), pltpu.VMEM((B, KW), F32)],
        compiler_params=_params("arbitrary"),
    )(sinks, z_swa, z_swa, z_swa, bias_st, o, do)


def _gate_fwd(zg, o_a, o_b, *, name):
    S, D = o_a.shape
    tr = min(S, 512)

    def body(z_ref, a_ref, b_ref, m_ref):
        ga = jax.nn.sigmoid(z_ref[:, :PAIR].astype(F32))
        gb = jax.nn.sigmoid(z_ref[:, PAIR:].astype(F32))
        m_ref[...] = (ga * a_ref[...] + gb * b_ref[...]).astype(BF16)

    col = pl.BlockSpec((tr, PAIR), lambda i, j: (i, j))
    return pl.pallas_call(
        body, name=name, out_shape=jax.ShapeDtypeStruct((S, D), BF16), grid=(S // tr, D // PAIR),
        in_specs=[pl.BlockSpec((tr, 2 * PAIR), lambda i, j: (i, j)), col, col], out_specs=col,
        compiler_params=_params("parallel", "parallel"),
    )(zg, o_a, o_b)


def _gate_bwd(dmix, zg, o_a, o_b, *, name):
    S, D = o_a.shape
    tr = min(S, 512)

    def body(d_ref, z_ref, a_ref, b_ref, da_ref, db_ref, dz_ref):
        d = d_ref[...].astype(F32)
        ga = jax.nn.sigmoid(z_ref[:, :PAIR].astype(F32))
        gb = jax.nn.sigmoid(z_ref[:, PAIR:].astype(F32))
        da_ref[...] = (d * ga).astype(BF16)
        db_ref[...] = (d * gb).astype(BF16)
        dz_ref[:, :PAIR] = (d * a_ref[...] * (ga * (1.0 - ga))).astype(BF16)
        dz_ref[:, PAIR:] = (d * b_ref[...] * (gb * (1.0 - gb))).astype(BF16)

    col = pl.BlockSpec((tr, PAIR), lambda i, j: (i, j))
    wide = pl.BlockSpec((tr, 2 * PAIR), lambda i, j: (i, j))
    return pl.pallas_call(
        body, name=name,
        out_shape=(jax.ShapeDtypeStruct((S, D), BF16), jax.ShapeDtypeStruct((S, D), BF16), jax.ShapeDtypeStruct((S, 2 * D), BF16)),
        grid=(S // tr, D // PAIR), in_specs=[col, wide, col, col], out_specs=(col, col, wide),
        compiler_params=_params("parallel", "parallel"),
    )(dmix, zg, o_a, o_b)


def _conv_u(t_ref, prev_ref, w_ref, b_ref, m, i):
    cur = t_ref[m].astype(F32)
    live = (i > 0).astype(F32)
    p6 = prev_ref[m, 14:15, :].astype(F32) * live
    p7 = prev_ref[m, 15:16, :].astype(F32) * live
    row = lax.broadcasted_iota(jnp.int32, cur.shape, 0)
    t1 = jnp.where(row == 0, p7, pltpu.roll(cur, 1, 0))
    t2 = jnp.where(row == 0, p6, jnp.where(row == 1, p7, pltpu.roll(cur, 2, 0)))
    u = ((b_ref[m] + w_ref[m, 0:1, :] * t2) + w_ref[m, 1:2, :] * t1) + w_ref[m, 2:3, :] * cur
    return u, cur, t1, t2


def _conv_specs(tr, tc):
    blk = pl.BlockSpec((2, tr, tc), lambda p, j, i: (p, i, j))
    prev = pl.BlockSpec((2, 16, tc), lambda p, j, i: (p, jnp.maximum(i * (tr // 16) - 1, 0), j))
    w3 = pl.BlockSpec((2, 3, tc), lambda p, j, i: (p, 0, j))
    w1 = pl.BlockSpec((2, 1, tc), lambda p, j, i: (p, 0, j))
    return blk, prev, w3, w1


def _conv_gate_fwd(t, cw, cb, *, name):
    _, S, C = t.shape
    tr, tc = min(S, 512), _tile(C, 1536)
    ncol = C // tc
    blk, prev, w3, w1 = _conv_specs(tr, tc)

    def body(t_ref, prev_ref, w_ref, b_ref, a_ref):
        i = pl.program_id(2)
        u1 = _conv_u(t_ref, prev_ref, w_ref, b_ref, 0, i)[0]
        u2 = _conv_u(t_ref, prev_ref, w_ref, b_ref, 1, i)[0]
        a_ref[...] = (jax.nn.silu(u1) * u2).astype(BF16)

    return pl.pallas_call(
        body, name=name, out_shape=jax.ShapeDtypeStruct((S, 2 * C), BF16), grid=(2, ncol, S // tr),
        in_specs=[blk, prev, w3, w1], out_specs=pl.BlockSpec((tr, tc), lambda p, j, i: (i, p * ncol + j)),
        compiler_params=_params("parallel", "parallel", "parallel"),
    )(t, t, cw, cb)


def _conv_gate_bwd(t, da, cw, cb, *, name):
    _, S, C = t.shape
    tr, tc = min(S, 256), _tile(C, 1536)
    ncol = C // tc
    blk, prev, w3, w1 = _conv_specs(tr, tc)

    def body(t_ref, prev_ref, da_ref, w_ref, b_ref, du_ref, dw_ref, db_ref):
        i = pl.program_id(2)
        first = i == 0
        u1, c1, a1, b1 = _conv_u(t_ref, prev_ref, w_ref, b_ref, 0, i)
        u2, c2, a2, b2 = _conv_u(t_ref, prev_ref, w_ref, b_ref, 1, i)
        d = da_ref[...].astype(F32)
        sg = jax.nn.sigmoid(u1)
        du1 = d * u2 * (sg * (1.0 + u1 * (1.0 - sg)))
        du2 = d * (u1 * sg)
        for m, (du, cur, t1, t2) in enumerate(((du1, c1, a1, b1), (du2, c2, a2, b2))):
            du_ref[m] = du.astype(BF16)
            dw = jnp.concatenate([jnp.sum(du * t2, axis=0, keepdims=True), jnp.sum(du * t1, axis=0, keepdims=True),
                                  jnp.sum(du * cur, axis=0, keepdims=True)], axis=0)
            db = jnp.sum(du, axis=0, keepdims=True)

            @pl.when(first)
            def _():
                dw_ref[m] = dw
                db_ref[m] = db

            @pl.when(jnp.logical_not(first))
            def _():
                dw_ref[m] += dw
                db_ref[m] += db

    return pl.pallas_call(
        body, name=name,
        out_shape=(jax.ShapeDtypeStruct(t.shape, BF16), jax.ShapeDtypeStruct(cw.shape, F32), jax.ShapeDtypeStruct(cb.shape, F32)),
        grid=(2, ncol, S // tr),
        in_specs=[blk, prev, pl.BlockSpec((tr, tc), lambda p, j, i: (i, p * ncol + j)), w3, w1], out_specs=(blk, w3, w1),
        compiler_params=_params("parallel", "parallel", "arbitrary"),
    )(t, t, da, cw, cb)


def _conv_bwd_dt(du, cw, *, name):
    _, S, C = du.shape
    tr, tc = min(S, 512), _tile(C, 1536)
    nrow = S // tr
    blk, _, w3, _ = _conv_specs(tr, tc)
    nxt = pl.BlockSpec((2, 16, tc), lambda p, j, i: (p, jnp.minimum((i + 1) * (tr // 16), S // 16 - 1), j))

    def body(d_ref, next_ref, w_ref, dt_ref):
        i = pl.program_id(2)
        live = (i < nrow - 1).astype(F32)
        for m in range(2):
            cur = d_ref[m].astype(F32)
            n0 = next_ref[m, 0:1, :].astype(F32) * live
            n1 = next_ref[m, 1:2, :].astype(F32) * live
            row = lax.broadcasted_iota(jnp.int32, cur.shape, 0)
            d1 = jnp.where(row == tr - 1, n0, pltpu.roll(cur, tr - 1, 0))
            d2 = jnp.where(row == tr - 1, n1, jnp.where(row == tr - 2, n0, pltpu.roll(cur, tr - 2, 0)))
            dt_ref[m] = ((w_ref[m, 2:3, :] * cur + w_ref[m, 1:2, :] * d1) + w_ref[m, 0:1, :] * d2).astype(BF16)

    return pl.pallas_call(
        body, name=name, out_shape=jax.ShapeDtypeStruct(du.shape, BF16), grid=(2, C // tc, nrow),
        in_specs=[blk, nxt, w3], out_specs=blk, compiler_params=_params("parallel", "parallel", "parallel"),
    )(du, du, cw)


def _ada_fwd(c_all, w, b, *, name):
    Bn, D = c_all.shape
    N = w.shape[1]
    tn = _tile(N, 512)

    def body(c_ref, w_ref, b_ref, o_ref):
        o_ref[...] = jnp.dot(jax.nn.silu(c_ref[...]), w_ref[...], preferred_element_type=F32, precision=HIGHEST) + b_ref[...]

    return pl.pallas_call(
        body, name=name, out_shape=jax.ShapeDtypeStruct((Bn, N), F32), grid=(N // tn,),
        in_specs=[pl.BlockSpec((Bn, D), lambda j: (0, 0)), pl.BlockSpec((D, tn), lambda j: (0, j)),
                  pl.BlockSpec((1, tn), lambda j: (0, j))],
        out_specs=pl.BlockSpec((Bn, tn), lambda j: (0, j)), compiler_params=_params("parallel"),
    )(c_all, w, b)


def _ada_bwd(c_all_t, dmod, *, name):
    D, Bn = c_all_t.shape
    N = dmod.shape[1]
    tm = _tile(D, 512, 8)
    tn = _tile(N, 1536)

    def body(c_ref, d_ref, o_ref):
        o_ref[...] = jnp.dot(jax.nn.silu(c_ref[...]), d_ref[...], preferred_element_type=F32, precision=HIGHEST)

    return pl.pallas_call(
        body, name=name, out_shape=jax.ShapeDtypeStruct((D, N), F32), grid=(D // tm, N // tn),
        in_specs=[pl.BlockSpec((tm, Bn), lambda i, j: (i, 0)), pl.BlockSpec((Bn, tn), lambda i, j: (0, j))],
        out_specs=pl.BlockSpec((tm, tn), lambda i, j: (i, j)), compiler_params=_params("parallel", "parallel"),
    )(c_all_t, dmod)


def _adamw(w, g, m, v, *, name):
    R, C = w.shape
    tr = R if R * C <= (1 << 19) else _tile(R, max(8, (1 << 19) // C), 8)

    def body(w_ref, g_ref, m_ref, v_ref, d_ref, nm_ref, nv_ref):
        gv = g_ref[...]
        nm = ADAM_B1 * m_ref[...] + (1.0 - ADAM_B1) * gv
        nv = ADAM_B2 * v_ref[...] + (1.0 - ADAM_B2) * (gv * gv)
        m_hat = nm / (1.0 - ADAM_B1 ** ADAM_STEP)
        v_hat = nv / (1.0 - ADAM_B2 ** ADAM_STEP)
        d_ref[...] = -ADAM_LR * (m_hat / (jnp.sqrt(v_hat) + ADAM_EPS) + ADAM_WD * w_ref[...])
        nm_ref[...] = nm
        nv_ref[...] = nv

    blk = pl.BlockSpec((tr, C), lambda i: (i, 0))
    shp = jax.ShapeDtypeStruct((R, C), F32)
    return pl.pallas_call(
        body, name=name, out_shape=(shp, shp, shp), grid=(R // tr,), in_specs=[blk] * 4, out_specs=(blk,) * 3,
        compiler_params=_params("parallel"),
    )(w, g, m, v)


def _place():
    x, y, c = lax.axis_index("x"), lax.axis_index("y"), lax.axis_index("c")
    return x, y, c, [(1 - x, y), (x, 1 - y), (1 - x, 1 - y)]


def _remote(src, dst, send_sem, recv_sem, dev):
    return pltpu.make_async_remote_copy(src_ref=src, dst_ref=dst, send_sem=send_sem, recv_sem=recv_sem,
                                        device_id=dev, device_id_type=MESH)


def _allgather8(v, *, tie=None, name):
    R, C = v.shape

    def body(v_ref, out_ref, send_sems, recv_sems, local_sem):
        x, y, c, chips = _place()
        me, sibling = (x, y, c), (x, y, 1 - c)

        def rows(px, py, pc):
            return out_ref.at[pl.ds((4 * px + 2 * py + pc) * R, R), :]

        def copy(k, block, to, src=None):
            return _remote(rows(*block) if src is None else src, rows(*block), send_sems.at[k], recv_sems.at[k], to)

        mine = pltpu.make_async_copy(v_ref, rows(*me), local_sem)
        mine.start()
        first = [copy(0, me, sibling, src=v_ref)]
        first += [copy(1 + j, me, (*chip, c), src=v_ref) for j, chip in enumerate(chips)]
        for cp in first:
            cp.start()
        passed = [copy(4 + j, (*chip, c), sibling) for j, chip in enumerate(chips)]
        for j, chip in enumerate(chips):
            copy(1 + j, (*chip, c), me).wait_recv()
            passed[j].start()
        copy(0, sibling, me).wait_recv()
        for j, chip in enumerate(chips):
            copy(4 + j, (*chip, 1 - c), me).wait_recv()
        for cp in first + passed:
            cp.wait_send()
        mine.wait()

    body, tspec, targ = _tied(body, tie)
    out = pl.pallas_call(
        body, name=name, out_shape=jax.ShapeDtypeStruct((N_DEV * R, C), v.dtype),
        in_specs=tspec + [VMEM_FULL], out_specs=VMEM_FULL,
        scratch_shapes=[pltpu.SemaphoreType.DMA((7,)), pltpu.SemaphoreType.DMA((7,)), pltpu.SemaphoreType.DMA],
    )(*targ, v)
    return out.reshape(N_DEV, R, C)


SEM = pl.BlockSpec(memory_space=pltpu.SEMAPHORE)
HBM = pl.BlockSpec(memory_space=pltpu.HBM)
EFFECT = pltpu.SideEffectType.DATAFLOW_SIDE_EFFECTING
DMA_SEM = pltpu.SemaphoreType.DMA(())


def _in_hbm(a):
    return pltpu.with_memory_space_constraint(a, pltpu.HBM)


def _three_halves(land, r2):
    return land.at[pl.ds(0, N_CHIP - 1), pl.ds(0, r2)]


def _slot(chip, swap):
    return (chip % 2) * 2 + chip // 2 if swap else chip


def _gather_start(ws, after, swaps, *, name):
    n = len(ws)
    na = len(after)
    lands = [lax.empty((N_CHIP,) + w.shape, w.dtype) for w in ws]

    def body(*refs):
        w_refs, land_refs = refs[:n], refs[n:2 * n]
        send, recv = refs[2 * n + na:3 * n + na], refs[3 * n + na:4 * n + na]
        token = refs[6 * n + na]
        x, y, c, chips = _place()
        k = 2 * x + y
        for i in range(n):
            r2 = ws[i].shape[0] // 2
            for cx, cy in chips:
                _remote(w_refs[i].at[pl.ds(c * r2, r2)], land_refs[i].at[_slot(k, swaps[i]), pl.ds(c * r2, r2)], send[i], recv[i],
                        (cx, cy, c)).start()
        token[...] = jnp.zeros_like(token)

    outs = pl.pallas_call(
        body, name=name,
        out_shape=[DMA_SEM] * (2 * n) + [pltpu.HBM(w.shape, w.dtype) for w in ws] + [pltpu.HBM(l.shape, l.dtype) for l in lands]
        + [jax.ShapeDtypeStruct((8, LANES), F32)],
        in_specs=[HBM] * (2 * n) + [ANY] * na, out_specs=[SEM] * (2 * n) + [HBM] * (2 * n) + [VMEM_FULL],
        input_output_aliases={i: 2 * n + i for i in range(2 * n)},
        compiler_params=pltpu.CompilerParams(has_side_effects=EFFECT),
    )(*[_in_hbm(w) for w in ws], *[_in_hbm(l) for l in lands], *after)
    return outs[:n], outs[n:2 * n], outs[2 * n:3 * n], outs[3 * n:4 * n], outs[4 * n]


def _gather_forward(send, recv, ws, lands, after, swaps, *, name):
    n = len(ws)

    def body(*refs):
        w_refs, land_refs = refs[:n], refs[n:2 * n]
        send1, recv1 = refs[2 * n:3 * n], refs[3 * n:4 * n]
        send2, recv2 = refs[4 * n + 1 + 2 * n:4 * n + 1 + 3 * n], refs[4 * n + 1 + 3 * n:4 * n + 1 + 4 * n]
        x, y, c, chips = _place()
        sibling = (x, y, 1 - c)
        for i in range(n):
            r2 = ws[i].shape[0] // 2
            win = _three_halves(land_refs[i], r2)
            done = _remote(win, win, send1[i], recv1[i], sibling)
            done.wait_send()
            done.wait_recv()
            for cx, cy in chips:
                got = land_refs[i].at[_slot(2 * cx + cy, swaps[i]), pl.ds(c * r2, r2)]
                _remote(got, got, send2[i], recv2[i], sibling).start()
        token = refs[8 * n + 1]
        token[...] = jnp.zeros_like(token)

    outs = pl.pallas_call(
        body, name=name,
        out_shape=[pltpu.HBM(w.shape, w.dtype) for w in ws] + [pltpu.HBM(l.shape, l.dtype) for l in lands] + [DMA_SEM] * (2 * n)
        + [jax.ShapeDtypeStruct((8, LANES), F32)],
        in_specs=[HBM] * (2 * n) + [SEM] * (2 * n) + [ANY], out_specs=[HBM] * (2 * n) + [SEM] * (2 * n) + [VMEM_FULL],
        input_output_aliases={i: i for i in range(2 * n)},
        compiler_params=pltpu.CompilerParams(has_side_effects=EFFECT),
    )(*ws, *lands, *send, *recv, after)
    return outs[2 * n:3 * n], outs[3 * n:4 * n], outs[n:2 * n], outs[4 * n]


def _gather_finish(send, recv, lands, after, *, name):
    n = len(lands)

    def body(*refs):
        land_refs = refs[:n]
        send2, recv2 = refs[n:2 * n], refs[2 * n:3 * n]
        x, y, c, _ = _place()
        for i in range(n):
            win = _three_halves(land_refs[i], lands[i].shape[1] // 2)
            done = _remote(win, win, send2[i], recv2[i], (x, y, 1 - c))
            done.wait_send()
            done.wait_recv()

    return pl.pallas_call(
        body, name=name,
        out_shape=[pltpu.HBM(l.shape, l.dtype) for l in lands],
        in_specs=[HBM] * n + [SEM] * (2 * n) + [ANY], out_specs=[HBM] * n,
        input_output_aliases={i: i for i in range(n)},
        compiler_params=pltpu.CompilerParams(has_side_effects=EFFECT),
    )(*lands, *send, *recv, after)


def _scatter_start(gs, swaps, *, name):
    n = len(gs)
    lands = [lax.empty((N_DEV, g.shape[1] // 2, g.shape[2]), g.dtype) for g in gs]

    def body(*refs):
        g_refs, land_refs = refs[:n], refs[n:2 * n]
        send, recv = refs[2 * n:3 * n], refs[3 * n:4 * n]
        token = refs[6 * n]
        x, y, c, chips = _place()
        k = 2 * x + y
        me = 2 * k + c
        for i in range(n):
            r2 = gs[i].shape[1] // 2
            for cx, cy in chips:
                for cc in range(2):
                    _remote(g_refs[i].at[_slot(2 * cx + cy, swaps[i]), pl.ds(cc * r2, r2)], land_refs[i].at[me], send[i], recv[i],
                            (cx, cy, cc)).start()
            _remote(g_refs[i].at[_slot(k, swaps[i]), pl.ds((1 - c) * r2, r2)], land_refs[i].at[me], send[i], recv[i],
                    (x, y, 1 - c)).start()
        token[...] = jnp.zeros_like(token)

    outs = pl.pallas_call(
        body, name=name,
        out_shape=[DMA_SEM] * (2 * n) + [pltpu.HBM(g.shape, g.dtype) for g in gs] + [pltpu.HBM(l.shape, l.dtype) for l in lands]
        + [jax.ShapeDtypeStruct((8, LANES), F32)],
        in_specs=[HBM] * (2 * n), out_specs=[SEM] * (2 * n) + [HBM] * (2 * n) + [VMEM_FULL],
        input_output_aliases={i: 2 * n + i for i in range(2 * n)},
        compiler_params=pltpu.CompilerParams(has_side_effects=EFFECT),
    )(*[_in_hbm(g) for g in gs], *[_in_hbm(l) for l in lands])
    return outs[:n], outs[n:2 * n], outs[2 * n:3 * n], outs[3 * n:4 * n], outs[4 * n]


def _scatter_wait(send, recv, gs, lands, after, *, name):
    n = len(gs)

    def body(*refs):
        land_refs = refs[n:2 * n]
        send1, recv1 = refs[2 * n:3 * n], refs[3 * n:4 * n]
        x, y, c, _ = _place()
        for i in range(n):
            win = land_refs[i].at[pl.ds(0, N_DEV - 1)]
            done = _remote(win, win, send1[i], recv1[i], (x, y, 1 - c))
            done.wait_send()
            done.wait_recv()

    outs = pl.pallas_call(
        body, name=name,
        out_shape=[pltpu.HBM(g.shape, g.dtype) for g in gs] + [pltpu.HBM(l.shape, l.dtype) for l in lands],
        in_specs=[HBM] * (2 * n) + [SEM] * (2 * n) + [ANY], out_specs=[HBM] * (2 * n),
        input_output_aliases={i: i for i in range(2 * n)},
        compiler_params=pltpu.CompilerParams(has_side_effects=EFFECT),
    )(*gs, *lands, *send, *recv, after)
    return outs[:n], outs[n:]


def _share_halves(ts, *, name):
    n = len(ts)

    def body(*refs):
        outs = refs[n:2 * n]
        send_sems, recv_sems = refs[2 * n:]
        x, y, c, _ = _place()
        sibling = (x, y, 1 - c)
        cps = []
        for i in range(n):
            r2 = ts[i].shape[0] // 2
            mine = outs[i].at[pl.ds(c * r2, r2)]
            cps.append(_remote(mine, mine, send_sems.at[i], recv_sems.at[i], sibling))
            cps[-1].start()
        for i in range(n):
            r2 = ts[i].shape[0] // 2
            got = outs[i].at[pl.ds((1 - c) * r2, r2)]
            _remote(got, got, send_sems.at[i], recv_sems.at[i], sibling).wait_recv()
        for cp in cps:
            cp.wait_send()

    return pl.pallas_call(
        body, name=name,
        out_shape=[jax.ShapeDtypeStruct(t.shape, t.dtype) for t in ts],
        in_specs=[ANY] * n, out_specs=[ANY] * n, input_output_aliases={i: i for i in range(n)},
        scratch_shapes=[pltpu.SemaphoreType.DMA((n,)), pltpu.SemaphoreType.DMA((n,))],
    )(*ts)


def _sum_pieces(land, g, idx, *, name):
    _, r2, C = land.shape
    tr = _tile(r2, max(16, (1 << 20) // C), 16)
    nr = r2 // tr

    def body(idx_ref, land_ref, own_ref, o_ref, acc_ref):
        d = pl.program_id(1)
        mine = d == idx_ref[0]

        @pl.when(d == 0)
        def _():
            acc_ref[...] = jnp.zeros_like(acc_ref)

        @pl.when(mine)
        def _():
            acc_ref[...] += own_ref[...].astype(F32)

        @pl.when(jnp.logical_not(mine))
        def _():
            acc_ref[...] += land_ref[...].astype(F32)

        @pl.when(d == N_DEV - 1)
        def _():
            o_ref[...] = acc_ref[...]

    return pl.pallas_call(
        body, name=name, out_shape=jax.ShapeDtypeStruct((2 * r2, C), F32),
        grid_spec=pltpu.PrefetchScalarGridSpec(
            num_scalar_prefetch=1, grid=(nr, N_DEV),
            in_specs=[pl.BlockSpec((None, tr, C), lambda i, d, ix: (jnp.where(d == ix[0], (d + 1) % N_DEV, d), i, 0)),
                      pl.BlockSpec((None, tr, C), lambda i, d, ix: (ix[1], ix[2] * nr + i, 0))],
            out_specs=pl.BlockSpec((tr, C), lambda i, d, ix: (ix[2] * nr + i, 0)),
            scratch_shapes=[pltpu.VMEM((tr, C), F32)]),
        compiler_params=_params("parallel", "arbitrary"),
    )(idx, land, g)


def _sum_devices(v, *, name):
    n, R, C = v.shape

    def body(v_ref, o_ref):
        acc = v_ref[0]
        for j in range(1, n):
            acc = acc + v_ref[j]
        o_ref[...] = acc

    return pl.pallas_call(body, name=name, out_shape=jax.ShapeDtypeStruct((R, C), F32),
                          in_specs=[VMEM_FULL], out_specs=VMEM_FULL)(v)


def _shard_cols(shards, lo, hi, width):
    out = []
    while lo < hi:
        j = lo // width
        end = min(hi, (j + 1) * width)
        out.append(shards[j][:, lo - j * width:end - j * width])
        lo = end
    return out


def _from_col_shards(g):
    return jnp.transpose(g, (1, 0, 2)).reshape(g.shape[1], N_CHIP * g.shape[2])


def _to_col_shards(w):
    R, N = w.shape
    return jnp.transpose(w.reshape(R, N_CHIP, N // N_CHIP), (1, 0, 2))


def _split_heads(w, widths):
    R, N = w.shape
    per = sum(widths)
    w3 = w.reshape(R, N // per, per)
    lo = w3[:, :, :widths[0]].reshape(R, -1)
    hi = w3[:, :, widths[0]:].reshape(R, -1)
    return jnp.concatenate([lo, hi], axis=1)


def _merge_heads(w, widths):
    R, N = w.shape
    H = N // sum(widths)
    lo = w[:, :H * widths[0]].reshape(R, H, widths[0])
    hi = w[:, H * widths[0]:].reshape(R, H, widths[1])
    return jnp.concatenate([lo, hi], axis=2).reshape(R, N)


def _t5_bucket(dist):
    max_exact = REL_BUCKETS // 2
    n = jnp.maximum(dist, 0)
    large = max_exact + (jnp.log(jnp.maximum(n, 1).astype(F32) / max_exact)
                         / jnp.log(jnp.asarray(REL_MAX_DIST / max_exact, F32))
                         * (REL_BUCKETS - max_exact)).astype(jnp.int32)
    large = jnp.minimum(large, REL_BUCKETS - 1)
    return jnp.where(n < max_exact, n, large)


def _rel_tables():
    a = jnp.arange(SWA_BLOCK)
    b = jnp.arange(2 * SWA_BLOCK)
    dist = SWA_BLOCK + a[:, None] - b[None, :]
    valid = jnp.logical_and(dist >= 0, dist < SWA_BLOCK)
    onehot = jnp.logical_and(_t5_bucket(dist)[..., None] == jnp.arange(REL_BUCKETS), valid[..., None])
    onehot = onehot.astype(F32).reshape(2 * SWA_BLOCK * SWA_BLOCK, REL_BUCKETS)
    negmask = jnp.where(valid, 0.0, NEG).astype(F32).reshape(1, -1)
    return onehot, negmask


def _rope_tables(S):
    pos = jnp.arange(S, dtype=F32)
    inv = ROPE_THETA ** (-jnp.arange(0, MLA_ROPE, 2, dtype=F32) / MLA_ROPE)
    ang = pos[:, None] * inv[None, :]
    ang = jnp.concatenate([ang, ang, ang, ang], axis=-1)
    return jnp.cos(ang), jnp.sin(ang)


def _flat_pad(parts, rows=8):
    flat = jnp.concatenate([p.reshape(1, -1) for p in parts], axis=1)
    n = flat.shape[1]
    width = -(-n // (rows * LANES)) * LANES
    return jnp.pad(flat, ((0, 0), (0, rows * width - n))).reshape(rows, width)


def _unflat(vec, shapes):
    flat = vec.reshape(-1)
    out, off = [], 0
    for s in shapes:
        n = 1
        for d in s:
            n *= d
        out.append(flat[off:off + n].reshape(s))
        off += n
    return out


def kernel(x, c, w_ada, b_ada, g_pre_mix, g_post_mix, w_in, g_q_lat, w_uq, g_kv_lat, w_ukv, rel_bias, sinks, w_o, g_pre_ffn, g_post_ffn, w_up, conv_w, conv_b, w_down, loss_target, m_w_ada, m_b_ada, m_g_pre_mix, m_g_post_mix, m_w_in, m_g_q_lat, m_w_uq, m_g_kv_lat, m_w_ukv, m_rel_bias, m_sinks, m_w_o, m_g_pre_ffn, m_g_post_ffn, m_w_up, m_conv_w, m_conv_b, m_w_down, v_w_ada, v_b_ada, v_g_pre_mix, v_g_post_mix, v_w_in, v_g_q_lat, v_w_uq, v_g_kv_lat, v_w_ukv, v_rel_bias, v_sinks, v_w_o, v_g_pre_ffn, v_g_post_ffn, v_w_up, v_conv_w, v_conv_b, v_w_down):
    S, D = x.shape[1], x.shape[2]
    Rq, Rkv = g_q_lat.shape[1], g_kv_lat.shape[1]
    H = D // MLA_V
    NH = D // SWA_HD
    KW = SWA_KVH * SWA_HD
    F = w_down.shape[1] * N_CHIP
    xi, yi, ci = lax.axis_index("x"), lax.axis_index("y"), lax.axis_index("c")
    chip = 2 * xi + yi
    me = 2 * chip + ci
    x2, tgt = x[0], loss_target[0]

    c_all = _allgather8(jnp.broadcast_to(c, (8, D)), name="gather_c")[:, 0, :]
    n3 = w_ada.shape[2]
    mod_part = _ada_fwd(c_all, w_ada[0], lax.dynamic_slice(b_ada, (0, chip * n3), (1, n3)), name="ada_fwd")
    mod_all = _allgather8(mod_part, name="gather_mod")
    mod_me = lax.dynamic_index_in_dim(mod_all[0::2], me, axis=1, keepdims=False).reshape(1, 6 * D)
    sh1, sc1, gt1, sh2, sc2, gt2 = [mod_me[:, k * D:(k + 1) * D] for k in range(6)]

    swaps = [False, False, False, False, True, False]
    local = [w_in[0].astype(BF16)]
    send_a, recv_a, srcs_a, lands_a, token = _gather_start(local, (mod_all,), swaps[:1], name="gather_start_in")
    rest, token = lax.optimization_barrier(((w_uq[0], w_ukv[0], w_o[0], w_up[0], w_down[0]), token))
    local += [w.astype(BF16) for w in rest]
    send_b, recv_b, srcs_b, lands_b, token = _gather_start(local[1:], (token,), swaps[1:], name="gather_start_rest")
    send1, recv1, srcs, lands = send_a + send_b, recv_a + recv_b, srcs_a + srcs_b, lands_a + lands_b
    onehot, negmask = _rel_tables()
    npb = NH // (2 * SWA_KVH)
    rb_st = jnp.transpose(rel_bias.T.reshape(SWA_KVH, npb, 2, REL_BUCKETS), (0, 2, 1, 3)).reshape(NH, REL_BUCKETS)
    bias_m = (_matmul(rb_st, onehot.T, tie=token, name="rel_bias_table") + negmask).reshape(
        2 * SWA_KVH, npb * SWA_BLOCK, 2 * SWA_BLOCK)
    h = _modnorm_fwd(x2, g_pre_mix, sc1, sh1, name="pre_mix_norm")

    def whole(land, i):
        return lax.dynamic_update_index_in_dim(land, local[i], _slot(chip, swaps[i]), 0)

    def conv_slots(v):
        return jnp.stack([v[0], v[2], v[1], v[3]])

    s2, r2, l_in, _ = _gather_forward(send1[:1], recv1[:1], srcs[:1], lands[:1], h, swaps[:1], name="gather_forward_in")
    (l_in,) = _gather_finish(s2, r2, l_in, h, name="gather_finish_in")
    gin = whole(l_in, 0)
    o_kr = Rq + Rkv
    o_q = o_kr + MLA_ROPE
    o_g = o_q + NH * SWA_HD + 2 * KW
    n_gate, n_swa = 2 * D, o_g - o_q
    n_lat = -(-(o_q + MLA_ROPE) // PAIR) * PAIR
    runs = []
    for tl in range(D // PAIR):
        runs.append((o_g + tl * PAIR, o_g + (tl + 1) * PAIR, 2 * tl * PAIR))
        runs.append((o_g + D + tl * PAIR, o_g + D + (tl + 1) * PAIR, (2 * tl + 1) * PAIR))
    runs.append((o_q, o_g, n_gate))
    runs.append((0, o_q, n_gate + n_swa))
    csh = gin.shape[2]
    parts = []
    for lo, hi, _ in runs + [(o_kr, o_q, 0)]:
        parts += _shard_cols([gin[j] for j in range(N_CHIP)], lo, hi, csh)
    parts.append(jnp.zeros((D, n_lat - o_q - MLA_ROPE), BF16))
    w_in_all = jnp.concatenate(parts, axis=1)
    cos, sin = _rope_tables(S)
    sink_v = sinks.reshape(NH)

    z_lat = _matmul(h, w_in_all, bcols=(n_gate + n_swa, n_lat), name="in_proj_lat")
    z_swa = _matmul(h, w_in_all, bcols=(n_gate, n_swa), name="in_proj_swa")
    zg = _matmul(h, w_in_all, bcols=(0, n_gate), out_dtype=BF16, name="in_proj_gate")
    s2b, r2b, l_b, _ = _gather_forward(send1[1:4], recv1[1:4], srcs[1:4], lands[1:4], zg, swaps[1:4],
                                       name="gather_forward_attn")
    nq, nkv = _lat_norm_fwd(z_lat, g_q_lat, g_kv_lat, name="lat_norm")
    l_uq, l_ukv, l_o = _gather_finish(s2b, r2b, l_b, nq, name="gather_finish_attn")
    wuq = _split_heads(_from_col_shards(whole(l_uq, 1)), (MLA_NOPE, MLA_ROPE))
    wukv = _split_heads(_from_col_shards(whole(l_ukv, 2)), (MLA_NOPE, MLA_V))
    wo = whole(l_o, 3).reshape(D, D)
    q_raw = _matmul(nq, wuq, out_dtype=BF16, name="uq_proj")
    kv_raw = _matmul(nkv, wukv, out_dtype=BF16, name="ukv_proj")
    qp, kp, vv = _mla_pack_fwd(q_raw, kv_raw, z_lat, cos, sin, o_kr, name="mla_pack")
    o_a, lse = _flash_fwd(qp, kp, vv, name="mla_attn")
    s2c, r2c, l_c, tok_c = _gather_forward(send1[4:], recv1[4:], srcs[4:], lands[4:], o_a, swaps[4:],
                                           name="gather_forward_ffn")
    o_b = _swa_fwd(z_swa, bias_m, sink_v, name="swa_attn")
    mixin = _gate_fwd(zg, o_a, o_b, name="gate_mix")
    mix = _matmul(mixin, wo, tie=tok_c, name="o_proj")
    x1, h2 = _resnorm_modnorm_fwd(x2, mix, g_post_mix, gt1, g_pre_ffn, sc2, sh2, name="post_mix_pre_ffn_norm")
    l_up, l_down = _gather_finish(s2c, r2c, l_c, h2, name="gather_finish_ffn")
    cw_all = _allgather8(jnp.pad(conv_w[0], ((0, 5), (0, 0))), tie=l_down, name="gather_conv_w")[0::2, :3]
    cw = conv_slots(cw_all)
    cb = conv_slots(conv_b.reshape(N_CHIP, 1, -1))
    wup = whole(l_up, 4)
    wdown = whole(l_down, 5).reshape(F, D)
    t = _matmul(h2, wup, out_dtype=BF16, shards="out", name="up_proj")
    a = _conv_gate_fwd(t, cw, cb, name="conv_gate")
    yv = _matmul(a, wdown, name="down_proj")
    dout, dy, dg_post_ffn, dgt2, loss_tile = _resnorm_loss(x1, yv, g_post_ffn, gt2, tgt, name="post_ffn_norm_loss")

    big_params = dict(w_in=(w_in, m_w_in, v_w_in), w_uq=(w_uq, m_w_uq, v_w_uq), w_ukv=(w_ukv, m_w_ukv, v_w_ukv),
                      w_o=(w_o, m_w_o, v_w_o), w_up=(w_up, m_w_up, v_w_up), w_down=(w_down, m_w_down, v_w_down))
    res = {}

    def start(nms, gs):
        sw = [nm == "w_up" for nm in nms]
        send, recv, gsrc, glands, tok = _scatter_start(gs, sw, name="grads_start_" + nms[0])
        return (nms, send, recv, gsrc, glands), tok

    def finish(pendings, after):
        nms, send, recv, gsrc, glands = [sum((list(p[k]) for p in pendings), []) for k in range(5)]
        gsrc, glands = _scatter_wait(send, recv, gsrc, glands, after, name="grads_wait_" + nms[0])
        halves = [_sum_pieces(l, g, jnp.stack([me, _slot(chip, nm == "w_up"), ci]).astype(jnp.int32), name="grad_sum_" + nm)
                  for l, g, nm in zip(glands, gsrc, nms)]
        for nm, g in zip(nms, _share_halves(halves, name="grads_share_" + nms[0])):
            w, m, v = big_params[nm]
            res[nm] = (g,) + tuple(_adamw(w[0], g, m[0], v[0], name="adamw_" + nm))

    dw_down = _matmul(a, dy, ta=True, out_dtype=BF16, name="down_proj_dw")
    p_down, tok = start(["w_down"], [dw_down.reshape(N_CHIP, F // N_CHIP, D)])
    da = _matmul(dy, wdown, tb=True, out_dtype=BF16, tie=tok, name="down_proj_dx")
    du, dcw_s, dcb_s = _conv_gate_bwd(t, da, cw, cb, name="conv_gate_bwd")
    dt = _conv_bwd_dt(du, cw, name="conv_bwd_dt")
    dw_up = _matmul(h2, dt, ta=True, out_dtype=BF16, shards="out", name="up_proj_dw")
    p_up, tok = start(["w_up"], [dw_up])
    dh2 = _matmul(dt, wup, tb=True, tie=tok, shards="k", name="up_proj_dx")
    dx1, dg_pre_ffn, dsc2, dsh2, dmix, dg_post_mix, dgt1 = _modnorm_resnorm_bwd(
        dh2, x1, g_pre_ffn, sc2, dout, mix, g_post_mix, gt1, name="pre_ffn_post_mix_norm_bwd")
    dw_o = _matmul(mixin, dmix, ta=True, out_dtype=BF16, name="o_proj_dw")
    p_o, tok = start(["w_o"], [dw_o.reshape(N_CHIP, D // N_CHIP, D)])
    dmixin = _matmul(dmix, wo, tb=True, out_dtype=BF16, tie=tok, name="o_proj_dx")
    do_a, do_b, dzg = _gate_bwd(dmixin, zg, o_a, o_b, name="gate_mix_bwd")
    dqp, dkp, dvv = _flash_bwd(qp, kp, vv, o_a, do_a, lse, name="mla_attn_bwd")
    dq_raw, dkv_raw, dkr = _mla_pack_bwd(dqp, dkp, dvv, cos, sin, name="mla_pack_bwd")
    dw_uq_p = _matmul(nq, dq_raw, ta=True, out_dtype=BF16, name="uq_proj_dw")
    dw_ukv_p = _matmul(nkv, dkv_raw, ta=True, out_dtype=BF16, name="ukv_proj_dw")
    p_qkv, tok = start(["w_uq", "w_ukv"], [_to_col_shards(_merge_heads(dw_uq_p, (MLA_NOPE, MLA_ROPE))),
                                           _to_col_shards(_merge_heads(dw_ukv_p, (MLA_NOPE, MLA_V)))])
    dnq = _matmul(dq_raw, wuq, tb=True, tie=tok, name="uq_proj_dx")
    dnkv = _matmul(dkv_raw, wukv, tb=True, name="ukv_proj_dx")
    dz_lat, dg_q, dg_kv = _lat_norm_bwd(z_lat, dnq, dnkv, dkr, g_q_lat, g_kv_lat, name="lat_norm_bwd")
    dz_swa, dbias, dsink = _swa_bwd(z_swa, bias_m, sink_v, o_b, do_b, name="swa_attn_bwd")
    dz = jnp.concatenate([dzg, dz_swa, dz_lat], axis=1)
    dw_in_p = _matmul(h, dz, ta=True, out_dtype=BF16, name="in_proj_dw")
    dw_shards = []
    for j in range(N_CHIP):
        cols = []
        for lo, hi, at in sorted(runs):
            a0, a1 = max(lo, j * csh), min(hi, (j + 1) * csh)
            if a0 < a1:
                cols.append(dw_in_p[:, at + a0 - lo:at + a1 - lo])
        dw_shards.append(jnp.concatenate(cols, axis=1))
    p_in, tok = start(["w_in"], [jnp.stack(dw_shards)])
    dh = _matmul(dz, w_in_all, tb=True, tie=tok, name="in_proj_dx")
    grad_x, dg_pre_mix, dsc1, dsh1 = _modnorm_bwd(dh, x2, g_pre_mix, sc1, dx1, name="pre_mix_norm_bwd")
    drel_st = _matmul(dbias.reshape(NH, -1), onehot, tie=grad_x, name="rel_bias_bwd")
    finish((p_down, p_up, p_o, p_qkv), drel_st)
    drel = jnp.transpose(drel_st.reshape(SWA_KVH, 2, npb, REL_BUCKETS), (0, 2, 1, 3)).reshape(NH, REL_BUCKETS).T

    dcw = _from_col_shards(conv_slots(dcw_s))
    dcb = conv_slots(dcb_s).reshape(1, -1)
    dmod = jnp.concatenate([dsh1, dsc1, dgt1, dsh2, dsc2, dgt2], axis=1)
    small = [dmod, dg_pre_mix, dg_post_mix, dg_pre_ffn, dg_post_ffn, dg_q, dg_kv, drel, dsink[:, :NH], dcb, dcw]
    shapes = [p.shape for p in small]
    done = [res[nm][1] for nm in ("w_down", "w_up", "w_o", "w_uq", "w_ukv")]
    small_all = _allgather8(_flat_pad(small), tie=done, name="gather_small_grads")
    tot = _unflat(_sum_devices(small_all, name="sum_small_grads"), shapes)
    g_b_ada, g_pre_mix_g, g_post_mix_g, g_pre_ffn_g, g_post_ffn_g, g_q_g, g_kv_g, g_rel, g_sinks, g_cb, g_cw_full = tot
    dmod_all = small_all.reshape(N_DEV, -1)[:, :6 * D]
    g_w_ada = _ada_bwd(c_all.T, lax.dynamic_slice(dmod_all, (0, chip * n3), (N_DEV, n3)), name="ada_bwd")
    ncw = conv_w.shape[2]
    g_cw = lax.dynamic_slice(g_cw_full, (0, chip * ncw), (3, ncw))

    res["w_ada"] = (g_w_ada,) + tuple(_adamw(w_ada[0], g_w_ada, m_w_ada[0], v_w_ada[0], name="adamw_w_ada"))
    finish((p_in,), g_w_ada)
    snames = ["b_ada", "g_pre_mix", "g_post_mix", "g_pre_ffn", "g_post_ffn", "g_q_lat", "g_kv_lat", "rel_bias", "sinks",
              "conv_b", "conv_w"]
    sw = [b_ada, g_pre_mix, g_post_mix, g_pre_ffn, g_post_ffn, g_q_lat, g_kv_lat, rel_bias, sinks, conv_b, conv_w]
    sm = [m_b_ada, m_g_pre_mix, m_g_post_mix, m_g_pre_ffn, m_g_post_ffn, m_g_q_lat, m_g_kv_lat, m_rel_bias, m_sinks,
          m_conv_b, m_conv_w]
    sv = [v_b_ada, v_g_pre_mix, v_g_post_mix, v_g_pre_ffn, v_g_post_ffn, v_g_q_lat, v_g_kv_lat, v_rel_bias, v_sinks,
          v_conv_b, v_conv_w]
    sg = [g_b_ada, g_pre_mix_g, g_post_mix_g, g_pre_ffn_g, g_post_ffn_g, g_q_g, g_kv_g, g_rel, g_sinks, g_cb, g_cw]
    sshapes = [w.shape for w in sw]
    sd, snm, snv = _adamw(_flat_pad(sw), _flat_pad(sg), _flat_pad(sm), _flat_pad(sv), name="adamw_small")
    sd, snm, snv = _unflat(sd, sshapes), _unflat(snm, sshapes), _unflat(snv, sshapes)
    for k, nm in enumerate(snames):
        res[nm] = (sg[k].reshape(sshapes[k]), sd[k], snm[k], snv[k])

    order = ["w_ada", "b_ada", "g_pre_mix", "g_post_mix", "w_in", "g_q_lat", "w_uq", "g_kv_lat", "w_ukv", "rel_bias", "sinks",
             "w_o", "g_pre_ffn", "g_post_ffn", "w_up", "conv_w", "conv_b", "w_down"]
    ref_shapes = dict(w_ada=w_ada.shape, w_in=w_in.shape, w_uq=w_uq.shape, w_ukv=w_ukv.shape, w_o=w_o.shape,
                      w_up=w_up.shape, w_down=w_down.shape)
    outs = []
    for k in range(4):
        for nm in order:
            arr = res[nm][k]
            outs.append(arr.reshape(ref_shapes[nm]) if nm in ref_shapes else arr)
    loss = lax.psum(loss_tile[0, 0], ("x", "y", "c"))
    return (loss, grad_x[None], *outs)
```

```python
import math

import jax
import jax.numpy as jnp
from jax import lax
from jax.experimental import pallas as pl
from jax.experimental.pallas import tpu as pltpu

F32 = jnp.float32
BF16 = jnp.bfloat16
MESH = pl.DeviceIdType.MESH
HIGHEST = lax.Precision.HIGHEST

N_DEV = 8
N_CHIP = 4
LANES = 128
MLA_NOPE = 128
MLA_ROPE = 64
MLA_V = 128
MLA_QK = MLA_NOPE + MLA_ROPE
MLA_QK_PAD = 256
ROPE_THETA = 10000.0
SWA_HD = 64
SWA_KVH = 4
SWA_BLOCK = 128
REL_BUCKETS = 32
REL_MAX_DIST = 128
PAIR = 512
EPS = 1e-6
NEG = -1e30
ADAM_LR = 0.001
ADAM_B1 = 0.9
ADAM_B2 = 0.999
ADAM_EPS = 1e-08
ADAM_WD = 0.01
ADAM_STEP = 10

ANY = pl.BlockSpec(memory_space=pl.ANY)
VMEM_FULL = pl.BlockSpec(memory_space=pltpu.VMEM)
SMEM_FULL = pl.BlockSpec(memory_space=pltpu.SMEM)


def _params(*sem):
    return pltpu.CompilerParams(dimension_semantics=sem if sem else None)


def _tied(body, tie):
    if tie is None:
        return body, [], []
    ties = list(tie) if isinstance(tie, (list, tuple)) else [tie]

    def tied_body(*refs):
        body(*refs[len(ties):])

    return tied_body, [ANY] * len(ties), ties


def _tile(n, pref, unit=LANES):
    best = None
    for t in range(unit, min(n, pref) + 1, unit):
        if n % t == 0:
            best = t
    return n if best is None else best


def _matmul(a, b, *, ta=False, tb=False, out_dtype=F32, tie=None, shards=None, bcols=None, name):
    a2 = a.shape[1:] if shards == "k" else a.shape
    b2 = b.shape[1:] if shards else b.shape
    nsh = b.shape[0] if shards else 1
    K, M = a2 if ta else a2[::-1]
    N, K2 = b2 if tb else b2[::-1]
    assert K == K2, (a.shape, b.shape, ta, tb)
    exact = a.dtype == F32
    col0 = 0
    if bcols is not None:
        assert not tb and shards is None
        col0, N = bcols
    tn = _tile(math.gcd(N, col0) if col0 else N, 2048)
    col0 //= tn
    tk = _tile(K, 2048)
    nkc = K // tk
    nk = nkc * (nsh if shards == "k" else 1)
    tm = M if M < 8 else _tile(M, 1024, LANES if ta else 8)
    dn = (((0 if ta else 1,), (1 if tb else 0,)), ((), ()))
    kax = 3 if shards == "out" else 2

    def product(a_ref, b_ref):
        return lax.dot_general(a_ref[...], b_ref[...], dn, preferred_element_type=F32,
                               precision=HIGHEST if exact else None)

    def body_acc(a_ref, b_ref, o_ref, acc_ref):
        k = pl.program_id(kax)

        @pl.when(k == 0)
        def _():
            acc_ref[...] = product(a_ref, b_ref)

        @pl.when(jnp.logical_and(k > 0, k < nk - 1))
        def _():
            acc_ref[...] += product(a_ref, b_ref)

        @pl.when(k == nk - 1)
        def _():
            o_ref[...] = (acc_ref[...] + product(a_ref, b_ref)).astype(o_ref.dtype)

    def body_one(a_ref, b_ref, o_ref):
        o_ref[...] = product(a_ref, b_ref).astype(o_ref.dtype)

    a_blk, b_blk = ((tk, tm) if ta else (tm, tk)), ((tn, tk) if tb else (tk, tn))
    a_at = (lambda i, k: (k, i)) if ta else (lambda i, k: (i, k))
    b_at = (lambda j, k: (j, k)) if tb else (lambda j, k: (k, j + col0))
    if shards == "out":
        grid = (nsh, M // tm, N // tn, nk)
        a_spec = pl.BlockSpec(a_blk, lambda s, i, j, k: a_at(i, k))
        b_spec = pl.BlockSpec((None,) + b_blk, lambda s, i, j, k: (s,) + b_at(j, k))
        o_spec = pl.BlockSpec((None, tm, tn), lambda s, i, j, k: (s, i, j))
        out_shape = jax.ShapeDtypeStruct((nsh, M, N), out_dtype)
        sem = ("parallel", "parallel", "parallel", "arbitrary")
    elif shards == "k":
        grid = (M // tm, N // tn, nk)
        a_spec = pl.BlockSpec((None,) + a_blk, lambda i, j, k: (k // nkc,) + a_at(i, k % nkc))
        b_spec = pl.BlockSpec((None,) + b_blk, lambda i, j, k: (k // nkc,) + b_at(j, k % nkc))
        o_spec = pl.BlockSpec((tm, tn), lambda i, j, k: (i, j))
        out_shape = jax.ShapeDtypeStruct((M, N), out_dtype)
        sem = ("parallel", "parallel", "arbitrary")
    else:
        grid = (M // tm, N // tn, nk)
        a_spec = pl.BlockSpec(a_blk, lambda i, j, k: a_at(i, k))
        b_spec = pl.BlockSpec(b_blk, lambda i, j, k: b_at(j, k))
        o_spec = pl.BlockSpec((tm, tn), lambda i, j, k: (i, j))
        out_shape = jax.ShapeDtypeStruct((M, N), out_dtype)
        sem = ("parallel", "parallel", "arbitrary")
    body, tspec, targ = _tied(body_one if nk == 1 else body_acc, tie)
    return pl.pallas_call(
        body, name=name, out_shape=out_shape, grid=grid, in_specs=tspec + [a_spec, b_spec], out_specs=o_spec,
        scratch_shapes=[] if nk == 1 else [pltpu.VMEM((tm, tn), F32)],
        compiler_params=_params(*sem),
    )(*targ, a, b)


def _row_tile(S, width):
    return _tile(S, max(8, (1 << 19) // width), 8)


def _rstd(x):
    return lax.rsqrt(jnp.mean(x * x, axis=-1, keepdims=True) + EPS)


def _acc_rows(ref, val, first):
    s = jnp.sum(val, axis=0, keepdims=True)

    @pl.when(first)
    def _():
        ref[...] = s

    @pl.when(jnp.logical_not(first))
    def _():
        ref[...] += s


def _modnorm_fwd(x, g, sc, sh, *, name):
    S, D = x.shape
    tr = _row_tile(S, D)

    def body(x_ref, g_ref, sc_ref, sh_ref, h_ref):
        xv = x_ref[...]
        n = (xv * _rstd(xv)) * g_ref[...]
        h_ref[...] = (n * (1.0 + sc_ref[...]) + sh_ref[...]).astype(BF16)

    row = pl.BlockSpec((tr, D), lambda i: (i, 0))
    vec = pl.BlockSpec((1, D), lambda i: (0, 0))
    return pl.pallas_call(
        body, name=name, out_shape=jax.ShapeDtypeStruct((S, D), BF16), grid=(S // tr,),
        in_specs=[row, vec, vec, vec], out_specs=row, compiler_params=_params("parallel"),
    )(x, g, sc, sh)


def _modnorm_bwd(dh, x, g, sc, dres, *, name):
    S, D = x.shape
    tr = _row_tile(S, D)

    def body(dh_ref, x_ref, g_ref, sc_ref, dres_ref, dx_ref, dg_ref, dsc_ref, dsh_ref):
        first = pl.program_id(0) == 0
        xv = x_ref[...]
        dhv = dh_ref[...]
        gv = g_ref[...]
        r = _rstd(xv)
        xhat = xv * r
        _acc_rows(dsh_ref, dhv, first)
        _acc_rows(dsc_ref, dhv * (xhat * gv), first)
        dn = dhv * (1.0 + sc_ref[...])
        _acc_rows(dg_ref, dn * xhat, first)
        dxhat = dn * gv
        proj = jnp.mean(dxhat * xhat, axis=-1, keepdims=True)
        dx_ref[...] = r * (dxhat - xhat * proj) + dres_ref[...]

    row = pl.BlockSpec((tr, D), lambda i: (i, 0))
    vec = pl.BlockSpec((1, D), lambda i: (0, 0))
    vshape = jax.ShapeDtypeStruct((1, D), F32)
    return pl.pallas_call(
        body, name=name,
        out_shape=(jax.ShapeDtypeStruct((S, D), F32), vshape, vshape, vshape), grid=(S // tr,),
        in_specs=[row, row, vec, vec, row], out_specs=(row, vec, vec, vec),
        compiler_params=_params("arbitrary"),
    )(dh, x, g, sc, dres)


def _resnorm_modnorm_fwd(xres, m, g, gt, g2, sc2, sh2, *, name):
    S, D = xres.shape
    tr = _row_tile(S, D)

    def body(x_ref, m_ref, g_ref, gt_ref, g2_ref, sc_ref, sh_ref, o_ref, h_ref):
        mv = m_ref[...]
        x1 = x_ref[...] + gt_ref[...] * ((mv * _rstd(mv)) * g_ref[...])
        o_ref[...] = x1
        n = (x1 * _rstd(x1)) * g2_ref[...]
        h_ref[...] = (n * (1.0 + sc_ref[...]) + sh_ref[...]).astype(BF16)

    row = pl.BlockSpec((tr, D), lambda i: (i, 0))
    vec = pl.BlockSpec((1, D), lambda i: (0, 0))
    return pl.pallas_call(
        body, name=name, out_shape=(jax.ShapeDtypeStruct((S, D), F32), jax.ShapeDtypeStruct((S, D), BF16)), grid=(S // tr,),
        in_specs=[row, row, vec, vec, vec, vec, vec], out_specs=(row, row), compiler_params=_params("parallel"),
    )(xres, m, g, gt, g2, sc2, sh2)


def _modnorm_resnorm_bwd(dh, x, g, sc, dres, m, g1, gt1, *, name):
    S, D = x.shape
    tr = _row_tile(S, D)

    def body(dh_ref, x_ref, g_ref, sc_ref, dres_ref, m_ref, g1_ref, gt1_ref,
             dx_ref, dg_ref, dsc_ref, dsh_ref, dm_ref, dg1_ref, dgt1_ref):
        first = pl.program_id(0) == 0
        xv = x_ref[...]
        dhv = dh_ref[...]
        gv = g_ref[...]
        r = _rstd(xv)
        xhat = xv * r
        _acc_rows(dsh_ref, dhv, first)
        _acc_rows(dsc_ref, dhv * (xhat * gv), first)
        dn = dhv * (1.0 + sc_ref[...])
        _acc_rows(dg_ref, dn * xhat, first)
        dxhat = dn * gv
        proj = jnp.mean(dxhat * xhat, axis=-1, keepdims=True)
        dx1 = r * (dxhat - xhat * proj) + dres_ref[...]
        dx_ref[...] = dx1
        mv = m_ref[...]
        g1v = g1_ref[...]
        r1 = _rstd(mv)
        mhat = mv * r1
        _acc_rows(dgt1_ref, dx1 * (mhat * g1v), first)
        dn1 = dx1 * gt1_ref[...]
        _acc_rows(dg1_ref, dn1 * mhat, first)
        dmhat = dn1 * g1v
        proj1 = jnp.mean(dmhat * mhat, axis=-1, keepdims=True)
        dm_ref[...] = (r1 * (dmhat - mhat * proj1)).astype(BF16)

    row = pl.BlockSpec((tr, D), lambda i: (i, 0))
    vec = pl.BlockSpec((1, D), lambda i: (0, 0))
    vshape = jax.ShapeDtypeStruct((1, D), F32)
    return pl.pallas_call(
        body, name=name,
        out_shape=(jax.ShapeDtypeStruct((S, D), F32), vshape, vshape, vshape, jax.ShapeDtypeStruct((S, D), BF16), vshape, vshape),
        grid=(S // tr,),
        in_specs=[row, row, vec, vec, row, row, vec, vec], out_specs=(row, vec, vec, vec, row, vec, vec),
        compiler_params=_params("arbitrary"),
    )(dh, x, g, sc, dres, m, g1, gt1)


def _resnorm_loss(xres, m, g, gt, target, *, name):
    S, D = xres.shape
    tr = _row_tile(S, D)

    def body(x_ref, m_ref, g_ref, gt_ref, t_ref, d_ref, dm_ref, dg_ref, dgt_ref, loss_ref):
        first = pl.program_id(0) == 0
        mv = m_ref[...]
        gv = g_ref[...]
        r = _rstd(mv)
        mhat = mv * r
        n = mhat * gv
        err = (x_ref[...] + gt_ref[...] * n) - t_ref[...]
        dv = err * (1.0 / D)
        d_ref[...] = dv
        part = 0.5 * jnp.sum(jnp.mean(err * err, axis=-1, keepdims=True), axis=0, keepdims=True)
        part = jnp.broadcast_to(part, loss_ref.shape)

        @pl.when(first)
        def _():
            loss_ref[...] = part

        @pl.when(jnp.logical_not(first))
        def _():
            loss_ref[...] += part

        _acc_rows(dgt_ref, dv * n, first)
        dn = dv * gt_ref[...]
        _acc_rows(dg_ref, dn * mhat, first)
        dmhat = dn * gv
        proj = jnp.mean(dmhat * mhat, axis=-1, keepdims=True)
        dm_ref[...] = (r * (dmhat - mhat * proj)).astype(BF16)

    row = pl.BlockSpec((tr, D), lambda i: (i, 0))
    vec = pl.BlockSpec((1, D), lambda i: (0, 0))
    vshape = jax.ShapeDtypeStruct((1, D), F32)
    return pl.pallas_call(
        body, name=name,
        out_shape=(jax.ShapeDtypeStruct((S, D), F32), jax.ShapeDtypeStruct((S, D), BF16), vshape, vshape,
                   jax.ShapeDtypeStruct((8, LANES), F32)), grid=(S // tr,),
        in_specs=[row, row, vec, vec, row], out_specs=(row, row, vec, vec, pl.BlockSpec((8, LANES), lambda i: (0, 0))),
        compiler_params=_params("arbitrary"),
    )(xres, m, g, gt, target)


def _lat_norm_fwd(z_lat, g_q, g_kv, *, name):
    S, W = z_lat.shape
    Rq, Rkv = g_q.shape[1], g_kv.shape[1]
    tr = _row_tile(S, W)

    def body(z_ref, gq_ref, gkv_ref, nq_ref, nkv_ref):
        cq = z_ref[:, :Rq]
        ckv = z_ref[:, Rq:Rq + Rkv]
        nq_ref[...] = ((cq * _rstd(cq)) * gq_ref[...]).astype(BF16)
        nkv_ref[...] = ((ckv * _rstd(ckv)) * gkv_ref[...]).astype(BF16)

    return pl.pallas_call(
        body, name=name,
        out_shape=(jax.ShapeDtypeStruct((S, Rq), BF16), jax.ShapeDtypeStruct((S, Rkv), BF16)), grid=(S // tr,),
        in_specs=[pl.BlockSpec((tr, W), lambda i: (i, 0)), pl.BlockSpec((1, Rq), lambda i: (0, 0)),
                  pl.BlockSpec((1, Rkv), lambda i: (0, 0))],
        out_specs=(pl.BlockSpec((tr, Rq), lambda i: (i, 0)), pl.BlockSpec((tr, Rkv), lambda i: (i, 0))),
        compiler_params=_params("parallel"),
    )(z_lat, g_q, g_kv)


def _lat_norm_bwd(z_lat, dnq, dnkv, dkr, g_q, g_kv, *, name):
    S, W = z_lat.shape
    Rq, Rkv = g_q.shape[1], g_kv.shape[1]
    tr = _row_tile(S, W)

    def one(c, dn, gv):
        r = _rstd(c)
        chat = c * r
        dchat = dn * gv
        proj = jnp.mean(dchat * chat, axis=-1, keepdims=True)
        return r * (dchat - chat * proj), dn * chat

    def body(z_ref, dnq_ref, dnkv_ref, dkr_ref, gq_ref, gkv_ref, dz_ref, dgq_ref, dgkv_ref):
        first = pl.program_id(0) == 0
        dcq, pq = one(z_ref[:, :Rq], dnq_ref[...], gq_ref[...])
        dckv, pkv = one(z_ref[:, Rq:Rq + Rkv], dnkv_ref[...], gkv_ref[...])
        _acc_rows(dgq_ref, pq, first)
        _acc_rows(dgkv_ref, pkv, first)
        dz_ref[:, :Rq] = dcq.astype(BF16)
        dz_ref[:, Rq:Rq + Rkv] = dckv.astype(BF16)
        dz_ref[:, Rq + Rkv:Rq + Rkv + LANES] = dkr_ref[...].astype(BF16)
        if W > Rq + Rkv + LANES:
            dz_ref[:, Rq + Rkv + LANES:] = jnp.zeros((tr, W - Rq - Rkv - LANES), BF16)

    return pl.pallas_call(
        body, name=name,
        out_shape=(jax.ShapeDtypeStruct((S, W), BF16), jax.ShapeDtypeStruct((1, Rq), F32),
                   jax.ShapeDtypeStruct((1, Rkv), F32)), grid=(S // tr,),
        in_specs=[pl.BlockSpec((tr, W), lambda i: (i, 0)), pl.BlockSpec((tr, Rq), lambda i: (i, 0)),
                  pl.BlockSpec((tr, Rkv), lambda i: (i, 0)), pl.BlockSpec((tr, LANES), lambda i: (i, 0)),
                  pl.BlockSpec((1, Rq), lambda i: (0, 0)), pl.BlockSpec((1, Rkv), lambda i: (0, 0))],
        out_specs=(pl.BlockSpec((tr, W), lambda i: (i, 0)), pl.BlockSpec((1, Rq), lambda i: (0, 0)),
                   pl.BlockSpec((1, Rkv), lambda i: (0, 0))),
        compiler_params=_params("arbitrary"),
    )(z_lat, dnq, dnkv, dkr, g_q, g_kv)


def _rot(x, lo32):
    a = pltpu.roll(x, 32, 1)
    b = pltpu.roll(x, LANES - 32, 1)
    return jnp.where(lo32, -b, a)


def _rot_t(g, lo32):
    a = pltpu.roll(g, 32, 1)
    b = pltpu.roll(g, LANES - 32, 1)
    return jnp.where(lo32, b, -a)


def _mla_pack_fwd(q_raw, kv_raw, z_lat, cos, sin, kr_off, *, name):
    S = q_raw.shape[0]
    H = kv_raw.shape[1] // (MLA_NOPE + MLA_V)
    assert kr_off % LANES == 0
    scale = MLA_QK ** -0.5
    tr = min(S, 128)
    nope_w = H * MLA_NOPE

    def body(q_ref, kv_ref, z_ref, cos_ref, sin_ref, qp_ref, kp_ref, v_ref):
        lane = lax.broadcasted_iota(jnp.int32, (tr, LANES), 1)
        lo32 = (lane % 64) < 32
        lo64 = lane < 64
        c = cos_ref[...]
        s = sin_ref[...]
        kr = z_ref[...]
        kr = (kr * c + _rot(kr, lo32) * s).astype(BF16)
        for hp in range(H // 2):
            xb = q_ref[:, nope_w + hp * LANES:nope_w + (hp + 1) * LANES].astype(F32)
            rb = (xb * c + _rot(xb, lo32) * s) * scale
            for e in range(2):
                h = 2 * hp + e
                base = h * MLA_QK_PAD
                qp_ref[:, base:base + LANES] = (q_ref[:, h * LANES:(h + 1) * LANES].astype(F32) * scale).astype(BF16)
                keep = lo64 if e == 0 else jnp.logical_not(lo64)
                qp_ref[:, base + LANES:base + 2 * LANES] = jnp.where(keep, rb, 0.0).astype(BF16)
                kp_ref[:, base:base + LANES] = kv_ref[:, h * LANES:(h + 1) * LANES].astype(BF16)
                kp_ref[:, base + LANES:base + 2 * LANES] = kr
        v_ref[...] = kv_ref[:, nope_w:].astype(BF16)

    return pl.pallas_call(
        body, name=name,
        out_shape=(jax.ShapeDtypeStruct((S, H * MLA_QK_PAD), BF16), jax.ShapeDtypeStruct((S, H * MLA_QK_PAD), BF16),
                   jax.ShapeDtypeStruct((S, H * MLA_V), BF16)), grid=(S // tr,),
        in_specs=[pl.BlockSpec((tr, q_raw.shape[1]), lambda i: (i, 0)), pl.BlockSpec((tr, kv_raw.shape[1]), lambda i: (i, 0)),
                  pl.BlockSpec((tr, LANES), lambda i: (i, kr_off // LANES)), pl.BlockSpec((tr, LANES), lambda i: (i, 0)),
                  pl.BlockSpec((tr, LANES), lambda i: (i, 0))],
        out_specs=(pl.BlockSpec((tr, H * MLA_QK_PAD), lambda i: (i, 0)), pl.BlockSpec((tr, H * MLA_QK_PAD), lambda i: (i, 0)),
                   pl.BlockSpec((tr, H * MLA_V), lambda i: (i, 0))),
        compiler_params=_params("parallel"),
    )(q_raw, kv_raw, z_lat, cos, sin)


def _mla_pack_bwd(dqp, dkp, dv, cos, sin, *, name):
    S = dqp.shape[0]
    H = dv.shape[1] // MLA_V
    scale = MLA_QK ** -0.5
    tr = min(S, 128)
    nope_w = H * MLA_NOPE

    def body(dqp_ref, dkp_ref, dv_ref, cos_ref, sin_ref, dq_ref, dkv_ref, dkr_ref):
        lane = lax.broadcasted_iota(jnp.int32, (tr, LANES), 1)
        lo32 = (lane % 64) < 32
        lo64 = lane < 64
        c = cos_ref[...]
        s = sin_ref[...]
        dkr2 = jnp.zeros((tr, LANES), F32)
        for hp in range(H // 2):
            be = (2 * hp) * MLA_QK_PAD
            bo = (2 * hp + 1) * MLA_QK_PAD
            g = jnp.where(lo64, dqp_ref[:, be + LANES:be + 2 * LANES].astype(F32),
                          dqp_ref[:, bo + LANES:bo + 2 * LANES].astype(F32)) * scale
            dq_ref[:, nope_w + hp * LANES:nope_w + (hp + 1) * LANES] = (g * c + _rot_t(g * s, lo32)).astype(BF16)
            for h, base in ((2 * hp, be), (2 * hp + 1, bo)):
                dq_ref[:, h * LANES:(h + 1) * LANES] = (dqp_ref[:, base:base + LANES].astype(F32) * scale).astype(BF16)
                dkv_ref[:, h * LANES:(h + 1) * LANES] = dkp_ref[:, base:base + LANES].astype(BF16)
                dkr2 = dkr2 + dkp_ref[:, base + LANES:base + 2 * LANES].astype(F32)
        dkr2 = dkr2 * c + _rot_t(dkr2 * s, lo32)
        dkr2 = dkr2 + pltpu.roll(dkr2, 64, 1)
        dkr_ref[...] = jnp.where(lo64, dkr2, 0.0)
        dkv_ref[:, nope_w:] = dv_ref[...].astype(BF16)

    return pl.pallas_call(
        body, name=name,
        out_shape=(jax.ShapeDtypeStruct((S, nope_w + H * MLA_ROPE), BF16), jax.ShapeDtypeStruct((S, 2 * nope_w), BF16),
                   jax.ShapeDtypeStruct((S, LANES), F32)), grid=(S // tr,),
        in_specs=[pl.BlockSpec((tr, H * MLA_QK_PAD), lambda i: (i, 0)), pl.BlockSpec((tr, H * MLA_QK_PAD), lambda i: (i, 0)),
                  pl.BlockSpec((tr, H * MLA_V), lambda i: (i, 0)), pl.BlockSpec((tr, LANES), lambda i: (i, 0)),
                  pl.BlockSpec((tr, LANES), lambda i: (i, 0))],
        out_specs=(pl.BlockSpec((tr, nope_w + H * MLA_ROPE), lambda i: (i, 0)), pl.BlockSpec((tr, 2 * nope_w), lambda i: (i, 0)),
                   pl.BlockSpec((tr, LANES), lambda i: (i, 0))),
        compiler_params=_params("parallel"),
    )(dqp, dkp, dv, cos, sin)


FLASH_HB_FWD = 8
FLASH_HB_BWD = 4


def _causal_pairs(nb):
    qi = [i for i in range(nb) for j in range(i + 1)]
    kj = [j for i in range(nb) for j in range(i + 1)]
    return jnp.asarray(qi, jnp.int32), jnp.asarray(kj, jnp.int32)


def _scores(q, k, diagonal, t):
    s = lax.dot_general(q, k, (((1,), (1,)), ((), ())), preferred_element_type=F32)
    if diagonal:
        row = lax.broadcasted_iota(jnp.int32, (t, t), 0)
        col = lax.broadcasted_iota(jnp.int32, (t, t), 1)
        s = jnp.where(col <= row, s, NEG)
    return s


def _flash_fwd(qp, kp, v, *, name):
    S = qp.shape[0]
    H = v.shape[1] // MLA_V
    t = min(S, 512)
    nb = S // t
    HB = min(FLASH_HB_FWD, H)
    qi, kj = _causal_pairs(nb)
    QW, VW = MLA_QK_PAD, MLA_V

    def body(qi_ref, kj_ref, q_ref, k_ref, v_ref, o_ref, lse_ref, m_s, l_s, acc_s):
        pr = pl.program_id(1)
        i = qi_ref[pr]
        j = kj_ref[pr]

        @pl.when(j == 0)
        def _():
            m_s[...] = jnp.full_like(m_s, NEG)
            l_s[...] = jnp.zeros_like(l_s)
            acc_s[...] = jnp.zeros_like(acc_s)

        def step(diagonal):
            state = [(m_s[hh], l_s[hh], acc_s[hh]) for hh in range(HB)]
            new = []
            for hh, (m_prev, l_prev, acc_prev) in enumerate(state):
                s = _scores(q_ref[:, hh * QW:(hh + 1) * QW], k_ref[:, hh * QW:(hh + 1) * QW], diagonal, t)
                m_cur = jnp.maximum(m_prev, jnp.max(s, axis=1, keepdims=True))
                alpha = jnp.exp(m_prev - m_cur)
                p = jnp.exp(s - m_cur[:, :1])
                l_new = alpha * l_prev + jnp.sum(p, axis=1, keepdims=True)
                acc = alpha * acc_prev + jnp.dot(p.astype(BF16), v_ref[:, hh * VW:(hh + 1) * VW], preferred_element_type=F32)
                new.append((m_cur, l_new, acc))
            for hh, (m_cur, l_new, acc) in enumerate(new):
                if diagonal:
                    o_ref[:, hh * VW:(hh + 1) * VW] = acc / l_new
                    lse_ref[hh] = m_cur + jnp.log(l_new)
                else:
                    l_s[hh] = l_new
                    acc_s[hh] = acc
                    m_s[hh] = m_cur

        @pl.when(i != j)
        def _():
            step(False)

        @pl.when(i == j)
        def _():
            step(True)

    return pl.pallas_call(
        body, name=name,
        out_shape=(jax.ShapeDtypeStruct((S, H * VW), F32), jax.ShapeDtypeStruct((H, S, LANES), F32)),
        grid_spec=pltpu.PrefetchScalarGridSpec(
            num_scalar_prefetch=2, grid=(H // HB, qi.shape[0]),
            in_specs=[pl.BlockSpec((t, HB * QW), lambda g, p, qi, kj: (qi[p], g)),
                      pl.BlockSpec((t, HB * QW), lambda g, p, qi, kj: (kj[p], g)),
                      pl.BlockSpec((t, HB * VW), lambda g, p, qi, kj: (kj[p], g))],
            out_specs=(pl.BlockSpec((t, HB * VW), lambda g, p, qi, kj: (qi[p], g)),
                       pl.BlockSpec((HB, t, LANES), lambda g, p, qi, kj: (g, qi[p], 0))),
            scratch_shapes=[pltpu.VMEM((HB, t, LANES), F32), pltpu.VMEM((HB, t, LANES), F32), pltpu.VMEM((HB, t, VW), F32)]),
        compiler_params=_params("parallel", "arbitrary"),
    )(qi, kj, qp, kp, v)


def _flash_bwd(qp, kp, v, o, do, lse, *, name):
    S = qp.shape[0]
    H = v.shape[1] // MLA_V
    t = min(S, 512)
    nb = S // t
    HB = min(FLASH_HB_BWD, H)
    qi = jnp.asarray([i for j in range(nb) for i in range(j, nb)], jnp.int32)
    kj = jnp.asarray([j for j in range(nb) for i in range(j, nb)], jnp.int32)
    npairs = qi.shape[0]
    QW, VW = MLA_QK_PAD, MLA_V
    tn = (((0,), (0,)), ((), ()))
    nt = (((1,), (1,)), ((), ()))

    def body(qi_ref, kj_ref, q_ref, k_ref, v_ref, o_ref, do_ref, lse_ref, dq_ref, dk_ref, dv_ref, dq_s, dk_s, dv_s):
        pr = pl.program_id(1)
        i = qi_ref[pr]
        j = kj_ref[pr]
        rows = pl.ds(pl.multiple_of(i * t, t), t)

        @pl.when(pr == 0)
        def _():
            dq_s[...] = jnp.zeros_like(dq_s)

        @pl.when(i == j)
        def _():
            dk_s[...] = jnp.zeros_like(dk_s)
            dv_s[...] = jnp.zeros_like(dv_s)

        def step(diagonal):
            for hh in range(HB):
                q = q_ref[:, hh * QW:(hh + 1) * QW]
                k = k_ref[:, hh * QW:(hh + 1) * QW]
                dob = do_ref[:, hh * VW:(hh + 1) * VW]
                p = jnp.exp(_scores(q, k, diagonal, t) - lse_ref[hh][:, :1])
                delta = jnp.sum(dob.astype(F32) * o_ref[:, hh * VW:(hh + 1) * VW], axis=1, keepdims=True)
                dp = lax.dot_general(dob, v_ref[:, hh * VW:(hh + 1) * VW], nt, preferred_element_type=F32)
                dsb = (p * (dp - delta)).astype(BF16)
                dv_s[:, hh * VW:(hh + 1) * VW] += lax.dot_general(p.astype(BF16), dob, tn, preferred_element_type=F32)
                dk_s[:, hh * QW:(hh + 1) * QW] += lax.dot_general(dsb, q, tn, preferred_element_type=F32)
                dq_s[rows, hh * QW:(hh + 1) * QW] += jnp.dot(dsb, k, preferred_element_type=F32)

        @pl.when(i != j)
        def _():
            step(False)

        @pl.when(i == j)
        def _():
            step(True)

        @pl.when(i == nb - 1)
        def _():
            dk_ref[...] = dk_s[...].astype(BF16)
            dv_ref[...] = dv_s[...].astype(BF16)

        @pl.when(pr == npairs - 1)
        def _():
            dq_ref[...] = dq_s[...].astype(BF16)

    qside = lambda g, p, qi, kj: (qi[p], g)
    kside = lambda g, p, qi, kj: (kj[p], g)
    whole = lambda g, p, qi, kj: (0, g)
    return pl.pallas_call(
        body, name=name,
        out_shape=(jax.ShapeDtypeStruct((S, H * QW), BF16), jax.ShapeDtypeStruct((S, H * QW), BF16),
                   jax.ShapeDtypeStruct((S, H * VW), BF16)),
        grid_spec=pltpu.PrefetchScalarGridSpec(
            num_scalar_prefetch=2, grid=(H // HB, npairs),
            in_specs=[pl.BlockSpec((t, HB * QW), qside), pl.BlockSpec((t, HB * QW), kside), pl.BlockSpec((t, HB * VW), kside),
                      pl.BlockSpec((t, HB * VW), qside), pl.BlockSpec((t, HB * VW), qside),
                      pl.BlockSpec((HB, t, LANES), lambda g, p, qi, kj: (g, qi[p], 0))],
            out_specs=(pl.BlockSpec((S, HB * QW), whole), pl.BlockSpec((t, HB * QW), kside), pl.BlockSpec((t, HB * VW), kside)),
            scratch_shapes=[pltpu.VMEM((S, HB * QW), F32), pltpu.VMEM((t, HB * QW), F32), pltpu.VMEM((t, HB * VW), F32)]),
        compiler_params=_params("parallel", "arbitrary"),
    )(qi, kj, qp, kp, v, o, do, lse)


def _swa_kv_halves(blk, hf, lo):
    if hf == 0:
        a = jnp.where(lo, blk, 0.0)
        b = pltpu.roll(a, 64, 1)
    else:
        b = jnp.where(lo, 0.0, blk)
        a = pltpu.roll(b, 64, 1)
    return a.astype(BF16), b.astype(BF16)


def _swa_softmax(qs, kx, bias, neg0, sk):
    s = lax.dot_general(qs, kx, (((1,), (1,)), ((), ())), preferred_element_type=F32) + bias + neg0
    m = jnp.maximum(jnp.max(s, axis=1, keepdims=True), sk)
    e = jnp.exp(s - m)
    es = jnp.exp(sk - m)
    inv = 1.0 / (jnp.sum(e, axis=1, keepdims=True) + es)
    return e * inv, es * inv


def _swa_stack(ref, kvh, npb, scale=None):
    parts = [ref[:, (kvh * npb + pb) * LANES:(kvh * npb + pb + 1) * LANES] for pb in range(npb)]
    x = jnp.concatenate(parts, axis=0)
    return x if scale is None else x * scale


def _swa_sink_col(sink_ref, kvh, e, npb):
    row = lax.broadcasted_iota(jnp.int32, (npb * SWA_BLOCK, 1), 0)
    col = jnp.zeros((npb * SWA_BLOCK, 1), F32) + sink_ref[2 * (kvh * npb) + e]
    for pb in range(1, npb):
        col = jnp.where(row >= pb * SWA_BLOCK, sink_ref[2 * (kvh * npb + pb) + e], col)
    return col


def _swa_fwd(z_swa, bias_st, sinks, *, name):
    S, W = z_swa.shape
    npb = bias_st.shape[1] // SWA_BLOCK
    NH = 2 * SWA_KVH * npb
    QW = NH * SWA_HD
    KW = SWA_KVH * SWA_HD
    nb = S // SWA_BLOCK
    B = SWA_BLOCK
    assert SWA_KVH % 2 == 0 and W == QW + 2 * KW

    def body(sink_ref, q_ref, kvc_ref, kvp_ref, b_ref, o_ref):
        n = pl.program_id(0)
        lo = lax.broadcasted_iota(jnp.int32, (2 * B, LANES), 1) < 64
        col = lax.broadcasted_iota(jnp.int32, (npb * B, 2 * B), 1)
        neg0 = jnp.where(jnp.logical_and(col < B, n == 0), NEG, 0.0)
        for kb in range(SWA_KVH // 2):
            kblk = jnp.concatenate([kvp_ref[:, kb * LANES:(kb + 1) * LANES], kvc_ref[:, kb * LANES:(kb + 1) * LANES]], axis=0)
            vblk = jnp.concatenate([kvp_ref[:, KW + kb * LANES:KW + (kb + 1) * LANES],
                                    kvc_ref[:, KW + kb * LANES:KW + (kb + 1) * LANES]], axis=0)
            for hf in range(2):
                kvh = 2 * kb + hf
                ks = _swa_kv_halves(kblk, hf, lo)
                vs = _swa_kv_halves(vblk, hf, lo)
                qs = _swa_stack(q_ref, kvh, npb, SWA_HD ** -0.5).astype(BF16)
                acc = jnp.zeros((npb * B, LANES), F32)
                for e in range(2):
                    p, _ = _swa_softmax(qs, ks[e], b_ref[2 * kvh + e], neg0, _swa_sink_col(sink_ref, kvh, e, npb))
                    acc = acc + jnp.dot(p.astype(BF16), vs[e], preferred_element_type=F32)
                for pb in range(npb):
                    P = kvh * npb + pb
                    o_ref[:, P * LANES:(P + 1) * LANES] = acc[pb * B:(pb + 1) * B]

    kvcol = QW // (2 * KW)
    assert QW % (2 * KW) == 0
    return pl.pallas_call(
        body, name=name,
        out_shape=jax.ShapeDtypeStruct((S, QW), F32), grid=(nb,),
        in_specs=[SMEM_FULL, pl.BlockSpec((B, QW), lambda n: (n, 0)), pl.BlockSpec((B, 2 * KW), lambda n: (n, kvcol)),
                  pl.BlockSpec((B, 2 * KW), lambda n: (jnp.maximum(n - 1, 0), kvcol)),
                  pl.BlockSpec(bias_st.shape, lambda n: (0, 0, 0))],
        out_specs=pl.BlockSpec((B, QW), lambda n: (n, 0)),
        compiler_params=_params("parallel"),
    )(sinks, z_swa, z_swa, z_swa, bias_st)


def _swa_bwd(z_swa, bias_st, sinks, o, do, *, name):
    S, W = z_swa.shape
    npb = bias_st.shape[1] // SWA_BLOCK
    NH = 2 * SWA_KVH * npb
    QW = NH * SWA_HD
    KW = SWA_KVH * SWA_HD
    nb = S // SWA_BLOCK
    B = SWA_BLOCK
    scale = SWA_HD ** -0.5
    tn = (((0,), (0,)), ((), ()))
    nt = (((1,), (1,)), ((), ()))

    def fold(x, hf, lo):
        x = x + pltpu.roll(x, 64, 1)
        return jnp.where(lo, x, 0.0) if hf == 0 else jnp.where(lo, 0.0, x)

    def body(sink_ref, q_ref, kvc_ref, kvp_ref, b_ref, o_ref, do_ref, dz_ref, dbias_ref, dsink_ref,
             cq_s, ck_s, cv_s, nq_s, nk_s, nv_s, pk_s, pv_s):
        n = pl.program_id(0)

        @pl.when(n == 0)
        def _():
            dbias_ref[...] = jnp.zeros_like(dbias_ref)
            dsink_ref[...] = jnp.zeros_like(dsink_ref)
            cq_s[...] = jnp.zeros_like(cq_s)
            ck_s[...] = jnp.zeros_like(ck_s)
            cv_s[...] = jnp.zeros_like(cv_s)

        @pl.when(n == nb)
        def _():
            pk_s[...] = jnp.zeros_like(pk_s)
            pv_s[...] = jnp.zeros_like(pv_s)

        @pl.when(n < nb)
        def _():
            lo = lax.broadcasted_iota(jnp.int32, (2 * B, LANES), 1) < 64
            lo1 = lax.broadcasted_iota(jnp.int32, (npb * B, LANES), 1) < 64
            lane1 = lax.broadcasted_iota(jnp.int32, (1, LANES), 1)
            col = lax.broadcasted_iota(jnp.int32, (npb * B, 2 * B), 1)
            neg0 = jnp.where(jnp.logical_and(col < B, n == 0), NEG, 0.0)
            dsink = jnp.zeros((1, LANES), F32)
            for kb in range(SWA_KVH // 2):
                kblk = jnp.concatenate([kvp_ref[:, kb * LANES:(kb + 1) * LANES], kvc_ref[:, kb * LANES:(kb + 1) * LANES]], axis=0)
                vblk = jnp.concatenate([kvp_ref[:, KW + kb * LANES:KW + (kb + 1) * LANES],
                                        kvc_ref[:, KW + kb * LANES:KW + (kb + 1) * LANES]], axis=0)
                dkblk = jnp.zeros((2 * B, LANES), F32)
                dvblk = jnp.zeros((2 * B, LANES), F32)
                for hf in range(2):
                    kvh = 2 * kb + hf
                    ks = _swa_kv_halves(kblk, hf, lo)
                    vs = _swa_kv_halves(vblk, hf, lo)
                    qs = _swa_stack(q_ref, kvh, npb, scale).astype(BF16)
                    dob = _swa_stack(do_ref, kvh, npb)
                    prod = dob.astype(F32) * _swa_stack(o_ref, kvh, npb)
                    dkj = jnp.zeros((2 * B, LANES), F32)
                    dvj = jnp.zeros((2 * B, LANES), F32)
                    dqs = jnp.zeros((npb * B, LANES), F32)
                    for e in range(2):
                        keep = lo1 if e == 0 else jnp.logical_not(lo1)
                        p, psink = _swa_softmax(qs, ks[e], b_ref[2 * kvh + e], neg0, _swa_sink_col(sink_ref, kvh, e, npb))
                        delta = jnp.sum(jnp.where(keep, prod, 0.0), axis=1, keepdims=True)
                        dp = lax.dot_general(dob, vs[e], nt, preferred_element_type=F32)
                        ds = p * (dp - delta)
                        dbias_ref[2 * kvh + e] += ds
                        pd = psink * delta
                        for pb in range(npb):
                            dsh = -jnp.sum(pd[pb * B:(pb + 1) * B], axis=0, keepdims=True)
                            dsink = dsink + jnp.where(lane1 == 2 * (kvh * npb + pb) + e, dsh, 0.0)
                        dsb = ds.astype(BF16)
                        dqs = dqs + jnp.dot(dsb, ks[e], preferred_element_type=F32)
                        keep2 = lo if e == 0 else jnp.logical_not(lo)
                        dkj = dkj + jnp.where(keep2, lax.dot_general(dsb, qs, tn, preferred_element_type=F32), 0.0)
                        dvj = dvj + jnp.where(keep2, lax.dot_general(p.astype(BF16), dob, tn, preferred_element_type=F32), 0.0)
                    for pb in range(npb):
                        P = kvh * npb + pb
                        nq_s[:, P * LANES:(P + 1) * LANES] = dqs[pb * B:(pb + 1) * B] * scale
                    dkblk = dkblk + fold(dkj, hf, lo)
                    dvblk = dvblk + fold(dvj, hf, lo)
                pk_s[:, kb * LANES:(kb + 1) * LANES] = dkblk[:B]
                nk_s[:, kb * LANES:(kb + 1) * LANES] = dkblk[B:]
                pv_s[:, kb * LANES:(kb + 1) * LANES] = dvblk[:B]
                nv_s[:, kb * LANES:(kb + 1) * LANES] = dvblk[B:]
            dsink_ref[...] += dsink

        dz_ref[:, :QW] = cq_s[...].astype(BF16)
        dz_ref[:, QW:QW + KW] = (ck_s[...] + pk_s[...]).astype(BF16)
        dz_ref[:, QW + KW:] = (cv_s[...] + pv_s[...]).astype(BF16)

        @pl.when(n < nb)
        def _():
            cq_s[...] = nq_s[...]
            ck_s[...] = nk_s[...]
            cv_s[...] = nv_s[...]

    kvcol = QW // (2 * KW)
    cur = lambda n: (jnp.minimum(n, nb - 1), 0)
    return pl.pallas_call(
        body, name=name,
        out_shape=(jax.ShapeDtypeStruct((S, W), BF16), jax.ShapeDtypeStruct(bias_st.shape, F32),
                   jax.ShapeDtypeStruct((1, LANES), F32)),
        grid=(nb + 1,),
        in_specs=[SMEM_FULL, pl.BlockSpec((B, QW), cur), pl.BlockSpec((B, 2 * KW), lambda n: (jnp.minimum(n, nb - 1), kvcol)),
                  pl.BlockSpec((B, 2 * KW), lambda n: (jnp.maximum(jnp.minimum(n, nb - 1) - 1, 0), kvcol)),
                  pl.BlockSpec(bias_st.shape, lambda n: (0, 0, 0)), pl.BlockSpec((B, QW), cur), pl.BlockSpec((B, QW), cur)],
        out_specs=(pl.BlockSpec((B, W), lambda n: (jnp.maximum(n - 1, 0), 0)),
                   pl.BlockSpec(bias_st.shape, lambda n: (0, 0, 0)), pl.BlockSpec((1, LANES), lambda n: (0, 0))),
        scratch_shapes=[pltpu.VMEM((B, QW), F32), pltpu.VMEM((B, KW), F32), pltpu.VMEM((B, KW), F32),
                        pltpu.VMEM((B, QW), F32), pltpu.VMEM((B, KW), F32), pltpu.VMEM((B, KW), F32),
                        pltpu.VMEM((B, KW), F32), pltpu.VMEM((B, KW), F32)],
        compiler_params=_params("arbitrary"),
    )(sinks, z_swa, z_swa, z_swa, bias_st, o, do)


def _gate_fwd(zg, o_a, o_b, *, name):
    S, D = o_a.shape
    tr = min(S, 512)

    def body(z_ref, a_ref, b_ref, m_ref):
        ga = jax.nn.sigmoid(z_ref[:, :PAIR].astype(F32))
        gb = jax.nn.sigmoid(z_ref[:, PAIR:].astype(F32))
        m_ref[...] = (ga * a_ref[...] + gb * b_ref[...]).astype(BF16)

    col = pl.BlockSpec((tr, PAIR), lambda i, j: (i, j))
    return pl.pallas_call(
        body, name=name, out_shape=jax.ShapeDtypeStruct((S, D), BF16), grid=(S // tr, D // PAIR),
        in_specs=[pl.BlockSpec((tr, 2 * PAIR), lambda i, j: (i, j)), col, col], out_specs=col,
        compiler_params=_params("parallel", "parallel"),
    )(zg, o_a, o_b)


def _gate_bwd(dmix, zg, o_a, o_b, *, name):
    S, D = o_a.shape
    tr = min(S, 512)

    def body(d_ref, z_ref, a_ref, b_ref, da_ref, db_ref, dz_ref):
        d = d_ref[...].astype(F32)
        ga = jax.nn.sigmoid(z_ref[:, :PAIR].astype(F32))
        gb = jax.nn.sigmoid(z_ref[:, PAIR:].astype(F32))
        da_ref[...] = (d * ga).astype(BF16)
        db_ref[...] = (d * gb).astype(BF16)
        dz_ref[:, :PAIR] = (d * a_ref[...] * (ga * (1.0 - ga))).astype(BF16)
        dz_ref[:, PAIR:] = (d * b_ref[...] * (gb * (1.0 - gb))).astype(BF16)

    col = pl.BlockSpec((tr, PAIR), lambda i, j: (i, j))
    wide = pl.BlockSpec((tr, 2 * PAIR), lambda i, j: (i, j))
    return pl.pallas_call(
        body, name=name,
        out_shape=(jax.ShapeDtypeStruct((S, D), BF16), jax.ShapeDtypeStruct((S, D), BF16), jax.ShapeDtypeStruct((S, 2 * D), BF16)),
        grid=(S // tr, D // PAIR), in_specs=[col, wide, col, col], out_specs=(col, col, wide),
        compiler_params=_params("parallel", "parallel"),
    )(dmix, zg, o_a, o_b)


def _conv_u(t_ref, prev_ref, w_ref, b_ref, m, i):
    cur = t_ref[m].astype(F32)
    live = (i > 0).astype(F32)
    p6 = prev_ref[m, 14:15, :].astype(F32) * live
    p7 = prev_ref[m, 15:16, :].astype(F32) * live
    row = lax.broadcasted_iota(jnp.int32, cur.shape, 0)
    t1 = jnp.where(row == 0, p7, pltpu.roll(cur, 1, 0))
    t2 = jnp.where(row == 0, p6, jnp.where(row == 1, p7, pltpu.roll(cur, 2, 0)))
    u = ((b_ref[m] + w_ref[m, 0:1, :] * t2) + w_ref[m, 1:2, :] * t1) + w_ref[m, 2:3, :] * cur
    return u, cur, t1, t2


def _conv_specs(tr, tc):
    blk = pl.BlockSpec((2, tr, tc), lambda p, j, i: (p, i, j))
    prev = pl.BlockSpec((2, 16, tc), lambda p, j, i: (p, jnp.maximum(i * (tr // 16) - 1, 0), j))
    w3 = pl.BlockSpec((2, 3, tc), lambda p, j, i: (p, 0, j))
    w1 = pl.BlockSpec((2, 1, tc), lambda p, j, i: (p, 0, j))
    return blk, prev, w3, w1


def _conv_gate_fwd(t, cw, cb, *, name):
    _, S, C = t.shape
    tr, tc = min(S, 512), _tile(C, 1536)
    ncol = C // tc
    blk, prev, w3, w1 = _conv_specs(tr, tc)

    def body(t_ref, prev_ref, w_ref, b_ref, a_ref):
        i = pl.program_id(2)
        u1 = _conv_u(t_ref, prev_ref, w_ref, b_ref, 0, i)[0]
        u2 = _conv_u(t_ref, prev_ref, w_ref, b_ref, 1, i)[0]
        a_ref[...] = (jax.nn.silu(u1) * u2).astype(BF16)

    return pl.pallas_call(
        body, name=name, out_shape=jax.ShapeDtypeStruct((S, 2 * C), BF16), grid=(2, ncol, S // tr),
        in_specs=[blk, prev, w3, w1], out_specs=pl.BlockSpec((tr, tc), lambda p, j, i: (i, p * ncol + j)),
        compiler_params=_params("parallel", "parallel", "parallel"),
    )(t, t, cw, cb)


def _conv_gate_bwd(t, da, cw, cb, *, name):
    _, S, C = t.shape
    tr, tc = min(S, 256), _tile(C, 1536)
    ncol = C // tc
    blk, prev, w3, w1 = _conv_specs(tr, tc)

    def body(t_ref, prev_ref, da_ref, w_ref, b_ref, du_ref, dw_ref, db_ref):
        i = pl.program_id(2)
        first = i == 0
        u1, c1, a1, b1 = _conv_u(t_ref, prev_ref, w_ref, b_ref, 0, i)
        u2, c2, a2, b2 = _conv_u(t_ref, prev_ref, w_ref, b_ref, 1, i)
        d = da_ref[...].astype(F32)
        sg = jax.nn.sigmoid(u1)
        du1 = d * u2 * (sg * (1.0 + u1 * (1.0 - sg)))
        du2 = d * (u1 * sg)
        for m, (du, cur, t1, t2) in enumerate(((du1, c1, a1, b1), (du2, c2, a2, b2))):
            du_ref[m] = du.astype(BF16)
            dw = jnp.concatenate([jnp.sum(du * t2, axis=0, keepdims=True), jnp.sum(du * t1, axis=0, keepdims=True),
                                  jnp.sum(du * cur, axis=0, keepdims=True)], axis=0)
            db = jnp.sum(du, axis=0, keepdims=True)

            @pl.when(first)
            def _():
                dw_ref[m] = dw
                db_ref[m] = db

            @pl.when(jnp.logical_not(first))
            def _():
                dw_ref[m] += dw
                db_ref[m] += db

    return pl.pallas_call(
        body, name=name,
        out_shape=(jax.ShapeDtypeStruct(t.shape, BF16), jax.ShapeDtypeStruct(cw.shape, F32), jax.ShapeDtypeStruct(cb.shape, F32)),
        grid=(2, ncol, S // tr),
        in_specs=[blk, prev, pl.BlockSpec((tr, tc), lambda p, j, i: (i, p * ncol + j)), w3, w1], out_specs=(blk, w3, w1),
        compiler_params=_params("parallel", "parallel", "arbitrary"),
    )(t, t, da, cw, cb)


def _conv_bwd_dt(du, cw, *, name):
    _, S, C = du.shape
    tr, tc = min(S, 512), _tile(C, 1536)
    nrow = S // tr
    blk, _, w3, _ = _conv_specs(tr, tc)
    nxt = pl.BlockSpec((2, 16, tc), lambda p, j, i: (p, jnp.minimum((i + 1) * (tr // 16), S // 16 - 1), j))

    def body(d_ref, next_ref, w_ref, dt_ref):
        i = pl.program_id(2)
        live = (i < nrow - 1).astype(F32)
        for m in range(2):
            cur = d_ref[m].astype(F32)
            n0 = next_ref[m, 0:1, :].astype(F32) * live
            n1 = next_ref[m, 1:2, :].astype(F32) * live
            row = lax.broadcasted_iota(jnp.int32, cur.shape, 0)
            d1 = jnp.where(row == tr - 1, n0, pltpu.roll(cur, tr - 1, 0))
            d2 = jnp.where(row == tr - 1, n1, jnp.where(row == tr - 2, n0, pltpu.roll(cur, tr - 2, 0)))
            dt_ref[m] = ((w_ref[m, 2:3, :] * cur + w_ref[m, 1:2, :] * d1) + w_ref[m, 0:1, :] * d2).astype(BF16)

    return pl.pallas_call(
        body, name=name, out_shape=jax.ShapeDtypeStruct(du.shape, BF16), grid=(2, C // tc, nrow),
        in_specs=[blk, nxt, w3], out_specs=blk, compiler_params=_params("parallel", "parallel", "parallel"),
    )(du, du, cw)


def _ada_fwd(c_all, w, b, *, name):
    Bn, D = c_all.shape
    N = w.shape[1]
    tn = _tile(N, 512)

    def body(c_ref, w_ref, b_ref, o_ref):
        o_ref[...] = jnp.dot(jax.nn.silu(c_ref[...]), w_ref[...], preferred_element_type=F32, precision=HIGHEST) + b_ref[...]

    return pl.pallas_call(
        body, name=name, out_shape=jax.ShapeDtypeStruct((Bn, N), F32), grid=(N // tn,),
        in_specs=[pl.BlockSpec((Bn, D), lambda j: (0, 0)), pl.BlockSpec((D, tn), lambda j: (0, j)),
                  pl.BlockSpec((1, tn), lambda j: (0, j))],
        out_specs=pl.BlockSpec((Bn, tn), lambda j: (0, j)), compiler_params=_params("parallel"),
    )(c_all, w, b)


def _ada_bwd(c_all_t, dmod, *, name):
    D, Bn = c_all_t.shape
    N = dmod.shape[1]
    tm = _tile(D, 512, 8)
    tn = _tile(N, 1536)

    def body(c_ref, d_ref, o_ref):
        o_ref[...] = jnp.dot(jax.nn.silu(c_ref[...]), d_ref[...], preferred_element_type=F32, precision=HIGHEST)

    return pl.pallas_call(
        body, name=name, out_shape=jax.ShapeDtypeStruct((D, N), F32), grid=(D // tm, N // tn),
        in_specs=[pl.BlockSpec((tm, Bn), lambda i, j: (i, 0)), pl.BlockSpec((Bn, tn), lambda i, j: (0, j))],
        out_specs=pl.BlockSpec((tm, tn), lambda i, j: (i, j)), compiler_params=_params("parallel", "parallel"),
    )(c_all_t, dmod)


def _adamw(w, g, m, v, *, name):
    R, C = w.shape
    tr = R if R * C <= (1 << 19) else _tile(R, max(8, (1 << 19) // C), 8)

    def body(w_ref, g_ref, m_ref, v_ref, d_ref, nm_ref, nv_ref):
        gv = g_ref[...]
        nm = ADAM_B1 * m_ref[...] + (1.0 - ADAM_B1) * gv
        nv = ADAM_B2 * v_ref[...] + (1.0 - ADAM_B2) * (gv * gv)
        m_hat = nm / (1.0 - ADAM_B1 ** ADAM_STEP)
        v_hat = nv / (1.0 - ADAM_B2 ** ADAM_STEP)
        d_ref[...] = -ADAM_LR * (m_hat / (jnp.sqrt(v_hat) + ADAM_EPS) + ADAM_WD * w_ref[...])
        nm_ref[...] = nm
        nv_ref[...] = nv

    blk = pl.BlockSpec((tr, C), lambda i: (i, 0))
    shp = jax.ShapeDtypeStruct((R, C), F32)
    return pl.pallas_call(
        body, name=name, out_shape=(shp, shp, shp), grid=(R // tr,), in_specs=[blk] * 4, out_specs=(blk,) * 3,
        compiler_params=_params("parallel"),
    )(w, g, m, v)


def _place():
    x, y, c = lax.axis_index("x"), lax.axis_index("y"), lax.axis_index("c")
    return x, y, c, [(1 - x, y), (x, 1 - y), (1 - x, 1 - y)]


def _remote(src, dst, send_sem, recv_sem, dev):
    return pltpu.make_async_remote_copy(src_ref=src, dst_ref=dst, send_sem=send_sem, recv_sem=recv_sem,
                                        device_id=dev, device_id_type=MESH)


def _allgather8(v, *, tie=None, name):
    R, C = v.shape

    def body(v_ref, out_ref, send_sems, recv_sems, local_sem):
        x, y, c, chips = _place()
        me, sibling = (x, y, c), (x, y, 1 - c)

        def rows(px, py, pc):
            return out_ref.at[pl.ds((4 * px + 2 * py + pc) * R, R), :]

        def copy(k, block, to, src=None):
            return _remote(rows(*block) if src is None else src, rows(*block), send_sems.at[k], recv_sems.at[k], to)

        mine = pltpu.make_async_copy(v_ref, rows(*me), local_sem)
        mine.start()
        first = [copy(0, me, sibling, src=v_ref)]
        first += [copy(1 + j, me, (*chip, c), src=v_ref) for j, chip in enumerate(chips)]
        for cp in first:
            cp.start()
        passed = [copy(4 + j, (*chip, c), sibling) for j, chip in enumerate(chips)]
        for j, chip in enumerate(chips):
            copy(1 + j, (*chip, c), me).wait_recv()
            passed[j].start()
        copy(0, sibling, me).wait_recv()
        for j, chip in enumerate(chips):
            copy(4 + j, (*chip, 1 - c), me).wait_recv()
        for cp in first + passed:
            cp.wait_send()
        mine.wait()

    body, tspec, targ = _tied(body, tie)
    out = pl.pallas_call(
        body, name=name, out_shape=jax.ShapeDtypeStruct((N_DEV * R, C), v.dtype),
        in_specs=tspec + [VMEM_FULL], out_specs=VMEM_FULL,
        scratch_shapes=[pltpu.SemaphoreType.DMA((7,)), pltpu.SemaphoreType.DMA((7,)), pltpu.SemaphoreType.DMA],
    )(*targ, v)
    return out.reshape(N_DEV, R, C)


SEM = pl.BlockSpec(memory_space=pltpu.SEMAPHORE)
HBM = pl.BlockSpec(memory_space=pltpu.HBM)
EFFECT = pltpu.SideEffectType.DATAFLOW_SIDE_EFFECTING
DMA_SEM = pltpu.SemaphoreType.DMA(())


def _in_hbm(a):
    return pltpu.with_memory_space_constraint(a, pltpu.HBM)


def _three_halves(land, r2):
    return land.at[pl.ds(0, N_CHIP - 1), pl.ds(0, r2)]


def _slot(chip, swap):
    return (chip % 2) * 2 + chip // 2 if swap else chip


def _gather_start(ws, after, swaps, *, name):
    n = len(ws)
    na = len(after)
    lands = [lax.empty((N_CHIP,) + w.shape, w.dtype) for w in ws]

    def body(*refs):
        w_refs, land_refs = refs[:n], refs[n:2 * n]
        send, recv = refs[2 * n + na:3 * n + na], refs[3 * n + na:4 * n + na]
        token = refs[6 * n + na]
        x, y, c, chips = _place()
        k = 2 * x + y
        for i in range(n):
            r2 = ws[i].shape[0] // 2
            for cx, cy in chips:
                _remote(w_refs[i].at[pl.ds(c * r2, r2)], land_refs[i].at[_slot(k, swaps[i]), pl.ds(c * r2, r2)], send[i], recv[i],
                        (cx, cy, c)).start()
        token[...] = jnp.zeros_like(token)

    outs = pl.pallas_call(
        body, name=name,
        out_shape=[DMA_SEM] * (2 * n) + [pltpu.HBM(w.shape, w.dtype) for w in ws] + [pltpu.HBM(l.shape, l.dtype) for l in lands]
        + [jax.ShapeDtypeStruct((8, LANES), F32)],
        in_specs=[HBM] * (2 * n) + [ANY] * na, out_specs=[SEM] * (2 * n) + [HBM] * (2 * n) + [VMEM_FULL],
        input_output_aliases={i: 2 * n + i for i in range(2 * n)},
        compiler_params=pltpu.CompilerParams(has_side_effects=EFFECT),
    )(*[_in_hbm(w) for w in ws], *[_in_hbm(l) for l in lands], *after)
    return outs[:n], outs[n:2 * n], outs[2 * n:3 * n], outs[3 * n:4 * n], outs[4 * n]


def _gather_forward(send, recv, ws, lands, after, swaps, *, name):
    n = len(ws)

    def body(*refs):
        w_refs, land_refs = refs[:n], refs[n:2 * n]
        send1, recv1 = refs[2 * n:3 * n], refs[3 * n:4 * n]
        send2, recv2 = refs[4 * n + 1 + 2 * n:4 * n + 1 + 3 * n], refs[4 * n + 1 + 3 * n:4 * n + 1 + 4 * n]
        x, y, c, chips = _place()
        sibling = (x, y, 1 - c)
        for i in range(n):
            r2 = ws[i].shape[0] // 2
            win = _three_halves(land_refs[i], r2)
            done = _remote(win, win, send1[i], recv1[i], sibling)
            done.wait_send()
            done.wait_recv()
            for cx, cy in chips:
                got = land_refs[i].at[_slot(2 * cx + cy, swaps[i]), pl.ds(c * r2, r2)]
                _remote(got, got, send2[i], recv2[i], sibling).start()
        token = refs[8 * n + 1]
        token[...] = jnp.zeros_like(token)

    outs = pl.pallas_call(
        body, name=name,
        out_shape=[pltpu.HBM(w.shape, w.dtype) for w in ws] + [pltpu.HBM(l.shape, l.dtype) for l in lands] + [DMA_SEM] * (2 * n)
        + [jax.ShapeDtypeStruct((8, LANES), F32)],
        in_specs=[HBM] * (2 * n) + [SEM] * (2 * n) + [ANY], out_specs=[HBM] * (2 * n) + [SEM] * (2 * n) + [VMEM_FULL],
        input_output_aliases={i: i for i in range(2 * n)},
        compiler_params=pltpu.CompilerParams(has_side_effects=EFFECT),
    )(*ws, *lands, *send, *recv, after)
    return outs[2 * n:3 * n], outs[3 * n:4 * n], outs[n:2 * n], outs[4 * n]


def _gather_finish(send, recv, lands, after, *, name):
    n = len(lands)

    def body(*refs):
        land_refs = refs[:n]
        send2, recv2 = refs[n:2 * n], refs[2 * n:3 * n]
        x, y, c, _ = _place()
        for i in range(n):
            win = _three_halves(land_refs[i], lands[i].shape[1] // 2)
            done = _remote(win, win, send2[i], recv2[i], (x, y, 1 - c))
            done.wait_send()
            done.wait_recv()

    return pl.pallas_call(
        body, name=name,
        out_shape=[pltpu.HBM(l.shape, l.dtype) for l in lands],
        in_specs=[HBM] * n + [SEM] * (2 * n) + [ANY], out_specs=[HBM] * n,
        input_output_aliases={i: i for i in range(n)},
        compiler_params=pltpu.CompilerParams(has_side_effects=EFFECT),
    )(*lands, *send, *recv, after)


def _scatter_start(gs, swaps, *, name):
    n = len(gs)
    lands = [lax.empty((N_DEV, g.shape[1] // 2, g.shape[2]), g.dtype) for g in gs]

    def body(*refs):
        g_refs, land_refs = refs[:n], refs[n:2 * n]
        send, recv = refs[2 * n:3 * n], refs[3 * n:4 * n]
        token = refs[6 * n]
        x, y, c, chips = _place()
        k = 2 * x + y
        me = 2 * k + c
        for i in range(n):
            r2 = gs[i].shape[1] // 2
            for cx, cy in chips:
                for cc in range(2):
                    _remote(g_refs[i].at[_slot(2 * cx + cy, swaps[i]), pl.ds(cc * r2, r2)], land_refs[i].at[me], send[i], recv[i],
                            (cx, cy, cc)).start()
            _remote(g_refs[i].at[_slot(k, swaps[i]), pl.ds((1 - c) * r2, r2)], land_refs[i].at[me], send[i], recv[i],
                    (x, y, 1 - c)).start()
        token[...] = jnp.zeros_like(token)

    outs = pl.pallas_call(
        body, name=name,
        out_shape=[DMA_SEM] * (2 * n) + [pltpu.HBM(g.shape, g.dtype) for g in gs] + [pltpu.HBM(l.shape, l.dtype) for l in lands]
        + [jax.ShapeDtypeStruct((8, LANES), F32)],
        in_specs=[HBM] * (2 * n), out_specs=[SEM] * (2 * n) + [HBM] * (2 * n) + [VMEM_FULL],
        input_output_aliases={i: 2 * n + i for i in range(2 * n)},
        compiler_params=pltpu.CompilerParams(has_side_effects=EFFECT),
    )(*[_in_hbm(g) for g in gs], *[_in_hbm(l) for l in lands])
    return outs[:n], outs[n:2 * n], outs[2 * n:3 * n], outs[3 * n:4 * n], outs[4 * n]


def _scatter_wait(send, recv, gs, lands, after, *, name):
    n = len(gs)

    def body(*refs):
        land_refs = refs[n:2 * n]
        send1, recv1 = refs[2 * n:3 * n], refs[3 * n:4 * n]
        x, y, c, _ = _place()
        for i in range(n):
            win = land_refs[i].at[pl.ds(0, N_DEV - 1)]
            done = _remote(win, win, send1[i], recv1[i], (x, y, 1 - c))
            done.wait_send()
            done.wait_recv()

    outs = pl.pallas_call(
        body, name=name,
        out_shape=[pltpu.HBM(g.shape, g.dtype) for g in gs] + [pltpu.HBM(l.shape, l.dtype) for l in lands],
        in_specs=[HBM] * (2 * n) + [SEM] * (2 * n) + [ANY], out_specs=[HBM] * (2 * n),
        input_output_aliases={i: i for i in range(2 * n)},
        compiler_params=pltpu.CompilerParams(has_side_effects=EFFECT),
    )(*gs, *lands, *send, *recv, after)
    return outs[:n], outs[n:]


def _share_halves(ts, *, name):
    n = len(ts)

    def body(*refs):
        outs = refs[n:2 * n]
        send_sems, recv_sems = refs[2 * n:]
        x, y, c, _ = _place()
        sibling = (x, y, 1 - c)
        cps = []
        for i in range(n):
            r2 = ts[i].shape[0] // 2
            mine = outs[i].at[pl.ds(c * r2, r2)]
            cps.append(_remote(mine, mine, send_sems.at[i], recv_sems.at[i], sibling))
            cps[-1].start()
        for i in range(n):
            r2 = ts[i].shape[0] // 2
            got = outs[i].at[pl.ds((1 - c) * r2, r2)]
            _remote(got, got, send_sems.at[i], recv_sems.at[i], sibling).wait_recv()
        for cp in cps:
            cp.wait_send()

    return pl.pallas_call(
        body, name=name,
        out_shape=[jax.ShapeDtypeStruct(t.shape, t.dtype) for t in ts],
        in_specs=[ANY] * n, out_specs=[ANY] * n, input_output_aliases={i: i for i in range(n)},
        scratch_shapes=[pltpu.SemaphoreType.DMA((n,)), pltpu.SemaphoreType.DMA((n,))],
    )(*ts)


def _sum_pieces(land, g, idx, *, name):
    _, r2, C = land.shape
    tr = _tile(r2, max(16, (1 << 20) // C), 16)
    nr = r2 // tr

    def body(idx_ref, land_ref, own_ref, o_ref, acc_ref):
        d = pl.program_id(1)
        mine = d == idx_ref[0]

        @pl.when(d == 0)
        def _():
            acc_ref[...] = jnp.zeros_like(acc_ref)

        @pl.when(mine)
        def _():
            acc_ref[...] += own_ref[...].astype(F32)

        @pl.when(jnp.logical_not(mine))
        def _():
            acc_ref[...] += land_ref[...].astype(F32)

        @pl.when(d == N_DEV - 1)
        def _():
            o_ref[...] = acc_ref[...]

    return pl.pallas_call(
        body, name=name, out_shape=jax.ShapeDtypeStruct((2 * r2, C), F32),
        grid_spec=pltpu.PrefetchScalarGridSpec(
            num_scalar_prefetch=1, grid=(nr, N_DEV),
            in_specs=[pl.BlockSpec((None, tr, C), lambda i, d, ix: (jnp.where(d == ix[0], (d + 1) % N_DEV, d), i, 0)),
                      pl.BlockSpec((None, tr, C), lambda i, d, ix: (ix[1], ix[2] * nr + i, 0))],
            out_specs=pl.BlockSpec((tr, C), lambda i, d, ix: (ix[2] * nr + i, 0)),
            scratch_shapes=[pltpu.VMEM((tr, C), F32)]),
        compiler_params=_params("parallel", "arbitrary"),
    )(idx, land, g)


def _sum_devices(v, *, name):
    n, R, C = v.shape

    def body(v_ref, o_ref):
        acc = v_ref[0]
        for j in range(1, n):
            acc = acc + v_ref[j]
        o_ref[...] = acc

    return pl.pallas_call(body, name=name, out_shape=jax.ShapeDtypeStruct((R, C), F32),
                          in_specs=[VMEM_FULL], out_specs=VMEM_FULL)(v)


def _shard_cols(shards, lo, hi, width):
    out = []
    while lo < hi:
        j = lo // width
        end = min(hi, (j + 1) * width)
        out.append(shards[j][:, lo - j * width:end - j * width])
        lo = end
    return out


def _from_col_shards(g):
    return jnp.transpose(g, (1, 0, 2)).reshape(g.shape[1], N_CHIP * g.shape[2])


def _to_col_shards(w):
    R, N = w.shape
    return jnp.transpose(w.reshape(R, N_CHIP, N // N_CHIP), (1, 0, 2))


def _split_heads(w, widths):
    R, N = w.shape
    per = sum(widths)
    w3 = w.reshape(R, N // per, per)
    lo = w3[:, :, :widths[0]].reshape(R, -1)
    hi = w3[:, :, widths[0]:].reshape(R, -1)
    return jnp.concatenate([lo, hi], axis=1)


def _merge_heads(w, widths):
    R, N = w.shape
    H = N // sum(widths)
    lo = w[:, :H * widths[0]].reshape(R, H, widths[0])
    hi = w[:, H * widths[0]:].reshape(R, H, widths[1])
    return jnp.concatenate([lo, hi], axis=2).reshape(R, N)


def _t5_bucket(dist):
    max_exact = REL_BUCKETS // 2
    n = jnp.maximum(dist, 0)
    large = max_exact + (jnp.log(jnp.maximum(n, 1).astype(F32) / max_exact)
                         / jnp.log(jnp.asarray(REL_MAX_DIST / max_exact, F32))
                         * (REL_BUCKETS - max_exact)).astype(jnp.int32)
    large = jnp.minimum(large, REL_BUCKETS - 1)
    return jnp.where(n < max_exact, n, large)


def _rel_tables():
    a = jnp.arange(SWA_BLOCK)
    b = jnp.arange(2 * SWA_BLOCK)
    dist = SWA_BLOCK + a[:, None] - b[None, :]
    valid = jnp.logical_and(dist >= 0, dist < SWA_BLOCK)
    onehot = jnp.logical_and(_t5_bucket(dist)[..., None] == jnp.arange(REL_BUCKETS), valid[..., None])
    onehot = onehot.astype(F32).reshape(2 * SWA_BLOCK * SWA_BLOCK, REL_BUCKETS)
    negmask = jnp.where(valid, 0.0, NEG).astype(F32).reshape(1, -1)
    return onehot, negmask


def _rope_tables(S):
    pos = jnp.arange(S, dtype=F32)
    inv = ROPE_THETA ** (-jnp.arange(0, MLA_ROPE, 2, dtype=F32) / MLA_ROPE)
    ang = pos[:, None] * inv[None, :]
    ang = jnp.concatenate([ang, ang, ang, ang], axis=-1)
    return jnp.cos(ang), jnp.sin(ang)


def _flat_pad(parts, rows=8):
    flat = jnp.concatenate([p.reshape(1, -1) for p in parts], axis=1)
    n = flat.shape[1]
    width = -(-n // (rows * LANES)) * LANES
    return jnp.pad(flat, ((0, 0), (0, rows * width - n))).reshape(rows, width)


def _unflat(vec, shapes):
    flat = vec.reshape(-1)
    out, off = [], 0
    for s in shapes:
        n = 1
        for d in s:
            n *= d
        out.append(flat[off:off + n].reshape(s))
        off += n
    return out


def kernel(x, c, w_ada, b_ada, g_pre_mix, g_post_mix, w_in, g_q_lat, w_uq, g_kv_lat, w_ukv, rel_bias, sinks, w_o, g_pre_ffn, g_post_ffn, w_up, conv_w, conv_b, w_down, loss_target, m_w_ada, m_b_ada, m_g_pre_mix, m_g_post_mix, m_w_in, m_g_q_lat, m_w_uq, m_g_kv_lat, m_w_ukv, m_rel_bias, m_sinks, m_w_o, m_g_pre_ffn, m_g_post_ffn, m_w_up, m_conv_w, m_conv_b, m_w_down, v_w_ada, v_b_ada, v_g_pre_mix, v_g_post_mix, v_w_in, v_g_q_lat, v_w_uq, v_g_kv_lat, v_w_ukv, v_rel_bias, v_sinks, v_w_o, v_g_pre_ffn, v_g_post_ffn, v_w_up, v_conv_w, v_conv_b, v_w_down):
    S, D = x.shape[1], x.shape[2]
    Rq, Rkv = g_q_lat.shape[1], g_kv_lat.shape[1]
    H = D // MLA_V
    NH = D // SWA_HD
    KW = SWA_KVH * SWA_HD
    F = w_down.shape[1] * N_CHIP
    xi, yi, ci = lax.axis_index("x"), lax.axis_index("y"), lax.axis_index("c")
    chip = 2 * xi + yi
    me = 2 * chip + ci
    x2, tgt = x[0], loss_target[0]

    c_all = _allgather8(jnp.broadcast_to(c, (8, D)), name="gather_c")[:, 0, :]
    n3 = w_ada.shape[2]
    mod_part = _ada_fwd(c_all, w_ada[0], lax.dynamic_slice(b_ada, (0, chip * n3), (1, n3)), name="ada_fwd")
    mod_all = _allgather8(mod_part, name="gather_mod")
    mod_me = lax.dynamic_index_in_dim(mod_all[0::2], me, axis=1, keepdims=False).reshape(1, 6 * D)
    sh1, sc1, gt1, sh2, sc2, gt2 = [mod_me[:, k * D:(k + 1) * D] for k in range(6)]

    swaps = [False, False, False, False, True, False]
    local = [w_in[0].astype(BF16)]
    send_a, recv_a, srcs_a, lands_a, token = _gather_start(local, (mod_all,), swaps[:1], name="gather_start_in")
    rest, token = lax.optimization_barrier(((w_uq[0], w_ukv[0], w_o[0], w_up[0], w_down[0]), token))
    local += [w.astype(BF16) for w in rest]
    send_b, recv_b, srcs_b, lands_b, token = _gather_start(local[1:], (token,), swaps[1:], name="gather_start_rest")
    send1, recv1, srcs, lands = send_a + send_b, recv_a + recv_b, srcs_a + srcs_b, lands_a + lands_b
    onehot, negmask = _rel_tables()
    npb = NH // (2 * SWA_KVH)
    rb_st = jnp.transpose(rel_bias.T.reshape(SWA_KVH, npb, 2, REL_BUCKETS), (0, 2, 1, 3)).reshape(NH, REL_BUCKETS)
    bias_m = (_matmul(rb_st, onehot.T, tie=token, name="rel_bias_table") + negmask).reshape(
        2 * SWA_KVH, npb * SWA_BLOCK, 2 * SWA_BLOCK)
    h = _modnorm_fwd(x2, g_pre_mix, sc1, sh1, name="pre_mix_norm")

    def whole(land, i):
        return lax.dynamic_update_index_in_dim(land, local[i], _slot(chip, swaps[i]), 0)

    def conv_slots(v):
        return jnp.stack([v[0], v[2], v[1], v[3]])

    s2, r2, l_in, _ = _gather_forward(send1[:1], recv1[:1], srcs[:1], lands[:1], h, swaps[:1], name="gather_forward_in")
    (l_in,) = _gather_finish(s2, r2, l_in, h, name="gather_finish_in")
    gin = whole(l_in, 0)
    o_kr = Rq + Rkv
    o_q = o_kr + MLA_ROPE
    o_g = o_q + NH * SWA_HD + 2 * KW
    n_gate, n_swa = 2 * D, o_g - o_q
    n_lat = -(-(o_q + MLA_ROPE) // PAIR) * PAIR
    runs = []
    for tl in range(D // PAIR):
        runs.append((o_g + tl * PAIR, o_g + (tl + 1) * PAIR, 2 * tl * PAIR))
        runs.append((o_g + D + tl * PAIR, o_g + D + (tl + 1) * PAIR, (2 * tl + 1) * PAIR))
    runs.append((o_q, o_g, n_gate))
    runs.append((0, o_q, n_gate + n_swa))
    csh = gin.shape[2]
    parts = []
    for lo, hi, _ in runs + [(o_kr, o_q, 0)]:
        parts += _shard_cols([gin[j] for j in range(N_CHIP)], lo, hi, csh)
    parts.append(jnp.zeros((D, n_lat - o_q - MLA_ROPE), BF16))
    w_in_all = jnp.concatenate(parts, axis=1)
    cos, sin = _rope_tables(S)
    sink_v = sinks.reshape(NH)

    z_lat = _matmul(h, w_in_all, bcols=(n_gate + n_swa, n_lat), name="in_proj_lat")
    z_swa = _matmul(h, w_in_all, bcols=(n_gate, n_swa), name="in_proj_swa")
    zg = _matmul(h, w_in_all, bcols=(0, n_gate), out_dtype=BF16, name="in_proj_gate")
    s2b, r2b, l_b, _ = _gather_forward(send1[1:4], recv1[1:4], srcs[1:4], lands[1:4], zg, swaps[1:4],
                                       name="gather_forward_attn")
    nq, nkv = _lat_norm_fwd(z_lat, g_q_lat, g_kv_lat, name="lat_norm")
    l_uq, l_ukv, l_o = _gather_finish(s2b, r2b, l_b, nq, name="gather_finish_attn")
    wuq = _split_heads(_from_col_shards(whole(l_uq, 1)), (MLA_NOPE, MLA_ROPE))
    wukv = _split_heads(_from_col_shards(whole(l_ukv, 2)), (MLA_NOPE, MLA_V))
    wo = whole(l_o, 3).reshape(D, D)
    q_raw = _matmul(nq, wuq, out_dtype=BF16, name="uq_proj")
    kv_raw = _matmul(nkv, wukv, out_dtype=BF16, name="ukv_proj")
    qp, kp, vv = _mla_pack_fwd(q_raw, kv_raw, z_lat, cos, sin, o_kr, name="mla_pack")
    o_a, lse = _flash_fwd(qp, kp, vv, name="mla_attn")
    s2c, r2c, l_c, tok_c = _gather_forward(send1[4:], recv1[4:], srcs[4:], lands[4:], o_a, swaps[4:],
                                           name="gather_forward_ffn")
    o_b = _swa_fwd(z_swa, bias_m, sink_v, name="swa_attn")
    mixin = _gate_fwd(zg, o_a, o_b, name="gate_mix")
    mix = _matmul(mixin, wo, tie=tok_c, name="o_proj")
    x1, h2 = _resnorm_modnorm_fwd(x2, mix, g_post_mix, gt1, g_pre_ffn, sc2, sh2, name="post_mix_pre_ffn_norm")
    l_up, l_down = _gather_finish(s2c, r2c, l_c, h2, name="gather_finish_ffn")
    cw_all = _allgather8(jnp.pad(conv_w[0], ((0, 5), (0, 0))), tie=l_down, name="gather_conv_w")[0::2, :3]
    cw = conv_slots(cw_all)
    cb = conv_slots(conv_b.reshape(N_CHIP, 1, -1))
    wup = whole(l_up, 4)
    wdown = whole(l_down, 5).reshape(F, D)
    t = _matmul(h2, wup, out_dtype=BF16, shards="out", name="up_proj")
    a = _conv_gate_fwd(t, cw, cb, name="conv_gate")
    yv = _matmul(a, wdown, name="down_proj")
    dout, dy, dg_post_ffn, dgt2, loss_tile = _resnorm_loss(x1, yv, g_post_ffn, gt2, tgt, name="post_ffn_norm_loss")

    big_params = dict(w_in=(w_in, m_w_in, v_w_in), w_uq=(w_uq, m_w_uq, v_w_uq), w_ukv=(w_ukv, m_w_ukv, v_w_ukv),
                      w_o=(w_o, m_w_o, v_w_o), w_up=(w_up, m_w_up, v_w_up), w_down=(w_down, m_w_down, v_w_down))
    res = {}

    def start(nms, gs):
        sw = [nm == "w_up" for nm in nms]
        send, recv, gsrc, glands, tok = _scatter_start(gs, sw, name="grads_start_" + nms[0])
        return (nms, send, recv, gsrc, glands), tok

    def finish(pendings, after):
        nms, send, recv, gsrc, glands = [sum((list(p[k]) for p in pendings), []) for k in range(5)]
        gsrc, glands = _scatter_wait(send, recv, gsrc, glands, after, name="grads_wait_" + nms[0])
        halves = [_sum_pieces(l, g, jnp.stack([me, _slot(chip, nm == "w_up"), ci]).astype(jnp.int32), name="grad_sum_" + nm)
                  for l, g, nm in zip(glands, gsrc, nms)]
        for nm, g in zip(nms, _share_halves(halves, name="grads_share_" + nms[0])):
            w, m, v = big_params[nm]
            res[nm] = (g,) + tuple(_adamw(w[0], g, m[0], v[0], name="adamw_" + nm))

    dw_down = _matmul(a, dy, ta=True, out_dtype=BF16, name="down_proj_dw")
    p_down, tok = start(["w_down"], [dw_down.reshape(N_CHIP, F // N_CHIP, D)])
    da = _matmul(dy, wdown, tb=True, out_dtype=BF16, tie=tok, name="down_proj_dx")
    du, dcw_s, dcb_s = _conv_gate_bwd(t, da, cw, cb, name="conv_gate_bwd")
    dt = _conv_bwd_dt(du, cw, name="conv_bwd_dt")
    dw_up = _matmul(h2, dt, ta=True, out_dtype=BF16, shards="out", name="up_proj_dw")
    p_up, tok = start(["w_up"], [dw_up])
    dh2 = _matmul(dt, wup, tb=True, tie=tok, shards="k", name="up_proj_dx")
    dx1, dg_pre_ffn, dsc2, dsh2, dmix, dg_post_mix, dgt1 = _modnorm_resnorm_bwd(
        dh2, x1, g_pre_ffn, sc2, dout, mix, g_post_mix, gt1, name="pre_ffn_post_mix_norm_bwd")
    dw_o = _matmul(mixin, dmix, ta=True, out_dtype=BF16, name="o_proj_dw")
    p_o, tok = start(["w_o"], [dw_o.reshape(N_CHIP, D // N_CHIP, D)])
    dmixin = _matmul(dmix, wo, tb=True, out_dtype=BF16, tie=tok, name="o_proj_dx")
    do_a, do_b, dzg = _gate_bwd(dmixin, zg, o_a, o_b, name="gate_mix_bwd")
    dqp, dkp, dvv = _flash_bwd(qp, kp, vv, o_a, do_a, lse, name="mla_attn_bwd")
    dq_raw, dkv_raw, dkr = _mla_pack_bwd(dqp, dkp, dvv, cos, sin, name="mla_pack_bwd")
    dw_uq_p = _matmul(nq, dq_raw, ta=True, out_dtype=BF16, name="uq_proj_dw")
    dw_ukv_p = _matmul(nkv, dkv_raw, ta=True, out_dtype=BF16, name="ukv_proj_dw")
    p_qkv, tok = start(["w_uq", "w_ukv"], [_to_col_shards(_merge_heads(dw_uq_p, (MLA_NOPE, MLA_ROPE))),
                                           _to_col_shards(_merge_heads(dw_ukv_p, (MLA_NOPE, MLA_V)))])
    dnq = _matmul(dq_raw, wuq, tb=True, tie=tok, name="uq_proj_dx")
    dnkv = _matmul(dkv_raw, wukv, tb=True, name="ukv_proj_dx")
    dz_lat, dg_q, dg_kv = _lat_norm_bwd(z_lat, dnq, dnkv, dkr, g_q_lat, g_kv_lat, name="lat_norm_bwd")
    dz_swa, dbias, dsink = _swa_bwd(z_swa, bias_m, sink_v, o_b, do_b, name="swa_attn_bwd")
    dz = jnp.concatenate([dzg, dz_swa, dz_lat], axis=1)
    dw_in_p = _matmul(h, dz, ta=True, out_dtype=BF16, name="in_proj_dw")
    dw_shards = []
    for j in range(N_CHIP):
        cols = []
        for lo, hi, at in sorted(runs):
            a0, a1 = max(lo, j * csh), min(hi, (j + 1) * csh)
            if a0 < a1:
                cols.append(dw_in_p[:, at + a0 - lo:at + a1 - lo])
        dw_shards.append(jnp.concatenate(cols, axis=1))
    p_in, tok = start(["w_in"], [jnp.stack(dw_shards)])
    dh = _matmul(dz, w_in_all, tb=True, tie=tok, name="in_proj_dx")
    grad_x, dg_pre_mix, dsc1, dsh1 = _modnorm_bwd(dh, x2, g_pre_mix, sc1, dx1, name="pre_mix_norm_bwd")
    drel_st = _matmul(dbias.reshape(NH, -1), onehot, tie=grad_x, name="rel_bias_bwd")
    finish((p_down, p_up, p_o, p_qkv), drel_st)
    drel = jnp.transpose(drel_st.reshape(SWA_KVH, 2, npb, REL_BUCKETS), (0, 2, 1, 3)).reshape(NH, REL_BUCKETS).T

    dcw = _from_col_shards(conv_slots(dcw_s))
    dcb = conv_slots(dcb_s).reshape(1, -1)
    dmod = jnp.concatenate([dsh1, dsc1, dgt1, dsh2, dsc2, dgt2], axis=1)
    small = [dmod, dg_pre_mix, dg_post_mix, dg_pre_ffn, dg_post_ffn, dg_q, dg_kv, drel, dsink[:, :NH], dcb, dcw]
    shapes = [p.shape for p in small]
    done = [res[nm][1] for nm in ("w_down", "w_up", "w_o", "w_uq", "w_ukv")]
    small_all = _allgather8(_flat_pad(small), tie=done, name="gather_small_grads")
    tot = _unflat(_sum_devices(small_all, name="sum_small_grads"), shapes)
    g_b_ada, g_pre_mix_g, g_post_mix_g, g_pre_ffn_g, g_post_ffn_g, g_q_g, g_kv_g, g_rel, g_sinks, g_cb, g_cw_full = tot
    dmod_all = small_all.reshape(N_DEV, -1)[:, :6 * D]
    g_w_ada = _ada_bwd(c_all.T, lax.dynamic_slice(dmod_all, (0, chip * n3), (N_DEV, n3)), name="ada_bwd")
    ncw = conv_w.shape[2]
    g_cw = lax.dynamic_slice(g_cw_full, (0, chip * ncw), (3, ncw))

    res["w_ada"] = (g_w_ada,) + tuple(_adamw(w_ada[0], g_w_ada, m_w_ada[0], v_w_ada[0], name="adamw_w_ada"))
    finish((p_in,), g_w_ada)
    snames = ["b_ada", "g_pre_mix", "g_post_mix", "g_pre_ffn", "g_post_ffn", "g_q_lat", "g_kv_lat", "rel_bias", "sinks",
              "conv_b", "conv_w"]
    sw = [b_ada, g_pre_mix, g_post_mix, g_pre_ffn, g_post_ffn, g_q_lat, g_kv_lat, rel_bias, sinks, conv_b, conv_w]
    sm = [m_b_ada, m_g_pre_mix, m_g_post_mix, m_g_pre_ffn, m_g_post_ffn, m_g_q_lat, m_g_kv_lat, m_rel_bias, m_sinks,
          m_conv_b, m_conv_w]
    sv = [v_b_ada, v_g_pre_mix, v_g_post_mix, v_g_pre_ffn, v_g_post_ffn, v_g_q_lat, v_g_kv_lat, v_rel_bias, v_sinks,
          v_conv_b, v_conv_w]
    sg = [g_b_ada, g_pre_mix_g, g_post_mix_g, g_pre_ffn_g, g_post_ffn_g, g_q_g, g_kv_g, g_rel, g_sinks, g_cb, g_cw]
    sshapes = [w.shape for w in sw]
    sd, snm, snv = _adamw(_flat_pad(sw), _flat_pad(sg), _flat_pad(sm), _flat_pad(sv), name="adamw_small")
    sd, snm, snv = _unflat(sd, sshapes), _unflat(snm, sshapes), _unflat(snv, sshapes)
    for k, nm in enumerate(snames):
        res[nm] = (sg[k].reshape(sshapes[k]), sd[k], snm[k], snv[k])

    order = ["w_ada", "b_ada", "g_pre_mix", "g_post_mix", "w_in", "g_q_lat", "w_uq", "g_kv_lat", "w_ukv", "rel_bias", "sinks",
             "w_o", "g_pre_ffn", "g_post_ffn", "w_up", "conv_w", "conv_b", "w_down"]
    ref_shapes = dict(w_ada=w_ada.shape, w_in=w_in.shape, w_uq=w_uq.shape, w_ukv=w_ukv.shape, w_o=w_o.shape,
                      w_up=w_up.shape, w_down=w_down.shape)
    outs = []
    for k in range(4):
        for nm in order:
            arr = res[nm][k]
            outs.append(arr.reshape(ref_shapes[nm]) if nm in ref_shapes else arr)
    loss = lax.psum(loss_tile[0, 0], ("x", "y", "c"))
    return (loss, grad_x[None], *outs)
```

```python
import math

import jax
import jax.numpy as jnp
from jax import lax
from jax.experimental import pallas as pl
from jax.experimental.pallas import tpu as pltpu

F32 = jnp.float32
BF16 = jnp.bfloat16
MESH = pl.DeviceIdType.MESH
HIGHEST = lax.Precision.HIGHEST

N_DEV = 8
N_CHIP = 4
LANES = 128
MLA_NOPE = 128
MLA_ROPE = 64
MLA_V = 128
MLA_QK = MLA_NOPE + MLA_ROPE
MLA_QK_PAD = 256
ROPE_THETA = 10000.0
SWA_HD = 64
SWA_KVH = 4
SWA_BLOCK = 128
REL_BUCKETS = 32
REL_MAX_DIST = 128
PAIR = 512
EPS = 1e-6
NEG = -1e30
ADAM_LR = 0.001
ADAM_B1 = 0.9
ADAM_B2 = 0.999
ADAM_EPS = 1e-08
ADAM_WD = 0.01
ADAM_STEP = 10

ANY = pl.BlockSpec(memory_space=pl.ANY)
VMEM_FULL = pl.BlockSpec(memory_space=pltpu.VMEM)
SMEM_FULL = pl.BlockSpec(memory_space=pltpu.SMEM)


def _params(*sem):
    return pltpu.CompilerParams(dimension_semantics=sem if sem else None)


def _tied(body, tie):
    if tie is None:
        return body, [], []
    ties = list(tie) if isinstance(tie, (list, tuple)) else [tie]

    def tied_body(*refs):
        body(*refs[len(ties):])

    return tied_body, [ANY] * len(ties), ties


def _tile(n, pref, unit=LANES):
    best = None
    for t in range(unit, min(n, pref) + 1, unit):
        if n % t == 0:
            best = t
    return n if best is None else best


def _matmul(a, b, *, ta=False, tb=False, out_dtype=F32, tie=None, shards=None, bcols=None, name):
    a2 = a.shape[1:] if shards == "k" else a.shape
    b2 = b.shape[1:] if shards else b.shape
    nsh = b.shape[0] if shards else 1
    K, M = a2 if ta else a2[::-1]
    N, K2 = b2 if tb else b2[::-1]
    assert K == K2, (a.shape, b.shape, ta, tb)
    exact = a.dtype == F32
    col0 = 0
    if bcols is not None:
        assert not tb and shards is None
        col0, N = bcols
    tn = _tile(math.gcd(N, col0) if col0 else N, 2048)
    col0 //= tn
    tk = _tile(K, 2048)
    nkc = K // tk
    nk = nkc * (nsh if shards == "k" else 1)
    tm = M if M < 8 else _tile(M, 1024, LANES if ta else 8)
    dn = (((0 if ta else 1,), (1 if tb else 0,)), ((), ()))
    kax = 3 if shards == "out" else 2

    def product(a_ref, b_ref):
        return lax.dot_general(a_ref[...], b_ref[...], dn, preferred_element_type=F32,
                               precision=HIGHEST if exact else None)

    def body_acc(a_ref, b_ref, o_ref, acc_ref):
        k = pl.program_id(kax)

        @pl.when(k == 0)
        def _():
            acc_ref[...] = product(a_ref, b_ref)

        @pl.when(jnp.logical_and(k > 0, k < nk - 1))
        def _():
            acc_ref[...] += product(a_ref, b_ref)

        @pl.when(k == nk - 1)
        def _():
            o_ref[...] = (acc_ref[...] + product(a_ref, b_ref)).astype(o_ref.dtype)

    def body_one(a_ref, b_ref, o_ref):
        o_ref[...] = product(a_ref, b_ref).astype(o_ref.dtype)

    a_blk, b_blk = ((tk, tm) if ta else (tm, tk)), ((tn, tk) if tb else (tk, tn))
    a_at = (lambda i, k: (k, i)) if ta else (lambda i, k: (i, k))
    b_at = (lambda j, k: (j, k)) if tb else (lambda j, k: (k, j + col0))
    if shards == "out":
        grid = (nsh, M // tm, N // tn, nk)
        a_spec = pl.BlockSpec(a_blk, lambda s, i, j, k: a_at(i, k))
        b_spec = pl.BlockSpec((None,) + b_blk, lambda s, i, j, k: (s,) + b_at(j, k))
        o_spec = pl.BlockSpec((None, tm, tn), lambda s, i, j, k: (s, i, j))
        out_shape = jax.ShapeDtypeStruct((nsh, M, N), out_dtype)
        sem = ("parallel", "parallel", "parallel", "arbitrary")
    elif shards == "k":
        grid = (M // tm, N // tn, nk)
        a_spec = pl.BlockSpec((None,) + a_blk, lambda i, j, k: (k // nkc,) + a_at(i, k % nkc))
        b_spec = pl.BlockSpec((None,) + b_blk, lambda i, j, k: (k // nkc,) + b_at(j, k % nkc))
        o_spec = pl.BlockSpec((tm, tn), lambda i, j, k: (i, j))
        out_shape = jax.ShapeDtypeStruct((M, N), out_dtype)
        sem = ("parallel", "parallel", "arbitrary")
    else:
        grid = (M // tm, N // tn, nk)
        a_spec = pl.BlockSpec(a_blk, lambda i, j, k: a_at(i, k))
        b_spec = pl.BlockSpec(b_blk, lambda i, j, k: b_at(j, k))
        o_spec = pl.BlockSpec((tm, tn), lambda i, j, k: (i, j))
        out_shape = jax.ShapeDtypeStruct((M, N), out_dtype)
        sem = ("parallel", "parallel", "arbitrary")
    body, tspec, targ = _tied(body_one if nk == 1 else body_acc, tie)
    return pl.pallas_call(
        body, name=name, out_shape=out_shape, grid=grid, in_specs=tspec + [a_spec, b_spec], out_specs=o_spec,
        scratch_shapes=[] if nk == 1 else [pltpu.VMEM((tm, tn), F32)],
        compiler_params=_params(*sem),
    )(*targ, a, b)


def _row_tile(S, width):
    return _tile(S, max(8, (1 << 19) // width), 8)


def _rstd(x):
    return lax.rsqrt(jnp.mean(x * x, axis=-1, keepdims=True) + EPS)


def _acc_rows(ref, val, first):
    s = jnp.sum(val, axis=0, keepdims=True)

    @pl.when(first)
    def _():
        ref[...] = s

    @pl.when(jnp.logical_not(first))
    def _():
        ref[...] += s


def _modnorm_fwd(x, g, sc, sh, *, name):
    S, D = x.shape
    tr = _row_tile(S, D)

    def body(x_ref, g_ref, sc_ref, sh_ref, h_ref):
        xv = x_ref[...]
        n = (xv * _rstd(xv)) * g_ref[...]
        h_ref[...] = (n * (1.0 + sc_ref[...]) + sh_ref[...]).astype(BF16)

    row = pl.BlockSpec((tr, D), lambda i: (i, 0))
    vec = pl.BlockSpec((1, D), lambda i: (0, 0))
    return pl.pallas_call(
        body, name=name, out_shape=jax.ShapeDtypeStruct((S, D), BF16), grid=(S // tr,),
        in_specs=[row, vec, vec, vec], out_specs=row, compiler_params=_params("parallel"),
    )(x, g, sc, sh)


def _modnorm_bwd(dh, x, g, sc, dres, *, name):
    S, D = x.shape
    tr = _row_tile(S, D)

    def body(dh_ref, x_ref, g_ref, sc_ref, dres_ref, dx_ref, dg_ref, dsc_ref, dsh_ref):
        first = pl.program_id(0) == 0
        xv = x_ref[...]
        dhv = dh_ref[...]
        gv = g_ref[...]
        r = _rstd(xv)
        xhat = xv * r
        _acc_rows(dsh_ref, dhv, first)
        _acc_rows(dsc_ref, dhv * (xhat * gv), first)
        dn = dhv * (1.0 + sc_ref[...])
        _acc_rows(dg_ref, dn * xhat, first)
        dxhat = dn * gv
        proj = jnp.mean(dxhat * xhat, axis=-1, keepdims=True)
        dx_ref[...] = r * (dxhat - xhat * proj) + dres_ref[...]

    row = pl.BlockSpec((tr, D), lambda i: (i, 0))
    vec = pl.BlockSpec((1, D), lambda i: (0, 0))
    vshape = jax.ShapeDtypeStruct((1, D), F32)
    return pl.pallas_call(
        body, name=name,
        out_shape=(jax.ShapeDtypeStruct((S, D), F32), vshape, vshape, vshape), grid=(S // tr,),
        in_specs=[row, row, vec, vec, row], out_specs=(row, vec, vec, vec),
        compiler_params=_params("arbitrary"),
    )(dh, x, g, sc, dres)


def _resnorm_modnorm_fwd(xres, m, g, gt, g2, sc2, sh2, *, name):
    S, D = xres.shape
    tr = _row_tile(S, D)

    def body(x_ref, m_ref, g_ref, gt_ref, g2_ref, sc_ref, sh_ref, o_ref, h_ref):
        mv = m_ref[...]
        x1 = x_ref[...] + gt_ref[...] * ((mv * _rstd(mv)) * g_ref[...])
        o_ref[...] = x1
        n = (x1 * _rstd(x1)) * g2_ref[...]
        h_ref[...] = (n * (1.0 + sc_ref[...]) + sh_ref[...]).astype(BF16)

    row = pl.BlockSpec((tr, D), lambda i: (i, 0))
    vec = pl.BlockSpec((1, D), lambda i: (0, 0))
    return pl.pallas_call(
        body, name=name, out_shape=(jax.ShapeDtypeStruct((S, D), F32), jax.ShapeDtypeStruct((S, D), BF16)), grid=(S // tr,),
        in_specs=[row, row, vec, vec, vec, vec, vec], out_specs=(row, row), compiler_params=_params("parallel"),
    )(xres, m, g, gt, g2, sc2, sh2)


def _modnorm_resnorm_bwd(dh, x, g, sc, dres, m, g1, gt1, *, name):
    S, D = x.shape
    tr = _row_tile(S, D)

    def body(dh_ref, x_ref, g_ref, sc_ref, dres_ref, m_ref, g1_ref, gt1_ref,
             dx_ref, dg_ref, dsc_ref, dsh_ref, dm_ref, dg1_ref, dgt1_ref):
        first = pl.program_id(0) == 0
        xv = x_ref[...]
        dhv = dh_ref[...]
        gv = g_ref[...]
        r = _rstd(xv)
        xhat = xv * r
        _acc_rows(dsh_ref, dhv, first)
        _acc_rows(dsc_ref, dhv * (xhat * gv), first)
        dn = dhv * (1.0 + sc_ref[...])
        _acc_rows(dg_ref, dn * xhat, first)
        dxhat = dn * gv
        proj = jnp.mean(dxhat * xhat, axis=-1, keepdims=True)
        dx1 = r * (dxhat - xhat * proj) + dres_ref[...]
        dx_ref[...] = dx1
        mv = m_ref[...]
        g1v = g1_ref[...]
        r1 = _rstd(mv)
        mhat = mv * r1
        _acc_rows(dgt1_ref, dx1 * (mhat * g1v), first)
        dn1 = dx1 * gt1_ref[...]
        _acc_rows(dg1_ref, dn1 * mhat, first)
        dmhat = dn1 * g1v
        proj1 = jnp.mean(dmhat * mhat, axis=-1, keepdims=True)
        dm_ref[...] = (r1 * (dmhat - mhat * proj1)).astype(BF16)

    row = pl.BlockSpec((tr, D), lambda i: (i, 0))
    vec = pl.BlockSpec((1, D), lambda i: (0, 0))
    vshape = jax.ShapeDtypeStruct((1, D), F32)
    return pl.pallas_call(
        body, name=name,
        out_shape=(jax.ShapeDtypeStruct((S, D), F32), vshape, vshape, vshape, jax.ShapeDtypeStruct((S, D), BF16), vshape, vshape),
        grid=(S // tr,),
        in_specs=[row, row, vec, vec, row, row, vec, vec], out_specs=(row, vec, vec, vec, row, vec, vec),
        compiler_params=_params("arbitrary"),
    )(dh, x, g, sc, dres, m, g1, gt1)


def _resnorm_loss(xres, m, g, gt, target, *, name):
    S, D = xres.shape
    tr = _row_tile(S, D)

    def body(x_ref, m_ref, g_ref, gt_ref, t_ref, d_ref, dm_ref, dg_ref, dgt_ref, loss_ref):
        first = pl.program_id(0) == 0
        mv = m_ref[...]
        gv = g_ref[...]
        r = _rstd(mv)
        mhat = mv * r
        n = mhat * gv
        err = (x_ref[...] + gt_ref[...] * n) - t_ref[...]
        dv = err * (1.0 / D)
        d_ref[...] = dv
        part = 0.5 * jnp.sum(jnp.mean(err * err, axis=-1, keepdims=True), axis=0, keepdims=True)
        part = jnp.broadcast_to(part, loss_ref.shape)

        @pl.when(first)
        def _():
            loss_ref[...] = part

        @pl.when(jnp.logical_not(first))
        def _():
            loss_ref[...] += part

        _acc_rows(dgt_ref, dv * n, first)
        dn = dv * gt_ref[...]
        _acc_rows(dg_ref, dn * mhat, first)
        dmhat = dn * gv
        proj = jnp.mean(dmhat * mhat, axis=-1, keepdims=True)
        dm_ref[...] = (r * (dmhat - mhat * proj)).astype(BF16)

    row = pl.BlockSpec((tr, D), lambda i: (i, 0))
    vec = pl.BlockSpec((1, D), lambda i: (0, 0))
    vshape = jax.ShapeDtypeStruct((1, D), F32)
    return pl.pallas_call(
        body, name=name,
        out_shape=(jax.ShapeDtypeStruct((S, D), F32), jax.ShapeDtypeStruct((S, D), BF16), vshape, vshape,
                   jax.ShapeDtypeStruct((8, LANES), F32)), grid=(S // tr,),
        in_specs=[row, row, vec, vec, row], out_specs=(row, row, vec, vec, pl.BlockSpec((8, LANES), lambda i: (0, 0))),
        compiler_params=_params("arbitrary"),
    )(xres, m, g, gt, target)


def _lat_norm_fwd(z_lat, g_q, g_kv, *, name):
    S, W = z_lat.shape
    Rq, Rkv = g_q.shape[1], g_kv.shape[1]
    tr = _row_tile(S, W)

    def body(z_ref, gq_ref, gkv_ref, nq_ref, nkv_ref):
        cq = z_ref[:, :Rq]
        ckv = z_ref[:, Rq:Rq + Rkv]
        nq_ref[...] = ((cq * _rstd(cq)) * gq_ref[...]).astype(BF16)
        nkv_ref[...] = ((ckv * _rstd(ckv)) * gkv_ref[...]).astype(BF16)

    return pl.pallas_call(
        body, name=name,
        out_shape=(jax.ShapeDtypeStruct((S, Rq), BF16), jax.ShapeDtypeStruct((S, Rkv), BF16)), grid=(S // tr,),
        in_specs=[pl.BlockSpec((tr, W), lambda i: (i, 0)), pl.BlockSpec((1, Rq), lambda i: (0, 0)),
                  pl.BlockSpec((1, Rkv), lambda i: (0, 0))],
        out_specs=(pl.BlockSpec((tr, Rq), lambda i: (i, 0)), pl.BlockSpec((tr, Rkv), lambda i: (i, 0))),
        compiler_params=_params("parallel"),
    )(z_lat, g_q, g_kv)


def _lat_norm_bwd(z_lat, dnq, dnkv, dkr, g_q, g_kv, *, name):
    S, W = z_lat.shape
    Rq, Rkv = g_q.shape[1], g_kv.shape[1]
    tr = _row_tile(S, W)

    def one(c, dn, gv):
        r = _rstd(c)
        chat = c * r
        dchat = dn * gv
        proj = jnp.mean(dchat * chat, axis=-1, keepdims=True)
        return r * (dchat - chat * proj), dn * chat

    def body(z_ref, dnq_ref, dnkv_ref, dkr_ref, gq_ref, gkv_ref, dz_ref, dgq_ref, dgkv_ref):
        first = pl.program_id(0) == 0
        dcq, pq = one(z_ref[:, :Rq], dnq_ref[...], gq_ref[...])
        dckv, pkv = one(z_ref[:, Rq:Rq + Rkv], dnkv_ref[...], gkv_ref[...])
        _acc_rows(dgq_ref, pq, first)
        _acc_rows(dgkv_ref, pkv, first)
        dz_ref[:, :Rq] = dcq.astype(BF16)
        dz_ref[:, Rq:Rq + Rkv] = dckv.astype(BF16)
        dz_ref[:, Rq + Rkv:Rq + Rkv + LANES] = dkr_ref[...].astype(BF16)
        if W > Rq + Rkv + LANES:
            dz_ref[:, Rq + Rkv + LANES:] = jnp.zeros((tr, W - Rq - Rkv - LANES), BF16)

    return pl.pallas_call(
        body, name=name,
        out_shape=(jax.ShapeDtypeStruct((S, W), BF16), jax.ShapeDtypeStruct((1, Rq), F32),
                   jax.ShapeDtypeStruct((1, Rkv), F32)), grid=(S // tr,),
        in_specs=[pl.BlockSpec((tr, W), lambda i: (i, 0)), pl.BlockSpec((tr, Rq), lambda i: (i, 0)),
                  pl.BlockSpec((tr, Rkv), lambda i: (i, 0)), pl.BlockSpec((tr, LANES), lambda i: (i, 0)),
                  pl.BlockSpec((1, Rq), lambda i: (0, 0)), pl.BlockSpec((1, Rkv), lambda i: (0, 0))],
        out_specs=(pl.BlockSpec((tr, W), lambda i: (i, 0)), pl.BlockSpec((1, Rq), lambda i: (0, 0)),
                   pl.BlockSpec((1, Rkv), lambda i: (0, 0))),
        compiler_params=_params("arbitrary"),
    )(z_lat, dnq, dnkv, dkr, g_q, g_kv)


def _rot(x, lo32):
    a = pltpu.roll(x, 32, 1)
    b = pltpu.roll(x, LANES - 32, 1)
    return jnp.where(lo32, -b, a)


def _rot_t(g, lo32):
    a = pltpu.roll(g, 32, 1)
    b = pltpu.roll(g, LANES - 32, 1)
    return jnp.where(lo32, b, -a)


def _mla_pack_fwd(q_raw, kv_raw, z_lat, cos, sin, kr_off, *, name):
    S = q_raw.shape[0]
    H = kv_raw.shape[1] // (MLA_NOPE + MLA_V)
    assert kr_off % LANES == 0
    scale = MLA_QK ** -0.5
    tr = min(S, 128)
    nope_w = H * MLA_NOPE

    def body(q_ref, kv_ref, z_ref, cos_ref, sin_ref, qp_ref, kp_ref, v_ref):
        lane = lax.broadcasted_iota(jnp.int32, (tr, LANES), 1)
        lo32 = (lane % 64) < 32
        lo64 = lane < 64
        c = cos_ref[...]
        s = sin_ref[...]
        kr = z_ref[...]
        kr = (kr * c + _rot(kr, lo32) * s).astype(BF16)
        for hp in range(H // 2):
            xb = q_ref[:, nope_w + hp * LANES:nope_w + (hp + 1) * LANES].astype(F32)
            rb = (xb * c + _rot(xb, lo32) * s) * scale
            for e in range(2):
                h = 2 * hp + e
                base = h * MLA_QK_PAD
                qp_ref[:, base:base + LANES] = (q_ref[:, h * LANES:(h + 1) * LANES].astype(F32) * scale).astype(BF16)
                keep = lo64 if e == 0 else jnp.logical_not(lo64)
                qp_ref[:, base + LANES:base + 2 * LANES] = jnp.where(keep, rb, 0.0).astype(BF16)
                kp_ref[:, base:base + LANES] = kv_ref[:, h * LANES:(h + 1) * LANES].astype(BF16)
                kp_ref[:, base + LANES:base + 2 * LANES] = kr
        v_ref[...] = kv_ref[:, nope_w:].astype(BF16)

    return pl.pallas_call(
        body, name=name,
        out_shape=(jax.ShapeDtypeStruct((S, H * MLA_QK_PAD), BF16), jax.ShapeDtypeStruct((S, H * MLA_QK_PAD), BF16),
                   jax.ShapeDtypeStruct((S, H * MLA_V), BF16)), grid=(S // tr,),
        in_specs=[pl.BlockSpec((tr, q_raw.shape[1]), lambda i: (i, 0)), pl.BlockSpec((tr, kv_raw.shape[1]), lambda i: (i, 0)),
                  pl.BlockSpec((tr, LANES), lambda i: (i, kr_off // LANES)), pl.BlockSpec((tr, LANES), lambda i: (i, 0)),
                  pl.BlockSpec((tr, LANES), lambda i: (i, 0))],
        out_specs=(pl.BlockSpec((tr, H * MLA_QK_PAD), lambda i: (i, 0)), pl.BlockSpec((tr, H * MLA_QK_PAD), lambda i: (i, 0)),
                   pl.BlockSpec((tr, H * MLA_V), lambda i: (i, 0))),
        compiler_params=_params("parallel"),
    )(q_raw, kv_raw, z_lat, cos, sin)


def _mla_pack_bwd(dqp, dkp, dv, cos, sin, *, name):
    S = dqp.shape[0]
    H = dv.shape[1] // MLA_V
    scale = MLA_QK ** -0.5
    tr = min(S, 128)
    nope_w = H * MLA_NOPE

    def body(dqp_ref, dkp_ref, dv_ref, cos_ref, sin_ref, dq_ref, dkv_ref, dkr_ref):
        lane = lax.broadcasted_iota(jnp.int32, (tr, LANES), 1)
        lo32 = (lane % 64) < 32
        lo64 = lane < 64
        c = cos_ref[...]
        s = sin_ref[...]
        dkr2 = jnp.zeros((tr, LANES), F32)
        for hp in range(H // 2):
            be = (2 * hp) * MLA_QK_PAD
            bo = (2 * hp + 1) * MLA_QK_PAD
            g = jnp.where(lo64, dqp_ref[:, be + LANES:be + 2 * LANES].astype(F32),
                          dqp_ref[:, bo + LANES:bo + 2 * LANES].astype(F32)) * scale
            dq_ref[:, nope_w + hp * LANES:nope_w + (hp + 1) * LANES] = (g * c + _rot_t(g * s, lo32)).astype(BF16)
            for h, base in ((2 * hp, be), (2 * hp + 1, bo)):
                dq_ref[:, h * LANES:(h + 1) * LANES] = (dqp_ref[:, base:base + LANES].astype(F32) * scale).astype(BF16)
                dkv_ref[:, h * LANES:(h + 1) * LANES] = dkp_ref[:, base:base + LANES].astype(BF16)
                dkr2 = dkr2 + dkp_ref[:, base + LANES:base + 2 * LANES].astype(F32)
        dkr2 = dkr2 * c + _rot_t(dkr2 * s, lo32)
        dkr2 = dkr2 + pltpu.roll(dkr2, 64, 1)
        dkr_ref[...] = jnp.where(lo64, dkr2, 0.0)
        dkv_ref[:, nope_w:] = dv_ref[...].astype(BF16)

    return pl.pallas_call(
        body, name=name,
        out_shape=(jax.ShapeDtypeStruct((S, nope_w + H * MLA_ROPE), BF16), jax.ShapeDtypeStruct((S, 2 * nope_w), BF16),
                   jax.ShapeDtypeStruct((S, LANES), F32)), grid=(S // tr,),
        in_specs=[pl.BlockSpec((tr, H * MLA_QK_PAD), lambda i: (i, 0)), pl.BlockSpec((tr, H * MLA_QK_PAD), lambda i: (i, 0)),
                  pl.BlockSpec((tr, H * MLA_V), lambda i: (i, 0)), pl.BlockSpec((tr, LANES), lambda i: (i, 0)),
                  pl.BlockSpec((tr, LANES), lambda i: (i, 0))],
        out_specs=(pl.BlockSpec((tr, nope_w + H * MLA_ROPE), lambda i: (i, 0)), pl.BlockSpec((tr, 2 * nope_w), lambda i: (i, 0)),
                   pl.BlockSpec((tr, LANES), lambda i: (i, 0))),
        compiler_params=_params("parallel"),
    )(dqp, dkp, dv, cos, sin)


FLASH_HB_FWD = 8
FLASH_HB_BWD = 4


def _causal_pairs(nb):
    qi = [i for i in range(nb) for j in range(i + 1)]
    kj = [j for i in range(nb) for j in range(i + 1)]
    return jnp.asarray(qi, jnp.int32), jnp.asarray(kj, jnp.int32)


def _scores(q, k, diagonal, t):
    s = lax.dot_general(q, k, (((1,), (1,)), ((), ())), preferred_element_type=F32)
    if diagonal:
        row = lax.broadcasted_iota(jnp.int32, (t, t), 0)
        col = lax.broadcasted_iota(jnp.int32, (t, t), 1)
        s = jnp.where(col <= row, s, NEG)
    return s


def _flash_fwd(qp, kp, v, *, name):
    S = qp.shape[0]
    H = v.shape[1] // MLA_V
    t = min(S, 512)
    nb = S // t
    HB = min(FLASH_HB_FWD, H)
    qi, kj = _causal_pairs(nb)
    QW, VW = MLA_QK_PAD, MLA_V

    def body(qi_ref, kj_ref, q_ref, k_ref, v_ref, o_ref, lse_ref, m_s, l_s, acc_s):
        pr = pl.program_id(1)
        i = qi_ref[pr]
        j = kj_ref[pr]

        @pl.when(j == 0)
        def _():
            m_s[...] = jnp.full_like(m_s, NEG)
            l_s[...] = jnp.zeros_like(l_s)
            acc_s[...] = jnp.zeros_like(acc_s)

        def step(diagonal):
            state = [(m_s[hh], l_s[hh], acc_s[hh]) for hh in range(HB)]
            new = []
            for hh, (m_prev, l_prev, acc_prev) in enumerate(state):
                s = _scores(q_ref[:, hh * QW:(hh + 1) * QW], k_ref[:, hh * QW:(hh + 1) * QW], diagonal, t)
                m_cur = jnp.maximum(m_prev, jnp.max(s, axis=1, keepdims=True))
                alpha = jnp.exp(m_prev - m_cur)
                p = jnp.exp(s - m_cur[:, :1])
                l_new = alpha * l_prev + jnp.sum(p, axis=1, keepdims=True)
                acc = alpha * acc_prev + jnp.dot(p.astype(BF16), v_ref[:, hh * VW:(hh + 1) * VW], preferred_element_type=F32)
                new.append((m_cur, l_new, acc))
            for hh, (m_cur, l_new, acc) in enumerate(new):
                if diagonal:
                    o_ref[:, hh * VW:(hh + 1) * VW] = acc / l_new
                    lse_ref[hh] = m_cur + jnp.log(l_new)
                else:
                    l_s[hh] = l_new
                    acc_s[hh] = acc
                    m_s[hh] = m_cur

        @pl.when(i != j)
        def _():
            step(False)

        @pl.when(i == j)
        def _():
            step(True)

    return pl.pallas_call(
        body, name=name,
        out_shape=(jax.ShapeDtypeStruct((S, H * VW), F32), jax.ShapeDtypeStruct((H, S, LANES), F32)),
        grid_spec=pltpu.PrefetchScalarGridSpec(
            num_scalar_prefetch=2, grid=(H // HB, qi.shape[0]),
            in_specs=[pl.BlockSpec((t, HB * QW), lambda g, p, qi, kj: (qi[p], g)),
                      pl.BlockSpec((t, HB * QW), lambda g, p, qi, kj: (kj[p], g)),
                      pl.BlockSpec((t, HB * VW), lambda g, p, qi, kj: (kj[p], g))],
            out_specs=(pl.BlockSpec((t, HB * VW), lambda g, p, qi, kj: (qi[p], g)),
                       pl.BlockSpec((HB, t, LANES), lambda g, p, qi, kj: (g, qi[p], 0))),
            scratch_shapes=[pltpu.VMEM((HB, t, LANES), F32), pltpu.VMEM((HB, t, LANES), F32), pltpu.VMEM((HB, t, VW), F32)]),
        compiler_params=_params("parallel", "arbitrary"),
    )(qi, kj, qp, kp, v)


def _flash_bwd(qp, kp, v, o, do, lse, *, name):
    S = qp.shape[0]
    H = v.shape[1] // MLA_V
    t = min(S, 512)
    nb = S // t
    HB = min(FLASH_HB_BWD, H)
    qi = jnp.asarray([i for j in range(nb) for i in range(j, nb)], jnp.int32)
    kj = jnp.asarray([j for j in range(nb) for i in range(j, nb)], jnp.int32)
    npairs = qi.shape[0]
    QW, VW = MLA_QK_PAD, MLA_V
    tn = (((0,), (0,)), ((), ()))
    nt = (((1,), (1,)), ((), ()))

    def body(qi_ref, kj_ref, q_ref, k_ref, v_ref, o_ref, do_ref, lse_ref, dq_ref, dk_ref, dv_ref, dq_s, dk_s, dv_s):
        pr = pl.program_id(1)
        i = qi_ref[pr]
        j = kj_ref[pr]
        rows = pl.ds(pl.multiple_of(i * t, t), t)

        @pl.when(pr == 0)
        def _():
            dq_s[...] = jnp.zeros_like(dq_s)

        @pl.when(i == j)
        def _():
            dk_s[...] = jnp.zeros_like(dk_s)
            dv_s[...] = jnp.zeros_like(dv_s)

        def step(diagonal):
            for hh in range(HB):
                q = q_ref[:, hh * QW:(hh + 1) * QW]
                k = k_ref[:, hh * QW:(hh + 1) * QW]
                dob = do_ref[:, hh * VW:(hh + 1) * VW]
                p = jnp.exp(_scores(q, k, diagonal, t) - lse_ref[hh][:, :1])
                delta = jnp.sum(dob.astype(F32) * o_ref[:, hh * VW:(hh + 1) * VW], axis=1, keepdims=True)
                dp = lax.dot_general(dob, v_ref[:, hh * VW:(hh + 1) * VW], nt, preferred_element_type=F32)
                dsb = (p * (dp - delta)).astype(BF16)
                dv_s[:, hh * VW:(hh + 1) * VW] += lax.dot_general(p.astype(BF16), dob, tn, preferred_element_type=F32)
                dk_s[:, hh * QW:(hh + 1) * QW] += lax.dot_general(dsb, q, tn, preferred_element_type=F32)
                dq_s[rows, hh * QW:(hh + 1) * QW] += jnp.dot(dsb, k, preferred_element_type=F32)

        @pl.when(i != j)
        def _():
            step(False)

        @pl.when(i == j)
        def _():
            step(True)

        @pl.when(i == nb - 1)
        def _():
            dk_ref[...] = dk_s[...].astype(BF16)
            dv_ref[...] = dv_s[...].astype(BF16)

        @pl.when(pr == npairs - 1)
        def _():
            dq_ref[...] = dq_s[...].astype(BF16)

    qside = lambda g, p, qi, kj: (qi[p], g)
    kside = lambda g, p, qi, kj: (kj[p], g)
    whole = lambda g, p, qi, kj: (0, g)
    return pl.pallas_call(
        body, name=name,
        out_shape=(jax.ShapeDtypeStruct((S, H * QW), BF16), jax.ShapeDtypeStruct((S, H * QW), BF16),
                   jax.ShapeDtypeStruct((S, H * VW), BF16)),
        grid_spec=pltpu.PrefetchScalarGridSpec(
            num_scalar_prefetch=2, grid=(H // HB, npairs),
            in_specs=[pl.BlockSpec((t, HB * QW), qside), pl.BlockSpec((t, HB * QW), kside), pl.BlockSpec((t, HB * VW), kside),
                      pl.BlockSpec((t, HB * VW), qside), pl.BlockSpec((t, HB * VW), qside),
                      pl.BlockSpec((HB, t, LANES), lambda g, p, qi, kj: (g, qi[p], 0))],
            out_specs=(pl.BlockSpec((S, HB * QW), whole), pl.BlockSpec((t, HB * QW), kside), pl.BlockSpec((t, HB * VW), kside)),
            scratch_shapes=[pltpu.VMEM((S, HB * QW), F32), pltpu.VMEM((t, HB * QW), F32), pltpu.VMEM((t, HB * VW), F32)]),
        compiler_params=_params("parallel", "arbitrary"),
    )(qi, kj, qp, kp, v, o, do, lse)


def _swa_kv_halves(blk, hf, lo):
    if hf == 0:
        a = jnp.where(lo, blk, 0.0)
        b = pltpu.roll(a, 64, 1)
    else:
        b = jnp.where(lo, 0.0, blk)
        a = pltpu.roll(b, 64, 1)
    return a.astype(BF16), b.astype(BF16)


def _swa_softmax(qs, kx, bias, neg0, sk):
    s = lax.dot_general(qs, kx, (((1,), (1,)), ((), ())), preferred_element_type=F32) + bias + neg0
    m = jnp.maximum(jnp.max(s, axis=1, keepdims=True), sk)
    e = jnp.exp(s - m)
    es = jnp.exp(sk - m)
    inv = 1.0 / (jnp.sum(e, axis=1, keepdims=True) + es)
    return e * inv, es * inv


def _swa_stack(ref, kvh, npb, scale=None):
    parts = [ref[:, (kvh * npb + pb) * LANES:(kvh * npb + pb + 1) * LANES] for pb in range(npb)]
    x = jnp.concatenate(parts, axis=0)
    return x if scale is None else x * scale


def _swa_sink_col(sink_ref, kvh, e, npb):
    row = lax.broadcasted_iota(jnp.int32, (npb * SWA_BLOCK, 1), 0)
    col = jnp.zeros((npb * SWA_BLOCK, 1), F32) + sink_ref[2 * (kvh * npb) + e]
    for pb in range(1, npb):
        col = jnp.where(row >= pb * SWA_BLOCK, sink_ref[2 * (kvh * npb + pb) + e], col)
    return col


def _swa_fwd(z_swa, bias_st, sinks, *, name):
    S, W = z_swa.shape
    npb = bias_st.shape[1] // SWA_BLOCK
    NH = 2 * SWA_KVH * npb
    QW = NH * SWA_HD
    KW = SWA_KVH * SWA_HD
    nb = S // SWA_BLOCK
    B = SWA_BLOCK
    assert SWA_KVH % 2 == 0 and W == QW + 2 * KW

    def body(sink_ref, q_ref, kvc_ref, kvp_ref, b_ref, o_ref):
        n = pl.program_id(0)
        lo = lax.broadcasted_iota(jnp.int32, (2 * B, LANES), 1) < 64
        col = lax.broadcasted_iota(jnp.int32, (npb * B, 2 * B), 1)
        neg0 = jnp.where(jnp.logical_and(col < B, n == 0), NEG, 0.0)
        for kb in range(SWA_KVH // 2):
            kblk = jnp.concatenate([kvp_ref[:, kb * LANES:(kb + 1) * LANES], kvc_ref[:, kb * LANES:(kb + 1) * LANES]], axis=0)
            vblk = jnp.concatenate([kvp_ref[:, KW + kb * LANES:KW + (kb + 1) * LANES],
                                    kvc_ref[:, KW + kb * LANES:KW + (kb + 1) * LANES]], axis=0)
            for hf in range(2):
                kvh = 2 * kb + hf
                ks = _swa_kv_halves(kblk, hf, lo)
                vs = _swa_kv_halves(vblk, hf, lo)
                qs = _swa_stack(q_ref, kvh, npb, SWA_HD ** -0.5).astype(BF16)
                acc = jnp.zeros((npb * B, LANES), F32)
                for e in range(2):
                    p, _ = _swa_softmax(qs, ks[e], b_ref[2 * kvh + e], neg0, _swa_sink_col(sink_ref, kvh, e, npb))
                    acc = acc + jnp.dot(p.astype(BF16), vs[e], preferred_element_type=F32)
                for pb in range(npb):
                    P = kvh * npb + pb
                    o_ref[:, P * LANES:(P + 1) * LANES] = acc[pb * B:(pb + 1) * B]

    kvcol = QW // (2 * KW)
    assert QW % (2 * KW) == 0
    return pl.pallas_call(
        body, name=name,
        out_shape=jax.ShapeDtypeStruct((S, QW), F32), grid=(nb,),
        in_specs=[SMEM_FULL, pl.BlockSpec((B, QW), lambda n: (n, 0)), pl.BlockSpec((B, 2 * KW), lambda n: (n, kvcol)),
                  pl.BlockSpec((B, 2 * KW), lambda n: (jnp.maximum(n - 1, 0), kvcol)),
                  pl.BlockSpec(bias_st.shape, lambda n: (0, 0, 0))],
        out_specs=pl.BlockSpec((B, QW), lambda n: (n, 0)),
        compiler_params=_params("parallel"),
    )(sinks, z_swa, z_swa, z_swa, bias_st)


def _swa_bwd(z_swa, bias_st, sinks, o, do, *, name):
    S, W = z_swa.shape
    npb = bias_st.shape[1] // SWA_BLOCK
    NH = 2 * SWA_KVH * npb
    QW = NH * SWA_HD
    KW = SWA_KVH * SWA_HD
    nb = S // SWA_BLOCK
    B = SWA_BLOCK
    scale = SWA_HD ** -0.5
    tn = (((0,), (0,)), ((), ()))
    nt = (((1,), (1,)), ((), ()))

    def fold(x, hf, lo):
        x = x + pltpu.roll(x, 64, 1)
        return jnp.where(lo, x, 0.0) if hf == 0 else jnp.where(lo, 0.0, x)

    def body(sink_ref, q_ref, kvc_ref, kvp_ref, b_ref, o_ref, do_ref, dz_ref, dbias_ref, dsink_ref,
             cq_s, ck_s, cv_s, nq_s, nk_s, nv_s, pk_s, pv_s):
        n = pl.program_id(0)

        @pl.when(n == 0)
        def _():
            dbias_ref[...] = jnp.zeros_like(dbias_ref)
            dsink_ref[...] = jnp.zeros_like(dsink_ref)
            cq_s[...] = jnp.zeros_like(cq_s)
            ck_s[...] = jnp.zeros_like(ck_s)
            cv_s[...] = jnp.zeros_like(cv_s)

        @pl.when(n == nb)
        def _():
            pk_s[...] = jnp.zeros_like(pk_s)
            pv_s[...] = jnp.zeros_like(pv_s)

        @pl.when(n < nb)
        def _():
            lo = lax.broadcasted_iota(jnp.int32, (2 * B, LANES), 1) < 64
            lo1 = lax.broadcasted_iota(jnp.int32, (npb * B, LANES), 1) < 64
            lane1 = lax.broadcasted_iota(jnp.int32, (1, LANES), 1)
            col = lax.broadcasted_iota(jnp.int32, (npb * B, 2 * B), 1)
            neg0 = jnp.where(jnp.logical_and(col < B, n == 0), NEG, 0.0)
            dsink = jnp.zeros((1, LANES), F32)
            for kb in range(SWA_KVH // 2):
                kblk = jnp.concatenate([kvp_ref[:, kb * LANES:(kb + 1) * LANES], kvc_ref[:, kb * LANES:(kb + 1) * LANES]], axis=0)
                vblk = jnp.concatenate([kvp_ref[:, KW + kb * LANES:KW + (kb + 1) * LANES],
                                        kvc_ref[:, KW + kb * LANES:KW + (kb + 1) * LANES]], axis=0)
                dkblk = jnp.zeros((2 * B, LANES), F32)
                dvblk = jnp.zeros((2 * B, LANES), F32)
                for hf in range(2):
                    kvh = 2 * kb + hf
                    ks = _swa_kv_halves(kblk, hf, lo)
                    vs = _swa_kv_halves(vblk, hf, lo)
                    qs = _swa_stack(q_ref, kvh, npb, scale).astype(BF16)
                    dob = _swa_stack(do_ref, kvh, npb)
                    prod = dob.astype(F32) * _swa_stack(o_ref, kvh, npb)
                    dkj = jnp.zeros((2 * B, LANES), F32)
                    dvj = jnp.zeros((2 * B, LANES), F32)
                    dqs = jnp.zeros((npb * B, LANES), F32)
                    for e in range(2):
                        keep = lo1 if e == 0 else jnp.logical_not(lo1)
                        p, psink = _swa_softmax(qs, ks[e], b_ref[2 * kvh + e], neg0, _swa_sink_col(sink_ref, kvh, e, npb))
                        delta = jnp.sum(jnp.where(keep, prod, 0.0), axis=1, keepdims=True)
                        dp = lax.dot_general(dob, vs[e], nt, preferred_element_type=F32)
                        ds = p * (dp - delta)
                        dbias_ref[2 * kvh + e] += ds
                        pd = psink * delta
                        for pb in range(npb):
                            dsh = -jnp.sum(pd[pb * B:(pb + 1) * B], axis=0, keepdims=True)
                            dsink = dsink + jnp.where(lane1 == 2 * (kvh * npb + pb) + e, dsh, 0.0)
                        dsb = ds.astype(BF16)
                        dqs = dqs + jnp.dot(dsb, ks[e], preferred_element_type=F32)
                        keep2 = lo if e == 0 else jnp.logical_not(lo)
                        dkj = dkj + jnp.where(keep2, lax.dot_general(dsb, qs, tn, preferred_element_type=F32), 0.0)
                        dvj = dvj + jnp.where(keep2, lax.dot_general(p.astype(BF16), dob, tn, preferred_element_type=F32), 0.0)
                    for pb in range(npb):
                        P = kvh * npb + pb
                        nq_s[:, P * LANES:(P + 1) * LANES] = dqs[pb * B:(pb + 1) * B] * scale
                    dkblk = dkblk + fold(dkj, hf, lo)
                    dvblk = dvblk + fold(dvj, hf, lo)
                pk_s[:, kb * LANES:(kb + 1) * LANES] = dkblk[:B]
                nk_s[:, kb * LANES:(kb + 1) * LANES] = dkblk[B:]
                pv_s[:, kb * LANES:(kb + 1) * LANES] = dvblk[:B]
                nv_s[:, kb * LANES:(kb + 1) * LANES] = dvblk[B:]
            dsink_ref[...] += dsink

        dz_ref[:, :QW] = cq_s[...].astype(BF16)
        dz_ref[:, QW:QW + KW] = (ck_s[...] + pk_s[...]).astype(BF16)
        dz_ref[:, QW + KW:] = (cv_s[...] + pv_s[...]).astype(BF16)

        @pl.when(n < nb)
        def _():
            cq_s[...] = nq_s[...]
            ck_s[...] = nk_s[...]
            cv_s[...] = nv_s[...]

    kvcol = QW // (2 * KW)
    cur = lambda n: (jnp.minimum(n, nb - 1), 0)
    return pl.pallas_call(
        body, name=name,
        out_shape=(jax.ShapeDtypeStruct((S, W), BF16), jax.ShapeDtypeStruct(bias_st.shape, F32),
                   jax.ShapeDtypeStruct((1, LANES), F32)),
        grid=(nb + 1,),
        in_specs=[SMEM_FULL, pl.BlockSpec((B, QW), cur), pl.BlockSpec((B, 2 * KW), lambda n: (jnp.minimum(n, nb - 1), kvcol)),
                  pl.BlockSpec((B, 2 * KW), lambda n: (jnp.maximum(jnp.minimum(n, nb - 1) - 1, 0), kvcol)),
                  pl.BlockSpec(bias_st.shape, lambda n: (0, 0, 0)), pl.BlockSpec((B, QW), cur), pl.BlockSpec((B, QW), cur)],
        out_specs=(pl.BlockSpec((B, W), lambda n: (jnp.maximum(n - 1, 0), 0)),
                   pl.BlockSpec(bias_st.shape, lambda n: (0, 0, 0)), pl.BlockSpec((1, LANES), lambda n: (0, 0))),
        scratch_shapes=[pltpu.VMEM((B, QW), F32), pltpu.VMEM((B, KW), F32), pltpu.VMEM((B, KW), F32),
                        pltpu.VMEM((B, QW), F32), pltpu.VMEM((B, KW), F32), pltpu.VMEM((B, KW), F32),
                        pltpu.VMEM((B, KW), F32), pltpu.VMEM((B, KW), F32)],
        compiler_params=_params("arbitrary"),
    )(sinks, z_swa, z_swa, z_swa, bias_st, o, do)


def _gate_fwd(zg, o_a, o_b, *, name):
    S, D = o_a.shape
    tr = min(S, 512)

    def body(z_ref, a_ref, b_ref, m_ref):
        ga = jax.nn.sigmoid(z_ref[:, :PAIR].astype(F32))
        gb = jax.nn.sigmoid(z_ref[:, PAIR:].astype(F32))
        m_ref[...] = (ga * a_ref[...] + gb * b_ref[...]).astype(BF16)

    col = pl.BlockSpec((tr, PAIR), lambda i, j: (i, j))
    return pl.pallas_call(
        body, name=name, out_shape=jax.ShapeDtypeStruct((S, D), BF16), grid=(S // tr, D // PAIR),
        in_specs=[pl.BlockSpec((tr, 2 * PAIR), lambda i, j: (i, j)), col, col], out_specs=col,
        compiler_params=_params("parallel", "parallel"),
    )(zg, o_a, o_b)


def _gate_bwd(dmix, zg, o_a, o_b, *, name):
    S, D = o_a.shape
    tr = min(S, 512)

    def body(d_ref, z_ref, a_ref, b_ref, da_ref, db_ref, dz_ref):
        d = d_ref[...].astype(F32)
        ga = jax.nn.sigmoid(z_ref[:, :PAIR].astype(F32))
        gb = jax.nn.sigmoid(z_ref[:, PAIR:].astype(F32))
        da_ref[...] = (d * ga).astype(BF16)
        db_ref[...] = (d * gb).astype(BF16)
        dz_ref[:, :PAIR] = (d * a_ref[...] * (ga * (1.0 - ga))).astype(BF16)
        dz_ref[:, PAIR:] = (d * b_ref[...] * (gb * (1.0 - gb))).astype(BF16)

    col = pl.BlockSpec((tr, PAIR), lambda i, j: (i, j))
    wide = pl.BlockSpec((tr, 2 * PAIR), lambda i, j: (i, j))
    return pl.pallas_call(
        body, name=name,
        out_shape=(jax.ShapeDtypeStruct((S, D), BF16), jax.ShapeDtypeStruct((S, D), BF16), jax.ShapeDtypeStruct((S, 2 * D), BF16)),
        grid=(S // tr, D // PAIR), in_specs=[col, wide, col, col], out_specs=(col, col, wide),
        compiler_params=_params("parallel", "parallel"),
    )(dmix, zg, o_a, o_b)


def _conv_u(t_ref, prev_ref, w_ref, b_ref, m, i):
    cur = t_ref[m].astype(F32)
    live = (i > 0).astype(F32)
    p6 = prev_ref[m, 14:15, :].astype(F32) * live
    p7 = prev_ref[m, 15:16, :].astype(F32) * live
    row = lax.broadcasted_iota(jnp.int32, cur.shape, 0)
    t1 = jnp.where(row == 0, p7, pltpu.roll(cur, 1, 0))
    t2 = jnp.where(row == 0, p6, jnp.where(row == 1, p7, pltpu.roll(cur, 2, 0)))
    u = ((b_ref[m] + w_ref[m, 0:1, :] * t2) + w_ref[m, 1:2, :] * t1) + w_ref[m, 2:3, :] * cur
    return u, cur, t1, t2


def _conv_specs(tr, tc):
    blk = pl.BlockSpec((2, tr, tc), lambda p, j, i: (p, i, j))
    prev = pl.BlockSpec((2, 16, tc), lambda p, j, i: (p, jnp.maximum(i * (tr // 16) - 1, 0), j))
    w3 = pl.BlockSpec((2, 3, tc), lambda p, j, i: (p, 0, j))
    w1 = pl.BlockSpec((2, 1, tc), lambda p, j, i: (p, 0, j))
    return blk, prev, w3, w1


def _conv_gate_fwd(t, cw, cb, *, name):
    _, S, C = t.shape
    tr, tc = min(S, 512), _tile(C, 1536)
    ncol = C // tc
    blk, prev, w3, w1 = _conv_specs(tr, tc)

    def body(t_ref, prev_ref, w_ref, b_ref, a_ref):
        i = pl.program_id(2)
        u1 = _conv_u(t_ref, prev_ref, w_ref, b_ref, 0, i)[0]
        u2 = _conv_u(t_ref, prev_ref, w_ref, b_ref, 1, i)[0]
        a_ref[...] = (jax.nn.silu(u1) * u2).astype(BF16)

    return pl.pallas_call(
        body, name=name, out_shape=jax.ShapeDtypeStruct((S, 2 * C), BF16), grid=(2, ncol, S // tr),
        in_specs=[blk, prev, w3, w1], out_specs=pl.BlockSpec((tr, tc), lambda p, j, i: (i, p * ncol + j)),
        compiler_params=_params("parallel", "parallel", "parallel"),
    )(t, t, cw, cb)


def _conv_gate_bwd(t, da, cw, cb, *, name):
    _, S, C = t.shape
    tr, tc = min(S, 256), _tile(C, 1536)
    ncol = C // tc
    blk, prev, w3, w1 = _conv_specs(tr, tc)

    def body(t_ref, prev_ref, da_ref, w_ref, b_ref, du_ref, dw_ref, db_ref):
        i = pl.program_id(2)
        first = i == 0
        u1, c1, a1, b1 = _conv_u(t_ref, prev_ref, w_ref, b_ref, 0, i)
        u2, c2, a2, b2 = _conv_u(t_ref, prev_ref, w_ref, b_ref, 1, i)
        d = da_ref[...].astype(F32)
        sg = jax.nn.sigmoid(u1)
        du1 = d * u2 * (sg * (1.0 + u1 * (1.0 - sg)))
        du2 = d * (u1 * sg)
        for m, (du, cur, t1, t2) in enumerate(((du1, c1, a1, b1), (du2, c2, a2, b2))):
            du_ref[m] = du.astype(BF16)
            dw = jnp.concatenate([jnp.sum(du * t2, axis=0, keepdims=True), jnp.sum(du * t1, axis=0, keepdims=True),
                                  jnp.sum(du * cur, axis=0, keepdims=True)], axis=0)
            db = jnp.sum(du, axis=0, keepdims=True)

            @pl.when(first)
            def _():
                dw_ref[m] = dw
                db_ref[m] = db

            @pl.when(jnp.logical_not(first))
            def _():
                dw_ref[m] += dw
                db_ref[m] += db

    return pl.pallas_call(
        body, name=name,
        out_shape=(jax.ShapeDtypeStruct(t.shape, BF16), jax.ShapeDtypeStruct(cw.shape, F32), jax.ShapeDtypeStruct(cb.shape, F32)),
        grid=(2, ncol, S // tr),
        in_specs=[blk, prev, pl.BlockSpec((tr, tc), lambda p, j, i: (i, p * ncol + j)), w3, w1], out_specs=(blk, w3, w1),
        compiler_params=_params("parallel", "parallel", "arbitrary"),
    )(t, t, da, cw, cb)


def _conv_bwd_dt(du, cw, *, name):
    _, S, C = du.shape
    tr, tc = min(S, 512), _tile(C, 1536)
    nrow = S // tr
    blk, _, w3, _ = _conv_specs(tr, tc)
    nxt = pl.BlockSpec((2, 16, tc), lambda p, j, i: (p, jnp.minimum((i + 1) * (tr // 16), S // 16 - 1), j))

    def body(d_ref, next_ref, w_ref, dt_ref):
        i = pl.program_id(2)
        live = (i < nrow - 1).astype(F32)
        for m in range(2):
            cur = d_ref[m].astype(F32)
            n0 = next_ref[m, 0:1, :].astype(F32) * live
            n1 = next_ref[m, 1:2, :].astype(F32) * live
            row = lax.broadcasted_iota(jnp.int32, cur.shape, 0)
            d1 = jnp.where(row == tr - 1, n0, pltpu.roll(cur, tr - 1, 0))
            d2 = jnp.where(row == tr - 1, n1, jnp.where(row == tr - 2, n0, pltpu.roll(cur, tr - 2, 0)))
            dt_ref[m] = ((w_ref[m, 2:3, :] * cur + w_ref[m, 1:2, :] * d1) + w_ref[m, 0:1, :] * d2).astype(BF16)

    return pl.pallas_call(
        body, name=name, out_shape=jax.ShapeDtypeStruct(du.shape, BF16), grid=(2, C // tc, nrow),
        in_specs=[blk, nxt, w3], out_specs=blk, compiler_params=_params("parallel", "parallel", "parallel"),
    )(du, du, cw)


def _ada_fwd(c_all, w, b, *, name):
    Bn, D = c_all.shape
    N = w.shape[1]
    tn = _tile(N, 512)

    def body(c_ref, w_ref, b_ref, o_ref):
        o_ref[...] = jnp.dot(jax.nn.silu(c_ref[...]), w_ref[...], preferred_element_type=F32, precision=HIGHEST) + b_ref[...]

    return pl.pallas_call(
        body, name=name, out_shape=jax.ShapeDtypeStruct((Bn, N), F32), grid=(N // tn,),
        in_specs=[pl.BlockSpec((Bn, D), lambda j: (0, 0)), pl.BlockSpec((D, tn), lambda j: (0, j)),
                  pl.BlockSpec((1, tn), lambda j: (0, j))],
        out_specs=pl.BlockSpec((Bn, tn), lambda j: (0, j)), compiler_params=_params("parallel"),
    )(c_all, w, b)


def _ada_bwd(c_all_t, dmod, *, name):
    D, Bn = c_all_t.shape
    N = dmod.shape[1]
    tm = _tile(D, 512, 8)
    tn = _tile(N, 1536)

    def body(c_ref, d_ref, o_ref):
        o_ref[...] = jnp.dot(jax.nn.silu(c_ref[...]), d_ref[...], preferred_element_type=F32, precision=HIGHEST)

    return pl.pallas_call(
        body, name=name, out_shape=jax.ShapeDtypeStruct((D, N), F32), grid=(D // tm, N // tn),
        in_specs=[pl.BlockSpec((tm, Bn), lambda i, j: (i, 0)), pl.BlockSpec((Bn, tn), lambda i, j: (0, j))],
        out_specs=pl.BlockSpec((tm, tn), lambda i, j: (i, j)), compiler_params=_params("parallel", "parallel"),
    )(c_all_t, dmod)


def _adamw(w, g, m, v, *, name):
    R, C = w.shape
    tr = R if R * C <= (1 << 19) else _tile(R, max(8, (1 << 19) // C), 8)

    def body(w_ref, g_ref, m_ref, v_ref, d_ref, nm_ref, nv_ref):
        gv = g_ref[...]
        nm = ADAM_B1 * m_ref[...] + (1.0 - ADAM_B1) * gv
        nv = ADAM_B2 * v_ref[...] + (1.0 - ADAM_B2) * (gv * gv)
        m_hat = nm / (1.0 - ADAM_B1 ** ADAM_STEP)
        v_hat = nv / (1.0 - ADAM_B2 ** ADAM_STEP)
        d_ref[...] = -ADAM_LR * (m_hat / (jnp.sqrt(v_hat) + ADAM_EPS) + ADAM_WD * w_ref[...])
        nm_ref[...] = nm
        nv_ref[...] = nv

    blk = pl.BlockSpec((tr, C), lambda i: (i, 0))
    shp = jax.ShapeDtypeStruct((R, C), F32)
    return pl.pallas_call(
        body, name=name, out_shape=(shp, shp, shp), grid=(R // tr,), in_specs=[blk] * 4, out_specs=(blk,) * 3,
        compiler_params=_params("parallel"),
    )(w, g, m, v)


def _place():
    x, y, c = lax.axis_index("x"), lax.axis_index("y"), lax.axis_index("c")
    return x, y, c, [(1 - x, y), (x, 1 - y), (1 - x, 1 - y)]


def _remote(src, dst, send_sem, recv_sem, dev):
    return pltpu.make_async_remote_copy(src_ref=src, dst_ref=dst, send_sem=send_sem, recv_sem=recv_sem,
                                        device_id=dev, device_id_type=MESH)


def _allgather8(v, *, tie=None, name):
    R, C = v.shape

    def body(v_ref, out_ref, send_sems, recv_sems, local_sem):
        x, y, c, chips = _place()
        me, sibling = (x, y, c), (x, y, 1 - c)

        def rows(px, py, pc):
            return out_ref.at[pl.ds((4 * px + 2 * py + pc) * R, R), :]

        def copy(k, block, to, src=None):
            return _remote(rows(*block) if src is None else src, rows(*block), send_sems.at[k], recv_sems.at[k], to)

        mine = pltpu.make_async_copy(v_ref, rows(*me), local_sem)
        mine.start()
        first = [copy(0, me, sibling, src=v_ref)]
        first += [copy(1 + j, me, (*chip, c), src=v_ref) for j, chip in enumerate(chips)]
        for cp in first:
            cp.start()
        passed = [copy(4 + j, (*chip, c), sibling) for j, chip in enumerate(chips)]
        for j, chip in enumerate(chips):
            copy(1 + j, (*chip, c), me).wait_recv()
            passed[j].start()
        copy(0, sibling, me).wait_recv()
        for j, chip in enumerate(chips):
            copy(4 + j, (*chip, 1 - c), me).wait_recv()
        for cp in first + passed:
            cp.wait_send()
        mine.wait()

    body, tspec, targ = _tied(body, tie)
    out = pl.pallas_call(
        body, name=name, out_shape=jax.ShapeDtypeStruct((N_DEV * R, C), v.dtype),
        in_specs=tspec + [VMEM_FULL], out_specs=VMEM_FULL,
        scratch_shapes=[pltpu.SemaphoreType.DMA((7,)), pltpu.SemaphoreType.DMA((7,)), pltpu.SemaphoreType.DMA],
    )(*targ, v)
    return out.reshape(N_DEV, R, C)


SEM = pl.BlockSpec(memory_space=pltpu.SEMAPHORE)
HBM = pl.BlockSpec(memory_space=pltpu.HBM)
EFFECT = pltpu.SideEffectType.DATAFLOW_SIDE_EFFECTING
DMA_SEM = pltpu.SemaphoreType.DMA(())


def _in_hbm(a):
    return pltpu.with_memory_space_constraint(a, pltpu.HBM)


def _three_halves(land, r2):
    return land.at[pl.ds(0, N_CHIP - 1), pl.ds(0, r2)]


def _slot(chip, swap):
    return (chip % 2) * 2 + chip // 2 if swap else chip


def _gather_start(ws, after, swaps, *, name):
    n = len(ws)
    na = len(after)
    lands = [lax.empty((N_CHIP,) + w.shape, w.dtype) for w in ws]

    def body(*refs):
        w_refs, land_refs = refs[:n], refs[n:2 * n]
        send, recv = refs[2 * n + na:3 * n + na], refs[3 * n + na:4 * n + na]
        token = refs[6 * n + na]
        x, y, c, chips = _place()
        k = 2 * x + y
        for i in range(n):
            r2 = ws[i].shape[0] // 2
            for cx, cy in chips:
                _remote(w_refs[i].at[pl.ds(c * r2, r2)], land_refs[i].at[_slot(k, swaps[i]), pl.ds(c * r2, r2)], send[i], recv[i],
                        (cx, cy, c)).start()
        token[...] = jnp.zeros_like(token)

    outs = pl.pallas_call(
        body, name=name,
        out_shape=[DMA_SEM] * (2 * n) + [pltpu.HBM(w.shape, w.dtype) for w in ws] + [pltpu.HBM(l.shape, l.dtype) for l in lands]
        + [jax.ShapeDtypeStruct((8, LANES), F32)],
        in_specs=[HBM] * (2 * n) + [ANY] * na, out_specs=[SEM] * (2 * n) + [HBM] * (2 * n) + [VMEM_FULL],
        input_output_aliases={i: 2 * n + i for i in range(2 * n)},
        compiler_params=pltpu.CompilerParams(has_side_effects=EFFECT),
    )(*[_in_hbm(w) for w in ws], *[_in_hbm(l) for l in lands], *after)
    return outs[:n], outs[n:2 * n], outs[2 * n:3 * n], outs[3 * n:4 * n], outs[4 * n]


def _gather_forward(send, recv, ws, lands, after, swaps, *, name):
    n = len(ws)

    def body(*refs):
        w_refs, land_refs = refs[:n], refs[n:2 * n]
        send1, recv1 = refs[2 * n:3 * n], refs[3 * n:4 * n]
        send2, recv2 = refs[4 * n + 1 + 2 * n:4 * n + 1 + 3 * n], refs[4 * n + 1 + 3 * n:4 * n + 1 + 4 * n]
        x, y, c, chips = _place()
        sibling = (x, y, 1 - c)
        for i in range(n):
            r2 = ws[i].shape[0] // 2
            win = _three_halves(land_refs[i], r2)
            done = _remote(win, win, send1[i], recv1[i], sibling)
            done.wait_send()
            done.wait_recv()
            for cx, cy in chips:
                got = land_refs[i].at[_slot(2 * cx + cy, swaps[i]), pl.ds(c * r2, r2)]
                _remote(got, got, send2[i], recv2[i], sibling).start()
        token = refs[8 * n + 1]
        token[...] = jnp.zeros_like(token)

    outs = pl.pallas_call(
        body, name=name,
        out_shape=[pltpu.HBM(w.shape, w.dtype) for w in ws] + [pltpu.HBM(l.shape, l.dtype) for l in lands] + [DMA_SEM] * (2 * n)
        + [jax.ShapeDtypeStruct((8, LANES), F32)],
        in_specs=[HBM] * (2 * n) + [SEM] * (2 * n) + [ANY], out_specs=[HBM] * (2 * n) + [SEM] * (2 * n) + [VMEM_FULL],
        input_output_aliases={i: i for i in range(2 * n)},
        compiler_params=pltpu.CompilerParams(has_side_effects=EFFECT),
    )(*ws, *lands, *send, *recv, after)
    return outs[2 * n:3 * n], outs[3 * n:4 * n], outs[n:2 * n], outs[4 * n]


def _gather_finish(send, recv, lands, after, *, name):
    n = len(lands)

    def body(*refs):
        land_refs = refs[:n]
        send2, recv2 = refs[n:2 * n], refs[2 * n:3 * n]
        x, y, c, _ = _place()
        for i in range(n):
            win = _three_halves(land_refs[i], lands[i].shape[1] // 2)
            done = _remote(win, win, send2[i], recv2[i], (x, y, 1 - c))
            done.wait_send()
            done.wait_recv()

    return pl.pallas_call(
        body, name=name,
        out_shape=[pltpu.HBM(l.shape, l.dtype) for l in lands],
        in_specs=[HBM] * n + [SEM] * (2 * n) + [ANY], out_specs=[HBM] * n,
        input_output_aliases={i: i for i in range(n)},
        compiler_params=pltpu.CompilerParams(has_side_effects=EFFECT),
    )(*lands, *send, *recv, after)


def _scatter_start(gs, swaps, *, name):
    n = len(gs)
    lands = [lax.empty((N_DEV, g.shape[1] // 2, g.shape[2]), g.dtype) for g in gs]

    def body(*refs):
        g_refs, land_refs = refs[:n], refs[n:2 * n]
        send, recv = refs[2 * n:3 * n], refs[3 * n:4 * n]
        token = refs[6 * n]
        x, y, c, chips = _place()
        k = 2 * x + y
        me = 2 * k + c
        for i in range(n):
            r2 = gs[i].shape[1] // 2
            for cx, cy in chips:
                for cc in range(2):
                    _remote(g_refs[i].at[_slot(2 * cx + cy, swaps[i]), pl.ds(cc * r2, r2)], land_refs[i].at[me], send[i], recv[i],
                            (cx, cy, cc)).start()
            _remote(g_refs[i].at[_slot(k, swaps[i]), pl.ds((1 - c) * r2, r2)], land_refs[i].at[me], send[i], recv[i],
                    (x, y, 1 - c)).start()
        token[...] = jnp.zeros_like(token)

    outs = pl.pallas_call(
        body, name=name,
        out_shape=[DMA_SEM] * (2 * n) + [pltpu.HBM(g.shape, g.dtype) for g in gs] + [pltpu.HBM(l.shape, l.dtype) for l in lands]
        + [jax.ShapeDtypeStruct((8, LANES), F32)],
        in_specs=[HBM] * (2 * n), out_specs=[SEM] * (2 * n) + [HBM] * (2 * n) + [VMEM_FULL],
        input_output_aliases={i: 2 * n + i for i in range(2 * n)},
        compiler_params=pltpu.CompilerParams(has_side_effects=EFFECT),
    )(*[_in_hbm(g) for g in gs], *[_in_hbm(l) for l in lands])
    return outs[:n], outs[n:2 * n], outs[2 * n:3 * n], outs[3 * n:4 * n], outs[4 * n]


def _scatter_wait(send, recv, gs, lands, after, *, name):
    n = len(gs)

    def body(*refs):
        land_refs = refs[n:2 * n]
        send1, recv1 = refs[2 * n:3 * n], refs[3 * n:4 * n]
        x, y, c, _ = _place()
        for i in range(n):
            win = land_refs[i].at[pl.ds(0, N_DEV - 1)]
            done = _remote(win, win, send1[i], recv1[i], (x, y, 1 - c))
            done.wait_send()
            done.wait_recv()

    outs = pl.pallas_call(
        body, name=name,
        out_shape=[pltpu.HBM(g.shape, g.dtype) for g in gs] + [pltpu.HBM(l.shape, l.dtype) for l in lands],
        in_specs=[HBM] * (2 * n) + [SEM] * (2 * n) + [ANY], out_specs=[HBM] * (2 * n),
        input_output_aliases={i: i for i in range(2 * n)},
        compiler_params=pltpu.CompilerParams(has_side_effects=EFFECT),
    )(*gs, *lands, *send, *recv, after)
    return outs[:n], outs[n:]


def _share_halves(ts, *, name):
    n = len(ts)

    def body(*refs):
        outs = refs[n:2 * n]
        send_sems, recv_sems = refs[2 * n:]
        x, y, c, _ = _place()
        sibling = (x, y, 1 - c)
        cps = []
        for i in range(n):
            r2 = ts[i].shape[0] // 2
            mine = outs[i].at[pl.ds(c * r2, r2)]
            cps.append(_remote(mine, mine, send_sems.at[i], recv_sems.at[i], sibling))
            cps[-1].start()
        for i in range(n):
            r2 = ts[i].shape[0] // 2
            got = outs[i].at[pl.ds((1 - c) * r2, r2)]
            _remote(got, got, send_sems.at[i], recv_sems.at[i], sibling).wait_recv()
        for cp in cps:
            cp.wait_send()

    return pl.pallas_call(
        body, name=name,
        out_shape=[jax.ShapeDtypeStruct(t.shape, t.dtype) for t in ts],
        in_specs=[ANY] * n, out_specs=[ANY] * n, input_output_aliases={i: i for i in range(n)},
        scratch_shapes=[pltpu.SemaphoreType.DMA((n,)), pltpu.SemaphoreType.DMA((n,))],
    )(*ts)


def _sum_pieces(land, g, idx, *, name):
    _, r2, C = land.shape
    tr = _tile(r2, max(16, (1 << 21) // C), 16)
    nr = r2 // tr

    def body(idx_ref, land_ref, own_ref, o_ref, acc_ref):
        d = pl.program_id(1)
        mine = d == idx_ref[0]

        @pl.when(d == 0)
        def _():
            acc_ref[...] = jnp.zeros_like(acc_ref)

        @pl.when(mine)
        def _():
            acc_ref[...] += own_ref[...].astype(F32)

        @pl.when(jnp.logical_not(mine))
        def _():
            acc_ref[...] += land_ref[...].astype(F32)

        @pl.when(d == N_DEV - 1)
        def _():
            o_ref[...] = acc_ref[...]

    return pl.pallas_call(
        body, name=name, out_shape=jax.ShapeDtypeStruct((2 * r2, C), F32),
        grid_spec=pltpu.PrefetchScalarGridSpec(
            num_scalar_prefetch=1, grid=(nr, N_DEV),
            in_specs=[pl.BlockSpec((None, tr, C), lambda i, d, ix: (jnp.where(d == ix[0], (d + 1) % N_DEV, d), i, 0)),
                      pl.BlockSpec((None, tr, C), lambda i, d, ix: (ix[1], ix[2] * nr + i, 0))],
            out_specs=pl.BlockSpec((tr, C), lambda i, d, ix: (ix[2] * nr + i, 0)),
            scratch_shapes=[pltpu.VMEM((tr, C), F32)]),
        compiler_params=_params("parallel", "arbitrary"),
    )(idx, land, g)


def _sum_devices(v, *, name):
    n, R, C = v.shape

    def body(v_ref, o_ref):
        acc = v_ref[0]
        for j in range(1, n):
            acc = acc + v_ref[j]
        o_ref[...] = acc

    return pl.pallas_call(body, name=name, out_shape=jax.ShapeDtypeStruct((R, C), F32),
                          in_specs=[VMEM_FULL], out_specs=VMEM_FULL)(v)


def _shard_cols(shards, lo, hi, width):
    out = []
    while lo < hi:
        j = lo // width
        end = min(hi, (j + 1) * width)
        out.append(shards[j][:, lo - j * width:end - j * width])
        lo = end
    return out


def _from_col_shards(g):
    return jnp.transpose(g, (1, 0, 2)).reshape(g.shape[1], N_CHIP * g.shape[2])


def _to_col_shards(w):
    R, N = w.shape
    return jnp.transpose(w.reshape(R, N_CHIP, N // N_CHIP), (1, 0, 2))


def _split_heads(w, widths):
    R, N = w.shape
    per = sum(widths)
    w3 = w.reshape(R, N // per, per)
    lo = w3[:, :, :widths[0]].reshape(R, -1)
    hi = w3[:, :, widths[0]:].reshape(R, -1)
    return jnp.concatenate([lo, hi], axis=1)


def _merge_heads(w, widths):
    R, N = w.shape
    H = N // sum(widths)
    lo = w[:, :H * widths[0]].reshape(R, H, widths[0])
    hi = w[:, H * widths[0]:].reshape(R, H, widths[1])
    return jnp.concatenate([lo, hi], axis=2).reshape(R, N)


def _t5_bucket(dist):
    max_exact = REL_BUCKETS // 2
    n = jnp.maximum(dist, 0)
    large = max_exact + (jnp.log(jnp.maximum(n, 1).astype(F32) / max_exact)
                         / jnp.log(jnp.asarray(REL_MAX_DIST / max_exact, F32))
                         * (REL_BUCKETS - max_exact)).astype(jnp.int32)
    large = jnp.minimum(large, REL_BUCKETS - 1)
    return jnp.where(n < max_exact, n, large)


def _rel_tables():
    a = jnp.arange(SWA_BLOCK)
    b = jnp.arange(2 * SWA_BLOCK)
    dist = SWA_BLOCK + a[:, None] - b[None, :]
    valid = jnp.logical_and(dist >= 0, dist < SWA_BLOCK)
    onehot = jnp.logical_and(_t5_bucket(dist)[..., None] == jnp.arange(REL_BUCKETS), valid[..., None])
    onehot = onehot.astype(F32).reshape(2 * SWA_BLOCK * SWA_BLOCK, REL_BUCKETS)
    negmask = jnp.where(valid, 0.0, NEG).astype(F32).reshape(1, -1)
    return onehot, negmask


def _rope_tables(S):
    pos = jnp.arange(S, dtype=F32)
    inv = ROPE_THETA ** (-jnp.arange(0, MLA_ROPE, 2, dtype=F32) / MLA_ROPE)
    ang = pos[:, None] * inv[None, :]
    ang = jnp.concatenate([ang, ang, ang, ang], axis=-1)
    return jnp.cos(ang), jnp.sin(ang)


def _flat_pad(parts, rows=8):
    flat = jnp.concatenate([p.reshape(1, -1) for p in parts], axis=1)
    n = flat.shape[1]
    width = -(-n // (rows * LANES)) * LANES
    return jnp.pad(flat, ((0, 0), (0, rows * width - n))).reshape(rows, width)


def _unflat(vec, shapes):
    flat = vec.reshape(-1)
    out, off = [], 0
    for s in shapes:
        n = 1
        for d in s:
            n *= d
        out.append(flat[off:off + n].reshape(s))
        off += n
    return out


def kernel(x, c, w_ada, b_ada, g_pre_mix, g_post_mix, w_in, g_q_lat, w_uq, g_kv_lat, w_ukv, rel_bias, sinks, w_o, g_pre_ffn, g_post_ffn, w_up, conv_w, conv_b, w_down, loss_target, m_w_ada, m_b_ada, m_g_pre_mix, m_g_post_mix, m_w_in, m_g_q_lat, m_w_uq, m_g_kv_lat, m_w_ukv, m_rel_bias, m_sinks, m_w_o, m_g_pre_ffn, m_g_post_ffn, m_w_up, m_conv_w, m_conv_b, m_w_down, v_w_ada, v_b_ada, v_g_pre_mix, v_g_post_mix, v_w_in, v_g_q_lat, v_w_uq, v_g_kv_lat, v_w_ukv, v_rel_bias, v_sinks, v_w_o, v_g_pre_ffn, v_g_post_ffn, v_w_up, v_conv_w, v_conv_b, v_w_down):
    S, D = x.shape[1], x.shape[2]
    Rq, Rkv = g_q_lat.shape[1], g_kv_lat.shape[1]
    H = D // MLA_V
    NH = D // SWA_HD
    KW = SWA_KVH * SWA_HD
    F = w_down.shape[1] * N_CHIP
    xi, yi, ci = lax.axis_index("x"), lax.axis_index("y"), lax.axis_index("c")
    chip = 2 * xi + yi
    me = 2 * chip + ci
    x2, tgt = x[0], loss_target[0]

    c_all = _allgather8(jnp.broadcast_to(c, (8, D)), name="gather_c")[:, 0, :]
    n3 = w_ada.shape[2]
    mod_part = _ada_fwd(c_all, w_ada[0], lax.dynamic_slice(b_ada, (0, chip * n3), (1, n3)), name="ada_fwd")
    mod_all = _allgather8(mod_part, name="gather_mod")
    mod_me = lax.dynamic_index_in_dim(mod_all[0::2], me, axis=1, keepdims=False).reshape(1, 6 * D)
    sh1, sc1, gt1, sh2, sc2, gt2 = [mod_me[:, k * D:(k + 1) * D] for k in range(6)]

    swaps = [False, False, False, False, True, False]
    local = [w_in[0].astype(BF16)]
    send_a, recv_a, srcs_a, lands_a, token = _gather_start(local, (mod_all,), swaps[:1], name="gather_start_in")
    rest, token = lax.optimization_barrier(((w_uq[0], w_ukv[0], w_o[0], w_up[0], w_down[0]), token))
    local += [w.astype(BF16) for w in rest]
    send_b, recv_b, srcs_b, lands_b, token = _gather_start(local[1:], (token,), swaps[1:], name="gather_start_rest")
    send1, recv1, srcs, lands = send_a + send_b, recv_a + recv_b, srcs_a + srcs_b, lands_a + lands_b
    onehot, negmask = _rel_tables()
    npb = NH // (2 * SWA_KVH)
    rb_st = jnp.transpose(rel_bias.T.reshape(SWA_KVH, npb, 2, REL_BUCKETS), (0, 2, 1, 3)).reshape(NH, REL_BUCKETS)
    bias_m = (_matmul(rb_st, onehot.T, tie=token, name="rel_bias_table") + negmask).reshape(
        2 * SWA_KVH, npb * SWA_BLOCK, 2 * SWA_BLOCK)
    h = _modnorm_fwd(x2, g_pre_mix, sc1, sh1, name="pre_mix_norm")

    def whole(land, i):
        return lax.dynamic_update_index_in_dim(land, local[i], _slot(chip, swaps[i]), 0)

    def conv_slots(v):
        return jnp.stack([v[0], v[2], v[1], v[3]])

    s2, r2, l_in, _ = _gather_forward(send1[:1], recv1[:1], srcs[:1], lands[:1], h, swaps[:1], name="gather_forward_in")
    (l_in,) = _gather_finish(s2, r2, l_in, h, name="gather_finish_in")
    gin = whole(l_in, 0)
    o_kr = Rq + Rkv
    o_q = o_kr + MLA_ROPE
    o_g = o_q + NH * SWA_HD + 2 * KW
    n_gate, n_swa = 2 * D, o_g - o_q
    n_lat = -(-(o_q + MLA_ROPE) // PAIR) * PAIR
    runs = []
    for tl in range(D // PAIR):
        runs.append((o_g + tl * PAIR, o_g + (tl + 1) * PAIR, 2 * tl * PAIR))
        runs.append((o_g + D + tl * PAIR, o_g + D + (tl + 1) * PAIR, (2 * tl + 1) * PAIR))
    runs.append((o_q, o_g, n_gate))
    runs.append((0, o_q, n_gate + n_swa))
    csh = gin.shape[2]
    parts = []
    for lo, hi, _ in runs + [(o_kr, o_q, 0)]:
        parts += _shard_cols([gin[j] for j in range(N_CHIP)], lo, hi, csh)
    parts.append(jnp.zeros((D, n_lat - o_q - MLA_ROPE), BF16))
    w_in_all = jnp.concatenate(parts, axis=1)
    cos, sin = _rope_tables(S)
    sink_v = sinks.reshape(NH)

    z_lat = _matmul(h, w_in_all, bcols=(n_gate + n_swa, n_lat), name="in_proj_lat")
    z_swa = _matmul(h, w_in_all, bcols=(n_gate, n_swa), name="in_proj_swa")
    zg = _matmul(h, w_in_all, bcols=(0, n_gate), out_dtype=BF16, name="in_proj_gate")
    s2b, r2b, l_b, _ = _gather_forward(send1[1:4], recv1[1:4], srcs[1:4], lands[1:4], zg, swaps[1:4],
                                       name="gather_forward_attn")
    nq, nkv = _lat_norm_fwd(z_lat, g_q_lat, g_kv_lat, name="lat_norm")
    l_uq, l_ukv, l_o = _gather_finish(s2b, r2b, l_b, nq, name="gather_finish_attn")
    wuq = _split_heads(_from_col_shards(whole(l_uq, 1)), (MLA_NOPE, MLA_ROPE))
    wukv = _split_heads(_from_col_shards(whole(l_ukv, 2)), (MLA_NOPE, MLA_V))
    wo = whole(l_o, 3).reshape(D, D)
    q_raw = _matmul(nq, wuq, out_dtype=BF16, name="uq_proj")
    kv_raw = _matmul(nkv, wukv, out_dtype=BF16, name="ukv_proj")
    qp, kp, vv = _mla_pack_fwd(q_raw, kv_raw, z_lat, cos, sin, o_kr, name="mla_pack")
    o_a, lse = _flash_fwd(qp, kp, vv, name="mla_attn")
    s2c, r2c, l_c, tok_c = _gather_forward(send1[4:], recv1[4:], srcs[4:], lands[4:], o_a, swaps[4:],
                                           name="gather_forward_ffn")
    o_b = _swa_fwd(z_swa, bias_m, sink_v, name="swa_attn")
    mixin = _gate_fwd(zg, o_a, o_b, name="gate_mix")
    mix = _matmul(mixin, wo, tie=tok_c, name="o_proj")
    x1, h2 = _resnorm_modnorm_fwd(x2, mix, g_post_mix, gt1, g_pre_ffn, sc2, sh2, name="post_mix_pre_ffn_norm")
    l_up, l_down = _gather_finish(s2c, r2c, l_c, h2, name="gather_finish_ffn")
    cw_all = _allgather8(jnp.pad(conv_w[0], ((0, 5), (0, 0))), tie=l_down, name="gather_conv_w")[0::2, :3]
    cw = conv_slots(cw_all)
    cb = conv_slots(conv_b.reshape(N_CHIP, 1, -1))
    wup = whole(l_up, 4)
    wdown = whole(l_down, 5).reshape(F, D)
    t = _matmul(h2, wup, out_dtype=BF16, shards="out", name="up_proj")
    a = _conv_gate_fwd(t, cw, cb, name="conv_gate")
    yv = _matmul(a, wdown, name="down_proj")
    dout, dy, dg_post_ffn, dgt2, loss_tile = _resnorm_loss(x1, yv, g_post_ffn, gt2, tgt, name="post_ffn_norm_loss")

    big_params = dict(w_in=(w_in, m_w_in, v_w_in), w_uq=(w_uq, m_w_uq, v_w_uq), w_ukv=(w_ukv, m_w_ukv, v_w_ukv),
                      w_o=(w_o, m_w_o, v_w_o), w_up=(w_up, m_w_up, v_w_up), w_down=(w_down, m_w_down, v_w_down))
    res = {}

    def start(nms, gs):
        sw = [nm == "w_up" for nm in nms]
        send, recv, gsrc, glands, tok = _scatter_start(gs, sw, name="grads_start_" + nms[0])
        return (nms, send, recv, gsrc, glands), tok

    def finish(pendings, after):
        nms, send, recv, gsrc, glands = [sum((list(p[k]) for p in pendings), []) for k in range(5)]
        gsrc, glands = _scatter_wait(send, recv, gsrc, glands, after, name="grads_wait_" + nms[0])
        halves = [_sum_pieces(l, g, jnp.stack([me, _slot(chip, nm == "w_up"), ci]).astype(jnp.int32), name="grad_sum_" + nm)
                  for l, g, nm in zip(glands, gsrc, nms)]
        for nm, g in zip(nms, _share_halves(halves, name="grads_share_" + nms[0])):
            w, m, v = big_params[nm]
            res[nm] = (g,) + tuple(_adamw(w[0], g, m[0], v[0], name="adamw_" + nm))

    dw_down = _matmul(a, dy, ta=True, out_dtype=BF16, name="down_proj_dw")
    p_down, tok = start(["w_down"], [dw_down.reshape(N_CHIP, F // N_CHIP, D)])
    da = _matmul(dy, wdown, tb=True, out_dtype=BF16, tie=tok, name="down_proj_dx")
    du, dcw_s, dcb_s = _conv_gate_bwd(t, da, cw, cb, name="conv_gate_bwd")
    dt = _conv_bwd_dt(du, cw, name="conv_bwd_dt")
    dw_up = _matmul(h2, dt, ta=True, out_dtype=BF16, shards="out", name="up_proj_dw")
    p_up, tok = start(["w_up"], [dw_up])
    dh2 = _matmul(dt, wup, tb=True, tie=tok, shards="k", name="up_proj_dx")
    dx1, dg_pre_ffn, dsc2, dsh2, dmix, dg_post_mix, dgt1 = _modnorm_resnorm_bwd(
        dh2, x1, g_pre_ffn, sc2, dout, mix, g_post_mix, gt1, name="pre_ffn_post_mix_norm_bwd")
    dw_o = _matmul(mixin, dmix, ta=True, out_dtype=BF16, name="o_proj_dw")
    p_o, tok = start(["w_o"], [dw_o.reshape(N_CHIP, D // N_CHIP, D)])
    dmixin = _matmul(dmix, wo, tb=True, out_dtype=BF16, tie=tok, name="o_proj_dx")
    do_a, do_b, dzg = _gate_bwd(dmixin, zg, o_a, o_b, name="gate_mix_bwd")
    dqp, dkp, dvv = _flash_bwd(qp, kp, vv, o_a, do_a, lse, name="mla_attn_bwd")
    dq_raw, dkv_raw, dkr = _mla_pack_bwd(dqp, dkp, dvv, cos, sin, name="mla_pack_bwd")
    dw_uq_p = _matmul(nq, dq_raw, ta=True, out_dtype=BF16, name="uq_proj_dw")
    dw_ukv_p = _matmul(nkv, dkv_raw, ta=True, out_dtype=BF16, name="ukv_proj_dw")
    p_qkv, tok = start(["w_uq", "w_ukv"], [_to_col_shards(_merge_heads(dw_uq_p, (MLA_NOPE, MLA_ROPE))),
                                           _to_col_shards(_merge_heads(dw_ukv_p, (MLA_NOPE, MLA_V)))])
    dnq = _matmul(dq_raw, wuq, tb=True, tie=tok, name="uq_proj_dx")
    dnkv = _matmul(dkv_raw, wukv, tb=True, name="ukv_proj_dx")
    dz_lat, dg_q, dg_kv = _lat_norm_bwd(z_lat, dnq, dnkv, dkr, g_q_lat, g_kv_lat, name="lat_norm_bwd")
    dz_swa, dbias, dsink = _swa_bwd(z_swa, bias_m, sink_v, o_b, do_b, name="swa_attn_bwd")
    dz = jnp.concatenate([dzg, dz_swa, dz_lat], axis=1)
    dw_in_p = _matmul(h, dz, ta=True, out_dtype=BF16, name="in_proj_dw")
    dw_shards = []
    for j in range(N_CHIP):
        cols = []
        for lo, hi, at in sorted(runs):
            a0, a1 = max(lo, j * csh), min(hi, (j + 1) * csh)
            if a0 < a1:
                cols.append(dw_in_p[:, at + a0 - lo:at + a1 - lo])
        dw_shards.append(jnp.concatenate(cols, axis=1))
    p_in, tok = start(["w_in"], [jnp.stack(dw_shards)])
    dh = _matmul(dz, w_in_all, tb=True, tie=tok, name="in_proj_dx")
    grad_x, dg_pre_mix, dsc1, dsh1 = _modnorm_bwd(dh, x2, g_pre_mix, sc1, dx1, name="pre_mix_norm_bwd")
    drel_st = _matmul(dbias.reshape(NH, -1), onehot, tie=grad_x, name="rel_bias_bwd")
    finish((p_down, p_up, p_o, p_qkv), drel_st)
    drel = jnp.transpose(drel_st.reshape(SWA_KVH, 2, npb, REL_BUCKETS), (0, 2, 1, 3)).reshape(NH, REL_BUCKETS).T

    dcw = _from_col_shards(conv_slots(dcw_s))
    dcb = conv_slots(dcb_s).reshape(1, -1)
    dmod = jnp.concatenate([dsh1, dsc1, dgt1, dsh2, dsc2, dgt2], axis=1)
    small = [dmod, dg_pre_mix, dg_post_mix, dg_pre_ffn, dg_post_ffn, dg_q, dg_kv, drel, dsink[:, :NH], dcb, dcw]
    shapes = [p.shape for p in small]
    done = [res[nm][1] for nm in ("w_down", "w_up", "w_o", "w_uq", "w_ukv")]
    small_all = _allgather8(_flat_pad(small), tie=done, name="gather_small_grads")
    tot = _unflat(_sum_devices(small_all, name="sum_small_grads"), shapes)
    g_b_ada, g_pre_mix_g, g_post_mix_g, g_pre_ffn_g, g_post_ffn_g, g_q_g, g_kv_g, g_rel, g_sinks, g_cb, g_cw_full = tot
    dmod_all = small_all.reshape(N_DEV, -1)[:, :6 * D]
    g_w_ada = _ada_bwd(c_all.T, lax.dynamic_slice(dmod_all, (0, chip * n3), (N_DEV, n3)), name="ada_bwd")
    ncw = conv_w.shape[2]
    g_cw = lax.dynamic_slice(g_cw_full, (0, chip * ncw), (3, ncw))

    res["w_ada"] = (g_w_ada,) + tuple(_adamw(w_ada[0], g_w_ada, m_w_ada[0], v_w_ada[0], name="adamw_w_ada"))
    finish((p_in,), g_w_ada)
    snames = ["b_ada", "g_pre_mix", "g_post_mix", "g_pre_ffn", "g_post_ffn", "g_q_lat", "g_kv_lat", "rel_bias", "sinks",
              "conv_b", "conv_w"]
    sw = [b_ada, g_pre_mix, g_post_mix, g_pre_ffn, g_post_ffn, g_q_lat, g_kv_lat, rel_bias, sinks, conv_b, conv_w]
    sm = [m_b_ada, m_g_pre_mix, m_g_post_mix, m_g_pre_ffn, m_g_post_ffn, m_g_q_lat, m_g_kv_lat, m_rel_bias, m_sinks,
          m_conv_b, m_conv_w]
    sv = [v_b_ada, v_g_pre_mix, v_g_post_mix, v_g_pre_ffn, v_g_post_ffn, v_g_q_lat, v_g_kv_lat, v_rel_bias, v_sinks,
          v_conv_b, v_conv_w]
    sg = [g_b_ada, g_pre_mix_g, g_post_mix_g, g_pre_ffn_g, g_post_ffn_g, g_q_g, g_kv_g, g_rel, g_sinks, g_cb, g_cw]
    sshapes = [w.shape for w in sw]
    sd, snm, snv = _adamw(_flat_pad(sw), _flat_pad(sg), _flat_pad(sm), _flat_pad(sv), name="adamw_small")
    sd, snm, snv = _unflat(sd, sshapes), _unflat(snm, sshapes), _unflat(snv, sshapes)
    for k, nm in enumerate(snames):
        res[nm] = (sg[k].reshape(sshapes[k]), sd[k], snm[k], snv[k])

    order = ["w_ada", "b_ada", "g_pre_mix", "g_post_mix", "w_in", "g_q_lat", "w_uq", "g_kv_lat", "w_ukv", "rel_bias", "sinks",
             "w_o", "g_pre_ffn", "g_post_ffn", "w_up", "conv_w", "conv_b", "w_down"]
    ref_shapes = dict(w_ada=w_ada.shape, w_in=w_in.shape, w_uq=w_uq.shape, w_ukv=w_ukv.shape, w_o=w_o.shape,
                      w_up=w_up.shape, w_down=w_down.shape)
    outs = []
    for k in range(4):
        for nm in order:
            arr = res[nm][k]
            outs.append(arr.reshape(ref_shapes[nm]) if nm in ref_shapes else arr)
    loss = lax.psum(loss_tile[0, 0], ("x", "y", "c"))
    return (loss, grad_x[None], *outs)
```

```python
import math

import jax
import jax.numpy as jnp
from jax import lax
from jax.experimental import pallas as pl
from jax.experimental.pallas import tpu as pltpu

F32 = jnp.float32
BF16 = jnp.bfloat16
MESH = pl.DeviceIdType.MESH
HIGHEST = lax.Precision.HIGHEST

N_DEV = 8
N_CHIP = 4
LANES = 128
MLA_NOPE = 128
MLA_ROPE = 64
MLA_V = 128
MLA_QK = MLA_NOPE + MLA_ROPE
MLA_QK_PAD = 256
ROPE_THETA = 10000.0
SWA_HD = 64
SWA_KVH = 4
SWA_BLOCK = 128
REL_BUCKETS = 32
REL_MAX_DIST = 128
PAIR = 512
EPS = 1e-6
NEG = -1e30
ADAM_LR = 0.001
ADAM_B1 = 0.9
ADAM_B2 = 0.999
ADAM_EPS = 1e-08
ADAM_WD = 0.01
ADAM_STEP = 10

ANY = pl.BlockSpec(memory_space=pl.ANY)
VMEM_FULL = pl.BlockSpec(memory_space=pltpu.VMEM)
SMEM_FULL = pl.BlockSpec(memory_space=pltpu.SMEM)


def _params(*sem):
    return pltpu.CompilerParams(dimension_semantics=sem if sem else None)


def _tied(body, tie):
    if tie is None:
        return body, [], []
    ties = list(tie) if isinstance(tie, (list, tuple)) else [tie]

    def tied_body(*refs):
        body(*refs[len(ties):])

    return tied_body, [ANY] * len(ties), ties


def _tile(n, pref, unit=LANES):
    best = None
    for t in range(unit, min(n, pref) + 1, unit):
        if n % t == 0:
            best = t
    return n if best is None else best


def _matmul(a, b, *, ta=False, tb=False, out_dtype=F32, tie=None, shards=None, bcols=None, name):
    a2 = a.shape[1:] if shards == "k" else a.shape
    b2 = b.shape[1:] if shards else b.shape
    nsh = b.shape[0] if shards else 1
    K, M = a2 if ta else a2[::-1]
    N, K2 = b2 if tb else b2[::-1]
    assert K == K2, (a.shape, b.shape, ta, tb)
    exact = a.dtype == F32
    col0 = 0
    if bcols is not None:
        assert not tb and shards is None
        col0, N = bcols
    tn = _tile(math.gcd(N, col0) if col0 else N, 2048)
    col0 //= tn
    tk = _tile(K, 2048)
    nkc = K // tk
    nk = nkc * (nsh if shards == "k" else 1)
    tm = M if M < 8 else _tile(M, 1024, LANES if ta else 8)
    dn = (((0 if ta else 1,), (1 if tb else 0,)), ((), ()))
    kax = 3 if shards == "out" else 2

    def product(a_ref, b_ref):
        return lax.dot_general(a_ref[...], b_ref[...], dn, preferred_element_type=F32,
                               precision=HIGHEST if exact else None)

    def body_acc(a_ref, b_ref, o_ref, acc_ref):
        k = pl.program_id(kax)

        @pl.when(k == 0)
        def _():
            acc_ref[...] = product(a_ref, b_ref)

        @pl.when(jnp.logical_and(k > 0, k < nk - 1))
        def _():
            acc_ref[...] += product(a_ref, b_ref)

        @pl.when(k == nk - 1)
        def _():
            o_ref[...] = (acc_ref[...] + product(a_ref, b_ref)).astype(o_ref.dtype)

    def body_one(a_ref, b_ref, o_ref):
        o_ref[...] = product(a_ref, b_ref).astype(o_ref.dtype)

    a_blk, b_blk = ((tk, tm) if ta else (tm, tk)), ((tn, tk) if tb else (tk, tn))
    a_at = (lambda i, k: (k, i)) if ta else (lambda i, k: (i, k))
    b_at = (lambda j, k: (j, k)) if tb else (lambda j, k: (k, j + col0))
    if shards == "out":
        grid = (nsh, M // tm, N // tn, nk)
        a_spec = pl.BlockSpec(a_blk, lambda s, i, j, k: a_at(i, k))
        b_spec = pl.BlockSpec((None,) + b_blk, lambda s, i, j, k: (s,) + b_at(j, k))
        o_spec = pl.BlockSpec((None, tm, tn), lambda s, i, j, k: (s, i, j))
        out_shape = jax.ShapeDtypeStruct((nsh, M, N), out_dtype)
        sem = ("parallel", "parallel", "parallel", "arbitrary")
    elif shards == "k":
        grid = (M // tm, N // tn, nk)
        a_spec = pl.BlockSpec((None,) + a_blk, lambda i, j, k: (k // nkc,) + a_at(i, k % nkc))
        b_spec = pl.BlockSpec((None,) + b_blk, lambda i, j, k: (k // nkc,) + b_at(j, k % nkc))
        o_spec = pl.BlockSpec((tm, tn), lambda i, j, k: (i, j))
        out_shape = jax.ShapeDtypeStruct((M, N), out_dtype)
        sem = ("parallel", "parallel", "arbitrary")
    else:
        grid = (M // tm, N // tn, nk)
        a_spec = pl.BlockSpec(a_blk, lambda i, j, k: a_at(i, k))
        b_spec = pl.BlockSpec(b_blk, lambda i, j, k: b_at(j, k))
        o_spec = pl.BlockSpec((tm, tn), lambda i, j, k: (i, j))
        out_shape = jax.ShapeDtypeStruct((M, N), out_dtype)
        sem = ("parallel", "parallel", "arbitrary")
    body, tspec, targ = _tied(body_one if nk == 1 else body_acc, tie)
    return pl.pallas_call(
        body, name=name, out_shape=out_shape, grid=grid, in_specs=tspec + [a_spec, b_spec], out_specs=o_spec,
        scratch_shapes=[] if nk == 1 else [pltpu.VMEM((tm, tn), F32)],
        compiler_params=_params(*sem),
    )(*targ, a, b)


def _row_tile(S, width):
    return _tile(S, max(8, (1 << 19) // width), 8)


def _rstd(x):
    return lax.rsqrt(jnp.mean(x * x, axis=-1, keepdims=True) + EPS)


def _acc_rows(ref, val, first):
    s = jnp.sum(val, axis=0, keepdims=True)

    @pl.when(first)
    def _():
        ref[...] = s

    @pl.when(jnp.logical_not(first))
    def _():
        ref[...] += s


def _modnorm_fwd(x, g, sc, sh, *, name):
    S, D = x.shape
    tr = _row_tile(S, D)

    def body(x_ref, g_ref, sc_ref, sh_ref, h_ref):
        xv = x_ref[...]
        n = (xv * _rstd(xv)) * g_ref[...]
        h_ref[...] = (n * (1.0 + sc_ref[...]) + sh_ref[...]).astype(BF16)

    row = pl.BlockSpec((tr, D), lambda i: (i, 0))
    vec = pl.BlockSpec((1, D), lambda i: (0, 0))
    return pl.pallas_call(
        body, name=name, out_shape=jax.ShapeDtypeStruct((S, D), BF16), grid=(S // tr,),
        in_specs=[row, vec, vec, vec], out_specs=row, compiler_params=_params("parallel"),
    )(x, g, sc, sh)


def _modnorm_bwd(dh, x, g, sc, dres, *, name):
    S, D = x.shape
    tr = _row_tile(S, D)

    def body(dh_ref, x_ref, g_ref, sc_ref, dres_ref, dx_ref, dg_ref, dsc_ref, dsh_ref):
        first = pl.program_id(0) == 0
        xv = x_ref[...]
        dhv = dh_ref[...]
        gv = g_ref[...]
        r = _rstd(xv)
        xhat = xv * r
        _acc_rows(dsh_ref, dhv, first)
        _acc_rows(dsc_ref, dhv * (xhat * gv), first)
        dn = dhv * (1.0 + sc_ref[...])
        _acc_rows(dg_ref, dn * xhat, first)
        dxhat = dn * gv
        proj = jnp.mean(dxhat * xhat, axis=-1, keepdims=True)
        dx_ref[...] = r * (dxhat - xhat * proj) + dres_ref[...]

    row = pl.BlockSpec((tr, D), lambda i: (i, 0))
    vec = pl.BlockSpec((1, D), lambda i: (0, 0))
    vshape = jax.ShapeDtypeStruct((1, D), F32)
    return pl.pallas_call(
        body, name=name,
        out_shape=(jax.ShapeDtypeStruct((S, D), F32), vshape, vshape, vshape), grid=(S // tr,),
        in_specs=[row, row, vec, vec, row], out_specs=(row, vec, vec, vec),
        compiler_params=_params("arbitrary"),
    )(dh, x, g, sc, dres)


def _resnorm_modnorm_fwd(xres, m, g, gt, g2, sc2, sh2, *, name):
    S, D = xres.shape
    tr = _row_tile(S, D)

    def body(x_ref, m_ref, g_ref, gt_ref, g2_ref, sc_ref, sh_ref, o_ref, h_ref):
        mv = m_ref[...]
        x1 = x_ref[...] + gt_ref[...] * ((mv * _rstd(mv)) * g_ref[...])
        o_ref[...] = x1
        n = (x1 * _rstd(x1)) * g2_ref[...]
        h_ref[...] = (n * (1.0 + sc_ref[...]) + sh_ref[...]).astype(BF16)

    row = pl.BlockSpec((tr, D), lambda i: (i, 0))
    vec = pl.BlockSpec((1, D), lambda i: (0, 0))
    return pl.pallas_call(
        body, name=name, out_shape=(jax.ShapeDtypeStruct((S, D), F32), jax.ShapeDtypeStruct((S, D), BF16)), grid=(S // tr,),
        in_specs=[row, row, vec, vec, vec, vec, vec], out_specs=(row, row), compiler_params=_params("parallel"),
    )(xres, m, g, gt, g2, sc2, sh2)


def _modnorm_resnorm_bwd(dh, x, g, sc, dres, m, g1, gt1, *, name):
    S, D = x.shape
    tr = _row_tile(S, D)

    def body(dh_ref, x_ref, g_ref, sc_ref, dres_ref, m_ref, g1_ref, gt1_ref,
             dx_ref, dg_ref, dsc_ref, dsh_ref, dm_ref, dg1_ref, dgt1_ref):
        first = pl.program_id(0) == 0
        xv = x_ref[...]
        dhv = dh_ref[...]
        gv = g_ref[...]
        r = _rstd(xv)
        xhat = xv * r
        _acc_rows(dsh_ref, dhv, first)
        _acc_rows(dsc_ref, dhv * (xhat * gv), first)
        dn = dhv * (1.0 + sc_ref[...])
        _acc_rows(dg_ref, dn * xhat, first)
        dxhat = dn * gv
        proj = jnp.mean(dxhat * xhat, axis=-1, keepdims=True)
        dx1 = r * (dxhat - xhat * proj) + dres_ref[...]
        dx_ref[...] = dx1
        mv = m_ref[...]
        g1v = g1_ref[...]
        r1 = _rstd(mv)
        mhat = mv * r1
        _acc_rows(dgt1_ref, dx1 * (mhat * g1v), first)
        dn1 = dx1 * gt1_ref[...]
        _acc_rows(dg1_ref, dn1 * mhat, first)
        dmhat = dn1 * g1v
        proj1 = jnp.mean(dmhat * mhat, axis=-1, keepdims=True)
        dm_ref[...] = (r1 * (dmhat - mhat * proj1)).astype(BF16)

    row = pl.BlockSpec((tr, D), lambda i: (i, 0))
    vec = pl.BlockSpec((1, D), lambda i: (0, 0))
    vshape = jax.ShapeDtypeStruct((1, D), F32)
    return pl.pallas_call(
        body, name=name,
        out_shape=(jax.ShapeDtypeStruct((S, D), F32), vshape, vshape, vshape, jax.ShapeDtypeStruct((S, D), BF16), vshape, vshape),
        grid=(S // tr,),
        in_specs=[row, row, vec, vec, row, row, vec, vec], out_specs=(row, vec, vec, vec, row, vec, vec),
        compiler_params=_params("arbitrary"),
    )(dh, x, g, sc, dres, m, g1, gt1)


def _resnorm_loss(xres, m, g, gt, target, *, name):
    S, D = xres.shape
    tr = _row_tile(S, D)

    def body(x_ref, m_ref, g_ref, gt_ref, t_ref, d_ref, dm_ref, dg_ref, dgt_ref, loss_ref):
        first = pl.program_id(0) == 0
        mv = m_ref[...]
        gv = g_ref[...]
        r = _rstd(mv)
        mhat = mv * r
        n = mhat * gv
        err = (x_ref[...] + gt_ref[...] * n) - t_ref[...]
        dv = err * (1.0 / D)
        d_ref[...] = dv
        part = 0.5 * jnp.sum(jnp.mean(err * err, axis=-1, keepdims=True), axis=0, keepdims=True)
        part = jnp.broadcast_to(part, loss_ref.shape)

        @pl.when(first)
        def _():
            loss_ref[...] = part

        @pl.when(jnp.logical_not(first))
        def _():
            loss_ref[...] += part

        _acc_rows(dgt_ref, dv * n, first)
        dn = dv * gt_ref[...]
        _acc_rows(dg_ref, dn * mhat, first)
        dmhat = dn * gv
        proj = jnp.mean(dmhat * mhat, axis=-1, keepdims=True)
        dm_ref[...] = (r * (dmhat - mhat * proj)).astype(BF16)

    row = pl.BlockSpec((tr, D), lambda i: (i, 0))
    vec = pl.BlockSpec((1, D), lambda i: (0, 0))
    vshape = jax.ShapeDtypeStruct((1, D), F32)
    return pl.pallas_call(
        body, name=name,
        out_shape=(jax.ShapeDtypeStruct((S, D), F32), jax.ShapeDtypeStruct((S, D), BF16), vshape, vshape,
                   jax.ShapeDtypeStruct((8, LANES), F32)), grid=(S // tr,),
        in_specs=[row, row, vec, vec, row], out_specs=(row, row, vec, vec, pl.BlockSpec((8, LANES), lambda i: (0, 0))),
        compiler_params=_params("arbitrary"),
    )(xres, m, g, gt, target)


def _lat_norm_fwd(z_lat, g_q, g_kv, *, name):
    S, W = z_lat.shape
    Rq, Rkv = g_q.shape[1], g_kv.shape[1]
    tr = _row_tile(S, W)

    def body(z_ref, gq_ref, gkv_ref, nq_ref, nkv_ref):
        cq = z_ref[:, :Rq]
        ckv = z_ref[:, Rq:Rq + Rkv]
        nq_ref[...] = ((cq * _rstd(cq)) * gq_ref[...]).astype(BF16)
        nkv_ref[...] = ((ckv * _rstd(ckv)) * gkv_ref[...]).astype(BF16)

    return pl.pallas_call(
        body, name=name,
        out_shape=(jax.ShapeDtypeStruct((S, Rq), BF16), jax.ShapeDtypeStruct((S, Rkv), BF16)), grid=(S // tr,),
        in_specs=[pl.BlockSpec((tr, W), lambda i: (i, 0)), pl.BlockSpec((1, Rq), lambda i: (0, 0)),
                  pl.BlockSpec((1, Rkv), lambda i: (0, 0))],
        out_specs=(pl.BlockSpec((tr, Rq), lambda i: (i, 0)), pl.BlockSpec((tr, Rkv), lambda i: (i, 0))),
        compiler_params=_params("parallel"),
    )(z_lat, g_q, g_kv)


def _lat_norm_bwd(z_lat, dnq, dnkv, dkr, g_q, g_kv, *, name):
    S, W = z_lat.shape
    Rq, Rkv = g_q.shape[1], g_kv.shape[1]
    tr = _row_tile(S, W)

    def one(c, dn, gv):
        r = _rstd(c)
        chat = c * r
        dchat = dn * gv
        proj = jnp.mean(dchat * chat, axis=-1, keepdims=True)
        return r * (dchat - chat * proj), dn * chat

    def body(z_ref, dnq_ref, dnkv_ref, dkr_ref, gq_ref, gkv_ref, dz_ref, dgq_ref, dgkv_ref):
        first = pl.program_id(0) == 0
        dcq, pq = one(z_ref[:, :Rq], dnq_ref[...], gq_ref[...])
        dckv, pkv = one(z_ref[:, Rq:Rq + Rkv], dnkv_ref[...], gkv_ref[...])
        _acc_rows(dgq_ref, pq, first)
        _acc_rows(dgkv_ref, pkv, first)
        dz_ref[:, :Rq] = dcq.astype(BF16)
        dz_ref[:, Rq:Rq + Rkv] = dckv.astype(BF16)
        dz_ref[:, Rq + Rkv:Rq + Rkv + LANES] = dkr_ref[...].astype(BF16)
        if W > Rq + Rkv + LANES:
            dz_ref[:, Rq + Rkv + LANES:] = jnp.zeros((tr, W - Rq - Rkv - LANES), BF16)

    return pl.pallas_call(
        body, name=name,
        out_shape=(jax.ShapeDtypeStruct((S, W), BF16), jax.ShapeDtypeStruct((1, Rq), F32),
                   jax.ShapeDtypeStruct((1, Rkv), F32)), grid=(S // tr,),
        in_specs=[pl.BlockSpec((tr, W), lambda i: (i, 0)), pl.BlockSpec((tr, Rq), lambda i: (i, 0)),
                  pl.BlockSpec((tr, Rkv), lambda i: (i, 0)), pl.BlockSpec((tr, LANES), lambda i: (i, 0)),
                  pl.BlockSpec((1, Rq), lambda i: (0, 0)), pl.BlockSpec((1, Rkv), lambda i: (0, 0))],
        out_specs=(pl.BlockSpec((tr, W), lambda i: (i, 0)), pl.BlockSpec((1, Rq), lambda i: (0, 0)),
                   pl.BlockSpec((1, Rkv), lambda i: (0, 0))),
        compiler_params=_params("arbitrary"),
    )(z_lat, dnq, dnkv, dkr, g_q, g_kv)


def _rot(x, lo32):
    a = pltpu.roll(x, 32, 1)
    b = pltpu.roll(x, LANES - 32, 1)
    return jnp.where(lo32, -b, a)


def _rot_t(g, lo32):
    a = pltpu.roll(g, 32, 1)
    b = pltpu.roll(g, LANES - 32, 1)
    return jnp.where(lo32, b, -a)


def _mla_pack_fwd(q_raw, kv_raw, z_lat, cos, sin, kr_off, *, name):
    S = q_raw.shape[0]
    H = kv_raw.shape[1] // (MLA_NOPE + MLA_V)
    assert kr_off % LANES == 0
    scale = MLA_QK ** -0.5
    tr = min(S, 128)
    nope_w = H * MLA_NOPE

    def body(q_ref, kv_ref, z_ref, cos_ref, sin_ref, qp_ref, kp_ref, v_ref):
        lane = lax.broadcasted_iota(jnp.int32, (tr, LANES), 1)
        lo32 = (lane % 64) < 32
        lo64 = lane < 64
        c = cos_ref[...]
        s = sin_ref[...]
        kr = z_ref[...]
        kr = (kr * c + _rot(kr, lo32) * s).astype(BF16)
        for hp in range(H // 2):
            xb = q_ref[:, nope_w + hp * LANES:nope_w + (hp + 1) * LANES].astype(F32)
            rb = (xb * c + _rot(xb, lo32) * s) * scale
            for e in range(2):
                h = 2 * hp + e
                base = h * MLA_QK_PAD
                qp_ref[:, base:base + LANES] = (q_ref[:, h * LANES:(h + 1) * LANES].astype(F32) * scale).astype(BF16)
                keep = lo64 if e == 0 else jnp.logical_not(lo64)
                qp_ref[:, base + LANES:base + 2 * LANES] = jnp.where(keep, rb, 0.0).astype(BF16)
                kp_ref[:, base:base + LANES] = kv_ref[:, h * LANES:(h + 1) * LANES].astype(BF16)
                kp_ref[:, base + LANES:base + 2 * LANES] = kr
        v_ref[...] = kv_ref[:, nope_w:].astype(BF16)

    return pl.pallas_call(
        body, name=name,
        out_shape=(jax.ShapeDtypeStruct((S, H * MLA_QK_PAD), BF16), jax.ShapeDtypeStruct((S, H * MLA_QK_PAD), BF16),
                   jax.ShapeDtypeStruct((S, H * MLA_V), BF16)), grid=(S // tr,),
        in_specs=[pl.BlockSpec((tr, q_raw.shape[1]), lambda i: (i, 0)), pl.BlockSpec((tr, kv_raw.shape[1]), lambda i: (i, 0)),
                  pl.BlockSpec((tr, LANES), lambda i: (i, kr_off // LANES)), pl.BlockSpec((tr, LANES), lambda i: (i, 0)),
                  pl.BlockSpec((tr, LANES), lambda i: (i, 0))],
        out_specs=(pl.BlockSpec((tr, H * MLA_QK_PAD), lambda i: (i, 0)), pl.BlockSpec((tr, H * MLA_QK_PAD), lambda i: (i, 0)),
                   pl.BlockSpec((tr, H * MLA_V), lambda i: (i, 0))),
        compiler_params=_params("parallel"),
    )(q_raw, kv_raw, z_lat, cos, sin)


def _mla_pack_bwd(dqp, dkp, dv, cos, sin, *, name):
    S = dqp.shape[0]
    H = dv.shape[1] // MLA_V
    scale = MLA_QK ** -0.5
    tr = min(S, 128)
    nope_w = H * MLA_NOPE

    def body(dqp_ref, dkp_ref, dv_ref, cos_ref, sin_ref, dq_ref, dkv_ref, dkr_ref):
        lane = lax.broadcasted_iota(jnp.int32, (tr, LANES), 1)
        lo32 = (lane % 64) < 32
        lo64 = lane < 64
        c = cos_ref[...]
        s = sin_ref[...]
        dkr2 = jnp.zeros((tr, LANES), F32)
        for hp in range(H // 2):
            be = (2 * hp) * MLA_QK_PAD
            bo = (2 * hp + 1) * MLA_QK_PAD
            g = jnp.where(lo64, dqp_ref[:, be + LANES:be + 2 * LANES].astype(F32),
                          dqp_ref[:, bo + LANES:bo + 2 * LANES].astype(F32)) * scale
            dq_ref[:, nope_w + hp * LANES:nope_w + (hp + 1) * LANES] = (g * c + _rot_t(g * s, lo32)).astype(BF16)
            for h, base in ((2 * hp, be), (2 * hp + 1, bo)):
                dq_ref[:, h * LANES:(h + 1) * LANES] = (dqp_ref[:, base:base + LANES].astype(F32) * scale).astype(BF16)
                dkv_ref[:, h * LANES:(h + 1) * LANES] = dkp_ref[:, base:base + LANES].astype(BF16)
                dkr2 = dkr2 + dkp_ref[:, base + LANES:base + 2 * LANES].astype(F32)
        dkr2 = dkr2 * c + _rot_t(dkr2 * s, lo32)
        dkr2 = dkr2 + pltpu.roll(dkr2, 64, 1)
        dkr_ref[...] = jnp.where(lo64, dkr2, 0.0)
        dkv_ref[:, nope_w:] = dv_ref[...].astype(BF16)

    return pl.pallas_call(
        body, name=name,
        out_shape=(jax.ShapeDtypeStruct((S, nope_w + H * MLA_ROPE), BF16), jax.ShapeDtypeStruct((S, 2 * nope_w), BF16),
                   jax.ShapeDtypeStruct((S, LANES), F32)), grid=(S // tr,),
        in_specs=[pl.BlockSpec((tr, H * MLA_QK_PAD), lambda i: (i, 0)), pl.BlockSpec((tr, H * MLA_QK_PAD), lambda i: (i, 0)),
                  pl.BlockSpec((tr, H * MLA_V), lambda i: (i, 0)), pl.BlockSpec((tr, LANES), lambda i: (i, 0)),
                  pl.BlockSpec((tr, LANES), lambda i: (i, 0))],
        out_specs=(pl.BlockSpec((tr, nope_w + H * MLA_ROPE), lambda i: (i, 0)), pl.BlockSpec((tr, 2 * nope_w), lambda i: (i, 0)),
                   pl.BlockSpec((tr, LANES), lambda i: (i, 0))),
        compiler_params=_params("parallel"),
    )(dqp, dkp, dv, cos, sin)


FLASH_HB_FWD = 8
FLASH_HB_BWD = 4


def _causal_pairs(nb):
    qi = [i for i in range(nb) for j in range(i + 1)]
    kj = [j for i in range(nb) for j in range(i + 1)]
    return jnp.asarray(qi, jnp.int32), jnp.asarray(kj, jnp.int32)


def _scores(q, k, diagonal, t):
    s = lax.dot_general(q, k, (((1,), (1,)), ((), ())), preferred_element_type=F32)
    if diagonal:
        row = lax.broadcasted_iota(jnp.int32, (t, t), 0)
        col = lax.broadcasted_iota(jnp.int32, (t, t), 1)
        s = jnp.where(col <= row, s, NEG)
    return s


def _flash_fwd(qp, kp, v, *, name):
    S = qp.shape[0]
    H = v.shape[1] // MLA_V
    t = min(S, 512)
    nb = S // t
    HB = min(FLASH_HB_FWD, H)
    qi, kj = _causal_pairs(nb)
    QW, VW = MLA_QK_PAD, MLA_V

    def body(qi_ref, kj_ref, q_ref, k_ref, v_ref, o_ref, lse_ref, m_s, l_s, acc_s):
        pr = pl.program_id(1)
        i = qi_ref[pr]
        j = kj_ref[pr]

        @pl.when(j == 0)
        def _():
            m_s[...] = jnp.full_like(m_s, NEG)
            l_s[...] = jnp.zeros_like(l_s)
            acc_s[...] = jnp.zeros_like(acc_s)

        def step(diagonal):
            state = [(m_s[hh], l_s[hh], acc_s[hh]) for hh in range(HB)]
            new = []
            for hh, (m_prev, l_prev, acc_prev) in enumerate(state):
                s = _scores(q_ref[:, hh * QW:(hh + 1) * QW], k_ref[:, hh * QW:(hh + 1) * QW], diagonal, t)
                m_cur = jnp.maximum(m_prev, jnp.max(s, axis=1, keepdims=True))
                alpha = jnp.exp(m_prev - m_cur)
                p = jnp.exp(s - m_cur[:, :1])
                l_new = alpha * l_prev + jnp.sum(p, axis=1, keepdims=True)
                acc = alpha * acc_prev + jnp.dot(p.astype(BF16), v_ref[:, hh * VW:(hh + 1) * VW], preferred_element_type=F32)
                new.append((m_cur, l_new, acc))
            for hh, (m_cur, l_new, acc) in enumerate(new):
                if diagonal:
                    o_ref[:, hh * VW:(hh + 1) * VW] = acc / l_new
                    lse_ref[hh] = m_cur + jnp.log(l_new)
                else:
                    l_s[hh] = l_new
                    acc_s[hh] = acc
                    m_s[hh] = m_cur

        @pl.when(i != j)
        def _():
            step(False)

        @pl.when(i == j)
        def _():
            step(True)

    return pl.pallas_call(
        body, name=name,
        out_shape=(jax.ShapeDtypeStruct((S, H * VW), F32), jax.ShapeDtypeStruct((H, S, LANES), F32)),
        grid_spec=pltpu.PrefetchScalarGridSpec(
            num_scalar_prefetch=2, grid=(H // HB, qi.shape[0]),
            in_specs=[pl.BlockSpec((t, HB * QW), lambda g, p, qi, kj: (qi[p], g)),
                      pl.BlockSpec((t, HB * QW), lambda g, p, qi, kj: (kj[p], g)),
                      pl.BlockSpec((t, HB * VW), lambda g, p, qi, kj: (kj[p], g))],
            out_specs=(pl.BlockSpec((t, HB * VW), lambda g, p, qi, kj: (qi[p], g)),
                       pl.BlockSpec((HB, t, LANES), lambda g, p, qi, kj: (g, qi[p], 0))),
            scratch_shapes=[pltpu.VMEM((HB, t, LANES), F32), pltpu.VMEM((HB, t, LANES), F32), pltpu.VMEM((HB, t, VW), F32)]),
        compiler_params=_params("parallel", "arbitrary"),
    )(qi, kj, qp, kp, v)


def _flash_bwd(qp, kp, v, o, do, lse, *, name):
    S = qp.shape[0]
    H = v.shape[1] // MLA_V
    t = min(S, 512)
    nb = S // t
    HB = min(FLASH_HB_BWD, H)
    qi = jnp.asarray([i for j in range(nb) for i in range(j, nb)], jnp.int32)
    kj = jnp.asarray([j for j in range(nb) for i in range(j, nb)], jnp.int32)
    npairs = qi.shape[0]
    QW, VW = MLA_QK_PAD, MLA_V
    tn = (((0,), (0,)), ((), ()))
    nt = (((1,), (1,)), ((), ()))

    def body(qi_ref, kj_ref, q_ref, k_ref, v_ref, o_ref, do_ref, lse_ref, dq_ref, dk_ref, dv_ref, dq_s, dk_s, dv_s):
        pr = pl.program_id(1)
        i = qi_ref[pr]
        j = kj_ref[pr]
        rows = pl.ds(pl.multiple_of(i * t, t), t)

        @pl.when(pr == 0)
        def _():
            dq_s[...] = jnp.zeros_like(dq_s)

        @pl.when(i == j)
        def _():
            dk_s[...] = jnp.zeros_like(dk_s)
            dv_s[...] = jnp.zeros_like(dv_s)

        def step(diagonal):
            for hh in range(HB):
                q = q_ref[:, hh * QW:(hh + 1) * QW]
                k = k_ref[:, hh * QW:(hh + 1) * QW]
                dob = do_ref[:, hh * VW:(hh + 1) * VW]
                p = jnp.exp(_scores(q, k, diagonal, t) - lse_ref[hh][:, :1])
                delta = jnp.sum(dob.astype(F32) * o_ref[:, hh * VW:(hh + 1) * VW], axis=1, keepdims=True)
                dp = lax.dot_general(dob, v_ref[:, hh * VW:(hh + 1) * VW], nt, preferred_element_type=F32)
                dsb = (p * (dp - delta)).astype(BF16)
                dv_s[:, hh * VW:(hh + 1) * VW] += lax.dot_general(p.astype(BF16), dob, tn, preferred_element_type=F32)
                dk_s[:, hh * QW:(hh + 1) * QW] += lax.dot_general(dsb, q, tn, preferred_element_type=F32)
                dq_s[rows, hh * QW:(hh + 1) * QW] += jnp.dot(dsb, k, preferred_element_type=F32)

        @pl.when(i != j)
        def _():
            step(False)

        @pl.when(i == j)
        def _():
            step(True)

        @pl.when(i == nb - 1)
        def _():
            dk_ref[...] = dk_s[...].astype(BF16)
            dv_ref[...] = dv_s[...].astype(BF16)

        @pl.when(pr == npairs - 1)
        def _():
            dq_ref[...] = dq_s[...].astype(BF16)

    qside = lambda g, p, qi, kj: (qi[p], g)
    kside = lambda g, p, qi, kj: (kj[p], g)
    whole = lambda g, p, qi, kj: (0, g)
    return pl.pallas_call(
        body, name=name,
        out_shape=(jax.ShapeDtypeStruct((S, H * QW), BF16), jax.ShapeDtypeStruct((S, H * QW), BF16),
                   jax.ShapeDtypeStruct((S, H * VW), BF16)),
        grid_spec=pltpu.PrefetchScalarGridSpec(
            num_scalar_prefetch=2, grid=(H // HB, npairs),
            in_specs=[pl.BlockSpec((t, HB * QW), qside), pl.BlockSpec((t, HB * QW), kside), pl.BlockSpec((t, HB * VW), kside),
                      pl.BlockSpec((t, HB * VW), qside), pl.BlockSpec((t, HB * VW), qside),
                      pl.BlockSpec((HB, t, LANES), lambda g, p, qi, kj: (g, qi[p], 0))],
            out_specs=(pl.BlockSpec((S, HB * QW), whole), pl.BlockSpec((t, HB * QW), kside), pl.BlockSpec((t, HB * VW), kside)),
            scratch_shapes=[pltpu.VMEM((S, HB * QW), F32), pltpu.VMEM((t, HB * QW), F32), pltpu.VMEM((t, HB * VW), F32)]),
        compiler_params=_params("parallel", "arbitrary"),
    )(qi, kj, qp, kp, v, o, do, lse)


def _swa_kv_halves(blk, hf, lo):
    if hf == 0:
        a = jnp.where(lo, blk, 0.0)
        b = pltpu.roll(a, 64, 1)
    else:
        b = jnp.where(lo, 0.0, blk)
        a = pltpu.roll(b, 64, 1)
    return a.astype(BF16), b.astype(BF16)


def _swa_softmax(qs, kx, bias, neg0, sk):
    s = lax.dot_general(qs, kx, (((1,), (1,)), ((), ())), preferred_element_type=F32) + bias + neg0
    m = jnp.maximum(jnp.max(s, axis=1, keepdims=True), sk)
    e = jnp.exp(s - m)
    es = jnp.exp(sk - m)
    inv = 1.0 / (jnp.sum(e, axis=1, keepdims=True) + es)
    return e * inv, es * inv


def _swa_stack(ref, kvh, npb, scale=None):
    parts = [ref[:, (kvh * npb + pb) * LANES:(kvh * npb + pb + 1) * LANES] for pb in range(npb)]
    x = jnp.concatenate(parts, axis=0)
    return x if scale is None else x * scale


def _swa_sink_col(sink_ref, kvh, e, npb):
    row = lax.broadcasted_iota(jnp.int32, (npb * SWA_BLOCK, 1), 0)
    col = jnp.zeros((npb * SWA_BLOCK, 1), F32) + sink_ref[2 * (kvh * npb) + e]
    for pb in range(1, npb):
        col = jnp.where(row >= pb * SWA_BLOCK, sink_ref[2 * (kvh * npb + pb) + e], col)
    return col


def _swa_fwd(z_swa, bias_st, sinks, *, name):
    S, W = z_swa.shape
    npb = bias_st.shape[1] // SWA_BLOCK
    NH = 2 * SWA_KVH * npb
    QW = NH * SWA_HD
    KW = SWA_KVH * SWA_HD
    nb = S // SWA_BLOCK
    B = SWA_BLOCK
    assert SWA_KVH % 2 == 0 and W == QW + 2 * KW

    def body(sink_ref, q_ref, kvc_ref, kvp_ref, b_ref, o_ref):
        n = pl.program_id(0)
        lo = lax.broadcasted_iota(jnp.int32, (2 * B, LANES), 1) < 64
        col = lax.broadcasted_iota(jnp.int32, (npb * B, 2 * B), 1)
        neg0 = jnp.where(jnp.logical_and(col < B, n == 0), NEG, 0.0)
        for kb in range(SWA_KVH // 2):
            kblk = jnp.concatenate([kvp_ref[:, kb * LANES:(kb + 1) * LANES], kvc_ref[:, kb * LANES:(kb + 1) * LANES]], axis=0)
            vblk = jnp.concatenate([kvp_ref[:, KW + kb * LANES:KW + (kb + 1) * LANES],
                                    kvc_ref[:, KW + kb * LANES:KW + (kb + 1) * LANES]], axis=0)
            for hf in range(2):
                kvh = 2 * kb + hf
                ks = _swa_kv_halves(kblk, hf, lo)
                vs = _swa_kv_halves(vblk, hf, lo)
                qs = _swa_stack(q_ref, kvh, npb, SWA_HD ** -0.5).astype(BF16)
                acc = jnp.zeros((npb * B, LANES), F32)
                for e in range(2):
                    p, _ = _swa_softmax(qs, ks[e], b_ref[2 * kvh + e], neg0, _swa_sink_col(sink_ref, kvh, e, npb))
                    acc = acc + jnp.dot(p.astype(BF16), vs[e], preferred_element_type=F32)
                for pb in range(npb):
                    P = kvh * npb + pb
                    o_ref[:, P * LANES:(P + 1) * LANES] = acc[pb * B:(pb + 1) * B]

    kvcol = QW // (2 * KW)
    assert QW % (2 * KW) == 0
    return pl.pallas_call(
        body, name=name,
        out_shape=jax.ShapeDtypeStruct((S, QW), F32), grid=(nb,),
        in_specs=[SMEM_FULL, pl.BlockSpec((B, QW), lambda n: (n, 0)), pl.BlockSpec((B, 2 * KW), lambda n: (n, kvcol)),
                  pl.BlockSpec((B, 2 * KW), lambda n: (jnp.maximum(n - 1, 0), kvcol)),
                  pl.BlockSpec(bias_st.shape, lambda n: (0, 0, 0))],
        out_specs=pl.BlockSpec((B, QW), lambda n: (n, 0)),
        compiler_params=_params("parallel"),
    )(sinks, z_swa, z_swa, z_swa, bias_st)


def _swa_bwd(z_swa, bias_st, sinks, o, do, *, name):
    S, W = z_swa.shape
    npb = bias_st.shape[1] // SWA_BLOCK
    NH = 2 * SWA_KVH * npb
    QW = NH * SWA_HD
    KW = SWA_KVH * SWA_HD
    nb = S // SWA_BLOCK
    B = SWA_BLOCK
    scale = SWA_HD ** -0.5
    tn = (((0,), (0,)), ((), ()))
    nt = (((1,), (1,)), ((), ()))

    def fold(x, hf, lo):
        x = x + pltpu.roll(x, 64, 1)
        return jnp.where(lo, x, 0.0) if hf == 0 else jnp.where(lo, 0.0, x)

    def body(sink_ref, q_ref, kvc_ref, kvp_ref, b_ref, o_ref, do_ref, dz_ref, dbias_ref, dsink_ref,
             cq_s, ck_s, cv_s, nq_s, nk_s, nv_s, pk_s, pv_s):
        n = pl.program_id(0)

        @pl.when(n == 0)
        def _():
            dbias_ref[...] = jnp.zeros_like(dbias_ref)
            dsink_ref[...] = jnp.zeros_like(dsink_ref)
            cq_s[...] = jnp.zeros_like(cq_s)
            ck_s[...] = jnp.zeros_like(ck_s)
            cv_s[...] = jnp.zeros_like(cv_s)

        @pl.when(n == nb)
        def _():
            pk_s[...] = jnp.zeros_like(pk_s)
            pv_s[...] = jnp.zeros_like(pv_s)

        @pl.when(n < nb)
        def _():
            lo = lax.broadcasted_iota(jnp.int32, (2 * B, LANES), 1) < 64
            lo1 = lax.broadcasted_iota(jnp.int32, (npb * B, LANES), 1) < 64
            lane1 = lax.broadcasted_iota(jnp.int32, (1, LANES), 1)
            col = lax.broadcasted_iota(jnp.int32, (npb * B, 2 * B), 1)
            neg0 = jnp.where(jnp.logical_and(col < B, n == 0), NEG, 0.0)
            dsink = jnp.zeros((1, LANES), F32)
            for kb in range(SWA_KVH // 2):
                kblk = jnp.concatenate([kvp_ref[:, kb * LANES:(kb + 1) * LANES], kvc_ref[:, kb * LANES:(kb + 1) * LANES]], axis=0)
                vblk = jnp.concatenate([kvp_ref[:, KW + kb * LANES:KW + (kb + 1) * LANES],
                                        kvc_ref[:, KW + kb * LANES:KW + (kb + 1) * LANES]], axis=0)
                dkblk = jnp.zeros((2 * B, LANES), F32)
                dvblk = jnp.zeros((2 * B, LANES), F32)
                for hf in range(2):
                    kvh = 2 * kb + hf
                    ks = _swa_kv_halves(kblk, hf, lo)
                    vs = _swa_kv_halves(vblk, hf, lo)
                    qs = _swa_stack(q_ref, kvh, npb, scale).astype(BF16)
                    dob = _swa_stack(do_ref, kvh, npb)
                    prod = dob.astype(F32) * _swa_stack(o_ref, kvh, npb)
                    dkj = jnp.zeros((2 * B, LANES), F32)
                    dvj = jnp.zeros((2 * B, LANES), F32)
                    dqs = jnp.zeros((npb * B, LANES), F32)
                    for e in range(2):
                        keep = lo1 if e == 0 else jnp.logical_not(lo1)
                        p, psink = _swa_softmax(qs, ks[e], b_ref[2 * kvh + e], neg0, _swa_sink_col(sink_ref, kvh, e, npb))
                        delta = jnp.sum(jnp.where(keep, prod, 0.0), axis=1, keepdims=True)
                        dp = lax.dot_general(dob, vs[e], nt, preferred_element_type=F32)
                        ds = p * (dp - delta)
                        dbias_ref[2 * kvh + e] += ds
                        pd = psink * delta
                        for pb in range(npb):
                            dsh = -jnp.sum(pd[pb * B:(pb + 1) * B], axis=0, keepdims=True)
                            dsink = dsink + jnp.where(lane1 == 2 * (kvh * npb + pb) + e, dsh, 0.0)
                        dsb = ds.astype(BF16)
                        dqs = dqs + jnp.dot(dsb, ks[e], preferred_element_type=F32)
                        keep2 = lo if e == 0 else jnp.logical_not(lo)
                        dkj = dkj + jnp.where(keep2, lax.dot_general(dsb, qs, tn, preferred_element_type=F32), 0.0)
                        dvj = dvj + jnp.where(keep2, lax.dot_general(p.astype(BF16), dob, tn, preferred_element_type=F32), 0.0)
                    for pb in range(npb):
                        P = kvh * npb + pb
                        nq_s[:, P * LANES:(P + 1) * LANES] = dqs[pb * B:(pb + 1) * B] * scale
                    dkblk = dkblk + fold(dkj, hf, lo)
                    dvblk = dvblk + fold(dvj, hf, lo)
                pk_s[:, kb * LANES:(kb + 1) * LANES] = dkblk[:B]
                nk_s[:, kb * LANES:(kb + 1) * LANES] = dkblk[B:]
                pv_s[:, kb * LANES:(kb + 1) * LANES] = dvblk[:B]
                nv_s[:, kb * LANES:(kb + 1) * LANES] = dvblk[B:]
            dsink_ref[...] += dsink

        dz_ref[:, :QW] = cq_s[...].astype(BF16)
        dz_ref[:, QW:QW + KW] = (ck_s[...] + pk_s[...]).astype(BF16)
        dz_ref[:, QW + KW:] = (cv_s[...] + pv_s[...]).astype(BF16)

        @pl.when(n < nb)
        def _():
            cq_s[...] = nq_s[...]
            ck_s[...] = nk_s[...]
            cv_s[...] = nv_s[...]

    kvcol = QW // (2 * KW)
    cur = lambda n: (jnp.minimum(n, nb - 1), 0)
    return pl.pallas_call(
        body, name=name,
        out_shape=(jax.ShapeDtypeStruct((S, W), BF16), jax.ShapeDtypeStruct(bias_st.shape, F32),
                   jax.ShapeDtypeStruct((1, LANES), F32)),
        grid=(nb + 1,),
        in_specs=[SMEM_FULL, pl.BlockSpec((B, QW), cur), pl.BlockSpec((B, 2 * KW), lambda n: (jnp.minimum(n, nb - 1), kvcol)),
                  pl.BlockSpec((B, 2 * KW), lambda n: (jnp.maximum(jnp.minimum(n, nb - 1) - 1, 0), kvcol)),
                  pl.BlockSpec(bias_st.shape, lambda n: (0, 0, 0)), pl.BlockSpec((B, QW), cur), pl.BlockSpec((B, QW), cur)],
        out_specs=(pl.BlockSpec((B, W), lambda n: (jnp.maximum(n - 1, 0), 0)),
                   pl.BlockSpec(bias_st.shape, lambda n: (0, 0, 0)), pl.BlockSpec((1, LANES), lambda n: (0, 0))),
        scratch_shapes=[pltpu.VMEM((B, QW), F32), pltpu.VMEM((B, KW), F32), pltpu.VMEM((B, KW), F32),
                        pltpu.VMEM((B, QW), F32), pltpu.VMEM((B, KW), F32), pltpu.VMEM((B, KW), F32),
                        pltpu.VMEM((B, KW), F32), pltpu.VMEM((B, KW), F32)],
        compiler_params=_params("arbitrary"),
    )(sinks, z_swa, z_swa, z_swa, bias_st, o, do)


def _gate_fwd(zg, o_a, o_b, *, name):
    S, D = o_a.shape
    tr = min(S, 1024)

    def body(z_ref, a_ref, b_ref, m_ref):
        ga = jax.nn.sigmoid(z_ref[:, :PAIR].astype(F32))
        gb = jax.nn.sigmoid(z_ref[:, PAIR:].astype(F32))
        m_ref[...] = (ga * a_ref[...] + gb * b_ref[...]).astype(BF16)

    col = pl.BlockSpec((tr, PAIR), lambda i, j: (i, j))
    return pl.pallas_call(
        body, name=name, out_shape=jax.ShapeDtypeStruct((S, D), BF16), grid=(S // tr, D // PAIR),
        in_specs=[pl.BlockSpec((tr, 2 * PAIR), lambda i, j: (i, j)), col, col], out_specs=col,
        compiler_params=_params("parallel", "parallel"),
    )(zg, o_a, o_b)


def _gate_bwd(dmix, zg, o_a, o_b, *, name):
    S, D = o_a.shape
    tr = min(S, 1024)

    def body(d_ref, z_ref, a_ref, b_ref, da_ref, db_ref, dz_ref):
        d = d_ref[...].astype(F32)
        ga = jax.nn.sigmoid(z_ref[:, :PAIR].astype(F32))
        gb = jax.nn.sigmoid(z_ref[:, PAIR:].astype(F32))
        da_ref[...] = (d * ga).astype(BF16)
        db_ref[...] = (d * gb).astype(BF16)
        dz_ref[:, :PAIR] = (d * a_ref[...] * (ga * (1.0 - ga))).astype(BF16)
        dz_ref[:, PAIR:] = (d * b_ref[...] * (gb * (1.0 - gb))).astype(BF16)

    col = pl.BlockSpec((tr, PAIR), lambda i, j: (i, j))
    wide = pl.BlockSpec((tr, 2 * PAIR), lambda i, j: (i, j))
    return pl.pallas_call(
        body, name=name,
        out_shape=(jax.ShapeDtypeStruct((S, D), BF16), jax.ShapeDtypeStruct((S, D), BF16), jax.ShapeDtypeStruct((S, 2 * D), BF16)),
        grid=(S // tr, D // PAIR), in_specs=[col, wide, col, col], out_specs=(col, col, wide),
        compiler_params=_params("parallel", "parallel"),
    )(dmix, zg, o_a, o_b)


def _conv_u(t_ref, prev_ref, w_ref, b_ref, m, i):
    cur = t_ref[m].astype(F32)
    live = (i > 0).astype(F32)
    p6 = prev_ref[m, 14:15, :].astype(F32) * live
    p7 = prev_ref[m, 15:16, :].astype(F32) * live
    row = lax.broadcasted_iota(jnp.int32, cur.shape, 0)
    t1 = jnp.where(row == 0, p7, pltpu.roll(cur, 1, 0))
    t2 = jnp.where(row == 0, p6, jnp.where(row == 1, p7, pltpu.roll(cur, 2, 0)))
    u = ((b_ref[m] + w_ref[m, 0:1, :] * t2) + w_ref[m, 1:2, :] * t1) + w_ref[m, 2:3, :] * cur
    return u, cur, t1, t2


def _conv_specs(tr, tc):
    blk = pl.BlockSpec((2, tr, tc), lambda p, j, i: (p, i, j))
    prev = pl.BlockSpec((2, 16, tc), lambda p, j, i: (p, jnp.maximum(i * (tr // 16) - 1, 0), j))
    w3 = pl.BlockSpec((2, 3, tc), lambda p, j, i: (p, 0, j))
    w1 = pl.BlockSpec((2, 1, tc), lambda p, j, i: (p, 0, j))
    return blk, prev, w3, w1


def _conv_gate_fwd(t, cw, cb, *, name):
    _, S, C = t.shape
    tr, tc = min(S, 512), _tile(C, 1536)
    ncol = C // tc
    blk, prev, w3, w1 = _conv_specs(tr, tc)

    def body(t_ref, prev_ref, w_ref, b_ref, a_ref):
        i = pl.program_id(2)
        u1 = _conv_u(t_ref, prev_ref, w_ref, b_ref, 0, i)[0]
        u2 = _conv_u(t_ref, prev_ref, w_ref, b_ref, 1, i)[0]
        a_ref[...] = (jax.nn.silu(u1) * u2).astype(BF16)

    return pl.pallas_call(
        body, name=name, out_shape=jax.ShapeDtypeStruct((S, 2 * C), BF16), grid=(2, ncol, S // tr),
        in_specs=[blk, prev, w3, w1], out_specs=pl.BlockSpec((tr, tc), lambda p, j, i: (i, p * ncol + j)),
        compiler_params=_params("parallel", "parallel", "parallel"),
    )(t, t, cw, cb)


def _conv_gate_bwd(t, da, cw, cb, *, name):
    _, S, C = t.shape
    tr, tc = min(S, 256), _tile(C, 1536)
    ncol = C // tc
    blk, prev, w3, w1 = _conv_specs(tr, tc)

    def body(t_ref, prev_ref, da_ref, w_ref, b_ref, du_ref, dw_ref, db_ref):
        i = pl.program_id(2)
        first = i == 0
        u1, c1, a1, b1 = _conv_u(t_ref, prev_ref, w_ref, b_ref, 0, i)
        u2, c2, a2, b2 = _conv_u(t_ref, prev_ref, w_ref, b_ref, 1, i)
        d = da_ref[...].astype(F32)
        sg = jax.nn.sigmoid(u1)
        du1 = d * u2 * (sg * (1.0 + u1 * (1.0 - sg)))
        du2 = d * (u1 * sg)
        for m, (du, cur, t1, t2) in enumerate(((du1, c1, a1, b1), (du2, c2, a2, b2))):
            du_ref[m] = du.astype(BF16)
            dw = jnp.concatenate([jnp.sum(du * t2, axis=0, keepdims=True), jnp.sum(du * t1, axis=0, keepdims=True),
                                  jnp.sum(du * cur, axis=0, keepdims=True)], axis=0)
            db = jnp.sum(du, axis=0, keepdims=True)

            @pl.when(first)
            def _():
                dw_ref[m] = dw
                db_ref[m] = db

            @pl.when(jnp.logical_not(first))
            def _():
                dw_ref[m] += dw
                db_ref[m] += db

    return pl.pallas_call(
        body, name=name,
        out_shape=(jax.ShapeDtypeStruct(t.shape, BF16), jax.ShapeDtypeStruct(cw.shape, F32), jax.ShapeDtypeStruct(cb.shape, F32)),
        grid=(2, ncol, S // tr),
        in_specs=[blk, prev, pl.BlockSpec((tr, tc), lambda p, j, i: (i, p * ncol + j)), w3, w1], out_specs=(blk, w3, w1),
        compiler_params=_params("parallel", "parallel", "arbitrary"),
    )(t, t, da, cw, cb)


def _conv_bwd_dt(du, cw, *, name):
    _, S, C = du.shape
    tr, tc = min(S, 512), _tile(C, 1536)
    nrow = S // tr
    blk, _, w3, _ = _conv_specs(tr, tc)
    nxt = pl.BlockSpec((2, 16, tc), lambda p, j, i: (p, jnp.minimum((i + 1) * (tr // 16), S // 16 - 1), j))

    def body(d_ref, next_ref, w_ref, dt_ref):
        i = pl.program_id(2)
        live = (i < nrow - 1).astype(F32)
        for m in range(2):
            cur = d_ref[m].astype(F32)
            n0 = next_ref[m, 0:1, :].astype(F32) * live
            n1 = next_ref[m, 1:2, :].astype(F32) * live
            row = lax.broadcasted_iota(jnp.int32, cur.shape, 0)
            d1 = jnp.where(row == tr - 1, n0, pltpu.roll(cur, tr - 1, 0))
            d2 = jnp.where(row == tr - 1, n1, jnp.where(row == tr - 2, n0, pltpu.roll(cur, tr - 2, 0)))
            dt_ref[m] = ((w_ref[m, 2:3, :] * cur + w_ref[m, 1:2, :] * d1) + w_ref[m, 0:1, :] * d2).astype(BF16)

    return pl.pallas_call(
        body, name=name, out_shape=jax.ShapeDtypeStruct(du.shape, BF16), grid=(2, C // tc, nrow),
        in_specs=[blk, nxt, w3], out_specs=blk, compiler_params=_params("parallel", "parallel", "parallel"),
    )(du, du, cw)


def _ada_fwd(c_all, w, b, *, name):
    Bn, D = c_all.shape
    N = w.shape[1]
    tn = _tile(N, 512)

    def body(c_ref, w_ref, b_ref, o_ref):
        o_ref[...] = jnp.dot(jax.nn.silu(c_ref[...]), w_ref[...], preferred_element_type=F32, precision=HIGHEST) + b_ref[...]

    return pl.pallas_call(
        body, name=name, out_shape=jax.ShapeDtypeStruct((Bn, N), F32), grid=(N // tn,),
        in_specs=[pl.BlockSpec((Bn, D), lambda j: (0, 0)), pl.BlockSpec((D, tn), lambda j: (0, j)),
                  pl.BlockSpec((1, tn), lambda j: (0, j))],
        out_specs=pl.BlockSpec((Bn, tn), lambda j: (0, j)), compiler_params=_params("parallel"),
    )(c_all, w, b)


def _ada_bwd(c_all_t, dmod, *, name):
    D, Bn = c_all_t.shape
    N = dmod.shape[1]
    tm = _tile(D, 512, 8)
    tn = _tile(N, 1536)

    def body(c_ref, d_ref, o_ref):
        o_ref[...] = jnp.dot(jax.nn.silu(c_ref[...]), d_ref[...], preferred_element_type=F32, precision=HIGHEST)

    return pl.pallas_call(
        body, name=name, out_shape=jax.ShapeDtypeStruct((D, N), F32), grid=(D // tm, N // tn),
        in_specs=[pl.BlockSpec((tm, Bn), lambda i, j: (i, 0)), pl.BlockSpec((Bn, tn), lambda i, j: (0, j))],
        out_specs=pl.BlockSpec((tm, tn), lambda i, j: (i, j)), compiler_params=_params("parallel", "parallel"),
    )(c_all_t, dmod)


def _adamw(w, g, m, v, *, name):
    R, C = w.shape
    tr = R if R * C <= (1 << 19) else _tile(R, max(8, (1 << 19) // C), 8)

    def body(w_ref, g_ref, m_ref, v_ref, d_ref, nm_ref, nv_ref):
        gv = g_ref[...]
        nm = ADAM_B1 * m_ref[...] + (1.0 - ADAM_B1) * gv
        nv = ADAM_B2 * v_ref[...] + (1.0 - ADAM_B2) * (gv * gv)
        m_hat = nm / (1.0 - ADAM_B1 ** ADAM_STEP)
        v_hat = nv / (1.0 - ADAM_B2 ** ADAM_STEP)
        d_ref[...] = -ADAM_LR * (m_hat / (jnp.sqrt(v_hat) + ADAM_EPS) + ADAM_WD * w_ref[...])
        nm_ref[...] = nm
        nv_ref[...] = nv

    blk = pl.BlockSpec((tr, C), lambda i: (i, 0))
    shp = jax.ShapeDtypeStruct((R, C), F32)
    return pl.pallas_call(
        body, name=name, out_shape=(shp, shp, shp), grid=(R // tr,), in_specs=[blk] * 4, out_specs=(blk,) * 3,
        compiler_params=_params("parallel"),
    )(w, g, m, v)


def _place():
    x, y, c = lax.axis_index("x"), lax.axis_index("y"), lax.axis_index("c")
    return x, y, c, [(1 - x, y), (x, 1 - y), (1 - x, 1 - y)]


def _remote(src, dst, send_sem, recv_sem, dev):
    return pltpu.make_async_remote_copy(src_ref=src, dst_ref=dst, send_sem=send_sem, recv_sem=recv_sem,
                                        device_id=dev, device_id_type=MESH)


def _allgather8(v, *, tie=None, name):
    R, C = v.shape

    def body(v_ref, out_ref, send_sems, recv_sems, local_sem):
        x, y, c, chips = _place()
        me, sibling = (x, y, c), (x, y, 1 - c)

        def rows(px, py, pc):
            return out_ref.at[pl.ds((4 * px + 2 * py + pc) * R, R), :]

        def copy(k, block, to, src=None):
            return _remote(rows(*block) if src is None else src, rows(*block), send_sems.at[k], recv_sems.at[k], to)

        mine = pltpu.make_async_copy(v_ref, rows(*me), local_sem)
        mine.start()
        first = [copy(0, me, sibling, src=v_ref)]
        first += [copy(1 + j, me, (*chip, c), src=v_ref) for j, chip in enumerate(chips)]
        for cp in first:
            cp.start()
        passed = [copy(4 + j, (*chip, c), sibling) for j, chip in enumerate(chips)]
        for j, chip in enumerate(chips):
            copy(1 + j, (*chip, c), me).wait_recv()
            passed[j].start()
        copy(0, sibling, me).wait_recv()
        for j, chip in enumerate(chips):
            copy(4 + j, (*chip, 1 - c), me).wait_recv()
        for cp in first + passed:
            cp.wait_send()
        mine.wait()

    body, tspec, targ = _tied(body, tie)
    out = pl.pallas_call(
        body, name=name, out_shape=jax.ShapeDtypeStruct((N_DEV * R, C), v.dtype),
        in_specs=tspec + [VMEM_FULL], out_specs=VMEM_FULL,
        scratch_shapes=[pltpu.SemaphoreType.DMA((7,)), pltpu.SemaphoreType.DMA((7,)), pltpu.SemaphoreType.DMA],
    )(*targ, v)
    return out.reshape(N_DEV, R, C)


SEM = pl.BlockSpec(memory_space=pltpu.SEMAPHORE)
HBM = pl.BlockSpec(memory_space=pltpu.HBM)
EFFECT = pltpu.SideEffectType.DATAFLOW_SIDE_EFFECTING
DMA_SEM = pltpu.SemaphoreType.DMA(())


def _in_hbm(a):
    return pltpu.with_memory_space_constraint(a, pltpu.HBM)


def _three_halves(land, r2):
    return land.at[pl.ds(0, N_CHIP - 1), pl.ds(0, r2)]


def _slot(chip, swap):
    return (chip % 2) * 2 + chip // 2 if swap else chip


def _gather_start(ws, after, swaps, *, name):
    n = len(ws)
    na = len(after)
    lands = [lax.empty((N_CHIP,) + w.shape, w.dtype) for w in ws]

    def body(*refs):
        w_refs, land_refs = refs[:n], refs[n:2 * n]
        send, recv = refs[2 * n + na:3 * n + na], refs[3 * n + na:4 * n + na]
        token = refs[6 * n + na]
        x, y, c, chips = _place()
        k = 2 * x + y
        for i in range(n):
            r2 = ws[i].shape[0] // 2
            for cx, cy in chips:
                _remote(w_refs[i].at[pl.ds(c * r2, r2)], land_refs[i].at[_slot(k, swaps[i]), pl.ds(c * r2, r2)], send[i], recv[i],
                        (cx, cy, c)).start()
        token[...] = jnp.zeros_like(token)

    outs = pl.pallas_call(
        body, name=name,
        out_shape=[DMA_SEM] * (2 * n) + [pltpu.HBM(w.shape, w.dtype) for w in ws] + [pltpu.HBM(l.shape, l.dtype) for l in lands]
        + [jax.ShapeDtypeStruct((8, LANES), F32)],
        in_specs=[HBM] * (2 * n) + [ANY] * na, out_specs=[SEM] * (2 * n) + [HBM] * (2 * n) + [VMEM_FULL],
        input_output_aliases={i: 2 * n + i for i in range(2 * n)},
        compiler_params=pltpu.CompilerParams(has_side_effects=EFFECT),
    )(*[_in_hbm(w) for w in ws], *[_in_hbm(l) for l in lands], *after)
    return outs[:n], outs[n:2 * n], outs[2 * n:3 * n], outs[3 * n:4 * n], outs[4 * n]


def _gather_forward(send, recv, ws, lands, after, swaps, *, name):
    n = len(ws)

    def body(*refs):
        w_refs, land_refs = refs[:n], refs[n:2 * n]
        send1, recv1 = refs[2 * n:3 * n], refs[3 * n:4 * n]
        send2, recv2 = refs[4 * n + 1 + 2 * n:4 * n + 1 + 3 * n], refs[4 * n + 1 + 3 * n:4 * n + 1 + 4 * n]
        x, y, c, chips = _place()
        sibling = (x, y, 1 - c)
        for i in range(n):
            r2 = ws[i].shape[0] // 2
            win = _three_halves(land_refs[i], r2)
            done = _remote(win, win, send1[i], recv1[i], sibling)
            done.wait_send()
            done.wait_recv()
            for cx, cy in chips:
                got = land_refs[i].at[_slot(2 * cx + cy, swaps[i]), pl.ds(c * r2, r2)]
                _remote(got, got, send2[i], recv2[i], sibling).start()
        token = refs[8 * n + 1]
        token[...] = jnp.zeros_like(token)

    outs = pl.pallas_call(
        body, name=name,
        out_shape=[pltpu.HBM(w.shape, w.dtype) for w in ws] + [pltpu.HBM(l.shape, l.dtype) for l in lands] + [DMA_SEM] * (2 * n)
        + [jax.ShapeDtypeStruct((8, LANES), F32)],
        in_specs=[HBM] * (2 * n) + [SEM] * (2 * n) + [ANY], out_specs=[HBM] * (2 * n) + [SEM] * (2 * n) + [VMEM_FULL],
        input_output_aliases={i: i for i in range(2 * n)},
        compiler_params=pltpu.CompilerParams(has_side_effects=EFFECT),
    )(*ws, *lands, *send, *recv, after)
    return outs[2 * n:3 * n], outs[3 * n:4 * n], outs[n:2 * n], outs[4 * n]


def _gather_finish(send, recv, lands, after, *, name):
    n = len(lands)

    def body(*refs):
        land_refs = refs[:n]
        send2, recv2 = refs[n:2 * n], refs[2 * n:3 * n]
        x, y, c, _ = _place()
        for i in range(n):
            win = _three_halves(land_refs[i], lands[i].shape[1] // 2)
            done = _remote(win, win, send2[i], recv2[i], (x, y, 1 - c))
            done.wait_send()
            done.wait_recv()

    return pl.pallas_call(
        body, name=name,
        out_shape=[pltpu.HBM(l.shape, l.dtype) for l in lands],
        in_specs=[HBM] * n + [SEM] * (2 * n) + [ANY], out_specs=[HBM] * n,
        input_output_aliases={i: i for i in range(n)},
        compiler_params=pltpu.CompilerParams(has_side_effects=EFFECT),
    )(*lands, *send, *recv, after)


def _scatter_start(gs, swaps, *, name):
    n = len(gs)
    lands = [lax.empty((N_DEV, g.shape[1] // 2, g.shape[2]), g.dtype) for g in gs]

    def body(*refs):
        g_refs, land_refs = refs[:n], refs[n:2 * n]
        send, recv = refs[2 * n:3 * n], refs[3 * n:4 * n]
        token = refs[6 * n]
        x, y, c, chips = _place()
        k = 2 * x + y
        me = 2 * k + c
        for i in range(n):
            r2 = gs[i].shape[1] // 2
            for cx, cy in chips:
                for cc in range(2):
                    _remote(g_refs[i].at[_slot(2 * cx + cy, swaps[i]), pl.ds(cc * r2, r2)], land_refs[i].at[me], send[i], recv[i],
                            (cx, cy, cc)).start()
            _remote(g_refs[i].at[_slot(k, swaps[i]), pl.ds((1 - c) * r2, r2)], land_refs[i].at[me], send[i], recv[i],
                    (x, y, 1 - c)).start()
        token[...] = jnp.zeros_like(token)

    outs = pl.pallas_call(
        body, name=name,
        out_shape=[DMA_SEM] * (2 * n) + [pltpu.HBM(g.shape, g.dtype) for g in gs] + [pltpu.HBM(l.shape, l.dtype) for l in lands]
        + [jax.ShapeDtypeStruct((8, LANES), F32)],
        in_specs=[HBM] * (2 * n), out_specs=[SEM] * (2 * n) + [HBM] * (2 * n) + [VMEM_FULL],
        input_output_aliases={i: 2 * n + i for i in range(2 * n)},
        compiler_params=pltpu.CompilerParams(has_side_effects=EFFECT),
    )(*[_in_hbm(g) for g in gs], *[_in_hbm(l) for l in lands])
    return outs[:n], outs[n:2 * n], outs[2 * n:3 * n], outs[3 * n:4 * n], outs[4 * n]


def _scatter_wait(send, recv, gs, lands, after, *, name):
    n = len(gs)

    def body(*refs):
        land_refs = refs[n:2 * n]
        send1, recv1 = refs[2 * n:3 * n], refs[3 * n:4 * n]
        x, y, c, _ = _place()
        for i in range(n):
            win = land_refs[i].at[pl.ds(0, N_DEV - 1)]
            done = _remote(win, win, send1[i], recv1[i], (x, y, 1 - c))
            done.wait_send()
            done.wait_recv()

    outs = pl.pallas_call(
        body, name=name,
        out_shape=[pltpu.HBM(g.shape, g.dtype) for g in gs] + [pltpu.HBM(l.shape, l.dtype) for l in lands],
        in_specs=[HBM] * (2 * n) + [SEM] * (2 * n) + [ANY], out_specs=[HBM] * (2 * n),
        input_output_aliases={i: i for i in range(2 * n)},
        compiler_params=pltpu.CompilerParams(has_side_effects=EFFECT),
    )(*gs, *lands, *send, *recv, after)
    return outs[:n], outs[n:]


def _share_halves(ts, *, name):
    n = len(ts)

    def body(*refs):
        outs = refs[n:2 * n]
        send_sems, recv_sems = refs[2 * n:]
        x, y, c, _ = _place()
        sibling = (x, y, 1 - c)
        cps = []
        for i in range(n):
            r2 = ts[i].shape[0] // 2
            mine = outs[i].at[pl.ds(c * r2, r2)]
            cps.append(_remote(mine, mine, send_sems.at[i], recv_sems.at[i], sibling))
            cps[-1].start()
        for i in range(n):
            r2 = ts[i].shape[0] // 2
            got = outs[i].at[pl.ds((1 - c) * r2, r2)]
            _remote(got, got, send_sems.at[i], recv_sems.at[i], sibling).wait_recv()
        for cp in cps:
            cp.wait_send()

    return pl.pallas_call(
        body, name=name,
        out_shape=[jax.ShapeDtypeStruct(t.shape, t.dtype) for t in ts],
        in_specs=[ANY] * n, out_specs=[ANY] * n, input_output_aliases={i: i for i in range(n)},
        scratch_shapes=[pltpu.SemaphoreType.DMA((n,)), pltpu.SemaphoreType.DMA((n,))],
    )(*ts)


def _sum_pieces(land, g, idx, *, name):
    _, r2, C = land.shape
    tr = _tile(r2, max(16, (1 << 21) // C), 16)
    nr = r2 // tr

    def body(idx_ref, land_ref, own_ref, o_ref, acc_ref):
        d = pl.program_id(1)
        mine = d == idx_ref[0]

        @pl.when(d == 0)
        def _():
            acc_ref[...] = jnp.zeros_like(acc_ref)

        @pl.when(mine)
        def _():
            acc_ref[...] += own_ref[...].astype(F32)

        @pl.when(jnp.logical_not(mine))
        def _():
            acc_ref[...] += land_ref[...].astype(F32)

        @pl.when(d == N_DEV - 1)
        def _():
            o_ref[...] = acc_ref[...]

    return pl.pallas_call(
        body, name=name, out_shape=jax.ShapeDtypeStruct((2 * r2, C), F32),
        grid_spec=pltpu.PrefetchScalarGridSpec(
            num_scalar_prefetch=1, grid=(nr, N_DEV),
            in_specs=[pl.BlockSpec((None, tr, C), lambda i, d, ix: (jnp.where(d == ix[0], (d + 1) % N_DEV, d), i, 0)),
                      pl.BlockSpec((None, tr, C), lambda i, d, ix: (ix[1], ix[2] * nr + i, 0))],
            out_specs=pl.BlockSpec((tr, C), lambda i, d, ix: (ix[2] * nr + i, 0)),
            scratch_shapes=[pltpu.VMEM((tr, C), F32)]),
        compiler_params=_params("parallel", "arbitrary"),
    )(idx, land, g)


def _sum_devices(v, *, name):
    n, R, C = v.shape

    def body(v_ref, o_ref):
        acc = v_ref[0]
        for j in range(1, n):
            acc = acc + v_ref[j]
        o_ref[...] = acc

    return pl.pallas_call(body, name=name, out_shape=jax.ShapeDtypeStruct((R, C), F32),
                          in_specs=[VMEM_FULL], out_specs=VMEM_FULL)(v)


def _shard_cols(shards, lo, hi, width):
    out = []
    while lo < hi:
        j = lo // width
        end = min(hi, (j + 1) * width)
        out.append(shards[j][:, lo - j * width:end - j * width])
        lo = end
    return out


def _from_col_shards(g):
    return jnp.transpose(g, (1, 0, 2)).reshape(g.shape[1], N_CHIP * g.shape[2])


def _to_col_shards(w):
    R, N = w.shape
    return jnp.transpose(w.reshape(R, N_CHIP, N // N_CHIP), (1, 0, 2))


def _split_heads(w, widths):
    R, N = w.shape
    per = sum(widths)
    w3 = w.reshape(R, N // per, per)
    lo = w3[:, :, :widths[0]].reshape(R, -1)
    hi = w3[:, :, widths[0]:].reshape(R, -1)
    return jnp.concatenate([lo, hi], axis=1)


def _merge_heads(w, widths):
    R, N = w.shape
    H = N // sum(widths)
    lo = w[:, :H * widths[0]].reshape(R, H, widths[0])
    hi = w[:, H * widths[0]:].reshape(R, H, widths[1])
    return jnp.concatenate([lo, hi], axis=2).reshape(R, N)


def _t5_bucket(dist):
    max_exact = REL_BUCKETS // 2
    n = jnp.maximum(dist, 0)
    large = max_exact + (jnp.log(jnp.maximum(n, 1).astype(F32) / max_exact)
                         / jnp.log(jnp.asarray(REL_MAX_DIST / max_exact, F32))
                         * (REL_BUCKETS - max_exact)).astype(jnp.int32)
    large = jnp.minimum(large, REL_BUCKETS - 1)
    return jnp.where(n < max_exact, n, large)


def _rel_tables():
    a = jnp.arange(SWA_BLOCK)
    b = jnp.arange(2 * SWA_BLOCK)
    dist = SWA_BLOCK + a[:, None] - b[None, :]
    valid = jnp.logical_and(dist >= 0, dist < SWA_BLOCK)
    onehot = jnp.logical_and(_t5_bucket(dist)[..., None] == jnp.arange(REL_BUCKETS), valid[..., None])
    onehot = onehot.astype(F32).reshape(2 * SWA_BLOCK * SWA_BLOCK, REL_BUCKETS)
    negmask = jnp.where(valid, 0.0, NEG).astype(F32).reshape(1, -1)
    return onehot, negmask


def _rope_tables(S):
    pos = jnp.arange(S, dtype=F32)
    inv = ROPE_THETA ** (-jnp.arange(0, MLA_ROPE, 2, dtype=F32) / MLA_ROPE)
    ang = pos[:, None] * inv[None, :]
    ang = jnp.concatenate([ang, ang, ang, ang], axis=-1)
    return jnp.cos(ang), jnp.sin(ang)


def _flat_pad(parts, rows=8):
    flat = jnp.concatenate([p.reshape(1, -1) for p in parts], axis=1)
    n = flat.shape[1]
    width = -(-n // (rows * LANES)) * LANES
    return jnp.pad(flat, ((0, 0), (0, rows * width - n))).reshape(rows, width)


def _unflat(vec, shapes):
    flat = vec.reshape(-1)
    out, off = [], 0
    for s in shapes:
        n = 1
        for d in s:
            n *= d
        out.append(flat[off:off + n].reshape(s))
        off += n
    return out


def kernel(x, c, w_ada, b_ada, g_pre_mix, g_post_mix, w_in, g_q_lat, w_uq, g_kv_lat, w_ukv, rel_bias, sinks, w_o, g_pre_ffn, g_post_ffn, w_up, conv_w, conv_b, w_down, loss_target, m_w_ada, m_b_ada, m_g_pre_mix, m_g_post_mix, m_w_in, m_g_q_lat, m_w_uq, m_g_kv_lat, m_w_ukv, m_rel_bias, m_sinks, m_w_o, m_g_pre_ffn, m_g_post_ffn, m_w_up, m_conv_w, m_conv_b, m_w_down, v_w_ada, v_b_ada, v_g_pre_mix, v_g_post_mix, v_w_in, v_g_q_lat, v_w_uq, v_g_kv_lat, v_w_ukv, v_rel_bias, v_sinks, v_w_o, v_g_pre_ffn, v_g_post_ffn, v_w_up, v_conv_w, v_conv_b, v_w_down):
    S, D = x.shape[1], x.shape[2]
    Rq, Rkv = g_q_lat.shape[1], g_kv_lat.shape[1]
    H = D // MLA_V
    NH = D // SWA_HD
    KW = SWA_KVH * SWA_HD
    F = w_down.shape[1] * N_CHIP
    xi, yi, ci = lax.axis_index("x"), lax.axis_index("y"), lax.axis_index("c")
    chip = 2 * xi + yi
    me = 2 * chip + ci
    x2, tgt = x[0], loss_target[0]

    c_all = _allgather8(jnp.broadcast_to(c, (8, D)), name="gather_c")[:, 0, :]
    n3 = w_ada.shape[2]
    mod_part = _ada_fwd(c_all, w_ada[0], lax.dynamic_slice(b_ada, (0, chip * n3), (1, n3)), name="ada_fwd")
    mod_all = _allgather8(mod_part, name="gather_mod")
    mod_me = lax.dynamic_index_in_dim(mod_all[0::2], me, axis=1, keepdims=False).reshape(1, 6 * D)
    sh1, sc1, gt1, sh2, sc2, gt2 = [mod_me[:, k * D:(k + 1) * D] for k in range(6)]

    swaps = [False, False, False, False, True, False]
    local = [w_in[0].astype(BF16)]
    send_a, recv_a, srcs_a, lands_a, token = _gather_start(local, (mod_all,), swaps[:1], name="gather_start_in")
    rest, token = lax.optimization_barrier(((w_uq[0], w_ukv[0], w_o[0], w_up[0], w_down[0]), token))
    local += [w.astype(BF16) for w in rest]
    send_b, recv_b, srcs_b, lands_b, token = _gather_start(local[1:], (token,), swaps[1:], name="gather_start_rest")
    send1, recv1, srcs, lands = send_a + send_b, recv_a + recv_b, srcs_a + srcs_b, lands_a + lands_b
    onehot, negmask = _rel_tables()
    npb = NH // (2 * SWA_KVH)
    rb_st = jnp.transpose(rel_bias.T.reshape(SWA_KVH, npb, 2, REL_BUCKETS), (0, 2, 1, 3)).reshape(NH, REL_BUCKETS)
    bias_m = (_matmul(rb_st, onehot.T, tie=token, name="rel_bias_table") + negmask).reshape(
        2 * SWA_KVH, npb * SWA_BLOCK, 2 * SWA_BLOCK)
    h = _modnorm_fwd(x2, g_pre_mix, sc1, sh1, name="pre_mix_norm")

    def whole(land, i):
        return lax.dynamic_update_index_in_dim(land, local[i], _slot(chip, swaps[i]), 0)

    def conv_slots(v):
        return jnp.stack([v[0], v[2], v[1], v[3]])

    s2, r2, l_in, _ = _gather_forward(send1[:1], recv1[:1], srcs[:1], lands[:1], h, swaps[:1], name="gather_forward_in")
    (l_in,) = _gather_finish(s2, r2, l_in, h, name="gather_finish_in")
    gin = whole(l_in, 0)
    o_kr = Rq + Rkv
    o_q = o_kr + MLA_ROPE
    o_g = o_q + NH * SWA_HD + 2 * KW
    n_gate, n_swa = 2 * D, o_g - o_q
    n_lat = -(-(o_q + MLA_ROPE) // PAIR) * PAIR
    runs = []
    for tl in range(D // PAIR):
        runs.append((o_g + tl * PAIR, o_g + (tl + 1) * PAIR, 2 * tl * PAIR))
        runs.append((o_g + D + tl * PAIR, o_g + D + (tl + 1) * PAIR, (2 * tl + 1) * PAIR))
    runs.append((o_q, o_g, n_gate))
    runs.append((0, o_q, n_gate + n_swa))
    csh = gin.shape[2]
    parts = []
    for lo, hi, _ in runs + [(o_kr, o_q, 0)]:
        parts += _shard_cols([gin[j] for j in range(N_CHIP)], lo, hi, csh)
    parts.append(jnp.zeros((D, n_lat - o_q - MLA_ROPE), BF16))
    w_in_all = jnp.concatenate(parts, axis=1)
    cos, sin = _rope_tables(S)
    sink_v = sinks.reshape(NH)

    z_lat = _matmul(h, w_in_all, bcols=(n_gate + n_swa, n_lat), name="in_proj_lat")
    z_swa = _matmul(h, w_in_all, bcols=(n_gate, n_swa), name="in_proj_swa")
    zg = _matmul(h, w_in_all, bcols=(0, n_gate), out_dtype=BF16, name="in_proj_gate")
    s2b, r2b, l_b, _ = _gather_forward(send1[1:4], recv1[1:4], srcs[1:4], lands[1:4], zg, swaps[1:4],
                                       name="gather_forward_attn")
    nq, nkv = _lat_norm_fwd(z_lat, g_q_lat, g_kv_lat, name="lat_norm")
    l_uq, l_ukv, l_o = _gather_finish(s2b, r2b, l_b, nq, name="gather_finish_attn")
    wuq = _split_heads(_from_col_shards(whole(l_uq, 1)), (MLA_NOPE, MLA_ROPE))
    wukv = _split_heads(_from_col_shards(whole(l_ukv, 2)), (MLA_NOPE, MLA_V))
    wo = whole(l_o, 3).reshape(D, D)
    q_raw = _matmul(nq, wuq, out_dtype=BF16, name="uq_proj")
    kv_raw = _matmul(nkv, wukv, out_dtype=BF16, name="ukv_proj")
    qp, kp, vv = _mla_pack_fwd(q_raw, kv_raw, z_lat, cos, sin, o_kr, name="mla_pack")
    o_a, lse = _flash_fwd(qp, kp, vv, name="mla_attn")
    s2c, r2c, l_c, tok_c = _gather_forward(send1[4:], recv1[4:], srcs[4:], lands[4:], o_a, swaps[4:],
                                           name="gather_forward_ffn")
    o_b = _swa_fwd(z_swa, bias_m, sink_v, name="swa_attn")
    mixin = _gate_fwd(zg, o_a, o_b, name="gate_mix")
    mix = _matmul(mixin, wo, tie=tok_c, name="o_proj")
    x1, h2 = _resnorm_modnorm_fwd(x2, mix, g_post_mix, gt1, g_pre_ffn, sc2, sh2, name="post_mix_pre_ffn_norm")
    l_up, l_down = _gather_finish(s2c, r2c, l_c, h2, name="gather_finish_ffn")
    cw_all = _allgather8(jnp.pad(conv_w[0], ((0, 5), (0, 0))), tie=l_down, name="gather_conv_w")[0::2, :3]
    cw = conv_slots(cw_all)
    cb = conv_slots(conv_b.reshape(N_CHIP, 1, -1))
    wup = whole(l_up, 4)
    wdown = whole(l_down, 5).reshape(F, D)
    t = _matmul(h2, wup, out_dtype=BF16, shards="out", name="up_proj")
    a = _conv_gate_fwd(t, cw, cb, name="conv_gate")
    yv = _matmul(a, wdown, name="down_proj")
    dout, dy, dg_post_ffn, dgt2, loss_tile = _resnorm_loss(x1, yv, g_post_ffn, gt2, tgt, name="post_ffn_norm_loss")

    big_params = dict(w_in=(w_in, m_w_in, v_w_in), w_uq=(w_uq, m_w_uq, v_w_uq), w_ukv=(w_ukv, m_w_ukv, v_w_ukv),
                      w_o=(w_o, m_w_o, v_w_o), w_up=(w_up, m_w_up, v_w_up), w_down=(w_down, m_w_down, v_w_down))
    res = {}

    def start(nms, gs):
        sw = [nm == "w_up" for nm in nms]
        send, recv, gsrc, glands, tok = _scatter_start(gs, sw, name="grads_start_" + nms[0])
        return (nms, send, recv, gsrc, glands), tok

    def finish(pendings, after):
        nms, send, recv, gsrc, glands = [sum((list(p[k]) for p in pendings), []) for k in range(5)]
        gsrc, glands = _scatter_wait(send, recv, gsrc, glands, after, name="grads_wait_" + nms[0])
        halves = [_sum_pieces(l, g, jnp.stack([me, _slot(chip, nm == "w_up"), ci]).astype(jnp.int32), name="grad_sum_" + nm)
                  for l, g, nm in zip(glands, gsrc, nms)]
        for nm, g in zip(nms, _share_halves(halves, name="grads_share_" + nms[0])):
            w, m, v = big_params[nm]
            res[nm] = (g,) + tuple(_adamw(w[0], g, m[0], v[0], name="adamw_" + nm))

    dw_down = _matmul(a, dy, ta=True, out_dtype=BF16, name="down_proj_dw")
    p_down, tok = start(["w_down"], [dw_down.reshape(N_CHIP, F // N_CHIP, D)])
    da = _matmul(dy, wdown, tb=True, out_dtype=BF16, tie=tok, name="down_proj_dx")
    du, dcw_s, dcb_s = _conv_gate_bwd(t, da, cw, cb, name="conv_gate_bwd")
    dt = _conv_bwd_dt(du, cw, name="conv_bwd_dt")
    dw_up = _matmul(h2, dt, ta=True, out_dtype=BF16, shards="out", name="up_proj_dw")
    p_up, tok = start(["w_up"], [dw_up])
    dh2 = _matmul(dt, wup, tb=True, tie=tok, shards="k", name="up_proj_dx")
    dx1, dg_pre_ffn, dsc2, dsh2, dmix, dg_post_mix, dgt1 = _modnorm_resnorm_bwd(
        dh2, x1, g_pre_ffn, sc2, dout, mix, g_post_mix, gt1, name="pre_ffn_post_mix_norm_bwd")
    dw_o = _matmul(mixin, dmix, ta=True, out_dtype=BF16, name="o_proj_dw")
    p_o, tok = start(["w_o"], [dw_o.reshape(N_CHIP, D // N_CHIP, D)])
    dmixin = _matmul(dmix, wo, tb=True, out_dtype=BF16, tie=tok, name="o_proj_dx")
    do_a, do_b, dzg = _gate_bwd(dmixin, zg, o_a, o_b, name="gate_mix_bwd")
    dqp, dkp, dvv = _flash_bwd(qp, kp, vv, o_a, do_a, lse, name="mla_attn_bwd")
    dq_raw, dkv_raw, dkr = _mla_pack_bwd(dqp, dkp, dvv, cos, sin, name="mla_pack_bwd")
    dw_uq_p = _matmul(nq, dq_raw, ta=True, out_dtype=BF16, name="uq_proj_dw")
    dw_ukv_p = _matmul(nkv, dkv_raw, ta=True, out_dtype=BF16, name="ukv_proj_dw")
    p_qkv, tok = start(["w_uq", "w_ukv"], [_to_col_shards(_merge_heads(dw_uq_p, (MLA_NOPE, MLA_ROPE))),
                                           _to_col_shards(_merge_heads(dw_ukv_p, (MLA_NOPE, MLA_V)))])
    dnq = _matmul(dq_raw, wuq, tb=True, tie=tok, name="uq_proj_dx")
    dnkv = _matmul(dkv_raw, wukv, tb=True, name="ukv_proj_dx")
    dz_lat, dg_q, dg_kv = _lat_norm_bwd(z_lat, dnq, dnkv, dkr, g_q_lat, g_kv_lat, name="lat_norm_bwd")
    dz_swa, dbias, dsink = _swa_bwd(z_swa, bias_m, sink_v, o_b, do_b, name="swa_attn_bwd")
    dz = jnp.concatenate([dzg, dz_swa, dz_lat], axis=1)
    dw_in_p = _matmul(h, dz, ta=True, out_dtype=BF16, name="in_proj_dw")
    dw_shards = []
    for j in range(N_CHIP):
        cols = []
        for lo, hi, at in sorted(runs):
            a0, a1 = max(lo, j * csh), min(hi, (j + 1) * csh)
            if a0 < a1:
                cols.append(dw_in_p[:, at + a0 - lo:at + a1 - lo])
        dw_shards.append(jnp.concatenate(cols, axis=1))
    p_in, tok = start(["w_in"], [jnp.stack(dw_shards)])
    dh = _matmul(dz, w_in_all, tb=True, tie=tok, name="in_proj_dx")
    grad_x, dg_pre_mix, dsc1, dsh1 = _modnorm_bwd(dh, x2, g_pre_mix, sc1, dx1, name="pre_mix_norm_bwd")
    drel_st = _matmul(dbias.reshape(NH, -1), onehot, tie=grad_x, name="rel_bias_bwd")
    finish((p_down, p_up, p_o, p_qkv), drel_st)
    drel = jnp.transpose(drel_st.reshape(SWA_KVH, 2, npb, REL_BUCKETS), (0, 2, 1, 3)).reshape(NH, REL_BUCKETS).T

    dcw = _from_col_shards(conv_slots(dcw_s))
    dcb = conv_slots(dcb_s).reshape(1, -1)
    dmod = jnp.concatenate([dsh1, dsc1, dgt1, dsh2, dsc2, dgt2], axis=1)
    small = [dmod, dg_pre_mix, dg_post_mix, dg_pre_ffn, dg_post_ffn, dg_q, dg_kv, drel, dsink[:, :NH], dcb, dcw]
    shapes = [p.shape for p in small]
    done = [res[nm][1] for nm in ("w_down", "w_up", "w_o", "w_uq", "w_ukv")]
    small_all = _allgather8(_flat_pad(small), tie=done, name="gather_small_grads")
    tot = _unflat(_sum_devices(small_all, name="sum_small_grads"), shapes)
    g_b_ada, g_pre_mix_g, g_post_mix_g, g_pre_ffn_g, g_post_ffn_g, g_q_g, g_kv_g, g_rel, g_sinks, g_cb, g_cw_full = tot
    dmod_all = small_all.reshape(N_DEV, -1)[:, :6 * D]
    g_w_ada = _ada_bwd(c_all.T, lax.dynamic_slice(dmod_all, (0, chip * n3), (N_DEV, n3)), name="ada_bwd")
    ncw = conv_w.shape[2]
    g_cw = lax.dynamic_slice(g_cw_full, (0, chip * ncw), (3, ncw))

    res["w_ada"] = (g_w_ada,) + tuple(_adamw(w_ada[0], g_w_ada, m_w_ada[0], v_w_ada[0], name="adamw_w_ada"))
    finish((p_in,), g_w_ada)
    snames = ["b_ada", "g_pre_mix", "g_post_mix", "g_pre_ffn", "g_post_ffn", "g_q_lat", "g_kv_lat", "rel_bias", "sinks",
              "conv_b", "conv_w"]
    sw = [b_ada, g_pre_mix, g_post_mix, g_pre_ffn, g_post_ffn, g_q_lat, g_kv_lat, rel_bias, sinks, conv_b, conv_w]
    sm = [m_b_ada, m_g_pre_mix, m_g_post_mix, m_g_pre_ffn, m_g_post_ffn, m_g_q_lat, m_g_kv_lat, m_rel_bias, m_sinks,
          m_conv_b, m_conv_w]
    sv = [v_b_ada, v_g_pre_mix, v_g_post_mix, v_g_pre_ffn, v_g_post_ffn, v_g_q_lat, v_g_kv_lat, v_rel_bias, v_sinks,
          v_conv_b, v_conv_w]
    sg = [g_b_ada, g_pre_mix_g, g_post_mix_g, g_pre_ffn_g, g_post_ffn_g, g_q_g, g_kv_g, g_rel, g_sinks, g_cb, g_cw]
    sshapes = [w.shape for w in sw]
    sd, snm, snv = _adamw(_flat_pad(sw), _flat_pad(sg), _flat_pad(sm), _flat_pad(sv), name="adamw_small")
    sd, snm, snv = _unflat(sd, sshapes), _unflat(snm, sshapes), _unflat(snv, sshapes)
    for k, nm in enumerate(snames):
        res[nm] = (sg[k].reshape(sshapes[k]), sd[k], snm[k], snv[k])

    order = ["w_ada", "b_ada", "g_pre_mix", "g_post_mix", "w_in", "g_q_lat", "w_uq", "g_kv_lat", "w_ukv", "rel_bias", "sinks",
             "w_o", "g_pre_ffn", "g_post_ffn", "w_up", "conv_w", "conv_b", "w_down"]
    ref_shapes = dict(w_ada=w_ada.shape, w_in=w_in.shape, w_uq=w_uq.shape, w_ukv=w_ukv.shape, w_o=w_o.shape,
                      w_up=w_up.shape, w_down=w_down.shape)
    outs = []
    for k in range(4):
        for nm in order:
            arr = res[nm][k]
            outs.append(arr.reshape(ref_shapes[nm]) if nm in ref_shapes else arr)
    loss = lax.psum(loss_tile[0, 0], ("x", "y", "c"))
    return (loss, grad_x[None], *outs)
```

```python
import math

import jax
import jax.numpy as jnp
from jax import lax
from jax.experimental import pallas as pl
from jax.experimental.pallas import tpu as pltpu

F32 = jnp.float32
BF16 = jnp.bfloat16
MESH = pl.DeviceIdType.MESH
HIGHEST = lax.Precision.HIGHEST

N_DEV = 8
N_CHIP = 4
LANES = 128
MLA_NOPE = 128
MLA_ROPE = 64
MLA_V = 128
MLA_QK = MLA_NOPE + MLA_ROPE
MLA_QK_PAD = 256
ROPE_THETA = 10000.0
SWA_HD = 64
SWA_KVH = 4
SWA_BLOCK = 128
REL_BUCKETS = 32
REL_MAX_DIST = 128
PAIR = 512
EPS = 1e-6
NEG = -1e30
ADAM_LR = 0.001
ADAM_B1 = 0.9
ADAM_B2 = 0.999
ADAM_EPS = 1e-08
ADAM_WD = 0.01
ADAM_STEP = 10

ANY = pl.BlockSpec(memory_space=pl.ANY)
VMEM_FULL = pl.BlockSpec(memory_space=pltpu.VMEM)
SMEM_FULL = pl.BlockSpec(memory_space=pltpu.SMEM)


def _params(*sem):
    return pltpu.CompilerParams(dimension_semantics=sem if sem else None)


def _tied(body, tie):
    if tie is None:
        return body, [], []
    ties = list(tie) if isinstance(tie, (list, tuple)) else [tie]

    def tied_body(*refs):
        body(*refs[len(ties):])

    return tied_body, [ANY] * len(ties), ties


def _tile(n, pref, unit=LANES):
    best = None
    for t in range(unit, min(n, pref) + 1, unit):
        if n % t == 0:
            best = t
    return n if best is None else best


def _matmul(a, b, *, ta=False, tb=False, out_dtype=F32, tie=None, shards=None, bcols=None, name):
    a2 = a.shape[1:] if shards == "k" else a.shape
    b2 = b.shape[1:] if shards else b.shape
    nsh = b.shape[0] if shards else 1
    K, M = a2 if ta else a2[::-1]
    N, K2 = b2 if tb else b2[::-1]
    assert K == K2, (a.shape, b.shape, ta, tb)
    exact = a.dtype == F32
    col0 = 0
    if bcols is not None:
        assert not tb and shards is None
        col0, N = bcols
    tn = _tile(math.gcd(N, col0) if col0 else N, 2048)
    col0 //= tn
    tk = _tile(K, 2048)
    nkc = K // tk
    nk = nkc * (nsh if shards == "k" else 1)
    tm = M if M < 8 else _tile(M, 1024, LANES if ta else 8)
    dn = (((0 if ta else 1,), (1 if tb else 0,)), ((), ()))
    kax = 3 if shards == "out" else 2

    def product(a_ref, b_ref):
        return lax.dot_general(a_ref[...], b_ref[...], dn, preferred_element_type=F32,
                               precision=HIGHEST if exact else None)

    def body_acc(a_ref, b_ref, o_ref, acc_ref):
        k = pl.program_id(kax)

        @pl.when(k == 0)
        def _():
            acc_ref[...] = product(a_ref, b_ref)

        @pl.when(jnp.logical_and(k > 0, k < nk - 1))
        def _():
            acc_ref[...] += product(a_ref, b_ref)

        @pl.when(k == nk - 1)
        def _():
            o_ref[...] = (acc_ref[...] + product(a_ref, b_ref)).astype(o_ref.dtype)

    def body_one(a_ref, b_ref, o_ref):
        o_ref[...] = product(a_ref, b_ref).astype(o_ref.dtype)

    a_blk, b_blk = ((tk, tm) if ta else (tm, tk)), ((tn, tk) if tb else (tk, tn))
    a_at = (lambda i, k: (k, i)) if ta else (lambda i, k: (i, k))
    b_at = (lambda j, k: (j, k)) if tb else (lambda j, k: (k, j + col0))
    if shards == "out":
        grid = (nsh, M // tm, N // tn, nk)
        a_spec = pl.BlockSpec(a_blk, lambda s, i, j, k: a_at(i, k))
        b_spec = pl.BlockSpec((None,) + b_blk, lambda s, i, j, k: (s,) + b_at(j, k))
        o_spec = pl.BlockSpec((None, tm, tn), lambda s, i, j, k: (s, i, j))
        out_shape = jax.ShapeDtypeStruct((nsh, M, N), out_dtype)
        sem = ("parallel", "parallel", "parallel", "arbitrary")
    elif shards == "k":
        grid = (M // tm, N // tn, nk)
        a_spec = pl.BlockSpec((None,) + a_blk, lambda i, j, k: (k // nkc,) + a_at(i, k % nkc))
        b_spec = pl.BlockSpec((None,) + b_blk, lambda i, j, k: (k // nkc,) + b_at(j, k % nkc))
        o_spec = pl.BlockSpec((tm, tn), lambda i, j, k: (i, j))
        out_shape = jax.ShapeDtypeStruct((M, N), out_dtype)
        sem = ("parallel", "parallel", "arbitrary")
    else:
        grid = (M // tm, N // tn, nk)
        a_spec = pl.BlockSpec(a_blk, lambda i, j, k: a_at(i, k))
        b_spec = pl.BlockSpec(b_blk, lambda i, j, k: b_at(j, k))
        o_spec = pl.BlockSpec((tm, tn), lambda i, j, k: (i, j))
        out_shape = jax.ShapeDtypeStruct((M, N), out_dtype)
        sem = ("parallel", "parallel", "arbitrary")
    body, tspec, targ = _tied(body_one if nk == 1 else body_acc, tie)
    return pl.pallas_call(
        body, name=name, out_shape=out_shape, grid=grid, in_specs=tspec + [a_spec, b_spec], out_specs=o_spec,
        scratch_shapes=[] if nk == 1 else [pltpu.VMEM((tm, tn), F32)],
        compiler_params=_params(*sem),
    )(*targ, a, b)


def _row_tile(S, width):
    return _tile(S, max(8, (1 << 19) // width), 8)


def _rstd(x):
    return lax.rsqrt(jnp.mean(x * x, axis=-1, keepdims=True) + EPS)


def _acc_rows(ref, val, first):
    s = jnp.sum(val, axis=0, keepdims=True)

    @pl.when(first)
    def _():
        ref[...] = s

    @pl.when(jnp.logical_not(first))
    def _():
        ref[...] += s


def _modnorm_fwd(x, g, sc, sh, *, name):
    S, D = x.shape
    tr = _row_tile(S, D)

    def body(x_ref, g_ref, sc_ref, sh_ref, h_ref):
        xv = x_ref[...]
        n = (xv * _rstd(xv)) * g_ref[...]
        h_ref[...] = (n * (1.0 + sc_ref[...]) + sh_ref[...]).astype(BF16)

    row = pl.BlockSpec((tr, D), lambda i: (i, 0))
    vec = pl.BlockSpec((1, D), lambda i: (0, 0))
    return pl.pallas_call(
        body, name=name, out_shape=jax.ShapeDtypeStruct((S, D), BF16), grid=(S // tr,),
        in_specs=[row, vec, vec, vec], out_specs=row, compiler_params=_params("parallel"),
    )(x, g, sc, sh)


def _modnorm_bwd(dh, x, g, sc, dres, *, name):
    S, D = x.shape
    tr = _row_tile(S, D)

    def body(dh_ref, x_ref, g_ref, sc_ref, dres_ref, dx_ref, dg_ref, dsc_ref, dsh_ref):
        first = pl.program_id(0) == 0
        xv = x_ref[...]
        dhv = dh_ref[...]
        gv = g_ref[...]
        r = _rstd(xv)
        xhat = xv * r
        _acc_rows(dsh_ref, dhv, first)
        _acc_rows(dsc_ref, dhv * (xhat * gv), first)
        dn = dhv * (1.0 + sc_ref[...])
        _acc_rows(dg_ref, dn * xhat, first)
        dxhat = dn * gv
        proj = jnp.mean(dxhat * xhat, axis=-1, keepdims=True)
        dx_ref[...] = r * (dxhat - xhat * proj) + dres_ref[...]

    row = pl.BlockSpec((tr, D), lambda i: (i, 0))
    vec = pl.BlockSpec((1, D), lambda i: (0, 0))
    vshape = jax.ShapeDtypeStruct((1, D), F32)
    return pl.pallas_call(
        body, name=name,
        out_shape=(jax.ShapeDtypeStruct((S, D), F32), vshape, vshape, vshape), grid=(S // tr,),
        in_specs=[row, row, vec, vec, row], out_specs=(row, vec, vec, vec),
        compiler_params=_params("arbitrary"),
    )(dh, x, g, sc, dres)


def _resnorm_modnorm_fwd(xres, m, g, gt, g2, sc2, sh2, *, name):
    S, D = xres.shape
    tr = _row_tile(S, D)

    def body(x_ref, m_ref, g_ref, gt_ref, g2_ref, sc_ref, sh_ref, o_ref, h_ref):
        mv = m_ref[...]
        x1 = x_ref[...] + gt_ref[...] * ((mv * _rstd(mv)) * g_ref[...])
        o_ref[...] = x1
        n = (x1 * _rstd(x1)) * g2_ref[...]
        h_ref[...] = (n * (1.0 + sc_ref[...]) + sh_ref[...]).astype(BF16)

    row = pl.BlockSpec((tr, D), lambda i: (i, 0))
    vec = pl.BlockSpec((1, D), lambda i: (0, 0))
    return pl.pallas_call(
        body, name=name, out_shape=(jax.ShapeDtypeStruct((S, D), F32), jax.ShapeDtypeStruct((S, D), BF16)), grid=(S // tr,),
        in_specs=[row, row, vec, vec, vec, vec, vec], out_specs=(row, row), compiler_params=_params("parallel"),
    )(xres, m, g, gt, g2, sc2, sh2)


def _modnorm_resnorm_bwd(dh, x, g, sc, dres, m, g1, gt1, *, name):
    S, D = x.shape
    tr = _row_tile(S, D)

    def body(dh_ref, x_ref, g_ref, sc_ref, dres_ref, m_ref, g1_ref, gt1_ref,
             dx_ref, dg_ref, dsc_ref, dsh_ref, dm_ref, dg1_ref, dgt1_ref):
        first = pl.program_id(0) == 0
        xv = x_ref[...]
        dhv = dh_ref[...]
        gv = g_ref[...]
        r = _rstd(xv)
        xhat = xv * r
        _acc_rows(dsh_ref, dhv, first)
        _acc_rows(dsc_ref, dhv * (xhat * gv), first)
        dn = dhv * (1.0 + sc_ref[...])
        _acc_rows(dg_ref, dn * xhat, first)
        dxhat = dn * gv
        proj = jnp.mean(dxhat * xhat, axis=-1, keepdims=True)
        dx1 = r * (dxhat - xhat * proj) + dres_ref[...]
        dx_ref[...] = dx1
        mv = m_ref[...]
        g1v = g1_ref[...]
        r1 = _rstd(mv)
        mhat = mv * r1
        _acc_rows(dgt1_ref, dx1 * (mhat * g1v), first)
        dn1 = dx1 * gt1_ref[...]
        _acc_rows(dg1_ref, dn1 * mhat, first)
        dmhat = dn1 * g1v
        proj1 = jnp.mean(dmhat * mhat, axis=-1, keepdims=True)
        dm_ref[...] = (r1 * (dmhat - mhat * proj1)).astype(BF16)

    row = pl.BlockSpec((tr, D), lambda i: (i, 0))
    vec = pl.BlockSpec((1, D), lambda i: (0, 0))
    vshape = jax.ShapeDtypeStruct((1, D), F32)
    return pl.pallas_call(
        body, name=name,
        out_shape=(jax.ShapeDtypeStruct((S, D), F32), vshape, vshape, vshape, jax.ShapeDtypeStruct((S, D), BF16), vshape, vshape),
        grid=(S // tr,),
        in_specs=[row, row, vec, vec, row, row, vec, vec], out_specs=(row, vec, vec, vec, row, vec, vec),
        compiler_params=_params("arbitrary"),
    )(dh, x, g, sc, dres, m, g1, gt1)


def _resnorm_loss(xres, m, g, gt, target, *, name):
    S, D = xres.shape
    tr = _row_tile(S, D)

    def body(x_ref, m_ref, g_ref, gt_ref, t_ref, d_ref, dm_ref, dg_ref, dgt_ref, loss_ref):
        first = pl.program_id(0) == 0
        mv = m_ref[...]
        gv = g_ref[...]
        r = _rstd(mv)
        mhat = mv * r
        n = mhat * gv
        err = (x_ref[...] + gt_ref[...] * n) - t_ref[...]
        dv = err * (1.0 / D)
        d_ref[...] = dv
        part = 0.5 * jnp.sum(jnp.mean(err * err, axis=-1, keepdims=True), axis=0, keepdims=True)
        part = jnp.broadcast_to(part, loss_ref.shape)

        @pl.when(first)
        def _():
            loss_ref[...] = part

        @pl.when(jnp.logical_not(first))
        def _():
            loss_ref[...] += part

        _acc_rows(dgt_ref, dv * n, first)
        dn = dv * gt_ref[...]
        _acc_rows(dg_ref, dn * mhat, first)
        dmhat = dn * gv
        proj = jnp.mean(dmhat * mhat, axis=-1, keepdims=True)
        dm_ref[...] = (r * (dmhat - mhat * proj)).astype(BF16)

    row = pl.BlockSpec((tr, D), lambda i: (i, 0))
    vec = pl.BlockSpec((1, D), lambda i: (0, 0))
    vshape = jax.ShapeDtypeStruct((1, D), F32)
    return pl.pallas_call(
        body, name=name,
        out_shape=(jax.ShapeDtypeStruct((S, D), F32), jax.ShapeDtypeStruct((S, D), BF16), vshape, vshape,
                   jax.ShapeDtypeStruct((8, LANES), F32)), grid=(S // tr,),
        in_specs=[row, row, vec, vec, row], out_specs=(row, row, vec, vec, pl.BlockSpec((8, LANES), lambda i: (0, 0))),
        compiler_params=_params("arbitrary"),
    )(xres, m, g, gt, target)


def _lat_norm_fwd(z_lat, g_q, g_kv, *, name):
    S, W = z_lat.shape
    Rq, Rkv = g_q.shape[1], g_kv.shape[1]
    tr = _row_tile(S, W)

    def body(z_ref, gq_ref, gkv_ref, nq_ref, nkv_ref):
        cq = z_ref[:, :Rq]
        ckv = z_ref[:, Rq:Rq + Rkv]
        nq_ref[...] = ((cq * _rstd(cq)) * gq_ref[...]).astype(BF16)
        nkv_ref[...] = ((ckv * _rstd(ckv)) * gkv_ref[...]).astype(BF16)

    return pl.pallas_call(
        body, name=name,
        out_shape=(jax.ShapeDtypeStruct((S, Rq), BF16), jax.ShapeDtypeStruct((S, Rkv), BF16)), grid=(S // tr,),
        in_specs=[pl.BlockSpec((tr, W), lambda i: (i, 0)), pl.BlockSpec((1, Rq), lambda i: (0, 0)),
                  pl.BlockSpec((1, Rkv), lambda i: (0, 0))],
        out_specs=(pl.BlockSpec((tr, Rq), lambda i: (i, 0)), pl.BlockSpec((tr, Rkv), lambda i: (i, 0))),
        compiler_params=_params("parallel"),
    )(z_lat, g_q, g_kv)


def _lat_norm_bwd(z_lat, dnq, dnkv, dkr, g_q, g_kv, *, name):
    S, W = z_lat.shape
    Rq, Rkv = g_q.shape[1], g_kv.shape[1]
    tr = _row_tile(S, W)

    def one(c, dn, gv):
        r = _rstd(c)
        chat = c * r
        dchat = dn * gv
        proj = jnp.mean(dchat * chat, axis=-1, keepdims=True)
        return r * (dchat - chat * proj), dn * chat

    def body(z_ref, dnq_ref, dnkv_ref, dkr_ref, gq_ref, gkv_ref, dz_ref, dgq_ref, dgkv_ref):
        first = pl.program_id(0) == 0
        dcq, pq = one(z_ref[:, :Rq], dnq_ref[...], gq_ref[...])
        dckv, pkv = one(z_ref[:, Rq:Rq + Rkv], dnkv_ref[...], gkv_ref[...])
        _acc_rows(dgq_ref, pq, first)
        _acc_rows(dgkv_ref, pkv, first)
        dz_ref[:, :Rq] = dcq.astype(BF16)
        dz_ref[:, Rq:Rq + Rkv] = dckv.astype(BF16)
        dz_ref[:, Rq + Rkv:Rq + Rkv + LANES] = dkr_ref[...].astype(BF16)
        if W > Rq + Rkv + LANES:
            dz_ref[:, Rq + Rkv + LANES:] = jnp.zeros((tr, W - Rq - Rkv - LANES), BF16)

    return pl.pallas_call(
        body, name=name,
        out_shape=(jax.ShapeDtypeStruct((S, W), BF16), jax.ShapeDtypeStruct((1, Rq), F32),
                   jax.ShapeDtypeStruct((1, Rkv), F32)), grid=(S // tr,),
        in_specs=[pl.BlockSpec((tr, W), lambda i: (i, 0)), pl.BlockSpec((tr, Rq), lambda i: (i, 0)),
                  pl.BlockSpec((tr, Rkv), lambda i: (i, 0)), pl.BlockSpec((tr, LANES), lambda i: (i, 0)),
                  pl.BlockSpec((1, Rq), lambda i: (0, 0)), pl.BlockSpec((1, Rkv), lambda i: (0, 0))],
        out_specs=(pl.BlockSpec((tr, W), lambda i: (i, 0)), pl.BlockSpec((1, Rq), lambda i: (0, 0)),
                   pl.BlockSpec((1, Rkv), lambda i: (0, 0))),
        compiler_params=_params("arbitrary"),
    )(z_lat, dnq, dnkv, dkr, g_q, g_kv)


def _rot(x, lo32):
    a = pltpu.roll(x, 32, 1)
    b = pltpu.roll(x, LANES - 32, 1)
    return jnp.where(lo32, -b, a)


def _rot_t(g, lo32):
    a = pltpu.roll(g, 32, 1)
    b = pltpu.roll(g, LANES - 32, 1)
    return jnp.where(lo32, b, -a)


def _mla_pack_fwd(q_raw, kv_raw, z_lat, cos, sin, kr_off, *, name):
    S = q_raw.shape[0]
    H = kv_raw.shape[1] // (MLA_NOPE + MLA_V)
    assert kr_off % LANES == 0
    scale = MLA_QK ** -0.5
    tr = min(S, 128)
    nope_w = H * MLA_NOPE

    def body(q_ref, kv_ref, z_ref, cos_ref, sin_ref, qp_ref, kp_ref, v_ref):
        lane = lax.broadcasted_iota(jnp.int32, (tr, LANES), 1)
        lo32 = (lane % 64) < 32
        lo64 = lane < 64
        c = cos_ref[...]
        s = sin_ref[...]
        kr = z_ref[...]
        kr = (kr * c + _rot(kr, lo32) * s).astype(BF16)
        for hp in range(H // 2):
            xb = q_ref[:, nope_w + hp * LANES:nope_w + (hp + 1) * LANES].astype(F32)
            rb = (xb * c + _rot(xb, lo32) * s) * scale
            for e in range(2):
                h = 2 * hp + e
                base = h * MLA_QK_PAD
                qp_ref[:, base:base + LANES] = (q_ref[:, h * LANES:(h + 1) * LANES].astype(F32) * scale).astype(BF16)
                keep = lo64 if e == 0 else jnp.logical_not(lo64)
                qp_ref[:, base + LANES:base + 2 * LANES] = jnp.where(keep, rb, 0.0).astype(BF16)
                kp_ref[:, base:base + LANES] = kv_ref[:, h * LANES:(h + 1) * LANES].astype(BF16)
                kp_ref[:, base + LANES:base + 2 * LANES] = kr
        v_ref[...] = kv_ref[:, nope_w:].astype(BF16)

    return pl.pallas_call(
        body, name=name,
        out_shape=(jax.ShapeDtypeStruct((S, H * MLA_QK_PAD), BF16), jax.ShapeDtypeStruct((S, H * MLA_QK_PAD), BF16),
                   jax.ShapeDtypeStruct((S, H * MLA_V), BF16)), grid=(S // tr,),
        in_specs=[pl.BlockSpec((tr, q_raw.shape[1]), lambda i: (i, 0)), pl.BlockSpec((tr, kv_raw.shape[1]), lambda i: (i, 0)),
                  pl.BlockSpec((tr, LANES), lambda i: (i, kr_off // LANES)), pl.BlockSpec((tr, LANES), lambda i: (i, 0)),
                  pl.BlockSpec((tr, LANES), lambda i: (i, 0))],
        out_specs=(pl.BlockSpec((tr, H * MLA_QK_PAD), lambda i: (i, 0)), pl.BlockSpec((tr, H * MLA_QK_PAD), lambda i: (i, 0)),
                   pl.BlockSpec((tr, H * MLA_V), lambda i: (i, 0))),
        compiler_params=_params("parallel"),
    )(q_raw, kv_raw, z_lat, cos, sin)


def _mla_pack_bwd(dqp, dkp, dv, cos, sin, *, name):
    S = dqp.shape[0]
    H = dv.shape[1] // MLA_V
    scale = MLA_QK ** -0.5
    tr = min(S, 128)
    nope_w = H * MLA_NOPE

    def body(dqp_ref, dkp_ref, dv_ref, cos_ref, sin_ref, dq_ref, dkv_ref, dkr_ref):
        lane = lax.broadcasted_iota(jnp.int32, (tr, LANES), 1)
        lo32 = (lane % 64) < 32
        lo64 = lane < 64
        c = cos_ref[...]
        s = sin_ref[...]
        dkr2 = jnp.zeros((tr, LANES), F32)
        for hp in range(H // 2):
            be = (2 * hp) * MLA_QK_PAD
            bo = (2 * hp + 1) * MLA_QK_PAD
            g = jnp.where(lo64, dqp_ref[:, be + LANES:be + 2 * LANES].astype(F32),
                          dqp_ref[:, bo + LANES:bo + 2 * LANES].astype(F32)) * scale
            dq_ref[:, nope_w + hp * LANES:nope_w + (hp + 1) * LANES] = (g * c + _rot_t(g * s, lo32)).astype(BF16)
            for h, base in ((2 * hp, be), (2 * hp + 1, bo)):
                dq_ref[:, h * LANES:(h + 1) * LANES] = (dqp_ref[:, base:base + LANES].astype(F32) * scale).astype(BF16)
                dkv_ref[:, h * LANES:(h + 1) * LANES] = dkp_ref[:, base:base + LANES].astype(BF16)
                dkr2 = dkr2 + dkp_ref[:, base + LANES:base + 2 * LANES].astype(F32)
        dkr2 = dkr2 * c + _rot_t(dkr2 * s, lo32)
        dkr2 = dkr2 + pltpu.roll(dkr2, 64, 1)
        dkr_ref[...] = jnp.where(lo64, dkr2, 0.0)
        dkv_ref[:, nope_w:] = dv_ref[...].astype(BF16)

    return pl.pallas_call(
        body, name=name,
        out_shape=(jax.ShapeDtypeStruct((S, nope_w + H * MLA_ROPE), BF16), jax.ShapeDtypeStruct((S, 2 * nope_w), BF16),
                   jax.ShapeDtypeStruct((S, LANES), F32)), grid=(S // tr,),
        in_specs=[pl.BlockSpec((tr, H * MLA_QK_PAD), lambda i: (i, 0)), pl.BlockSpec((tr, H * MLA_QK_PAD), lambda i: (i, 0)),
                  pl.BlockSpec((tr, H * MLA_V), lambda i: (i, 0)), pl.BlockSpec((tr, LANES), lambda i: (i, 0)),
                  pl.BlockSpec((tr, LANES), lambda i: (i, 0))],
        out_specs=(pl.BlockSpec((tr, nope_w + H * MLA_ROPE), lambda i: (i, 0)), pl.BlockSpec((tr, 2 * nope_w), lambda i: (i, 0)),
                   pl.BlockSpec((tr, LANES), lambda i: (i, 0))),
        compiler_params=_params("parallel"),
    )(dqp, dkp, dv, cos, sin)


FLASH_HB_FWD = 8
FLASH_HB_BWD = 4


def _causal_pairs(nb):
    qi = [i for i in range(nb) for j in range(i + 1)]
    kj = [j for i in range(nb) for j in range(i + 1)]
    return jnp.asarray(qi, jnp.int32), jnp.asarray(kj, jnp.int32)


def _scores(q, k, diagonal, t):
    s = lax.dot_general(q, k, (((1,), (1,)), ((), ())), preferred_element_type=F32)
    if diagonal:
        row = lax.broadcasted_iota(jnp.int32, (t, t), 0)
        col = lax.broadcasted_iota(jnp.int32, (t, t), 1)
        s = jnp.where(col <= row, s, NEG)
    return s


def _flash_fwd(qp, kp, v, *, name):
    S = qp.shape[0]
    H = v.shape[1] // MLA_V
    t = min(S, 512)
    nb = S // t
    HB = min(FLASH_HB_FWD, H)
    qi, kj = _causal_pairs(nb)
    QW, VW = MLA_QK_PAD, MLA_V

    def body(qi_ref, kj_ref, q_ref, k_ref, v_ref, o_ref, lse_ref, m_s, l_s, acc_s):
        pr = pl.program_id(1)
        i = qi_ref[pr]
        j = kj_ref[pr]

        @pl.when(j == 0)
        def _():
            m_s[...] = jnp.full_like(m_s, NEG)
            l_s[...] = jnp.zeros_like(l_s)
            acc_s[...] = jnp.zeros_like(acc_s)

        def step(diagonal):
            state = [(m_s[hh], l_s[hh], acc_s[hh]) for hh in range(HB)]
            new = []
            for hh, (m_prev, l_prev, acc_prev) in enumerate(state):
                s = _scores(q_ref[:, hh * QW:(hh + 1) * QW], k_ref[:, hh * QW:(hh + 1) * QW], diagonal, t)
                m_cur = jnp.maximum(m_prev, jnp.max(s, axis=1, keepdims=True))
                alpha = jnp.exp(m_prev - m_cur)
                p = jnp.exp(s - m_cur[:, :1])
                l_new = alpha * l_prev + jnp.sum(p, axis=1, keepdims=True)
                acc = alpha * acc_prev + jnp.dot(p.astype(BF16), v_ref[:, hh * VW:(hh + 1) * VW], preferred_element_type=F32)
                new.append((m_cur, l_new, acc))
            for hh, (m_cur, l_new, acc) in enumerate(new):
                if diagonal:
                    o_ref[:, hh * VW:(hh + 1) * VW] = acc / l_new
                    lse_ref[hh] = m_cur + jnp.log(l_new)
                else:
                    l_s[hh] = l_new
                    acc_s[hh] = acc
                    m_s[hh] = m_cur

        @pl.when(i != j)
        def _():
            step(False)

        @pl.when(i == j)
        def _():
            step(True)

    return pl.pallas_call(
        body, name=name,
        out_shape=(jax.ShapeDtypeStruct((S, H * VW), F32), jax.ShapeDtypeStruct((H, S, LANES), F32)),
        grid_spec=pltpu.PrefetchScalarGridSpec(
            num_scalar_prefetch=2, grid=(H // HB, qi.shape[0]),
            in_specs=[pl.BlockSpec((t, HB * QW), lambda g, p, qi, kj: (qi[p], g)),
                      pl.BlockSpec((t, HB * QW), lambda g, p, qi, kj: (kj[p], g)),
                      pl.BlockSpec((t, HB * VW), lambda g, p, qi, kj: (kj[p], g))],
            out_specs=(pl.BlockSpec((t, HB * VW), lambda g, p, qi, kj: (qi[p], g)),
                       pl.BlockSpec((HB, t, LANES), lambda g, p, qi, kj: (g, qi[p], 0))),
            scratch_shapes=[pltpu.VMEM((HB, t, LANES), F32), pltpu.VMEM((HB, t, LANES), F32), pltpu.VMEM((HB, t, VW), F32)]),
        compiler_params=_params("parallel", "arbitrary"),
    )(qi, kj, qp, kp, v)


def _flash_bwd(qp, kp, v, o, do, lse, *, name):
    S = qp.shape[0]
    H = v.shape[1] // MLA_V
    t = min(S, 512)
    nb = S // t
    HB = min(FLASH_HB_BWD, H)
    qi = jnp.asarray([i for j in range(nb) for i in range(j, nb)], jnp.int32)
    kj = jnp.asarray([j for j in range(nb) for i in range(j, nb)], jnp.int32)
    npairs = qi.shape[0]
    QW, VW = MLA_QK_PAD, MLA_V
    tn = (((0,), (0,)), ((), ()))
    nt = (((1,), (1,)), ((), ()))

    def body(qi_ref, kj_ref, q_ref, k_ref, v_ref, o_ref, do_ref, lse_ref, dq_ref, dk_ref, dv_ref, dq_s, dk_s, dv_s):
        pr = pl.program_id(1)
        i = qi_ref[pr]
        j = kj_ref[pr]
        rows = pl.ds(pl.multiple_of(i * t, t), t)

        @pl.when(pr == 0)
        def _():
            dq_s[...] = jnp.zeros_like(dq_s)

        @pl.when(i == j)
        def _():
            dk_s[...] = jnp.zeros_like(dk_s)
            dv_s[...] = jnp.zeros_like(dv_s)

        def step(diagonal):
            for hh in range(HB):
                q = q_ref[:, hh * QW:(hh + 1) * QW]
                k = k_ref[:, hh * QW:(hh + 1) * QW]
                dob = do_ref[:, hh * VW:(hh + 1) * VW]
                p = jnp.exp(_scores(q, k, diagonal, t) - lse_ref[hh][:, :1])
                delta = jnp.sum(dob.astype(F32) * o_ref[:, hh * VW:(hh + 1) * VW], axis=1, keepdims=True)
                dp = lax.dot_general(dob, v_ref[:, hh * VW:(hh + 1) * VW], nt, preferred_element_type=F32)
                dsb = (p * (dp - delta)).astype(BF16)
                dv_s[:, hh * VW:(hh + 1) * VW] += lax.dot_general(p.astype(BF16), dob, tn, preferred_element_type=F32)
                dk_s[:, hh * QW:(hh + 1) * QW] += lax.dot_general(dsb, q, tn, preferred_element_type=F32)
                dq_s[rows, hh * QW:(hh + 1) * QW] += jnp.dot(dsb, k, preferred_element_type=F32)

        @pl.when(i != j)
        def _():
            step(False)

        @pl.when(i == j)
        def _():
            step(True)

        @pl.when(i == nb - 1)
        def _():
            dk_ref[...] = dk_s[...].astype(BF16)
            dv_ref[...] = dv_s[...].astype(BF16)

        @pl.when(pr == npairs - 1)
        def _():
            dq_ref[...] = dq_s[...].astype(BF16)

    qside = lambda g, p, qi, kj: (qi[p], g)
    kside = lambda g, p, qi, kj: (kj[p], g)
    whole = lambda g, p, qi, kj: (0, g)
    return pl.pallas_call(
        body, name=name,
        out_shape=(jax.ShapeDtypeStruct((S, H * QW), BF16), jax.ShapeDtypeStruct((S, H * QW), BF16),
                   jax.ShapeDtypeStruct((S, H * VW), BF16)),
        grid_spec=pltpu.PrefetchScalarGridSpec(
            num_scalar_prefetch=2, grid=(H // HB, npairs),
            in_specs=[pl.BlockSpec((t, HB * QW), qside), pl.BlockSpec((t, HB * QW), kside), pl.BlockSpec((t, HB * VW), kside),
                      pl.BlockSpec((t, HB * VW), qside), pl.BlockSpec((t, HB * VW), qside),
                      pl.BlockSpec((HB, t, LANES), lambda g, p, qi, kj: (g, qi[p], 0))],
            out_specs=(pl.BlockSpec((S, HB * QW), whole), pl.BlockSpec((t, HB * QW), kside), pl.BlockSpec((t, HB * VW), kside)),
            scratch_shapes=[pltpu.VMEM((S, HB * QW), F32), pltpu.VMEM((t, HB * QW), F32), pltpu.VMEM((t, HB * VW), F32)]),
        compiler_params=_params("parallel", "arbitrary"),
    )(qi, kj, qp, kp, v, o, do, lse)


def _swa_kv_halves(blk, hf, lo):
    if hf == 0:
        a = jnp.where(lo, blk, 0.0)
        b = pltpu.roll(a, 64, 1)
    else:
        b = jnp.where(lo, 0.0, blk)
        a = pltpu.roll(b, 64, 1)
    return a.astype(BF16), b.astype(BF16)


def _swa_softmax(qs, kx, bias, neg0, sk):
    s = lax.dot_general(qs, kx, (((1,), (1,)), ((), ())), preferred_element_type=F32) + bias + neg0
    m = jnp.maximum(jnp.max(s, axis=1, keepdims=True), sk)
    e = jnp.exp(s - m)
    es = jnp.exp(sk - m)
    inv = 1.0 / (jnp.sum(e, axis=1, keepdims=True) + es)
    return e * inv, es * inv


def _swa_stack(ref, kvh, npb, scale=None):
    parts = [ref[:, (kvh * npb + pb) * LANES:(kvh * npb + pb + 1) * LANES] for pb in range(npb)]
    x = jnp.concatenate(parts, axis=0)
    return x if scale is None else x * scale


def _swa_sink_col(sink_ref, kvh, e, npb):
    row = lax.broadcasted_iota(jnp.int32, (npb * SWA_BLOCK, 1), 0)
    col = jnp.zeros((npb * SWA_BLOCK, 1), F32) + sink_ref[2 * (kvh * npb) + e]
    for pb in range(1, npb):
        col = jnp.where(row >= pb * SWA_BLOCK, sink_ref[2 * (kvh * npb + pb) + e], col)
    return col


def _swa_fwd(z_swa, bias_st, sinks, *, name):
    S, W = z_swa.shape
    npb = bias_st.shape[1] // SWA_BLOCK
    NH = 2 * SWA_KVH * npb
    QW = NH * SWA_HD
    KW = SWA_KVH * SWA_HD
    nb = S // SWA_BLOCK
    B = SWA_BLOCK
    assert SWA_KVH % 2 == 0 and W == QW + 2 * KW

    def body(sink_ref, q_ref, kvc_ref, kvp_ref, b_ref, o_ref):
        n = pl.program_id(0)
        lo = lax.broadcasted_iota(jnp.int32, (2 * B, LANES), 1) < 64
        col = lax.broadcasted_iota(jnp.int32, (npb * B, 2 * B), 1)
        neg0 = jnp.where(jnp.logical_and(col < B, n == 0), NEG, 0.0)
        for kb in range(SWA_KVH // 2):
            kblk = jnp.concatenate([kvp_ref[:, kb * LANES:(kb + 1) * LANES], kvc_ref[:, kb * LANES:(kb + 1) * LANES]], axis=0)
            vblk = jnp.concatenate([kvp_ref[:, KW + kb * LANES:KW + (kb + 1) * LANES],
                                    kvc_ref[:, KW + kb * LANES:KW + (kb + 1) * LANES]], axis=0)
            for hf in range(2):
                kvh = 2 * kb + hf
                ks = _swa_kv_halves(kblk, hf, lo)
                vs = _swa_kv_halves(vblk, hf, lo)
                qs = _swa_stack(q_ref, kvh, npb, SWA_HD ** -0.5).astype(BF16)
                acc = jnp.zeros((npb * B, LANES), F32)
                for e in range(2):
                    p, _ = _swa_softmax(qs, ks[e], b_ref[2 * kvh + e], neg0, _swa_sink_col(sink_ref, kvh, e, npb))
                    acc = acc + jnp.dot(p.astype(BF16), vs[e], preferred_element_type=F32)
                for pb in range(npb):
                    P = kvh * npb + pb
                    o_ref[:, P * LANES:(P + 1) * LANES] = acc[pb * B:(pb + 1) * B]

    kvcol = QW // (2 * KW)
    assert QW % (2 * KW) == 0
    return pl.pallas_call(
        body, name=name,
        out_shape=jax.ShapeDtypeStruct((S, QW), F32), grid=(nb,),
        in_specs=[SMEM_FULL, pl.BlockSpec((B, QW), lambda n: (n, 0)), pl.BlockSpec((B, 2 * KW), lambda n: (n, kvcol)),
                  pl.BlockSpec((B, 2 * KW), lambda n: (jnp.maximum(n - 1, 0), kvcol)),
                  pl.BlockSpec(bias_st.shape, lambda n: (0, 0, 0))],
        out_specs=pl.BlockSpec((B, QW), lambda n: (n, 0)),
        compiler_params=_params("parallel"),
    )(sinks, z_swa, z_swa, z_swa, bias_st)


def _swa_bwd(z_swa, bias_st, sinks, o, do, *, name):
    S, W = z_swa.shape
    npb = bias_st.shape[1] // SWA_BLOCK
    NH = 2 * SWA_KVH * npb
    QW = NH * SWA_HD
    KW = SWA_KVH * SWA_HD
    nb = S // SWA_BLOCK
    B = SWA_BLOCK
    scale = SWA_HD ** -0.5
    tn = (((0,), (0,)), ((), ()))
    nt = (((1,), (1,)), ((), ()))

    def fold(x, hf, lo):
        x = x + pltpu.roll(x, 64, 1)
        return jnp.where(lo, x, 0.0) if hf == 0 else jnp.where(lo, 0.0, x)

    def body(sink_ref, q_ref, kvc_ref, kvp_ref, b_ref, o_ref, do_ref, dz_ref, dbias_ref, dsink_ref,
             cq_s, ck_s, cv_s, nq_s, nk_s, nv_s, pk_s, pv_s):
        n = pl.program_id(0)

        @pl.when(n == 0)
        def _():
            dbias_ref[...] = jnp.zeros_like(dbias_ref)
            dsink_ref[...] = jnp.zeros_like(dsink_ref)
            cq_s[...] = jnp.zeros_like(cq_s)
            ck_s[...] = jnp.zeros_like(ck_s)
            cv_s[...] = jnp.zeros_like(cv_s)

        @pl.when(n == nb)
        def _():
            pk_s[...] = jnp.zeros_like(pk_s)
            pv_s[...] = jnp.zeros_like(pv_s)

        @pl.when(n < nb)
        def _():
            lo = lax.broadcasted_iota(jnp.int32, (2 * B, LANES), 1) < 64
            lo1 = lax.broadcasted_iota(jnp.int32, (npb * B, LANES), 1) < 64
            lane1 = lax.broadcasted_iota(jnp.int32, (1, LANES), 1)
            col = lax.broadcasted_iota(jnp.int32, (npb * B, 2 * B), 1)
            neg0 = jnp.where(jnp.logical_and(col < B, n == 0), NEG, 0.0)
            dsink = jnp.zeros((1, LANES), F32)
            for kb in range(SWA_KVH // 2):
                kblk = jnp.concatenate([kvp_ref[:, kb * LANES:(kb + 1) * LANES], kvc_ref[:, kb * LANES:(kb + 1) * LANES]], axis=0)
                vblk = jnp.concatenate([kvp_ref[:, KW + kb * LANES:KW + (kb + 1) * LANES],
                                        kvc_ref[:, KW + kb * LANES:KW + (kb + 1) * LANES]], axis=0)
                dkblk = jnp.zeros((2 * B, LANES), F32)
                dvblk = jnp.zeros((2 * B, LANES), F32)
                for hf in range(2):
                    kvh = 2 * kb + hf
                    ks = _swa_kv_halves(kblk, hf, lo)
                    vs = _swa_kv_halves(vblk, hf, lo)
                    qs = _swa_stack(q_ref, kvh, npb, scale).astype(BF16)
                    dob = _swa_stack(do_ref, kvh, npb)
                    prod = dob.astype(F32) * _swa_stack(o_ref, kvh, npb)
                    dkj = jnp.zeros((2 * B, LANES), F32)
                    dvj = jnp.zeros((2 * B, LANES), F32)
                    dqs = jnp.zeros((npb * B, LANES), F32)
                    for e in range(2):
                        keep = lo1 if e == 0 else jnp.logical_not(lo1)
                        p, psink = _swa_softmax(qs, ks[e], b_ref[2 * kvh + e], neg0, _swa_sink_col(sink_ref, kvh, e, npb))
                        delta = jnp.sum(jnp.where(keep, prod, 0.0), axis=1, keepdims=True)
                        dp = lax.dot_general(dob, vs[e], nt, preferred_element_type=F32)
                        ds = p * (dp - delta)
                        dbias_ref[2 * kvh + e] += ds
                        pd = psink * delta
                        for pb in range(npb):
                            dsh = -jnp.sum(pd[pb * B:(pb + 1) * B], axis=0, keepdims=True)
                            dsink = dsink + jnp.where(lane1 == 2 * (kvh * npb + pb) + e, dsh, 0.0)
                        dsb = ds.astype(BF16)
                        dqs = dqs + jnp.dot(dsb, ks[e], preferred_element_type=F32)
                        keep2 = lo if e == 0 else jnp.logical_not(lo)
                        dkj = dkj + jnp.where(keep2, lax.dot_general(dsb, qs, tn, preferred_element_type=F32), 0.0)
                        dvj = dvj + jnp.where(keep2, lax.dot_general(p.astype(BF16), dob, tn, preferred_element_type=F32), 0.0)
                    for pb in range(npb):
                        P = kvh * npb + pb
                        nq_s[:, P * LANES:(P + 1) * LANES] = dqs[pb * B:(pb + 1) * B] * scale
                    dkblk = dkblk + fold(dkj, hf, lo)
                    dvblk = dvblk + fold(dvj, hf, lo)
                pk_s[:, kb * LANES:(kb + 1) * LANES] = dkblk[:B]
                nk_s[:, kb * LANES:(kb + 1) * LANES] = dkblk[B:]
                pv_s[:, kb * LANES:(kb + 1) * LANES] = dvblk[:B]
                nv_s[:, kb * LANES:(kb + 1) * LANES] = dvblk[B:]
            dsink_ref[...] += dsink

        dz_ref[:, :QW] = cq_s[...].astype(BF16)
        dz_ref[:, QW:QW + KW] = (ck_s[...] + pk_s[...]).astype(BF16)
        dz_ref[:, QW + KW:] = (cv_s[...] + pv_s[...]).astype(BF16)

        @pl.when(n < nb)
        def _():
            cq_s[...] = nq_s[...]
            ck_s[...] = nk_s[...]
            cv_s[...] = nv_s[...]

    kvcol = QW // (2 * KW)
    cur = lambda n: (jnp.minimum(n, nb - 1), 0)
    return pl.pallas_call(
        body, name=name,
        out_shape=(jax.ShapeDtypeStruct((S, W), BF16), jax.ShapeDtypeStruct(bias_st.shape, F32),
                   jax.ShapeDtypeStruct((1, LANES), F32)),
        grid=(nb + 1,),
        in_specs=[SMEM_FULL, pl.BlockSpec((B, QW), cur), pl.BlockSpec((B, 2 * KW), lambda n: (jnp.minimum(n, nb - 1), kvcol)),
                  pl.BlockSpec((B, 2 * KW), lambda n: (jnp.maximum(jnp.minimum(n, nb - 1) - 1, 0), kvcol)),
                  pl.BlockSpec(bias_st.shape, lambda n: (0, 0, 0)), pl.BlockSpec((B, QW), cur), pl.BlockSpec((B, QW), cur)],
        out_specs=(pl.BlockSpec((B, W), lambda n: (jnp.maximum(n - 1, 0), 0)),
                   pl.BlockSpec(bias_st.shape, lambda n: (0, 0, 0)), pl.BlockSpec((1, LANES), lambda n: (0, 0))),
        scratch_shapes=[pltpu.VMEM((B, QW), F32), pltpu.VMEM((B, KW), F32), pltpu.VMEM((B, KW), F32),
                        pltpu.VMEM((B, QW), F32), pltpu.VMEM((B, KW), F32), pltpu.VMEM((B, KW), F32),
                        pltpu.VMEM((B, KW), F32), pltpu.VMEM((B, KW), F32)],
        compiler_params=_params("arbitrary"),
    )(sinks, z_swa, z_swa, z_swa, bias_st, o, do)


def _gate_fwd(zg, o_a, o_b, *, name):
    S, D = o_a.shape
    tr = min(S, 512)

    def body(z_ref, a_ref, b_ref, m_ref):
        ga = jax.nn.sigmoid(z_ref[:, :PAIR].astype(F32))
        gb = jax.nn.sigmoid(z_ref[:, PAIR:].astype(F32))
        m_ref[...] = (ga * a_ref[...] + gb * b_ref[...]).astype(BF16)

    col = pl.BlockSpec((tr, PAIR), lambda i, j: (i, j))
    return pl.pallas_call(
        body, name=name, out_shape=jax.ShapeDtypeStruct((S, D), BF16), grid=(S // tr, D // PAIR),
        in_specs=[pl.BlockSpec((tr, 2 * PAIR), lambda i, j: (i, j)), col, col], out_specs=col,
        compiler_params=_params("parallel", "parallel"),
    )(zg, o_a, o_b)


def _gate_bwd(dmix, zg, o_a, o_b, *, name):
    S, D = o_a.shape
    tr = min(S, 512)

    def body(d_ref, z_ref, a_ref, b_ref, da_ref, db_ref, dz_ref):
        d = d_ref[...].astype(F32)
        ga = jax.nn.sigmoid(z_ref[:, :PAIR].astype(F32))
        gb = jax.nn.sigmoid(z_ref[:, PAIR:].astype(F32))
        da_ref[...] = (d * ga).astype(BF16)
        db_ref[...] = (d * gb).astype(BF16)
        dz_ref[:, :PAIR] = (d * a_ref[...] * (ga * (1.0 - ga))).astype(BF16)
        dz_ref[:, PAIR:] = (d * b_ref[...] * (gb * (1.0 - gb))).astype(BF16)

    col = pl.BlockSpec((tr, PAIR), lambda i, j: (i, j))
    wide = pl.BlockSpec((tr, 2 * PAIR), lambda i, j: (i, j))
    return pl.pallas_call(
        body, name=name,
        out_shape=(jax.ShapeDtypeStruct((S, D), BF16), jax.ShapeDtypeStruct((S, D), BF16), jax.ShapeDtypeStruct((S, 2 * D), BF16)),
        grid=(S // tr, D // PAIR), in_specs=[col, wide, col, col], out_specs=(col, col, wide),
        compiler_params=_params("parallel", "parallel"),
    )(dmix, zg, o_a, o_b)


def _conv_u(t_ref, prev_ref, w_ref, b_ref, m, i):
    cur = t_ref[m].astype(F32)
    live = (i > 0).astype(F32)
    p6 = prev_ref[m, 14:15, :].astype(F32) * live
    p7 = prev_ref[m, 15:16, :].astype(F32) * live
    row = lax.broadcasted_iota(jnp.int32, cur.shape, 0)
    t1 = jnp.where(row == 0, p7, pltpu.roll(cur, 1, 0))
    t2 = jnp.where(row == 0, p6, jnp.where(row == 1, p7, pltpu.roll(cur, 2, 0)))
    u = ((b_ref[m] + w_ref[m, 0:1, :] * t2) + w_ref[m, 1:2, :] * t1) + w_ref[m, 2:3, :] * cur
    return u, cur, t1, t2


def _conv_specs(tr, tc):
    blk = pl.BlockSpec((2, tr, tc), lambda p, j, i: (p, i, j))
    prev = pl.BlockSpec((2, 16, tc), lambda p, j, i: (p, jnp.maximum(i * (tr // 16) - 1, 0), j))
    w3 = pl.BlockSpec((2, 3, tc), lambda p, j, i: (p, 0, j))
    w1 = pl.BlockSpec((2, 1, tc), lambda p, j, i: (p, 0, j))
    return blk, prev, w3, w1


def _conv_gate_fwd(t, cw, cb, *, name):
    _, S, C = t.shape
    tr, tc = min(S, 512), _tile(C, 1536)
    ncol = C // tc
    blk, prev, w3, w1 = _conv_specs(tr, tc)

    def body(t_ref, prev_ref, w_ref, b_ref, a_ref):
        i = pl.program_id(2)
        u1 = _conv_u(t_ref, prev_ref, w_ref, b_ref, 0, i)[0]
        u2 = _conv_u(t_ref, prev_ref, w_ref, b_ref, 1, i)[0]
        a_ref[...] = (jax.nn.silu(u1) * u2).astype(BF16)

    return pl.pallas_call(
        body, name=name, out_shape=jax.ShapeDtypeStruct((S, 2 * C), BF16), grid=(2, ncol, S // tr),
        in_specs=[blk, prev, w3, w1], out_specs=pl.BlockSpec((tr, tc), lambda p, j, i: (i, p * ncol + j)),
        compiler_params=_params("parallel", "parallel", "parallel"),
    )(t, t, cw, cb)


def _conv_gate_bwd(t, da, cw, cb, *, name):
    _, S, C = t.shape
    tr, tc = min(S, 256), _tile(C, 1536)
    ncol = C // tc
    blk, prev, w3, w1 = _conv_specs(tr, tc)

    def body(t_ref, prev_ref, da_ref, w_ref, b_ref, du_ref, dw_ref, db_ref):
        i = pl.program_id(2)
        first = i == 0
        u1, c1, a1, b1 = _conv_u(t_ref, prev_ref, w_ref, b_ref, 0, i)
        u2, c2, a2, b2 = _conv_u(t_ref, prev_ref, w_ref, b_ref, 1, i)
        d = da_ref[...].astype(F32)
        sg = jax.nn.sigmoid(u1)
        du1 = d * u2 * (sg * (1.0 + u1 * (1.0 - sg)))
        du2 = d * (u1 * sg)
        for m, (du, cur, t1, t2) in enumerate(((du1, c1, a1, b1), (du2, c2, a2, b2))):
            du_ref[m] = du.astype(BF16)
            dw = jnp.concatenate([jnp.sum(du * t2, axis=0, keepdims=True), jnp.sum(du * t1, axis=0, keepdims=True),
                                  jnp.sum(du * cur, axis=0, keepdims=True)], axis=0)
            db = jnp.sum(du, axis=0, keepdims=True)

            @pl.when(first)
            def _():
                dw_ref[m] = dw
                db_ref[m] = db

            @pl.when(jnp.logical_not(first))
            def _():
                dw_ref[m] += dw
                db_ref[m] += db

    return pl.pallas_call(
        body, name=name,
        out_shape=(jax.ShapeDtypeStruct(t.shape, BF16), jax.ShapeDtypeStruct(cw.shape, F32), jax.ShapeDtypeStruct(cb.shape, F32)),
        grid=(2, ncol, S // tr),
        in_specs=[blk, prev, pl.BlockSpec((tr, tc), lambda p, j, i: (i, p * ncol + j)), w3, w1], out_specs=(blk, w3, w1),
        compiler_params=_params("parallel", "parallel", "arbitrary"),
    )(t, t, da, cw, cb)


def _conv_bwd_dt(du, cw, *, name):
    _, S, C = du.shape
    tr, tc = min(S, 512), _tile(C, 1536)
    nrow = S // tr
    blk, _, w3, _ = _conv_specs(tr, tc)
    nxt = pl.BlockSpec((2, 16, tc), lambda p, j, i: (p, jnp.minimum((i + 1) * (tr // 16), S // 16 - 1), j))

    def body(d_ref, next_ref, w_ref, dt_ref):
        i = pl.program_id(2)
        live = (i < nrow - 1).astype(F32)
        for m in range(2):
            cur = d_ref[m].astype(F32)
            n0 = next_ref[m, 0:1, :].astype(F32) * live
            n1 = next_ref[m, 1:2, :].astype(F32) * live
            row = lax.broadcasted_iota(jnp.int32, cur.shape, 0)
            d1 = jnp.where(row == tr - 1, n0, pltpu.roll(cur, tr - 1, 0))
            d2 = jnp.where(row == tr - 1, n1, jnp.where(row == tr - 2, n0, pltpu.roll(cur, tr - 2, 0)))
            dt_ref[m] = ((w_ref[m, 2:3, :] * cur + w_ref[m, 1:2, :] * d1) + w_ref[m, 0:1, :] * d2).astype(BF16)

    return pl.pallas_call(
        body, name=name, out_shape=jax.ShapeDtypeStruct(du.shape, BF16), grid=(2, C // tc, nrow),
        in_specs=[blk, nxt, w3], out_specs=blk, compiler_params=_params("parallel", "parallel", "parallel"),
    )(du, du, cw)


def _ada_fwd(c_all, w, b, *, name):
    Bn, D = c_all.shape
    N = w.shape[1]
    tn = _tile(N, 512)

    def body(c_ref, w_ref, b_ref, o_ref):
        o_ref[...] = jnp.dot(jax.nn.silu(c_ref[...]), w_ref[...], preferred_element_type=F32, precision=HIGHEST) + b_ref[...]

    return pl.pallas_call(
        body, name=name, out_shape=jax.ShapeDtypeStruct((Bn, N), F32), grid=(N // tn,),
        in_specs=[pl.BlockSpec((Bn, D), lambda j: (0, 0)), pl.BlockSpec((D, tn), lambda j: (0, j)),
                  pl.BlockSpec((1, tn), lambda j: (0, j))],
        out_specs=pl.BlockSpec((Bn, tn), lambda j: (0, j)), compiler_params=_params("parallel"),
    )(c_all, w, b)


def _adamw(w, g, m, v, *, name):
    R, C = w.shape
    tr = R if R * C <= (1 << 19) else _tile(R, max(8, (1 << 19) // C), 8)

    def body(w_ref, g_ref, m_ref, v_ref, d_ref, nm_ref, nv_ref):
        gv = g_ref[...]
        nm = ADAM_B1 * m_ref[...] + (1.0 - ADAM_B1) * gv
        nv = ADAM_B2 * v_ref[...] + (1.0 - ADAM_B2) * (gv * gv)
        m_hat = nm / (1.0 - ADAM_B1 ** ADAM_STEP)
        v_hat = nv / (1.0 - ADAM_B2 ** ADAM_STEP)
        d_ref[...] = -ADAM_LR * (m_hat / (jnp.sqrt(v_hat) + ADAM_EPS) + ADAM_WD * w_ref[...])
        nm_ref[...] = nm
        nv_ref[...] = nv

    blk = pl.BlockSpec((tr, C), lambda i: (i, 0))
    shp = jax.ShapeDtypeStruct((R, C), F32)
    return pl.pallas_call(
        body, name=name, out_shape=(shp, shp, shp), grid=(R // tr,), in_specs=[blk] * 4, out_specs=(blk,) * 3,
        compiler_params=_params("parallel"),
    )(w, g, m, v)


def _ada_bwd_adamw(c_all_t, dmod, w, m, v, *, name):
    D, Bn = c_all_t.shape
    N = dmod.shape[1]
    tr = _tile(D, max(8, (1 << 19) // N), 8)

    def body(c_ref, d_ref, w_ref, m_ref, v_ref, g_ref, dl_ref, nm_ref, nv_ref):
        gv = jnp.dot(jax.nn.silu(c_ref[...]), d_ref[...], preferred_element_type=F32, precision=HIGHEST)
        nm = ADAM_B1 * m_ref[...] + (1.0 - ADAM_B1) * gv
        nv = ADAM_B2 * v_ref[...] + (1.0 - ADAM_B2) * (gv * gv)
        m_hat = nm / (1.0 - ADAM_B1 ** ADAM_STEP)
        v_hat = nv / (1.0 - ADAM_B2 ** ADAM_STEP)
        g_ref[...] = gv
        dl_ref[...] = -ADAM_LR * (m_hat / (jnp.sqrt(v_hat) + ADAM_EPS) + ADAM_WD * w_ref[...])
        nm_ref[...] = nm
        nv_ref[...] = nv

    blk = pl.BlockSpec((tr, N), lambda i: (i, 0))
    shp = jax.ShapeDtypeStruct((D, N), F32)
    return pl.pallas_call(
        body, name=name, out_shape=(shp,) * 4, grid=(D // tr,),
        in_specs=[pl.BlockSpec((tr, Bn), lambda i: (i, 0)), pl.BlockSpec((Bn, N), lambda i: (0, 0)), blk, blk, blk],
        out_specs=(blk,) * 4, compiler_params=_params("parallel"),
    )(c_all_t, dmod, w, m, v)


def _place():
    x, y, c = lax.axis_index("x"), lax.axis_index("y"), lax.axis_index("c")
    return x, y, c, [(1 - x, y), (x, 1 - y), (1 - x, 1 - y)]


def _remote(src, dst, send_sem, recv_sem, dev):
    return pltpu.make_async_remote_copy(src_ref=src, dst_ref=dst, send_sem=send_sem, recv_sem=recv_sem,
                                        device_id=dev, device_id_type=MESH)


def _allgather8(v, *, tie=None, name):
    R, C = v.shape

    def body(v_ref, out_ref, send_sems, recv_sems, local_sem):
        x, y, c, chips = _place()
        me, sibling = (x, y, c), (x, y, 1 - c)

        def rows(px, py, pc):
            return out_ref.at[pl.ds((4 * px + 2 * py + pc) * R, R), :]

        def copy(k, block, to, src=None):
            return _remote(rows(*block) if src is None else src, rows(*block), send_sems.at[k], recv_sems.at[k], to)

        mine = pltpu.make_async_copy(v_ref, rows(*me), local_sem)
        mine.start()
        first = [copy(0, me, sibling, src=v_ref)]
        first += [copy(1 + j, me, (*chip, c), src=v_ref) for j, chip in enumerate(chips)]
        for cp in first:
            cp.start()
        passed = [copy(4 + j, (*chip, c), sibling) for j, chip in enumerate(chips)]
        for j, chip in enumerate(chips):
            copy(1 + j, (*chip, c), me).wait_recv()
            passed[j].start()
        copy(0, sibling, me).wait_recv()
        for j, chip in enumerate(chips):
            copy(4 + j, (*chip, 1 - c), me).wait_recv()
        for cp in first + passed:
            cp.wait_send()
        mine.wait()

    body, tspec, targ = _tied(body, tie)
    out = pl.pallas_call(
        body, name=name, out_shape=jax.ShapeDtypeStruct((N_DEV * R, C), v.dtype),
        in_specs=tspec + [VMEM_FULL], out_specs=VMEM_FULL,
        scratch_shapes=[pltpu.SemaphoreType.DMA((7,)), pltpu.SemaphoreType.DMA((7,)), pltpu.SemaphoreType.DMA],
    )(*targ, v)
    return out.reshape(N_DEV, R, C)


SEM = pl.BlockSpec(memory_space=pltpu.SEMAPHORE)
HBM = pl.BlockSpec(memory_space=pltpu.HBM)
EFFECT = pltpu.SideEffectType.DATAFLOW_SIDE_EFFECTING
DMA_SEM = pltpu.SemaphoreType.DMA(())


def _in_hbm(a):
    return pltpu.with_memory_space_constraint(a, pltpu.HBM)


def _three_halves(land, r2):
    return land.at[pl.ds(0, N_CHIP - 1), pl.ds(0, r2)]


def _slot(chip, swap):
    return (chip % 2) * 2 + chip // 2 if swap else chip


def _gather_start(ws, after, swaps, *, name):
    n = len(ws)
    na = len(after)
    lands = [lax.empty((N_CHIP,) + w.shape, w.dtype) for w in ws]

    def body(*refs):
        w_refs, land_refs = refs[:n], refs[n:2 * n]
        send, recv = refs[2 * n + na:3 * n + na], refs[3 * n + na:4 * n + na]
        token = refs[6 * n + na]
        x, y, c, chips = _place()
        k = 2 * x + y
        for i in range(n):
            r2 = ws[i].shape[0] // 2
            for cx, cy in chips:
                _remote(w_refs[i].at[pl.ds(c * r2, r2)], land_refs[i].at[_slot(k, swaps[i]), pl.ds(c * r2, r2)], send[i], recv[i],
                        (cx, cy, c)).start()
        token[...] = jnp.zeros_like(token)

    outs = pl.pallas_call(
        body, name=name,
        out_shape=[DMA_SEM] * (2 * n) + [pltpu.HBM(w.shape, w.dtype) for w in ws] + [pltpu.HBM(l.shape, l.dtype) for l in lands]
        + [jax.ShapeDtypeStruct((8, LANES), F32)],
        in_specs=[HBM] * (2 * n) + [ANY] * na, out_specs=[SEM] * (2 * n) + [HBM] * (2 * n) + [VMEM_FULL],
        input_output_aliases={i: 2 * n + i for i in range(2 * n)},
        compiler_params=pltpu.CompilerParams(has_side_effects=EFFECT),
    )(*[_in_hbm(w) for w in ws], *[_in_hbm(l) for l in lands], *after)
    return outs[:n], outs[n:2 * n], outs[2 * n:3 * n], outs[3 * n:4 * n], outs[4 * n]


def _gather_forward(send, recv, ws, lands, after, swaps, *, name):
    n = len(ws)

    def body(*refs):
        w_refs, land_refs = refs[:n], refs[n:2 * n]
        send1, recv1 = refs[2 * n:3 * n], refs[3 * n:4 * n]
        send2, recv2 = refs[4 * n + 1 + 2 * n:4 * n + 1 + 3 * n], refs[4 * n + 1 + 3 * n:4 * n + 1 + 4 * n]
        x, y, c, chips = _place()
        sibling = (x, y, 1 - c)
        for i in range(n):
            r2 = ws[i].shape[0] // 2
            win = _three_halves(land_refs[i], r2)
            done = _remote(win, win, send1[i], recv1[i], sibling)
            done.wait_send()
            done.wait_recv()
            for cx, cy in chips:
                got = land_refs[i].at[_slot(2 * cx + cy, swaps[i]), pl.ds(c * r2, r2)]
                _remote(got, got, send2[i], recv2[i], sibling).start()
        token = refs[8 * n + 1]
        token[...] = jnp.zeros_like(token)

    outs = pl.pallas_call(
        body, name=name,
        out_shape=[pltpu.HBM(w.shape, w.dtype) for w in ws] + [pltpu.HBM(l.shape, l.dtype) for l in lands] + [DMA_SEM] * (2 * n)
        + [jax.ShapeDtypeStruct((8, LANES), F32)],
        in_specs=[HBM] * (2 * n) + [SEM] * (2 * n) + [ANY], out_specs=[HBM] * (2 * n) + [SEM] * (2 * n) + [VMEM_FULL],
        input_output_aliases={i: i for i in range(2 * n)},
        compiler_params=pltpu.CompilerParams(has_side_effects=EFFECT),
    )(*ws, *lands, *send, *recv, after)
    return outs[2 * n:3 * n], outs[3 * n:4 * n], outs[n:2 * n], outs[4 * n]


def _gather_finish(send, recv, lands, after, *, name):
    n = len(lands)

    def body(*refs):
        land_refs = refs[:n]
        send2, recv2 = refs[n:2 * n], refs[2 * n:3 * n]
        x, y, c, _ = _place()
        for i in range(n):
            win = _three_halves(land_refs[i], lands[i].shape[1] // 2)
            done = _remote(win, win, send2[i], recv2[i], (x, y, 1 - c))
            done.wait_send()
            done.wait_recv()

    return pl.pallas_call(
        body, name=name,
        out_shape=[pltpu.HBM(l.shape, l.dtype) for l in lands],
        in_specs=[HBM] * n + [SEM] * (2 * n) + [ANY], out_specs=[HBM] * n,
        input_output_aliases={i: i for i in range(n)},
        compiler_params=pltpu.CompilerParams(has_side_effects=EFFECT),
    )(*lands, *send, *recv, after)


def _scatter_start(gs, swaps, *, name):
    n = len(gs)
    lands = [lax.empty((N_DEV, g.shape[1] // 2, g.shape[2]), g.dtype) for g in gs]

    def body(*refs):
        g_refs, land_refs = refs[:n], refs[n:2 * n]
        send, recv = refs[2 * n:3 * n], refs[3 * n:4 * n]
        token = refs[6 * n]
        x, y, c, chips = _place()
        k = 2 * x + y
        me = 2 * k + c
        for i in range(n):
            r2 = gs[i].shape[1] // 2
            for cx, cy in chips:
                for cc in range(2):
                    _remote(g_refs[i].at[_slot(2 * cx + cy, swaps[i]), pl.ds(cc * r2, r2)], land_refs[i].at[me], send[i], recv[i],
                            (cx, cy, cc)).start()
            _remote(g_refs[i].at[_slot(k, swaps[i]), pl.ds((1 - c) * r2, r2)], land_refs[i].at[me], send[i], recv[i],
                    (x, y, 1 - c)).start()
        token[...] = jnp.zeros_like(token)

    outs = pl.pallas_call(
        body, name=name,
        out_shape=[DMA_SEM] * (2 * n) + [pltpu.HBM(g.shape, g.dtype) for g in gs] + [pltpu.HBM(l.shape, l.dtype) for l in lands]
        + [jax.ShapeDtypeStruct((8, LANES), F32)],
        in_specs=[HBM] * (2 * n), out_specs=[SEM] * (2 * n) + [HBM] * (2 * n) + [VMEM_FULL],
        input_output_aliases={i: 2 * n + i for i in range(2 * n)},
        compiler_params=pltpu.CompilerParams(has_side_effects=EFFECT),
    )(*[_in_hbm(g) for g in gs], *[_in_hbm(l) for l in lands])
    return outs[:n], outs[n:2 * n], outs[2 * n:3 * n], outs[3 * n:4 * n], outs[4 * n]


def _scatter_wait(send, recv, gs, lands, after, *, name):
    n = len(gs)

    def body(*refs):
        land_refs = refs[n:2 * n]
        send1, recv1 = refs[2 * n:3 * n], refs[3 * n:4 * n]
        x, y, c, _ = _place()
        for i in range(n):
            win = land_refs[i].at[pl.ds(0, N_DEV - 1)]
            done = _remote(win, win, send1[i], recv1[i], (x, y, 1 - c))
            done.wait_send()
            done.wait_recv()

    outs = pl.pallas_call(
        body, name=name,
        out_shape=[pltpu.HBM(g.shape, g.dtype) for g in gs] + [pltpu.HBM(l.shape, l.dtype) for l in lands],
        in_specs=[HBM] * (2 * n) + [SEM] * (2 * n) + [ANY], out_specs=[HBM] * (2 * n),
        input_output_aliases={i: i for i in range(2 * n)},
        compiler_params=pltpu.CompilerParams(has_side_effects=EFFECT),
    )(*gs, *lands, *send, *recv, after)
    return outs[:n], outs[n:]


def _share_halves(ts, *, name):
    n = len(ts)

    def body(*refs):
        outs = refs[n:2 * n]
        send_sems, recv_sems = refs[2 * n:]
        x, y, c, _ = _place()
        sibling = (x, y, 1 - c)
        cps = []
        for i in range(n):
            r2 = ts[i].shape[0] // 2
            mine = outs[i].at[pl.ds(c * r2, r2)]
            cps.append(_remote(mine, mine, send_sems.at[i], recv_sems.at[i], sibling))
            cps[-1].start()
        for i in range(n):
            r2 = ts[i].shape[0] // 2
            got = outs[i].at[pl.ds((1 - c) * r2, r2)]
            _remote(got, got, send_sems.at[i], recv_sems.at[i], sibling).wait_recv()
        for cp in cps:
            cp.wait_send()

    return pl.pallas_call(
        body, name=name,
        out_shape=[jax.ShapeDtypeStruct(t.shape, t.dtype) for t in ts],
        in_specs=[ANY] * n, out_specs=[ANY] * n, input_output_aliases={i: i for i in range(n)},
        scratch_shapes=[pltpu.SemaphoreType.DMA((n,)), pltpu.SemaphoreType.DMA((n,))],
    )(*ts)


def _sum_pieces(land, g, idx, *, name):
    _, r2, C = land.shape
    tr = _tile(r2, max(16, (1 << 21) // C), 16)
    nr = r2 // tr

    def body(idx_ref, land_ref, own_ref, o_ref, acc_ref):
        d = pl.program_id(1)
        mine = d == idx_ref[0]

        @pl.when(d == 0)
        def _():
            acc_ref[...] = jnp.zeros_like(acc_ref)

        @pl.when(mine)
        def _():
            acc_ref[...] += own_ref[...].astype(F32)

        @pl.when(jnp.logical_not(mine))
        def _():
            acc_ref[...] += land_ref[...].astype(F32)

        @pl.when(d == N_DEV - 1)
        def _():
            o_ref[...] = acc_ref[...]

    return pl.pallas_call(
        body, name=name, out_shape=jax.ShapeDtypeStruct((2 * r2, C), F32),
        grid_spec=pltpu.PrefetchScalarGridSpec(
            num_scalar_prefetch=1, grid=(nr, N_DEV),
            in_specs=[pl.BlockSpec((None, tr, C), lambda i, d, ix: (jnp.where(d == ix[0], (d + 1) % N_DEV, d), i, 0)),
                      pl.BlockSpec((None, tr, C), lambda i, d, ix: (ix[1], ix[2] * nr + i, 0))],
            out_specs=pl.BlockSpec((tr, C), lambda i, d, ix: (ix[2] * nr + i, 0)),
            scratch_shapes=[pltpu.VMEM((tr, C), F32)]),
        compiler_params=_params("parallel", "arbitrary"),
    )(idx, land, g)


def _sum_devices(v, *, name):
    n, R, C = v.shape

    def body(v_ref, o_ref):
        acc = v_ref[0]
        for j in range(1, n):
            acc = acc + v_ref[j]
        o_ref[...] = acc

    return pl.pallas_call(body, name=name, out_shape=jax.ShapeDtypeStruct((R, C), F32),
                          in_specs=[VMEM_FULL], out_specs=VMEM_FULL)(v)


def _shard_cols(shards, lo, hi, width):
    out = []
    while lo < hi:
        j = lo // width
        end = min(hi, (j + 1) * width)
        out.append(shards[j][:, lo - j * width:end - j * width])
        lo = end
    return out


def _from_col_shards(g):
    return jnp.transpose(g, (1, 0, 2)).reshape(g.shape[1], N_CHIP * g.shape[2])


def _to_col_shards(w):
    R, N = w.shape
    return jnp.transpose(w.reshape(R, N_CHIP, N // N_CHIP), (1, 0, 2))


def _split_heads(w, widths):
    R, N = w.shape
    per = sum(widths)
    w3 = w.reshape(R, N // per, per)
    lo = w3[:, :, :widths[0]].reshape(R, -1)
    hi = w3[:, :, widths[0]:].reshape(R, -1)
    return jnp.concatenate([lo, hi], axis=1)


def _merge_heads(w, widths):
    R, N = w.shape
    H = N // sum(widths)
    lo = w[:, :H * widths[0]].reshape(R, H, widths[0])
    hi = w[:, H * widths[0]:].reshape(R, H, widths[1])
    return jnp.concatenate([lo, hi], axis=2).reshape(R, N)


def _t5_bucket(dist):
    max_exact = REL_BUCKETS // 2
    n = jnp.maximum(dist, 0)
    large = max_exact + (jnp.log(jnp.maximum(n, 1).astype(F32) / max_exact)
                         / jnp.log(jnp.asarray(REL_MAX_DIST / max_exact, F32))
                         * (REL_BUCKETS - max_exact)).astype(jnp.int32)
    large = jnp.minimum(large, REL_BUCKETS - 1)
    return jnp.where(n < max_exact, n, large)


def _rel_tables():
    a = jnp.arange(SWA_BLOCK)
    b = jnp.arange(2 * SWA_BLOCK)
    dist = SWA_BLOCK + a[:, None] - b[None, :]
    valid = jnp.logical_and(dist >= 0, dist < SWA_BLOCK)
    onehot = jnp.logical_and(_t5_bucket(dist)[..., None] == jnp.arange(REL_BUCKETS), valid[..., None])
    onehot = onehot.astype(F32).reshape(2 * SWA_BLOCK * SWA_BLOCK, REL_BUCKETS)
    negmask = jnp.where(valid, 0.0, NEG).astype(F32).reshape(1, -1)
    return onehot, negmask


def _rope_tables(S):
    pos = jnp.arange(S, dtype=F32)
    inv = ROPE_THETA ** (-jnp.arange(0, MLA_ROPE, 2, dtype=F32) / MLA_ROPE)
    ang = pos[:, None] * inv[None, :]
    ang = jnp.concatenate([ang, ang, ang, ang], axis=-1)
    return jnp.cos(ang), jnp.sin(ang)


def _flat_pad(parts, rows=8):
    flat = jnp.concatenate([p.reshape(1, -1) for p in parts], axis=1)
    n = flat.shape[1]
    width = -(-n // (rows * LANES)) * LANES
    return jnp.pad(flat, ((0, 0), (0, rows * width - n))).reshape(rows, width)


def _unflat(vec, shapes):
    flat = vec.reshape(-1)
    out, off = [], 0
    for s in shapes:
        n = 1
        for d in s:
            n *= d
        out.append(flat[off:off + n].reshape(s))
        off += n
    return out


def kernel(x, c, w_ada, b_ada, g_pre_mix, g_post_mix, w_in, g_q_lat, w_uq, g_kv_lat, w_ukv, rel_bias, sinks, w_o, g_pre_ffn, g_post_ffn, w_up, conv_w, conv_b, w_down, loss_target, m_w_ada, m_b_ada, m_g_pre_mix, m_g_post_mix, m_w_in, m_g_q_lat, m_w_uq, m_g_kv_lat, m_w_ukv, m_rel_bias, m_sinks, m_w_o, m_g_pre_ffn, m_g_post_ffn, m_w_up, m_conv_w, m_conv_b, m_w_down, v_w_ada, v_b_ada, v_g_pre_mix, v_g_post_mix, v_w_in, v_g_q_lat, v_w_uq, v_g_kv_lat, v_w_ukv, v_rel_bias, v_sinks, v_w_o, v_g_pre_ffn, v_g_post_ffn, v_w_up, v_conv_w, v_conv_b, v_w_down):
    S, D = x.shape[1], x.shape[2]
    Rq, Rkv = g_q_lat.shape[1], g_kv_lat.shape[1]
    H = D // MLA_V
    NH = D // SWA_HD
    KW = SWA_KVH * SWA_HD
    F = w_down.shape[1] * N_CHIP
    xi, yi, ci = lax.axis_index("x"), lax.axis_index("y"), lax.axis_index("c")
    chip = 2 * xi + yi
    me = 2 * chip + ci
    x2, tgt = x[0], loss_target[0]

    c_all = _allgather8(jnp.broadcast_to(c, (8, D)), name="gather_c")[:, 0, :]
    n3 = w_ada.shape[2]
    mod_part = _ada_fwd(c_all, w_ada[0], lax.dynamic_slice(b_ada, (0, chip * n3), (1, n3)), name="ada_fwd")
    mod_all = _allgather8(mod_part, name="gather_mod")
    mod_me = lax.dynamic_index_in_dim(mod_all[0::2], me, axis=1, keepdims=False).reshape(1, 6 * D)
    sh1, sc1, gt1, sh2, sc2, gt2 = [mod_me[:, k * D:(k + 1) * D] for k in range(6)]

    swaps = [False, False, False, False, True, False]
    local = [w_in[0].astype(BF16)]
    send_a, recv_a, srcs_a, lands_a, token = _gather_start(local, (mod_all,), swaps[:1], name="gather_start_in")
    rest, token = lax.optimization_barrier(((w_uq[0], w_ukv[0], w_o[0], w_up[0], w_down[0]), token))
    local += [w.astype(BF16) for w in rest]
    send_b, recv_b, srcs_b, lands_b, token = _gather_start(local[1:], (token,), swaps[1:], name="gather_start_rest")
    send1, recv1, srcs, lands = send_a + send_b, recv_a + recv_b, srcs_a + srcs_b, lands_a + lands_b
    onehot, negmask = _rel_tables()
    npb = NH // (2 * SWA_KVH)
    rb_st = jnp.transpose(rel_bias.T.reshape(SWA_KVH, npb, 2, REL_BUCKETS), (0, 2, 1, 3)).reshape(NH, REL_BUCKETS)
    bias_m = (_matmul(rb_st, onehot.T, tie=token, name="rel_bias_table") + negmask).reshape(
        2 * SWA_KVH, npb * SWA_BLOCK, 2 * SWA_BLOCK)
    h = _modnorm_fwd(x2, g_pre_mix, sc1, sh1, name="pre_mix_norm")

    def whole(land, i):
        return lax.dynamic_update_index_in_dim(land, local[i], _slot(chip, swaps[i]), 0)

    def conv_slots(v):
        return jnp.stack([v[0], v[2], v[1], v[3]])

    s2, r2, l_in, _ = _gather_forward(send1[:1], recv1[:1], srcs[:1], lands[:1], h, swaps[:1], name="gather_forward_in")
    (l_in,) = _gather_finish(s2, r2, l_in, h, name="gather_finish_in")
    gin = whole(l_in, 0)
    o_kr = Rq + Rkv
    o_q = o_kr + MLA_ROPE
    o_g = o_q + NH * SWA_HD + 2 * KW
    n_gate, n_swa = 2 * D, o_g - o_q
    n_lat = -(-(o_q + MLA_ROPE) // PAIR) * PAIR
    runs = []
    for tl in range(D // PAIR):
        runs.append((o_g + tl * PAIR, o_g + (tl + 1) * PAIR, 2 * tl * PAIR))
        runs.append((o_g + D + tl * PAIR, o_g + D + (tl + 1) * PAIR, (2 * tl + 1) * PAIR))
    runs.append((o_q, o_g, n_gate))
    runs.append((0, o_q, n_gate + n_swa))
    csh = gin.shape[2]
    parts = []
    for lo, hi, _ in runs + [(o_kr, o_q, 0)]:
        parts += _shard_cols([gin[j] for j in range(N_CHIP)], lo, hi, csh)
    parts.append(jnp.zeros((D, n_lat - o_q - MLA_ROPE), BF16))
    w_in_all = jnp.concatenate(parts, axis=1)
    cos, sin = _rope_tables(S)
    sink_v = sinks.reshape(NH)

    z_lat = _matmul(h, w_in_all, bcols=(n_gate + n_swa, n_lat), name="in_proj_lat")
    z_swa = _matmul(h, w_in_all, bcols=(n_gate, n_swa), name="in_proj_swa")
    zg = _matmul(h, w_in_all, bcols=(0, n_gate), out_dtype=BF16, name="in_proj_gate")
    s2b, r2b, l_b, _ = _gather_forward(send1[1:4], recv1[1:4], srcs[1:4], lands[1:4], zg, swaps[1:4],
                                       name="gather_forward_attn")
    nq, nkv = _lat_norm_fwd(z_lat, g_q_lat, g_kv_lat, name="lat_norm")
    l_uq, l_ukv, l_o = _gather_finish(s2b, r2b, l_b, nq, name="gather_finish_attn")
    wuq = _split_heads(_from_col_shards(whole(l_uq, 1)), (MLA_NOPE, MLA_ROPE))
    wukv = _split_heads(_from_col_shards(whole(l_ukv, 2)), (MLA_NOPE, MLA_V))
    wo = whole(l_o, 3).reshape(D, D)
    q_raw = _matmul(nq, wuq, out_dtype=BF16, name="uq_proj")
    kv_raw = _matmul(nkv, wukv, out_dtype=BF16, name="ukv_proj")
    qp, kp, vv = _mla_pack_fwd(q_raw, kv_raw, z_lat, cos, sin, o_kr, name="mla_pack")
    o_a, lse = _flash_fwd(qp, kp, vv, name="mla_attn")
    s2c, r2c, l_c, tok_c = _gather_forward(send1[4:], recv1[4:], srcs[4:], lands[4:], o_a, swaps[4:],
                                           name="gather_forward_ffn")
    o_b = _swa_fwd(z_swa, bias_m, sink_v, name="swa_attn")
    mixin = _gate_fwd(zg, o_a, o_b, name="gate_mix")
    mix = _matmul(mixin, wo, tie=tok_c, name="o_proj")
    x1, h2 = _resnorm_modnorm_fwd(x2, mix, g_post_mix, gt1, g_pre_ffn, sc2, sh2, name="post_mix_pre_ffn_norm")
    l_up, l_down = _gather_finish(s2c, r2c, l_c, h2, name="gather_finish_ffn")
    cw_all = _allgather8(jnp.pad(conv_w[0], ((0, 5), (0, 0))), tie=l_down, name="gather_conv_w")[0::2, :3]
    cw = conv_slots(cw_all)
    cb = conv_slots(conv_b.reshape(N_CHIP, 1, -1))
    wup = whole(l_up, 4)
    wdown = whole(l_down, 5).reshape(F, D)
    t = _matmul(h2, wup, out_dtype=BF16, shards="out", name="up_proj")
    a = _conv_gate_fwd(t, cw, cb, name="conv_gate")
    yv = _matmul(a, wdown, name="down_proj")
    dout, dy, dg_post_ffn, dgt2, loss_tile = _resnorm_loss(x1, yv, g_post_ffn, gt2, tgt, name="post_ffn_norm_loss")

    big_params = dict(w_in=(w_in, m_w_in, v_w_in), w_uq=(w_uq, m_w_uq, v_w_uq), w_ukv=(w_ukv, m_w_ukv, v_w_ukv),
                      w_o=(w_o, m_w_o, v_w_o), w_up=(w_up, m_w_up, v_w_up), w_down=(w_down, m_w_down, v_w_down))
    res = {}

    def start(nms, gs):
        sw = [nm == "w_up" for nm in nms]
        send, recv, gsrc, glands, tok = _scatter_start(gs, sw, name="grads_start_" + nms[0])
        return (nms, send, recv, gsrc, glands), tok

    def finish(pendings, after):
        nms, send, recv, gsrc, glands = [sum((list(p[k]) for p in pendings), []) for k in range(5)]
        gsrc, glands = _scatter_wait(send, recv, gsrc, glands, after, name="grads_wait_" + nms[0])
        halves = [_sum_pieces(l, g, jnp.stack([me, _slot(chip, nm == "w_up"), ci]).astype(jnp.int32), name="grad_sum_" + nm)
                  for l, g, nm in zip(glands, gsrc, nms)]
        for nm, g in zip(nms, _share_halves(halves, name="grads_share_" + nms[0])):
            w, m, v = big_params[nm]
            res[nm] = (g,) + tuple(_adamw(w[0], g, m[0], v[0], name="adamw_" + nm))

    dw_down = _matmul(a, dy, ta=True, out_dtype=BF16, name="down_proj_dw")
    p_down, tok = start(["w_down"], [dw_down.reshape(N_CHIP, F // N_CHIP, D)])
    da = _matmul(dy, wdown, tb=True, out_dtype=BF16, tie=tok, name="down_proj_dx")
    du, dcw_s, dcb_s = _conv_gate_bwd(t, da, cw, cb, name="conv_gate_bwd")
    dt = _conv_bwd_dt(du, cw, name="conv_bwd_dt")
    dw_up = _matmul(h2, dt, ta=True, out_dtype=BF16, shards="out", name="up_proj_dw")
    p_up, tok = start(["w_up"], [dw_up])
    dh2 = _matmul(dt, wup, tb=True, tie=tok, shards="k", name="up_proj_dx")
    dx1, dg_pre_ffn, dsc2, dsh2, dmix, dg_post_mix, dgt1 = _modnorm_resnorm_bwd(
        dh2, x1, g_pre_ffn, sc2, dout, mix, g_post_mix, gt1, name="pre_ffn_post_mix_norm_bwd")
    dw_o = _matmul(mixin, dmix, ta=True, out_dtype=BF16, name="o_proj_dw")
    p_o, tok = start(["w_o"], [dw_o.reshape(N_CHIP, D // N_CHIP, D)])
    dmixin = _matmul(dmix, wo, tb=True, out_dtype=BF16, tie=tok, name="o_proj_dx")
    do_a, do_b, dzg = _gate_bwd(dmixin, zg, o_a, o_b, name="gate_mix_bwd")
    dqp, dkp, dvv = _flash_bwd(qp, kp, vv, o_a, do_a, lse, name="mla_attn_bwd")
    dq_raw, dkv_raw, dkr = _mla_pack_bwd(dqp, dkp, dvv, cos, sin, name="mla_pack_bwd")
    dw_uq_p = _matmul(nq, dq_raw, ta=True, out_dtype=BF16, name="uq_proj_dw")
    dw_ukv_p = _matmul(nkv, dkv_raw, ta=True, out_dtype=BF16, name="ukv_proj_dw")
    p_qkv, tok = start(["w_uq", "w_ukv"], [_to_col_shards(_merge_heads(dw_uq_p, (MLA_NOPE, MLA_ROPE))),
                                           _to_col_shards(_merge_heads(dw_ukv_p, (MLA_NOPE, MLA_V)))])
    dnq = _matmul(dq_raw, wuq, tb=True, tie=tok, name="uq_proj_dx")
    dnkv = _matmul(dkv_raw, wukv, tb=True, name="ukv_proj_dx")
    dz_lat, dg_q, dg_kv = _lat_norm_bwd(z_lat, dnq, dnkv, dkr, g_q_lat, g_kv_lat, name="lat_norm_bwd")
    dz_swa, dbias, dsink = _swa_bwd(z_swa, bias_m, sink_v, o_b, do_b, name="swa_attn_bwd")
    dz = jnp.concatenate([dzg, dz_swa, dz_lat], axis=1)
    dw_in_p = _matmul(h, dz, ta=True, out_dtype=BF16, name="in_proj_dw")
    dw_shards = []
    for j in range(N_CHIP):
        cols = []
        for lo, hi, at in sorted(runs):
            a0, a1 = max(lo, j * csh), min(hi, (j + 1) * csh)
            if a0 < a1:
                cols.append(dw_in_p[:, at + a0 - lo:at + a1 - lo])
        dw_shards.append(jnp.concatenate(cols, axis=1))
    p_in, tok = start(["w_in"], [jnp.stack(dw_shards)])
    dh = _matmul(dz, w_in_all, tb=True, tie=tok, name="in_proj_dx")
    grad_x, dg_pre_mix, dsc1, dsh1 = _modnorm_bwd(dh, x2, g_pre_mix, sc1, dx1, name="pre_mix_norm_bwd")
    drel_st = _matmul(dbias.reshape(NH, -1), onehot, tie=grad_x, name="rel_bias_bwd")
    finish((p_down, p_up, p_o, p_qkv), drel_st)
    drel = jnp.transpose(drel_st.reshape(SWA_KVH, 2, npb, REL_BUCKETS), (0, 2, 1, 3)).reshape(NH, REL_BUCKETS).T

    dcw = _from_col_shards(conv_slots(dcw_s))
    dcb = conv_slots(dcb_s).reshape(1, -1)
    dmod = jnp.concatenate([dsh1, dsc1, dgt1, dsh2, dsc2, dgt2], axis=1)
    small = [dmod, dg_pre_mix, dg_post_mix, dg_pre_ffn, dg_post_ffn, dg_q, dg_kv, drel, dsink[:, :NH], dcb, dcw]
    shapes = [p.shape for p in small]
    done = [res[nm][1] for nm in ("w_down", "w_up", "w_o", "w_uq", "w_ukv")]
    small_all = _allgather8(_flat_pad(small), tie=done, name="gather_small_grads")
    tot = _unflat(_sum_devices(small_all, name="sum_small_grads"), shapes)
    g_b_ada, g_pre_mix_g, g_post_mix_g, g_pre_ffn_g, g_post_ffn_g, g_q_g, g_kv_g, g_rel, g_sinks, g_cb, g_cw_full = tot
    dmod_all = small_all.reshape(N_DEV, -1)[:, :6 * D]
    ncw = conv_w.shape[2]
    g_cw = lax.dynamic_slice(g_cw_full, (0, chip * ncw), (3, ncw))

    res["w_ada"] = tuple(_ada_bwd_adamw(c_all.T, lax.dynamic_slice(dmod_all, (0, chip * n3), (N_DEV, n3)),
                                        w_ada[0], m_w_ada[0], v_w_ada[0], name="ada_bwd_adamw"))
    finish((p_in,), res["w_ada"][0])
    snames = ["b_ada", "g_pre_mix", "g_post_mix", "g_pre_ffn", "g_post_ffn", "g_q_lat", "g_kv_lat", "rel_bias", "sinks",
              "conv_b", "conv_w"]
    sw = [b_ada, g_pre_mix, g_post_mix, g_pre_ffn, g_post_ffn, g_q_lat, g_kv_lat, rel_bias, sinks, conv_b, conv_w]
    sm = [m_b_ada, m_g_pre_mix, m_g_post_mix, m_g_pre_ffn, m_g_post_ffn, m_g_q_lat, m_g_kv_lat, m_rel_bias, m_sinks,
          m_conv_b, m_conv_w]
    sv = [v_b_ada, v_g_pre_mix, v_g_post_mix, v_g_pre_ffn, v_g_post_ffn, v_g_q_lat, v_g_kv_lat, v_rel_bias, v_sinks,
          v_conv_b, v_conv_w]
    sg = [g_b_ada, g_pre_mix_g, g_post_mix_g, g_pre_ffn_g, g_post_ffn_g, g_q_g, g_kv_g, g_rel, g_sinks, g_cb, g_cw]
    sshapes = [w.shape for w in sw]
    sd, snm, snv = _adamw(_flat_pad(sw), _flat_pad(sg), _flat_pad(sm), _flat_pad(sv), name="adamw_small")
    sd, snm, snv = _unflat(sd, sshapes), _unflat(snm, sshapes), _unflat(snv, sshapes)
    for k, nm in enumerate(snames):
        res[nm] = (sg[k].reshape(sshapes[k]), sd[k], snm[k], snv[k])

    order = ["w_ada", "b_ada", "g_pre_mix", "g_post_mix", "w_in", "g_q_lat", "w_uq", "g_kv_lat", "w_ukv", "rel_bias", "sinks",
             "w_o", "g_pre_ffn", "g_post_ffn", "w_up", "conv_w", "conv_b", "w_down"]
    ref_shapes = dict(w_ada=w_ada.shape, w_in=w_in.shape, w_uq=w_uq.shape, w_ukv=w_ukv.shape, w_o=w_o.shape,
                      w_up=w_up.shape, w_down=w_down.shape)
    outs = []
    for k in range(4):
        for nm in order:
            arr = res[nm][k]
            outs.append(arr.reshape(ref_shapes[nm]) if nm in ref_shapes else arr)
    loss = lax.psum(loss_tile[0, 0], ("x", "y", "c"))
    return (loss, grad_x[None], *outs)
```
